```python
import math
import jax
import jax.numpy as jnp
from jax import lax
import numpy as np

D_MODEL = 1024
BATCH = 32
SEQ = 256
DEPTH = 2
DEC_BATCH = 2
DEC_SEQ = 1024
PAST_LEN = 512

GRID_W = 64
D_FF = 2816
N_MOD = 9
NORM_EPS = 1e-6
CHUNK = 128
D_SSM = 256
SSM_HEADS = 4
SSM_HEAD_DIM = 64
SSM_STATE = 128
SSM_GROUPS = 2
SSM_CONV = 3
SSM_CONV_CH = D_SSM + 2 * SSM_GROUPS * SSM_STATE
SSM_IN = D_SSM + SSM_CONV_CH + 2 * SSM_HEADS
D_HY = 256
HY_ORDER = 2
HY_CONV = 3
HY_BANDS = 16
HY_EMB = 1 + 2 * HY_BANDS
HY_HIDDEN = 64
HY_FAST_DECAY = 0.3
HY_SLOW_DECAY = 1.5
HY_TARGET = 1e-2
HY_IN = (HY_ORDER + 1) * D_HY
HY_FILT = 2 * HY_ORDER * D_HY
D_RET = 256
RET_HEADS = 4
RET_HEAD_DIM = 64
RET_IN = 4 * D_RET
ATT_HEADS = 4
ATT_KV_HEADS = 2
HEAD_DIM = 64
D_ATT = ATT_HEADS * HEAD_DIM
ATT_IN = D_ATT + 2 * ATT_KV_HEADS * HEAD_DIM
WINDOW = 128
ATT_BLOCK = 128
ROPE_BASE = 10000.0

D_MIX = D_SSM + D_HY + D_RET + D_ATT
D_IN = SSM_IN + HY_IN + RET_IN + ATT_IN

kernel_name = 'hybrid_ssd_hyena_retention_swa_prefix_trunk'


def rmsnorm(x, w):
    xf = x.astype(jnp.float32)
    y = xf * lax.rsqrt(jnp.mean(xf * xf, axis=-1, keepdims=True) + NORM_EPS)
    return y.astype(x.dtype) * w


def modulate(x, shift, scale):
    return x * (1 + scale) + shift


def swiglu(h, w_in, w_out):
    gate, up = jnp.split(h @ w_in, 2, axis=-1)
    return (jax.nn.silu(gate) * up) @ w_out


def centred_conv(x, w, b):
    k, l = w.shape[0], x.shape[1]
    half = k // 2
    xp = jnp.pad(x, ((0, 0), (half, half), (0, 0)))
    out = xp[:, 0:l] * w[0]
    for i in range(1, k):
        out = out + xp[:, i:i + l] * w[i]
    return out + b


def chunked_scan(q, k, v, log_a, s0):
    f32 = jnp.float32
    b, l, h, n = q.shape
    p = v.shape[-1]
    nc = l // CHUNK
    qc = q.astype(f32).reshape(b, nc, CHUNK, h, n)
    kc = k.astype(f32).reshape(b, nc, CHUNK, h, n)
    vc = v.astype(f32).reshape(b, nc, CHUNK, h, p)
    cum = jnp.cumsum(log_a.astype(f32).reshape(b, nc, CHUNK, h), axis=2)
    mask = jnp.tril(jnp.ones((CHUNK, CHUNK), dtype=bool))[None, None, :, :, None]
    seg = cum[:, :, :, None, :] - cum[:, :, None, :, :]
    decay = jnp.exp(jnp.where(mask, seg, -jnp.inf))
    scores = jnp.einsum('bcihn,bcjhn->bcijh', qc, kc) * decay
    y_intra = jnp.einsum('bcijh,bcjhp->bcihp', scores, vc)
    tail = jnp.exp(cum[:, :, -1:, :] - cum)
    inc = jnp.einsum('bcjhn,bcjh,bcjhp->bchnp', kc, tail, vc)
    chunk_decay = jnp.exp(cum[:, :, -1, :])

    def step(s, inp):
        d, dinc = inp
        return d[..., None, None] * s + dinc, s

    s_final, s_prev = lax.scan(step, s0.astype(f32), (jnp.moveaxis(chunk_decay, 1, 0), jnp.moveaxis(inc, 1, 0)))
    s_prev = jnp.moveaxis(s_prev, 0, 1)
    y_inter = jnp.einsum('bcihn,bcih,bchnp->bcihp', qc, jnp.exp(cum), s_prev)
    y = (y_intra + y_inter).reshape(b, l, h, p).astype(v.dtype)
    return y, s_final


def bidir_scan(q, k_f, k_b, v, la_f, la_b, s0):
    y_f, s_f = chunked_scan(q, k_f, v, la_f, s0[:, 0])
    fl = lambda a: jnp.flip(a, axis=1)
    y_b, s_b = chunked_scan(fl(q), fl(k_b), fl(v), fl(la_b), s0[:, 1])
    return y_f + fl(y_b), jnp.stack([s_f, s_b], axis=1)


def ssd_mixer(u, conv_w, conv_b, dt_bias, a_log, d_skip, norm_w, s0):
    b, l, _ = u.shape
    z = u[..., :D_SSM]
    xbc = jax.nn.silu(centred_conv(u[..., D_SSM:D_SSM + SSM_CONV_CH], conv_w, conv_b))
    dt_raw = u[..., D_SSM + SSM_CONV_CH:].reshape(b, l, 2, SSM_HEADS)
    xs = xbc[..., :D_SSM].reshape(b, l, SSM_HEADS, SSM_HEAD_DIM)
    gs = SSM_GROUPS * SSM_STATE
    rep = SSM_HEADS // SSM_GROUPS
    bm = jnp.repeat(xbc[..., D_SSM:D_SSM + gs].reshape(b, l, SSM_GROUPS, SSM_STATE), rep, axis=2)
    cm = jnp.repeat(xbc[..., D_SSM + gs:].reshape(b, l, SSM_GROUPS, SSM_STATE), rep, axis=2)
    dt = jax.nn.softplus(dt_raw.astype(jnp.float32) + dt_bias.astype(jnp.float32))
    log_a = dt * (-jnp.exp(a_log.astype(jnp.float32)))
    k_f = bm * dt[:, :, 0, :, None]
    k_b = bm * dt[:, :, 1, :, None]
    y, s = bidir_scan(cm, k_f, k_b, xs, log_a[:, :, 0], log_a[:, :, 1], s0)
    y = (y + d_skip[:, None] * xs).reshape(b, l, D_SSM)
    return rmsnorm(y * jax.nn.silu(z), norm_w), s


def hyena_filters(l, w1, b1, w2, b2, w3, freq):
    f32 = jnp.float32
    pos = jnp.arange(l, dtype=f32)
    t = pos / (l - 1)
    bands = jnp.linspace(1e-4, HY_BANDS - 1, HY_BANDS, dtype=f32)
    ang = (2.0 * math.pi / l) * pos[:, None] * bands[None, :]
    feats = jnp.concatenate([t[:, None], jnp.cos(ang), -jnp.sin(ang)], axis=-1)
    fr = freq.astype(f32)
    h = jnp.sin(fr * (feats @ w1.astype(f32) + b1.astype(f32)))
    h = jnp.sin(fr * (h @ w2.astype(f32) + b2.astype(f32)))
    h = (h @ w3.astype(f32)).reshape(l, 2, HY_ORDER, D_HY)
    max_decay = math.log(HY_TARGET) / HY_FAST_DECAY
    min_decay = math.log(HY_TARGET) / HY_SLOW_DECAY
    deltas = jnp.abs(jnp.linspace(min_decay, max_decay, D_HY, dtype=f32))
    h = h * jnp.exp(-t[:, None] * deltas[None, :])[:, None, None, :]
    h_fwd, h_bwd = h[:, 0], h[:, 1]
    return jnp.concatenate([h_fwd, jnp.zeros((1, HY_ORDER, D_HY), f32), h_bwd[:0:-1]], axis=0)


def fft_long_conv(z, kern, d_bias):
    l = z.shape[1]
    zf = z.astype(jnp.float32)
    zk = jnp.fft.rfft(zf, n=2 * l, axis=1)
    kk = jnp.fft.rfft(kern, axis=0)
    y = jnp.fft.irfft(zk * kk[None], n=2 * l, axis=1)[:, :l]
    return (y + zf * d_bias.astype(jnp.float32)).astype(z.dtype)


def hyena_mixer(u, conv_w, conv_b, w1, b1, w2, b2, w3, freq, d_bias):
    l = u.shape[1]
    uc = centred_conv(u, conv_w, conv_b)
    v, x1, x2 = jnp.split(uc, 3, axis=-1)
    kern = hyena_filters(l, w1, b1, w2, b2, w3, freq)
    z = x1 * fft_long_conv(v, kern[:, 0], d_bias[0])
    z = x2 * fft_long_conv(z, kern[:, 1], d_bias[1])
    return z


def retention_mixer(u, decay_logit, gn_w, s0):
    b, l, _ = u.shape
    q, k, v, g = jnp.split(u, 4, axis=-1)
    hs = lambda a: a.reshape(b, l, RET_HEADS, RET_HEAD_DIM)
    q, k, v = hs(q), hs(k) * RET_HEAD_DIM ** -0.5, hs(v)
    log_g = jax.nn.log_sigmoid(decay_logit.astype(jnp.float32))
    la_f = jnp.broadcast_to(log_g[0], (b, l, RET_HEADS))
    la_b = jnp.broadcast_to(log_g[1], (b, l, RET_HEADS))
    y, s = bidir_scan(q, k, k, v, la_f, la_b, s0)
    yf = y.astype(jnp.float32)
    mu = jnp.mean(yf, axis=-1, keepdims=True)
    var = jnp.mean(jnp.square(yf - mu), axis=-1, keepdims=True)
    yn = ((yf - mu) * lax.rsqrt(var + NORM_EPS)).reshape(b, l, D_RET).astype(u.dtype) * gn_w
    return yn * jax.nn.silu(g), s


def axial_rope(x):
    l = x.shape[1]
    n_rows = l // GRID_W
    rows = jnp.repeat(jnp.arange(n_rows, dtype=jnp.float32), GRID_W)
    cols = jnp.tile(jnp.arange(GRID_W, dtype=jnp.float32), n_rows)
    nf = HEAD_DIM // 4
    inv = ROPE_BASE ** (-jnp.arange(nf, dtype=jnp.float32) / nf)

    def rot(xh, pos):
        ang = pos[:, None] * inv[None, :]
        c = jnp.cos(ang)[None, :, None, :]
        s = jnp.sin(ang)[None, :, None, :]
        x1 = xh[..., :nf].astype(jnp.float32)
        x2 = xh[..., nf:].astype(jnp.float32)
        return jnp.concatenate([x1 * c - x2 * s, x1 * s + x2 * c], axis=-1)

    half = HEAD_DIM // 2
    return jnp.concatenate([rot(x[..., :half], rows), rot(x[..., half:], cols)], axis=-1).astype(x.dtype)


def context_attention(q, k, v, sink):
    b, l = q.shape[:2]
    g = ATT_HEADS // ATT_KV_HEADS
    nb = l // ATT_BLOCK
    qb = q.reshape(b, nb, ATT_BLOCK, ATT_KV_HEADS, g, HEAD_DIM)
    sink_l = sink.astype(jnp.float32).reshape(ATT_KV_HEADS, g)[None, :, :, None, None]
    scale = HEAD_DIM ** -0.5

    def block(i):
        qi = lax.dynamic_index_in_dim(qb, i, axis=1, keepdims=False)
        s = jnp.einsum('bqkgd,bskd->bkgqs', qi, k).astype(jnp.float32) * scale
        s = jnp.concatenate([jnp.broadcast_to(sink_l, s.shape[:-1] + (1,)), s], axis=-1)
        p = jax.nn.softmax(s, axis=-1)[..., 1:].astype(v.dtype)
        return jnp.einsum('bkgqs,bskd->bqkgd', p, v)

    o = lax.map(block, jnp.arange(nb))
    return jnp.moveaxis(o, 0, 1).reshape(b, l, D_ATT)


def latent_attention(q, k, v, ck, cv, sink):
    b, l = q.shape[:2]
    lc = ck.shape[1]
    g = ATT_HEADS // ATT_KV_HEADS
    nb = l // ATT_BLOCK
    qb = q.reshape(b, nb, ATT_BLOCK, ATT_KV_HEADS, g, HEAD_DIM)
    pad = ((0, 0), (ATT_BLOCK, ATT_BLOCK), (0, 0), (0, 0))
    kp, vp = jnp.pad(k, pad), jnp.pad(v, pad)
    sink_l = sink.astype(jnp.float32).reshape(ATT_KV_HEADS, g)[None, :, :, None, None]
    scale = HEAD_DIM ** -0.5
    q_off = jnp.arange(ATT_BLOCK)
    k_off = jnp.arange(3 * ATT_BLOCK)

    def block(i):
        qi = lax.dynamic_index_in_dim(qb, i, axis=1, keepdims=False)
        ki = lax.dynamic_slice_in_dim(kp, i * ATT_BLOCK, 3 * ATT_BLOCK, axis=1)
        vi = lax.dynamic_slice_in_dim(vp, i * ATT_BLOCK, 3 * ATT_BLOCK, axis=1)
        qpos = i * ATT_BLOCK + q_off
        kpos = (i - 1) * ATT_BLOCK + k_off
        valid = (jnp.abs(qpos[:, None] - kpos[None, :]) <= WINDOW) & ((kpos >= 0) & (kpos < l))[None, :]
        s_loc = jnp.einsum('bqkgd,bskd->bkgqs', qi, ki).astype(jnp.float32) * scale
        s_loc = jnp.where(valid, s_loc, -jnp.inf)
        s_ctx = jnp.einsum('bqkgd,bskd->bkgqs', qi, ck).astype(jnp.float32) * scale
        s = jnp.concatenate([jnp.broadcast_to(sink_l, s_loc.shape[:-1] + (1,)), s_ctx, s_loc], axis=-1)
        p = jax.nn.softmax(s, axis=-1).astype(v.dtype)
        return (jnp.einsum('bkgqs,bskd->bqkgd', p[..., 1:1 + lc], cv)
                + jnp.einsum('bkgqs,bskd->bqkgd', p[..., 1 + lc:], vi))

    o = lax.map(block, jnp.arange(nb))
    return jnp.moveaxis(o, 0, 1).reshape(b, l, D_ATT)


def token_mixing(h, lp, ssd_s0, ret_s0, ctx_kv):
    u = h @ lp['mix_w_in']
    o1 = SSM_IN
    o2 = o1 + HY_IN
    o3 = o2 + RET_IN
    y_ssd, s_ssd = ssd_mixer(u[..., :o1], lp['ssd_conv_w'], lp['ssd_conv_b'], lp['ssd_dt_bias'],
                             lp['ssd_a_log'], lp['ssd_d'], lp['ssd_norm_w'], ssd_s0)
    y_hy = hyena_mixer(u[..., o1:o2], lp['hy_conv_w'], lp['hy_conv_b'], lp['hy_w1'], lp['hy_b1'],
                       lp['hy_w2'], lp['hy_b2'], lp['hy_w3'], lp['hy_freq'], lp['hy_bias'])
    y_ret, s_ret = retention_mixer(u[..., o2:o3], lp['ret_decay_logit'], lp['ret_gn_w'], ret_s0)
    ua = u[..., o3:]
    b, l = ua.shape[:2]
    kvw = ATT_KV_HEADS * HEAD_DIM
    q = rmsnorm(ua[..., :D_ATT].reshape(b, l, ATT_HEADS, HEAD_DIM), lp['attn_q_norm'])
    k = rmsnorm(ua[..., D_ATT:D_ATT + kvw].reshape(b, l, ATT_KV_HEADS, HEAD_DIM), lp['attn_k_norm'])
    v = ua[..., D_ATT + kvw:].reshape(b, l, ATT_KV_HEADS, HEAD_DIM)
    if ctx_kv is None:
        y_att = context_attention(q, k, v, lp['attn_sink'])
    else:
        y_att = latent_attention(axial_rope(q), axial_rope(k), v, ctx_kv[0], ctx_kv[1], lp['attn_sink'])
    y = jnp.concatenate([y_ssd, y_hy, y_ret, y_att], axis=-1) @ lp['mix_w_out']
    return y, s_ssd, s_ret, k, v


def trunk_layer(x, cond, lp, ssd_s0, ret_s0, ctx_kv):
    mod = (jax.nn.silu(cond) @ lp['w_mod'] + lp['b_mod'])[:, None, :]
    sh1, sc1, g1, sh2, sc2, g2, sh3, sc3, g3 = jnp.split(mod, N_MOD, axis=-1)
    nw = lp['norm_w']
    x = x + 0.5 * g1 * swiglu(modulate(rmsnorm(x, nw[0]), sh1, sc1), lp['ffn_w_in'][0], lp['ffn_w_out'][0])
    y, s_ssd, s_ret, k, v = token_mixing(modulate(rmsnorm(x, nw[1]), sh2, sc2), lp, ssd_s0, ret_s0, ctx_kv)
    x = x + g2 * y
    x = x + 0.5 * g3 * swiglu(modulate(rmsnorm(x, nw[2]), sh3, sc3), lp['ffn_w_in'][1], lp['ffn_w_out'][1])
    return x, s_ssd, s_ret, k, v


def setup_inputs(seed: int = 0) -> dict:
    key = jax.random.key(seed)
    ks = iter(jax.random.split(key, 48))
    f32 = jnp.float32

    def nrm(shape, s):
        return jax.random.normal(next(ks), shape, f32) * s

    x_prompt = nrm((BATCH, SEQ, D_MODEL), 1.0)
    x_sample = nrm((DEC_BATCH, DEC_SEQ, D_MODEL), 1.0)
    cache_k = nrm((DEC_BATCH, DEPTH, PAST_LEN, ATT_KV_HEADS, HEAD_DIM), 1.0)
    cache_v = nrm((DEC_BATCH, DEPTH, PAST_LEN, ATT_KV_HEADS, HEAD_DIM), 1.0)
    state_ssd = nrm((DEC_BATCH, DEPTH, 2, SSM_HEADS, SSM_STATE, SSM_HEAD_DIM), 0.1)
    state_ret = nrm((DEC_BATCH, DEPTH, 2, RET_HEADS, RET_HEAD_DIM, RET_HEAD_DIM), 1.0)
    c = nrm((DEC_BATCH, D_MODEL), 1.0)
    c_ctx = nrm((D_MODEL,), 1.0)
    w_mod = nrm((DEPTH, D_MODEL, N_MOD * D_MODEL), 0.5 * D_MODEL ** -0.5)
    b_mod = nrm((DEPTH, N_MOD * D_MODEL), 0.02)
    norm_w = 1.0 + nrm((DEPTH, 3, D_MODEL), 0.02)
    ffn_w_in = nrm((DEPTH, 2, D_MODEL, 2 * D_FF), D_MODEL ** -0.5)
    ffn_w_out = nrm((DEPTH, 2, D_FF, D_MODEL), D_FF ** -0.5)
    mix_w_in = nrm((DEPTH, D_MODEL, D_IN), D_MODEL ** -0.5)
    mix_w_out = nrm((DEPTH, D_MIX, D_MODEL), D_MIX ** -0.5)
    ssd_conv_w = nrm((DEPTH, SSM_CONV, SSM_CONV_CH), SSM_CONV ** -0.5)
    ssd_conv_b = nrm((DEPTH, SSM_CONV_CH), 0.02)
    dt0 = jnp.exp(jax.random.uniform(next(ks), (DEPTH, 2, SSM_HEADS), f32, math.log(1e-3), math.log(1e-1)))
    ssd_dt_bias = dt0 + jnp.log(-jnp.expm1(-dt0))
    ssd_a_log = jnp.log(jax.random.uniform(next(ks), (DEPTH, 2, SSM_HEADS), f32, 1.0, 16.0))
    ssd_d = 1.0 + nrm((DEPTH, SSM_HEADS), 0.1)
    ssd_norm_w = 1.0 + nrm((DEPTH, D_SSM), 0.02)
    hy_conv_w = nrm((DEPTH, HY_CONV, HY_IN), HY_CONV ** -0.5)
    hy_conv_b = nrm((DEPTH, HY_IN), 0.02)
    hy_w1 = nrm((DEPTH, HY_EMB, HY_HIDDEN), HY_EMB ** -0.5)
    hy_b1 = nrm((DEPTH, HY_HIDDEN), 0.1)
    hy_w2 = nrm((DEPTH, HY_HIDDEN, HY_HIDDEN), HY_HIDDEN ** -0.5)
    hy_b2 = nrm((DEPTH, HY_HIDDEN), 0.1)
    hy_w3 = nrm((DEPTH, HY_HIDDEN, HY_FILT), 0.05 * HY_HIDDEN ** -0.5)
    hy_freq = 1.0 + nrm((DEPTH, HY_HIDDEN), 0.1)
    hy_bias = 1.0 + nrm((DEPTH, HY_ORDER, D_HY), 0.1)
    gamma0 = 1.0 - 2.0 ** (-5.0 - jnp.arange(RET_HEADS, dtype=f32))
    ret_decay_logit = jnp.log(gamma0 / (1.0 - gamma0)) + nrm((DEPTH, 2, RET_HEADS), 0.05)
    ret_gn_w = 1.0 + nrm((DEPTH, D_RET), 0.02)
    attn_q_norm = 1.0 + nrm((DEPTH, HEAD_DIM), 0.02)
    attn_k_norm = 1.0 + nrm((DEPTH, HEAD_DIM), 0.02)
    attn_sink = nrm((DEPTH, ATT_HEADS), 0.5)
    return {'x_prompt': x_prompt, 'x_sample': x_sample, 'cache_k': cache_k, 'cache_v': cache_v,
            'state_ssd': state_ssd, 'state_ret': state_ret, 'c': c, 'c_ctx': c_ctx,
            'w_mod': w_mod, 'b_mod': b_mod, 'norm_w': norm_w, 'ffn_w_in': ffn_w_in, 'ffn_w_out': ffn_w_out,
            'mix_w_in': mix_w_in, 'mix_w_out': mix_w_out, 'ssd_conv_w': ssd_conv_w, 'ssd_conv_b': ssd_conv_b,
            'ssd_dt_bias': ssd_dt_bias, 'ssd_a_log': ssd_a_log, 'ssd_d': ssd_d, 'ssd_norm_w': ssd_norm_w,
            'hy_conv_w': hy_conv_w, 'hy_conv_b': hy_conv_b, 'hy_w1': hy_w1, 'hy_b1': hy_b1, 'hy_w2': hy_w2,
            'hy_b2': hy_b2, 'hy_w3': hy_w3, 'hy_freq': hy_freq, 'hy_bias': hy_bias,
            'ret_decay_logit': ret_decay_logit, 'ret_gn_w': ret_gn_w, 'attn_q_norm': attn_q_norm,
            'attn_k_norm': attn_k_norm, 'attn_sink': attn_sink}


def reference(x_prompt, x_sample, cache_k, cache_v, state_ssd, state_ret, c, c_ctx,
              w_mod, b_mod, norm_w, ffn_w_in, ffn_w_out, mix_w_in, mix_w_out,
              ssd_conv_w, ssd_conv_b, ssd_dt_bias, ssd_a_log, ssd_d, ssd_norm_w,
              hy_conv_w, hy_conv_b, hy_w1, hy_b1, hy_w2, hy_b2, hy_w3, hy_freq, hy_bias,
              ret_decay_logit, ret_gn_w, attn_q_norm, attn_k_norm, attn_sink):
    bp = x_prompt.shape[0]
    yp, ys = x_prompt, x_sample
    new_k, new_v, new_ssd, new_ret = [], [], [], []
    for li in range(DEPTH):
        lp = dict(w_mod=w_mod[li], b_mod=b_mod[li], norm_w=norm_w[li], ffn_w_in=ffn_w_in[li],
                  ffn_w_out=ffn_w_out[li], mix_w_in=mix_w_in[li], mix_w_out=mix_w_out[li],
                  ssd_conv_w=ssd_conv_w[li], ssd_conv_b=ssd_conv_b[li], ssd_dt_bias=ssd_dt_bias[li],
                  ssd_a_log=ssd_a_log[li], ssd_d=ssd_d[li], ssd_norm_w=ssd_norm_w[li],
                  hy_conv_w=hy_conv_w[li], hy_conv_b=hy_conv_b[li], hy_w1=hy_w1[li], hy_b1=hy_b1[li],
                  hy_w2=hy_w2[li], hy_b2=hy_b2[li], hy_w3=hy_w3[li], hy_freq=hy_freq[li], hy_bias=hy_bias[li],
                  ret_decay_logit=ret_decay_logit[li], ret_gn_w=ret_gn_w[li], attn_q_norm=attn_q_norm[li],
                  attn_k_norm=attn_k_norm[li], attn_sink=attn_sink[li])
        zs_ssd = jnp.zeros((bp, 2, SSM_HEADS, SSM_STATE, SSM_HEAD_DIM), jnp.float32)
        zs_ret = jnp.zeros((bp, 2, RET_HEADS, RET_HEAD_DIM, RET_HEAD_DIM), jnp.float32)
        yp, s_ssd, s_ret, k_ctx, v_ctx = trunk_layer(yp, c_ctx[None, :], lp, zs_ssd, zs_ret, None)
        new_k.append(k_ctx)
        new_v.append(v_ctx)
        new_ssd.append(s_ssd)
        new_ret.append(s_ret)
        ys = trunk_layer(ys, c, lp, state_ssd[:, li], state_ret[:, li], (cache_k[:, li], cache_v[:, li]))[0]
    new_cache_k = jnp.stack(new_k, axis=1)
    new_cache_v = jnp.stack(new_v, axis=1)
    new_state_ssd = jnp.stack(new_ssd, axis=1)
    new_state_ret = jnp.stack(new_ret, axis=1)
    return (yp, ys, new_cache_k, new_cache_v, new_state_ssd, new_state_ret)
```

```python
import functools
import math

import jax
import jax.numpy as jnp
from jax import lax
from jax.experimental import pallas as pl
from jax.experimental.pallas import tpu as pltpu

F32 = jnp.float32
BF16 = jnp.bfloat16
HI = lax.Precision.HIGHEST

D_MODEL = 1024
DEPTH = 2
GRID_W = 64
D_FF = 2816
N_MOD = 9
NORM_EPS = 1e-6
CHUNK = 128
D_SSM = 256
SSM_HEADS = 4
SSM_HEAD_DIM = 64
SSM_STATE = 128
SSM_GROUPS = 2
SSM_CONV_CH = D_SSM + 2 * SSM_GROUPS * SSM_STATE
D_HY = 256
HY_ORDER = 2
HY_BANDS = 16
HY_EMB = 1 + 2 * HY_BANDS
HY_HIDDEN = 64
HY_FAST_DECAY = 0.3
HY_SLOW_DECAY = 1.5
HY_TARGET = 1e-2
HY_IN = (HY_ORDER + 1) * D_HY
D_RET = 256
RET_HEADS = 4
RET_HEAD_DIM = 64
RET_IN = 4 * D_RET
ATT_HEADS = 4
ATT_KV_HEADS = 2
HEAD_DIM = 64
D_ATT = ATT_HEADS * HEAD_DIM
D_KV = ATT_KV_HEADS * HEAD_DIM
ATT_IN = D_ATT + 2 * D_KV
WINDOW = 128
ATT_BLOCK = 128
ROPE_BASE = 10000.0
D_MIX = D_SSM + D_HY + D_RET + D_ATT

LANES = 128
DT_PAD = LANES
D_IN_PAD = D_SSM + SSM_CONV_CH + HY_IN + RET_IN + ATT_IN + DT_PAD
VMEM_LIMIT = 56 * 1024 * 1024

TOKEN_TILE = 512
FF_TILE = 256


def _cparams(*sem):
    return pltpu.CompilerParams(dimension_semantics=sem, vmem_limit_bytes=VMEM_LIMIT)


def _rms(x, w):
    return x * lax.rsqrt(jnp.mean(x * x, axis=-1, keepdims=True) + NORM_EPS) * w


def _silu(x):
    return x * (1.0 / (1.0 + jnp.exp(-x)))


def _softplus(x):
    return jnp.maximum(x, 0.0) + jnp.log1p(jnp.exp(-jnp.abs(x)))


def _bdot(a, b):
    return jnp.dot(a.astype(BF16), b.astype(BF16), preferred_element_type=F32)


def _bdot_nt(a, b):
    return lax.dot_general(a.astype(BF16), b.astype(BF16), (((1,), (1,)), ((), ())),
                           preferred_element_type=F32)


def _bdot_tn(a, b):
    return lax.dot_general(a.astype(BF16), b.astype(BF16), (((0,), (0,)), ((), ())),
                           preferred_element_type=F32)


def _hdot(a, b):
    return jnp.dot(a, b, preferred_element_type=F32, precision=HI)


def _full(shape):
    n = len(shape)
    return pl.BlockSpec(shape, lambda *_: (0,) * n)


def _mod_spec(k, tm, rows_per_mod, extra_grid):
    if extra_grid:
        return pl.BlockSpec((None, 1, D_MODEL), lambda i, j: ((i * tm) // rows_per_mod, 0, k))
    return pl.BlockSpec((None, 1, D_MODEL), lambda i: ((i * tm) // rows_per_mod, 0, k))


def _mod_kernel(c_ref, w_ref, b_ref, o_ref):
    c = c_ref[...]
    o_ref[...] = _bdot(_silu(c), w_ref[...]) + b_ref[...]


def _modulation(cond, w_mod, b_mod):
    return pl.pallas_call(
        _mod_kernel,
        grid=(DEPTH, N_MOD),
        in_specs=[pl.BlockSpec((8, D_MODEL), lambda l, j: (0, 0)),
                  pl.BlockSpec((None, D_MODEL, D_MODEL), lambda l, j: (l, 0, j)),
                  pl.BlockSpec((None, 1, D_MODEL), lambda l, j: (l, 0, j))],
        out_specs=pl.BlockSpec((None, 8, D_MODEL), lambda l, j: (l, 0, j)),
        out_shape=jax.ShapeDtypeStruct((DEPTH, 8, N_MOD * D_MODEL), F32),
        compiler_params=_cparams("arbitrary", "arbitrary"),
        name="modulation",
    )(cond, w_mod, b_mod.reshape(DEPTH, 1, N_MOD * D_MODEL))


def _ffn_kernel(nf, x_ref, sh_ref, sc_ref, g_ref, nw_ref, wg_ref, wu_ref, wo_ref, o_ref, h_ref, acc_ref):
    j = pl.program_id(1)

    @pl.when(j == 0)
    def _():
        h = _rms(x_ref[...], nw_ref[...]) * (1.0 + sc_ref[...]) + sh_ref[...]
        h_ref[...] = h.astype(BF16)
        acc_ref[...] = jnp.zeros_like(acc_ref)

    h = h_ref[...]
    gate = jnp.dot(h, wg_ref[...], preferred_element_type=F32)
    up = jnp.dot(h, wu_ref[...], preferred_element_type=F32)
    acc_ref[...] += _bdot(_silu(gate) * up, wo_ref[...])

    @pl.when(j == nf - 1)
    def _():
        o_ref[...] = x_ref[...] + 0.5 * g_ref[...] * acc_ref[...]


def _ffn(x, mod, k0, rows_per_mod, norm_w, w_in, w_out):
    t = x.shape[0]
    tm = min(TOKEN_TILE, rows_per_mod)
    nf = D_FF // FF_TILE
    return pl.pallas_call(
        functools.partial(_ffn_kernel, nf),
        grid=(t // tm, nf),
        in_specs=[pl.BlockSpec((tm, D_MODEL), lambda i, j: (i, 0)),
                  _mod_spec(k0, tm, rows_per_mod, True),
                  _mod_spec(k0 + 1, tm, rows_per_mod, True),
                  _mod_spec(k0 + 2, tm, rows_per_mod, True),
                  pl.BlockSpec((1, D_MODEL), lambda i, j: (0, 0)),
                  pl.BlockSpec((D_MODEL, FF_TILE), lambda i, j: (0, j)),
                  pl.BlockSpec((D_MODEL, FF_TILE), lambda i, j: (0, j + nf)),
                  pl.BlockSpec((FF_TILE, D_MODEL), lambda i, j: (j, 0))],
        out_specs=pl.BlockSpec((tm, D_MODEL), lambda i, j: (i, 0)),
        out_shape=jax.ShapeDtypeStruct((t, D_MODEL), F32),
        scratch_shapes=[pltpu.VMEM((tm, D_MODEL), BF16), pltpu.VMEM((tm, D_MODEL), F32)],
        compiler_params=_cparams("parallel", "arbitrary"),
        name="ffn",
    )(x, mod, mod, mod, norm_w.reshape(1, D_MODEL), w_in, w_in, w_out)


_IN_SPLITS = (("z", D_SSM), ("xbc", SSM_CONV_CH), ("hy", HY_IN), ("ret", RET_IN), ("att", ATT_IN), ("dt", DT_PAD))


def _inproj_kernel(x_ref, sh_ref, sc_ref, nw_ref, w_ref, *o_refs):
    h = (_rms(x_ref[...], nw_ref[...]) * (1.0 + sc_ref[...]) + sh_ref[...]).astype(BF16)
    off = 0
    for (_, width), o_ref in zip(_IN_SPLITS, o_refs):
        o_ref[...] = jnp.dot(h, w_ref[:, off:off + width], preferred_element_type=F32)
        off += width


def _inproj(x, mod, rows_per_mod, norm_w, w):
    t = x.shape[0]
    tm = min(TOKEN_TILE, rows_per_mod)
    return pl.pallas_call(
        _inproj_kernel,
        grid=(t // tm,),
        in_specs=[pl.BlockSpec((tm, D_MODEL), lambda i: (i, 0)),
                  _mod_spec(3, tm, rows_per_mod, False),
                  _mod_spec(4, tm, rows_per_mod, False),
                  pl.BlockSpec((1, D_MODEL), lambda i: (0, 0)),
                  pl.BlockSpec((D_MODEL, D_IN_PAD), lambda i: (0, 0))],
        out_specs=[pl.BlockSpec((tm, width), lambda i: (i, 0)) for _, width in _IN_SPLITS],
        out_shape=[jax.ShapeDtypeStruct((t, width), F32) for _, width in _IN_SPLITS],
        compiler_params=_cparams("parallel"),
        name="mix_in",
    )(x, mod, mod, norm_w.reshape(1, D_MODEL), w)


def _outproj_kernel(x_ref, g_ref, y0_ref, y1_ref, y2_ref, y3_ref, w_ref, o_ref):
    acc = _bdot(y0_ref[...], w_ref[0:256, :])
    acc += _bdot(y1_ref[...], w_ref[256:512, :])
    acc += _bdot(y2_ref[...], w_ref[512:768, :])
    acc += _bdot(y3_ref[...], w_ref[768:1024, :])
    o_ref[...] = x_ref[...] + g_ref[...] * acc


def _outproj(x, mod, rows_per_mod, ys, w):
    t = x.shape[0]
    tm = min(TOKEN_TILE, rows_per_mod)
    yspec = pl.BlockSpec((tm, 256), lambda i: (i, 0))
    return pl.pallas_call(
        _outproj_kernel,
        grid=(t // tm,),
        in_specs=[pl.BlockSpec((tm, D_MODEL), lambda i: (i, 0)),
                  _mod_spec(5, tm, rows_per_mod, False),
                  yspec, yspec, yspec, yspec,
                  pl.BlockSpec((D_MIX, D_MODEL), lambda i: (0, 0))],
        out_specs=pl.BlockSpec((tm, D_MODEL), lambda i: (i, 0)),
        out_shape=jax.ShapeDtypeStruct((t, D_MODEL), F32),
        compiler_params=_cparams("parallel"),
        name="mix_out",
    )(x, mod, *ys, w)


def _conv3_chunk(x_ref, c, nc, w, b):
    q = CHUNK
    l = nc * q
    r0 = pl.multiple_of(c * q, q)
    x = x_ref[pl.ds(r0, q), :]
    prev = x_ref[pl.ds(jnp.maximum(r0 - 1, 0), 1), :]
    nxt = x_ref[pl.ds(jnp.minimum(r0 + q, l - 1), 1), :]
    prev = jnp.where(c > 0, prev, 0.0)
    nxt = jnp.where(c < nc - 1, nxt, 0.0)
    rid = lax.broadcasted_iota(jnp.int32, (q, 1), 0)
    xm1 = jnp.where(rid == 0, prev, pltpu.roll(x, 1, 0))
    xp1 = jnp.where(rid == q - 1, nxt, pltpu.roll(x, q - 1, 0))
    return xm1 * w[0:1, :] + x * w[1:2, :] + xp1 * w[2:3, :] + b


def _tri(lower):
    i = lax.broadcasted_iota(jnp.int32, (CHUNK, CHUNK), 0)
    j = lax.broadcasted_iota(jnp.int32, (CHUNK, CHUNK), 1)
    return (j <= i) if lower else (j >= i)


def _rows(c):
    return pl.ds(pl.multiple_of(c * CHUNK, CHUNK), CHUNK)


def _ssd_kernel(nc, has_s0, *refs):
    if has_s0:
        (z_ref, xbc_ref, dt_ref, cw_ref, cb_ref, dtb_ref, alog_ref, dsk_ref, nw_ref, s0_ref,
         y_ref, s_ref, xs_scr, dt_scr, cf_scr, cr_scr, y_scr) = refs
    else:
        (z_ref, xbc_ref, dt_ref, cw_ref, cb_ref, dtb_ref, alog_ref, dsk_ref, nw_ref,
         y_ref, s_ref, xs_scr, dt_scr, cf_scr, cr_scr, y_scr) = refs
    q = CHUNK
    tril = _tri(True)
    triu = _tri(False)
    tril_f = tril.astype(F32)
    triu_f = triu.astype(F32)
    cw = cw_ref[...]
    cb = cb_ref[...]
    neg_a = -jnp.exp(alog_ref[...])
    dtb = dtb_ref[...]
    rep = SSM_HEADS // SSM_GROUPS

    def prep(c, carry):
        rows = _rows(c)
        xs_scr[rows, :] = _silu(_conv3_chunk(xbc_ref, c, nc, cw, cb))
        dt = _softplus(dt_ref[rows, :] + dtb)
        la = dt * neg_a
        dt_scr[rows, :] = dt
        cf_scr[rows, :] = _hdot(tril_f, la)
        cr_scr[rows, :] = _hdot(triu_f, la)
        return carry

    lax.fori_loop(0, nc, prep, 0)

    if has_s0:
        s_ref[...] = s0_ref[...]
    else:
        s_ref[...] = jnp.zeros_like(s_ref)

    def scan_chunk(c, direction):
        rows = _rows(c)
        xs = xs_scr[rows, 0:D_SSM]
        dt = dt_scr[rows, :]
        cum = (cf_scr if direction == 0 else cr_scr)[rows, :]
        dt_t = dt.T
        cum_t = cum.T
        mask = tril if direction == 0 else triu
        edge = q - 1 if direction == 0 else 0
        outs = []
        for g in range(SSM_GROUPS):
            bm = xs_scr[rows, D_SSM + g * SSM_STATE:D_SSM + (g + 1) * SSM_STATE]
            cm = xs_scr[rows, D_SSM + (SSM_GROUPS + g) * SSM_STATE:D_SSM + (SSM_GROUPS + g + 1) * SSM_STATE]
            cb_t = _bdot_nt(cm, bm)
            for h in range(g * rep, (g + 1) * rep):
                lane = direction * SSM_HEADS + h
                col = cum[:, lane:lane + 1]
                row = cum_t[lane:lane + 1, :]
                decay = jnp.exp(jnp.where(mask, col - row, -jnp.inf))
                m = cb_t * decay * dt_t[lane:lane + 1, :]
                xh = xs[:, h * SSM_HEAD_DIM:(h + 1) * SSM_HEAD_DIM]
                s_old = s_ref[direction, h]
                y = _bdot(m, xh) + jnp.exp(col) * _bdot(cm, s_old)
                outs.append(y)
                last = cum[edge:edge + 1, lane:lane + 1]
                wk = bm * (jnp.exp(last - col) * dt[:, lane:lane + 1])
                s_ref[direction, h] = jnp.exp(last) * s_old + _bdot_tn(wk, xh)
        return jnp.concatenate(outs, axis=-1)

    def fwd(c, carry):
        y_scr[_rows(c), :] = scan_chunk(c, 0)
        return carry

    def bwd(i, carry):
        c = nc - 1 - i
        y_scr[_rows(c), :] += scan_chunk(c, 1)
        return carry

    lax.fori_loop(0, nc, fwd, 0)
    lax.fori_loop(0, nc, bwd, 0)

    dsk = dsk_ref[...]
    nw = nw_ref[...]

    def fin(c, carry):
        rows = _rows(c)
        y = y_scr[rows, :] + dsk * xs_scr[rows, 0:D_SSM]
        y_ref[rows, :] = _rms(y * _silu(z_ref[rows, :]), nw)
        return carry

    lax.fori_loop(0, nc, fin, 0)


def _ssd(z, xbc, dt, conv_w, conv_b, dt_bias, a_log, d_skip, norm_w, s0, nb, l):
    nc = l // CHUNK
    has_s0 = s0 is not None
    pad8 = lambda a: jnp.pad(a.reshape(1, 2 * SSM_HEADS), ((0, 0), (0, DT_PAD - 2 * SSM_HEADS)))
    seq = lambda w: pl.BlockSpec((None, l, w), lambda b: (b, 0, 0))
    st_spec = pl.BlockSpec((None, 2, SSM_HEADS, SSM_STATE, SSM_HEAD_DIM), lambda b: (b, 0, 0, 0, 0))
    args = [z.reshape(nb, l, D_SSM), xbc.reshape(nb, l, SSM_CONV_CH), dt.reshape(nb, l, DT_PAD),
            conv_w, conv_b.reshape(1, SSM_CONV_CH), pad8(dt_bias), pad8(a_log),
            jnp.repeat(d_skip, SSM_HEAD_DIM).reshape(1, D_SSM), norm_w.reshape(1, D_SSM)]
    in_specs = [seq(D_SSM), seq(SSM_CONV_CH), seq(DT_PAD), _full((3, SSM_CONV_CH)), _full((1, SSM_CONV_CH)),
                _full((1, DT_PAD)), _full((1, DT_PAD)), _full((1, D_SSM)), _full((1, D_SSM))]
    if has_s0:
        args.append(s0)
        in_specs.append(st_spec)
    y, s = pl.pallas_call(
        functools.partial(_ssd_kernel, nc, has_s0),
        grid=(nb,),
        in_specs=in_specs,
        out_specs=[seq(D_SSM), st_spec],
        out_shape=[jax.ShapeDtypeStruct((nb, l, D_SSM), F32),
                   jax.ShapeDtypeStruct((nb, 2, SSM_HEADS, SSM_STATE, SSM_HEAD_DIM), F32)],
        scratch_shapes=[pltpu.VMEM((l, SSM_CONV_CH), F32), pltpu.VMEM((l, DT_PAD), F32),
                        pltpu.VMEM((l, DT_PAD), F32), pltpu.VMEM((l, DT_PAD), F32),
                        pltpu.VMEM((l, D_SSM), F32)],
        compiler_params=_cparams("parallel"),
        name="ssd",
    )(*args)
    return y.reshape(nb * l, D_SSM), s


def _ret_kernel(nc, has_s0, *refs):
    if has_s0:
        u_ref, dl_ref, gn_ref, s0_ref, y_ref, s_ref, y_scr = refs
    else:
        u_ref, dl_ref, gn_ref, y_ref, s_ref, y_scr = refs
    q = CHUNK
    hd = RET_HEAD_DIM
    dl = dl_ref[...]
    log_g = -_softplus(-dl)
    ii = lax.broadcasted_iota(jnp.int32, (q, q), 0)
    jj = lax.broadcasted_iota(jnp.int32, (q, q), 1)
    dij = (ii - jj).astype(F32)
    ri = lax.broadcasted_iota(jnp.int32, (q, 1), 0).astype(F32)
    dmat, e_in, e_tail, e_all = [], [], [], []
    for h in range(RET_HEADS):
        lf = log_g[:, h:h + 1]
        lb = log_g[:, RET_HEADS + h:RET_HEADS + h + 1]
        d_f = jnp.exp(jnp.where(dij >= 0, dij * lf, -jnp.inf))
        d_b = jnp.exp(jnp.where(dij <= 0, -dij * lb, -jnp.inf))
        dmat.append(d_f + d_b)
        e_in.append((jnp.exp((ri + 1.0) * lf), jnp.exp((q - ri) * lb)))
        e_tail.append((jnp.exp((q - 1.0 - ri) * lf), jnp.exp(ri * lb)))
        e_all.append((jnp.exp(q * lf), jnp.exp(q * lb)))

    if has_s0:
        s_ref[...] = s0_ref[...]
    else:
        s_ref[...] = jnp.zeros_like(s_ref)

    def head_slices(rows, h):
        qh = u_ref[rows, h * hd:(h + 1) * hd]
        kh = u_ref[rows, D_RET + h * hd:D_RET + (h + 1) * hd] * (RET_HEAD_DIM ** -0.5)
        vh = u_ref[rows, 2 * D_RET + h * hd:2 * D_RET + (h + 1) * hd]
        return qh, kh, vh

    def state_step(direction, h, qh, kh, vh):
        s_old = s_ref[direction, h]
        y = e_in[h][direction] * _bdot(qh, s_old)
        s_ref[direction, h] = e_all[h][direction] * s_old + _bdot_tn(kh * e_tail[h][direction], vh)
        return y

    def fwd(c, carry):
        rows = _rows(c)
        outs = []
        for h in range(RET_HEADS):
            qh, kh, vh = head_slices(rows, h)
            y = _bdot(_bdot_nt(qh, kh) * dmat[h], vh)
            outs.append(y + state_step(0, h, qh, kh, vh))
        y_scr[rows, :] = jnp.concatenate(outs, axis=-1)
        return carry

    def bwd(i, carry):
        rows = _rows(nc - 1 - i)
        outs = []
        for h in range(RET_HEADS):
            qh, kh, vh = head_slices(rows, h)
            outs.append(state_step(1, h, qh, kh, vh))
        y_scr[rows, :] += jnp.concatenate(outs, axis=-1)
        return carry

    lax.fori_loop(0, nc, fwd, 0)
    lax.fori_loop(0, nc, bwd, 0)

    gn = gn_ref[...]

    def fin(c, carry):
        rows = _rows(c)
        y = y_scr[rows, :]
        outs = []
        for h in range(RET_HEADS):
            yh = y[:, h * hd:(h + 1) * hd]
            mu = jnp.mean(yh, axis=-1, keepdims=True)
            var = jnp.mean(jnp.square(yh - mu), axis=-1, keepdims=True)
            outs.append((yh - mu) * lax.rsqrt(var + NORM_EPS))
        yn = jnp.concatenate(outs, axis=-1) * gn
        y_ref[rows, :] = yn * _silu(u_ref[rows, 3 * D_RET:4 * D_RET])
        return carry

    lax.fori_loop(0, nc, fin, 0)


def _retention(u, decay_logit, gn_w, s0, nb, l):
    nc = l // CHUNK
    has_s0 = s0 is not None
    st_spec = pl.BlockSpec((None, 2, RET_HEADS, RET_HEAD_DIM, RET_HEAD_DIM), lambda b: (b, 0, 0, 0, 0))
    args = [u.reshape(nb, l, RET_IN),
            jnp.pad(decay_logit.reshape(1, 2 * RET_HEADS), ((0, 0), (0, LANES - 2 * RET_HEADS))),
            gn_w.reshape(1, D_RET)]
    in_specs = [pl.BlockSpec((None, l, RET_IN), lambda b: (b, 0, 0)), _full((1, LANES)), _full((1, D_RET))]
    if has_s0:
        args.append(s0)
        in_specs.append(st_spec)
    y, s = pl.pallas_call(
        functools.partial(_ret_kernel, nc, has_s0),
        grid=(nb,),
        in_specs=in_specs,
        out_specs=[pl.BlockSpec((None, l, D_RET), lambda b: (b, 0, 0)), st_spec],
        out_shape=[jax.ShapeDtypeStruct((nb, l, D_RET), F32),
                   jax.ShapeDtypeStruct((nb, 2, RET_HEADS, RET_HEAD_DIM, RET_HEAD_DIM), F32)],
        scratch_shapes=[pltpu.VMEM((l, D_RET), F32)],
        compiler_params=_cparams("parallel"),
        name="retention",
    )(*args)
    return y.reshape(nb * l, D_RET), s


def _dft_tables(l):
    n = 2 * l
    f = jnp.arange(l, dtype=jnp.int32)[:, None]
    s = jnp.arange(l, dtype=jnp.int32)[None, :]
    ang = ((f * s) % n).astype(F32) * (2.0 * math.pi / n)
    re = jnp.cos(ang)
    im = -jnp.sin(ang)
    nyq = jnp.where(jnp.arange(l) % 2 == 0, 1.0, -1.0).astype(F32)[None, :]
    im = jnp.concatenate([nyq, im[1:]], axis=0)
    fwd = jnp.concatenate([re, im], axis=0)
    wgt = jnp.full((n, 1), 2.0, F32).at[0].set(1.0).at[l].set(1.0) / n
    return fwd, (fwd * wgt).T


def _hy_filter_kernel(l, feats_ref, dec_ref, w1_ref, b1_ref, w2_ref, b2_ref, w3_ref, fr_ref, f_ref,
                      a_ref, b_ref, d_ref):
    fr = fr_ref[...]
    h = jnp.sin(fr * (_hdot(feats_ref[...], w1_ref[...]) + b1_ref[...]))
    h = jnp.sin(fr * (_hdot(h, w2_ref[...]) + b2_ref[...]))
    h = _hdot(h, w3_ref[...])
    dec = dec_ref[...]
    row0 = lax.broadcasted_iota(jnp.int32, (l, 1), 0) == 0
    for o in range(HY_ORDER):
        hf = h[:, o * D_HY:(o + 1) * D_HY] * dec
        hb = h[:, (HY_ORDER + o) * D_HY:(HY_ORDER + o + 1) * D_HY] * dec
        hb = jnp.where(row0, 0.0, hb)
        hs = hf + hb
        hdiff = hf - hb
        re = _hdot(f_ref[0:l, :], hs)
        ny = _hdot(f_ref[l:2 * l, :], hs)
        im = _hdot(f_ref[l:2 * l, :], hdiff)
        a_ref[o] = re
        b_ref[o] = jnp.where(row0, 0.0, im)
        d_ref[o] = jnp.where(row0, ny, re)


def _hy_filter(l, fwd, w1, b1, w2, b2, w3, freq):
    pos = jnp.arange(l, dtype=F32)
    t = pos / (l - 1)
    bands = jnp.linspace(1e-4, HY_BANDS - 1, HY_BANDS, dtype=F32)
    ang = (2.0 * math.pi / l) * pos[:, None] * bands[None, :]
    feats = jnp.concatenate([t[:, None], jnp.cos(ang), -jnp.sin(ang)], axis=-1)
    feats = jnp.pad(feats, ((0, 0), (0, LANES - HY_EMB)))
    max_decay = math.log(HY_TARGET) / HY_FAST_DECAY
    min_decay = math.log(HY_TARGET) / HY_SLOW_DECAY
    deltas = jnp.abs(jnp.linspace(min_decay, max_decay, D_HY, dtype=F32))
    dec = jnp.exp(-t[:, None] * deltas[None, :])
    w1p = jnp.pad(w1, ((0, LANES - HY_EMB), (0, 0)))
    spec = jax.ShapeDtypeStruct((HY_ORDER, l, D_HY), F32)
    args = [feats, dec, w1p, b1.reshape(1, HY_HIDDEN), w2, b2.reshape(1, HY_HIDDEN), w3,
            freq.reshape(1, HY_HIDDEN), fwd]
    return pl.pallas_call(
        functools.partial(_hy_filter_kernel, l),
        grid=(1,),
        in_specs=[_full(a.shape) for a in args],
        out_specs=[_full(spec.shape)] * 3,
        out_shape=[spec] * 3,
        compiler_params=_cparams("arbitrary"),
        name="hyena_filter",
    )(*args)


def _hy_kernel(nc, u_ref, cw_ref, cb_ref, f_ref, g_ref, a_ref, b_ref, d_ref, bias_ref, y_ref,
               uc_scr, sp_scr, z_scr):
    l = nc * CHUNK
    cw = cw_ref[...]
    cb = cb_ref[...]

    def conv(c, carry):
        uc_scr[_rows(c), :] = _conv3_chunk(u_ref, c, nc, cw, cb)
        return carry

    lax.fori_loop(0, nc, conv, 0)

    def long_conv(o, src_ref, src_col, gate_col, dst_ref):
        def spectrum(c, carry):
            rows = _rows(c)
            rows_im = pl.ds(pl.multiple_of(l + c * CHUNK, CHUNK), CHUNK)
            v = src_ref[:, src_col:src_col + D_HY]
            zr = _hdot(f_ref[rows, :], v)
            zi = _hdot(f_ref[rows_im, :], v)
            sp_scr[rows, :] = zr * a_ref[o, rows, :] - zi * b_ref[o, rows, :]
            sp_scr[rows_im, :] = zr * b_ref[o, rows, :] + zi * d_ref[o, rows, :]
            return carry

        lax.fori_loop(0, nc, spectrum, 0)

        def inverse(c, carry):
            rows = _rows(c)
            y = _hdot(g_ref[rows, :], sp_scr[...])
            v = src_ref[rows, src_col:src_col + D_HY]
            gate = uc_scr[rows, gate_col:gate_col + D_HY]
            dst_ref[rows, :] = gate * (y + v * bias_ref[o:o + 1, :])
            return carry

        lax.fori_loop(0, nc, inverse, 0)

    long_conv(0, uc_scr, 0, D_HY, z_scr)
    long_conv(1, z_scr, 0, 2 * D_HY, y_ref)


def _hyena(u, conv_w, conv_b, tables, spectra, d_bias, nb, l):
    nc = l // CHUNK
    fwd, inv = tables
    a, b, d = spectra
    once = pl.Buffered(1)
    const = lambda shape: pl.BlockSpec(shape, lambda i: (0,) * len(shape), pipeline_mode=once)
    y = pl.pallas_call(
        functools.partial(_hy_kernel, nc),
        grid=(nb,),
        in_specs=[pl.BlockSpec((None, l, HY_IN), lambda i: (i, 0, 0)),
                  _full((3, HY_IN)), _full((1, HY_IN)),
                  const((2 * l, l)), const((l, 2 * l)),
                  const((HY_ORDER, l, D_HY)), const((HY_ORDER, l, D_HY)), const((HY_ORDER, l, D_HY)),
                  _full((HY_ORDER, D_HY))],
        out_specs=pl.BlockSpec((None, l, D_HY), lambda i: (i, 0, 0)),
        out_shape=jax.ShapeDtypeStruct((nb, l, D_HY), F32),
        scratch_shapes=[pltpu.VMEM((l, HY_IN), F32), pltpu.VMEM((2 * l, D_HY), F32),
                        pltpu.VMEM((l, D_HY), F32)],
        compiler_params=_cparams("parallel"),
        name="hyena",
    )(u.reshape(nb, l, HY_IN), conv_w, conv_b.reshape(1, HY_IN), fwd, inv, a, b, d, d_bias)
    return y.reshape(nb * l, D_HY)


def _head_rms(x, w, heads):
    outs = []
    for h in range(heads):
        outs.append(_rms(x[:, h * HEAD_DIM:(h + 1) * HEAD_DIM], w))
    return jnp.concatenate(outs, axis=-1)


def _ctx_attn_kernel(u_ref, qn_ref, kn_ref, sink_ref, y_ref, k_ref, v_ref):
    u = u_ref[...]
    qn = qn_ref[...]
    kn = kn_ref[...]
    q = _head_rms(u[:, 0:D_ATT], qn, ATT_HEADS)
    k = _head_rms(u[:, D_ATT:D_ATT + D_KV], kn, ATT_KV_HEADS)
    v = u[:, D_ATT + D_KV:D_ATT + 2 * D_KV]
    k_ref[...] = k
    v_ref[...] = v
    grp = ATT_HEADS // ATT_KV_HEADS
    scale = HEAD_DIM ** -0.5
    outs = []
    for h in range(ATT_HEADS):
        kv = h // grp
        kh = k[:, kv * HEAD_DIM:(kv + 1) * HEAD_DIM]
        vh = v[:, kv * HEAD_DIM:(kv + 1) * HEAD_DIM]
        s = _bdot_nt(q[:, h * HEAD_DIM:(h + 1) * HEAD_DIM], kh) * scale
        sink = sink_ref[h]
        m = jnp.maximum(jnp.max(s, axis=-1, keepdims=True), sink)
        p = jnp.exp(s - m)
        den = jnp.sum(p, axis=-1, keepdims=True) + jnp.exp(sink - m)
        outs.append(_bdot(p / den, vh))
    y_ref[...] = jnp.concatenate(outs, axis=-1)


def _ctx_attention(u, q_norm, k_norm, sink, nb, l):
    y, k, v = pl.pallas_call(
        _ctx_attn_kernel,
        grid=(nb,),
        in_specs=[pl.BlockSpec((None, l, ATT_IN), lambda b: (b, 0, 0)),
                  _full((1, HEAD_DIM)), _full((1, HEAD_DIM)),
                  pl.BlockSpec(memory_space=pltpu.SMEM)],
        out_specs=[pl.BlockSpec((None, l, D_ATT), lambda b: (b, 0, 0)),
                   pl.BlockSpec((None, l, D_KV), lambda b: (b, 0, 0)),
                   pl.BlockSpec((None, l, D_KV), lambda b: (b, 0, 0))],
        out_shape=[jax.ShapeDtypeStruct((nb, l, D_ATT), F32),
                   jax.ShapeDtypeStruct((nb, l, D_KV), F32),
                   jax.ShapeDtypeStruct((nb, l, D_KV), F32)],
        compiler_params=_cparams("parallel"),
        name="ctx_attention",
    )(u.reshape(nb, l, ATT_IN), q_norm.reshape(1, HEAD_DIM), k_norm.reshape(1, HEAD_DIM), sink)
    return y.reshape(nb * l, D_ATT), k, v


def _rope_tables(l):
    n_rows = l // GRID_W
    rows = jnp.repeat(jnp.arange(n_rows, dtype=F32), GRID_W)
    cols = jnp.tile(jnp.arange(GRID_W, dtype=F32), n_rows)
    nf = HEAD_DIM // 4
    inv = ROPE_BASE ** (-jnp.arange(nf, dtype=F32) / nf)
    ar = rows[:, None] * inv[None, :]
    ac = cols[:, None] * inv[None, :]
    cos = jnp.concatenate([jnp.cos(ar), jnp.cos(ar), jnp.cos(ac), jnp.cos(ac)], axis=-1)
    sin = jnp.concatenate([-jnp.sin(ar), jnp.sin(ar), -jnp.sin(ac), jnp.sin(ac)], axis=-1)
    return cos, sin


def _rope(x, cos, sin):
    w = x.shape[-1]
    nf = HEAD_DIM // 4
    lane = lax.broadcasted_iota(jnp.int32, x.shape, 1)
    first = (lane % (2 * nf)) < nf
    partner = jnp.where(first, pltpu.roll(x, w - nf, 1), pltpu.roll(x, nf, 1))
    return x * cos + partner * sin


def _lat_attn_kernel(nblk, lc, u_ref, qn_ref, kn_ref, cos_ref, sin_ref, ck_ref, cv_ref, sink_ref,
                     y_ref, q_scr, k_scr, v_scr):
    blk = ATT_BLOCK
    l = nblk * blk
    cos_q = cos_ref[...]
    sin_q = sin_ref[...]
    u = u_ref[...]
    q = _head_rms(u[:, 0:D_ATT], qn_ref[...], ATT_HEADS)
    k = _head_rms(u[:, D_ATT:D_ATT + D_KV], kn_ref[...], ATT_KV_HEADS)
    q_scr[...] = _rope(q, cos_q, sin_q)
    zeros = jnp.zeros((blk, D_KV), F32)
    k_scr[0:blk, :] = zeros
    k_scr[blk + l:2 * blk + l, :] = zeros
    v_scr[0:blk, :] = zeros
    v_scr[blk + l:2 * blk + l, :] = zeros
    k_scr[blk:blk + l, :] = _rope(k, cos_q[:, 0:D_KV], sin_q[:, 0:D_KV])
    v_scr[blk:blk + l, :] = u[:, D_ATT + D_KV:D_ATT + 2 * D_KV]

    grp = ATT_HEADS // ATT_KV_HEADS
    scale = HEAD_DIM ** -0.5
    r = lax.broadcasted_iota(jnp.int32, (blk, 3 * blk), 0)
    cidx = lax.broadcasted_iota(jnp.int32, (blk, 3 * blk), 1)
    band = (cidx - r >= blk - WINDOW) & (cidx - r <= blk + WINDOW)

    def block(i, carry):
        rows = _rows(i)
        win = pl.ds(pl.multiple_of(i * blk, blk), 3 * blk)
        kpos = cidx + (i - 1) * blk
        valid = band & (kpos >= 0) & (kpos < l)
        outs = []
        for h in range(ATT_HEADS):
            kv = h // grp
            hs = slice(kv * HEAD_DIM, (kv + 1) * HEAD_DIM)
            qh = q_scr[rows, h * HEAD_DIM:(h + 1) * HEAD_DIM]
            s_loc = _bdot_nt(qh, k_scr[win, hs]) * scale
            s_loc = jnp.where(valid, s_loc, -jnp.inf)
            s_ctx = _bdot_nt(qh, ck_ref[:, hs]) * scale
            sink = sink_ref[h]
            m = jnp.maximum(jnp.maximum(jnp.max(s_loc, axis=-1, keepdims=True),
                                        jnp.max(s_ctx, axis=-1, keepdims=True)), sink)
            p_loc = jnp.exp(s_loc - m)
            p_ctx = jnp.exp(s_ctx - m)
            den = (jnp.sum(p_loc, axis=-1, keepdims=True) + jnp.sum(p_ctx, axis=-1, keepdims=True)
                   + jnp.exp(sink - m))
            inv = 1.0 / den
            outs.append(_bdot(p_ctx * inv, cv_ref[:, hs]) + _bdot(p_loc * inv, v_scr[win, hs]))
        y_ref[rows, :] = jnp.concatenate(outs, axis=-1)
        return carry

    lax.fori_loop(0, nblk, block, 0)


def _lat_attention(u, q_norm, k_norm, sink, ck, cv, nb, l):
    lc = ck.shape[1]
    nblk = l // ATT_BLOCK
    cos, sin = _rope_tables(l)
    cos = jnp.tile(cos, (1, ATT_HEADS))
    sin = jnp.tile(sin, (1, ATT_HEADS))
    y = pl.pallas_call(
        functools.partial(_lat_attn_kernel, nblk, lc),
        grid=(nb,),
        in_specs=[pl.BlockSpec((None, l, ATT_IN), lambda b: (b, 0, 0)),
                  _full((1, HEAD_DIM)), _full((1, HEAD_DIM)),
                  _full((l, D_ATT)), _full((l, D_ATT)),
                  pl.BlockSpec((None, lc, D_KV), lambda b: (b, 0, 0)),
                  pl.BlockSpec((None, lc, D_KV), lambda b: (b, 0, 0)),
                  pl.BlockSpec(memory_space=pltpu.SMEM)],
        out_specs=pl.BlockSpec((None, l, D_ATT), lambda b: (b, 0, 0)),
        out_shape=jax.ShapeDtypeStruct((nb, l, D_ATT), F32),
        scratch_shapes=[pltpu.VMEM((l, D_ATT), F32), pltpu.VMEM((l + 2 * ATT_BLOCK, D_KV), F32),
                        pltpu.VMEM((l + 2 * ATT_BLOCK, D_KV), F32)],
        compiler_params=_cparams("parallel"),
        name="lat_attention",
    )(u.reshape(nb, l, ATT_IN), q_norm.reshape(1, HEAD_DIM), k_norm.reshape(1, HEAD_DIM), cos, sin,
      ck.reshape(nb, lc, D_KV), cv.reshape(nb, lc, D_KV), sink)
    return y.reshape(nb * l, D_ATT)


def _pack_mix_w_in(w):
    c0 = D_SSM + SSM_CONV_CH
    n_dt = 2 * SSM_HEADS
    zeros = jnp.zeros((w.shape[0], DT_PAD - n_dt), w.dtype)
    return jnp.concatenate([w[:, :c0], w[:, c0 + n_dt:], w[:, c0:c0 + n_dt], zeros], axis=1).astype(BF16)


def _layer(x, mod, lp, nb, l, ssd_s0, ret_s0, ctx_kv, hy_tables, hy_spectra):
    rpm = l if mod.shape[0] > 1 else nb * l
    x = _ffn(x, mod, 0, rpm, lp["norm_w"][0], lp["ffn_w_in"][0], lp["ffn_w_out"][0])
    z, xbc, hy, ret, att, dt = _inproj(x, mod, rpm, lp["norm_w"][1], lp["mix_w_in"])
    y_ssd, s_ssd = _ssd(z, xbc, dt, lp["ssd_conv_w"], lp["ssd_conv_b"], lp["ssd_dt_bias"], lp["ssd_a_log"],
                        lp["ssd_d"], lp["ssd_norm_w"], ssd_s0, nb, l)
    y_hy = _hyena(hy, lp["hy_conv_w"], lp["hy_conv_b"], hy_tables, hy_spectra, lp["hy_bias"], nb, l)
    y_ret, s_ret = _retention(ret, lp["ret_decay_logit"], lp["ret_gn_w"], ret_s0, nb, l)
    if ctx_kv is None:
        y_att, k, v = _ctx_attention(att, lp["attn_q_norm"], lp["attn_k_norm"], lp["attn_sink"], nb, l)
    else:
        y_att = _lat_attention(att, lp["attn_q_norm"], lp["attn_k_norm"], lp["attn_sink"],
                               ctx_kv[0], ctx_kv[1], nb, l)
        k = v = None
    x = _outproj(x, mod, rpm, (y_ssd, y_hy, y_ret, y_att), lp["mix_w_out"])
    x = _ffn(x, mod, 6, rpm, lp["norm_w"][2], lp["ffn_w_in"][1], lp["ffn_w_out"][1])
    return x, s_ssd, s_ret, k, v


def kernel(x_prompt, x_sample, cache_k, cache_v, state_ssd, state_ret, c, c_ctx, w_mod, b_mod, norm_w, ffn_w_in, ffn_w_out, mix_w_in, mix_w_out, ssd_conv_w, ssd_conv_b, ssd_dt_bias, ssd_a_log, ssd_d, ssd_norm_w, hy_conv_w, hy_conv_b, hy_w1, hy_b1, hy_w2, hy_b2, hy_w3, hy_freq, hy_bias, ret_decay_logit, ret_gn_w, attn_q_norm, attn_k_norm, attn_sink):
    bp, lp_len, _ = x_prompt.shape
    bs, ls_len, _ = x_sample.shape

    cond = jnp.concatenate([c_ctx[None, :], c, jnp.zeros((8 - 1 - bs, D_MODEL), F32)], axis=0)
    mod = _modulation(cond, w_mod, b_mod)

    tab_p = _dft_tables(lp_len)
    tab_s = _dft_tables(ls_len)

    yp = x_prompt.reshape(bp * lp_len, D_MODEL)
    ys = x_sample.reshape(bs * ls_len, D_MODEL)
    new_k, new_v, new_ssd, new_ret = [], [], [], []
    for li in range(DEPTH):
        lp = dict(norm_w=norm_w[li], ffn_w_in=ffn_w_in[li].astype(BF16), ffn_w_out=ffn_w_out[li].astype(BF16),
                  mix_w_in=_pack_mix_w_in(mix_w_in[li]), mix_w_out=mix_w_out[li].astype(BF16),
                  ssd_conv_w=ssd_conv_w[li], ssd_conv_b=ssd_conv_b[li], ssd_dt_bias=ssd_dt_bias[li],
                  ssd_a_log=ssd_a_log[li], ssd_d=ssd_d[li], ssd_norm_w=ssd_norm_w[li],
                  hy_conv_w=hy_conv_w[li], hy_conv_b=hy_conv_b[li], hy_bias=hy_bias[li],
                  ret_decay_logit=ret_decay_logit[li], ret_gn_w=ret_gn_w[li], attn_q_norm=attn_q_norm[li],
                  attn_k_norm=attn_k_norm[li], attn_sink=attn_sink[li])
        filt = (hy_w1[li], hy_b1[li], hy_w2[li], hy_b2[li], hy_w3[li], hy_freq[li])
        spec_p = _hy_filter(lp_len, tab_p[0], *filt)
        spec_s = _hy_filter(ls_len, tab_s[0], *filt)
        mod_p = mod[li, 0:1].reshape(1, 1, N_MOD * D_MODEL)
        mod_s = mod[li, 1:1 + bs].reshape(bs, 1, N_MOD * D_MODEL)

        yp, s_ssd, s_ret, k_ctx, v_ctx = _layer(yp, mod_p, lp, bp, lp_len, None, None, None, tab_p, spec_p)
        new_k.append(k_ctx.reshape(bp, lp_len, ATT_KV_HEADS, HEAD_DIM))
        new_v.append(v_ctx.reshape(bp, lp_len, ATT_KV_HEADS, HEAD_DIM))
        new_ssd.append(s_ssd)
        new_ret.append(s_ret)
        ys = _layer(ys, mod_s, lp, bs, ls_len, state_ssd[:, li], state_ret[:, li],
                    (cache_k[:, li], cache_v[:, li]), tab_s, spec_s)[0]

    return (yp.reshape(bp, lp_len, D_MODEL), ys.reshape(bs, ls_len, D_MODEL),
            jnp.stack(new_k, axis=1), jnp.stack(new_v, axis=1),
            jnp.stack(new_ssd, axis=1), jnp.stack(new_ret, axis=1))
```

```python
import functools
import math

import jax
import jax.numpy as jnp
from jax import lax
from jax.experimental import pallas as pl
from jax.experimental.pallas import tpu as pltpu

F32 = jnp.float32
BF16 = jnp.bfloat16
HI = lax.Precision.HIGHEST

D_MODEL = 1024
DEPTH = 2
GRID_W = 64
D_FF = 2816
N_MOD = 9
NORM_EPS = 1e-6
CHUNK = 128
D_SSM = 256
SSM_HEADS = 4
SSM_HEAD_DIM = 64
SSM_STATE = 128
SSM_GROUPS = 2
SSM_CONV_CH = D_SSM + 2 * SSM_GROUPS * SSM_STATE
D_HY = 256
HY_ORDER = 2
HY_BANDS = 16
HY_EMB = 1 + 2 * HY_BANDS
HY_HIDDEN = 64
HY_FAST_DECAY = 0.3
HY_SLOW_DECAY = 1.5
HY_TARGET = 1e-2
HY_IN = (HY_ORDER + 1) * D_HY
D_RET = 256
RET_HEADS = 4
RET_HEAD_DIM = 64
RET_IN = 4 * D_RET
ATT_HEADS = 4
ATT_KV_HEADS = 2
HEAD_DIM = 64
D_ATT = ATT_HEADS * HEAD_DIM
D_KV = ATT_KV_HEADS * HEAD_DIM
ATT_IN = D_ATT + 2 * D_KV
WINDOW = 128
ATT_BLOCK = 128
ROPE_BASE = 10000.0
D_MIX = D_SSM + D_HY + D_RET + D_ATT

LANES = 128
DT_PAD = LANES
D_IN_PAD = D_SSM + SSM_CONV_CH + HY_IN + RET_IN + ATT_IN + DT_PAD
VMEM_LIMIT = 56 * 1024 * 1024

TOKEN_TILE = 512


def _cparams(*sem):
    return pltpu.CompilerParams(dimension_semantics=sem, vmem_limit_bytes=VMEM_LIMIT)


def _rms(x, w):
    return x * lax.rsqrt(jnp.mean(x * x, axis=-1, keepdims=True) + NORM_EPS) * w


def _silu(x):
    return x * (1.0 / (1.0 + jnp.exp(-x)))


def _softplus(x):
    return jnp.maximum(x, 0.0) + jnp.log1p(jnp.exp(-jnp.abs(x)))


def _bdot(a, b):
    return jnp.dot(a.astype(BF16), b.astype(BF16), preferred_element_type=F32)


def _bdot_nt(a, b):
    return lax.dot_general(a.astype(BF16), b.astype(BF16), (((1,), (1,)), ((), ())),
                           preferred_element_type=F32)


def _bdot_tn(a, b):
    return lax.dot_general(a.astype(BF16), b.astype(BF16), (((0,), (0,)), ((), ())),
                           preferred_element_type=F32)


def _hdot(a, b):
    return jnp.dot(a, b, preferred_element_type=F32, precision=HI)


def _full(shape):
    n = len(shape)
    return pl.BlockSpec(shape, lambda *_: (0,) * n)


def _const(shape):
    n = len(shape)
    return pl.BlockSpec(shape, lambda *_: (0,) * n, pipeline_mode=pl.Buffered(1))


def _mod_spec(k, tm, rows_per_mod):
    return pl.BlockSpec((None, 1, D_MODEL), lambda i: ((i * tm) // rows_per_mod, 0, k))


def _mod_kernel(c_ref, w_ref, b_ref, o_ref):
    c = c_ref[...]
    o_ref[...] = _bdot(_silu(c), w_ref[...]) + b_ref[...]


def _modulation(cond, w_mod, b_mod):
    return pl.pallas_call(
        _mod_kernel,
        grid=(DEPTH, N_MOD),
        in_specs=[pl.BlockSpec((8, D_MODEL), lambda l, j: (0, 0)),
                  pl.BlockSpec((None, D_MODEL, D_MODEL), lambda l, j: (l, 0, j)),
                  pl.BlockSpec((None, 1, D_MODEL), lambda l, j: (l, 0, j))],
        out_specs=pl.BlockSpec((None, 8, D_MODEL), lambda l, j: (l, 0, j)),
        out_shape=jax.ShapeDtypeStruct((DEPTH, 8, N_MOD * D_MODEL), F32),
        compiler_params=_cparams("arbitrary", "arbitrary"),
        name="modulation",
    )(cond, w_mod, b_mod.reshape(DEPTH, 1, N_MOD * D_MODEL))


def _ffn_kernel(x_ref, sh_ref, sc_ref, g_ref, nw_ref, wi_ref, wo_ref, o_ref):
    x = x_ref[...]
    h = (_rms(x, nw_ref[...]) * (1.0 + sc_ref[...]) + sh_ref[...]).astype(BF16)
    gate = jnp.dot(h, wi_ref[:, 0:D_FF], preferred_element_type=F32)
    up = jnp.dot(h, wi_ref[:, D_FF:2 * D_FF], preferred_element_type=F32)
    o_ref[...] = x + 0.5 * g_ref[...] * _bdot(_silu(gate) * up, wo_ref[...])


def _ffn(x, mod, k0, rows_per_mod, norm_w, w_in, w_out):
    t = x.shape[0]
    tm = min(TOKEN_TILE, rows_per_mod)
    return pl.pallas_call(
        _ffn_kernel,
        grid=(t // tm,),
        in_specs=[pl.BlockSpec((tm, D_MODEL), lambda i: (i, 0)),
                  _mod_spec(k0, tm, rows_per_mod),
                  _mod_spec(k0 + 1, tm, rows_per_mod),
                  _mod_spec(k0 + 2, tm, rows_per_mod),
                  _const((1, D_MODEL)), _const((D_MODEL, 2 * D_FF)), _const((D_FF, D_MODEL))],
        out_specs=pl.BlockSpec((tm, D_MODEL), lambda i: (i, 0)),
        out_shape=jax.ShapeDtypeStruct((t, D_MODEL), F32),
        compiler_params=_cparams("parallel"),
        name="ffn",
    )(x, mod, mod, mod, norm_w.reshape(1, D_MODEL), w_in, w_out)


_IN_SPLITS = (("z", D_SSM), ("xbc", SSM_CONV_CH), ("hy", HY_IN), ("ret", RET_IN), ("att", ATT_IN), ("dt", DT_PAD))


def _inproj_kernel(x_ref, sh_ref, sc_ref, nw_ref, w_ref, *o_refs):
    h = (_rms(x_ref[...], nw_ref[...]) * (1.0 + sc_ref[...]) + sh_ref[...]).astype(BF16)
    off = 0
    for (_, width), o_ref in zip(_IN_SPLITS, o_refs):
        o_ref[...] = jnp.dot(h, w_ref[:, off:off + width], preferred_element_type=F32)
        off += width


def _inproj(x, mod, rows_per_mod, norm_w, w):
    t = x.shape[0]
    tm = min(TOKEN_TILE, rows_per_mod)
    return pl.pallas_call(
        _inproj_kernel,
        grid=(t // tm,),
        in_specs=[pl.BlockSpec((tm, D_MODEL), lambda i: (i, 0)),
                  _mod_spec(3, tm, rows_per_mod),
                  _mod_spec(4, tm, rows_per_mod),
                  pl.BlockSpec((1, D_MODEL), lambda i: (0, 0)),
                  pl.BlockSpec((D_MODEL, D_IN_PAD), lambda i: (0, 0))],
        out_specs=[pl.BlockSpec((tm, width), lambda i: (i, 0)) for _, width in _IN_SPLITS],
        out_shape=[jax.ShapeDtypeStruct((t, width), F32) for _, width in _IN_SPLITS],
        compiler_params=_cparams("parallel"),
        name="mix_in",
    )(x, mod, mod, norm_w.reshape(1, D_MODEL), w)


def _outproj_kernel(x_ref, g_ref, y0_ref, y1_ref, y2_ref, y3_ref, w_ref, o_ref):
    acc = _bdot(y0_ref[...], w_ref[0:256, :])
    acc += _bdot(y1_ref[...], w_ref[256:512, :])
    acc += _bdot(y2_ref[...], w_ref[512:768, :])
    acc += _bdot(y3_ref[...], w_ref[768:1024, :])
    o_ref[...] = x_ref[...] + g_ref[...] * acc


def _outproj(x, mod, rows_per_mod, ys, w):
    t = x.shape[0]
    tm = min(TOKEN_TILE, rows_per_mod)
    yspec = pl.BlockSpec((tm, 256), lambda i: (i, 0))
    return pl.pallas_call(
        _outproj_kernel,
        grid=(t // tm,),
        in_specs=[pl.BlockSpec((tm, D_MODEL), lambda i: (i, 0)),
                  _mod_spec(5, tm, rows_per_mod),
                  yspec, yspec, yspec, yspec,
                  pl.BlockSpec((D_MIX, D_MODEL), lambda i: (0, 0))],
        out_specs=pl.BlockSpec((tm, D_MODEL), lambda i: (i, 0)),
        out_shape=jax.ShapeDtypeStruct((t, D_MODEL), F32),
        compiler_params=_cparams("parallel"),
        name="mix_out",
    )(x, mod, *ys, w)


def _conv3_chunk(x_ref, c, nc, w, b):
    q = CHUNK
    l = nc * q
    r0 = pl.multiple_of(c * q, q)
    x = x_ref[pl.ds(r0, q), :]
    prev = x_ref[pl.ds(jnp.maximum(r0 - 1, 0), 1), :]
    nxt = x_ref[pl.ds(jnp.minimum(r0 + q, l - 1), 1), :]
    prev = jnp.where(c > 0, prev, 0.0)
    nxt = jnp.where(c < nc - 1, nxt, 0.0)
    rid = lax.broadcasted_iota(jnp.int32, (q, 1), 0)
    xm1 = jnp.where(rid == 0, prev, pltpu.roll(x, 1, 0))
    xp1 = jnp.where(rid == q - 1, nxt, pltpu.roll(x, q - 1, 0))
    return xm1 * w[0:1, :] + x * w[1:2, :] + xp1 * w[2:3, :] + b


def _tri(lower):
    i = lax.broadcasted_iota(jnp.int32, (CHUNK, CHUNK), 0)
    j = lax.broadcasted_iota(jnp.int32, (CHUNK, CHUNK), 1)
    return (j <= i) if lower else (j >= i)


def _rows(c):
    return pl.ds(pl.multiple_of(c * CHUNK, CHUNK), CHUNK)


def _ssd_kernel(nc, has_s0, *refs):
    if has_s0:
        (z_ref, xbc_ref, dt_ref, cw_ref, cb_ref, dtb_ref, alog_ref, dsk_ref, nw_ref, s0_ref,
         y_ref, s_ref, xs_scr, dt_scr, cf_scr, cr_scr, y_scr) = refs
    else:
        (z_ref, xbc_ref, dt_ref, cw_ref, cb_ref, dtb_ref, alog_ref, dsk_ref, nw_ref,
         y_ref, s_ref, xs_scr, dt_scr, cf_scr, cr_scr, y_scr) = refs
    q = CHUNK
    tril = _tri(True)
    triu = _tri(False)
    tril_f = tril.astype(F32)
    triu_f = triu.astype(F32)
    cw = cw_ref[...]
    cb = cb_ref[...]
    neg_a = -jnp.exp(alog_ref[...])
    dtb = dtb_ref[...]
    rep = SSM_HEADS // SSM_GROUPS

    def prep(c, carry):
        rows = _rows(c)
        xs_scr[rows, :] = _silu(_conv3_chunk(xbc_ref, c, nc, cw, cb))
        dt = _softplus(dt_ref[rows, :] + dtb)
        la = dt * neg_a
        dt_scr[rows, :] = dt
        cf_scr[rows, :] = _hdot(tril_f, la)
        cr_scr[rows, :] = _hdot(triu_f, la)
        return carry

    lax.fori_loop(0, nc, prep, 0)

    if has_s0:
        s_ref[...] = s0_ref[...]
    else:
        s_ref[...] = jnp.zeros_like(s_ref)

    def scan_chunk(c, direction):
        rows = _rows(c)
        xs = xs_scr[rows, 0:D_SSM]
        dt = dt_scr[rows, :]
        cum = (cf_scr if direction == 0 else cr_scr)[rows, :]
        dt_t = dt.T
        cum_t = cum.T
        mask = tril if direction == 0 else triu
        edge = q - 1 if direction == 0 else 0
        outs = []
        for g in range(SSM_GROUPS):
            bm = xs_scr[rows, D_SSM + g * SSM_STATE:D_SSM + (g + 1) * SSM_STATE]
            cm = xs_scr[rows, D_SSM + (SSM_GROUPS + g) * SSM_STATE:D_SSM + (SSM_GROUPS + g + 1) * SSM_STATE]
            cb_t = _bdot_nt(cm, bm)
            for h in range(g * rep, (g + 1) * rep):
                lane = direction * SSM_HEADS + h
                col = cum[:, lane:lane + 1]
                row = cum_t[lane:lane + 1, :]
                decay = jnp.exp(jnp.where(mask, col - row, -jnp.inf))
                m = cb_t * decay * dt_t[lane:lane + 1, :]
                xh = xs[:, h * SSM_HEAD_DIM:(h + 1) * SSM_HEAD_DIM]
                s_old = s_ref[direction, h]
                y = _bdot(m, xh) + jnp.exp(col) * _bdot(cm, s_old)
                outs.append(y)
                last = cum[edge:edge + 1, lane:lane + 1]
                wk = bm * (jnp.exp(last - col) * dt[:, lane:lane + 1])
                s_ref[direction, h] = jnp.exp(last) * s_old + _bdot_tn(wk, xh)
        return jnp.concatenate(outs, axis=-1)

    def fwd(c, carry):
        y_scr[_rows(c), :] = scan_chunk(c, 0)
        return carry

    def bwd(i, carry):
        c = nc - 1 - i
        y_scr[_rows(c), :] += scan_chunk(c, 1)
        return carry

    lax.fori_loop(0, nc, fwd, 0)
    lax.fori_loop(0, nc, bwd, 0)

    dsk = dsk_ref[...]
    nw = nw_ref[...]

    def fin(c, carry):
        rows = _rows(c)
        y = y_scr[rows, :] + dsk * xs_scr[rows, 0:D_SSM]
        y_ref[rows, :] = _rms(y * _silu(z_ref[rows, :]), nw)
        return carry

    lax.fori_loop(0, nc, fin, 0)


def _ssd(z, xbc, dt, conv_w, conv_b, dt_bias, a_log, d_skip, norm_w, s0, nb, l):
    nc = l // CHUNK
    has_s0 = s0 is not None
    pad8 = lambda a: jnp.pad(a.reshape(1, 2 * SSM_HEADS), ((0, 0), (0, DT_PAD - 2 * SSM_HEADS)))
    seq = lambda w: pl.BlockSpec((None, l, w), lambda b: (b, 0, 0))
    st_spec = pl.BlockSpec((None, 2, SSM_HEADS, SSM_STATE, SSM_HEAD_DIM), lambda b: (b, 0, 0, 0, 0))
    args = [z.reshape(nb, l, D_SSM), xbc.reshape(nb, l, SSM_CONV_CH), dt.reshape(nb, l, DT_PAD),
            conv_w, conv_b.reshape(1, SSM_CONV_CH), pad8(dt_bias), pad8(a_log),
            jnp.repeat(d_skip, SSM_HEAD_DIM).reshape(1, D_SSM), norm_w.reshape(1, D_SSM)]
    in_specs = [seq(D_SSM), seq(SSM_CONV_CH), seq(DT_PAD), _full((3, SSM_CONV_CH)), _full((1, SSM_CONV_CH)),
                _full((1, DT_PAD)), _full((1, DT_PAD)), _full((1, D_SSM)), _full((1, D_SSM))]
    if has_s0:
        args.append(s0)
        in_specs.append(st_spec)
    y, s = pl.pallas_call(
        functools.partial(_ssd_kernel, nc, has_s0),
        grid=(nb,),
        in_specs=in_specs,
        out_specs=[seq(D_SSM), st_spec],
        out_shape=[jax.ShapeDtypeStruct((nb, l, D_SSM), F32),
                   jax.ShapeDtypeStruct((nb, 2, SSM_HEADS, SSM_STATE, SSM_HEAD_DIM), F32)],
        scratch_shapes=[pltpu.VMEM((l, SSM_CONV_CH), F32), pltpu.VMEM((l, DT_PAD), F32),
                        pltpu.VMEM((l, DT_PAD), F32), pltpu.VMEM((l, DT_PAD), F32),
                        pltpu.VMEM((l, D_SSM), F32)],
        compiler_params=_cparams("parallel"),
        name="ssd",
    )(*args)
    return y.reshape(nb * l, D_SSM), s


def _ret_kernel(nc, has_s0, *refs):
    if has_s0:
        u_ref, dl_ref, gn_ref, s0_ref, y_ref, s_ref, y_scr = refs
    else:
        u_ref, dl_ref, gn_ref, y_ref, s_ref, y_scr = refs
    q = CHUNK
    hd = RET_HEAD_DIM
    dl = dl_ref[...]
    log_g = -_softplus(-dl)
    ii = lax.broadcasted_iota(jnp.int32, (q, q), 0)
    jj = lax.broadcasted_iota(jnp.int32, (q, q), 1)
    dij = (ii - jj).astype(F32)
    ri = lax.broadcasted_iota(jnp.int32, (q, 1), 0).astype(F32)
    dmat, e_in, e_tail, e_all = [], [], [], []
    for h in range(RET_HEADS):
        lf = log_g[:, h:h + 1]
        lb = log_g[:, RET_HEADS + h:RET_HEADS + h + 1]
        d_f = jnp.exp(jnp.where(dij >= 0, dij * lf, -jnp.inf))
        d_b = jnp.exp(jnp.where(dij <= 0, -dij * lb, -jnp.inf))
        dmat.append(d_f + d_b)
        e_in.append((jnp.exp((ri + 1.0) * lf), jnp.exp((q - ri) * lb)))
        e_tail.append((jnp.exp((q - 1.0 - ri) * lf), jnp.exp(ri * lb)))
        e_all.append((jnp.exp(q * lf), jnp.exp(q * lb)))

    if has_s0:
        s_ref[...] = s0_ref[...]
    else:
        s_ref[...] = jnp.zeros_like(s_ref)

    def head_slices(rows, h):
        qh = u_ref[rows, h * hd:(h + 1) * hd]
        kh = u_ref[rows, D_RET + h * hd:D_RET + (h + 1) * hd] * (RET_HEAD_DIM ** -0.5)
        vh = u_ref[rows, 2 * D_RET + h * hd:2 * D_RET + (h + 1) * hd]
        return qh, kh, vh

    def state_step(direction, h, qh, kh, vh):
        s_old = s_ref[direction, h]
        y = e_in[h][direction] * _bdot(qh, s_old)
        s_ref[direction, h] = e_all[h][direction] * s_old + _bdot_tn(kh * e_tail[h][direction], vh)
        return y

    def fwd(c, carry):
        rows = _rows(c)
        outs = []
        for h in range(RET_HEADS):
            qh, kh, vh = head_slices(rows, h)
            y = _bdot(_bdot_nt(qh, kh) * dmat[h], vh)
            outs.append(y + state_step(0, h, qh, kh, vh))
        y_scr[rows, :] = jnp.concatenate(outs, axis=-1)
        return carry

    def bwd(i, carry):
        rows = _rows(nc - 1 - i)
        outs = []
        for h in range(RET_HEADS):
            qh, kh, vh = head_slices(rows, h)
            outs.append(state_step(1, h, qh, kh, vh))
        y_scr[rows, :] += jnp.concatenate(outs, axis=-1)
        return carry

    lax.fori_loop(0, nc, fwd, 0)
    lax.fori_loop(0, nc, bwd, 0)

    gn = gn_ref[...]

    def fin(c, carry):
        rows = _rows(c)
        y = y_scr[rows, :]
        outs = []
        for h in range(RET_HEADS):
            yh = y[:, h * hd:(h + 1) * hd]
            mu = jnp.mean(yh, axis=-1, keepdims=True)
            var = jnp.mean(jnp.square(yh - mu), axis=-1, keepdims=True)
            outs.append((yh - mu) * lax.rsqrt(var + NORM_EPS))
        yn = jnp.concatenate(outs, axis=-1) * gn
        y_ref[rows, :] = yn * _silu(u_ref[rows, 3 * D_RET:4 * D_RET])
        return carry

    lax.fori_loop(0, nc, fin, 0)


def _retention(u, decay_logit, gn_w, s0, nb, l):
    nc = l // CHUNK
    has_s0 = s0 is not None
    st_spec = pl.BlockSpec((None, 2, RET_HEADS, RET_HEAD_DIM, RET_HEAD_DIM), lambda b: (b, 0, 0, 0, 0))
    args = [u.reshape(nb, l, RET_IN),
            jnp.pad(decay_logit.reshape(1, 2 * RET_HEADS), ((0, 0), (0, LANES - 2 * RET_HEADS))),
            gn_w.reshape(1, D_RET)]
    in_specs = [pl.BlockSpec((None, l, RET_IN), lambda b: (b, 0, 0)), _full((1, LANES)), _full((1, D_RET))]
    if has_s0:
        args.append(s0)
        in_specs.append(st_spec)
    y, s = pl.pallas_call(
        functools.partial(_ret_kernel, nc, has_s0),
        grid=(nb,),
        in_specs=in_specs,
        out_specs=[pl.BlockSpec((None, l, D_RET), lambda b: (b, 0, 0)), st_spec],
        out_shape=[jax.ShapeDtypeStruct((nb, l, D_RET), F32),
                   jax.ShapeDtypeStruct((nb, 2, RET_HEADS, RET_HEAD_DIM, RET_HEAD_DIM), F32)],
        scratch_shapes=[pltpu.VMEM((l, D_RET), F32)],
        compiler_params=_cparams("parallel"),
        name="retention",
    )(*args)
    return y.reshape(nb * l, D_RET), s


def _split(x):
    hi = x.astype(BF16)
    return hi, (x - hi.astype(F32)).astype(BF16)


def _dot3(a_hi, a_lo, b_hi, b_lo):
    d = lambda p, q: jnp.dot(p, q, preferred_element_type=F32)
    return d(a_hi, b_hi) + (d(a_lo, b_hi) + d(a_hi, b_lo))


def _dft_tables(l):
    n = 2 * l
    f = jnp.arange(l, dtype=jnp.int32)[:, None]
    s = jnp.arange(l, dtype=jnp.int32)[None, :]
    ang = ((f * s) % n).astype(F32) * (2.0 * math.pi / n)
    re = jnp.cos(ang)
    im = -jnp.sin(ang)
    nyq = jnp.where(jnp.arange(l) % 2 == 0, 1.0, -1.0).astype(F32)[None, :]
    im = jnp.concatenate([nyq, im[1:]], axis=0)
    fwd = jnp.concatenate([re, im], axis=0)
    wgt = jnp.full((n, 1), 2.0, F32).at[0].set(1.0).at[l].set(1.0) / n
    return _split(fwd) + _split((fwd * wgt).T)


def _hy_filter_kernel(l, feats_ref, dec_ref, w1_ref, b1_ref, w2_ref, b2_ref, w3_ref, fr_ref, fh_ref, fl_ref,
                      a_ref, b_ref, d_ref):
    fr = fr_ref[...]
    h = jnp.sin(fr * (_hdot(feats_ref[...], w1_ref[...]) + b1_ref[...]))
    h = jnp.sin(fr * (_hdot(h, w2_ref[...]) + b2_ref[...]))
    h = _hdot(h, w3_ref[...])
    dec = jnp.concatenate([dec_ref[...]] * HY_ORDER, axis=-1)
    row0 = lax.broadcasted_iota(jnp.int32, (l, 1), 0) == 0
    hf = h[:, 0:HY_ORDER * D_HY] * dec
    hb = h[:, HY_ORDER * D_HY:2 * HY_ORDER * D_HY] * dec
    hb = jnp.where(row0, 0.0, hb)
    hs = _split(hf + hb)
    hd = _split(hf - hb)
    re = _dot3(fh_ref[0:l, :], fl_ref[0:l, :], *hs)
    ny = _dot3(fh_ref[l:l + 8, :], fl_ref[l:l + 8, :], *hs)[0:1]
    im = _dot3(fh_ref[l:2 * l, :], fl_ref[l:2 * l, :], *hd)
    for o in range(HY_ORDER):
        cols = slice(o * D_HY, (o + 1) * D_HY)
        a_ref[o] = re[:, cols]
        b_ref[o] = jnp.where(row0, 0.0, im[:, cols])
        d_ref[o] = jnp.where(row0, ny[:, cols], re[:, cols])


def _hy_filter(l, fwd_hi, fwd_lo, w1, b1, w2, b2, w3, freq):
    pos = jnp.arange(l, dtype=F32)
    t = pos / (l - 1)
    bands = jnp.linspace(1e-4, HY_BANDS - 1, HY_BANDS, dtype=F32)
    ang = (2.0 * math.pi / l) * pos[:, None] * bands[None, :]
    feats = jnp.concatenate([t[:, None], jnp.cos(ang), -jnp.sin(ang)], axis=-1)
    feats = jnp.pad(feats, ((0, 0), (0, LANES - HY_EMB)))
    max_decay = math.log(HY_TARGET) / HY_FAST_DECAY
    min_decay = math.log(HY_TARGET) / HY_SLOW_DECAY
    deltas = jnp.abs(jnp.linspace(min_decay, max_decay, D_HY, dtype=F32))
    dec = jnp.exp(-t[:, None] * deltas[None, :])
    w1p = jnp.pad(w1, ((0, LANES - HY_EMB), (0, 0)))
    spec = jax.ShapeDtypeStruct((HY_ORDER, l, D_HY), F32)
    args = [feats, dec, w1p, b1.reshape(1, HY_HIDDEN), w2, b2.reshape(1, HY_HIDDEN), w3,
            freq.reshape(1, HY_HIDDEN), fwd_hi, fwd_lo]
    return pl.pallas_call(
        functools.partial(_hy_filter_kernel, l),
        grid=(1,),
        in_specs=[_full(a.shape) for a in args],
        out_specs=[_full(spec.shape)] * 3,
        out_shape=[spec] * 3,
        compiler_params=_cparams("arbitrary"),
        name="hyena_filter",
    )(*args)


def _hy_kernel(nc, u_ref, cw_ref, cb_ref, fh_ref, fl_ref, gh_ref, gl_ref, a_ref, b_ref, d_ref, bias_ref, y_ref,
               uc_scr, vh_scr, vl_scr, sh_scr, sl_scr, z_scr):
    l = nc * CHUNK
    cw = cw_ref[...]
    cb = cb_ref[...]

    def conv(c, carry):
        rows = _rows(c)
        uc = _conv3_chunk(u_ref, c, nc, cw, cb)
        uc_scr[rows, :] = uc
        vh_scr[rows, :], vl_scr[rows, :] = _split(uc[:, 0:D_HY])
        return carry

    lax.fori_loop(0, nc, conv, 0)

    def long_conv(o, src_ref, src_col, gate_col, dst_ref, split_dst):
        def spectrum(c, carry):
            rows = _rows(c)
            rows_im = pl.ds(pl.multiple_of(l + c * CHUNK, CHUNK), CHUNK)
            vh = vh_scr[...]
            vl = vl_scr[...]
            zr = _dot3(fh_ref[rows, :], fl_ref[rows, :], vh, vl)
            zi = _dot3(fh_ref[rows_im, :], fl_ref[rows_im, :], vh, vl)
            sh_scr[rows, :], sl_scr[rows, :] = _split(zr * a_ref[o, rows, :] - zi * b_ref[o, rows, :])
            sh_scr[rows_im, :], sl_scr[rows_im, :] = _split(zr * b_ref[o, rows, :] + zi * d_ref[o, rows, :])
            return carry

        lax.fori_loop(0, nc, spectrum, 0)

        def inverse(c, carry):
            rows = _rows(c)
            y = _dot3(gh_ref[rows, :], gl_ref[rows, :], sh_scr[...], sl_scr[...])
            v = src_ref[rows, src_col:src_col + D_HY]
            gate = uc_scr[rows, gate_col:gate_col + D_HY]
            out = gate * (y + v * bias_ref[o:o + 1, :])
            dst_ref[rows, :] = out
            if split_dst:
                vh_scr[rows, :], vl_scr[rows, :] = _split(out)
            return carry

        lax.fori_loop(0, nc, inverse, 0)

    long_conv(0, uc_scr, 0, D_HY, z_scr, True)
    long_conv(1, z_scr, 0, 2 * D_HY, y_ref, False)


def _hyena(u, conv_w, conv_b, tables, spectra, d_bias, nb, l):
    nc = l // CHUNK
    a, b, d = spectra
    y = pl.pallas_call(
        functools.partial(_hy_kernel, nc),
        grid=(nb,),
        in_specs=[pl.BlockSpec((None, l, HY_IN), lambda i: (i, 0, 0)),
                  _const((3, HY_IN)), _const((1, HY_IN)),
                  _const((2 * l, l)), _const((2 * l, l)), _const((l, 2 * l)), _const((l, 2 * l)),
                  _const((HY_ORDER, l, D_HY)), _const((HY_ORDER, l, D_HY)), _const((HY_ORDER, l, D_HY)),
                  _const((HY_ORDER, D_HY))],
        out_specs=pl.BlockSpec((None, l, D_HY), lambda i: (i, 0, 0)),
        out_shape=jax.ShapeDtypeStruct((nb, l, D_HY), F32),
        scratch_shapes=[pltpu.VMEM((l, HY_IN), F32), pltpu.VMEM((l, D_HY), BF16), pltpu.VMEM((l, D_HY), BF16),
                        pltpu.VMEM((2 * l, D_HY), BF16), pltpu.VMEM((2 * l, D_HY), BF16),
                        pltpu.VMEM((l, D_HY), F32)],
        compiler_params=_cparams("parallel"),
        name="hyena",
    )(u.reshape(nb, l, HY_IN), conv_w, conv_b.reshape(1, HY_IN), *tables, a, b, d, d_bias)
    return y.reshape(nb * l, D_HY)


def _head_rms(x, w, heads):
    outs = []
    for h in range(heads):
        outs.append(_rms(x[:, h * HEAD_DIM:(h + 1) * HEAD_DIM], w))
    return jnp.concatenate(outs, axis=-1)


def _ctx_attn_kernel(u_ref, qn_ref, kn_ref, sink_ref, y_ref, k_ref, v_ref):
    u = u_ref[...]
    qn = qn_ref[...]
    kn = kn_ref[...]
    q = _head_rms(u[:, 0:D_ATT], qn, ATT_HEADS)
    k = _head_rms(u[:, D_ATT:D_ATT + D_KV], kn, ATT_KV_HEADS)
    v = u[:, D_ATT + D_KV:D_ATT + 2 * D_KV]
    k_ref[...] = k
    v_ref[...] = v
    grp = ATT_HEADS // ATT_KV_HEADS
    scale = HEAD_DIM ** -0.5
    outs = []
    for h in range(ATT_HEADS):
        kv = h // grp
        kh = k[:, kv * HEAD_DIM:(kv + 1) * HEAD_DIM]
        vh = v[:, kv * HEAD_DIM:(kv + 1) * HEAD_DIM]
        s = _bdot_nt(q[:, h * HEAD_DIM:(h + 1) * HEAD_DIM], kh) * scale
        sink = sink_ref[h]
        m = jnp.maximum(jnp.max(s, axis=-1, keepdims=True), sink)
        p = jnp.exp(s - m)
        den = jnp.sum(p, axis=-1, keepdims=True) + jnp.exp(sink - m)
        outs.append(_bdot(p / den, vh))
    y_ref[...] = jnp.concatenate(outs, axis=-1)


def _ctx_attention(u, q_norm, k_norm, sink, nb, l):
    y, k, v = pl.pallas_call(
        _ctx_attn_kernel,
        grid=(nb,),
        in_specs=[pl.BlockSpec((None, l, ATT_IN), lambda b: (b, 0, 0)),
                  _full((1, HEAD_DIM)), _full((1, HEAD_DIM)),
                  pl.BlockSpec(memory_space=pltpu.SMEM)],
        out_specs=[pl.BlockSpec((None, l, D_ATT), lambda b: (b, 0, 0)),
                   pl.BlockSpec((None, l, D_KV), lambda b: (b, 0, 0)),
                   pl.BlockSpec((None, l, D_KV), lambda b: (b, 0, 0))],
        out_shape=[jax.ShapeDtypeStruct((nb, l, D_ATT), F32),
                   jax.ShapeDtypeStruct((nb, l, D_KV), F32),
                   jax.ShapeDtypeStruct((nb, l, D_KV), F32)],
        compiler_params=_cparams("parallel"),
        name="ctx_attention",
    )(u.reshape(nb, l, ATT_IN), q_norm.reshape(1, HEAD_DIM), k_norm.reshape(1, HEAD_DIM), sink)
    return y.reshape(nb * l, D_ATT), k, v


def _rope_tables(l):
    n_rows = l // GRID_W
    rows = jnp.repeat(jnp.arange(n_rows, dtype=F32), GRID_W)
    cols = jnp.tile(jnp.arange(GRID_W, dtype=F32), n_rows)
    nf = HEAD_DIM // 4
    inv = ROPE_BASE ** (-jnp.arange(nf, dtype=F32) / nf)
    ar = rows[:, None] * inv[None, :]
    ac = cols[:, None] * inv[None, :]
    cos = jnp.concatenate([jnp.cos(ar), jnp.cos(ar), jnp.cos(ac), jnp.cos(ac)], axis=-1)
    sin = jnp.concatenate([-jnp.sin(ar), jnp.sin(ar), -jnp.sin(ac), jnp.sin(ac)], axis=-1)
    return cos, sin


def _rope(x, cos, sin):
    w = x.shape[-1]
    nf = HEAD_DIM // 4
    lane = lax.broadcasted_iota(jnp.int32, x.shape, 1)
    first = (lane % (2 * nf)) < nf
    partner = jnp.where(first, pltpu.roll(x, w - nf, 1), pltpu.roll(x, nf, 1))
    return x * cos + partner * sin


def _lat_attn_kernel(nblk, lc, u_ref, qn_ref, kn_ref, cos_ref, sin_ref, ck_ref, cv_ref, sink_ref,
                     y_ref, q_scr, k_scr, v_scr):
    blk = ATT_BLOCK
    l = nblk * blk
    cos_q = cos_ref[...]
    sin_q = sin_ref[...]
    u = u_ref[...]
    q = _head_rms(u[:, 0:D_ATT], qn_ref[...], ATT_HEADS)
    k = _head_rms(u[:, D_ATT:D_ATT + D_KV], kn_ref[...], ATT_KV_HEADS)
    q_scr[...] = _rope(q, cos_q, sin_q)
    zeros = jnp.zeros((blk, D_KV), F32)
    k_scr[0:blk, :] = zeros
    k_scr[blk + l:2 * blk + l, :] = zeros
    v_scr[0:blk, :] = zeros
    v_scr[blk + l:2 * blk + l, :] = zeros
    k_scr[blk:blk + l, :] = _rope(k, cos_q[:, 0:D_KV], sin_q[:, 0:D_KV])
    v_scr[blk:blk + l, :] = u[:, D_ATT + D_KV:D_ATT + 2 * D_KV]

    grp = ATT_HEADS // ATT_KV_HEADS
    scale = HEAD_DIM ** -0.5
    r = lax.broadcasted_iota(jnp.int32, (blk, 3 * blk), 0)
    cidx = lax.broadcasted_iota(jnp.int32, (blk, 3 * blk), 1)
    band = (cidx - r >= blk - WINDOW) & (cidx - r <= blk + WINDOW)

    def block(i, carry):
        rows = _rows(i)
        win = pl.ds(pl.multiple_of(i * blk, blk), 3 * blk)
        kpos = cidx + (i - 1) * blk
        valid = band & (kpos >= 0) & (kpos < l)
        outs = []
        for h in range(ATT_HEADS):
            kv = h // grp
            hs = slice(kv * HEAD_DIM, (kv + 1) * HEAD_DIM)
            qh = q_scr[rows, h * HEAD_DIM:(h + 1) * HEAD_DIM]
            s_loc = _bdot_nt(qh, k_scr[win, hs]) * scale
            s_loc = jnp.where(valid, s_loc, -jnp.inf)
            s_ctx = _bdot_nt(qh, ck_ref[:, hs]) * scale
            sink = sink_ref[h]
            m = jnp.maximum(jnp.maximum(jnp.max(s_loc, axis=-1, keepdims=True),
                                        jnp.max(s_ctx, axis=-1, keepdims=True)), sink)
            p_loc = jnp.exp(s_loc - m)
            p_ctx = jnp.exp(s_ctx - m)
            den = (jnp.sum(p_loc, axis=-1, keepdims=True) + jnp.sum(p_ctx, axis=-1, keepdims=True)
                   + jnp.exp(sink - m))
            inv = 1.0 / den
            outs.append(_bdot(p_ctx * inv, cv_ref[:, hs]) + _bdot(p_loc * inv, v_scr[win, hs]))
        y_ref[rows, :] = jnp.concatenate(outs, axis=-1)
        return carry

    lax.fori_loop(0, nblk, block, 0)


def _lat_attention(u, q_norm, k_norm, sink, ck, cv, nb, l):
    lc = ck.shape[1]
    nblk = l // ATT_BLOCK
    cos, sin = _rope_tables(l)
    cos = jnp.tile(cos, (1, ATT_HEADS))
    sin = jnp.tile(sin, (1, ATT_HEADS))
    y = pl.pallas_call(
        functools.partial(_lat_attn_kernel, nblk, lc),
        grid=(nb,),
        in_specs=[pl.BlockSpec((None, l, ATT_IN), lambda b: (b, 0, 0)),
                  _full((1, HEAD_DIM)), _full((1, HEAD_DIM)),
                  _full((l, D_ATT)), _full((l, D_ATT)),
                  pl.BlockSpec((None, lc, D_KV), lambda b: (b, 0, 0)),
                  pl.BlockSpec((None, lc, D_KV), lambda b: (b, 0, 0)),
                  pl.BlockSpec(memory_space=pltpu.SMEM)],
        out_specs=pl.BlockSpec((None, l, D_ATT), lambda b: (b, 0, 0)),
        out_shape=jax.ShapeDtypeStruct((nb, l, D_ATT), F32),
        scratch_shapes=[pltpu.VMEM((l, D_ATT), F32), pltpu.VMEM((l + 2 * ATT_BLOCK, D_KV), F32),
                        pltpu.VMEM((l + 2 * ATT_BLOCK, D_KV), F32)],
        compiler_params=_cparams("parallel"),
        name="lat_attention",
    )(u.reshape(nb, l, ATT_IN), q_norm.reshape(1, HEAD_DIM), k_norm.reshape(1, HEAD_DIM), cos, sin,
      ck.reshape(nb, lc, D_KV), cv.reshape(nb, lc, D_KV), sink)
    return y.reshape(nb * l, D_ATT)


def _pack_mix_w_in(w):
    c0 = D_SSM + SSM_CONV_CH
    n_dt = 2 * SSM_HEADS
    zeros = jnp.zeros((w.shape[0], DT_PAD - n_dt), w.dtype)
    return jnp.concatenate([w[:, :c0], w[:, c0 + n_dt:], w[:, c0:c0 + n_dt], zeros], axis=1).astype(BF16)


def _layer(x, mod, lp, nb, l, ssd_s0, ret_s0, ctx_kv, hy_tables, hy_spectra):
    rpm = l if mod.shape[0] > 1 else nb * l
    x = _ffn(x, mod, 0, rpm, lp["norm_w"][0], lp["ffn_w_in"][0], lp["ffn_w_out"][0])
    z, xbc, hy, ret, att, dt = _inproj(x, mod, rpm, lp["norm_w"][1], lp["mix_w_in"])
    y_ssd, s_ssd = _ssd(z, xbc, dt, lp["ssd_conv_w"], lp["ssd_conv_b"], lp["ssd_dt_bias"], lp["ssd_a_log"],
                        lp["ssd_d"], lp["ssd_norm_w"], ssd_s0, nb, l)
    y_hy = _hyena(hy, lp["hy_conv_w"], lp["hy_conv_b"], hy_tables, hy_spectra, lp["hy_bias"], nb, l)
    y_ret, s_ret = _retention(ret, lp["ret_decay_logit"], lp["ret_gn_w"], ret_s0, nb, l)
    if ctx_kv is None:
        y_att, k, v = _ctx_attention(att, lp["attn_q_norm"], lp["attn_k_norm"], lp["attn_sink"], nb, l)
    else:
        y_att = _lat_attention(att, lp["attn_q_norm"], lp["attn_k_norm"], lp["attn_sink"],
                               ctx_kv[0], ctx_kv[1], nb, l)
        k = v = None
    x = _outproj(x, mod, rpm, (y_ssd, y_hy, y_ret, y_att), lp["mix_w_out"])
    x = _ffn(x, mod, 6, rpm, lp["norm_w"][2], lp["ffn_w_in"][1], lp["ffn_w_out"][1])
    return x, s_ssd, s_ret, k, v


def kernel(x_prompt, x_sample, cache_k, cache_v, state_ssd, state_ret, c, c_ctx, w_mod, b_mod, norm_w, ffn_w_in, ffn_w_out, mix_w_in, mix_w_out, ssd_conv_w, ssd_conv_b, ssd_dt_bias, ssd_a_log, ssd_d, ssd_norm_w, hy_conv_w, hy_conv_b, hy_w1, hy_b1, hy_w2, hy_b2, hy_w3, hy_freq, hy_bias, ret_decay_logit, ret_gn_w, attn_q_norm, attn_k_norm, attn_sink):
    bp, lp_len, _ = x_prompt.shape
    bs, ls_len, _ = x_sample.shape

    cond = jnp.concatenate([c_ctx[None, :], c, jnp.zeros((8 - 1 - bs, D_MODEL), F32)], axis=0)
    mod = _modulation(cond, w_mod, b_mod)

    tab_p = _dft_tables(lp_len)
    tab_s = _dft_tables(ls_len)

    yp = x_prompt.reshape(bp * lp_len, D_MODEL)
    ys = x_sample.reshape(bs * ls_len, D_MODEL)
    new_k, new_v, new_ssd, new_ret = [], [], [], []
    for li in range(DEPTH):
        lp = dict(norm_w=norm_w[li], ffn_w_in=ffn_w_in[li].astype(BF16), ffn_w_out=ffn_w_out[li].astype(BF16),
                  mix_w_in=_pack_mix_w_in(mix_w_in[li]), mix_w_out=mix_w_out[li].astype(BF16),
                  ssd_conv_w=ssd_conv_w[li], ssd_conv_b=ssd_conv_b[li], ssd_dt_bias=ssd_dt_bias[li],
                  ssd_a_log=ssd_a_log[li], ssd_d=ssd_d[li], ssd_norm_w=ssd_norm_w[li],
                  hy_conv_w=hy_conv_w[li], hy_conv_b=hy_conv_b[li], hy_bias=hy_bias[li],
                  ret_decay_logit=ret_decay_logit[li], ret_gn_w=ret_gn_w[li], attn_q_norm=attn_q_norm[li],
                  attn_k_norm=attn_k_norm[li], attn_sink=attn_sink[li])
        filt = (hy_w1[li], hy_b1[li], hy_w2[li], hy_b2[li], hy_w3[li], hy_freq[li])
        spec_p = _hy_filter(lp_len, tab_p[0], tab_p[1], *filt)
        spec_s = _hy_filter(ls_len, tab_s[0], tab_s[1], *filt)
        mod_p = mod[li, 0:1].reshape(1, 1, N_MOD * D_MODEL)
        mod_s = mod[li, 1:1 + bs].reshape(bs, 1, N_MOD * D_MODEL)

        yp, s_ssd, s_ret, k_ctx, v_ctx = _layer(yp, mod_p, lp, bp, lp_len, None, None, None, tab_p, spec_p)
        new_k.append(k_ctx.reshape(bp, lp_len, ATT_KV_HEADS, HEAD_DIM))
        new_v.append(v_ctx.reshape(bp, lp_len, ATT_KV_HEADS, HEAD_DIM))
        new_ssd.append(s_ssd)
        new_ret.append(s_ret)
        ys = _layer(ys, mod_s, lp, bs, ls_len, state_ssd[:, li], state_ret[:, li],
                    (cache_k[:, li], cache_v[:, li]), tab_s, spec_s)[0]

    return (yp.reshape(bp, lp_len, D_MODEL), ys.reshape(bs, ls_len, D_MODEL),
            jnp.stack(new_k, axis=1), jnp.stack(new_v, axis=1),
            jnp.stack(new_ssd, axis=1), jnp.stack(new_ret, axis=1))
```

```python
import functools
import math

import numpy as np
import jax
import jax.numpy as jnp
from jax import lax
from jax.experimental import pallas as pl
from jax.experimental.pallas import tpu as pltpu

F32 = jnp.float32
BF16 = jnp.bfloat16
HI = lax.Precision.HIGHEST

D_MODEL = 1024
DEPTH = 2
GRID_W = 64
D_FF = 2816
N_MOD = 9
NORM_EPS = 1e-6
CHUNK = 128
D_SSM = 256
SSM_HEADS = 4
SSM_HEAD_DIM = 64
SSM_STATE = 128
SSM_GROUPS = 2
SSM_CONV_CH = D_SSM + 2 * SSM_GROUPS * SSM_STATE
D_HY = 256
HY_ORDER = 2
HY_BANDS = 16
HY_EMB = 1 + 2 * HY_BANDS
HY_HIDDEN = 64
HY_FAST_DECAY = 0.3
HY_SLOW_DECAY = 1.5
HY_TARGET = 1e-2
HY_IN = (HY_ORDER + 1) * D_HY
D_RET = 256
RET_HEADS = 4
RET_HEAD_DIM = 64
RET_IN = 4 * D_RET
ATT_HEADS = 4
ATT_KV_HEADS = 2
HEAD_DIM = 64
D_ATT = ATT_HEADS * HEAD_DIM
D_KV = ATT_KV_HEADS * HEAD_DIM
ATT_IN = D_ATT + 2 * D_KV
WINDOW = 128
ATT_BLOCK = 128
ROPE_BASE = 10000.0
D_MIX = D_SSM + D_HY + D_RET + D_ATT

LANES = 128
DT_PAD = LANES
D_IN_PAD = D_SSM + SSM_CONV_CH + HY_IN + RET_IN + ATT_IN + DT_PAD
VMEM_LIMIT = 56 * 1024 * 1024
MOD_ROWS = 8

TOKEN_TILE = 512


def _cparams(*sem):
    return pltpu.CompilerParams(dimension_semantics=sem, vmem_limit_bytes=VMEM_LIMIT)


def _rms(x, w):
    return x * lax.rsqrt(jnp.mean(x * x, axis=-1, keepdims=True) + NORM_EPS) * w


def _silu(x):
    return x * (1.0 / (1.0 + jnp.exp(-x)))


def _softplus(x):
    return jnp.maximum(x, 0.0) + jnp.log1p(jnp.exp(-jnp.abs(x)))


def _bdot(a, b):
    return jnp.dot(a.astype(BF16), b.astype(BF16), preferred_element_type=F32)


def _bdot_nt(a, b):
    return lax.dot_general(a.astype(BF16), b.astype(BF16), (((1,), (1,)), ((), ())),
                           preferred_element_type=F32)


def _bdot_tn(a, b):
    return lax.dot_general(a.astype(BF16), b.astype(BF16), (((0,), (0,)), ((), ())),
                           preferred_element_type=F32)


def _hdot(a, b):
    return jnp.dot(a, b, preferred_element_type=F32, precision=HI)


def _full(shape):
    n = len(shape)
    return pl.BlockSpec(shape, lambda *_: (0,) * n)


def _const(shape):
    n = len(shape)
    return pl.BlockSpec(shape, lambda *_: (0,) * n, pipeline_mode=pl.Buffered(1))


def _layer_const(arr, li):
    tail = arr.shape[1:]
    zeros = (0,) * len(tail)
    return pl.BlockSpec((None,) + tail, lambda *_: (li,) + zeros, pipeline_mode=pl.Buffered(1))


def _mod_spec(k, tm, rows_per_mod, row0):
    return pl.BlockSpec((None, 1, D_MODEL), lambda i: (row0 + (i * tm) // rows_per_mod, 0, k))


def _seq_spec(l, w):
    return pl.BlockSpec((None, l, w), lambda b: (b, 0, 0))


def _head_masks(width, heads):
    lane = lax.broadcasted_iota(jnp.int32, (1, width), 1)
    hd = width // heads
    return [(lane >= h * hd) & (lane < (h + 1) * hd) for h in range(heads)]


def _by_head(masks, vals):
    out = vals[-1]
    for m, v in zip(masks[-2::-1], vals[-2::-1]):
        out = jnp.where(m, v, out)
    return out


def _block_diag(n, blk, value):
    i = lax.broadcasted_iota(jnp.int32, (n, n), 0) // blk
    j = lax.broadcasted_iota(jnp.int32, (n, n), 1) // blk
    return jnp.where(i == j, value, 0.0).astype(F32)


def _mod_kernel(c_ref, w_ref, b_ref, o_ref):
    c = c_ref[...]
    o_ref[...] = _bdot(_silu(c), w_ref[...]) + b_ref[...]


def _modulation(cond, w_mod, b_mod):
    out = pl.pallas_call(
        _mod_kernel,
        grid=(DEPTH, N_MOD),
        in_specs=[pl.BlockSpec((MOD_ROWS, D_MODEL), lambda l, j: (0, 0)),
                  pl.BlockSpec((None, D_MODEL, D_MODEL), lambda l, j: (l, 0, j)),
                  pl.BlockSpec((None, 1, D_MODEL), lambda l, j: (l, 0, j))],
        out_specs=pl.BlockSpec((None, MOD_ROWS, D_MODEL), lambda l, j: (l, 0, j)),
        out_shape=jax.ShapeDtypeStruct((DEPTH, MOD_ROWS, N_MOD * D_MODEL), F32),
        compiler_params=_cparams("arbitrary", "arbitrary"),
        name="modulation",
    )(cond, w_mod, b_mod.reshape(DEPTH, 1, N_MOD * D_MODEL))
    return out.reshape(DEPTH * MOD_ROWS, 1, N_MOD * D_MODEL)


def _ffn_kernel(x_ref, sh_ref, sc_ref, g_ref, nw_ref, wi_ref, wo_ref, o_ref):
    x = x_ref[...]
    h = (_rms(x, nw_ref[...]) * (1.0 + sc_ref[...]) + sh_ref[...]).astype(BF16)
    gate = jnp.dot(h, wi_ref[:, 0:D_FF], preferred_element_type=F32)
    up = jnp.dot(h, wi_ref[:, D_FF:2 * D_FF], preferred_element_type=F32)
    o_ref[...] = x + 0.5 * g_ref[...] * _bdot(_silu(gate) * up, wo_ref[...])


def _ffn(x, mod, row0, rows_per_mod, wts, li, k):
    t = x.shape[0]
    tm = min(TOKEN_TILE, rows_per_mod)
    w_in, w_out = wts["ffn_w_in"], wts["ffn_w_out"]
    return pl.pallas_call(
        _ffn_kernel,
        grid=(t // tm,),
        in_specs=[pl.BlockSpec((tm, D_MODEL), lambda i: (i, 0)),
                  _mod_spec(6 * k, tm, rows_per_mod, row0),
                  _mod_spec(6 * k + 1, tm, rows_per_mod, row0),
                  _mod_spec(6 * k + 2, tm, rows_per_mod, row0),
                  _layer_const(wts["norm_w"], 3 * li + 2 * k),
                  _layer_const(w_in, 2 * li + k), _layer_const(w_out, 2 * li + k)],
        out_specs=pl.BlockSpec((tm, D_MODEL), lambda i: (i, 0)),
        out_shape=jax.ShapeDtypeStruct((t, D_MODEL), F32),
        compiler_params=_cparams("parallel"),
        name="ffn",
    )(x, mod, mod, mod, wts["norm_w"], w_in, w_out)


_IN_SPLITS = (("z", D_SSM), ("xbc", SSM_CONV_CH), ("hy", HY_IN), ("ret", RET_IN), ("att", ATT_IN), ("dt", DT_PAD))


def _inproj_kernel(x_ref, sh_ref, sc_ref, nw_ref, w_ref, *o_refs):
    h = (_rms(x_ref[...], nw_ref[...]) * (1.0 + sc_ref[...]) + sh_ref[...]).astype(BF16)
    off = 0
    for (_, width), o_ref in zip(_IN_SPLITS, o_refs):
        o_ref[...] = jnp.dot(h, w_ref[:, off:off + width], preferred_element_type=F32)
        off += width


def _inproj(x, mod, row0, rows_per_mod, wts, li):
    t = x.shape[0]
    tm = min(TOKEN_TILE, rows_per_mod)
    return pl.pallas_call(
        _inproj_kernel,
        grid=(t // tm,),
        in_specs=[pl.BlockSpec((tm, D_MODEL), lambda i: (i, 0)),
                  _mod_spec(3, tm, rows_per_mod, row0),
                  _mod_spec(4, tm, rows_per_mod, row0),
                  _layer_const(wts["norm_w"], 3 * li + 1),
                  _layer_const(wts["mix_w_in"], li)],
        out_specs=[pl.BlockSpec((tm, width), lambda i: (i, 0)) for _, width in _IN_SPLITS],
        out_shape=[jax.ShapeDtypeStruct((t, width), F32) for _, width in _IN_SPLITS],
        compiler_params=_cparams("parallel"),
        name="mix_in",
    )(x, mod, mod, wts["norm_w"], wts["mix_w_in"])


def _outproj_kernel(x_ref, g_ref, y0_ref, y1_ref, y2_ref, y3_ref, w_ref, o_ref):
    acc = _bdot(y0_ref[...], w_ref[0:256, :])
    acc += _bdot(y1_ref[...], w_ref[256:512, :])
    acc += _bdot(y2_ref[...], w_ref[512:768, :])
    acc += _bdot(y3_ref[...], w_ref[768:1024, :])
    o_ref[...] = x_ref[...] + g_ref[...] * acc


def _outproj(x, mod, row0, rows_per_mod, ys, wts, li):
    t = x.shape[0]
    tm = min(TOKEN_TILE, rows_per_mod)
    yspec = pl.BlockSpec((tm, 256), lambda i: (i, 0))
    return pl.pallas_call(
        _outproj_kernel,
        grid=(t // tm,),
        in_specs=[pl.BlockSpec((tm, D_MODEL), lambda i: (i, 0)),
                  _mod_spec(5, tm, rows_per_mod, row0),
                  yspec, yspec, yspec, yspec,
                  _layer_const(wts["mix_w_out"], li)],
        out_specs=pl.BlockSpec((tm, D_MODEL), lambda i: (i, 0)),
        out_shape=jax.ShapeDtypeStruct((t, D_MODEL), F32),
        compiler_params=_cparams("parallel"),
        name="mix_out",
    )(x, mod, *ys, wts["mix_w_out"])


def _conv3_chunk(x_ref, c, nc, w, b):
    q = CHUNK
    l = nc * q
    r0 = pl.multiple_of(c * q, q)
    x = x_ref[pl.ds(r0, q), :]
    prev = x_ref[pl.ds(jnp.maximum(r0 - 1, 0), 1), :]
    nxt = x_ref[pl.ds(jnp.minimum(r0 + q, l - 1), 1), :]
    prev = jnp.where(c > 0, prev, 0.0)
    nxt = jnp.where(c < nc - 1, nxt, 0.0)
    rid = lax.broadcasted_iota(jnp.int32, (q, 1), 0)
    xm1 = jnp.where(rid == 0, prev, pltpu.roll(x, 1, 0))
    xp1 = jnp.where(rid == q - 1, nxt, pltpu.roll(x, q - 1, 0))
    return xm1 * w[0:1, :] + x * w[1:2, :] + xp1 * w[2:3, :] + b


def _tri(lower):
    i = lax.broadcasted_iota(jnp.int32, (CHUNK, CHUNK), 0)
    j = lax.broadcasted_iota(jnp.int32, (CHUNK, CHUNK), 1)
    return (j <= i) if lower else (j >= i)


def _rows(c):
    return pl.ds(pl.multiple_of(c * CHUNK, CHUNK), CHUNK)


def _stack_heads(x, masks):
    return jnp.concatenate([jnp.where(m, x, 0.0) for m in masks], axis=0)


def _unstack_heads(y, masks):
    q = y.shape[0] // len(masks)
    out = jnp.where(masks[0], y[0:q], 0.0)
    for h in range(1, len(masks)):
        out = out + jnp.where(masks[h], y[h * q:(h + 1) * q], 0.0)
    return out


def _state_out_spec(heads, n, p):
    return pl.BlockSpec((None, None, 2, heads, n, p), lambda b: (b, 0, 0, 0, 0, 0))


def _ssd_kernel(nc, has_s0, *refs):
    refs = list(refs)
    (z_ref, xbc_ref, dt_ref, cw_ref, cb_ref, dtb_ref, alog_ref, dsk_ref, nw_ref) = refs[:9]
    s0_ref = refs[9] if has_s0 else None
    y_ref, s_ref, xs_scr, dt_scr, cf_scr, cr_scr, yf_scr, yb_scr, st_scr, msk_scr = refs[-10:]
    q = CHUNK
    rep = SSM_HEADS // SSM_GROUPS
    tril = _tri(True)
    triu = _tri(False)
    tril_f = tril.astype(F32)
    triu_f = triu.astype(F32)
    hm = _head_masks(D_SSM, SSM_HEADS)
    cw = cw_ref[...]
    cb = cb_ref[...]
    neg_a = -jnp.exp(alog_ref[...])
    dtb = dtb_ref[...]

    @pl.when(pl.program_id(0) == 0)
    def _():
        i = lax.broadcasted_iota(jnp.int32, (SSM_HEADS * SSM_STATE, D_SSM), 0) // SSM_STATE
        j = lax.broadcasted_iota(jnp.int32, (SSM_HEADS * SSM_STATE, D_SSM), 1) // SSM_HEAD_DIM
        msk_scr[...] = jnp.where(i == j, 1.0, 0.0).astype(F32)

    def prep(c, carry):
        rows = _rows(c)
        xs_scr[rows, :] = _silu(_conv3_chunk(xbc_ref, c, nc, cw, cb))
        dt = _softplus(dt_ref[rows, :] + dtb)
        la = dt * neg_a
        dt_scr[rows, :] = dt
        cf_scr[rows, :] = _hdot(tril_f, la)
        cr_scr[rows, :] = _hdot(triu_f, la)
        return carry

    lax.fori_loop(0, nc, prep, 0)

    for d in range(2):
        for h in range(SSM_HEADS):
            blk = slice(h * SSM_STATE, (h + 1) * SSM_STATE)
            if has_s0:
                parts = [jnp.zeros((SSM_STATE, SSM_HEAD_DIM), F32)] * SSM_HEADS
                parts[h] = s0_ref[d, h]
                st_scr[d, blk, :] = jnp.concatenate(parts, axis=1)
            else:
                st_scr[d, blk, :] = jnp.zeros((SSM_STATE, D_SSM), F32)

    def scan_chunk(c, d):
        rows = _rows(c)
        xs = xs_scr[rows, 0:D_SSM]
        bm = [xs_scr[rows, D_SSM + g * SSM_STATE:D_SSM + (g + 1) * SSM_STATE] for g in range(SSM_GROUPS)]
        cm = [xs_scr[rows, D_SSM + (SSM_GROUPS + g) * SSM_STATE:D_SSM + (SSM_GROUPS + g + 1) * SSM_STATE]
              for g in range(SSM_GROUPS)]
        dt = dt_scr[rows, :]
        cum = (cf_scr if d == 0 else cr_scr)[rows, :]
        dt_t = dt.T
        cum_t = cum.T
        mask = tril if d == 0 else triu
        edge = q - 1 if d == 0 else 0
        cb_t = [_bdot_nt(cm[g], bm[g]) for g in range(SSM_GROUPS)]
        blocks, wparts, ecols, lasts = [], [], [], []
        for h in range(SSM_HEADS):
            g = h // rep
            lane = d * SSM_HEADS + h
            col = cum[:, lane:lane + 1]
            row = cum_t[lane:lane + 1, :]
            decay = jnp.exp(jnp.where(mask, col - row, -jnp.inf))
            blocks.append(cb_t[g] * decay * dt_t[lane:lane + 1, :])
            last = cum[edge:edge + 1, lane:lane + 1]
            wparts.append(bm[g] * (jnp.exp(last - col) * dt[:, lane:lane + 1]))
            ecols.append(jnp.exp(col))
            lasts.append(jnp.exp(last))
        y = _unstack_heads(_bdot(jnp.concatenate(blocks, axis=0), xs), hm)
        s_old = st_scr[d]
        c4 = jnp.concatenate([cm[h // rep] for h in range(SSM_HEADS)], axis=1)
        y = y + _by_head(hm, ecols) * _bdot(c4, s_old)
        wk = jnp.concatenate(wparts, axis=1)
        st_scr[d] = _by_head(hm, lasts) * s_old + _bdot_tn(wk, xs) * msk_scr[...]
        return y

    def scan(i, carry):
        yf_scr[_rows(i), :] = scan_chunk(i, 0)
        yb_scr[_rows(nc - 1 - i), :] = scan_chunk(nc - 1 - i, 1)
        return carry

    lax.fori_loop(0, nc, scan, 0)

    for d in range(2):
        for h in range(SSM_HEADS):
            s_ref[d, h] = st_scr[d, h * SSM_STATE:(h + 1) * SSM_STATE, h * SSM_HEAD_DIM:(h + 1) * SSM_HEAD_DIM]

    dsk = dsk_ref[...]
    nw = nw_ref[...]

    def fin(c, carry):
        rows = _rows(c)
        y = yf_scr[rows, :] + yb_scr[rows, :] + dsk * xs_scr[rows, 0:D_SSM]
        y_ref[rows, :] = _rms(y * _silu(z_ref[rows, :]), nw)
        return carry

    lax.fori_loop(0, nc, fin, 0)


def _ssd(z, xbc, dt, wts, li, s0, nb, l):
    nc = l // CHUNK
    has_s0 = s0 is not None
    names = ("ssd_conv_w", "ssd_conv_b", "ssd_dt_bias", "ssd_a_log", "ssd_d", "ssd_norm_w")
    args = [z.reshape(nb, l, D_SSM), xbc.reshape(nb, l, SSM_CONV_CH), dt.reshape(nb, l, DT_PAD)]
    args += [wts[n] for n in names]
    in_specs = [_seq_spec(l, D_SSM), _seq_spec(l, SSM_CONV_CH), _seq_spec(l, DT_PAD)]
    in_specs += [_layer_const(wts[n], li) for n in names]
    if has_s0:
        args.append(s0)
        in_specs.append(pl.BlockSpec((None, None, 2, SSM_HEADS, SSM_STATE, SSM_HEAD_DIM),
                                     lambda b: (b, li, 0, 0, 0, 0)))
    y, s = pl.pallas_call(
        functools.partial(_ssd_kernel, nc, has_s0),
        grid=(nb,),
        in_specs=in_specs,
        out_specs=[_seq_spec(l, D_SSM), _state_out_spec(SSM_HEADS, SSM_STATE, SSM_HEAD_DIM)],
        out_shape=[jax.ShapeDtypeStruct((nb, l, D_SSM), F32),
                   jax.ShapeDtypeStruct((nb, 1, 2, SSM_HEADS, SSM_STATE, SSM_HEAD_DIM), F32)],
        scratch_shapes=[pltpu.VMEM((l, SSM_CONV_CH), F32), pltpu.VMEM((l, DT_PAD), F32),
                        pltpu.VMEM((l, DT_PAD), F32), pltpu.VMEM((l, DT_PAD), F32),
                        pltpu.VMEM((l, D_SSM), F32), pltpu.VMEM((l, D_SSM), F32),
                        pltpu.VMEM((2, SSM_HEADS * SSM_STATE, D_SSM), F32),
                        pltpu.VMEM((SSM_HEADS * SSM_STATE, D_SSM), F32)],
        compiler_params=_cparams("arbitrary"),
        name="ssd",
    )(*args)
    return y.reshape(nb * l, D_SSM), s


def _ret_kernel(nc, has_s0, *refs):
    refs = list(refs)
    u_ref, dl_ref, gn_ref = refs[:3]
    s0_ref = refs[3] if has_s0 else None
    y_ref, s_ref, yf_scr, yb_scr, st_scr, dm_scr, e_scr, ea_scr, bd_scr = refs[-9:]
    q = CHUNK
    hd = RET_HEAD_DIM
    hm = _head_masks(D_RET, RET_HEADS)

    @pl.when(pl.program_id(0) == 0)
    def _():
        log_g = -_softplus(-dl_ref[...])
        ii = lax.broadcasted_iota(jnp.int32, (q, q), 0)
        jj = lax.broadcasted_iota(jnp.int32, (q, q), 1)
        dij = (ii - jj).astype(F32)
        ri = lax.broadcasted_iota(jnp.int32, (q, 1), 0).astype(F32)
        lfs, lbs = [], []
        for h in range(RET_HEADS):
            lf = log_g[:, h:h + 1]
            lb = log_g[:, RET_HEADS + h:RET_HEADS + h + 1]
            lfs.append(lf)
            lbs.append(lb)
            d_f = jnp.exp(jnp.where(dij >= 0, dij * lf, -jnp.inf))
            d_b = jnp.exp(jnp.where(dij <= 0, -dij * lb, -jnp.inf))
            dm_scr[h * q:(h + 1) * q, :] = d_f + d_b
        lf_l = _by_head(hm, lfs)
        lb_l = _by_head(hm, lbs)
        e_scr[0] = jnp.exp((ri + 1.0) * lf_l)
        e_scr[1] = jnp.exp((q - ri) * lb_l)
        e_scr[2] = jnp.exp((q - 1.0 - ri) * lf_l)
        e_scr[3] = jnp.exp(ri * lb_l)
        ea_scr[0:1, :] = jnp.exp(q * lf_l)
        ea_scr[1:2, :] = jnp.exp(q * lb_l)
        bd_scr[...] = _block_diag(D_RET, hd, 1.0)

    for d in range(2):
        for h in range(RET_HEADS):
            blk = slice(h * hd, (h + 1) * hd)
            if has_s0:
                parts = [jnp.zeros((hd, hd), F32)] * RET_HEADS
                parts[h] = s0_ref[d, h]
                st_scr[d, blk, :] = jnp.concatenate(parts, axis=1)
            else:
                st_scr[d, blk, :] = jnp.zeros((hd, D_RET), F32)

    def qkv(rows):
        return (u_ref[rows, 0:D_RET], u_ref[rows, D_RET:2 * D_RET] * (RET_HEAD_DIM ** -0.5),
                u_ref[rows, 2 * D_RET:3 * D_RET])

    def state_step(d, qq, kk, vv):
        s_old = st_scr[d]
        y = e_scr[d] * _bdot(qq, s_old)
        st_scr[d] = ea_scr[d:d + 1, :] * s_old + _bdot_tn(kk * e_scr[2 + d], vv) * bd_scr[...]
        return y

    def scan(i, carry):
        rows = _rows(i)
        qq, kk, vv = qkv(rows)
        sc = _bdot_nt(_stack_heads(qq, hm), kk) * dm_scr[...]
        yf_scr[rows, :] = _unstack_heads(_bdot(sc, vv), hm) + state_step(0, qq, kk, vv)
        rows_b = _rows(nc - 1 - i)
        qq, kk, vv = qkv(rows_b)
        yb_scr[rows_b, :] = state_step(1, qq, kk, vv)
        return carry

    lax.fori_loop(0, nc, scan, 0)

    for d in range(2):
        for h in range(RET_HEADS):
            s_ref[d, h] = st_scr[d, h * hd:(h + 1) * hd, h * hd:(h + 1) * hd]

    gn = gn_ref[...]

    def fin(c, carry):
        rows = _rows(c)
        y = yf_scr[rows, :] + yb_scr[rows, :]
        avg = bd_scr[...] * (1.0 / hd)
        cen = y - _hdot(y, avg)
        var = _hdot(cen * cen, avg)
        y_ref[rows, :] = cen * lax.rsqrt(var + NORM_EPS) * gn * _silu(u_ref[rows, 3 * D_RET:4 * D_RET])
        return carry

    lax.fori_loop(0, nc, fin, 0)


def _retention(u, wts, li, s0, nb, l):
    nc = l // CHUNK
    has_s0 = s0 is not None
    names = ("ret_decay_logit", "ret_gn_w")
    args = [u.reshape(nb, l, RET_IN)] + [wts[n] for n in names]
    in_specs = [_seq_spec(l, RET_IN)] + [_layer_const(wts[n], li) for n in names]
    if has_s0:
        args.append(s0)
        in_specs.append(pl.BlockSpec((None, None, 2, RET_HEADS, RET_HEAD_DIM, RET_HEAD_DIM),
                                     lambda b: (b, li, 0, 0, 0, 0)))
    y, s = pl.pallas_call(
        functools.partial(_ret_kernel, nc, has_s0),
        grid=(nb,),
        in_specs=in_specs,
        out_specs=[_seq_spec(l, D_RET), _state_out_spec(RET_HEADS, RET_HEAD_DIM, RET_HEAD_DIM)],
        out_shape=[jax.ShapeDtypeStruct((nb, l, D_RET), F32),
                   jax.ShapeDtypeStruct((nb, 1, 2, RET_HEADS, RET_HEAD_DIM, RET_HEAD_DIM), F32)],
        scratch_shapes=[pltpu.VMEM((l, D_RET), F32), pltpu.VMEM((l, D_RET), F32),
                        pltpu.VMEM((2, D_RET, D_RET), F32), pltpu.VMEM((RET_HEADS * CHUNK, CHUNK), F32),
                        pltpu.VMEM((4, CHUNK, D_RET), F32), pltpu.VMEM((8, D_RET), F32),
                        pltpu.VMEM((D_RET, D_RET), F32)],
        compiler_params=_cparams("arbitrary"),
        name="retention",
    )(*args)
    return y.reshape(nb * l, D_RET), s


def _split(x):
    hi = x.astype(BF16)
    return hi, (x - hi.astype(F32)).astype(BF16)


def _dot3(a_hi, a_lo, b_hi, b_lo):
    d = lambda p, q: jnp.dot(p, q, preferred_element_type=F32)
    return d(a_hi, b_hi) + (d(a_lo, b_hi) + d(a_hi, b_lo))


@functools.lru_cache(maxsize=None)
def _dft_fwd_host(l):
    n = 2 * l
    f = np.arange(l, dtype=np.int64)[:, None]
    s = np.arange(l, dtype=np.int64)[None, :]
    ang = ((f * s) % n).astype(np.float64) * (2.0 * math.pi / n)
    im = -np.sin(ang)
    im[0] = np.where(np.arange(l) % 2 == 0, 1.0, -1.0)
    return np.concatenate([np.cos(ang), im], axis=0).astype(np.float32)


def _dft_tables(l):
    n = 2 * l
    fwd = jnp.asarray(_dft_fwd_host(l))
    wgt = np.full((n, 1), 2.0 / n, np.float32)
    wgt[0] = wgt[l] = 1.0 / n
    return _split(fwd) + _split((fwd * wgt).T)


def _hy_filter_kernel(l, feats_ref, dec_ref, w1_ref, b1_ref, w2_ref, b2_ref, w3_ref, fr_ref, fh_ref, fl_ref,
                      a_ref, b_ref, d_ref):
    fr = fr_ref[...]
    h = jnp.sin(fr * (_hdot(feats_ref[...], w1_ref[...]) + b1_ref[...]))
    h = jnp.sin(fr * (_hdot(h, w2_ref[...]) + b2_ref[...]))
    h = _hdot(h, w3_ref[...])
    dec = jnp.concatenate([dec_ref[...]] * HY_ORDER, axis=-1)
    row0 = lax.broadcasted_iota(jnp.int32, (l, 1), 0) == 0
    hf = h[:, 0:HY_ORDER * D_HY] * dec
    hb = h[:, HY_ORDER * D_HY:2 * HY_ORDER * D_HY] * dec
    hb = jnp.where(row0, 0.0, hb)
    hs = _split(hf + hb)
    hd = _split(hf - hb)
    re = _dot3(fh_ref[0:l, :], fl_ref[0:l, :], *hs)
    ny = _dot3(fh_ref[l:l + 8, :], fl_ref[l:l + 8, :], *hs)[0:1]
    im = _dot3(fh_ref[l:2 * l, :], fl_ref[l:2 * l, :], *hd)
    for o in range(HY_ORDER):
        cols = slice(o * D_HY, (o + 1) * D_HY)
        a_ref[o] = re[:, cols]
        b_ref[o] = jnp.where(row0, 0.0, im[:, cols])
        d_ref[o] = jnp.where(row0, ny[:, cols], re[:, cols])


def _hy_filter(l, fwd_hi, fwd_lo, wts, li):
    pos = np.arange(l, dtype=np.float32)
    t = pos / np.float32(l - 1)
    bands = np.linspace(1e-4, HY_BANDS - 1, HY_BANDS, dtype=np.float32)
    ang = np.float32(2.0 * math.pi / l) * pos[:, None] * bands[None, :]
    feats = np.concatenate([t[:, None], np.cos(ang), -np.sin(ang)], axis=-1).astype(np.float32)
    feats = np.pad(feats, ((0, 0), (0, LANES - HY_EMB)))
    max_decay = math.log(HY_TARGET) / HY_FAST_DECAY
    min_decay = math.log(HY_TARGET) / HY_SLOW_DECAY
    deltas = np.abs(np.linspace(min_decay, max_decay, D_HY, dtype=np.float32))
    dec = np.exp(-t[:, None] * deltas[None, :]).astype(np.float32)
    spec = jax.ShapeDtypeStruct((HY_ORDER, l, D_HY), F32)
    names = ("hy_w1", "hy_b1", "hy_w2", "hy_b2", "hy_w3", "hy_freq")
    return pl.pallas_call(
        functools.partial(_hy_filter_kernel, l),
        grid=(1,),
        in_specs=[_full(feats.shape), _full(dec.shape)] + [_layer_const(wts[n], li) for n in names]
                 + [_full(fwd_hi.shape), _full(fwd_lo.shape)],
        out_specs=[_full(spec.shape)] * 3,
        out_shape=[spec] * 3,
        compiler_params=_cparams("arbitrary"),
        name="hyena_filter",
    )(jnp.asarray(feats), jnp.asarray(dec), *[wts[n] for n in names], fwd_hi, fwd_lo)


def _hy_kernel(nc, u_ref, cw_ref, cb_ref, fh_ref, fl_ref, gh_ref, gl_ref, a_ref, b_ref, d_ref, bias_ref, y_ref,
               uc_scr, vh_scr, vl_scr, sh_scr, sl_scr, z_scr):
    l = nc * CHUNK
    cw = cw_ref[...]
    cb = cb_ref[...]

    def conv(c, carry):
        rows = _rows(c)
        uc = _conv3_chunk(u_ref, c, nc, cw, cb)
        uc_scr[rows, :] = uc
        vh_scr[rows, :], vl_scr[rows, :] = _split(uc[:, 0:D_HY])
        return carry

    lax.fori_loop(0, nc, conv, 0)

    def long_conv(o, src_ref, src_col, gate_col, dst_ref, split_dst):
        def spectrum(c, carry):
            rows = _rows(c)
            rows_im = pl.ds(pl.multiple_of(l + c * CHUNK, CHUNK), CHUNK)
            vh = vh_scr[...]
            vl = vl_scr[...]
            zr = _dot3(fh_ref[rows, :], fl_ref[rows, :], vh, vl)
            zi = _dot3(fh_ref[rows_im, :], fl_ref[rows_im, :], vh, vl)
            sh_scr[rows, :], sl_scr[rows, :] = _split(zr * a_ref[o, rows, :] - zi * b_ref[o, rows, :])
            sh_scr[rows_im, :], sl_scr[rows_im, :] = _split(zr * b_ref[o, rows, :] + zi * d_ref[o, rows, :])
            return carry

        lax.fori_loop(0, nc, spectrum, 0)

        def inverse(c, carry):
            rows = _rows(c)
            y = _dot3(gh_ref[rows, :], gl_ref[rows, :], sh_scr[...], sl_scr[...])
            v = src_ref[rows, src_col:src_col + D_HY]
            gate = uc_scr[rows, gate_col:gate_col + D_HY]
            out = gate * (y + v * bias_ref[o:o + 1, :])
            dst_ref[rows, :] = out
            if split_dst:
                vh_scr[rows, :], vl_scr[rows, :] = _split(out)
            return carry

        lax.fori_loop(0, nc, inverse, 0)

    long_conv(0, uc_scr, 0, D_HY, z_scr, True)
    long_conv(1, z_scr, 0, 2 * D_HY, y_ref, False)


def _hyena(u, wts, li, tables, spectra, nb, l):
    nc = l // CHUNK
    a, b, d = spectra
    names = ("hy_conv_w", "hy_conv_b")
    y = pl.pallas_call(
        functools.partial(_hy_kernel, nc),
        grid=(nb,),
        in_specs=[_seq_spec(l, HY_IN)] + [_layer_const(wts[n], li) for n in names]
                 + [_const((2 * l, l)), _const((2 * l, l)), _const((l, 2 * l)), _const((l, 2 * l)),
                    _const((HY_ORDER, l, D_HY)), _const((HY_ORDER, l, D_HY)), _const((HY_ORDER, l, D_HY)),
                    _layer_const(wts["hy_bias"], li)],
        out_specs=_seq_spec(l, D_HY),
        out_shape=jax.ShapeDtypeStruct((nb, l, D_HY), F32),
        scratch_shapes=[pltpu.VMEM((l, HY_IN), F32), pltpu.VMEM((l, D_HY), BF16), pltpu.VMEM((l, D_HY), BF16),
                        pltpu.VMEM((2 * l, D_HY), BF16), pltpu.VMEM((2 * l, D_HY), BF16),
                        pltpu.VMEM((l, D_HY), F32)],
        compiler_params=_cparams("parallel"),
        name="hyena",
    )(u.reshape(nb, l, HY_IN), *[wts[n] for n in names], *tables, a, b, d, wts["hy_bias"])
    return y.reshape(nb * l, D_HY)


def _seg_rms(x, w):
    avg = _block_diag(x.shape[-1], HEAD_DIM, 1.0 / HEAD_DIM)
    return x * lax.rsqrt(_hdot(x * x, avg) + NORM_EPS) * w


def _stack_q(q):
    lo = lax.broadcasted_iota(jnp.int32, (1, D_KV), 1) < HEAD_DIM
    qa = q[:, 0:D_KV]
    qb = q[:, D_KV:2 * D_KV]
    return jnp.concatenate([jnp.where(lo, qa, 0.0), jnp.where(lo, pltpu.roll(qa, HEAD_DIM, 1), 0.0),
                            jnp.where(lo, 0.0, pltpu.roll(qb, HEAD_DIM, 1)), jnp.where(lo, 0.0, qb)], axis=0)


def _unstack_o(o):
    r = o.shape[0] // ATT_HEADS
    lo = lax.broadcasted_iota(jnp.int32, (1, D_KV), 1) < HEAD_DIM
    ya = jnp.where(lo, o[0:r], pltpu.roll(o[r:2 * r], HEAD_DIM, 1))
    yb = jnp.where(lo, pltpu.roll(o[2 * r:3 * r], HEAD_DIM, 1), o[3 * r:4 * r])
    return jnp.concatenate([ya, yb], axis=1)


def _sink_col(sink_ref, li, r):
    rb = lax.broadcasted_iota(jnp.int32, (ATT_HEADS * r, 1), 0) // r
    col = jnp.full((ATT_HEADS * r, 1), sink_ref[li * ATT_HEADS + ATT_HEADS - 1], F32)
    for h in range(ATT_HEADS - 2, -1, -1):
        col = jnp.where(rb == h, sink_ref[li * ATT_HEADS + h], col)
    return col


def _ctx_attn_kernel(li, u_ref, qn_ref, kn_ref, sink_ref, y_ref, k_ref, v_ref):
    u = u_ref[...]
    l = u.shape[0]
    q = _seg_rms(u[:, 0:D_ATT], qn_ref[...])
    k = _seg_rms(u[:, D_ATT:D_ATT + D_KV], kn_ref[:, 0:D_KV])
    v = u[:, D_ATT + D_KV:D_ATT + 2 * D_KV]
    k_ref[...] = k
    v_ref[...] = v
    s = _bdot_nt(_stack_q(q), k) * (HEAD_DIM ** -0.5)
    sink = _sink_col(sink_ref, li, l)
    m = jnp.maximum(jnp.max(s, axis=-1, keepdims=True), sink)
    p = jnp.exp(s - m)
    den = jnp.sum(p, axis=-1, keepdims=True) + jnp.exp(sink - m)
    y_ref[...] = _unstack_o(_bdot(p * (1.0 / den), v))


def _ctx_attention(u, wts, li, nb, l):
    kv_spec = pl.BlockSpec((None, None, l, D_KV), lambda b: (b, 0, 0, 0))
    y, k, v = pl.pallas_call(
        functools.partial(_ctx_attn_kernel, li),
        grid=(nb,),
        in_specs=[_seq_spec(l, ATT_IN), _layer_const(wts["attn_q_norm"], li), _layer_const(wts["attn_k_norm"], li),
                  pl.BlockSpec(memory_space=pltpu.SMEM)],
        out_specs=[_seq_spec(l, D_ATT), kv_spec, kv_spec],
        out_shape=[jax.ShapeDtypeStruct((nb, l, D_ATT), F32),
                   jax.ShapeDtypeStruct((nb, 1, l, D_KV), F32),
                   jax.ShapeDtypeStruct((nb, 1, l, D_KV), F32)],
        compiler_params=_cparams("parallel"),
        name="ctx_attention",
    )(u.reshape(nb, l, ATT_IN), wts["attn_q_norm"], wts["attn_k_norm"], wts["attn_sink"])
    return y.reshape(nb * l, D_ATT), k, v


@functools.lru_cache(maxsize=None)
def _rope_tables_host(l):
    n_rows = l // GRID_W
    rows = np.repeat(np.arange(n_rows, dtype=np.float32), GRID_W)
    cols = np.tile(np.arange(GRID_W, dtype=np.float32), n_rows)
    nf = HEAD_DIM // 4
    inv = (np.float32(ROPE_BASE) ** (-np.arange(nf, dtype=np.float32) / np.float32(nf))).astype(np.float32)
    ar = rows[:, None] * inv[None, :]
    ac = cols[:, None] * inv[None, :]
    cos = np.concatenate([np.cos(ar), np.cos(ar), np.cos(ac), np.cos(ac)], axis=-1)
    sin = np.concatenate([-np.sin(ar), np.sin(ar), -np.sin(ac), np.sin(ac)], axis=-1)
    return (np.tile(cos, (1, ATT_HEADS)).astype(np.float32), np.tile(sin, (1, ATT_HEADS)).astype(np.float32))


def _rope(x, cos, sin):
    w = x.shape[-1]
    nf = HEAD_DIM // 4
    lane = lax.broadcasted_iota(jnp.int32, x.shape, 1)
    first = (lane % (2 * nf)) < nf
    partner = jnp.where(first, pltpu.roll(x, w - nf, 1), pltpu.roll(x, nf, 1))
    return x * cos + partner * sin


def _lat_attn_kernel(nblk, li, u_ref, qn_ref, kn_ref, cos_ref, sin_ref, ck_ref, cv_ref, sink_ref,
                     y_ref, q_scr, k_scr, v_scr):
    blk = ATT_BLOCK
    l = nblk * blk
    cos_q = cos_ref[...]
    sin_q = sin_ref[...]
    u = u_ref[...]
    q = _seg_rms(u[:, 0:D_ATT], qn_ref[...])
    k = _seg_rms(u[:, D_ATT:D_ATT + D_KV], kn_ref[:, 0:D_KV])
    q_scr[...] = _rope(q, cos_q, sin_q)
    zeros = jnp.zeros((blk, D_KV), F32)
    k_scr[0:blk, :] = zeros
    k_scr[blk + l:2 * blk + l, :] = zeros
    v_scr[0:blk, :] = zeros
    v_scr[blk + l:2 * blk + l, :] = zeros
    k_scr[blk:blk + l, :] = _rope(k, cos_q[:, 0:D_KV], sin_q[:, 0:D_KV])
    v_scr[blk:blk + l, :] = u[:, D_ATT + D_KV:D_ATT + 2 * D_KV]

    scale = HEAD_DIM ** -0.5
    r = lax.broadcasted_iota(jnp.int32, (blk, 3 * blk), 0)
    cidx = lax.broadcasted_iota(jnp.int32, (blk, 3 * blk), 1)
    band = (cidx - r >= blk - WINDOW) & (cidx - r <= blk + WINDOW)
    sink = _sink_col(sink_ref, li, blk)

    def block(i, carry):
        rows = _rows(i)
        win = pl.ds(pl.multiple_of(i * blk, blk), 3 * blk)
        kpos = cidx + (i - 1) * blk
        valid = band & (kpos >= 0) & (kpos < l)
        valid = jnp.concatenate([valid] * ATT_HEADS, axis=0)
        qs = _stack_q(q_scr[rows, :])
        s_loc = jnp.where(valid, _bdot_nt(qs, k_scr[win, :]) * scale, -jnp.inf)
        s_ctx = _bdot_nt(qs, ck_ref[...]) * scale
        m = jnp.maximum(jnp.maximum(jnp.max(s_loc, axis=-1, keepdims=True),
                                    jnp.max(s_ctx, axis=-1, keepdims=True)), sink)
        p_loc = jnp.exp(s_loc - m)
        p_ctx = jnp.exp(s_ctx - m)
        den = (jnp.sum(p_loc, axis=-1, keepdims=True) + jnp.sum(p_ctx, axis=-1, keepdims=True)
               + jnp.exp(sink - m))
        inv = 1.0 / den
        y_ref[rows, :] = _unstack_o(_bdot(p_ctx * inv, cv_ref[...]) + _bdot(p_loc * inv, v_scr[win, :]))
        return carry

    lax.fori_loop(0, nblk, block, 0)


def _lat_attention(u, wts, li, ck, cv, nb, l):
    lc = ck.shape[2]
    nblk = l // ATT_BLOCK
    cos, sin = _rope_tables_host(l)
    cache_spec = pl.BlockSpec((None, None, lc, D_KV), lambda b: (b, li, 0, 0))
    y = pl.pallas_call(
        functools.partial(_lat_attn_kernel, nblk, li),
        grid=(nb,),
        in_specs=[_seq_spec(l, ATT_IN), _layer_const(wts["attn_q_norm"], li), _layer_const(wts["attn_k_norm"], li),
                  _const((l, D_ATT)), _const((l, D_ATT)), cache_spec, cache_spec,
                  pl.BlockSpec(memory_space=pltpu.SMEM)],
        out_specs=_seq_spec(l, D_ATT),
        out_shape=jax.ShapeDtypeStruct((nb, l, D_ATT), F32),
        scratch_shapes=[pltpu.VMEM((l, D_ATT), F32), pltpu.VMEM((l + 2 * ATT_BLOCK, D_KV), F32),
                        pltpu.VMEM((l + 2 * ATT_BLOCK, D_KV), F32)],
        compiler_params=_cparams("parallel"),
        name="lat_attention",
    )(u.reshape(nb, l, ATT_IN), wts["attn_q_norm"], wts["attn_k_norm"], jnp.asarray(cos), jnp.asarray(sin),
      ck, cv, wts["attn_sink"])
    return y.reshape(nb * l, D_ATT)


def _pack_mix_w_in(w):
    c0 = D_SSM + SSM_CONV_CH
    n_dt = 2 * SSM_HEADS
    zeros = jnp.zeros(w.shape[:-1] + (DT_PAD - n_dt,), w.dtype)
    return jnp.concatenate([w[..., :c0], w[..., c0 + n_dt:], w[..., c0:c0 + n_dt], zeros], axis=-1).astype(BF16)


def _prep_weights(p):
    row = lambda a: a.reshape(DEPTH, 1, -1)
    pad_lanes = lambda a: jnp.pad(row(a), ((0, 0), (0, 0), (0, LANES - a[0].size)))
    return dict(
        norm_w=p["norm_w"].reshape(DEPTH * 3, 1, D_MODEL),
        ffn_w_in=p["ffn_w_in"].astype(BF16).reshape(DEPTH * 2, D_MODEL, 2 * D_FF),
        ffn_w_out=p["ffn_w_out"].astype(BF16).reshape(DEPTH * 2, D_FF, D_MODEL),
        mix_w_in=_pack_mix_w_in(p["mix_w_in"]), mix_w_out=p["mix_w_out"].astype(BF16),
        ssd_conv_w=p["ssd_conv_w"], ssd_conv_b=row(p["ssd_conv_b"]),
        ssd_dt_bias=pad_lanes(p["ssd_dt_bias"]), ssd_a_log=pad_lanes(p["ssd_a_log"]),
        ssd_d=row(jnp.repeat(p["ssd_d"], SSM_HEAD_DIM, axis=-1)), ssd_norm_w=row(p["ssd_norm_w"]),
        hy_conv_w=p["hy_conv_w"], hy_conv_b=row(p["hy_conv_b"]), hy_bias=p["hy_bias"],
        hy_w1=jnp.pad(p["hy_w1"], ((0, 0), (0, LANES - HY_EMB), (0, 0))), hy_b1=row(p["hy_b1"]),
        hy_w2=p["hy_w2"], hy_b2=row(p["hy_b2"]), hy_w3=p["hy_w3"], hy_freq=row(p["hy_freq"]),
        ret_decay_logit=pad_lanes(p["ret_decay_logit"]), ret_gn_w=row(p["ret_gn_w"]),
        attn_q_norm=row(jnp.tile(p["attn_q_norm"], (1, ATT_HEADS))),
        attn_k_norm=row(jnp.tile(p["attn_k_norm"], (1, ATT_HEADS))),
        attn_sink=p["attn_sink"].reshape(DEPTH * ATT_HEADS),
    )


def _layer(x, mod, row0, wts, li, nb, l, ssd_s0, ret_s0, ctx_kv, hy_tables, hy_spectra):
    rpm = nb * l if ctx_kv is None else l
    x = _ffn(x, mod, row0, rpm, wts, li, 0)
    z, xbc, hy, ret, att, dt = _inproj(x, mod, row0, rpm, wts, li)
    y_ssd, s_ssd = _ssd(z, xbc, dt, wts, li, ssd_s0, nb, l)
    y_hy = _hyena(hy, wts, li, hy_tables, hy_spectra, nb, l)
    y_ret, s_ret = _retention(ret, wts, li, ret_s0, nb, l)
    if ctx_kv is None:
        y_att, k, v = _ctx_attention(att, wts, li, nb, l)
    else:
        y_att = _lat_attention(att, wts, li, ctx_kv[0], ctx_kv[1], nb, l)
        k = v = None
    x = _outproj(x, mod, row0, rpm, (y_ssd, y_hy, y_ret, y_att), wts, li)
    x = _ffn(x, mod, row0, rpm, wts, li, 1)
    return x, s_ssd, s_ret, k, v


def kernel(x_prompt, x_sample, cache_k, cache_v, state_ssd, state_ret, c, c_ctx, w_mod, b_mod, norm_w, ffn_w_in, ffn_w_out, mix_w_in, mix_w_out, ssd_conv_w, ssd_conv_b, ssd_dt_bias, ssd_a_log, ssd_d, ssd_norm_w, hy_conv_w, hy_conv_b, hy_w1, hy_b1, hy_w2, hy_b2, hy_w3, hy_freq, hy_bias, ret_decay_logit, ret_gn_w, attn_q_norm, attn_k_norm, attn_sink):
    bp, lp_len, _ = x_prompt.shape
    bs, ls_len, _ = x_sample.shape
    lc = cache_k.shape[2]

    cond = jnp.concatenate([c_ctx[None, :], c, jnp.zeros((MOD_ROWS - 1 - bs, D_MODEL), F32)], axis=0)
    mod = _modulation(cond, w_mod, b_mod)
    wts = _prep_weights(dict(
        norm_w=norm_w, ffn_w_in=ffn_w_in, ffn_w_out=ffn_w_out, mix_w_in=mix_w_in, mix_w_out=mix_w_out,
        ssd_conv_w=ssd_conv_w, ssd_conv_b=ssd_conv_b, ssd_dt_bias=ssd_dt_bias, ssd_a_log=ssd_a_log, ssd_d=ssd_d,
        ssd_norm_w=ssd_norm_w, hy_conv_w=hy_conv_w, hy_conv_b=hy_conv_b, hy_w1=hy_w1, hy_b1=hy_b1, hy_w2=hy_w2,
        hy_b2=hy_b2, hy_w3=hy_w3, hy_freq=hy_freq, hy_bias=hy_bias, ret_decay_logit=ret_decay_logit,
        ret_gn_w=ret_gn_w, attn_q_norm=attn_q_norm, attn_k_norm=attn_k_norm, attn_sink=attn_sink))
    ck = cache_k.reshape(bs, DEPTH, lc, D_KV)
    cv = cache_v.reshape(bs, DEPTH, lc, D_KV)

    tab_p = _dft_tables(lp_len)
    tab_s = _dft_tables(ls_len)

    yp = x_prompt.reshape(bp * lp_len, D_MODEL)
    ys = x_sample.reshape(bs * ls_len, D_MODEL)
    new_k, new_v, new_ssd, new_ret = [], [], [], []
    for li in range(DEPTH):
        spec_p = _hy_filter(lp_len, tab_p[0], tab_p[1], wts, li)
        spec_s = _hy_filter(ls_len, tab_s[0], tab_s[1], wts, li)
        yp, s_ssd, s_ret, k_ctx, v_ctx = _layer(yp, mod, li * MOD_ROWS, wts, li, bp, lp_len, None, None, None,
                                                tab_p, spec_p)
        new_k.append(k_ctx)
        new_v.append(v_ctx)
        new_ssd.append(s_ssd)
        new_ret.append(s_ret)
        ys = _layer(ys, mod, li * MOD_ROWS + 1, wts, li, bs, ls_len, state_ssd, state_ret, (ck, cv),
                    tab_s, spec_s)[0]

    kv_shape = (bp, DEPTH, lp_len, ATT_KV_HEADS, HEAD_DIM)
    return (yp.reshape(bp, lp_len, D_MODEL), ys.reshape(bs, ls_len, D_MODEL),
            jnp.concatenate(new_k, axis=1).reshape(kv_shape), jnp.concatenate(new_v, axis=1).reshape(kv_shape),
            jnp.concatenate(new_ssd, axis=1), jnp.concatenate(new_ret, axis=1))
```

```python
import functools
import math

import numpy as np
import jax
import jax.numpy as jnp
from jax import lax
from jax.experimental import pallas as pl
from jax.experimental.pallas import tpu as pltpu

F32 = jnp.float32
BF16 = jnp.bfloat16
HI = lax.Precision.HIGHEST

D_MODEL = 1024
DEPTH = 2
GRID_W = 64
D_FF = 2816
N_MOD = 9
NORM_EPS = 1e-6
CHUNK = 128
D_SSM = 256
SSM_HEADS = 4
SSM_HEAD_DIM = 64
SSM_STATE = 128
SSM_GROUPS = 2
SSM_CONV_CH = D_SSM + 2 * SSM_GROUPS * SSM_STATE
D_HY = 256
HY_ORDER = 2
HY_BANDS = 16
HY_EMB = 1 + 2 * HY_BANDS
HY_HIDDEN = 64
HY_FAST_DECAY = 0.3
HY_SLOW_DECAY = 1.5
HY_TARGET = 1e-2
HY_IN = (HY_ORDER + 1) * D_HY
D_RET = 256
RET_HEADS = 4
RET_HEAD_DIM = 64
RET_IN = 4 * D_RET
ATT_HEADS = 4
ATT_KV_HEADS = 2
HEAD_DIM = 64
D_ATT = ATT_HEADS * HEAD_DIM
D_KV = ATT_KV_HEADS * HEAD_DIM
ATT_IN = D_ATT + 2 * D_KV
WINDOW = 128
ATT_BLOCK = 128
ROPE_BASE = 10000.0
D_MIX = D_SSM + D_HY + D_RET + D_ATT

LANES = 128
DT_PAD = LANES
D_IN_PAD = D_SSM + SSM_CONV_CH + HY_IN + RET_IN + ATT_IN + DT_PAD
VMEM_LIMIT = 56 * 1024 * 1024
MOD_ROWS = 8

TOKEN_TILE = 512
DFT_BLOCK = 256


def _cparams(*sem):
    return pltpu.CompilerParams(dimension_semantics=sem, vmem_limit_bytes=VMEM_LIMIT)


def _rms(x, w):
    return x * lax.rsqrt(jnp.mean(x * x, axis=-1, keepdims=True) + NORM_EPS) * w


def _silu(x):
    return x * (1.0 / (1.0 + jnp.exp(-x)))


def _softplus(x):
    return jnp.maximum(x, 0.0) + jnp.log1p(jnp.exp(-jnp.abs(x)))


def _bdot(a, b):
    return jnp.dot(a.astype(BF16), b.astype(BF16), preferred_element_type=F32)


def _bdot_nt(a, b):
    return lax.dot_general(a.astype(BF16), b.astype(BF16), (((1,), (1,)), ((), ())),
                           preferred_element_type=F32)


def _bdot_tn(a, b):
    return lax.dot_general(a.astype(BF16), b.astype(BF16), (((0,), (0,)), ((), ())),
                           preferred_element_type=F32)


def _hdot(a, b):
    return jnp.dot(a, b, preferred_element_type=F32, precision=HI)


def _full(shape):
    n = len(shape)
    return pl.BlockSpec(shape, lambda *_: (0,) * n)


def _const(shape):
    n = len(shape)
    return pl.BlockSpec(shape, lambda *_: (0,) * n, pipeline_mode=pl.Buffered(1))


def _layer_const(arr, li):
    tail = arr.shape[1:]
    zeros = (0,) * len(tail)
    return pl.BlockSpec((None,) + tail, lambda *_: (li,) + zeros, pipeline_mode=pl.Buffered(1))


def _mod_spec(k, tm, rows_per_mod, row0):
    return pl.BlockSpec((None, 1, D_MODEL), lambda i: (row0 + (i * tm) // rows_per_mod, 0, k))


def _seq_spec(l, w):
    return pl.BlockSpec((None, l, w), lambda b: (b, 0, 0))


def _head_masks(width, heads):
    lane = lax.broadcasted_iota(jnp.int32, (1, width), 1)
    hd = width // heads
    return [(lane >= h * hd) & (lane < (h + 1) * hd) for h in range(heads)]


def _by_head(masks, vals):
    out = vals[-1]
    for m, v in zip(masks[-2::-1], vals[-2::-1]):
        out = jnp.where(m, v, out)
    return out


def _block_diag(n, blk, value):
    i = lax.broadcasted_iota(jnp.int32, (n, n), 0) // blk
    j = lax.broadcasted_iota(jnp.int32, (n, n), 1) // blk
    return jnp.where(i == j, value, 0.0).astype(F32)


def _mod_kernel(c_ref, w_ref, b_ref, o_ref):
    c = c_ref[...]
    o_ref[...] = _bdot(_silu(c), w_ref[...]) + b_ref[...]


def _modulation(cond, w_mod, b_mod):
    out = pl.pallas_call(
        _mod_kernel,
        grid=(DEPTH, N_MOD),
        in_specs=[pl.BlockSpec((MOD_ROWS, D_MODEL), lambda l, j: (0, 0)),
                  pl.BlockSpec((None, D_MODEL, D_MODEL), lambda l, j: (l, 0, j)),
                  pl.BlockSpec((None, 1, D_MODEL), lambda l, j: (l, 0, j))],
        out_specs=pl.BlockSpec((None, MOD_ROWS, D_MODEL), lambda l, j: (l, 0, j)),
        out_shape=jax.ShapeDtypeStruct((DEPTH, MOD_ROWS, N_MOD * D_MODEL), F32),
        compiler_params=_cparams("arbitrary", "arbitrary"),
        name="modulation",
    )(cond, w_mod, b_mod.reshape(DEPTH, 1, N_MOD * D_MODEL))
    return out.reshape(DEPTH * MOD_ROWS, 1, N_MOD * D_MODEL)


def _ffn_kernel(x_ref, sh_ref, sc_ref, g_ref, nw_ref, wi_ref, wo_ref, o_ref):
    x = x_ref[...]
    h = (_rms(x, nw_ref[...]) * (1.0 + sc_ref[...]) + sh_ref[...]).astype(BF16)
    gate = jnp.dot(h, wi_ref[:, 0:D_FF], preferred_element_type=F32)
    up = jnp.dot(h, wi_ref[:, D_FF:2 * D_FF], preferred_element_type=F32)
    o_ref[...] = x + 0.5 * g_ref[...] * _bdot(_silu(gate) * up, wo_ref[...])


def _ffn(x, mod, row0, rows_per_mod, wts, li, k):
    t = x.shape[0]
    tm = min(TOKEN_TILE, rows_per_mod)
    w_in, w_out = wts["ffn_w_in"], wts["ffn_w_out"]
    return pl.pallas_call(
        _ffn_kernel,
        grid=(t // tm,),
        in_specs=[pl.BlockSpec((tm, D_MODEL), lambda i: (i, 0)),
                  _mod_spec(6 * k, tm, rows_per_mod, row0),
                  _mod_spec(6 * k + 1, tm, rows_per_mod, row0),
                  _mod_spec(6 * k + 2, tm, rows_per_mod, row0),
                  _layer_const(wts["norm_w"], 3 * li + 2 * k),
                  _layer_const(w_in, 2 * li + k), _layer_const(w_out, 2 * li + k)],
        out_specs=pl.BlockSpec((tm, D_MODEL), lambda i: (i, 0)),
        out_shape=jax.ShapeDtypeStruct((t, D_MODEL), F32),
        compiler_params=_cparams("parallel"),
        name="ffn",
    )(x, mod, mod, mod, wts["norm_w"], w_in, w_out)


_IN_SPLITS = (("z", D_SSM), ("xbc", SSM_CONV_CH), ("hy", HY_IN), ("ret", RET_IN), ("att", ATT_IN), ("dt", DT_PAD))


def _inproj_kernel(x_ref, sh_ref, sc_ref, nw_ref, w_ref, *o_refs):
    h = (_rms(x_ref[...], nw_ref[...]) * (1.0 + sc_ref[...]) + sh_ref[...]).astype(BF16)
    off = 0
    for (_, width), o_ref in zip(_IN_SPLITS, o_refs):
        o_ref[...] = jnp.dot(h, w_ref[:, off:off + width], preferred_element_type=F32)
        off += width


def _inproj(x, mod, row0, rows_per_mod, wts, li):
    t = x.shape[0]
    tm = min(TOKEN_TILE, rows_per_mod)
    return pl.pallas_call(
        _inproj_kernel,
        grid=(t // tm,),
        in_specs=[pl.BlockSpec((tm, D_MODEL), lambda i: (i, 0)),
                  _mod_spec(3, tm, rows_per_mod, row0),
                  _mod_spec(4, tm, rows_per_mod, row0),
                  _layer_const(wts["norm_w"], 3 * li + 1),
                  _layer_const(wts["mix_w_in"], li)],
        out_specs=[pl.BlockSpec((tm, width), lambda i: (i, 0)) for _, width in _IN_SPLITS],
        out_shape=[jax.ShapeDtypeStruct((t, width), F32) for _, width in _IN_SPLITS],
        compiler_params=_cparams("parallel"),
        name="mix_in",
    )(x, mod, mod, wts["norm_w"], wts["mix_w_in"])


def _outproj_kernel(x_ref, g_ref, y0_ref, y1_ref, y2_ref, y3_ref, w_ref, o_ref):
    acc = _bdot(y0_ref[...], w_ref[0:256, :])
    acc += _bdot(y1_ref[...], w_ref[256:512, :])
    acc += _bdot(y2_ref[...], w_ref[512:768, :])
    acc += _bdot(y3_ref[...], w_ref[768:1024, :])
    o_ref[...] = x_ref[...] + g_ref[...] * acc


def _outproj(x, mod, row0, rows_per_mod, ys, wts, li):
    t = x.shape[0]
    tm = min(TOKEN_TILE, rows_per_mod)
    yspec = pl.BlockSpec((tm, 256), lambda i: (i, 0))
    return pl.pallas_call(
        _outproj_kernel,
        grid=(t // tm,),
        in_specs=[pl.BlockSpec((tm, D_MODEL), lambda i: (i, 0)),
                  _mod_spec(5, tm, rows_per_mod, row0),
                  yspec, yspec, yspec, yspec,
                  _layer_const(wts["mix_w_out"], li)],
        out_specs=pl.BlockSpec((tm, D_MODEL), lambda i: (i, 0)),
        out_shape=jax.ShapeDtypeStruct((t, D_MODEL), F32),
        compiler_params=_cparams("parallel"),
        name="mix_out",
    )(x, mod, *ys, wts["mix_w_out"])


def _conv3_chunk(x_ref, c, nc, w, b):
    q = CHUNK
    l = nc * q
    r0 = pl.multiple_of(c * q, q)
    x = x_ref[pl.ds(r0, q), :]
    prev = x_ref[pl.ds(jnp.maximum(r0 - 1, 0), 1), :]
    nxt = x_ref[pl.ds(jnp.minimum(r0 + q, l - 1), 1), :]
    prev = jnp.where(c > 0, prev, 0.0)
    nxt = jnp.where(c < nc - 1, nxt, 0.0)
    rid = lax.broadcasted_iota(jnp.int32, (q, 1), 0)
    xm1 = jnp.where(rid == 0, prev, pltpu.roll(x, 1, 0))
    xp1 = jnp.where(rid == q - 1, nxt, pltpu.roll(x, q - 1, 0))
    return xm1 * w[0:1, :] + x * w[1:2, :] + xp1 * w[2:3, :] + b


def _tri(lower):
    i = lax.broadcasted_iota(jnp.int32, (CHUNK, CHUNK), 0)
    j = lax.broadcasted_iota(jnp.int32, (CHUNK, CHUNK), 1)
    return (j <= i) if lower else (j >= i)


def _rows(c):
    return pl.ds(pl.multiple_of(c * CHUNK, CHUNK), CHUNK)


def _split_cat(v, parts, axis):
    out, r = [], v
    for i in range(parts):
        piece = r.astype(BF16)
        out.append(piece)
        if i + 1 < parts:
            r = r - piece.astype(F32)
    return jnp.concatenate(out, axis=axis)


def _seg_mean(x, seg):
    w = x.shape[-1]
    ones = _block_diag(w, seg, 1.0).astype(BF16)
    return jnp.dot(_split_cat(x, 2, axis=1), jnp.concatenate([ones, ones], axis=0),
                   preferred_element_type=F32) * (1.0 / seg)


def _stack_heads(x, masks):
    return jnp.concatenate([jnp.where(m, x, 0.0) for m in masks], axis=0)


def _unstack_heads(y, masks):
    q = y.shape[0] // len(masks)
    out = jnp.where(masks[0], y[0:q], 0.0)
    for h in range(1, len(masks)):
        out = out + jnp.where(masks[h], y[h * q:(h + 1) * q], 0.0)
    return out


def _state_out_spec(heads, n, p):
    return pl.BlockSpec((None, None, 2, heads, n, p), lambda b: (b, 0, 0, 0, 0, 0))


SSD_SEL_W = 2 * SSM_HEADS * SSM_STATE + D_SSM
DT_REP = 3


def _ssd_kernel(nc, has_s0, *refs):
    refs = list(refs)
    (z_ref, xbc_ref, dt_ref, cw_ref, cb_ref, dtb_ref, alog_ref, dsk_ref, nw_ref) = refs[:9]
    s0_ref = refs[9] if has_s0 else None
    (y_ref, s_ref, xs_scr, xk_scr, xt_scr, dt_scr, cum_scr, dtt_scr, cumt_scr, yf_scr, yb_scr, st_scr,
     msk_scr, sel_scr) = refs[-14:]
    q = CHUNK
    nh, n, p = SSM_HEADS, SSM_STATE, SSM_HEAD_DIM
    rep = nh // SSM_GROUPS
    hm = _head_masks(D_SSM, nh)

    @pl.when(pl.program_id(0) == 0)
    def _():
        i = lax.broadcasted_iota(jnp.int32, (D_SSM, nh * n), 0) // p
        j = lax.broadcasted_iota(jnp.int32, (D_SSM, nh * n), 1) // n
        msk_scr[...] = jnp.where(i == j, 1.0, 0.0).astype(F32)
        k = lax.broadcasted_iota(jnp.int32, (DT_REP * LANES, SSD_SEL_W), 0) % LANES
        col = lax.broadcasted_iota(jnp.int32, (DT_REP * LANES, SSD_SEL_W), 1)
        grp = jnp.where(col < nh * n, 0, jnp.where(col < 2 * nh * n, 1, 2))
        head = jnp.where(col < 2 * nh * n, (col % (nh * n)) // n, (col - 2 * nh * n) // p)
        for d in range(2):
            sel_scr[d] = jnp.where(k == grp * 2 * nh + d * nh + head, 1.0, 0.0).astype(BF16)

    cw = cw_ref[...]
    cb = cb_ref[...]
    neg_a = -jnp.exp(alog_ref[...])
    dtb = dtb_ref[...]

    def prep(c, carry):
        rows = _rows(c)
        xall = _silu(_conv3_chunk(xbc_ref, c, nc, cw, cb))
        xs_scr[rows, :] = xall
        xs = xall[:, 0:D_SSM]
        xk_scr[c] = _stack_heads(xs, hm).astype(BF16)
        xt_scr[c] = xs.T.astype(BF16)
        dt = _softplus(dt_ref[rows, :] + dtb)
        la = _split_cat(dt * neg_a, 3, axis=0)
        tri = jnp.concatenate([_tri(True), _tri(False)], axis=0)
        tri = jnp.where(tri, 1.0, 0.0).astype(BF16)
        cs = jnp.dot(jnp.concatenate([tri] * 3, axis=1), la, preferred_element_type=F32)
        lane = lax.broadcasted_iota(jnp.int32, (1, LANES), 1)
        cum = jnp.where(lane % (2 * nh) < nh, cs[0:q], cs[q:2 * q])
        dt_scr[rows, :] = dt
        cum_scr[rows, :] = cum
        dtt_scr[c] = dt.T
        cumt_scr[c] = cum.T
        return carry

    lax.fori_loop(0, nc, prep, 0)

    for d in range(2):
        for h in range(nh):
            blk = slice(h * p, (h + 1) * p)
            if has_s0:
                parts = [jnp.zeros((p, n), F32)] * nh
                parts[h] = s0_ref[d, h].T
                st_scr[d, blk, :] = jnp.concatenate(parts, axis=1)
            else:
                st_scr[d, blk, :] = jnp.zeros((p, nh * n), F32)

    def scan_chunk(c, d):
        rows = _rows(c)
        bm = [xs_scr[rows, D_SSM + g * n:D_SSM + (g + 1) * n] for g in range(SSM_GROUPS)]
        cm = [xs_scr[rows, D_SSM + (SSM_GROUPS + g) * n:D_SSM + (SSM_GROUPS + g + 1) * n]
              for g in range(SSM_GROUPS)]
        dt = dt_scr[rows, :]
        cum = cum_scr[rows, :]
        mask = _tri(d == 0)
        edge = q - 1 if d == 0 else 0
        lane = lax.broadcasted_iota(jnp.int32, (1, LANES), 1)
        used = (lane < DT_REP * 2 * nh) & ((lane % (2 * nh)) // nh == d)
        cum = jnp.where(used, cum, 0.0)
        wgt = jnp.exp(cum[edge:edge + 1, :] - cum) * dt
        v = jnp.where(lane < 2 * nh, cum, jnp.where(lane < 4 * nh, wgt, jnp.exp(cum)))
        v = jnp.where(used, v, 0.0)
        e = jnp.dot(_split_cat(v, 3, axis=1), sel_scr[d], preferred_element_type=F32)
        cb_t = [_bdot_nt(cm[g], bm[g]) for g in range(SSM_GROUPS)]
        blocks = []
        for h in range(nh):
            r = d * nh + h
            seg = e[:, h * n:(h + 1) * n] - cumt_scr[c, r:r + 1, :]
            decay = jnp.exp(jnp.where(mask, seg, -jnp.inf))
            blocks.append((cb_t[h // rep] * decay * dtt_scr[c, r:r + 1, :]).astype(BF16))
        y = jnp.dot(jnp.concatenate(blocks, axis=1), xk_scr[c], preferred_element_type=F32)
        s_old = st_scr[d]
        c4 = jnp.concatenate([cm[h // rep] for h in range(nh)], axis=1)
        y = y + e[:, 2 * nh * n:SSD_SEL_W] * _bdot_nt(c4, s_old)
        wk = jnp.concatenate([bm[h // rep] for h in range(nh)], axis=1) * e[:, nh * n:2 * nh * n]
        upd = jnp.dot(xt_scr[c], wk.astype(BF16), preferred_element_type=F32)
        st_scr[d] = jnp.exp(e[edge:edge + 1, 0:nh * n]) * s_old + upd * msk_scr[...]
        return y

    def scan(i, carry):
        yf_scr[_rows(i), :] = scan_chunk(i, 0)
        yb_scr[_rows(nc - 1 - i), :] = scan_chunk(nc - 1 - i, 1)
        return carry

    lax.fori_loop(0, nc, scan, 0)

    for d in range(2):
        for h in range(nh):
            s_ref[d, h] = st_scr[d, h * p:(h + 1) * p, h * n:(h + 1) * n].T

    dsk = dsk_ref[...]
    nw = nw_ref[...]

    def fin(c, carry):
        rows = _rows(c)
        y = yf_scr[rows, :] + yb_scr[rows, :] + dsk * xs_scr[rows, 0:D_SSM]
        y_ref[rows, :] = _rms(y * _silu(z_ref[rows, :]), nw)
        return carry

    lax.fori_loop(0, nc, fin, 0)


def _ssd(z, xbc, dt, wts, li, s0, nb, l):
    nc = l // CHUNK
    has_s0 = s0 is not None
    names = ("ssd_conv_w", "ssd_conv_b", "ssd_dt_bias", "ssd_a_log", "ssd_d", "ssd_norm_w")
    args = [z.reshape(nb, l, D_SSM), xbc.reshape(nb, l, SSM_CONV_CH), dt.reshape(nb, l, DT_PAD)]
    args += [wts[n] for n in names]
    in_specs = [_seq_spec(l, D_SSM), _seq_spec(l, SSM_CONV_CH), _seq_spec(l, DT_PAD)]
    in_specs += [_layer_const(wts[n], li) for n in names]
    if has_s0:
        args.append(s0)
        in_specs.append(pl.BlockSpec((None, None, 2, SSM_HEADS, SSM_STATE, SSM_HEAD_DIM),
                                     lambda b: (b, li, 0, 0, 0, 0)))
    y, s = pl.pallas_call(
        functools.partial(_ssd_kernel, nc, has_s0),
        grid=(nb,),
        in_specs=in_specs,
        out_specs=[_seq_spec(l, D_SSM), _state_out_spec(SSM_HEADS, SSM_STATE, SSM_HEAD_DIM)],
        out_shape=[jax.ShapeDtypeStruct((nb, l, D_SSM), F32),
                   jax.ShapeDtypeStruct((nb, 1, 2, SSM_HEADS, SSM_STATE, SSM_HEAD_DIM), F32)],
        scratch_shapes=[pltpu.VMEM((l, SSM_CONV_CH), F32),
                        pltpu.VMEM((nc, SSM_HEADS * CHUNK, D_SSM), BF16), pltpu.VMEM((nc, D_SSM, CHUNK), BF16),
                        pltpu.VMEM((l, DT_PAD), F32), pltpu.VMEM((l, DT_PAD), F32),
                        pltpu.VMEM((nc, DT_PAD, CHUNK), F32), pltpu.VMEM((nc, DT_PAD, CHUNK), F32),
                        pltpu.VMEM((l, D_SSM), F32), pltpu.VMEM((l, D_SSM), F32),
                        pltpu.VMEM((2, D_SSM, SSM_HEADS * SSM_STATE), F32),
                        pltpu.VMEM((D_SSM, SSM_HEADS * SSM_STATE), F32),
                        pltpu.VMEM((2, DT_REP * LANES, SSD_SEL_W), BF16)],
        compiler_params=_cparams("arbitrary"),
        name="ssd",
    )(*args)
    return y.reshape(nb * l, D_SSM), s


def _ret_kernel(nc, has_s0, *refs):
    refs = list(refs)
    u_ref, dl_ref, gn_ref = refs[:3]
    s0_ref = refs[3] if has_s0 else None
    y_ref, s_ref, yf_scr, yb_scr, st_scr, dm_scr, e_scr, ea_scr, bd_scr = refs[-9:]
    q = CHUNK
    hd = RET_HEAD_DIM
    hm = _head_masks(D_RET, RET_HEADS)

    @pl.when(pl.program_id(0) == 0)
    def _():
        log_g = -_softplus(-dl_ref[...])
        ii = lax.broadcasted_iota(jnp.int32, (q, q), 0)
        jj = lax.broadcasted_iota(jnp.int32, (q, q), 1)
        dij = (ii - jj).astype(F32)
        ri = lax.broadcasted_iota(jnp.int32, (q, 1), 0).astype(F32)
        lfs, lbs = [], []
        for h in range(RET_HEADS):
            lf = log_g[:, h:h + 1]
            lb = log_g[:, RET_HEADS + h:RET_HEADS + h + 1]
            lfs.append(lf)
            lbs.append(lb)
            d_f = jnp.exp(jnp.where(dij >= 0, dij * lf, -jnp.inf))
            d_b = jnp.exp(jnp.where(dij <= 0, -dij * lb, -jnp.inf))
            dm_scr[:, h * q:(h + 1) * q] = d_f + d_b
        lf_l = _by_head(hm, lfs)
        lb_l = _by_head(hm, lbs)
        e_scr[0] = jnp.exp((ri + 1.0) * lf_l)
        e_scr[1] = jnp.exp((q - ri) * lb_l)
        e_scr[2] = jnp.exp((q - 1.0 - ri) * lf_l)
        e_scr[3] = jnp.exp(ri * lb_l)
        ea_scr[0:1, :] = jnp.exp(q * lf_l)
        ea_scr[1:2, :] = jnp.exp(q * lb_l)
        bd_scr[...] = _block_diag(D_RET, hd, 1.0)

    for d in range(2):
        for h in range(RET_HEADS):
            blk = slice(h * hd, (h + 1) * hd)
            if has_s0:
                parts = [jnp.zeros((hd, hd), F32)] * RET_HEADS
                parts[h] = s0_ref[d, h]
                st_scr[d, blk, :] = jnp.concatenate(parts, axis=1)
            else:
                st_scr[d, blk, :] = jnp.zeros((hd, D_RET), F32)

    def qkv(rows):
        return (u_ref[rows, 0:D_RET], u_ref[rows, D_RET:2 * D_RET] * (RET_HEAD_DIM ** -0.5),
                u_ref[rows, 2 * D_RET:3 * D_RET])

    def state_step(d, qq, kk, vv):
        s_old = st_scr[d]
        y = e_scr[d] * _bdot(qq, s_old)
        st_scr[d] = ea_scr[d:d + 1, :] * s_old + _bdot_tn(kk * e_scr[2 + d], vv) * bd_scr[...]
        return y

    def scan(i, carry):
        rows = _rows(i)
        qq, kk, vv = qkv(rows)
        sc = _bdot_nt(qq, _stack_heads(kk, hm)) * dm_scr[...]
        yf_scr[rows, :] = _bdot(sc, _stack_heads(vv, hm)) + state_step(0, qq, kk, vv)
        rows_b = _rows(nc - 1 - i)
        qq, kk, vv = qkv(rows_b)
        yb_scr[rows_b, :] = state_step(1, qq, kk, vv)
        return carry

    lax.fori_loop(0, nc, scan, 0)

    for d in range(2):
        for h in range(RET_HEADS):
            s_ref[d, h] = st_scr[d, h * hd:(h + 1) * hd, h * hd:(h + 1) * hd]

    gn = gn_ref[...]

    def fin(c, carry):
        rows = _rows(c)
        y = yf_scr[rows, :] + yb_scr[rows, :]
        cen = y - _seg_mean(y, hd)
        var = _seg_mean(cen * cen, hd)
        y_ref[rows, :] = cen * lax.rsqrt(var + NORM_EPS) * gn * _silu(u_ref[rows, 3 * D_RET:4 * D_RET])
        return carry

    lax.fori_loop(0, nc, fin, 0)


def _retention(u, wts, li, s0, nb, l):
    nc = l // CHUNK
    has_s0 = s0 is not None
    names = ("ret_decay_logit", "ret_gn_w")
    args = [u.reshape(nb, l, RET_IN)] + [wts[n] for n in names]
    in_specs = [_seq_spec(l, RET_IN)] + [_layer_const(wts[n], li) for n in names]
    if has_s0:
        args.append(s0)
        in_specs.append(pl.BlockSpec((None, None, 2, RET_HEADS, RET_HEAD_DIM, RET_HEAD_DIM),
                                     lambda b: (b, li, 0, 0, 0, 0)))
    y, s = pl.pallas_call(
        functools.partial(_ret_kernel, nc, has_s0),
        grid=(nb,),
        in_specs=in_specs,
        out_specs=[_seq_spec(l, D_RET), _state_out_spec(RET_HEADS, RET_HEAD_DIM, RET_HEAD_DIM)],
        out_shape=[jax.ShapeDtypeStruct((nb, l, D_RET), F32),
                   jax.ShapeDtypeStruct((nb, 1, 2, RET_HEADS, RET_HEAD_DIM, RET_HEAD_DIM), F32)],
        scratch_shapes=[pltpu.VMEM((l, D_RET), F32), pltpu.VMEM((l, D_RET), F32),
                        pltpu.VMEM((2, D_RET, D_RET), F32), pltpu.VMEM((CHUNK, RET_HEADS * CHUNK), F32),
                        pltpu.VMEM((4, CHUNK, D_RET), F32), pltpu.VMEM((8, D_RET), F32),
                        pltpu.VMEM((D_RET, D_RET), F32)],
        compiler_params=_cparams("arbitrary"),
        name="retention",
    )(*args)
    return y.reshape(nb * l, D_RET), s


def _split(x):
    hi = x.astype(BF16)
    return hi, (x - hi.astype(F32)).astype(BF16)


def _dot3(a_hi, a_lo, b_hi, b_lo):
    d = lambda p, q: jnp.dot(p, q, preferred_element_type=F32)
    return d(a_hi, b_hi) + (d(a_lo, b_hi) + d(a_hi, b_lo))


@functools.lru_cache(maxsize=None)
def _dft_fwd_host(l):
    n = 2 * l
    f = np.arange(l, dtype=np.int64)[:, None]
    s = np.arange(l, dtype=np.int64)[None, :]
    ang = ((f * s) % n).astype(np.float64) * (2.0 * math.pi / n)
    im = -np.sin(ang)
    im[0] = np.where(np.arange(l) % 2 == 0, 1.0, -1.0)
    return np.concatenate([np.cos(ang), im], axis=0).astype(np.float32)


def _dft_tables(l):
    n = 2 * l
    fwd = jnp.asarray(_dft_fwd_host(l))
    wgt = np.full((n, 1), 2.0 / n, np.float32)
    wgt[0] = wgt[l] = 1.0 / n
    return _split(fwd) + _split((fwd * wgt).T)


def _dft_block_tables(tables, l):
    fwd_hi, fwd_lo, inv_hi, inv_lo = tables
    r = DFT_BLOCK
    nblk = l // r
    fb = lambda t: (t[0:l].reshape(nblk, r, l), t[l:2 * l].reshape(nblk, r, l))
    fwd_blk = jnp.concatenate(fb(fwd_hi) + fb(fwd_lo), axis=1)
    inv_blk = jnp.concatenate([inv_hi.reshape(nblk, r, 2 * l), inv_lo.reshape(nblk, r, 2 * l)], axis=1)
    return fwd_blk, inv_blk


def _hy_filter_kernel(l, feats_ref, dec_ref, w1_ref, b1_ref, w2_ref, b2_ref, w3_ref, fr_ref, fh_ref, fl_ref,
                      a_ref, b_ref, d_ref):
    fr = fr_ref[...]
    h = jnp.sin(fr * (_hdot(feats_ref[...], w1_ref[...]) + b1_ref[...]))
    h = jnp.sin(fr * (_hdot(h, w2_ref[...]) + b2_ref[...]))
    h = _hdot(h, w3_ref[...])
    dec = jnp.concatenate([dec_ref[...]] * HY_ORDER, axis=-1)
    row0 = lax.broadcasted_iota(jnp.int32, (l, 1), 0) == 0
    hf = h[:, 0:HY_ORDER * D_HY] * dec
    hb = h[:, HY_ORDER * D_HY:2 * HY_ORDER * D_HY] * dec
    hb = jnp.where(row0, 0.0, hb)
    hs = _split(hf + hb)
    hd = _split(hf - hb)
    re = _dot3(fh_ref[0:l, :], fl_ref[0:l, :], *hs)
    ny = _dot3(fh_ref[l:l + 8, :], fl_ref[l:l + 8, :], *hs)[0:1]
    im = _dot3(fh_ref[l:2 * l, :], fl_ref[l:2 * l, :], *hd)
    for o in range(HY_ORDER):
        cols = slice(o * D_HY, (o + 1) * D_HY)
        a_ref[o] = re[:, cols]
        b_ref[o] = jnp.where(row0, 0.0, im[:, cols])
        d_ref[o] = jnp.where(row0, ny[:, cols], re[:, cols])


def _hy_filter(l, fwd_hi, fwd_lo, wts, li):
    pos = np.arange(l, dtype=np.float32)
    t = pos / np.float32(l - 1)
    bands = np.linspace(1e-4, HY_BANDS - 1, HY_BANDS, dtype=np.float32)
    ang = np.float32(2.0 * math.pi / l) * pos[:, None] * bands[None, :]
    feats = np.concatenate([t[:, None], np.cos(ang), -np.sin(ang)], axis=-1).astype(np.float32)
    feats = np.pad(feats, ((0, 0), (0, LANES - HY_EMB)))
    max_decay = math.log(HY_TARGET) / HY_FAST_DECAY
    min_decay = math.log(HY_TARGET) / HY_SLOW_DECAY
    deltas = np.abs(np.linspace(min_decay, max_decay, D_HY, dtype=np.float32))
    dec = np.exp(-t[:, None] * deltas[None, :]).astype(np.float32)
    spec = jax.ShapeDtypeStruct((HY_ORDER, l, D_HY), F32)
    names = ("hy_w1", "hy_b1", "hy_w2", "hy_b2", "hy_w3", "hy_freq")
    return pl.pallas_call(
        functools.partial(_hy_filter_kernel, l),
        grid=(1,),
        in_specs=[_full(feats.shape), _full(dec.shape)] + [_layer_const(wts[n], li) for n in names]
                 + [_full(fwd_hi.shape), _full(fwd_lo.shape)],
        out_specs=[_full(spec.shape)] * 3,
        out_shape=[spec] * 3,
        compiler_params=_cparams("arbitrary"),
        name="hyena_filter",
    )(jnp.asarray(feats), jnp.asarray(dec), *[wts[n] for n in names], fwd_hi, fwd_lo)


def _hy_kernel(nc, u_ref, cw_ref, cb_ref, f_ref, g_ref, a_ref, b_ref, d_ref, bias_ref, y_ref,
               uc_scr, vh_scr, vl_scr, sh_scr, sl_scr, z_scr):
    l = nc * CHUNK
    r = DFT_BLOCK
    nblk = l // r
    cw = cw_ref[...]
    cb = cb_ref[...]
    dot = lambda p, q: jnp.dot(p, q, preferred_element_type=F32)

    def for_blocks(body):
        if nblk == 1:
            body(0)
        else:
            lax.fori_loop(0, nblk, lambda i, carry: (body(i), carry)[1], 0)

    def conv(c, carry):
        rows = _rows(c)
        uc = _conv3_chunk(u_ref, c, nc, cw, cb)
        uc_scr[rows, :] = uc
        vh_scr[rows, :], vl_scr[rows, :] = _split(uc[:, 0:D_HY])
        return carry

    lax.fori_loop(0, nc, conv, 0)

    def long_conv(o, src_ref, src_col, gate_col, dst_ref, split_dst):
        def spectrum(i):
            rows = pl.ds(pl.multiple_of(i * r, r), r)
            rows_im = pl.ds(pl.multiple_of(l + i * r, r), r)
            p = dot(f_ref[i], vh_scr[...])
            pl_ = dot(f_ref[i, 0:2 * r, :], vl_scr[...])
            zr = p[0:r] + (p[2 * r:3 * r] + pl_[0:r])
            zi = p[r:2 * r] + (p[3 * r:4 * r] + pl_[r:2 * r])
            sh_scr[rows, :], sl_scr[rows, :] = _split(zr * a_ref[o, rows, :] - zi * b_ref[o, rows, :])
            sh_scr[rows_im, :], sl_scr[rows_im, :] = _split(zr * b_ref[o, rows, :] + zi * d_ref[o, rows, :])

        for_blocks(spectrum)

        def inverse(i):
            rows = pl.ds(pl.multiple_of(i * r, r), r)
            p = dot(g_ref[i], sh_scr[...])
            y = p[0:r] + (p[r:2 * r] + dot(g_ref[i, 0:r, :], sl_scr[...]))
            v = src_ref[rows, src_col:src_col + D_HY]
            gate = uc_scr[rows, gate_col:gate_col + D_HY]
            out = gate * (y + v * bias_ref[o:o + 1, :])
            dst_ref[rows, :] = out
            if split_dst:
                vh_scr[rows, :], vl_scr[rows, :] = _split(out)

        for_blocks(inverse)

    long_conv(0, uc_scr, 0, D_HY, z_scr, True)
    long_conv(1, z_scr, 0, 2 * D_HY, y_ref, False)


def _hyena(u, wts, li, tables, spectra, nb, l):
    nc = l // CHUNK
    a, b, d = spectra
    names = ("hy_conv_w", "hy_conv_b")
    y = pl.pallas_call(
        functools.partial(_hy_kernel, nc),
        grid=(nb,),
        in_specs=[_seq_spec(l, HY_IN)] + [_layer_const(wts[n], li) for n in names]
                 + [_const(tables[0].shape), _const(tables[1].shape),
                    _const((HY_ORDER, l, D_HY)), _const((HY_ORDER, l, D_HY)), _const((HY_ORDER, l, D_HY)),
                    _layer_const(wts["hy_bias"], li)],
        out_specs=_seq_spec(l, D_HY),
        out_shape=jax.ShapeDtypeStruct((nb, l, D_HY), F32),
        scratch_shapes=[pltpu.VMEM((l, HY_IN), F32), pltpu.VMEM((l, D_HY), BF16), pltpu.VMEM((l, D_HY), BF16),
                        pltpu.VMEM((2 * l, D_HY), BF16), pltpu.VMEM((2 * l, D_HY), BF16),
                        pltpu.VMEM((l, D_HY), F32)],
        compiler_params=_cparams("parallel"),
        name="hyena",
    )(u.reshape(nb, l, HY_IN), *[wts[n] for n in names], *tables, a, b, d, wts["hy_bias"])
    return y.reshape(nb * l, D_HY)


def _seg_rms(x, w):
    return x * lax.rsqrt(_seg_mean(x * x, HEAD_DIM) + NORM_EPS) * w


def _stack_q(q):
    lo = lax.broadcasted_iota(jnp.int32, (1, D_KV), 1) < HEAD_DIM
    qa = q[:, 0:D_KV]
    qb = q[:, D_KV:2 * D_KV]
    return jnp.concatenate([jnp.where(lo, qa, 0.0), jnp.where(lo, pltpu.roll(qa, HEAD_DIM, 1), 0.0),
                            jnp.where(lo, 0.0, pltpu.roll(qb, HEAD_DIM, 1)), jnp.where(lo, 0.0, qb)], axis=0)


def _unstack_o(o):
    r = o.shape[0] // ATT_HEADS
    lo = lax.broadcasted_iota(jnp.int32, (1, D_KV), 1) < HEAD_DIM
    ya = jnp.where(lo, o[0:r], pltpu.roll(o[r:2 * r], HEAD_DIM, 1))
    yb = jnp.where(lo, pltpu.roll(o[2 * r:3 * r], HEAD_DIM, 1), o[3 * r:4 * r])
    return jnp.concatenate([ya, yb], axis=1)


def _sink_col(sink_ref, li, r):
    rb = lax.broadcasted_iota(jnp.int32, (ATT_HEADS * r, 1), 0) // r
    col = jnp.full((ATT_HEADS * r, 1), sink_ref[li * ATT_HEADS + ATT_HEADS - 1], F32)
    for h in range(ATT_HEADS - 2, -1, -1):
        col = jnp.where(rb == h, sink_ref[li * ATT_HEADS + h], col)
    return col


def _ctx_attn_kernel(li, u_ref, qn_ref, kn_ref, sink_ref, y_ref, k_ref, v_ref):
    u = u_ref[...]
    l = u.shape[0]
    q = _seg_rms(u[:, 0:D_ATT], qn_ref[...])
    k = _seg_rms(u[:, D_ATT:D_ATT + D_KV], kn_ref[:, 0:D_KV])
    v = u[:, D_ATT + D_KV:D_ATT + 2 * D_KV]
    k_ref[...] = k
    v_ref[...] = v
    s = _bdot_nt(_stack_q(q), k) * (HEAD_DIM ** -0.5)
    sink = _sink_col(sink_ref, li, l)
    m = jnp.maximum(jnp.max(s, axis=-1, keepdims=True), sink)
    p = jnp.exp(s - m)
    den = jnp.sum(p, axis=-1, keepdims=True) + jnp.exp(sink - m)
    y_ref[...] = _unstack_o(_bdot(p * (1.0 / den), v))


def _ctx_attention(u, wts, li, nb, l):
    kv_spec = pl.BlockSpec((None, None, l, D_KV), lambda b: (b, 0, 0, 0))
    y, k, v = pl.pallas_call(
        functools.partial(_ctx_attn_kernel, li),
        grid=(nb,),
        in_specs=[_seq_spec(l, ATT_IN), _layer_const(wts["attn_q_norm"], li), _layer_const(wts["attn_k_norm"], li),
                  pl.BlockSpec(memory_space=pltpu.SMEM)],
        out_specs=[_seq_spec(l, D_ATT), kv_spec, kv_spec],
        out_shape=[jax.ShapeDtypeStruct((nb, l, D_ATT), F32),
                   jax.ShapeDtypeStruct((nb, 1, l, D_KV), F32),
                   jax.ShapeDtypeStruct((nb, 1, l, D_KV), F32)],
        compiler_params=_cparams("parallel"),
        name="ctx_attention",
    )(u.reshape(nb, l, ATT_IN), wts["attn_q_norm"], wts["attn_k_norm"], wts["attn_sink"])
    return y.reshape(nb * l, D_ATT), k, v


@functools.lru_cache(maxsize=None)
def _rope_tables_host(l):
    n_rows = l // GRID_W
    rows = np.repeat(np.arange(n_rows, dtype=np.float32), GRID_W)
    cols = np.tile(np.arange(GRID_W, dtype=np.float32), n_rows)
    nf = HEAD_DIM // 4
    inv = (np.float32(ROPE_BASE) ** (-np.arange(nf, dtype=np.float32) / np.float32(nf))).astype(np.float32)
    ar = rows[:, None] * inv[None, :]
    ac = cols[:, None] * inv[None, :]
    cos = np.concatenate([np.cos(ar), np.cos(ar), np.cos(ac), np.cos(ac)], axis=-1)
    sin = np.concatenate([-np.sin(ar), np.sin(ar), -np.sin(ac), np.sin(ac)], axis=-1)
    return (np.tile(cos, (1, ATT_HEADS)).astype(np.float32), np.tile(sin, (1, ATT_HEADS)).astype(np.float32))


def _rope(x, cos, sin):
    w = x.shape[-1]
    nf = HEAD_DIM // 4
    lane = lax.broadcasted_iota(jnp.int32, x.shape, 1)
    first = (lane % (2 * nf)) < nf
    partner = jnp.where(first, pltpu.roll(x, w - nf, 1), pltpu.roll(x, nf, 1))
    return x * cos + partner * sin


def _lat_attn_kernel(nblk, li, u_ref, qn_ref, kn_ref, cos_ref, sin_ref, ck_ref, cv_ref, sink_ref,
                     y_ref, q_scr, k_scr, v_scr):
    blk = ATT_BLOCK
    l = nblk * blk
    cos_q = cos_ref[...]
    sin_q = sin_ref[...]
    u = u_ref[...]
    q = _seg_rms(u[:, 0:D_ATT], qn_ref[...])
    k = _seg_rms(u[:, D_ATT:D_ATT + D_KV], kn_ref[:, 0:D_KV])
    q_scr[...] = _rope(q, cos_q, sin_q)
    zeros = jnp.zeros((blk, D_KV), F32)
    k_scr[0:blk, :] = zeros
    k_scr[blk + l:2 * blk + l, :] = zeros
    v_scr[0:blk, :] = zeros
    v_scr[blk + l:2 * blk + l, :] = zeros
    k_scr[blk:blk + l, :] = _rope(k, cos_q[:, 0:D_KV], sin_q[:, 0:D_KV])
    v_scr[blk:blk + l, :] = u[:, D_ATT + D_KV:D_ATT + 2 * D_KV]

    scale = HEAD_DIM ** -0.5
    r = lax.broadcasted_iota(jnp.int32, (blk, 3 * blk), 0)
    cidx = lax.broadcasted_iota(jnp.int32, (blk, 3 * blk), 1)
    band = (cidx - r >= blk - WINDOW) & (cidx - r <= blk + WINDOW)
    sink = _sink_col(sink_ref, li, blk)

    def block(i, carry):
        rows = _rows(i)
        win = pl.ds(pl.multiple_of(i * blk, blk), 3 * blk)
        kpos = cidx + (i - 1) * blk
        valid = band & (kpos >= 0) & (kpos < l)
        valid = jnp.concatenate([valid] * ATT_HEADS, axis=0)
        qs = _stack_q(q_scr[rows, :])
        s_loc = jnp.where(valid, _bdot_nt(qs, k_scr[win, :]) * scale, -jnp.inf)
        s_ctx = _bdot_nt(qs, ck_ref[...]) * scale
        m = jnp.maximum(jnp.maximum(jnp.max(s_loc, axis=-1, keepdims=True),
                                    jnp.max(s_ctx, axis=-1, keepdims=True)), sink)
        p_loc = jnp.exp(s_loc - m)
        p_ctx = jnp.exp(s_ctx - m)
        den = (jnp.sum(p_loc, axis=-1, keepdims=True) + jnp.sum(p_ctx, axis=-1, keepdims=True)
               + jnp.exp(sink - m))
        inv = 1.0 / den
        y_ref[rows, :] = _unstack_o(_bdot(p_ctx * inv, cv_ref[...]) + _bdot(p_loc * inv, v_scr[win, :]))
        return carry

    lax.fori_loop(0, nblk, block, 0)


def _lat_attention(u, wts, li, ck, cv, nb, l):
    lc = ck.shape[2]
    nblk = l // ATT_BLOCK
    cos, sin = _rope_tables_host(l)
    cache_spec = pl.BlockSpec((None, None, lc, D_KV), lambda b: (b, li, 0, 0))
    y = pl.pallas_call(
        functools.partial(_lat_attn_kernel, nblk, li),
        grid=(nb,),
        in_specs=[_seq_spec(l, ATT_IN), _layer_const(wts["attn_q_norm"], li), _layer_const(wts["attn_k_norm"], li),
                  _const((l, D_ATT)), _const((l, D_ATT)), cache_spec, cache_spec,
                  pl.BlockSpec(memory_space=pltpu.SMEM)],
        out_specs=_seq_spec(l, D_ATT),
        out_shape=jax.ShapeDtypeStruct((nb, l, D_ATT), F32),
        scratch_shapes=[pltpu.VMEM((l, D_ATT), F32), pltpu.VMEM((l + 2 * ATT_BLOCK, D_KV), F32),
                        pltpu.VMEM((l + 2 * ATT_BLOCK, D_KV), F32)],
        compiler_params=_cparams("parallel"),
        name="lat_attention",
    )(u.reshape(nb, l, ATT_IN), wts["attn_q_norm"], wts["attn_k_norm"], jnp.asarray(cos), jnp.asarray(sin),
      ck, cv, wts["attn_sink"])
    return y.reshape(nb * l, D_ATT)


def _pack_mix_w_in(w):
    c0 = D_SSM + SSM_CONV_CH
    n_dt = 2 * SSM_HEADS
    zeros = jnp.zeros(w.shape[:-1] + (DT_PAD - DT_REP * n_dt,), w.dtype)
    dt_cols = [w[..., c0:c0 + n_dt]] * DT_REP
    return jnp.concatenate([w[..., :c0], w[..., c0 + n_dt:]] + dt_cols + [zeros], axis=-1).astype(BF16)


def _prep_weights(p):
    row = lambda a: a.reshape(DEPTH, 1, -1)
    pad_lanes = lambda a: jnp.pad(row(a), ((0, 0), (0, 0), (0, LANES - a[0].size)))
    return dict(
        norm_w=p["norm_w"].reshape(DEPTH * 3, 1, D_MODEL),
        ffn_w_in=p["ffn_w_in"].astype(BF16).reshape(DEPTH * 2, D_MODEL, 2 * D_FF),
        ffn_w_out=p["ffn_w_out"].astype(BF16).reshape(DEPTH * 2, D_FF, D_MODEL),
        mix_w_in=_pack_mix_w_in(p["mix_w_in"]), mix_w_out=p["mix_w_out"].astype(BF16),
        ssd_conv_w=p["ssd_conv_w"], ssd_conv_b=row(p["ssd_conv_b"]),
        ssd_dt_bias=pad_lanes(jnp.tile(row(p["ssd_dt_bias"]), (1, 1, DT_REP))),
        ssd_a_log=pad_lanes(jnp.tile(row(p["ssd_a_log"]), (1, 1, DT_REP))),
        ssd_d=row(jnp.repeat(p["ssd_d"], SSM_HEAD_DIM, axis=-1)), ssd_norm_w=row(p["ssd_norm_w"]),
        hy_conv_w=p["hy_conv_w"], hy_conv_b=row(p["hy_conv_b"]), hy_bias=p["hy_bias"],
        hy_w1=jnp.pad(p["hy_w1"], ((0, 0), (0, LANES - HY_EMB), (0, 0))), hy_b1=row(p["hy_b1"]),
        hy_w2=p["hy_w2"], hy_b2=row(p["hy_b2"]), hy_w3=p["hy_w3"], hy_freq=row(p["hy_freq"]),
        ret_decay_logit=pad_lanes(p["ret_decay_logit"]), ret_gn_w=row(p["ret_gn_w"]),
        attn_q_norm=row(jnp.tile(p["attn_q_norm"], (1, ATT_HEADS))),
        attn_k_norm=row(jnp.tile(p["attn_k_norm"], (1, ATT_HEADS))),
        attn_sink=p["attn_sink"].reshape(DEPTH * ATT_HEADS),
    )


def _layer(x, mod, row0, wts, li, nb, l, ssd_s0, ret_s0, ctx_kv, hy_tables, hy_spectra):
    rpm = nb * l if ctx_kv is None else l
    x = _ffn(x, mod, row0, rpm, wts, li, 0)
    z, xbc, hy, ret, att, dt = _inproj(x, mod, row0, rpm, wts, li)
    y_ssd, s_ssd = _ssd(z, xbc, dt, wts, li, ssd_s0, nb, l)
    y_hy = _hyena(hy, wts, li, hy_tables, hy_spectra, nb, l)
    y_ret, s_ret = _retention(ret, wts, li, ret_s0, nb, l)
    if ctx_kv is None:
        y_att, k, v = _ctx_attention(att, wts, li, nb, l)
    else:
        y_att = _lat_attention(att, wts, li, ctx_kv[0], ctx_kv[1], nb, l)
        k = v = None
    x = _outproj(x, mod, row0, rpm, (y_ssd, y_hy, y_ret, y_att), wts, li)
    x = _ffn(x, mod, row0, rpm, wts, li, 1)
    return x, s_ssd, s_ret, k, v


def kernel(x_prompt, x_sample, cache_k, cache_v, state_ssd, state_ret, c, c_ctx, w_mod, b_mod, norm_w, ffn_w_in, ffn_w_out, mix_w_in, mix_w_out, ssd_conv_w, ssd_conv_b, ssd_dt_bias, ssd_a_log, ssd_d, ssd_norm_w, hy_conv_w, hy_conv_b, hy_w1, hy_b1, hy_w2, hy_b2, hy_w3, hy_freq, hy_bias, ret_decay_logit, ret_gn_w, attn_q_norm, attn_k_norm, attn_sink):
    bp, lp_len, _ = x_prompt.shape
    bs, ls_len, _ = x_sample.shape
    lc = cache_k.shape[2]

    cond = jnp.concatenate([c_ctx[None, :], c, jnp.zeros((MOD_ROWS - 1 - bs, D_MODEL), F32)], axis=0)
    mod = _modulation(cond, w_mod, b_mod)
    wts = _prep_weights(dict(
        norm_w=norm_w, ffn_w_in=ffn_w_in, ffn_w_out=ffn_w_out, mix_w_in=mix_w_in, mix_w_out=mix_w_out,
        ssd_conv_w=ssd_conv_w, ssd_conv_b=ssd_conv_b, ssd_dt_bias=ssd_dt_bias, ssd_a_log=ssd_a_log, ssd_d=ssd_d,
        ssd_norm_w=ssd_norm_w, hy_conv_w=hy_conv_w, hy_conv_b=hy_conv_b, hy_w1=hy_w1, hy_b1=hy_b1, hy_w2=hy_w2,
        hy_b2=hy_b2, hy_w3=hy_w3, hy_freq=hy_freq, hy_bias=hy_bias, ret_decay_logit=ret_decay_logit,
        ret_gn_w=ret_gn_w, attn_q_norm=attn_q_norm, attn_k_norm=attn_k_norm, attn_sink=attn_sink))
    ck = cache_k.reshape(bs, DEPTH, lc, D_KV)
    cv = cache_v.reshape(bs, DEPTH, lc, D_KV)

    tab_p = _dft_tables(lp_len)
    tab_s = _dft_tables(ls_len)
    blk_p = _dft_block_tables(tab_p, lp_len)
    blk_s = _dft_block_tables(tab_s, ls_len)

    yp = x_prompt.reshape(bp * lp_len, D_MODEL)
    ys = x_sample.reshape(bs * ls_len, D_MODEL)
    new_k, new_v, new_ssd, new_ret = [], [], [], []
    for li in range(DEPTH):
        spec_p = _hy_filter(lp_len, tab_p[0], tab_p[1], wts, li)
        spec_s = _hy_filter(ls_len, tab_s[0], tab_s[1], wts, li)
        yp, s_ssd, s_ret, k_ctx, v_ctx = _layer(yp, mod, li * MOD_ROWS, wts, li, bp, lp_len, None, None, None,
                                                blk_p, spec_p)
        new_k.append(k_ctx)
        new_v.append(v_ctx)
        new_ssd.append(s_ssd)
        new_ret.append(s_ret)
        ys = _layer(ys, mod, li * MOD_ROWS + 1, wts, li, bs, ls_len, state_ssd, state_ret, (ck, cv),
                    blk_s, spec_s)[0]

    kv_shape = (bp, DEPTH, lp_len, ATT_KV_HEADS, HEAD_DIM)
    return (yp.reshape(bp, lp_len, D_MODEL), ys.reshape(bs, ls_len, D_MODEL),
            jnp.concatenate(new_k, axis=1).reshape(kv_shape), jnp.concatenate(new_v, axis=1).reshape(kv_shape),
            jnp.concatenate(new_ssd, axis=1), jnp.concatenate(new_ret, axis=1))
```

```python
import functools
import math

import numpy as np
import jax
import jax.numpy as jnp
from jax import lax
from jax.experimental import pallas as pl
from jax.experimental.pallas import tpu as pltpu

F32 = jnp.float32
BF16 = jnp.bfloat16
HI = lax.Precision.HIGHEST

D_MODEL = 1024
DEPTH = 2
GRID_W = 64
D_FF = 2816
N_MOD = 9
NORM_EPS = 1e-6
CHUNK = 128
D_SSM = 256
SSM_HEADS = 4
SSM_HEAD_DIM = 64
SSM_STATE = 128
SSM_GROUPS = 2
SSM_CONV_CH = D_SSM + 2 * SSM_GROUPS * SSM_STATE
D_HY = 256
HY_ORDER = 2
HY_BANDS = 16
HY_EMB = 1 + 2 * HY_BANDS
HY_HIDDEN = 64
HY_FAST_DECAY = 0.3
HY_SLOW_DECAY = 1.5
HY_TARGET = 1e-2
HY_IN = (HY_ORDER + 1) * D_HY
D_RET = 256
RET_HEADS = 4
RET_HEAD_DIM = 64
RET_IN = 4 * D_RET
ATT_HEADS = 4
ATT_KV_HEADS = 2
HEAD_DIM = 64
D_ATT = ATT_HEADS * HEAD_DIM
D_KV = ATT_KV_HEADS * HEAD_DIM
ATT_IN = D_ATT + 2 * D_KV
WINDOW = 128
ATT_BLOCK = 128
ROPE_BASE = 10000.0
D_MIX = D_SSM + D_HY + D_RET + D_ATT

LANES = 128
DT_PAD = LANES
D_IN_PAD = D_SSM + SSM_CONV_CH + HY_IN + RET_IN + ATT_IN + DT_PAD
VMEM_LIMIT = 56 * 1024 * 1024
MOD_ROWS = 8

TOKEN_TILE = 512
DFT_BLOCK = 256


def _cparams(*sem):
    return pltpu.CompilerParams(dimension_semantics=sem, vmem_limit_bytes=VMEM_LIMIT)


def _rms(x, w):
    return x * lax.rsqrt(jnp.mean(x * x, axis=-1, keepdims=True) + NORM_EPS) * w


def _silu(x):
    return x * (1.0 / (1.0 + jnp.exp(-x)))


def _softplus(x):
    return jnp.maximum(x, 0.0) + jnp.log1p(jnp.exp(-jnp.abs(x)))


def _bdot(a, b):
    return jnp.dot(a.astype(BF16), b.astype(BF16), preferred_element_type=F32)


def _bdot_nt(a, b):
    return lax.dot_general(a.astype(BF16), b.astype(BF16), (((1,), (1,)), ((), ())),
                           preferred_element_type=F32)


def _bdot_tn(a, b):
    return lax.dot_general(a.astype(BF16), b.astype(BF16), (((0,), (0,)), ((), ())),
                           preferred_element_type=F32)


def _hdot(a, b):
    return jnp.dot(a, b, preferred_element_type=F32, precision=HI)


def _full(shape):
    n = len(shape)
    return pl.BlockSpec(shape, lambda *_: (0,) * n)


def _const(shape):
    n = len(shape)
    return pl.BlockSpec(shape, lambda *_: (0,) * n, pipeline_mode=pl.Buffered(1))


def _layer_const(arr, li):
    tail = arr.shape[1:]
    zeros = (0,) * len(tail)
    return pl.BlockSpec((None,) + tail, lambda *_: (li,) + zeros, pipeline_mode=pl.Buffered(1))


def _mod_spec(k, tm, rows_per_mod, row0):
    return pl.BlockSpec((None, 1, D_MODEL), lambda i: (row0 + (i * tm) // rows_per_mod, 0, k))


def _seq_spec(l, w):
    return pl.BlockSpec((None, l, w), lambda b: (b, 0, 0))


def _head_masks(width, heads):
    lane = lax.broadcasted_iota(jnp.int32, (1, width), 1)
    hd = width // heads
    return [(lane >= h * hd) & (lane < (h + 1) * hd) for h in range(heads)]


def _by_head(masks, vals):
    out = vals[-1]
    for m, v in zip(masks[-2::-1], vals[-2::-1]):
        out = jnp.where(m, v, out)
    return out


def _block_diag(n, blk, value):
    i = lax.broadcasted_iota(jnp.int32, (n, n), 0) // blk
    j = lax.broadcasted_iota(jnp.int32, (n, n), 1) // blk
    return jnp.where(i == j, value, 0.0).astype(F32)


def _mod_kernel(c_ref, w_ref, b_ref, o_ref):
    c = c_ref[...]
    o_ref[...] = _bdot(_silu(c), w_ref[...]) + b_ref[...]


def _modulation(cond, w_mod, b_mod):
    out = pl.pallas_call(
        _mod_kernel,
        grid=(DEPTH, N_MOD),
        in_specs=[pl.BlockSpec((MOD_ROWS, D_MODEL), lambda l, j: (0, 0)),
                  pl.BlockSpec((None, D_MODEL, D_MODEL), lambda l, j: (l, 0, j)),
                  pl.BlockSpec((None, 1, D_MODEL), lambda l, j: (l, 0, j))],
        out_specs=pl.BlockSpec((None, MOD_ROWS, D_MODEL), lambda l, j: (l, 0, j)),
        out_shape=jax.ShapeDtypeStruct((DEPTH, MOD_ROWS, N_MOD * D_MODEL), F32),
        compiler_params=_cparams("arbitrary", "arbitrary"),
        name="modulation",
    )(cond, w_mod, b_mod.reshape(DEPTH, 1, N_MOD * D_MODEL))
    return out.reshape(DEPTH * MOD_ROWS, 1, N_MOD * D_MODEL)


def _ffn_kernel(n_mix, x_ref, sh_ref, sc_ref, g_ref, nw_ref, wi_ref, wo_ref, *rest):
    o_ref = rest[-1]
    x = x_ref[...]
    if n_mix:
        gm_ref, wm_ref = rest[0], rest[1 + n_mix]
        w = D_MIX // n_mix
        acc = _bdot(rest[1][...], wm_ref[0:w, :])
        for j in range(1, n_mix):
            acc += _bdot(rest[1 + j][...], wm_ref[j * w:(j + 1) * w, :])
        x = x + gm_ref[...] * acc
    h = (_rms(x, nw_ref[...]) * (1.0 + sc_ref[...]) + sh_ref[...]).astype(BF16)
    gate = jnp.dot(h, wi_ref[:, 0:D_FF], preferred_element_type=F32)
    up = jnp.dot(h, wi_ref[:, D_FF:2 * D_FF], preferred_element_type=F32)
    o_ref[...] = x + 0.5 * g_ref[...] * _bdot(_silu(gate) * up, wo_ref[...])


def _ffn(x, mod, row0, rows_per_mod, wts, li, k, mix=()):
    t = x.shape[0]
    tm = min(TOKEN_TILE, rows_per_mod)
    w_in, w_out = wts["ffn_w_in"], wts["ffn_w_out"]
    args = [x, mod, mod, mod, wts["norm_w"], w_in, w_out]
    in_specs = [pl.BlockSpec((tm, D_MODEL), lambda i: (i, 0)),
                _mod_spec(6 * k, tm, rows_per_mod, row0),
                _mod_spec(6 * k + 1, tm, rows_per_mod, row0),
                _mod_spec(6 * k + 2, tm, rows_per_mod, row0),
                _layer_const(wts["norm_w"], 3 * li + 2 * k),
                _layer_const(w_in, 2 * li + k), _layer_const(w_out, 2 * li + k)]
    if mix:
        args += [mod, *mix, wts["mix_w_out"]]
        in_specs += [_mod_spec(5, tm, rows_per_mod, row0)]
        in_specs += [pl.BlockSpec((tm, y.shape[1]), lambda i: (i, 0)) for y in mix]
        in_specs += [_layer_const(wts["mix_w_out"], li)]
    return pl.pallas_call(
        functools.partial(_ffn_kernel, len(mix)),
        grid=(t // tm,),
        in_specs=in_specs,
        out_specs=pl.BlockSpec((tm, D_MODEL), lambda i: (i, 0)),
        out_shape=jax.ShapeDtypeStruct((t, D_MODEL), F32),
        compiler_params=_cparams("parallel"),
        name="ffn",
    )(*args)


_IN_SPLITS = (("z", D_SSM), ("xbc", SSM_CONV_CH), ("hy", HY_IN), ("ret", RET_IN), ("att", ATT_IN), ("dt", DT_PAD))


def _inproj_kernel(x_ref, sh_ref, sc_ref, nw_ref, w_ref, *o_refs):
    h = (_rms(x_ref[...], nw_ref[...]) * (1.0 + sc_ref[...]) + sh_ref[...]).astype(BF16)
    off = 0
    for (_, width), o_ref in zip(_IN_SPLITS, o_refs):
        o_ref[...] = jnp.dot(h, w_ref[:, off:off + width], preferred_element_type=F32)
        off += width


def _inproj(x, mod, row0, rows_per_mod, wts, li):
    t = x.shape[0]
    tm = min(TOKEN_TILE, rows_per_mod)
    return pl.pallas_call(
        _inproj_kernel,
        grid=(t // tm,),
        in_specs=[pl.BlockSpec((tm, D_MODEL), lambda i: (i, 0)),
                  _mod_spec(3, tm, rows_per_mod, row0),
                  _mod_spec(4, tm, rows_per_mod, row0),
                  _layer_const(wts["norm_w"], 3 * li + 1),
                  _layer_const(wts["mix_w_in"], li)],
        out_specs=[pl.BlockSpec((tm, width), lambda i: (i, 0)) for _, width in _IN_SPLITS],
        out_shape=[jax.ShapeDtypeStruct((t, width), F32) for _, width in _IN_SPLITS],
        compiler_params=_cparams("parallel"),
        name="mix_in",
    )(x, mod, mod, wts["norm_w"], wts["mix_w_in"])


def _conv3_chunk(x_ref, c, nc, w, b):
    q = CHUNK
    l = nc * q
    r0 = pl.multiple_of(c * q, q)
    x = x_ref[pl.ds(r0, q), :]
    prev = x_ref[pl.ds(jnp.maximum(r0 - 1, 0), 1), :]
    nxt = x_ref[pl.ds(jnp.minimum(r0 + q, l - 1), 1), :]
    prev = jnp.where(c > 0, prev, 0.0)
    nxt = jnp.where(c < nc - 1, nxt, 0.0)
    rid = lax.broadcasted_iota(jnp.int32, (q, 1), 0)
    xm1 = jnp.where(rid == 0, prev, pltpu.roll(x, 1, 0))
    xp1 = jnp.where(rid == q - 1, nxt, pltpu.roll(x, q - 1, 0))
    return xm1 * w[0:1, :] + x * w[1:2, :] + xp1 * w[2:3, :] + b


def _tri(lower):
    i = lax.broadcasted_iota(jnp.int32, (CHUNK, CHUNK), 0)
    j = lax.broadcasted_iota(jnp.int32, (CHUNK, CHUNK), 1)
    return (j <= i) if lower else (j >= i)


def _rows(c):
    return pl.ds(pl.multiple_of(c * CHUNK, CHUNK), CHUNK)


def _split_cat(v, parts, axis):
    out, r = [], v
    for i in range(parts):
        piece = r.astype(BF16)
        out.append(piece)
        if i + 1 < parts:
            r = r - piece.astype(F32)
    return jnp.concatenate(out, axis=axis)


def _seg_mean(x, seg):
    w = x.shape[-1]
    ones = _block_diag(w, seg, 1.0).astype(BF16)
    return jnp.dot(_split_cat(x, 2, axis=1), jnp.concatenate([ones, ones], axis=0),
                   preferred_element_type=F32) * (1.0 / seg)


def _stack_heads(x, masks):
    return jnp.concatenate([jnp.where(m, x, 0.0) for m in masks], axis=0)


def _slot_out(tail, nb, li, stacked):
    zeros = (0,) * len(tail)
    slot = li if stacked else 0
    spec = pl.BlockSpec((None, None) + tail, lambda b: (b, slot) + zeros)
    return spec, jax.ShapeDtypeStruct((nb, DEPTH if stacked else 1) + tail, F32)


def _carry_prev(args, in_specs, prevs, first_out):
    aliases = {}
    for j, prev in enumerate(prevs):
        aliases[len(args)] = first_out + j
        args.append(prev)
        in_specs.append(pl.BlockSpec(memory_space=pl.ANY))
    return aliases


SSD_SEL_W = 2 * SSM_HEADS * SSM_STATE
CUM_PIECES = 3
WGT_PIECES = 2
DT_REP = CUM_PIECES + WGT_PIECES


def _ssd_kernel(nc, has_s0, *refs):
    refs = list(refs)
    (z_ref, xbc_ref, dt_ref, cw_ref, cb_ref, dtb_ref, alog_ref, dsk_ref, nw_ref) = refs[:9]
    s0_ref = refs[9] if has_s0 else None
    (y_ref, s_ref, xs_scr, xk_scr, xt_scr, dt_scr, cum_scr, dtt_scr, cumt_scr, yf_scr, yb_scr, st_scr,
     msk_scr, sel_scr) = refs[-14:]
    q = CHUNK
    nh, n, p = SSM_HEADS, SSM_STATE, SSM_HEAD_DIM
    rep = nh // SSM_GROUPS
    hm = _head_masks(D_SSM, nh)

    @pl.when(pl.program_id(0) == 0)
    def _():
        i = lax.broadcasted_iota(jnp.int32, (D_SSM, nh * n), 0) // p
        j = lax.broadcasted_iota(jnp.int32, (D_SSM, nh * n), 1) // n
        msk_scr[...] = jnp.where(i == j, 1.0, 0.0).astype(F32)
        k = lax.broadcasted_iota(jnp.int32, (LANES, SSD_SEL_W), 0)
        col = lax.broadcasted_iota(jnp.int32, (LANES, SSD_SEL_W), 1)
        is_cum = col < nh * n
        grp = k // (2 * nh)
        grp_ok = (is_cum & (grp < CUM_PIECES)) | ((~is_cum) & (grp >= CUM_PIECES) & (grp < DT_REP))
        for d in range(2):
            hit = grp_ok & (k % (2 * nh) == d * nh + (col % (nh * n)) // n)
            sel_scr[d] = jnp.where(hit, 1.0, 0.0).astype(BF16)

    cw = cw_ref[...]
    cb = cb_ref[...]
    neg_a = -jnp.exp(alog_ref[...])
    dtb = dtb_ref[...]

    def prep(c, carry):
        rows = _rows(c)
        xall = _silu(_conv3_chunk(xbc_ref, c, nc, cw, cb))
        xs_scr[rows, :] = xall
        xs = xall[:, 0:D_SSM]
        xk_scr[c] = _stack_heads(xs, hm).astype(BF16)
        xt_scr[c] = xs.T.astype(BF16)
        dt = _softplus(dt_ref[rows, :] + dtb)
        la = _split_cat(dt * neg_a, 3, axis=0)
        tri = jnp.concatenate([_tri(True), _tri(False)], axis=0)
        tri = jnp.where(tri, 1.0, 0.0).astype(BF16)
        cs = jnp.dot(jnp.concatenate([tri] * 3, axis=1), la, preferred_element_type=F32)
        lane = lax.broadcasted_iota(jnp.int32, (1, LANES), 1)
        cum = jnp.where(lane % (2 * nh) < nh, cs[0:q], cs[q:2 * q])
        dt_scr[rows, :] = dt
        cum_scr[rows, :] = cum
        dtt_scr[c] = dt.T
        cumt_scr[c] = cum.T
        return carry

    lax.fori_loop(0, nc, prep, 0)

    for d in range(2):
        for h in range(nh):
            blk = slice(h * p, (h + 1) * p)
            if has_s0:
                parts = [jnp.zeros((p, n), F32)] * nh
                parts[h] = s0_ref[d, h].T
                st_scr[d, blk, :] = jnp.concatenate(parts, axis=1)
            else:
                st_scr[d, blk, :] = jnp.zeros((p, nh * n), F32)

    def scan_chunk(c, d):
        rows = _rows(c)
        bm = [xs_scr[rows, D_SSM + g * n:D_SSM + (g + 1) * n] for g in range(SSM_GROUPS)]
        cm = [xs_scr[rows, D_SSM + (SSM_GROUPS + g) * n:D_SSM + (SSM_GROUPS + g + 1) * n]
              for g in range(SSM_GROUPS)]
        dt = dt_scr[rows, :]
        cum = cum_scr[rows, :]
        mask = _tri(d == 0)
        edge = q - 1 if d == 0 else 0
        lane = lax.broadcasted_iota(jnp.int32, (1, LANES), 1)
        used = (lane < DT_REP * 2 * nh) & ((lane % (2 * nh)) // nh == d)
        cum = jnp.where(used, cum, 0.0)
        wgt = jnp.exp(cum[edge:edge + 1, :] - cum) * dt
        grp = lane // (2 * nh)
        packed = jnp.zeros_like(cum)
        for src, pieces, g0 in ((cum, CUM_PIECES, 0), (wgt, WGT_PIECES, CUM_PIECES)):
            rest = src
            for i in range(pieces):
                piece = rest.astype(BF16).astype(F32)
                packed = jnp.where(grp == g0 + i, piece, packed)
                rest = rest - piece
        packed = jnp.where(used, packed, 0.0).astype(BF16)
        e = jnp.dot(packed, sel_scr[d], preferred_element_type=F32)
        cb_t = [_bdot_nt(cm[g], bm[g]) for g in range(SSM_GROUPS)]
        blocks, ecol = [], []
        for h in range(nh):
            r = d * nh + h
            col = e[:, h * n:(h + 1) * n]
            decay = jnp.exp(jnp.where(mask, col - cumt_scr[c, r:r + 1, :], -jnp.inf))
            blocks.append((cb_t[h // rep] * decay * dtt_scr[c, r:r + 1, :]).astype(BF16))
            ecol.append(jnp.exp(col))
        y = jnp.dot(jnp.concatenate(blocks, axis=1), xk_scr[c], preferred_element_type=F32)
        s_old = st_scr[d]
        c4 = jnp.concatenate([cm[h // rep] for h in range(nh)], axis=1)
        lo = lax.broadcasted_iota(jnp.int32, (1, n), 1) < p
        ecol = jnp.concatenate([jnp.where(lo, ecol[2 * j], ecol[2 * j + 1]) for j in range(nh // 2)], axis=1)
        y = y + ecol * _bdot_nt(c4, s_old)
        wk = jnp.concatenate([bm[h // rep] for h in range(nh)], axis=1) * e[:, nh * n:2 * nh * n]
        upd = jnp.dot(xt_scr[c], wk.astype(BF16), preferred_element_type=F32)
        st_scr[d] = jnp.exp(e[edge:edge + 1, 0:nh * n]) * s_old + upd * msk_scr[...]
        return y

    def scan(i, carry):
        yf_scr[_rows(i), :] = scan_chunk(i, 0)
        yb_scr[_rows(nc - 1 - i), :] = scan_chunk(nc - 1 - i, 1)
        return carry

    lax.fori_loop(0, nc, scan, 0)

    for d in range(2):
        for h in range(nh):
            s_ref[d, h] = st_scr[d, h * p:(h + 1) * p, h * n:(h + 1) * n].T

    dsk = dsk_ref[...]
    nw = nw_ref[...]

    def fin(c, carry):
        rows = _rows(c)
        y = yf_scr[rows, :] + yb_scr[rows, :] + dsk * xs_scr[rows, 0:D_SSM]
        y_ref[rows, :] = _rms(y * _silu(z_ref[rows, :]), nw)
        return carry

    lax.fori_loop(0, nc, fin, 0)


def _ssd(z, xbc, dt, wts, li, s0, nb, l, stacked=False, prev=None):
    nc = l // CHUNK
    has_s0 = s0 is not None
    names = ("ssd_conv_w", "ssd_conv_b", "ssd_dt_bias", "ssd_a_log", "ssd_d", "ssd_norm_w")
    args = [z.reshape(nb, l, D_SSM), xbc.reshape(nb, l, SSM_CONV_CH), dt.reshape(nb, l, DT_PAD)]
    args += [wts[n] for n in names]
    in_specs = [_seq_spec(l, D_SSM), _seq_spec(l, SSM_CONV_CH), _seq_spec(l, DT_PAD)]
    in_specs += [_layer_const(wts[n], li) for n in names]
    if has_s0:
        args.append(s0)
        in_specs.append(pl.BlockSpec((None, None, 2, SSM_HEADS, SSM_STATE, SSM_HEAD_DIM),
                                     lambda b: (b, li, 0, 0, 0, 0)))
    st_spec, st_shape = _slot_out((2, SSM_HEADS, SSM_STATE, SSM_HEAD_DIM), nb, li, stacked)
    aliases = _carry_prev(args, in_specs, [] if prev is None else [prev], 1)
    y, s = pl.pallas_call(
        functools.partial(_ssd_kernel, nc, has_s0),
        grid=(nb,),
        in_specs=in_specs,
        out_specs=[_seq_spec(l, D_SSM), st_spec],
        out_shape=[jax.ShapeDtypeStruct((nb, l, D_SSM), F32), st_shape],
        input_output_aliases=aliases,
        scratch_shapes=[pltpu.VMEM((l, SSM_CONV_CH), F32),
                        pltpu.VMEM((nc, SSM_HEADS * CHUNK, D_SSM), BF16), pltpu.VMEM((nc, D_SSM, CHUNK), BF16),
                        pltpu.VMEM((l, DT_PAD), F32), pltpu.VMEM((l, DT_PAD), F32),
                        pltpu.VMEM((nc, DT_PAD, CHUNK), F32), pltpu.VMEM((nc, DT_PAD, CHUNK), F32),
                        pltpu.VMEM((l, D_SSM), F32), pltpu.VMEM((l, D_SSM), F32),
                        pltpu.VMEM((2, D_SSM, SSM_HEADS * SSM_STATE), F32),
                        pltpu.VMEM((D_SSM, SSM_HEADS * SSM_STATE), F32),
                        pltpu.VMEM((2, LANES, SSD_SEL_W), BF16)],
        compiler_params=_cparams("arbitrary"),
        name="ssd",
    )(*args)
    return y.reshape(nb * l, D_SSM), s


def _ret_kernel(nc, has_s0, *refs):
    refs = list(refs)
    u_ref, dl_ref, gn_ref = refs[:3]
    s0_ref = refs[3] if has_s0 else None
    y_ref, s_ref, yf_scr, yb_scr, st_scr, dm_scr, e_scr, ea_scr, bd_scr = refs[-9:]
    q = CHUNK
    hd = RET_HEAD_DIM
    hm = _head_masks(D_RET, RET_HEADS)

    @pl.when(pl.program_id(0) == 0)
    def _():
        log_g = -_softplus(-dl_ref[...])
        ii = lax.broadcasted_iota(jnp.int32, (q, q), 0)
        jj = lax.broadcasted_iota(jnp.int32, (q, q), 1)
        dij = (ii - jj).astype(F32)
        ri = lax.broadcasted_iota(jnp.int32, (q, 1), 0).astype(F32)
        lfs, lbs = [], []
        for h in range(RET_HEADS):
            lf = log_g[:, h:h + 1]
            lb = log_g[:, RET_HEADS + h:RET_HEADS + h + 1]
            lfs.append(lf)
            lbs.append(lb)
            d_f = jnp.exp(jnp.where(dij >= 0, dij * lf, -jnp.inf))
            d_b = jnp.exp(jnp.where(dij <= 0, -dij * lb, -jnp.inf))
            dm_scr[:, h * q:(h + 1) * q] = d_f + d_b
        lf_l = _by_head(hm, lfs)
        lb_l = _by_head(hm, lbs)
        e_scr[0] = jnp.exp((ri + 1.0) * lf_l)
        e_scr[1] = jnp.exp((q - ri) * lb_l)
        e_scr[2] = jnp.exp((q - 1.0 - ri) * lf_l)
        e_scr[3] = jnp.exp(ri * lb_l)
        ea_scr[0:1, :] = jnp.exp(q * lf_l)
        ea_scr[1:2, :] = jnp.exp(q * lb_l)
        bd_scr[...] = _block_diag(D_RET, hd, 1.0)

    for d in range(2):
        for h in range(RET_HEADS):
            blk = slice(h * hd, (h + 1) * hd)
            if has_s0:
                parts = [jnp.zeros((hd, hd), F32)] * RET_HEADS
                parts[h] = s0_ref[d, h]
                st_scr[d, blk, :] = jnp.concatenate(parts, axis=1)
            else:
                st_scr[d, blk, :] = jnp.zeros((hd, D_RET), F32)

    def qkv(rows):
        return (u_ref[rows, 0:D_RET], u_ref[rows, D_RET:2 * D_RET] * (RET_HEAD_DIM ** -0.5),
                u_ref[rows, 2 * D_RET:3 * D_RET])

    def state_step(d, qq, kk, vv):
        s_old = st_scr[d]
        y = e_scr[d] * _bdot(qq, s_old)
        st_scr[d] = ea_scr[d:d + 1, :] * s_old + _bdot_tn(kk * e_scr[2 + d], vv) * bd_scr[...]
        return y

    def scan(i, carry):
        rows = _rows(i)
        qq, kk, vv = qkv(rows)
        sc = _bdot_nt(qq, _stack_heads(kk, hm)) * dm_scr[...]
        yf_scr[rows, :] = _bdot(sc, _stack_heads(vv, hm)) + state_step(0, qq, kk, vv)
        rows_b = _rows(nc - 1 - i)
        qq, kk, vv = qkv(rows_b)
        yb_scr[rows_b, :] = state_step(1, qq, kk, vv)
        return carry

    lax.fori_loop(0, nc, scan, 0)

    for d in range(2):
        for h in range(RET_HEADS):
            s_ref[d, h] = st_scr[d, h * hd:(h + 1) * hd, h * hd:(h + 1) * hd]

    gn = gn_ref[...]

    def fin(c, carry):
        rows = _rows(c)
        y = yf_scr[rows, :] + yb_scr[rows, :]
        cen = y - _seg_mean(y, hd)
        var = _seg_mean(cen * cen, hd)
        y_ref[rows, :] = cen * lax.rsqrt(var + NORM_EPS) * gn * _silu(u_ref[rows, 3 * D_RET:4 * D_RET])
        return carry

    lax.fori_loop(0, nc, fin, 0)


def _retention(u, wts, li, s0, nb, l, stacked=False, prev=None):
    nc = l // CHUNK
    has_s0 = s0 is not None
    names = ("ret_decay_logit", "ret_gn_w")
    args = [u.reshape(nb, l, RET_IN)] + [wts[n] for n in names]
    in_specs = [_seq_spec(l, RET_IN)] + [_layer_const(wts[n], li) for n in names]
    if has_s0:
        args.append(s0)
        in_specs.append(pl.BlockSpec((None, None, 2, RET_HEADS, RET_HEAD_DIM, RET_HEAD_DIM),
                                     lambda b: (b, li, 0, 0, 0, 0)))
    st_spec, st_shape = _slot_out((2, RET_HEADS, RET_HEAD_DIM, RET_HEAD_DIM), nb, li, stacked)
    aliases = _carry_prev(args, in_specs, [] if prev is None else [prev], 1)
    y, s = pl.pallas_call(
        functools.partial(_ret_kernel, nc, has_s0),
        grid=(nb,),
        in_specs=in_specs,
        out_specs=[_seq_spec(l, D_RET), st_spec],
        out_shape=[jax.ShapeDtypeStruct((nb, l, D_RET), F32), st_shape],
        input_output_aliases=aliases,
        scratch_shapes=[pltpu.VMEM((l, D_RET), F32), pltpu.VMEM((l, D_RET), F32),
                        pltpu.VMEM((2, D_RET, D_RET), F32), pltpu.VMEM((CHUNK, RET_HEADS * CHUNK), F32),
                        pltpu.VMEM((4, CHUNK, D_RET), F32), pltpu.VMEM((8, D_RET), F32),
                        pltpu.VMEM((D_RET, D_RET), F32)],
        compiler_params=_cparams("arbitrary"),
        name="retention",
    )(*args)
    return y.reshape(nb * l, D_RET), s


def _split(x):
    hi = x.astype(BF16)
    return hi, (x - hi.astype(F32)).astype(BF16)


def _dot3(a_hi, a_lo, b_hi, b_lo):
    d = lambda p, q: jnp.dot(p, q, preferred_element_type=F32)
    return d(a_hi, b_hi) + (d(a_lo, b_hi) + d(a_hi, b_lo))


@functools.lru_cache(maxsize=None)
def _dft_fwd_host(l):
    n = 2 * l
    f = np.arange(l, dtype=np.int64)[:, None]
    s = np.arange(l, dtype=np.int64)[None, :]
    ang = ((f * s) % n).astype(np.float64) * (2.0 * math.pi / n)
    im = -np.sin(ang)
    im[0] = np.where(np.arange(l) % 2 == 0, 1.0, -1.0)
    return np.concatenate([np.cos(ang), im], axis=0).astype(np.float32)


def _dft_tables(l):
    n = 2 * l
    fwd = jnp.asarray(_dft_fwd_host(l))
    wgt = np.full((n, 1), 2.0 / n, np.float32)
    wgt[0] = wgt[l] = 1.0 / n
    return _split(fwd) + _split((fwd * wgt).T)


def _dft_block_tables(tables, l):
    fwd_hi, fwd_lo, inv_hi, inv_lo = tables
    r = DFT_BLOCK
    nblk = l // r
    fb = lambda t: (t[0:l].reshape(nblk, r, l), t[l:2 * l].reshape(nblk, r, l))
    fwd_blk = jnp.concatenate(fb(fwd_hi) + fb(fwd_lo), axis=1)
    inv_blk = jnp.concatenate([inv_hi.reshape(nblk, r, 2 * l), inv_lo.reshape(nblk, r, 2 * l)], axis=1)
    return fwd_blk, inv_blk


def _hy_filter_kernel(l, feats_ref, dec_ref, w1_ref, b1_ref, w2_ref, b2_ref, w3_ref, fr_ref, fh_ref, fl_ref,
                      a_ref, b_ref, d_ref):
    fr = fr_ref[...]
    h = jnp.sin(fr * (_hdot(feats_ref[...], w1_ref[...]) + b1_ref[...]))
    h = jnp.sin(fr * (_hdot(h, w2_ref[...]) + b2_ref[...]))
    h = _hdot(h, w3_ref[...])
    dec = jnp.concatenate([dec_ref[...]] * HY_ORDER, axis=-1)
    row0 = lax.broadcasted_iota(jnp.int32, (l, 1), 0) == 0
    hf = h[:, 0:HY_ORDER * D_HY] * dec
    hb = h[:, HY_ORDER * D_HY:2 * HY_ORDER * D_HY] * dec
    hb = jnp.where(row0, 0.0, hb)
    hs = _split(hf + hb)
    hd = _split(hf - hb)
    re = _dot3(fh_ref[0:l, :], fl_ref[0:l, :], *hs)
    ny = _dot3(fh_ref[l:l + 8, :], fl_ref[l:l + 8, :], *hs)[0:1]
    im = _dot3(fh_ref[l:2 * l, :], fl_ref[l:2 * l, :], *hd)
    for o in range(HY_ORDER):
        cols = slice(o * D_HY, (o + 1) * D_HY)
        a_ref[o] = re[:, cols]
        b_ref[o] = jnp.where(row0, 0.0, im[:, cols])
        d_ref[o] = jnp.where(row0, ny[:, cols], re[:, cols])


def _hy_filter(l, fwd_hi, fwd_lo, wts, li):
    pos = np.arange(l, dtype=np.float32)
    t = pos / np.float32(l - 1)
    bands = np.linspace(1e-4, HY_BANDS - 1, HY_BANDS, dtype=np.float32)
    ang = np.float32(2.0 * math.pi / l) * pos[:, None] * bands[None, :]
    feats = np.concatenate([t[:, None], np.cos(ang), -np.sin(ang)], axis=-1).astype(np.float32)
    feats = np.pad(feats, ((0, 0), (0, LANES - HY_EMB)))
    max_decay = math.log(HY_TARGET) / HY_FAST_DECAY
    min_decay = math.log(HY_TARGET) / HY_SLOW_DECAY
    deltas = np.abs(np.linspace(min_decay, max_decay, D_HY, dtype=np.float32))
    dec = np.exp(-t[:, None] * deltas[None, :]).astype(np.float32)
    spec = jax.ShapeDtypeStruct((HY_ORDER, l, D_HY), F32)
    names = ("hy_w1", "hy_b1", "hy_w2", "hy_b2", "hy_w3", "hy_freq")
    return pl.pallas_call(
        functools.partial(_hy_filter_kernel, l),
        grid=(1,),
        in_specs=[_full(feats.shape), _full(dec.shape)] + [_layer_const(wts[n], li) for n in names]
                 + [_full(fwd_hi.shape), _full(fwd_lo.shape)],
        out_specs=[_full(spec.shape)] * 3,
        out_shape=[spec] * 3,
        compiler_params=_cparams("arbitrary"),
        name="hyena_filter",
    )(jnp.asarray(feats), jnp.asarray(dec), *[wts[n] for n in names], fwd_hi, fwd_lo)


def _hy_kernel(nc, u_ref, cw_ref, cb_ref, f_ref, g_ref, a_ref, b_ref, d_ref, bias_ref, y_ref,
               uc_scr, vh_scr, vl_scr, sh_scr, sl_scr, z_scr):
    l = nc * CHUNK
    r = DFT_BLOCK
    nblk = l // r
    cw = cw_ref[...]
    cb = cb_ref[...]
    dot = lambda p, q: jnp.dot(p, q, preferred_element_type=F32)

    def for_blocks(body):
        if nblk == 1:
            body(0)
        else:
            lax.fori_loop(0, nblk, lambda i, carry: (body(i), carry)[1], 0)

    def conv(c, carry):
        rows = _rows(c)
        uc = _conv3_chunk(u_ref, c, nc, cw, cb)
        uc_scr[rows, :] = uc
        vh_scr[rows, :], vl_scr[rows, :] = _split(uc[:, 0:D_HY])
        return carry

    lax.fori_loop(0, nc, conv, 0)

    def long_conv(o, src_ref, src_col, gate_col, dst_ref, split_dst):
        def spectrum(i):
            rows = pl.ds(pl.multiple_of(i * r, r), r)
            rows_im = pl.ds(pl.multiple_of(l + i * r, r), r)
            p = dot(f_ref[i], vh_scr[...])
            pl_ = dot(f_ref[i, 0:2 * r, :], vl_scr[...])
            zr = p[0:r] + (p[2 * r:3 * r] + pl_[0:r])
            zi = p[r:2 * r] + (p[3 * r:4 * r] + pl_[r:2 * r])
            sh_scr[rows, :], sl_scr[rows, :] = _split(zr * a_ref[o, rows, :] - zi * b_ref[o, rows, :])
            sh_scr[rows_im, :], sl_scr[rows_im, :] = _split(zr * b_ref[o, rows, :] + zi * d_ref[o, rows, :])

        for_blocks(spectrum)

        def inverse(i):
            rows = pl.ds(pl.multiple_of(i * r, r), r)
            p = dot(g_ref[i], sh_scr[...])
            y = p[0:r] + (p[r:2 * r] + dot(g_ref[i, 0:r, :], sl_scr[...]))
            v = src_ref[rows, src_col:src_col + D_HY]
            gate = uc_scr[rows, gate_col:gate_col + D_HY]
            out = gate * (y + v * bias_ref[o:o + 1, :])
            dst_ref[rows, :] = out
            if split_dst:
                vh_scr[rows, :], vl_scr[rows, :] = _split(out)

        for_blocks(inverse)

    long_conv(0, uc_scr, 0, D_HY, z_scr, True)
    long_conv(1, z_scr, 0, 2 * D_HY, y_ref, False)


def _hyena(u, wts, li, tables, spectra, nb, l):
    nc = l // CHUNK
    a, b, d = spectra
    names = ("hy_conv_w", "hy_conv_b")
    y = pl.pallas_call(
        functools.partial(_hy_kernel, nc),
        grid=(nb,),
        in_specs=[_seq_spec(l, HY_IN)] + [_layer_const(wts[n], li) for n in names]
                 + [_const(tables[0].shape), _const(tables[1].shape),
                    _const((HY_ORDER, l, D_HY)), _const((HY_ORDER, l, D_HY)), _const((HY_ORDER, l, D_HY)),
                    _layer_const(wts["hy_bias"], li)],
        out_specs=_seq_spec(l, D_HY),
        out_shape=jax.ShapeDtypeStruct((nb, l, D_HY), F32),
        scratch_shapes=[pltpu.VMEM((l, HY_IN), F32), pltpu.VMEM((l, D_HY), BF16), pltpu.VMEM((l, D_HY), BF16),
                        pltpu.VMEM((2 * l, D_HY), BF16), pltpu.VMEM((2 * l, D_HY), BF16),
                        pltpu.VMEM((l, D_HY), F32)],
        compiler_params=_cparams("parallel"),
        name="hyena",
    )(u.reshape(nb, l, HY_IN), *[wts[n] for n in names], *tables, a, b, d, wts["hy_bias"])
    return y.reshape(nb * l, D_HY)


def _seg_rms(x, w):
    return x * lax.rsqrt(_seg_mean(x * x, HEAD_DIM) + NORM_EPS) * w


def _stack_q(q):
    lo = lax.broadcasted_iota(jnp.int32, (1, D_KV), 1) < HEAD_DIM
    qa = q[:, 0:D_KV]
    qb = q[:, D_KV:2 * D_KV]
    return jnp.concatenate([jnp.where(lo, qa, 0.0), jnp.where(lo, pltpu.roll(qa, HEAD_DIM, 1), 0.0),
                            jnp.where(lo, 0.0, pltpu.roll(qb, HEAD_DIM, 1)), jnp.where(lo, 0.0, qb)], axis=0)


def _unstack_o(o):
    r = o.shape[0] // ATT_HEADS
    lo = lax.broadcasted_iota(jnp.int32, (1, D_KV), 1) < HEAD_DIM
    ya = jnp.where(lo, o[0:r], pltpu.roll(o[r:2 * r], HEAD_DIM, 1))
    yb = jnp.where(lo, pltpu.roll(o[2 * r:3 * r], HEAD_DIM, 1), o[3 * r:4 * r])
    return jnp.concatenate([ya, yb], axis=1)


def _sink_col(sink_ref, li, r):
    rb = lax.broadcasted_iota(jnp.int32, (ATT_HEADS * r, 1), 0) // r
    col = jnp.full((ATT_HEADS * r, 1), sink_ref[li * ATT_HEADS + ATT_HEADS - 1], F32)
    for h in range(ATT_HEADS - 2, -1, -1):
        col = jnp.where(rb == h, sink_ref[li * ATT_HEADS + h], col)
    return col


def _ctx_attn_kernel(li, u_ref, qn_ref, kn_ref, sink_ref, *rest):
    y_ref, k_ref, v_ref = rest[-3:]
    u = u_ref[...]
    l = u.shape[0]
    q = _seg_rms(u[:, 0:D_ATT], qn_ref[...])
    k = _seg_rms(u[:, D_ATT:D_ATT + D_KV], kn_ref[:, 0:D_KV])
    v = u[:, D_ATT + D_KV:D_ATT + 2 * D_KV]
    k_ref[...] = k
    v_ref[...] = v
    s = _bdot_nt(_stack_q(q), k) * (HEAD_DIM ** -0.5)
    sink = _sink_col(sink_ref, li, l)
    m = jnp.maximum(jnp.max(s, axis=-1, keepdims=True), sink)
    p = jnp.exp(s - m)
    den = jnp.sum(p, axis=-1, keepdims=True) + jnp.exp(sink - m)
    y_ref[...] = _unstack_o(_bdot(p * (1.0 / den), v))


def _ctx_attention(u, wts, li, nb, l, prev=None):
    kv_spec, kv_shape = _slot_out((l, D_KV), nb, li, True)
    args = [u.reshape(nb, l, ATT_IN), wts["attn_q_norm"], wts["attn_k_norm"], wts["attn_sink"]]
    in_specs = [_seq_spec(l, ATT_IN), _layer_const(wts["attn_q_norm"], li), _layer_const(wts["attn_k_norm"], li),
                pl.BlockSpec(memory_space=pltpu.SMEM)]
    aliases = _carry_prev(args, in_specs, [] if prev is None else list(prev), 1)
    y, k, v = pl.pallas_call(
        functools.partial(_ctx_attn_kernel, li),
        grid=(nb,),
        in_specs=in_specs,
        out_specs=[_seq_spec(l, D_ATT), kv_spec, kv_spec],
        out_shape=[jax.ShapeDtypeStruct((nb, l, D_ATT), F32), kv_shape, kv_shape],
        input_output_aliases=aliases,
        compiler_params=_cparams("parallel"),
        name="ctx_attention",
    )(*args)
    return y.reshape(nb * l, D_ATT), k, v


@functools.lru_cache(maxsize=None)
def _rope_tables_host(l):
    n_rows = l // GRID_W
    rows = np.repeat(np.arange(n_rows, dtype=np.float32), GRID_W)
    cols = np.tile(np.arange(GRID_W, dtype=np.float32), n_rows)
    nf = HEAD_DIM // 4
    inv = (np.float32(ROPE_BASE) ** (-np.arange(nf, dtype=np.float32) / np.float32(nf))).astype(np.float32)
    ar = rows[:, None] * inv[None, :]
    ac = cols[:, None] * inv[None, :]
    cos = np.concatenate([np.cos(ar), np.cos(ar), np.cos(ac), np.cos(ac)], axis=-1)
    sin = np.concatenate([-np.sin(ar), np.sin(ar), -np.sin(ac), np.sin(ac)], axis=-1)
    return (np.tile(cos, (1, ATT_HEADS)).astype(np.float32), np.tile(sin, (1, ATT_HEADS)).astype(np.float32))


def _rope(x, cos, sin):
    w = x.shape[-1]
    nf = HEAD_DIM // 4
    lane = lax.broadcasted_iota(jnp.int32, x.shape, 1)
    first = (lane % (2 * nf)) < nf
    partner = jnp.where(first, pltpu.roll(x, w - nf, 1), pltpu.roll(x, nf, 1))
    return x * cos + partner * sin


def _lat_attn_kernel(nblk, li, u_ref, qn_ref, kn_ref, cos_ref, sin_ref, ck_ref, cv_ref, sink_ref,
                     y_ref, q_scr, k_scr, v_scr):
    blk = ATT_BLOCK
    l = nblk * blk
    cos_q = cos_ref[...]
    sin_q = sin_ref[...]
    u = u_ref[...]
    q = _seg_rms(u[:, 0:D_ATT], qn_ref[...])
    k = _seg_rms(u[:, D_ATT:D_ATT + D_KV], kn_ref[:, 0:D_KV])
    q_scr[...] = _rope(q, cos_q, sin_q)
    zeros = jnp.zeros((blk, D_KV), F32)
    k_scr[0:blk, :] = zeros
    k_scr[blk + l:2 * blk + l, :] = zeros
    v_scr[0:blk, :] = zeros
    v_scr[blk + l:2 * blk + l, :] = zeros
    k_scr[blk:blk + l, :] = _rope(k, cos_q[:, 0:D_KV], sin_q[:, 0:D_KV])
    v_scr[blk:blk + l, :] = u[:, D_ATT + D_KV:D_ATT + 2 * D_KV]

    scale = HEAD_DIM ** -0.5
    r = lax.broadcasted_iota(jnp.int32, (blk, 3 * blk), 0)
    cidx = lax.broadcasted_iota(jnp.int32, (blk, 3 * blk), 1)
    band = (cidx - r >= blk - WINDOW) & (cidx - r <= blk + WINDOW)
    sink = _sink_col(sink_ref, li, blk)

    def block(i, carry):
        rows = _rows(i)
        win = pl.ds(pl.multiple_of(i * blk, blk), 3 * blk)
        kpos = cidx + (i - 1) * blk
        valid = band & (kpos >= 0) & (kpos < l)
        valid = jnp.concatenate([valid] * ATT_HEADS, axis=0)
        qs = _stack_q(q_scr[rows, :])
        s_loc = jnp.where(valid, _bdot_nt(qs, k_scr[win, :]) * scale, -jnp.inf)
        s_ctx = _bdot_nt(qs, ck_ref[...]) * scale
        m = jnp.maximum(jnp.maximum(jnp.max(s_loc, axis=-1, keepdims=True),
                                    jnp.max(s_ctx, axis=-1, keepdims=True)), sink)
        p_loc = jnp.exp(s_loc - m)
        p_ctx = jnp.exp(s_ctx - m)
        den = (jnp.sum(p_loc, axis=-1, keepdims=True) + jnp.sum(p_ctx, axis=-1, keepdims=True)
               + jnp.exp(sink - m))
        inv = 1.0 / den
        y_ref[rows, :] = _unstack_o(_bdot(p_ctx * inv, cv_ref[...]) + _bdot(p_loc * inv, v_scr[win, :]))
        return carry

    lax.fori_loop(0, nblk, block, 0)


def _lat_attention(u, wts, li, ck, cv, nb, l):
    lc = ck.shape[2]
    nblk = l // ATT_BLOCK
    cos, sin = _rope_tables_host(l)
    cache_spec = pl.BlockSpec((None, None, lc, D_KV), lambda b: (b, li, 0, 0))
    y = pl.pallas_call(
        functools.partial(_lat_attn_kernel, nblk, li),
        grid=(nb,),
        in_specs=[_seq_spec(l, ATT_IN), _layer_const(wts["attn_q_norm"], li), _layer_const(wts["attn_k_norm"], li),
                  _const((l, D_ATT)), _const((l, D_ATT)), cache_spec, cache_spec,
                  pl.BlockSpec(memory_space=pltpu.SMEM)],
        out_specs=_seq_spec(l, D_ATT),
        out_shape=jax.ShapeDtypeStruct((nb, l, D_ATT), F32),
        scratch_shapes=[pltpu.VMEM((l, D_ATT), F32), pltpu.VMEM((l + 2 * ATT_BLOCK, D_KV), F32),
                        pltpu.VMEM((l + 2 * ATT_BLOCK, D_KV), F32)],
        compiler_params=_cparams("parallel"),
        name="lat_attention",
    )(u.reshape(nb, l, ATT_IN), wts["attn_q_norm"], wts["attn_k_norm"], jnp.asarray(cos), jnp.asarray(sin),
      ck, cv, wts["attn_sink"])
    return y.reshape(nb * l, D_ATT)


def _pack_mix_w_in(w):
    c0 = D_SSM + SSM_CONV_CH
    n_dt = 2 * SSM_HEADS
    zeros = jnp.zeros(w.shape[:-1] + (DT_PAD - DT_REP * n_dt,), w.dtype)
    dt_cols = [w[..., c0:c0 + n_dt]] * DT_REP
    return jnp.concatenate([w[..., :c0], w[..., c0 + n_dt:]] + dt_cols + [zeros], axis=-1).astype(BF16)


def _prep_weights(p):
    row = lambda a: a.reshape(DEPTH, 1, -1)
    pad_lanes = lambda a: jnp.pad(row(a), ((0, 0), (0, 0), (0, LANES - a[0].size)))
    return dict(
        norm_w=p["norm_w"].reshape(DEPTH * 3, 1, D_MODEL),
        ffn_w_in=p["ffn_w_in"].astype(BF16).reshape(DEPTH * 2, D_MODEL, 2 * D_FF),
        ffn_w_out=p["ffn_w_out"].astype(BF16).reshape(DEPTH * 2, D_FF, D_MODEL),
        mix_w_in=_pack_mix_w_in(p["mix_w_in"]), mix_w_out=p["mix_w_out"].astype(BF16),
        ssd_conv_w=p["ssd_conv_w"], ssd_conv_b=row(p["ssd_conv_b"]),
        ssd_dt_bias=pad_lanes(jnp.tile(row(p["ssd_dt_bias"]), (1, 1, DT_REP))),
        ssd_a_log=pad_lanes(jnp.tile(row(p["ssd_a_log"]), (1, 1, DT_REP))),
        ssd_d=row(jnp.repeat(p["ssd_d"], SSM_HEAD_DIM, axis=-1)), ssd_norm_w=row(p["ssd_norm_w"]),
        hy_conv_w=p["hy_conv_w"], hy_conv_b=row(p["hy_conv_b"]), hy_bias=p["hy_bias"],
        hy_w1=jnp.pad(p["hy_w1"], ((0, 0), (0, LANES - HY_EMB), (0, 0))), hy_b1=row(p["hy_b1"]),
        hy_w2=p["hy_w2"], hy_b2=row(p["hy_b2"]), hy_w3=p["hy_w3"], hy_freq=row(p["hy_freq"]),
        ret_decay_logit=pad_lanes(p["ret_decay_logit"]), ret_gn_w=row(p["ret_gn_w"]),
        attn_q_norm=row(jnp.tile(p["attn_q_norm"], (1, ATT_HEADS))),
        attn_k_norm=row(jnp.tile(p["attn_k_norm"], (1, ATT_HEADS))),
        attn_sink=p["attn_sink"].reshape(DEPTH * ATT_HEADS),
    )


def _layer(x, mod, row0, wts, li, nb, l, ssd_s0, ret_s0, ctx_kv, hy_tables, hy_spectra, carried):
    ctx = ctx_kv is None
    rpm = nb * l if ctx else l
    x = _ffn(x, mod, row0, rpm, wts, li, 0)
    z, xbc, hy, ret, att, dt = _inproj(x, mod, row0, rpm, wts, li)
    y_ssd, s_ssd = _ssd(z, xbc, dt, wts, li, ssd_s0, nb, l, stacked=ctx, prev=carried.get("ssd"))
    y_hy = _hyena(hy, wts, li, hy_tables, hy_spectra, nb, l)
    y_ret, s_ret = _retention(ret, wts, li, ret_s0, nb, l, stacked=ctx, prev=carried.get("ret"))
    if ctx:
        y_att, k, v = _ctx_attention(att, wts, li, nb, l, prev=carried.get("kv"))
        carried = dict(ssd=s_ssd, ret=s_ret, kv=(k, v))
    else:
        y_att = _lat_attention(att, wts, li, ctx_kv[0], ctx_kv[1], nb, l)
    x = _ffn(x, mod, row0, rpm, wts, li, 1, mix=(y_ssd, y_hy, y_ret, y_att))
    return x, carried


def kernel(x_prompt, x_sample, cache_k, cache_v, state_ssd, state_ret, c, c_ctx, w_mod, b_mod, norm_w, ffn_w_in, ffn_w_out, mix_w_in, mix_w_out, ssd_conv_w, ssd_conv_b, ssd_dt_bias, ssd_a_log, ssd_d, ssd_norm_w, hy_conv_w, hy_conv_b, hy_w1, hy_b1, hy_w2, hy_b2, hy_w3, hy_freq, hy_bias, ret_decay_logit, ret_gn_w, attn_q_norm, attn_k_norm, attn_sink):
    bp, lp_len, _ = x_prompt.shape
    bs, ls_len, _ = x_sample.shape
    lc = cache_k.shape[2]

    cond = jnp.concatenate([c_ctx[None, :], c, jnp.zeros((MOD_ROWS - 1 - bs, D_MODEL), F32)], axis=0)
    mod = _modulation(cond, w_mod, b_mod)
    wts = _prep_weights(dict(
        norm_w=norm_w, ffn_w_in=ffn_w_in, ffn_w_out=ffn_w_out, mix_w_in=mix_w_in, mix_w_out=mix_w_out,
        ssd_conv_w=ssd_conv_w, ssd_conv_b=ssd_conv_b, ssd_dt_bias=ssd_dt_bias, ssd_a_log=ssd_a_log, ssd_d=ssd_d,
        ssd_norm_w=ssd_norm_w, hy_conv_w=hy_conv_w, hy_conv_b=hy_conv_b, hy_w1=hy_w1, hy_b1=hy_b1, hy_w2=hy_w2,
        hy_b2=hy_b2, hy_w3=hy_w3, hy_freq=hy_freq, hy_bias=hy_bias, ret_decay_logit=ret_decay_logit,
        ret_gn_w=ret_gn_w, attn_q_norm=attn_q_norm, attn_k_norm=attn_k_norm, attn_sink=attn_sink))
    ck = cache_k.reshape(bs, DEPTH, lc, D_KV)
    cv = cache_v.reshape(bs, DEPTH, lc, D_KV)

    tab_p = _dft_tables(lp_len)
    tab_s = _dft_tables(ls_len)
    blk_p = _dft_block_tables(tab_p, lp_len)
    blk_s = _dft_block_tables(tab_s, ls_len)

    yp = x_prompt.reshape(bp * lp_len, D_MODEL)
    ys = x_sample.reshape(bs * ls_len, D_MODEL)
    carried = {}
    for li in range(DEPTH):
        spec_p = _hy_filter(lp_len, tab_p[0], tab_p[1], wts, li)
        spec_s = _hy_filter(ls_len, tab_s[0], tab_s[1], wts, li)
        yp, carried = _layer(yp, mod, li * MOD_ROWS, wts, li, bp, lp_len, None, None, None, blk_p, spec_p, carried)
        ys, _ = _layer(ys, mod, li * MOD_ROWS + 1, wts, li, bs, ls_len, state_ssd, state_ret, (ck, cv),
                       blk_s, spec_s, {})

    kv_shape = (bp, DEPTH, lp_len, ATT_KV_HEADS, HEAD_DIM)
    new_k, new_v = carried["kv"]
    return (yp.reshape(bp, lp_len, D_MODEL), ys.reshape(bs, ls_len, D_MODEL),
            new_k.reshape(kv_shape), new_v.reshape(kv_shape), carried["ssd"], carried["ret"])
```

```python
import functools
import math

import numpy as np
import jax
import jax.numpy as jnp
from jax import lax
from jax.experimental import pallas as pl
from jax.experimental.pallas import tpu as pltpu

F32 = jnp.float32
BF16 = jnp.bfloat16
HI = lax.Precision.HIGHEST

D_MODEL = 1024
DEPTH = 2
GRID_W = 64
D_FF = 2816
N_MOD = 9
NORM_EPS = 1e-6
CHUNK = 128
D_SSM = 256
SSM_HEADS = 4
SSM_HEAD_DIM = 64
SSM_STATE = 128
SSM_GROUPS = 2
SSM_CONV_CH = D_SSM + 2 * SSM_GROUPS * SSM_STATE
D_HY = 256
HY_ORDER = 2
HY_BANDS = 16
HY_EMB = 1 + 2 * HY_BANDS
HY_HIDDEN = 64
HY_FAST_DECAY = 0.3
HY_SLOW_DECAY = 1.5
HY_TARGET = 1e-2
HY_IN = (HY_ORDER + 1) * D_HY
D_RET = 256
RET_HEADS = 4
RET_HEAD_DIM = 64
RET_IN = 4 * D_RET
ATT_HEADS = 4
ATT_KV_HEADS = 2
HEAD_DIM = 64
D_ATT = ATT_HEADS * HEAD_DIM
D_KV = ATT_KV_HEADS * HEAD_DIM
ATT_IN = D_ATT + 2 * D_KV
WINDOW = 128
ATT_BLOCK = 128
ROPE_BASE = 10000.0
D_MIX = D_SSM + D_HY + D_RET + D_ATT

LANES = 128
DT_PAD = LANES
D_IN_PAD = D_SSM + SSM_CONV_CH + HY_IN + RET_IN + ATT_IN + DT_PAD
VMEM_LIMIT = 56 * 1024 * 1024
MOD_ROWS = 8

TOKEN_TILE = 512
DFT_BLOCK = 256
HY_GROUP = 2
SEQ_GROUP = 2
ATT_GROUP = 1


def _cparams(*sem):
    return pltpu.CompilerParams(dimension_semantics=sem, vmem_limit_bytes=VMEM_LIMIT)


def _rms(x, w):
    return x * lax.rsqrt(jnp.mean(x * x, axis=-1, keepdims=True) + NORM_EPS) * w


def _silu(x):
    return x * (1.0 / (1.0 + jnp.exp(-x)))


def _softplus(x):
    return jnp.maximum(x, 0.0) + jnp.log1p(jnp.exp(-jnp.abs(x)))


def _bdot(a, b):
    return jnp.dot(a.astype(BF16), b.astype(BF16), preferred_element_type=F32)


def _bdot_nt(a, b):
    return lax.dot_general(a.astype(BF16), b.astype(BF16), (((1,), (1,)), ((), ())),
                           preferred_element_type=F32)


def _bdot_tn(a, b):
    return lax.dot_general(a.astype(BF16), b.astype(BF16), (((0,), (0,)), ((), ())),
                           preferred_element_type=F32)


def _hdot(a, b):
    return jnp.dot(a, b, preferred_element_type=F32, precision=HI)


def _full(shape):
    n = len(shape)
    return pl.BlockSpec(shape, lambda *_: (0,) * n)


def _const(shape):
    n = len(shape)
    return pl.BlockSpec(shape, lambda *_: (0,) * n, pipeline_mode=pl.Buffered(1))


def _layer_const(arr, li):
    tail = arr.shape[1:]
    zeros = (0,) * len(tail)
    return pl.BlockSpec((None,) + tail, lambda *_: (li,) + zeros, pipeline_mode=pl.Buffered(1))


def _mod_spec(k, tm, rows_per_mod, row0):
    return pl.BlockSpec((None, 1, D_MODEL), lambda i: (row0 + (i * tm) // rows_per_mod, 0, k))


def _seq_spec(l, w, ng=None):
    return pl.BlockSpec((ng, l, w), lambda b: (b, 0, 0))


def _head_masks(width, heads):
    lane = lax.broadcasted_iota(jnp.int32, (1, width), 1)
    hd = width // heads
    return [(lane >= h * hd) & (lane < (h + 1) * hd) for h in range(heads)]


def _by_head(masks, vals):
    out = vals[-1]
    for m, v in zip(masks[-2::-1], vals[-2::-1]):
        out = jnp.where(m, v, out)
    return out


def _block_diag(n, blk, value):
    i = lax.broadcasted_iota(jnp.int32, (n, n), 0) // blk
    j = lax.broadcasted_iota(jnp.int32, (n, n), 1) // blk
    return jnp.where(i == j, value, 0.0).astype(F32)


def _mod_kernel(c_ref, w_ref, b_ref, o_ref):
    c = c_ref[...]
    o_ref[...] = _bdot(_silu(c), w_ref[...]) + b_ref[...]


def _modulation(cond, w_mod, b_mod):
    out = pl.pallas_call(
        _mod_kernel,
        grid=(DEPTH, N_MOD),
        in_specs=[pl.BlockSpec((MOD_ROWS, D_MODEL), lambda l, j: (0, 0)),
                  pl.BlockSpec((None, D_MODEL, D_MODEL), lambda l, j: (l, 0, j)),
                  pl.BlockSpec((None, 1, D_MODEL), lambda l, j: (l, 0, j))],
        out_specs=pl.BlockSpec((None, MOD_ROWS, D_MODEL), lambda l, j: (l, 0, j)),
        out_shape=jax.ShapeDtypeStruct((DEPTH, MOD_ROWS, N_MOD * D_MODEL), F32),
        compiler_params=_cparams("arbitrary", "arbitrary"),
        name="modulation",
    )(cond, w_mod, b_mod.reshape(DEPTH, 1, N_MOD * D_MODEL))
    return out.reshape(DEPTH * MOD_ROWS, 1, N_MOD * D_MODEL)


def _ffn_kernel(n_mix, x_ref, sh_ref, sc_ref, g_ref, nw_ref, wi_ref, wo_ref, *rest):
    o_ref = rest[-1]
    x = x_ref[...]
    if n_mix:
        gm_ref, wm_ref = rest[0], rest[1 + n_mix]
        w = D_MIX // n_mix
        acc = _bdot(rest[1][...], wm_ref[0:w, :])
        for j in range(1, n_mix):
            acc += _bdot(rest[1 + j][...], wm_ref[j * w:(j + 1) * w, :])
        x = x + gm_ref[...] * acc
    h = (_rms(x, nw_ref[...]) * (1.0 + sc_ref[...]) + sh_ref[...]).astype(BF16)
    gate = jnp.dot(h, wi_ref[:, 0:D_FF], preferred_element_type=F32)
    up = jnp.dot(h, wi_ref[:, D_FF:2 * D_FF], preferred_element_type=F32)
    o_ref[...] = x + 0.5 * g_ref[...] * _bdot(_silu(gate) * up, wo_ref[...])


def _ffn(x, mod, row0, rows_per_mod, wts, li, k, mix=()):
    t = x.shape[0]
    tm = min(TOKEN_TILE, rows_per_mod)
    w_in, w_out = wts["ffn_w_in"], wts["ffn_w_out"]
    args = [x, mod, mod, mod, wts["norm_w"], w_in, w_out]
    in_specs = [pl.BlockSpec((tm, D_MODEL), lambda i: (i, 0)),
                _mod_spec(6 * k, tm, rows_per_mod, row0),
                _mod_spec(6 * k + 1, tm, rows_per_mod, row0),
                _mod_spec(6 * k + 2, tm, rows_per_mod, row0),
                _layer_const(wts["norm_w"], 3 * li + 2 * k),
                _layer_const(w_in, 2 * li + k), _layer_const(w_out, 2 * li + k)]
    if mix:
        args += [mod, *mix, wts["mix_w_out"]]
        in_specs += [_mod_spec(5, tm, rows_per_mod, row0)]
        in_specs += [pl.BlockSpec((tm, y.shape[1]), lambda i: (i, 0)) for y in mix]
        in_specs += [_layer_const(wts["mix_w_out"], li)]
    return pl.pallas_call(
        functools.partial(_ffn_kernel, len(mix)),
        grid=(t // tm,),
        in_specs=in_specs,
        out_specs=pl.BlockSpec((tm, D_MODEL), lambda i: (i, 0)),
        out_shape=jax.ShapeDtypeStruct((t, D_MODEL), F32),
        compiler_params=_cparams("parallel"),
        name="ffn",
    )(*args)


_IN_SPLITS = (("z", D_SSM), ("xbc", SSM_CONV_CH), ("hy", HY_IN), ("ret", RET_IN), ("att", ATT_IN), ("dt", DT_PAD))


def _inproj_kernel(x_ref, sh_ref, sc_ref, nw_ref, w_ref, *o_refs):
    h = (_rms(x_ref[...], nw_ref[...]) * (1.0 + sc_ref[...]) + sh_ref[...]).astype(BF16)
    off = 0
    for (_, width), o_ref in zip(_IN_SPLITS, o_refs):
        o_ref[...] = jnp.dot(h, w_ref[:, off:off + width], preferred_element_type=F32)
        off += width


def _inproj(x, mod, row0, rows_per_mod, wts, li):
    t = x.shape[0]
    tm = min(TOKEN_TILE, rows_per_mod)
    return pl.pallas_call(
        _inproj_kernel,
        grid=(t // tm,),
        in_specs=[pl.BlockSpec((tm, D_MODEL), lambda i: (i, 0)),
                  _mod_spec(3, tm, rows_per_mod, row0),
                  _mod_spec(4, tm, rows_per_mod, row0),
                  _layer_const(wts["norm_w"], 3 * li + 1),
                  _layer_const(wts["mix_w_in"], li)],
        out_specs=[pl.BlockSpec((tm, width), lambda i: (i, 0)) for _, width in _IN_SPLITS],
        out_shape=[jax.ShapeDtypeStruct((t, width), F32) for _, width in _IN_SPLITS],
        compiler_params=_cparams("parallel"),
        name="mix_in",
    )(x, mod, mod, wts["norm_w"], wts["mix_w_in"])


def _conv3_chunk(x_ref, c, nc, w, b):
    q = CHUNK
    l = nc * q
    r0 = pl.multiple_of(c * q, q)
    x = x_ref[pl.ds(r0, q), :]
    prev = x_ref[pl.ds(jnp.maximum(r0 - 1, 0), 1), :]
    nxt = x_ref[pl.ds(jnp.minimum(r0 + q, l - 1), 1), :]
    prev = jnp.where(c > 0, prev, 0.0)
    nxt = jnp.where(c < nc - 1, nxt, 0.0)
    rid = lax.broadcasted_iota(jnp.int32, (q, 1), 0)
    xm1 = jnp.where(rid == 0, prev, pltpu.roll(x, 1, 0))
    xp1 = jnp.where(rid == q - 1, nxt, pltpu.roll(x, q - 1, 0))
    return xm1 * w[0:1, :] + x * w[1:2, :] + xp1 * w[2:3, :] + b


def _tri(lower):
    i = lax.broadcasted_iota(jnp.int32, (CHUNK, CHUNK), 0)
    j = lax.broadcasted_iota(jnp.int32, (CHUNK, CHUNK), 1)
    return (j <= i) if lower else (j >= i)


def _rows(c):
    return pl.ds(pl.multiple_of(c * CHUNK, CHUNK), CHUNK)


def _split_cat(v, parts, axis):
    out, r = [], v
    for i in range(parts):
        piece = r.astype(BF16)
        out.append(piece)
        if i + 1 < parts:
            r = r - piece.astype(F32)
    return jnp.concatenate(out, axis=axis)


def _seg_mean(x, seg):
    w = x.shape[-1]
    ones = _block_diag(w, seg, 1.0).astype(BF16)
    return jnp.dot(_split_cat(x, 2, axis=1), jnp.concatenate([ones, ones], axis=0),
                   preferred_element_type=F32) * (1.0 / seg)


def _stack_heads(x, masks):
    return jnp.concatenate([jnp.where(m, x, 0.0) for m in masks], axis=0)


def _slot_out(tail, nb, li, stacked, ng=None):
    zeros = (0,) * len(tail)
    slot = li if stacked else 0
    spec = pl.BlockSpec((ng, None) + tail, lambda b: (b, slot) + zeros)
    return spec, jax.ShapeDtypeStruct((nb, DEPTH if stacked else 1) + tail, F32)


def _carry_prev(args, in_specs, prevs, first_out):
    aliases = {}
    for j, prev in enumerate(prevs):
        aliases[len(args)] = first_out + j
        args.append(prev)
        in_specs.append(pl.BlockSpec(memory_space=pl.ANY))
    return aliases


SSD_SEL_W = 2 * SSM_HEADS * SSM_STATE
CUM_PIECES = 3
WGT_PIECES = 2
DT_REP = CUM_PIECES + WGT_PIECES


def _ssd_kernel(nc, has_s0, *refs):
    refs = list(refs)
    (z_ref, xbc_ref, dt_ref, cw_ref, cb_ref, dtb_ref, alog_ref, dsk_ref, nw_ref) = refs[:9]
    s0_ref = refs[9] if has_s0 else None
    (y_ref, s_ref, xs_scr, xk_scr, xt_scr, dt_scr, cum_scr, dtt_scr, cumt_scr, yf_scr, yb_scr, st_scr,
     msk_scr, sel_scr) = refs[-14:]
    q = CHUNK
    nh, n, p = SSM_HEADS, SSM_STATE, SSM_HEAD_DIM
    rep = nh // SSM_GROUPS
    hm = _head_masks(D_SSM, nh)

    @pl.when(pl.program_id(0) == 0)
    def _():
        i = lax.broadcasted_iota(jnp.int32, (D_SSM, nh * n), 0) // p
        j = lax.broadcasted_iota(jnp.int32, (D_SSM, nh * n), 1) // n
        msk_scr[...] = jnp.where(i == j, 1.0, 0.0).astype(F32)
        k = lax.broadcasted_iota(jnp.int32, (LANES, SSD_SEL_W), 0)
        col = lax.broadcasted_iota(jnp.int32, (LANES, SSD_SEL_W), 1)
        is_cum = col < nh * n
        grp = k // (2 * nh)
        grp_ok = (is_cum & (grp < CUM_PIECES)) | ((~is_cum) & (grp >= CUM_PIECES) & (grp < DT_REP))
        hit = grp_ok & (k % nh == (col % (nh * n)) // n)
        sel_scr[...] = jnp.where(hit, 1.0, 0.0).astype(BF16)

    cw = cw_ref[...]
    cb = cb_ref[...]
    neg_a = -jnp.exp(alog_ref[...])
    dtb = dtb_ref[...]

    def prep(c, carry):
        rows = _rows(c)
        xall = _silu(_conv3_chunk(xbc_ref, c, nc, cw, cb))
        xs_scr[rows, :] = xall
        xs = xall[:, 0:D_SSM]
        xk_scr[c] = _stack_heads(xs, hm).astype(BF16)
        xt_scr[c] = xs.T.astype(BF16)
        dt = _softplus(dt_ref[rows, :] + dtb)
        la = _split_cat(dt * neg_a, 3, axis=0)
        tri = jnp.concatenate([_tri(True), _tri(False)], axis=0)
        tri = jnp.where(tri, 1.0, 0.0).astype(BF16)
        cs = jnp.dot(jnp.concatenate([tri] * 3, axis=1), la, preferred_element_type=F32)
        lane = lax.broadcasted_iota(jnp.int32, (1, LANES), 1)
        cum = jnp.where(lane % (2 * nh) < nh, cs[0:q], cs[q:2 * q])
        dt_scr[rows, :] = dt
        cum_scr[rows, :] = cum
        dtt_scr[c] = dt.T
        cumt_scr[c] = cum.T
        return carry

    lax.fori_loop(0, nc, prep, 0)

    for d in range(2):
        for h in range(nh):
            blk = slice(h * p, (h + 1) * p)
            if has_s0:
                parts = [jnp.zeros((p, n), F32)] * nh
                parts[h] = s0_ref[d, h].T
                st_scr[d, blk, :] = jnp.concatenate(parts, axis=1)
            else:
                st_scr[d, blk, :] = jnp.zeros((p, nh * n), F32)

    def pack_scalars(c, d):
        rows = _rows(c)
        dt = dt_scr[rows, :]
        cum = cum_scr[rows, :]
        edge = q - 1 if d == 0 else 0
        lane = lax.broadcasted_iota(jnp.int32, (1, LANES), 1)
        used = (lane < DT_REP * 2 * nh) & ((lane % (2 * nh)) // nh == d)
        cum = jnp.where(used, cum, 0.0)
        wgt = jnp.exp(cum[edge:edge + 1, :] - cum) * dt
        grp = lane // (2 * nh)
        packed = jnp.zeros_like(cum)
        for src, pieces, g0 in ((cum, CUM_PIECES, 0), (wgt, WGT_PIECES, CUM_PIECES)):
            rest = src
            for i in range(pieces):
                piece = rest.astype(BF16).astype(F32)
                packed = jnp.where(grp == g0 + i, piece, packed)
                rest = rest - piece
        return jnp.where(used, packed, 0.0).astype(BF16)

    def scan_chunk(c, d, e):
        rows = _rows(c)
        bm = [xs_scr[rows, D_SSM + g * n:D_SSM + (g + 1) * n] for g in range(SSM_GROUPS)]
        cm = [xs_scr[rows, D_SSM + (SSM_GROUPS + g) * n:D_SSM + (SSM_GROUPS + g + 1) * n]
              for g in range(SSM_GROUPS)]
        mask = _tri(d == 0)
        edge = q - 1 if d == 0 else 0
        cb_t = [_bdot_nt(cm[g], bm[g]) for g in range(SSM_GROUPS)]
        blocks, ecol = [], []
        for h in range(nh):
            r = d * nh + h
            col = e[:, h * n:(h + 1) * n]
            decay = jnp.exp(jnp.where(mask, col - cumt_scr[c, r:r + 1, :], -jnp.inf))
            blocks.append((cb_t[h // rep] * decay * dtt_scr[c, r:r + 1, :]).astype(BF16))
            ecol.append(jnp.exp(col))
        y = jnp.dot(jnp.concatenate(blocks, axis=1), xk_scr[c], preferred_element_type=F32)
        s_old = st_scr[d]
        c4 = jnp.concatenate([cm[h // rep] for h in range(nh)], axis=1)
        lo = lax.broadcasted_iota(jnp.int32, (1, n), 1) < p
        ecol = jnp.concatenate([jnp.where(lo, ecol[2 * j], ecol[2 * j + 1]) for j in range(nh // 2)], axis=1)
        y = y + ecol * _bdot_nt(c4, s_old)
        wk = jnp.concatenate([bm[h // rep] for h in range(nh)], axis=1) * e[:, nh * n:2 * nh * n]
        upd = jnp.dot(xt_scr[c], wk.astype(BF16), preferred_element_type=F32)
        st_scr[d] = jnp.exp(e[edge:edge + 1, 0:nh * n]) * s_old + upd * msk_scr[...]
        return y

    def scan(i, carry):
        j = nc - 1 - i
        packed = jnp.concatenate([pack_scalars(i, 0), pack_scalars(j, 1)], axis=0)
        e = jnp.dot(packed, sel_scr[...], preferred_element_type=F32)
        yf_scr[_rows(i), :] = scan_chunk(i, 0, e[0:q])
        yb_scr[_rows(j), :] = scan_chunk(j, 1, e[q:2 * q])
        return carry

    lax.fori_loop(0, nc, scan, 0)

    for d in range(2):
        for h in range(nh):
            s_ref[d, h] = st_scr[d, h * p:(h + 1) * p, h * n:(h + 1) * n].T

    dsk = dsk_ref[...]
    nw = nw_ref[...]

    def fin(c, carry):
        rows = _rows(c)
        y = yf_scr[rows, :] + yb_scr[rows, :] + dsk * xs_scr[rows, 0:D_SSM]
        y_ref[rows, :] = _rms(y * _silu(z_ref[rows, :]), nw)
        return carry

    lax.fori_loop(0, nc, fin, 0)


def _ssd(z, xbc, dt, wts, li, s0, nb, l, stacked=False, prev=None):
    nc = l // CHUNK
    has_s0 = s0 is not None
    names = ("ssd_conv_w", "ssd_conv_b", "ssd_dt_bias", "ssd_a_log", "ssd_d", "ssd_norm_w")
    args = [z.reshape(nb, l, D_SSM), xbc.reshape(nb, l, SSM_CONV_CH), dt.reshape(nb, l, DT_PAD)]
    args += [wts[n] for n in names]
    in_specs = [_seq_spec(l, D_SSM), _seq_spec(l, SSM_CONV_CH), _seq_spec(l, DT_PAD)]
    in_specs += [_layer_const(wts[n], li) for n in names]
    if has_s0:
        args.append(s0)
        in_specs.append(pl.BlockSpec((None, None, 2, SSM_HEADS, SSM_STATE, SSM_HEAD_DIM),
                                     lambda b: (b, li, 0, 0, 0, 0)))
    st_spec, st_shape = _slot_out((2, SSM_HEADS, SSM_STATE, SSM_HEAD_DIM), nb, li, stacked)
    aliases = _carry_prev(args, in_specs, [] if prev is None else [prev], 1)
    y, s = pl.pallas_call(
        functools.partial(_ssd_kernel, nc, has_s0),
        grid=(nb,),
        in_specs=in_specs,
        out_specs=[_seq_spec(l, D_SSM), st_spec],
        out_shape=[jax.ShapeDtypeStruct((nb, l, D_SSM), F32), st_shape],
        input_output_aliases=aliases,
        scratch_shapes=[pltpu.VMEM((l, SSM_CONV_CH), F32),
                        pltpu.VMEM((nc, SSM_HEADS * CHUNK, D_SSM), BF16), pltpu.VMEM((nc, D_SSM, CHUNK), BF16),
                        pltpu.VMEM((l, DT_PAD), F32), pltpu.VMEM((l, DT_PAD), F32),
                        pltpu.VMEM((nc, DT_PAD, CHUNK), F32), pltpu.VMEM((nc, DT_PAD, CHUNK), F32),
                        pltpu.VMEM((l, D_SSM), F32), pltpu.VMEM((l, D_SSM), F32),
                        pltpu.VMEM((2, D_SSM, SSM_HEADS * SSM_STATE), F32),
                        pltpu.VMEM((D_SSM, SSM_HEADS * SSM_STATE), F32),
                        pltpu.VMEM((LANES, SSD_SEL_W), BF16)],
        compiler_params=_cparams("arbitrary"),
        name="ssd",
    )(*args)
    return y.reshape(nb * l, D_SSM), s


def _ret_kernel(nc, ng, has_s0, *refs):
    refs = list(refs)
    u_ref, dl_ref, gn_ref = refs[:3]
    s0_ref = refs[3] if has_s0 else None
    y_ref, s_ref, yf_scr, yb_scr, st_scr, dm_scr, e_scr, ea_scr, bd_scr = refs[-9:]
    q = CHUNK
    hd = RET_HEAD_DIM
    hm = _head_masks(D_RET, RET_HEADS)

    @pl.when(pl.program_id(0) == 0)
    def _():
        log_g = -_softplus(-dl_ref[...])
        ii = lax.broadcasted_iota(jnp.int32, (q, q), 0)
        jj = lax.broadcasted_iota(jnp.int32, (q, q), 1)
        dij = (ii - jj).astype(F32)
        ri = lax.broadcasted_iota(jnp.int32, (q, 1), 0).astype(F32)
        lfs, lbs = [], []
        for h in range(RET_HEADS):
            lf = log_g[:, h:h + 1]
            lb = log_g[:, RET_HEADS + h:RET_HEADS + h + 1]
            lfs.append(lf)
            lbs.append(lb)
            d_f = jnp.exp(jnp.where(dij >= 0, dij * lf, -jnp.inf))
            d_b = jnp.exp(jnp.where(dij <= 0, -dij * lb, -jnp.inf))
            dm_scr[:, h * q:(h + 1) * q] = d_f + d_b
        lf_l = _by_head(hm, lfs)
        lb_l = _by_head(hm, lbs)
        e_scr[0] = jnp.exp((ri + 1.0) * lf_l)
        e_scr[1] = jnp.exp((q - ri) * lb_l)
        e_scr[2] = jnp.exp((q - 1.0 - ri) * lf_l)
        e_scr[3] = jnp.exp(ri * lb_l)
        ea_scr[0:1, :] = jnp.exp(q * lf_l)
        ea_scr[1:2, :] = jnp.exp(q * lb_l)
        bd_scr[...] = _block_diag(D_RET, hd, 1.0)

    class Seq:
        def __init__(self, g):
            self.u, self.y, self.s = u_ref.at[g], y_ref.at[g], s_ref.at[g]
            self.s0 = s0_ref.at[g] if has_s0 else None
            self.yf, self.yb, self.st = yf_scr.at[g], yb_scr.at[g], st_scr.at[g]

    seqs = [Seq(g) for g in range(ng)]

    for sq in seqs:
        for d in range(2):
            for h in range(RET_HEADS):
                blk = slice(h * hd, (h + 1) * hd)
                if has_s0:
                    parts = [jnp.zeros((hd, hd), F32)] * RET_HEADS
                    parts[h] = sq.s0[d, h]
                    sq.st[d, blk, :] = jnp.concatenate(parts, axis=1)
                else:
                    sq.st[d, blk, :] = jnp.zeros((hd, D_RET), F32)

    def qkv(sq, rows):
        return (sq.u[rows, 0:D_RET], sq.u[rows, D_RET:2 * D_RET] * (RET_HEAD_DIM ** -0.5),
                sq.u[rows, 2 * D_RET:3 * D_RET])

    def state_step(sq, d, qq, kk, vv):
        s_old = sq.st[d]
        y = e_scr[d] * _bdot(qq, s_old)
        sq.st[d] = ea_scr[d:d + 1, :] * s_old + _bdot_tn(kk * e_scr[2 + d], vv) * bd_scr[...]
        return y

    def scan(i, carry):
        rows = _rows(i)
        rows_b = _rows(nc - 1 - i)
        for sq in seqs:
            qq, kk, vv = qkv(sq, rows)
            sc = _bdot_nt(qq, _stack_heads(kk, hm)) * dm_scr[...]
            sq.yf[rows, :] = _bdot(sc, _stack_heads(vv, hm)) + state_step(sq, 0, qq, kk, vv)
            qq, kk, vv = qkv(sq, rows_b)
            sq.yb[rows_b, :] = state_step(sq, 1, qq, kk, vv)
        return carry

    lax.fori_loop(0, nc, scan, 0)

    for sq in seqs:
        for d in range(2):
            for h in range(RET_HEADS):
                sq.s[d, h] = sq.st[d, h * hd:(h + 1) * hd, h * hd:(h + 1) * hd]

    gn = gn_ref[...]

    def fin(c, carry):
        rows = _rows(c)
        for sq in seqs:
            y = sq.yf[rows, :] + sq.yb[rows, :]
            cen = y - _seg_mean(y, hd)
            var = _seg_mean(cen * cen, hd)
            sq.y[rows, :] = cen * lax.rsqrt(var + NORM_EPS) * gn * _silu(sq.u[rows, 3 * D_RET:4 * D_RET])
        return carry

    lax.fori_loop(0, nc, fin, 0)


def _retention(u, wts, li, s0, nb, l, stacked=False, prev=None):
    nc = l // CHUNK
    has_s0 = s0 is not None
    names = ("ret_decay_logit", "ret_gn_w")
    ng = SEQ_GROUP
    args = [u.reshape(nb, l, RET_IN)] + [wts[n] for n in names]
    in_specs = [_seq_spec(l, RET_IN, ng)] + [_layer_const(wts[n], li) for n in names]
    if has_s0:
        args.append(s0)
        in_specs.append(pl.BlockSpec((ng, None, 2, RET_HEADS, RET_HEAD_DIM, RET_HEAD_DIM),
                                     lambda b: (b, li, 0, 0, 0, 0)))
    st_spec, st_shape = _slot_out((2, RET_HEADS, RET_HEAD_DIM, RET_HEAD_DIM), nb, li, stacked, ng)
    aliases = _carry_prev(args, in_specs, [] if prev is None else [prev], 1)
    y, s = pl.pallas_call(
        functools.partial(_ret_kernel, nc, ng, has_s0),
        grid=(nb // ng,),
        in_specs=in_specs,
        out_specs=[_seq_spec(l, D_RET, ng), st_spec],
        out_shape=[jax.ShapeDtypeStruct((nb, l, D_RET), F32), st_shape],
        input_output_aliases=aliases,
        scratch_shapes=[pltpu.VMEM((ng, l, D_RET), F32), pltpu.VMEM((ng, l, D_RET), F32),
                        pltpu.VMEM((ng, 2, D_RET, D_RET), F32), pltpu.VMEM((CHUNK, RET_HEADS * CHUNK), F32),
                        pltpu.VMEM((4, CHUNK, D_RET), F32), pltpu.VMEM((8, D_RET), F32),
                        pltpu.VMEM((D_RET, D_RET), F32)],
        compiler_params=_cparams("arbitrary"),
        name="retention",
    )(*args)
    return y.reshape(nb * l, D_RET), s


def _split(x):
    hi = x.astype(BF16)
    return hi, (x - hi.astype(F32)).astype(BF16)


def _dot3(a_hi, a_lo, b_hi, b_lo):
    d = lambda p, q: jnp.dot(p, q, preferred_element_type=F32)
    return d(a_hi, b_hi) + (d(a_lo, b_hi) + d(a_hi, b_lo))


@functools.lru_cache(maxsize=None)
def _dft_fwd_host(l):
    n = 2 * l
    f = np.arange(l, dtype=np.int64)[:, None]
    s = np.arange(l, dtype=np.int64)[None, :]
    ang = ((f * s) % n).astype(np.float64) * (2.0 * math.pi / n)
    im = -np.sin(ang)
    im[0] = np.where(np.arange(l) % 2 == 0, 1.0, -1.0)
    return np.concatenate([np.cos(ang), im], axis=0).astype(np.float32)


def _dft_tables(l):
    n = 2 * l
    fwd = jnp.asarray(_dft_fwd_host(l))
    wgt = np.full((n, 1), 2.0 / n, np.float32)
    wgt[0] = wgt[l] = 1.0 / n
    return _split(fwd) + _split((fwd * wgt).T)


def _dft_block_tables(tables, l):
    fwd_hi, fwd_lo, inv_hi, inv_lo = tables
    r = DFT_BLOCK
    nblk = l // r
    fb = lambda t: (t[0:l].reshape(nblk, r, l), t[l:2 * l].reshape(nblk, r, l))
    fwd_blk = jnp.concatenate(fb(fwd_hi) + fb(fwd_lo), axis=1)
    inv_blk = jnp.concatenate([inv_hi.reshape(nblk, r, 2 * l), inv_lo.reshape(nblk, r, 2 * l)], axis=1)
    return fwd_blk, inv_blk


def _hy_filter_kernel(l, feats_ref, dec_ref, w1_ref, b1_ref, w2_ref, b2_ref, w3_ref, fr_ref, fh_ref, fl_ref,
                      a_ref, b_ref, d_ref):
    fr = fr_ref[...]
    h = jnp.sin(fr * (_hdot(feats_ref[...], w1_ref[...]) + b1_ref[...]))
    h = jnp.sin(fr * (_hdot(h, w2_ref[...]) + b2_ref[...]))
    h = _hdot(h, w3_ref[...])
    dec = jnp.concatenate([dec_ref[...]] * HY_ORDER, axis=-1)
    row0 = lax.broadcasted_iota(jnp.int32, (l, 1), 0) == 0
    hf = h[:, 0:HY_ORDER * D_HY] * dec
    hb = h[:, HY_ORDER * D_HY:2 * HY_ORDER * D_HY] * dec
    hb = jnp.where(row0, 0.0, hb)
    hs = _split(hf + hb)
    hd = _split(hf - hb)
    re = _dot3(fh_ref[0:l, :], fl_ref[0:l, :], *hs)
    ny = _dot3(fh_ref[l:l + 8, :], fl_ref[l:l + 8, :], *hs)[0:1]
    im = _dot3(fh_ref[l:2 * l, :], fl_ref[l:2 * l, :], *hd)
    for o in range(HY_ORDER):
        cols = slice(o * D_HY, (o + 1) * D_HY)
        a_ref[o] = re[:, cols]
        b_ref[o] = jnp.where(row0, 0.0, im[:, cols])
        d_ref[o] = jnp.where(row0, ny[:, cols], re[:, cols])


def _hy_filter(l, fwd_hi, fwd_lo, wts, li):
    pos = np.arange(l, dtype=np.float32)
    t = pos / np.float32(l - 1)
    bands = np.linspace(1e-4, HY_BANDS - 1, HY_BANDS, dtype=np.float32)
    ang = np.float32(2.0 * math.pi / l) * pos[:, None] * bands[None, :]
    feats = np.concatenate([t[:, None], np.cos(ang), -np.sin(ang)], axis=-1).astype(np.float32)
    feats = np.pad(feats, ((0, 0), (0, LANES - HY_EMB)))
    max_decay = math.log(HY_TARGET) / HY_FAST_DECAY
    min_decay = math.log(HY_TARGET) / HY_SLOW_DECAY
    deltas = np.abs(np.linspace(min_decay, max_decay, D_HY, dtype=np.float32))
    dec = np.exp(-t[:, None] * deltas[None, :]).astype(np.float32)
    spec = jax.ShapeDtypeStruct((HY_ORDER, l, D_HY), F32)
    names = ("hy_w1", "hy_b1", "hy_w2", "hy_b2", "hy_w3", "hy_freq")
    return pl.pallas_call(
        functools.partial(_hy_filter_kernel, l),
        grid=(1,),
        in_specs=[_full(feats.shape), _full(dec.shape)] + [_layer_const(wts[n], li) for n in names]
                 + [_full(fwd_hi.shape), _full(fwd_lo.shape)],
        out_specs=[_full(spec.shape)] * 3,
        out_shape=[spec] * 3,
        compiler_params=_cparams("arbitrary"),
        name="hyena_filter",
    )(jnp.asarray(feats), jnp.asarray(dec), *[wts[n] for n in names], fwd_hi, fwd_lo)


def _hy_kernel(nc, ng, u_ref, cw_ref, cb_ref, f_ref, g_ref, a_ref, b_ref, d_ref, bias_ref, y_ref,
               uc_scr, vh_scr, vl_scr, sh_scr, sl_scr, z_scr):
    l = nc * CHUNK
    r = DFT_BLOCK
    nblk = l // r
    cw = cw_ref[...]
    cb = cb_ref[...]
    dot = lambda p, q: jnp.dot(p, q, preferred_element_type=F32)
    wide = lambda x: jnp.concatenate([x] * ng, axis=1)
    seq_cols = lambda g: slice(g * D_HY, (g + 1) * D_HY)

    def for_blocks(body):
        if nblk == 1:
            body(0)
        else:
            lax.fori_loop(0, nblk, lambda i, carry: (body(i), carry)[1], 0)

    def conv(c, carry):
        rows = _rows(c)
        for g in range(ng):
            uc = _conv3_chunk(u_ref.at[g], c, nc, cw, cb)
            uc_scr[g, rows, :] = uc
            vh_scr[rows, seq_cols(g)], vl_scr[rows, seq_cols(g)] = _split(uc[:, 0:D_HY])
        return carry

    lax.fori_loop(0, nc, conv, 0)

    def long_conv(o):
        def spectrum(i):
            rows = pl.ds(pl.multiple_of(i * r, r), r)
            rows_im = pl.ds(pl.multiple_of(l + i * r, r), r)
            p = dot(f_ref[i], vh_scr[...])
            pl_ = dot(f_ref[i, 0:2 * r, :], vl_scr[...])
            zr = p[0:r] + (p[2 * r:3 * r] + pl_[0:r])
            zi = p[r:2 * r] + (p[3 * r:4 * r] + pl_[r:2 * r])
            fa, fb, fd = wide(a_ref[o, rows, :]), wide(b_ref[o, rows, :]), wide(d_ref[o, rows, :])
            sh_scr[rows, :], sl_scr[rows, :] = _split(zr * fa - zi * fb)
            sh_scr[rows_im, :], sl_scr[rows_im, :] = _split(zr * fb + zi * fd)

        for_blocks(spectrum)

        def inverse(i):
            rows = pl.ds(pl.multiple_of(i * r, r), r)
            p = dot(g_ref[i], sh_scr[...])
            y = p[0:r] + (p[r:2 * r] + dot(g_ref[i, 0:r, :], sl_scr[...]))
            gate = jnp.concatenate([uc_scr[g, rows, (o + 1) * D_HY:(o + 2) * D_HY] for g in range(ng)], axis=1)
            if o == 0:
                v = jnp.concatenate([uc_scr[g, rows, 0:D_HY] for g in range(ng)], axis=1)
            else:
                v = z_scr[rows, :]
            out = gate * (y + v * wide(bias_ref[o:o + 1, :]))
            if o + 1 < HY_ORDER:
                z_scr[rows, :] = out
                vh_scr[rows, :], vl_scr[rows, :] = _split(out)
            else:
                for g in range(ng):
                    y_ref[g, rows, :] = out[:, seq_cols(g)]

        for_blocks(inverse)

    for o in range(HY_ORDER):
        long_conv(o)


def _hyena(u, wts, li, tables, spectra, nb, l):
    nc = l // CHUNK
    ng = HY_GROUP
    a, b, d = spectra
    names = ("hy_conv_w", "hy_conv_b")
    y = pl.pallas_call(
        functools.partial(_hy_kernel, nc, ng),
        grid=(nb // ng,),
        in_specs=[pl.BlockSpec((ng, l, HY_IN), lambda i: (i, 0, 0))] + [_layer_const(wts[n], li) for n in names]
                 + [_const(tables[0].shape), _const(tables[1].shape),
                    _const((HY_ORDER, l, D_HY)), _const((HY_ORDER, l, D_HY)), _const((HY_ORDER, l, D_HY)),
                    _layer_const(wts["hy_bias"], li)],
        out_specs=pl.BlockSpec((ng, l, D_HY), lambda i: (i, 0, 0)),
        out_shape=jax.ShapeDtypeStruct((nb, l, D_HY), F32),
        scratch_shapes=[pltpu.VMEM((ng, l, HY_IN), F32),
                        pltpu.VMEM((l, ng * D_HY), BF16), pltpu.VMEM((l, ng * D_HY), BF16),
                        pltpu.VMEM((2 * l, ng * D_HY), BF16), pltpu.VMEM((2 * l, ng * D_HY), BF16),
                        pltpu.VMEM((l, ng * D_HY), F32)],
        compiler_params=_cparams("parallel"),
        name="hyena",
    )(u.reshape(nb, l, HY_IN), *[wts[n] for n in names], *tables, a, b, d, wts["hy_bias"])
    return y.reshape(nb * l, D_HY)


def _seg_rms(x, w):
    return x * lax.rsqrt(_seg_mean(x * x, HEAD_DIM) + NORM_EPS) * w


def _stack_q(q):
    lo = lax.broadcasted_iota(jnp.int32, (1, D_KV), 1) < HEAD_DIM
    qa = q[:, 0:D_KV]
    qb = q[:, D_KV:2 * D_KV]
    return jnp.concatenate([jnp.where(lo, qa, 0.0), jnp.where(lo, pltpu.roll(qa, HEAD_DIM, 1), 0.0),
                            jnp.where(lo, 0.0, pltpu.roll(qb, HEAD_DIM, 1)), jnp.where(lo, 0.0, qb)], axis=0)


def _unstack_o(o):
    r = o.shape[0] // ATT_HEADS
    lo = lax.broadcasted_iota(jnp.int32, (1, D_KV), 1) < HEAD_DIM
    ya = jnp.where(lo, o[0:r], pltpu.roll(o[r:2 * r], HEAD_DIM, 1))
    yb = jnp.where(lo, pltpu.roll(o[2 * r:3 * r], HEAD_DIM, 1), o[3 * r:4 * r])
    return jnp.concatenate([ya, yb], axis=1)


def _sink_col(sink_ref, li, r):
    rb = lax.broadcasted_iota(jnp.int32, (ATT_HEADS * r, 1), 0) // r
    col = jnp.full((ATT_HEADS * r, 1), sink_ref[li * ATT_HEADS + ATT_HEADS - 1], F32)
    for h in range(ATT_HEADS - 2, -1, -1):
        col = jnp.where(rb == h, sink_ref[li * ATT_HEADS + h], col)
    return col


def _ctx_attn_kernel(li, ng, u_ref, qn_ref, kn_ref, sink_ref, *rest):
    y_ref, k_ref, v_ref = rest[-3:]
    l = u_ref.shape[1]
    sink = _sink_col(sink_ref, li, l)
    for g in range(ng):
        u = u_ref[g]
        q = _seg_rms(u[:, 0:D_ATT], qn_ref[...])
        k = _seg_rms(u[:, D_ATT:D_ATT + D_KV], kn_ref[:, 0:D_KV])
        v = u[:, D_ATT + D_KV:D_ATT + 2 * D_KV]
        k_ref[g] = k
        v_ref[g] = v
        s = _bdot_nt(_stack_q(q), k) * (HEAD_DIM ** -0.5)
        m = jnp.maximum(jnp.max(s, axis=-1, keepdims=True), sink)
        p = jnp.exp(s - m)
        den = jnp.sum(p, axis=-1, keepdims=True) + jnp.exp(sink - m)
        y_ref[g] = _unstack_o(_bdot(p * (1.0 / den), v))


def _ctx_attention(u, wts, li, nb, l, prev=None):
    ng = ATT_GROUP
    kv_spec, kv_shape = _slot_out((l, D_KV), nb, li, True, ng)
    args = [u.reshape(nb, l, ATT_IN), wts["attn_q_norm"], wts["attn_k_norm"], wts["attn_sink"]]
    in_specs = [_seq_spec(l, ATT_IN, ng), _layer_const(wts["attn_q_norm"], li),
                _layer_const(wts["attn_k_norm"], li), pl.BlockSpec(memory_space=pltpu.SMEM)]
    aliases = _carry_prev(args, in_specs, [] if prev is None else list(prev), 1)
    y, k, v = pl.pallas_call(
        functools.partial(_ctx_attn_kernel, li, ng),
        grid=(nb // ng,),
        in_specs=in_specs,
        out_specs=[_seq_spec(l, D_ATT, ng), kv_spec, kv_spec],
        out_shape=[jax.ShapeDtypeStruct((nb, l, D_ATT), F32), kv_shape, kv_shape],
        input_output_aliases=aliases,
        compiler_params=_cparams("parallel"),
        name="ctx_attention",
    )(*args)
    return y.reshape(nb * l, D_ATT), k, v


@functools.lru_cache(maxsize=None)
def _rope_tables_host(l):
    n_rows = l // GRID_W
    rows = np.repeat(np.arange(n_rows, dtype=np.float32), GRID_W)
    cols = np.tile(np.arange(GRID_W, dtype=np.float32), n_rows)
    nf = HEAD_DIM // 4
    inv = (np.float32(ROPE_BASE) ** (-np.arange(nf, dtype=np.float32) / np.float32(nf))).astype(np.float32)
    ar = rows[:, None] * inv[None, :]
    ac = cols[:, None] * inv[None, :]
    cos = np.concatenate([np.cos(ar), np.cos(ar), np.cos(ac), np.cos(ac)], axis=-1)
    sin = np.concatenate([-np.sin(ar), np.sin(ar), -np.sin(ac), np.sin(ac)], axis=-1)
    return (np.tile(cos, (1, ATT_HEADS)).astype(np.float32), np.tile(sin, (1, ATT_HEADS)).astype(np.float32))


def _rope(x, cos, sin):
    w = x.shape[-1]
    nf = HEAD_DIM // 4
    lane = lax.broadcasted_iota(jnp.int32, x.shape, 1)
    first = (lane % (2 * nf)) < nf
    partner = jnp.where(first, pltpu.roll(x, w - nf, 1), pltpu.roll(x, nf, 1))
    return x * cos + partner * sin


def _lat_attn_kernel(nblk, li, u_ref, qn_ref, kn_ref, cos_ref, sin_ref, ck_ref, cv_ref, sink_ref,
                     y_ref, q_scr, k_scr, v_scr):
    blk = ATT_BLOCK
    l = nblk * blk
    cos_q = cos_ref[...]
    sin_q = sin_ref[...]
    u = u_ref[...]
    q = _seg_rms(u[:, 0:D_ATT], qn_ref[...])
    k = _seg_rms(u[:, D_ATT:D_ATT + D_KV], kn_ref[:, 0:D_KV])
    q_scr[...] = _rope(q, cos_q, sin_q)
    zeros = jnp.zeros((blk, D_KV), F32)
    k_scr[0:blk, :] = zeros
    k_scr[blk + l:2 * blk + l, :] = zeros
    v_scr[0:blk, :] = zeros
    v_scr[blk + l:2 * blk + l, :] = zeros
    k_scr[blk:blk + l, :] = _rope(k, cos_q[:, 0:D_KV], sin_q[:, 0:D_KV])
    v_scr[blk:blk + l, :] = u[:, D_ATT + D_KV:D_ATT + 2 * D_KV]

    scale = HEAD_DIM ** -0.5
    r = lax.broadcasted_iota(jnp.int32, (blk, 3 * blk), 0)
    cidx = lax.broadcasted_iota(jnp.int32, (blk, 3 * blk), 1)
    band = (cidx - r >= blk - WINDOW) & (cidx - r <= blk + WINDOW)
    sink = _sink_col(sink_ref, li, blk)

    def block(i, carry):
        rows = _rows(i)
        win = pl.ds(pl.multiple_of(i * blk, blk), 3 * blk)
        kpos = cidx + (i - 1) * blk
        valid = band & (kpos >= 0) & (kpos < l)
        valid = jnp.concatenate([valid] * ATT_HEADS, axis=0)
        qs = _stack_q(q_scr[rows, :])
        s_loc = jnp.where(valid, _bdot_nt(qs, k_scr[win, :]) * scale, -jnp.inf)
        s_ctx = _bdot_nt(qs, ck_ref[...]) * scale
        m = jnp.maximum(jnp.maximum(jnp.max(s_loc, axis=-1, keepdims=True),
                                    jnp.max(s_ctx, axis=-1, keepdims=True)), sink)
        p_loc = jnp.exp(s_loc - m)
        p_ctx = jnp.exp(s_ctx - m)
        den = (jnp.sum(p_loc, axis=-1, keepdims=True) + jnp.sum(p_ctx, axis=-1, keepdims=True)
               + jnp.exp(sink - m))
        inv = 1.0 / den
        y_ref[rows, :] = _unstack_o(_bdot(p_ctx * inv, cv_ref[...]) + _bdot(p_loc * inv, v_scr[win, :]))
        return carry

    lax.fori_loop(0, nblk, block, 0)


def _lat_attention(u, wts, li, ck, cv, nb, l):
    lc = ck.shape[2]
    nblk = l // ATT_BLOCK
    cos, sin = _rope_tables_host(l)
    cache_spec = pl.BlockSpec((None, None, lc, D_KV), lambda b: (b, li, 0, 0))
    y = pl.pallas_call(
        functools.partial(_lat_attn_kernel, nblk, li),
        grid=(nb,),
        in_specs=[_seq_spec(l, ATT_IN), _layer_const(wts["attn_q_norm"], li), _layer_const(wts["attn_k_norm"], li),
                  _const((l, D_ATT)), _const((l, D_ATT)), cache_spec, cache_spec,
                  pl.BlockSpec(memory_space=pltpu.SMEM)],
        out_specs=_seq_spec(l, D_ATT),
        out_shape=jax.ShapeDtypeStruct((nb, l, D_ATT), F32),
        scratch_shapes=[pltpu.VMEM((l, D_ATT), F32), pltpu.VMEM((l + 2 * ATT_BLOCK, D_KV), F32),
                        pltpu.VMEM((l + 2 * ATT_BLOCK, D_KV), F32)],
        compiler_params=_cparams("parallel"),
        name="lat_attention",
    )(u.reshape(nb, l, ATT_IN), wts["attn_q_norm"], wts["attn_k_norm"], jnp.asarray(cos), jnp.asarray(sin),
      ck, cv, wts["attn_sink"])
    return y.reshape(nb * l, D_ATT)


def _pack_mix_w_in(w):
    c0 = D_SSM + SSM_CONV_CH
    n_dt = 2 * SSM_HEADS
    zeros = jnp.zeros(w.shape[:-1] + (DT_PAD - DT_REP * n_dt,), w.dtype)
    dt_cols = [w[..., c0:c0 + n_dt]] * DT_REP
    return jnp.concatenate([w[..., :c0], w[..., c0 + n_dt:]] + dt_cols + [zeros], axis=-1).astype(BF16)


def _prep_weights(p):
    row = lambda a: a.reshape(DEPTH, 1, -1)
    pad_lanes = lambda a: jnp.pad(row(a), ((0, 0), (0, 0), (0, LANES - a[0].size)))
    return dict(
        norm_w=p["norm_w"].reshape(DEPTH * 3, 1, D_MODEL),
        ffn_w_in=p["ffn_w_in"].astype(BF16).reshape(DEPTH * 2, D_MODEL, 2 * D_FF),
        ffn_w_out=p["ffn_w_out"].astype(BF16).reshape(DEPTH * 2, D_FF, D_MODEL),
        mix_w_in=_pack_mix_w_in(p["mix_w_in"]), mix_w_out=p["mix_w_out"].astype(BF16),
        ssd_conv_w=p["ssd_conv_w"], ssd_conv_b=row(p["ssd_conv_b"]),
        ssd_dt_bias=pad_lanes(jnp.tile(row(p["ssd_dt_bias"]), (1, 1, DT_REP))),
        ssd_a_log=pad_lanes(jnp.tile(row(p["ssd_a_log"]), (1, 1, DT_REP))),
        ssd_d=row(jnp.repeat(p["ssd_d"], SSM_HEAD_DIM, axis=-1)), ssd_norm_w=row(p["ssd_norm_w"]),
        hy_conv_w=p["hy_conv_w"], hy_conv_b=row(p["hy_conv_b"]), hy_bias=p["hy_bias"],
        hy_w1=jnp.pad(p["hy_w1"], ((0, 0), (0, LANES - HY_EMB), (0, 0))), hy_b1=row(p["hy_b1"]),
        hy_w2=p["hy_w2"], hy_b2=row(p["hy_b2"]), hy_w3=p["hy_w3"], hy_freq=row(p["hy_freq"]),
        ret_decay_logit=pad_lanes(p["ret_decay_logit"]), ret_gn_w=row(p["ret_gn_w"]),
        attn_q_norm=row(jnp.tile(p["attn_q_norm"], (1, ATT_HEADS))),
        attn_k_norm=row(jnp.tile(p["attn_k_norm"], (1, ATT_HEADS))),
        attn_sink=p["attn_sink"].reshape(DEPTH * ATT_HEADS),
    )


def _layer(x, mod, row0, wts, li, nb, l, ssd_s0, ret_s0, ctx_kv, hy_tables, hy_spectra, carried):
    ctx = ctx_kv is None
    rpm = nb * l if ctx else l
    x = _ffn(x, mod, row0, rpm, wts, li, 0)
    z, xbc, hy, ret, att, dt = _inproj(x, mod, row0, rpm, wts, li)
    y_ssd, s_ssd = _ssd(z, xbc, dt, wts, li, ssd_s0, nb, l, stacked=ctx, prev=carried.get("ssd"))
    y_hy = _hyena(hy, wts, li, hy_tables, hy_spectra, nb, l)
    y_ret, s_ret = _retention(ret, wts, li, ret_s0, nb, l, stacked=ctx, prev=carried.get("ret"))
    if ctx:
        y_att, k, v = _ctx_attention(att, wts, li, nb, l, prev=carried.get("kv"))
        carried = dict(ssd=s_ssd, ret=s_ret, kv=(k, v))
    else:
        y_att = _lat_attention(att, wts, li, ctx_kv[0], ctx_kv[1], nb, l)
    x = _ffn(x, mod, row0, rpm, wts, li, 1, mix=(y_ssd, y_hy, y_ret, y_att))
    return x, carried


def kernel(x_prompt, x_sample, cache_k, cache_v, state_ssd, state_ret, c, c_ctx, w_mod, b_mod, norm_w, ffn_w_in, ffn_w_out, mix_w_in, mix_w_out, ssd_conv_w, ssd_conv_b, ssd_dt_bias, ssd_a_log, ssd_d, ssd_norm_w, hy_conv_w, hy_conv_b, hy_w1, hy_b1, hy_w2, hy_b2, hy_w3, hy_freq, hy_bias, ret_decay_logit, ret_gn_w, attn_q_norm, attn_k_norm, attn_sink):
    bp, lp_len, _ = x_prompt.shape
    bs, ls_len, _ = x_sample.shape
    lc = cache_k.shape[2]

    cond = jnp.concatenate([c_ctx[None, :], c, jnp.zeros((MOD_ROWS - 1 - bs, D_MODEL), F32)], axis=0)
    mod = _modulation(cond, w_mod, b_mod)
    wts = _prep_weights(dict(
        norm_w=norm_w, ffn_w_in=ffn_w_in, ffn_w_out=ffn_w_out, mix_w_in=mix_w_in, mix_w_out=mix_w_out,
        ssd_conv_w=ssd_conv_w, ssd_conv_b=ssd_conv_b, ssd_dt_bias=ssd_dt_bias, ssd_a_log=ssd_a_log, ssd_d=ssd_d,
        ssd_norm_w=ssd_norm_w, hy_conv_w=hy_conv_w, hy_conv_b=hy_conv_b, hy_w1=hy_w1, hy_b1=hy_b1, hy_w2=hy_w2,
        hy_b2=hy_b2, hy_w3=hy_w3, hy_freq=hy_freq, hy_bias=hy_bias, ret_decay_logit=ret_decay_logit,
        ret_gn_w=ret_gn_w, attn_q_norm=attn_q_norm, attn_k_norm=attn_k_norm, attn_sink=attn_sink))
    ck = cache_k.reshape(bs, DEPTH, lc, D_KV)
    cv = cache_v.reshape(bs, DEPTH, lc, D_KV)

    tab_p = _dft_tables(lp_len)
    tab_s = _dft_tables(ls_len)
    blk_p = _dft_block_tables(tab_p, lp_len)
    blk_s = _dft_block_tables(tab_s, ls_len)

    yp = x_prompt.reshape(bp * lp_len, D_MODEL)
    ys = x_sample.reshape(bs * ls_len, D_MODEL)
    carried = dict(
        ssd=jnp.zeros((bp, DEPTH, 2, SSM_HEADS, SSM_STATE, SSM_HEAD_DIM), F32),
        ret=jnp.zeros((bp, DEPTH, 2, RET_HEADS, RET_HEAD_DIM, RET_HEAD_DIM), F32),
        kv=(jnp.zeros((bp, DEPTH, lp_len, D_KV), F32), jnp.zeros((bp, DEPTH, lp_len, D_KV), F32)))
    for li in range(DEPTH):
        spec_p = _hy_filter(lp_len, tab_p[0], tab_p[1], wts, li)
        spec_s = _hy_filter(ls_len, tab_s[0], tab_s[1], wts, li)
        yp, carried = _layer(yp, mod, li * MOD_ROWS, wts, li, bp, lp_len, None, None, None, blk_p, spec_p, carried)
        ys, _ = _layer(ys, mod, li * MOD_ROWS + 1, wts, li, bs, ls_len, state_ssd, state_ret, (ck, cv),
                       blk_s, spec_s, {})

    kv_shape = (bp, DEPTH, lp_len, ATT_KV_HEADS, HEAD_DIM)
    new_k, new_v = carried["kv"]
    return (yp.reshape(bp, lp_len, D_MODEL), ys.reshape(bs, ls_len, D_MODEL),
            new_k.reshape(kv_shape), new_v.reshape(kv_shape), carried["ssd"], carried["ret"])
```

```python
import functools
import math

import numpy as np
import jax
import jax.numpy as jnp
from jax import lax
from jax.experimental import pallas as pl
from jax.experimental.pallas import tpu as pltpu

F32 = jnp.float32
BF16 = jnp.bfloat16
HI = lax.Precision.HIGHEST

D_MODEL = 1024
DEPTH = 2
GRID_W = 64
D_FF = 2816
N_MOD = 9
NORM_EPS = 1e-6
CHUNK = 128
D_SSM = 256
SSM_HEADS = 4
SSM_HEAD_DIM = 64
SSM_STATE = 128
SSM_GROUPS = 2
SSM_CONV_CH = D_SSM + 2 * SSM_GROUPS * SSM_STATE
D_HY = 256
HY_ORDER = 2
HY_BANDS = 16
HY_EMB = 1 + 2 * HY_BANDS
HY_HIDDEN = 64
HY_FAST_DECAY = 0.3
HY_SLOW_DECAY = 1.5
HY_TARGET = 1e-2
HY_IN = (HY_ORDER + 1) * D_HY
D_RET = 256
RET_HEADS = 4
RET_HEAD_DIM = 64
RET_IN = 4 * D_RET
ATT_HEADS = 4
ATT_KV_HEADS = 2
HEAD_DIM = 64
D_ATT = ATT_HEADS * HEAD_DIM
D_KV = ATT_KV_HEADS * HEAD_DIM
ATT_IN = D_ATT + 2 * D_KV
WINDOW = 128
ATT_BLOCK = 128
ROPE_BASE = 10000.0
D_MIX = D_SSM + D_HY + D_RET + D_ATT

LANES = 128
DT_PAD = LANES
D_IN_PAD = D_SSM + SSM_CONV_CH + HY_IN + RET_IN + ATT_IN + DT_PAD
VMEM_LIMIT = 56 * 1024 * 1024
MOD_ROWS = 8

TOKEN_TILE = 512
DFT_BLOCK = 256
HY_GROUP = 2
SEQ_GROUP = 2
ATT_GROUP = 1


def _cparams(*sem):
    return pltpu.CompilerParams(dimension_semantics=sem, vmem_limit_bytes=VMEM_LIMIT)


def _rms(x, w):
    return x * lax.rsqrt(jnp.mean(x * x, axis=-1, keepdims=True) + NORM_EPS) * w


def _silu(x):
    return x * (1.0 / (1.0 + jnp.exp(-x)))


def _softplus(x):
    return jnp.maximum(x, 0.0) + jnp.log1p(jnp.exp(-jnp.abs(x)))


def _bdot(a, b):
    return jnp.dot(a.astype(BF16), b.astype(BF16), preferred_element_type=F32)


def _bdot_nt(a, b):
    return lax.dot_general(a.astype(BF16), b.astype(BF16), (((1,), (1,)), ((), ())),
                           preferred_element_type=F32)


def _bdot_tn(a, b):
    return lax.dot_general(a.astype(BF16), b.astype(BF16), (((0,), (0,)), ((), ())),
                           preferred_element_type=F32)


def _hdot(a, b):
    return jnp.dot(a, b, preferred_element_type=F32, precision=HI)


def _full(shape):
    n = len(shape)
    return pl.BlockSpec(shape, lambda *_: (0,) * n)


def _const(shape):
    n = len(shape)
    return pl.BlockSpec(shape, lambda *_: (0,) * n, pipeline_mode=pl.Buffered(1))


def _layer_const(arr, li):
    tail = arr.shape[1:]
    zeros = (0,) * len(tail)
    return pl.BlockSpec((None,) + tail, lambda *_: (li,) + zeros, pipeline_mode=pl.Buffered(1))


def _mod_spec(k, tm, rows_per_mod, row0):
    return pl.BlockSpec((None, 1, D_MODEL), lambda i: (row0 + (i * tm) // rows_per_mod, 0, k))


def _seq_spec(l, w, ng=None):
    return pl.BlockSpec((ng, l, w), lambda b: (b, 0, 0))


def _head_masks(width, heads):
    lane = lax.broadcasted_iota(jnp.int32, (1, width), 1)
    hd = width // heads
    return [(lane >= h * hd) & (lane < (h + 1) * hd) for h in range(heads)]


def _by_head(masks, vals):
    out = vals[-1]
    for m, v in zip(masks[-2::-1], vals[-2::-1]):
        out = jnp.where(m, v, out)
    return out


def _block_diag(n, blk, value):
    i = lax.broadcasted_iota(jnp.int32, (n, n), 0) // blk
    j = lax.broadcasted_iota(jnp.int32, (n, n), 1) // blk
    return jnp.where(i == j, value, 0.0).astype(F32)


def _mod_kernel(c_ref, w_ref, b_ref, o_ref):
    c = c_ref[...]
    o_ref[...] = _bdot(_silu(c), w_ref[...]) + b_ref[...]


def _modulation(cond, w_mod, b_mod):
    out = pl.pallas_call(
        _mod_kernel,
        grid=(DEPTH, N_MOD),
        in_specs=[pl.BlockSpec((MOD_ROWS, D_MODEL), lambda l, j: (0, 0)),
                  pl.BlockSpec((None, D_MODEL, D_MODEL), lambda l, j: (l, 0, j)),
                  pl.BlockSpec((None, 1, D_MODEL), lambda l, j: (l, 0, j))],
        out_specs=pl.BlockSpec((None, MOD_ROWS, D_MODEL), lambda l, j: (l, 0, j)),
        out_shape=jax.ShapeDtypeStruct((DEPTH, MOD_ROWS, N_MOD * D_MODEL), F32),
        compiler_params=_cparams("arbitrary", "arbitrary"),
        name="modulation",
    )(cond, w_mod, b_mod.reshape(DEPTH, 1, N_MOD * D_MODEL))
    return out.reshape(DEPTH * MOD_ROWS, 1, N_MOD * D_MODEL)


def _ffn_kernel(n_mix, x_ref, sh_ref, sc_ref, g_ref, nw_ref, wi_ref, wo_ref, *rest):
    o_ref = rest[-1]
    x = x_ref[...]
    if n_mix:
        gm_ref, wm_ref = rest[0], rest[1 + n_mix]
        w = D_MIX // n_mix
        acc = _bdot(rest[1][...], wm_ref[0:w, :])
        for j in range(1, n_mix):
            acc += _bdot(rest[1 + j][...], wm_ref[j * w:(j + 1) * w, :])
        x = x + gm_ref[...] * acc
    h = (_rms(x, nw_ref[...]) * (1.0 + sc_ref[...]) + sh_ref[...]).astype(BF16)
    gate = jnp.dot(h, wi_ref[:, 0:D_FF], preferred_element_type=F32)
    up = jnp.dot(h, wi_ref[:, D_FF:2 * D_FF], preferred_element_type=F32)
    o_ref[...] = x + 0.5 * g_ref[...] * _bdot(_silu(gate) * up, wo_ref[...])


def _ffn(x, mod, row0, rows_per_mod, wts, li, k, mix=()):
    t = x.shape[0]
    tm = min(TOKEN_TILE, rows_per_mod)
    w_in, w_out = wts["ffn_w_in"], wts["ffn_w_out"]
    args = [x, mod, mod, mod, wts["norm_w"], w_in, w_out]
    in_specs = [pl.BlockSpec((tm, D_MODEL), lambda i: (i, 0)),
                _mod_spec(6 * k, tm, rows_per_mod, row0),
                _mod_spec(6 * k + 1, tm, rows_per_mod, row0),
                _mod_spec(6 * k + 2, tm, rows_per_mod, row0),
                _layer_const(wts["norm_w"], 3 * li + 2 * k),
                _layer_const(w_in, 2 * li + k), _layer_const(w_out, 2 * li + k)]
    if mix:
        args += [mod, *mix, wts["mix_w_out"]]
        in_specs += [_mod_spec(5, tm, rows_per_mod, row0)]
        in_specs += [pl.BlockSpec((tm, y.shape[1]), lambda i: (i, 0)) for y in mix]
        in_specs += [_layer_const(wts["mix_w_out"], li)]
    return pl.pallas_call(
        functools.partial(_ffn_kernel, len(mix)),
        grid=(t // tm,),
        in_specs=in_specs,
        out_specs=pl.BlockSpec((tm, D_MODEL), lambda i: (i, 0)),
        out_shape=jax.ShapeDtypeStruct((t, D_MODEL), F32),
        compiler_params=_cparams("parallel"),
        name="ffn",
    )(*args)


_IN_SPLITS = (("z", D_SSM), ("xbc", SSM_CONV_CH), ("hy", HY_IN), ("ret", RET_IN), ("att", ATT_IN), ("dt", DT_PAD))


def _inproj_kernel(x_ref, sh_ref, sc_ref, nw_ref, w_ref, *o_refs):
    h = (_rms(x_ref[...], nw_ref[...]) * (1.0 + sc_ref[...]) + sh_ref[...]).astype(BF16)
    off = 0
    for (_, width), o_ref in zip(_IN_SPLITS, o_refs):
        o_ref[...] = jnp.dot(h, w_ref[:, off:off + width], preferred_element_type=F32)
        off += width


def _inproj(x, mod, row0, rows_per_mod, wts, li):
    t = x.shape[0]
    tm = min(TOKEN_TILE, rows_per_mod)
    return pl.pallas_call(
        _inproj_kernel,
        grid=(t // tm,),
        in_specs=[pl.BlockSpec((tm, D_MODEL), lambda i: (i, 0)),
                  _mod_spec(3, tm, rows_per_mod, row0),
                  _mod_spec(4, tm, rows_per_mod, row0),
                  _layer_const(wts["norm_w"], 3 * li + 1),
                  _layer_const(wts["mix_w_in"], li)],
        out_specs=[pl.BlockSpec((tm, width), lambda i: (i, 0)) for _, width in _IN_SPLITS],
        out_shape=[jax.ShapeDtypeStruct((t, width), F32) for _, width in _IN_SPLITS],
        compiler_params=_cparams("parallel"),
        name="mix_in",
    )(x, mod, mod, wts["norm_w"], wts["mix_w_in"])


def _conv3_chunk(x_ref, c, nc, w, b):
    q = CHUNK
    l = nc * q
    r0 = pl.multiple_of(c * q, q)
    x = x_ref[pl.ds(r0, q), :]
    prev = x_ref[pl.ds(jnp.maximum(r0 - 1, 0), 1), :]
    nxt = x_ref[pl.ds(jnp.minimum(r0 + q, l - 1), 1), :]
    prev = jnp.where(c > 0, prev, 0.0)
    nxt = jnp.where(c < nc - 1, nxt, 0.0)
    rid = lax.broadcasted_iota(jnp.int32, (q, 1), 0)
    xm1 = jnp.where(rid == 0, prev, pltpu.roll(x, 1, 0))
    xp1 = jnp.where(rid == q - 1, nxt, pltpu.roll(x, q - 1, 0))
    return xm1 * w[0:1, :] + x * w[1:2, :] + xp1 * w[2:3, :] + b


def _tri(lower):
    i = lax.broadcasted_iota(jnp.int32, (CHUNK, CHUNK), 0)
    j = lax.broadcasted_iota(jnp.int32, (CHUNK, CHUNK), 1)
    return (j <= i) if lower else (j >= i)


def _rows(c):
    return pl.ds(pl.multiple_of(c * CHUNK, CHUNK), CHUNK)


def _split_cat(v, parts, axis):
    out, r = [], v
    for i in range(parts):
        piece = r.astype(BF16)
        out.append(piece)
        if i + 1 < parts:
            r = r - piece.astype(F32)
    return jnp.concatenate(out, axis=axis)


def _seg_mean(x, seg):
    w = x.shape[-1]
    ones = _block_diag(w, seg, 1.0).astype(BF16)
    return jnp.dot(_split_cat(x, 2, axis=1), jnp.concatenate([ones, ones], axis=0),
                   preferred_element_type=F32) * (1.0 / seg)


def _stack_heads(x, masks):
    return jnp.concatenate([jnp.where(m, x, 0.0) for m in masks], axis=0)


def _slot_out(tail, nb, li, stacked, ng=None):
    zeros = (0,) * len(tail)
    slot = li if stacked else 0
    spec = pl.BlockSpec((ng, None) + tail, lambda b: (b, slot) + zeros)
    return spec, jax.ShapeDtypeStruct((nb, DEPTH if stacked else 1) + tail, F32)


def _carry_prev(args, in_specs, prevs, first_out):
    aliases = {}
    for j, prev in enumerate(prevs):
        aliases[len(args)] = first_out + j
        args.append(prev)
        in_specs.append(pl.BlockSpec(memory_space=pl.ANY))
    return aliases


SSD_SEL_W = 2 * SSM_HEADS * SSM_STATE
CUM_PIECES = 3
WGT_PIECES = 2
DT_REP = CUM_PIECES + WGT_PIECES


def _ssd_kernel(nc, ng, has_s0, *refs):
    refs = list(refs)
    (z_ref, xbc_ref, dt_ref, cw_ref, cb_ref, dtb_ref, alog_ref, dsk_ref, nw_ref) = refs[:9]
    s0_ref = refs[9] if has_s0 else None
    (y_ref, s_ref, xs_scr, xk_scr, xt_scr, dt_scr, cum_scr, dtt_scr, cumt_scr, yf_scr, yb_scr, st_scr,
     msk_scr, sel_scr) = refs[-14:]
    q = CHUNK
    nh, n, p = SSM_HEADS, SSM_STATE, SSM_HEAD_DIM
    rep = nh // SSM_GROUPS
    hm = _head_masks(D_SSM, nh)

    @pl.when(pl.program_id(0) == 0)
    def _():
        i = lax.broadcasted_iota(jnp.int32, (D_SSM, nh * n), 0) // p
        j = lax.broadcasted_iota(jnp.int32, (D_SSM, nh * n), 1) // n
        msk_scr[...] = jnp.where(i == j, 1.0, 0.0).astype(F32)
        k = lax.broadcasted_iota(jnp.int32, (LANES, SSD_SEL_W), 0)
        col = lax.broadcasted_iota(jnp.int32, (LANES, SSD_SEL_W), 1)
        is_cum = col < nh * n
        grp = k // (2 * nh)
        grp_ok = (is_cum & (grp < CUM_PIECES)) | ((~is_cum) & (grp >= CUM_PIECES) & (grp < DT_REP))
        hit = grp_ok & (k % nh == (col % (nh * n)) // n)
        sel_scr[...] = jnp.where(hit, 1.0, 0.0).astype(BF16)

    cw = cw_ref[...]
    cb = cb_ref[...]
    neg_a = -jnp.exp(alog_ref[...])
    dtb = dtb_ref[...]

    class Seq:
        def __init__(self, g):
            self.z, self.xbc, self.dt_in, self.y, self.s = (z_ref.at[g], xbc_ref.at[g], dt_ref.at[g], y_ref.at[g],
                                                            s_ref.at[g])
            self.s0 = s0_ref.at[g] if has_s0 else None
            self.xs, self.xk, self.xt, self.dt, self.cum = (xs_scr.at[g], xk_scr.at[g], xt_scr.at[g], dt_scr.at[g],
                                                            cum_scr.at[g])
            self.dtt, self.cumt, self.yf, self.yb, self.st = (dtt_scr.at[g], cumt_scr.at[g], yf_scr.at[g],
                                                              yb_scr.at[g], st_scr.at[g])

    seqs = [Seq(g) for g in range(ng)]

    def prep_seq(sq, c):
        rows = _rows(c)
        xall = _silu(_conv3_chunk(sq.xbc, c, nc, cw, cb))
        sq.xs[rows, :] = xall
        xs = xall[:, 0:D_SSM]
        sq.xk[c] = _stack_heads(xs, hm).astype(BF16)
        sq.xt[c] = xs.T.astype(BF16)
        dt = _softplus(sq.dt_in[rows, :] + dtb)
        la = _split_cat(dt * neg_a, 3, axis=0)
        tri = jnp.concatenate([_tri(True), _tri(False)], axis=0)
        tri = jnp.where(tri, 1.0, 0.0).astype(BF16)
        cs = jnp.dot(jnp.concatenate([tri] * 3, axis=1), la, preferred_element_type=F32)
        lane = lax.broadcasted_iota(jnp.int32, (1, LANES), 1)
        cum = jnp.where(lane % (2 * nh) < nh, cs[0:q], cs[q:2 * q])
        sq.dt[rows, :] = dt
        sq.cum[rows, :] = cum
        sq.dtt[c] = dt.T
        sq.cumt[c] = cum.T

    def prep(c, carry):
        for sq in seqs:
            prep_seq(sq, c)
        return carry

    lax.fori_loop(0, nc, prep, 0)

    for sq in seqs:
        for d in range(2):
            for h in range(nh):
                blk = slice(h * p, (h + 1) * p)
                if has_s0:
                    parts = [jnp.zeros((p, n), F32)] * nh
                    parts[h] = sq.s0[d, h].T
                    sq.st[d, blk, :] = jnp.concatenate(parts, axis=1)
                else:
                    sq.st[d, blk, :] = jnp.zeros((p, nh * n), F32)

    def pack_scalars(sq, c, d):
        rows = _rows(c)
        dt = sq.dt[rows, :]
        cum = sq.cum[rows, :]
        edge = q - 1 if d == 0 else 0
        lane = lax.broadcasted_iota(jnp.int32, (1, LANES), 1)
        used = (lane < DT_REP * 2 * nh) & ((lane % (2 * nh)) // nh == d)
        cum = jnp.where(used, cum, 0.0)
        wgt = jnp.exp(cum[edge:edge + 1, :] - cum) * dt
        grp = lane // (2 * nh)
        packed = jnp.zeros_like(cum)
        for src, pieces, g0 in ((cum, CUM_PIECES, 0), (wgt, WGT_PIECES, CUM_PIECES)):
            rest = src
            for i in range(pieces):
                piece = rest.astype(BF16).astype(F32)
                packed = jnp.where(grp == g0 + i, piece, packed)
                rest = rest - piece
        return jnp.where(used, packed, 0.0).astype(BF16)

    def scan_chunk(sq, c, d, e):
        rows = _rows(c)
        bm = [sq.xs[rows, D_SSM + g * n:D_SSM + (g + 1) * n] for g in range(SSM_GROUPS)]
        cm = [sq.xs[rows, D_SSM + (SSM_GROUPS + g) * n:D_SSM + (SSM_GROUPS + g + 1) * n]
              for g in range(SSM_GROUPS)]
        mask = _tri(d == 0)
        edge = q - 1 if d == 0 else 0
        cb_t = [_bdot_nt(cm[g], bm[g]) for g in range(SSM_GROUPS)]
        blocks, ecol = [], []
        for h in range(nh):
            r = d * nh + h
            col = e[:, h * n:(h + 1) * n]
            decay = jnp.exp(jnp.where(mask, col - sq.cumt[c, r:r + 1, :], -jnp.inf))
            blocks.append((cb_t[h // rep] * decay * sq.dtt[c, r:r + 1, :]).astype(BF16))
            ecol.append(jnp.exp(col))
        y = jnp.dot(jnp.concatenate(blocks, axis=1), sq.xk[c], preferred_element_type=F32)
        s_old = sq.st[d]
        c4 = jnp.concatenate([cm[h // rep] for h in range(nh)], axis=1)
        lo = lax.broadcasted_iota(jnp.int32, (1, n), 1) < p
        ecol = jnp.concatenate([jnp.where(lo, ecol[2 * j], ecol[2 * j + 1]) for j in range(nh // 2)], axis=1)
        y = y + ecol * _bdot_nt(c4, s_old)
        wk = jnp.concatenate([bm[h // rep] for h in range(nh)], axis=1) * e[:, nh * n:2 * nh * n]
        upd = jnp.dot(sq.xt[c], wk.astype(BF16), preferred_element_type=F32)
        sq.st[d] = jnp.exp(e[edge:edge + 1, 0:nh * n]) * s_old + upd * msk_scr[...]
        return y

    def scan(i, carry):
        j = nc - 1 - i
        packed = jnp.concatenate([pack_scalars(sq, c, d) for sq in seqs for c, d in ((i, 0), (j, 1))], axis=0)
        e = jnp.dot(packed, sel_scr[...], preferred_element_type=F32)
        for g, sq in enumerate(seqs):
            sq.yf[_rows(i), :] = scan_chunk(sq, i, 0, e[2 * g * q:(2 * g + 1) * q])
            sq.yb[_rows(j), :] = scan_chunk(sq, j, 1, e[(2 * g + 1) * q:(2 * g + 2) * q])
        return carry

    lax.fori_loop(0, nc, scan, 0)

    for sq in seqs:
        for d in range(2):
            for h in range(nh):
                sq.s[d, h] = sq.st[d, h * p:(h + 1) * p, h * n:(h + 1) * n].T

    dsk = dsk_ref[...]
    nw = nw_ref[...]

    def fin(c, carry):
        rows = _rows(c)
        for sq in seqs:
            y = sq.yf[rows, :] + sq.yb[rows, :] + dsk * sq.xs[rows, 0:D_SSM]
            sq.y[rows, :] = _rms(y * _silu(sq.z[rows, :]), nw)
        return carry

    lax.fori_loop(0, nc, fin, 0)


def _ssd(z, xbc, dt, wts, li, s0, nb, l, stacked=False, prev=None):
    nc = l // CHUNK
    has_s0 = s0 is not None
    names = ("ssd_conv_w", "ssd_conv_b", "ssd_dt_bias", "ssd_a_log", "ssd_d", "ssd_norm_w")
    ng = SEQ_GROUP
    args = [z.reshape(nb, l, D_SSM), xbc.reshape(nb, l, SSM_CONV_CH), dt.reshape(nb, l, DT_PAD)]
    args += [wts[n] for n in names]
    in_specs = [_seq_spec(l, D_SSM, ng), _seq_spec(l, SSM_CONV_CH, ng), _seq_spec(l, DT_PAD, ng)]
    in_specs += [_layer_const(wts[n], li) for n in names]
    if has_s0:
        args.append(s0)
        in_specs.append(pl.BlockSpec((ng, None, 2, SSM_HEADS, SSM_STATE, SSM_HEAD_DIM),
                                     lambda b: (b, li, 0, 0, 0, 0)))
    st_spec, st_shape = _slot_out((2, SSM_HEADS, SSM_STATE, SSM_HEAD_DIM), nb, li, stacked, ng)
    aliases = _carry_prev(args, in_specs, [] if prev is None else [prev], 1)
    y, s = pl.pallas_call(
        functools.partial(_ssd_kernel, nc, ng, has_s0),
        grid=(nb // ng,),
        in_specs=in_specs,
        out_specs=[_seq_spec(l, D_SSM, ng), st_spec],
        out_shape=[jax.ShapeDtypeStruct((nb, l, D_SSM), F32), st_shape],
        input_output_aliases=aliases,
        scratch_shapes=[pltpu.VMEM((ng, l, SSM_CONV_CH), F32),
                        pltpu.VMEM((ng, nc, SSM_HEADS * CHUNK, D_SSM), BF16),
                        pltpu.VMEM((ng, nc, D_SSM, CHUNK), BF16),
                        pltpu.VMEM((ng, l, DT_PAD), F32), pltpu.VMEM((ng, l, DT_PAD), F32),
                        pltpu.VMEM((ng, nc, DT_PAD, CHUNK), F32), pltpu.VMEM((ng, nc, DT_PAD, CHUNK), F32),
                        pltpu.VMEM((ng, l, D_SSM), F32), pltpu.VMEM((ng, l, D_SSM), F32),
                        pltpu.VMEM((ng, 2, D_SSM, SSM_HEADS * SSM_STATE), F32),
                        pltpu.VMEM((D_SSM, SSM_HEADS * SSM_STATE), F32),
                        pltpu.VMEM((LANES, SSD_SEL_W), BF16)],
        compiler_params=_cparams("arbitrary"),
        name="ssd",
    )(*args)
    return y.reshape(nb * l, D_SSM), s


def _ret_kernel(nc, ng, has_s0, *refs):
    refs = list(refs)
    u_ref, dl_ref, gn_ref = refs[:3]
    s0_ref = refs[3] if has_s0 else None
    y_ref, s_ref, yf_scr, yb_scr, st_scr, dm_scr, e_scr, ea_scr, bd_scr = refs[-9:]
    q = CHUNK
    hd = RET_HEAD_DIM
    hm = _head_masks(D_RET, RET_HEADS)

    @pl.when(pl.program_id(0) == 0)
    def _():
        log_g = -_softplus(-dl_ref[...])
        ii = lax.broadcasted_iota(jnp.int32, (q, q), 0)
        jj = lax.broadcasted_iota(jnp.int32, (q, q), 1)
        dij = (ii - jj).astype(F32)
        ri = lax.broadcasted_iota(jnp.int32, (q, 1), 0).astype(F32)
        lfs, lbs = [], []
        for h in range(RET_HEADS):
            lf = log_g[:, h:h + 1]
            lb = log_g[:, RET_HEADS + h:RET_HEADS + h + 1]
            lfs.append(lf)
            lbs.append(lb)
            d_f = jnp.exp(jnp.where(dij >= 0, dij * lf, -jnp.inf))
            d_b = jnp.exp(jnp.where(dij <= 0, -dij * lb, -jnp.inf))
            dm_scr[:, h * q:(h + 1) * q] = d_f + d_b
        lf_l = _by_head(hm, lfs)
        lb_l = _by_head(hm, lbs)
        e_scr[0] = jnp.exp((ri + 1.0) * lf_l)
        e_scr[1] = jnp.exp((q - ri) * lb_l)
        e_scr[2] = jnp.exp((q - 1.0 - ri) * lf_l)
        e_scr[3] = jnp.exp(ri * lb_l)
        ea_scr[0:1, :] = jnp.exp(q * lf_l)
        ea_scr[1:2, :] = jnp.exp(q * lb_l)
        bd_scr[...] = _block_diag(D_RET, hd, 1.0)

    class Seq:
        def __init__(self, g):
            self.u, self.y, self.s = u_ref.at[g], y_ref.at[g], s_ref.at[g]
            self.s0 = s0_ref.at[g] if has_s0 else None
            self.yf, self.yb, self.st = yf_scr.at[g], yb_scr.at[g], st_scr.at[g]

    seqs = [Seq(g) for g in range(ng)]

    for sq in seqs:
        for d in range(2):
            for h in range(RET_HEADS):
                blk = slice(h * hd, (h + 1) * hd)
                if has_s0:
                    parts = [jnp.zeros((hd, hd), F32)] * RET_HEADS
                    parts[h] = sq.s0[d, h]
                    sq.st[d, blk, :] = jnp.concatenate(parts, axis=1)
                else:
                    sq.st[d, blk, :] = jnp.zeros((hd, D_RET), F32)

    def qkv(sq, rows):
        return (sq.u[rows, 0:D_RET], sq.u[rows, D_RET:2 * D_RET] * (RET_HEAD_DIM ** -0.5),
                sq.u[rows, 2 * D_RET:3 * D_RET])

    def state_step(sq, d, qq, kk, vv):
        s_old = sq.st[d]
        y = e_scr[d] * _bdot(qq, s_old)
        sq.st[d] = ea_scr[d:d + 1, :] * s_old + _bdot_tn(kk * e_scr[2 + d], vv) * bd_scr[...]
        return y

    def scan(i, carry):
        rows = _rows(i)
        rows_b = _rows(nc - 1 - i)
        for sq in seqs:
            qq, kk, vv = qkv(sq, rows)
            sc = _bdot_nt(qq, _stack_heads(kk, hm)) * dm_scr[...]
            sq.yf[rows, :] = _bdot(sc, _stack_heads(vv, hm)) + state_step(sq, 0, qq, kk, vv)
            qq, kk, vv = qkv(sq, rows_b)
            sq.yb[rows_b, :] = state_step(sq, 1, qq, kk, vv)
        return carry

    lax.fori_loop(0, nc, scan, 0)

    for sq in seqs:
        for d in range(2):
            for h in range(RET_HEADS):
                sq.s[d, h] = sq.st[d, h * hd:(h + 1) * hd, h * hd:(h + 1) * hd]

    gn = gn_ref[...]

    def fin(c, carry):
        rows = _rows(c)
        for sq in seqs:
            y = sq.yf[rows, :] + sq.yb[rows, :]
            cen = y - _seg_mean(y, hd)
            var = _seg_mean(cen * cen, hd)
            sq.y[rows, :] = cen * lax.rsqrt(var + NORM_EPS) * gn * _silu(sq.u[rows, 3 * D_RET:4 * D_RET])
        return carry

    lax.fori_loop(0, nc, fin, 0)


def _retention(u, wts, li, s0, nb, l, stacked=False, prev=None):
    nc = l // CHUNK
    has_s0 = s0 is not None
    names = ("ret_decay_logit", "ret_gn_w")
    ng = SEQ_GROUP
    args = [u.reshape(nb, l, RET_IN)] + [wts[n] for n in names]
    in_specs = [_seq_spec(l, RET_IN, ng)] + [_layer_const(wts[n], li) for n in names]
    if has_s0:
        args.append(s0)
        in_specs.append(pl.BlockSpec((ng, None, 2, RET_HEADS, RET_HEAD_DIM, RET_HEAD_DIM),
                                     lambda b: (b, li, 0, 0, 0, 0)))
    st_spec, st_shape = _slot_out((2, RET_HEADS, RET_HEAD_DIM, RET_HEAD_DIM), nb, li, stacked, ng)
    aliases = _carry_prev(args, in_specs, [] if prev is None else [prev], 1)
    y, s = pl.pallas_call(
        functools.partial(_ret_kernel, nc, ng, has_s0),
        grid=(nb // ng,),
        in_specs=in_specs,
        out_specs=[_seq_spec(l, D_RET, ng), st_spec],
        out_shape=[jax.ShapeDtypeStruct((nb, l, D_RET), F32), st_shape],
        input_output_aliases=aliases,
        scratch_shapes=[pltpu.VMEM((ng, l, D_RET), F32), pltpu.VMEM((ng, l, D_RET), F32),
                        pltpu.VMEM((ng, 2, D_RET, D_RET), F32), pltpu.VMEM((CHUNK, RET_HEADS * CHUNK), F32),
                        pltpu.VMEM((4, CHUNK, D_RET), F32), pltpu.VMEM((8, D_RET), F32),
                        pltpu.VMEM((D_RET, D_RET), F32)],
        compiler_params=_cparams("arbitrary"),
        name="retention",
    )(*args)
    return y.reshape(nb * l, D_RET), s


def _split(x):
    hi = x.astype(BF16)
    return hi, (x - hi.astype(F32)).astype(BF16)


def _dot3(a_hi, a_lo, b_hi, b_lo):
    d = lambda p, q: jnp.dot(p, q, preferred_element_type=F32)
    return d(a_hi, b_hi) + (d(a_lo, b_hi) + d(a_hi, b_lo))


@functools.lru_cache(maxsize=None)
def _dft_fwd_host(l):
    n = 2 * l
    f = np.arange(l, dtype=np.int64)[:, None]
    s = np.arange(l, dtype=np.int64)[None, :]
    ang = ((f * s) % n).astype(np.float64) * (2.0 * math.pi / n)
    im = -np.sin(ang)
    im[0] = np.where(np.arange(l) % 2 == 0, 1.0, -1.0)
    return np.concatenate([np.cos(ang), im], axis=0).astype(np.float32)


def _dft_tables(l):
    n = 2 * l
    fwd = jnp.asarray(_dft_fwd_host(l))
    wgt = np.full((n, 1), 2.0 / n, np.float32)
    wgt[0] = wgt[l] = 1.0 / n
    return _split(fwd) + _split((fwd * wgt).T)


def _dft_block_tables(tables, l):
    fwd_hi, fwd_lo, inv_hi, inv_lo = tables
    r = DFT_BLOCK
    nblk = l // r
    fb = lambda t: (t[0:l].reshape(nblk, r, l), t[l:2 * l].reshape(nblk, r, l))
    fwd_blk = jnp.concatenate(fb(fwd_hi) + fb(fwd_lo), axis=1)
    inv_blk = jnp.concatenate([inv_hi.reshape(nblk, r, 2 * l), inv_lo.reshape(nblk, r, 2 * l)], axis=1)
    return fwd_blk, inv_blk


def _hy_filter_kernel(l, feats_ref, dec_ref, w1_ref, b1_ref, w2_ref, b2_ref, w3_ref, fr_ref, fh_ref, fl_ref,
                      a_ref, b_ref, d_ref):
    fr = fr_ref[...]
    h = jnp.sin(fr * (_hdot(feats_ref[...], w1_ref[...]) + b1_ref[...]))
    h = jnp.sin(fr * (_hdot(h, w2_ref[...]) + b2_ref[...]))
    h = _hdot(h, w3_ref[...])
    dec = jnp.concatenate([dec_ref[...]] * HY_ORDER, axis=-1)
    row0 = lax.broadcasted_iota(jnp.int32, (l, 1), 0) == 0
    hf = h[:, 0:HY_ORDER * D_HY] * dec
    hb = h[:, HY_ORDER * D_HY:2 * HY_ORDER * D_HY] * dec
    hb = jnp.where(row0, 0.0, hb)
    hs = _split(hf + hb)
    hd = _split(hf - hb)
    re = _dot3(fh_ref[0:l, :], fl_ref[0:l, :], *hs)
    ny = _dot3(fh_ref[l:l + 8, :], fl_ref[l:l + 8, :], *hs)[0:1]
    im = _dot3(fh_ref[l:2 * l, :], fl_ref[l:2 * l, :], *hd)
    for o in range(HY_ORDER):
        cols = slice(o * D_HY, (o + 1) * D_HY)
        a_ref[o] = re[:, cols]
        b_ref[o] = jnp.where(row0, 0.0, im[:, cols])
        d_ref[o] = jnp.where(row0, ny[:, cols], re[:, cols])


def _hy_filter(l, fwd_hi, fwd_lo, wts, li):
    pos = np.arange(l, dtype=np.float32)
    t = pos / np.float32(l - 1)
    bands = np.linspace(1e-4, HY_BANDS - 1, HY_BANDS, dtype=np.float32)
    ang = np.float32(2.0 * math.pi / l) * pos[:, None] * bands[None, :]
    feats = np.concatenate([t[:, None], np.cos(ang), -np.sin(ang)], axis=-1).astype(np.float32)
    feats = np.pad(feats, ((0, 0), (0, LANES - HY_EMB)))
    max_decay = math.log(HY_TARGET) / HY_FAST_DECAY
    min_decay = math.log(HY_TARGET) / HY_SLOW_DECAY
    deltas = np.abs(np.linspace(min_decay, max_decay, D_HY, dtype=np.float32))
    dec = np.exp(-t[:, None] * deltas[None, :]).astype(np.float32)
    spec = jax.ShapeDtypeStruct((HY_ORDER, l, D_HY), F32)
    names = ("hy_w1", "hy_b1", "hy_w2", "hy_b2", "hy_w3", "hy_freq")
    return pl.pallas_call(
        functools.partial(_hy_filter_kernel, l),
        grid=(1,),
        in_specs=[_full(feats.shape), _full(dec.shape)] + [_layer_const(wts[n], li) for n in names]
                 + [_full(fwd_hi.shape), _full(fwd_lo.shape)],
        out_specs=[_full(spec.shape)] * 3,
        out_shape=[spec] * 3,
        compiler_params=_cparams("arbitrary"),
        name="hyena_filter",
    )(jnp.asarray(feats), jnp.asarray(dec), *[wts[n] for n in names], fwd_hi, fwd_lo)


def _hy_kernel(nc, ng, u_ref, cw_ref, cb_ref, f_ref, g_ref, a_ref, b_ref, d_ref, bias_ref, y_ref,
               uc_scr, vh_scr, vl_scr, sh_scr, sl_scr, z_scr):
    l = nc * CHUNK
    r = DFT_BLOCK
    nblk = l // r
    cw = cw_ref[...]
    cb = cb_ref[...]
    dot = lambda p, q: jnp.dot(p, q, preferred_element_type=F32)
    wide = lambda x: jnp.concatenate([x] * ng, axis=1)
    seq_cols = lambda g: slice(g * D_HY, (g + 1) * D_HY)

    def for_blocks(body):
        if nblk == 1:
            body(0)
        else:
            lax.fori_loop(0, nblk, lambda i, carry: (body(i), carry)[1], 0)

    def conv(c, carry):
        rows = _rows(c)
        for g in range(ng):
            uc = _conv3_chunk(u_ref.at[g], c, nc, cw, cb)
            uc_scr[g, rows, :] = uc
            vh_scr[rows, seq_cols(g)], vl_scr[rows, seq_cols(g)] = _split(uc[:, 0:D_HY])
        return carry

    lax.fori_loop(0, nc, conv, 0)

    def long_conv(o):
        def spectrum(i):
            rows = pl.ds(pl.multiple_of(i * r, r), r)
            rows_im = pl.ds(pl.multiple_of(l + i * r, r), r)
            p = dot(f_ref[i], vh_scr[...])
            pl_ = dot(f_ref[i, 0:2 * r, :], vl_scr[...])
            zr = p[0:r] + (p[2 * r:3 * r] + pl_[0:r])
            zi = p[r:2 * r] + (p[3 * r:4 * r] + pl_[r:2 * r])
            fa, fb, fd = wide(a_ref[o, rows, :]), wide(b_ref[o, rows, :]), wide(d_ref[o, rows, :])
            sh_scr[rows, :], sl_scr[rows, :] = _split(zr * fa - zi * fb)
            sh_scr[rows_im, :], sl_scr[rows_im, :] = _split(zr * fb + zi * fd)

        for_blocks(spectrum)

        def inverse(i):
            rows = pl.ds(pl.multiple_of(i * r, r), r)
            p = dot(g_ref[i], sh_scr[...])
            y = p[0:r] + (p[r:2 * r] + dot(g_ref[i, 0:r, :], sl_scr[...]))
            gate = jnp.concatenate([uc_scr[g, rows, (o + 1) * D_HY:(o + 2) * D_HY] for g in range(ng)], axis=1)
            if o == 0:
                v = jnp.concatenate([uc_scr[g, rows, 0:D_HY] for g in range(ng)], axis=1)
            else:
                v = z_scr[rows, :]
            out = gate * (y + v * wide(bias_ref[o:o + 1, :]))
            if o + 1 < HY_ORDER:
                z_scr[rows, :] = out
                vh_scr[rows, :], vl_scr[rows, :] = _split(out)
            else:
                for g in range(ng):
                    y_ref[g, rows, :] = out[:, seq_cols(g)]

        for_blocks(inverse)

    for o in range(HY_ORDER):
        long_conv(o)


def _hyena(u, wts, li, tables, spectra, nb, l):
    nc = l // CHUNK
    ng = HY_GROUP
    a, b, d = spectra
    names = ("hy_conv_w", "hy_conv_b")
    y = pl.pallas_call(
        functools.partial(_hy_kernel, nc, ng),
        grid=(nb // ng,),
        in_specs=[pl.BlockSpec((ng, l, HY_IN), lambda i: (i, 0, 0))] + [_layer_const(wts[n], li) for n in names]
                 + [_const(tables[0].shape), _const(tables[1].shape),
                    _const((HY_ORDER, l, D_HY)), _const((HY_ORDER, l, D_HY)), _const((HY_ORDER, l, D_HY)),
                    _layer_const(wts["hy_bias"], li)],
        out_specs=pl.BlockSpec((ng, l, D_HY), lambda i: (i, 0, 0)),
        out_shape=jax.ShapeDtypeStruct((nb, l, D_HY), F32),
        scratch_shapes=[pltpu.VMEM((ng, l, HY_IN), F32),
                        pltpu.VMEM((l, ng * D_HY), BF16), pltpu.VMEM((l, ng * D_HY), BF16),
                        pltpu.VMEM((2 * l, ng * D_HY), BF16), pltpu.VMEM((2 * l, ng * D_HY), BF16),
                        pltpu.VMEM((l, ng * D_HY), F32)],
        compiler_params=_cparams("parallel"),
        name="hyena",
    )(u.reshape(nb, l, HY_IN), *[wts[n] for n in names], *tables, a, b, d, wts["hy_bias"])
    return y.reshape(nb * l, D_HY)


def _seg_rms(x, w):
    return x * lax.rsqrt(_seg_mean(x * x, HEAD_DIM) + NORM_EPS) * w


def _stack_q(q):
    lo = lax.broadcasted_iota(jnp.int32, (1, D_KV), 1) < HEAD_DIM
    qa = q[:, 0:D_KV]
    qb = q[:, D_KV:2 * D_KV]
    return jnp.concatenate([jnp.where(lo, qa, 0.0), jnp.where(lo, pltpu.roll(qa, HEAD_DIM, 1), 0.0),
                            jnp.where(lo, 0.0, pltpu.roll(qb, HEAD_DIM, 1)), jnp.where(lo, 0.0, qb)], axis=0)


def _unstack_o(o):
    r = o.shape[0] // ATT_HEADS
    lo = lax.broadcasted_iota(jnp.int32, (1, D_KV), 1) < HEAD_DIM
    ya = jnp.where(lo, o[0:r], pltpu.roll(o[r:2 * r], HEAD_DIM, 1))
    yb = jnp.where(lo, pltpu.roll(o[2 * r:3 * r], HEAD_DIM, 1), o[3 * r:4 * r])
    return jnp.concatenate([ya, yb], axis=1)


def _sink_col(sink_ref, li, r):
    rb = lax.broadcasted_iota(jnp.int32, (ATT_HEADS * r, 1), 0) // r
    col = jnp.full((ATT_HEADS * r, 1), sink_ref[li * ATT_HEADS + ATT_HEADS - 1], F32)
    for h in range(ATT_HEADS - 2, -1, -1):
        col = jnp.where(rb == h, sink_ref[li * ATT_HEADS + h], col)
    return col


def _ctx_attn_kernel(li, ng, u_ref, qn_ref, kn_ref, sink_ref, *rest):
    y_ref, k_ref, v_ref = rest[-3:]
    l = u_ref.shape[1]
    sink = _sink_col(sink_ref, li, l)
    for g in range(ng):
        u = u_ref[g]
        q = _seg_rms(u[:, 0:D_ATT], qn_ref[...])
        k = _seg_rms(u[:, D_ATT:D_ATT + D_KV], kn_ref[:, 0:D_KV])
        v = u[:, D_ATT + D_KV:D_ATT + 2 * D_KV]
        k_ref[g] = k
        v_ref[g] = v
        s = _bdot_nt(_stack_q(q), k) * (HEAD_DIM ** -0.5)
        m = jnp.maximum(jnp.max(s, axis=-1, keepdims=True), sink)
        p = jnp.exp(s - m)
        den = jnp.sum(p, axis=-1, keepdims=True) + jnp.exp(sink - m)
        y_ref[g] = _unstack_o(_bdot(p * (1.0 / den), v))


def _ctx_attention(u, wts, li, nb, l, prev=None):
    ng = ATT_GROUP
    kv_spec, kv_shape = _slot_out((l, D_KV), nb, li, True, ng)
    args = [u.reshape(nb, l, ATT_IN), wts["attn_q_norm"], wts["attn_k_norm"], wts["attn_sink"]]
    in_specs = [_seq_spec(l, ATT_IN, ng), _layer_const(wts["attn_q_norm"], li),
                _layer_const(wts["attn_k_norm"], li), pl.BlockSpec(memory_space=pltpu.SMEM)]
    aliases = _carry_prev(args, in_specs, [] if prev is None else list(prev), 1)
    y, k, v = pl.pallas_call(
        functools.partial(_ctx_attn_kernel, li, ng),
        grid=(nb // ng,),
        in_specs=in_specs,
        out_specs=[_seq_spec(l, D_ATT, ng), kv_spec, kv_spec],
        out_shape=[jax.ShapeDtypeStruct((nb, l, D_ATT), F32), kv_shape, kv_shape],
        input_output_aliases=aliases,
        compiler_params=_cparams("parallel"),
        name="ctx_attention",
    )(*args)
    return y.reshape(nb * l, D_ATT), k, v


@functools.lru_cache(maxsize=None)
def _rope_tables_host(l):
    n_rows = l // GRID_W
    rows = np.repeat(np.arange(n_rows, dtype=np.float32), GRID_W)
    cols = np.tile(np.arange(GRID_W, dtype=np.float32), n_rows)
    nf = HEAD_DIM // 4
    inv = (np.float32(ROPE_BASE) ** (-np.arange(nf, dtype=np.float32) / np.float32(nf))).astype(np.float32)
    ar = rows[:, None] * inv[None, :]
    ac = cols[:, None] * inv[None, :]
    cos = np.concatenate([np.cos(ar), np.cos(ar), np.cos(ac), np.cos(ac)], axis=-1)
    sin = np.concatenate([-np.sin(ar), np.sin(ar), -np.sin(ac), np.sin(ac)], axis=-1)
    return (np.tile(cos, (1, ATT_HEADS)).astype(np.float32), np.tile(sin, (1, ATT_HEADS)).astype(np.float32))


def _rope(x, cos, sin):
    w = x.shape[-1]
    nf = HEAD_DIM // 4
    lane = lax.broadcasted_iota(jnp.int32, x.shape, 1)
    first = (lane % (2 * nf)) < nf
    partner = jnp.where(first, pltpu.roll(x, w - nf, 1), pltpu.roll(x, nf, 1))
    return x * cos + partner * sin


def _lat_attn_kernel(nblk, li, u_ref, qn_ref, kn_ref, cos_ref, sin_ref, ck_ref, cv_ref, sink_ref,
                     y_ref, q_scr, k_scr, v_scr):
    blk = ATT_BLOCK
    l = nblk * blk
    cos_q = cos_ref[...]
    sin_q = sin_ref[...]
    u = u_ref[...]
    q = _seg_rms(u[:, 0:D_ATT], qn_ref[...])
    k = _seg_rms(u[:, D_ATT:D_ATT + D_KV], kn_ref[:, 0:D_KV])
    q_scr[...] = _rope(q, cos_q, sin_q)
    zeros = jnp.zeros((blk, D_KV), F32)
    k_scr[0:blk, :] = zeros
    k_scr[blk + l:2 * blk + l, :] = zeros
    v_scr[0:blk, :] = zeros
    v_scr[blk + l:2 * blk + l, :] = zeros
    k_scr[blk:blk + l, :] = _rope(k, cos_q[:, 0:D_KV], sin_q[:, 0:D_KV])
    v_scr[blk:blk + l, :] = u[:, D_ATT + D_KV:D_ATT + 2 * D_KV]

    scale = HEAD_DIM ** -0.5
    r = lax.broadcasted_iota(jnp.int32, (blk, 3 * blk), 0)
    cidx = lax.broadcasted_iota(jnp.int32, (blk, 3 * blk), 1)
    band = (cidx - r >= blk - WINDOW) & (cidx - r <= blk + WINDOW)
    sink = _sink_col(sink_ref, li, blk)

    def block(i, carry):
        rows = _rows(i)
        win = pl.ds(pl.multiple_of(i * blk, blk), 3 * blk)
        kpos = cidx + (i - 1) * blk
        valid = band & (kpos >= 0) & (kpos < l)
        valid = jnp.concatenate([valid] * ATT_HEADS, axis=0)
        qs = _stack_q(q_scr[rows, :])
        s_loc = jnp.where(valid, _bdot_nt(qs, k_scr[win, :]) * scale, -jnp.inf)
        s_ctx = _bdot_nt(qs, ck_ref[...]) * scale
        m = jnp.maximum(jnp.maximum(jnp.max(s_loc, axis=-1, keepdims=True),
                                    jnp.max(s_ctx, axis=-1, keepdims=True)), sink)
        p_loc = jnp.exp(s_loc - m)
        p_ctx = jnp.exp(s_ctx - m)
        den = (jnp.sum(p_loc, axis=-1, keepdims=True) + jnp.sum(p_ctx, axis=-1, keepdims=True)
               + jnp.exp(sink - m))
        inv = 1.0 / den
        y_ref[rows, :] = _unstack_o(_bdot(p_ctx * inv, cv_ref[...]) + _bdot(p_loc * inv, v_scr[win, :]))
        return carry

    lax.fori_loop(0, nblk, block, 0)


def _lat_attention(u, wts, li, ck, cv, nb, l):
    lc = ck.shape[2]
    nblk = l // ATT_BLOCK
    cos, sin = _rope_tables_host(l)
    cache_spec = pl.BlockSpec((None, None, lc, D_KV), lambda b: (b, li, 0, 0))
    y = pl.pallas_call(
        functools.partial(_lat_attn_kernel, nblk, li),
        grid=(nb,),
        in_specs=[_seq_spec(l, ATT_IN), _layer_const(wts["attn_q_norm"], li), _layer_const(wts["attn_k_norm"], li),
                  _const((l, D_ATT)), _const((l, D_ATT)), cache_spec, cache_spec,
                  pl.BlockSpec(memory_space=pltpu.SMEM)],
        out_specs=_seq_spec(l, D_ATT),
        out_shape=jax.ShapeDtypeStruct((nb, l, D_ATT), F32),
        scratch_shapes=[pltpu.VMEM((l, D_ATT), F32), pltpu.VMEM((l + 2 * ATT_BLOCK, D_KV), F32),
                        pltpu.VMEM((l + 2 * ATT_BLOCK, D_KV), F32)],
        compiler_params=_cparams("parallel"),
        name="lat_attention",
    )(u.reshape(nb, l, ATT_IN), wts["attn_q_norm"], wts["attn_k_norm"], jnp.asarray(cos), jnp.asarray(sin),
      ck, cv, wts["attn_sink"])
    return y.reshape(nb * l, D_ATT)


def _pack_mix_w_in(w):
    c0 = D_SSM + SSM_CONV_CH
    n_dt = 2 * SSM_HEADS
    zeros = jnp.zeros(w.shape[:-1] + (DT_PAD - DT_REP * n_dt,), w.dtype)
    dt_cols = [w[..., c0:c0 + n_dt]] * DT_REP
    return jnp.concatenate([w[..., :c0], w[..., c0 + n_dt:]] + dt_cols + [zeros], axis=-1).astype(BF16)


def _prep_weights(p):
    row = lambda a: a.reshape(DEPTH, 1, -1)
    pad_lanes = lambda a: jnp.pad(row(a), ((0, 0), (0, 0), (0, LANES - a[0].size)))
    return dict(
        norm_w=p["norm_w"].reshape(DEPTH * 3, 1, D_MODEL),
        ffn_w_in=p["ffn_w_in"].astype(BF16).reshape(DEPTH * 2, D_MODEL, 2 * D_FF),
        ffn_w_out=p["ffn_w_out"].astype(BF16).reshape(DEPTH * 2, D_FF, D_MODEL),
        mix_w_in=_pack_mix_w_in(p["mix_w_in"]), mix_w_out=p["mix_w_out"].astype(BF16),
        ssd_conv_w=p["ssd_conv_w"], ssd_conv_b=row(p["ssd_conv_b"]),
        ssd_dt_bias=pad_lanes(jnp.tile(row(p["ssd_dt_bias"]), (1, 1, DT_REP))),
        ssd_a_log=pad_lanes(jnp.tile(row(p["ssd_a_log"]), (1, 1, DT_REP))),
        ssd_d=row(jnp.repeat(p["ssd_d"], SSM_HEAD_DIM, axis=-1)), ssd_norm_w=row(p["ssd_norm_w"]),
        hy_conv_w=p["hy_conv_w"], hy_conv_b=row(p["hy_conv_b"]), hy_bias=p["hy_bias"],
        hy_w1=jnp.pad(p["hy_w1"], ((0, 0), (0, LANES - HY_EMB), (0, 0))), hy_b1=row(p["hy_b1"]),
        hy_w2=p["hy_w2"], hy_b2=row(p["hy_b2"]), hy_w3=p["hy_w3"], hy_freq=row(p["hy_freq"]),
        ret_decay_logit=pad_lanes(p["ret_decay_logit"]), ret_gn_w=row(p["ret_gn_w"]),
        attn_q_norm=row(jnp.tile(p["attn_q_norm"], (1, ATT_HEADS))),
        attn_k_norm=row(jnp.tile(p["attn_k_norm"], (1, ATT_HEADS))),
        attn_sink=p["attn_sink"].reshape(DEPTH * ATT_HEADS),
    )


def _layer(x, mod, row0, wts, li, nb, l, ssd_s0, ret_s0, ctx_kv, hy_tables, hy_spectra, carried):
    ctx = ctx_kv is None
    rpm = nb * l if ctx else l
    x = _ffn(x, mod, row0, rpm, wts, li, 0)
    z, xbc, hy, ret, att, dt = _inproj(x, mod, row0, rpm, wts, li)
    y_ssd, s_ssd = _ssd(z, xbc, dt, wts, li, ssd_s0, nb, l, stacked=ctx, prev=carried.get("ssd"))
    y_hy = _hyena(hy, wts, li, hy_tables, hy_spectra, nb, l)
    y_ret, s_ret = _retention(ret, wts, li, ret_s0, nb, l, stacked=ctx, prev=carried.get("ret"))
    if ctx:
        y_att, k, v = _ctx_attention(att, wts, li, nb, l, prev=carried.get("kv"))
        carried = dict(ssd=s_ssd, ret=s_ret, kv=(k, v))
    else:
        y_att = _lat_attention(att, wts, li, ctx_kv[0], ctx_kv[1], nb, l)
    x = _ffn(x, mod, row0, rpm, wts, li, 1, mix=(y_ssd, y_hy, y_ret, y_att))
    return x, carried


def kernel(x_prompt, x_sample, cache_k, cache_v, state_ssd, state_ret, c, c_ctx, w_mod, b_mod, norm_w, ffn_w_in, ffn_w_out, mix_w_in, mix_w_out, ssd_conv_w, ssd_conv_b, ssd_dt_bias, ssd_a_log, ssd_d, ssd_norm_w, hy_conv_w, hy_conv_b, hy_w1, hy_b1, hy_w2, hy_b2, hy_w3, hy_freq, hy_bias, ret_decay_logit, ret_gn_w, attn_q_norm, attn_k_norm, attn_sink):
    bp, lp_len, _ = x_prompt.shape
    bs, ls_len, _ = x_sample.shape
    lc = cache_k.shape[2]

    cond = jnp.concatenate([c_ctx[None, :], c, jnp.zeros((MOD_ROWS - 1 - bs, D_MODEL), F32)], axis=0)
    mod = _modulation(cond, w_mod, b_mod)
    wts = _prep_weights(dict(
        norm_w=norm_w, ffn_w_in=ffn_w_in, ffn_w_out=ffn_w_out, mix_w_in=mix_w_in, mix_w_out=mix_w_out,
        ssd_conv_w=ssd_conv_w, ssd_conv_b=ssd_conv_b, ssd_dt_bias=ssd_dt_bias, ssd_a_log=ssd_a_log, ssd_d=ssd_d,
        ssd_norm_w=ssd_norm_w, hy_conv_w=hy_conv_w, hy_conv_b=hy_conv_b, hy_w1=hy_w1, hy_b1=hy_b1, hy_w2=hy_w2,
        hy_b2=hy_b2, hy_w3=hy_w3, hy_freq=hy_freq, hy_bias=hy_bias, ret_decay_logit=ret_decay_logit,
        ret_gn_w=ret_gn_w, attn_q_norm=attn_q_norm, attn_k_norm=attn_k_norm, attn_sink=attn_sink))
    ck = cache_k.reshape(bs, DEPTH, lc, D_KV)
    cv = cache_v.reshape(bs, DEPTH, lc, D_KV)

    tab_p = _dft_tables(lp_len)
    tab_s = _dft_tables(ls_len)
    blk_p = _dft_block_tables(tab_p, lp_len)
    blk_s = _dft_block_tables(tab_s, ls_len)

    yp = x_prompt.reshape(bp * lp_len, D_MODEL)
    ys = x_sample.reshape(bs * ls_len, D_MODEL)
    carried = dict(
        ssd=jnp.zeros((bp, DEPTH, 2, SSM_HEADS, SSM_STATE, SSM_HEAD_DIM), F32),
        ret=jnp.zeros((bp, DEPTH, 2, RET_HEADS, RET_HEAD_DIM, RET_HEAD_DIM), F32),
        kv=(jnp.zeros((bp, DEPTH, lp_len, D_KV), F32), jnp.zeros((bp, DEPTH, lp_len, D_KV), F32)))
    for li in range(DEPTH):
        spec_p = _hy_filter(lp_len, tab_p[0], tab_p[1], wts, li)
        spec_s = _hy_filter(ls_len, tab_s[0], tab_s[1], wts, li)
        yp, carried = _layer(yp, mod, li * MOD_ROWS, wts, li, bp, lp_len, None, None, None, blk_p, spec_p, carried)
        ys, _ = _layer(ys, mod, li * MOD_ROWS + 1, wts, li, bs, ls_len, state_ssd, state_ret, (ck, cv),
                       blk_s, spec_s, {})

    kv_shape = (bp, DEPTH, lp_len, ATT_KV_HEADS, HEAD_DIM)
    new_k, new_v = carried["kv"]
    return (yp.reshape(bp, lp_len, D_MODEL), ys.reshape(bs, ls_len, D_MODEL),
            new_k.reshape(kv_shape), new_v.reshape(kv_shape), carried["ssd"], carried["ret"])
```

```python
import functools
import math

import numpy as np
import jax
import jax.numpy as jnp
from jax import lax
from jax.experimental import pallas as pl
from jax.experimental.pallas import tpu as pltpu

F32 = jnp.float32
BF16 = jnp.bfloat16
HI = lax.Precision.HIGHEST

D_MODEL = 1024
DEPTH = 2
GRID_W = 64
D_FF = 2816
N_MOD = 9
NORM_EPS = 1e-6
CHUNK = 128
D_SSM = 256
SSM_HEADS = 4
SSM_HEAD_DIM = 64
SSM_STATE = 128
SSM_GROUPS = 2
SSM_CONV_CH = D_SSM + 2 * SSM_GROUPS * SSM_STATE
D_HY = 256
HY_ORDER = 2
HY_BANDS = 16
HY_EMB = 1 + 2 * HY_BANDS
HY_HIDDEN = 64
HY_FAST_DECAY = 0.3
HY_SLOW_DECAY = 1.5
HY_TARGET = 1e-2
HY_IN = (HY_ORDER + 1) * D_HY
D_RET = 256
RET_HEADS = 4
RET_HEAD_DIM = 64
RET_IN = 4 * D_RET
ATT_HEADS = 4
ATT_KV_HEADS = 2
HEAD_DIM = 64
D_ATT = ATT_HEADS * HEAD_DIM
D_KV = ATT_KV_HEADS * HEAD_DIM
ATT_IN = D_ATT + 2 * D_KV
WINDOW = 128
ATT_BLOCK = 128
ROPE_BASE = 10000.0
D_MIX = D_SSM + D_HY + D_RET + D_ATT

LANES = 128
DT_PAD = LANES
VMEM_LIMIT = 56 * 1024 * 1024
MOD_ROWS = 8

TOKEN_TILE = 512
DFT_BLOCK = 256
HY_GROUP = 2
SEQ_GROUP = 2
ATT_GROUP = 1


def _cparams(*sem):
    return pltpu.CompilerParams(dimension_semantics=sem, vmem_limit_bytes=VMEM_LIMIT)


def _rms(x, w):
    return x * lax.rsqrt(jnp.mean(x * x, axis=-1, keepdims=True) + NORM_EPS) * w


def _silu(x):
    return x * (1.0 / (1.0 + jnp.exp(-x)))


def _softplus(x):
    return jnp.maximum(x, 0.0) + jnp.log1p(jnp.exp(-jnp.abs(x)))


def _bdot(a, b):
    return jnp.dot(a.astype(BF16), b.astype(BF16), preferred_element_type=F32)


def _bdot_nt(a, b):
    return lax.dot_general(a.astype(BF16), b.astype(BF16), (((1,), (1,)), ((), ())),
                           preferred_element_type=F32)


def _bdot_tn(a, b):
    return lax.dot_general(a.astype(BF16), b.astype(BF16), (((0,), (0,)), ((), ())),
                           preferred_element_type=F32)


def _hdot(a, b):
    return jnp.dot(a, b, preferred_element_type=F32, precision=HI)


def _full(shape):
    n = len(shape)
    return pl.BlockSpec(shape, lambda *_: (0,) * n)


def _const(shape):
    n = len(shape)
    return pl.BlockSpec(shape, lambda *_: (0,) * n, pipeline_mode=pl.Buffered(1))


def _layer_const(arr, li):
    tail = arr.shape[1:]
    zeros = (0,) * len(tail)
    return pl.BlockSpec((None,) + tail, lambda *_: (li,) + zeros, pipeline_mode=pl.Buffered(1))


def _mod_spec(k, tm, rows_per_mod, row0):
    return pl.BlockSpec((None, 1, D_MODEL), lambda i: (row0 + (i * tm) // rows_per_mod, 0, k))


def _seq_spec(l, w, ng=None):
    return pl.BlockSpec((ng, l, w), lambda b: (b, 0, 0))


def _head_masks(width, heads):
    lane = lax.broadcasted_iota(jnp.int32, (1, width), 1)
    hd = width // heads
    return [(lane >= h * hd) & (lane < (h + 1) * hd) for h in range(heads)]


def _by_head(masks, vals):
    out = vals[-1]
    for m, v in zip(masks[-2::-1], vals[-2::-1]):
        out = jnp.where(m, v, out)
    return out


def _block_diag(n, blk, value):
    i = lax.broadcasted_iota(jnp.int32, (n, n), 0) // blk
    j = lax.broadcasted_iota(jnp.int32, (n, n), 1) // blk
    return jnp.where(i == j, value, 0.0).astype(F32)


def _mod_kernel(c_ref, w_ref, b_ref, o_ref):
    c = c_ref[...]
    o_ref[...] = _bdot(_silu(c), w_ref[...]) + b_ref[...]


def _modulation(cond, w_mod, b_mod):
    out = pl.pallas_call(
        _mod_kernel,
        grid=(DEPTH, N_MOD),
        in_specs=[pl.BlockSpec((MOD_ROWS, D_MODEL), lambda l, j: (0, 0)),
                  pl.BlockSpec((None, D_MODEL, D_MODEL), lambda l, j: (l, 0, j)),
                  pl.BlockSpec((None, 1, D_MODEL), lambda l, j: (l, 0, j))],
        out_specs=pl.BlockSpec((None, MOD_ROWS, D_MODEL), lambda l, j: (l, 0, j)),
        out_shape=jax.ShapeDtypeStruct((DEPTH, MOD_ROWS, N_MOD * D_MODEL), F32),
        compiler_params=_cparams("arbitrary", "arbitrary"),
        name="modulation",
    )(cond, w_mod, b_mod.reshape(DEPTH, 1, N_MOD * D_MODEL))
    return out.reshape(DEPTH * MOD_ROWS, 1, N_MOD * D_MODEL)


def _ffn_kernel(n_mix, x_ref, sh_ref, sc_ref, g_ref, nw_ref, wi_ref, wo_ref, *rest):
    o_ref = rest[-1]
    x = x_ref[...]
    if n_mix:
        gm_ref, wm_ref = rest[0], rest[1 + n_mix]
        w = D_MIX // n_mix
        acc = _bdot(rest[1][...], wm_ref[0:w, :])
        for j in range(1, n_mix):
            acc += _bdot(rest[1 + j][...], wm_ref[j * w:(j + 1) * w, :])
        x = x + gm_ref[...] * acc
    h = (_rms(x, nw_ref[...]) * (1.0 + sc_ref[...]) + sh_ref[...]).astype(BF16)
    gate = jnp.dot(h, wi_ref[:, 0:D_FF], preferred_element_type=F32)
    up = jnp.dot(h, wi_ref[:, D_FF:2 * D_FF], preferred_element_type=F32)
    o_ref[...] = x + 0.5 * g_ref[...] * _bdot(_silu(gate) * up, wo_ref[...])


def _ffn(x, mod, row0, rows_per_mod, wts, li, k, mix=()):
    t = x.shape[0]
    tm = min(TOKEN_TILE, rows_per_mod)
    w_in, w_out = wts["ffn_w_in"], wts["ffn_w_out"]
    args = [x, mod, mod, mod, wts["norm_w"], w_in, w_out]
    in_specs = [pl.BlockSpec((tm, D_MODEL), lambda i: (i, 0)),
                _mod_spec(6 * k, tm, rows_per_mod, row0),
                _mod_spec(6 * k + 1, tm, rows_per_mod, row0),
                _mod_spec(6 * k + 2, tm, rows_per_mod, row0),
                _layer_const(wts["norm_w"], 3 * li + 2 * k),
                _layer_const(w_in, 2 * li + k), _layer_const(w_out, 2 * li + k)]
    if mix:
        args += [mod, *mix, wts["mix_w_out"]]
        in_specs += [_mod_spec(5, tm, rows_per_mod, row0)]
        in_specs += [pl.BlockSpec((tm, y.shape[1]), lambda i: (i, 0)) for y in mix]
        in_specs += [_layer_const(wts["mix_w_out"], li)]
    return pl.pallas_call(
        functools.partial(_ffn_kernel, len(mix)),
        grid=(t // tm,),
        in_specs=in_specs,
        out_specs=pl.BlockSpec((tm, D_MODEL), lambda i: (i, 0)),
        out_shape=jax.ShapeDtypeStruct((t, D_MODEL), F32),
        compiler_params=_cparams("parallel"),
        name="ffn",
    )(*args)


_IN_SPLITS = (("z", D_SSM, BF16), ("xbc", SSM_CONV_CH, F32), ("hy", HY_IN, F32), ("ret", RET_IN, BF16),
              ("att", ATT_IN, BF16), ("dt", DT_PAD, F32))


C_DT = D_SSM + SSM_CONV_CH
N_DT = 2 * SSM_HEADS
W_TAIL = HY_IN + RET_IN + ATT_IN


def _inproj_kernel(x_ref, sh_ref, sc_ref, nw_ref, w_ref, wdt_ref, *rest):
    o_refs, wt_scr = rest[:-1], rest[-1]

    @pl.when(pl.program_id(0) == 0)
    def _():
        wt_scr[...] = w_ref[:, C_DT + N_DT:C_DT + N_DT + W_TAIL]

    h = (_rms(x_ref[...], nw_ref[...]) * (1.0 + sc_ref[...]) + sh_ref[...]).astype(BF16)
    off = 0
    for (name, width, dtype), o_ref in zip(_IN_SPLITS, o_refs):
        if name == "dt":
            w = wdt_ref[...]
        elif off < C_DT:
            w = w_ref[:, off:off + width]
        else:
            w = wt_scr[:, off - C_DT:off - C_DT + width]
        o_ref[...] = jnp.dot(h, w, preferred_element_type=F32).astype(dtype)
        off += width


def _inproj(x, mod, row0, rows_per_mod, wts, li):
    t = x.shape[0]
    tm = min(TOKEN_TILE, rows_per_mod)
    return pl.pallas_call(
        _inproj_kernel,
        grid=(t // tm,),
        in_specs=[pl.BlockSpec((tm, D_MODEL), lambda i: (i, 0)),
                  _mod_spec(3, tm, rows_per_mod, row0),
                  _mod_spec(4, tm, rows_per_mod, row0),
                  _layer_const(wts["norm_w"], 3 * li + 1),
                  _layer_const(wts["mix_w_in"], li), _layer_const(wts["mix_w_dt"], li)],
        out_specs=[pl.BlockSpec((tm, width), lambda i: (i, 0)) for _, width, _ in _IN_SPLITS],
        out_shape=[jax.ShapeDtypeStruct((t, width), dtype) for _, width, dtype in _IN_SPLITS],
        scratch_shapes=[pltpu.VMEM((D_MODEL, W_TAIL), BF16)],
        compiler_params=_cparams("arbitrary"),
        name="mix_in",
    )(x, mod, mod, wts["norm_w"], wts["mix_w_in"], wts["mix_w_dt"])


def _conv3_chunk(x_ref, c, nc, w, b):
    q = CHUNK
    l = nc * q
    r0 = pl.multiple_of(c * q, q)
    x = x_ref[pl.ds(r0, q), :]
    prev = x_ref[pl.ds(jnp.maximum(r0 - 1, 0), 1), :]
    nxt = x_ref[pl.ds(jnp.minimum(r0 + q, l - 1), 1), :]
    prev = jnp.where(c > 0, prev, 0.0)
    nxt = jnp.where(c < nc - 1, nxt, 0.0)
    rid = lax.broadcasted_iota(jnp.int32, (q, 1), 0)
    xm1 = jnp.where(rid == 0, prev, pltpu.roll(x, 1, 0))
    xp1 = jnp.where(rid == q - 1, nxt, pltpu.roll(x, q - 1, 0))
    return xm1 * w[0:1, :] + x * w[1:2, :] + xp1 * w[2:3, :] + b


def _tri(lower):
    i = lax.broadcasted_iota(jnp.int32, (CHUNK, CHUNK), 0)
    j = lax.broadcasted_iota(jnp.int32, (CHUNK, CHUNK), 1)
    return (j <= i) if lower else (j >= i)


def _rows(c):
    return pl.ds(pl.multiple_of(c * CHUNK, CHUNK), CHUNK)


def _split_cat(v, parts, axis):
    out, r = [], v
    for i in range(parts):
        piece = r.astype(BF16)
        out.append(piece)
        if i + 1 < parts:
            r = r - piece.astype(F32)
    return jnp.concatenate(out, axis=axis)


def _seg_mean(x, seg):
    w = x.shape[-1]
    ones = _block_diag(w, seg, 1.0).astype(BF16)
    return jnp.dot(_split_cat(x, 2, axis=1), jnp.concatenate([ones, ones], axis=0),
                   preferred_element_type=F32) * (1.0 / seg)


def _stack_heads(x, masks):
    return jnp.concatenate([jnp.where(m, x, 0.0) for m in masks], axis=0)


def _slot_out(tail, nb, li, stacked, ng=None):
    zeros = (0,) * len(tail)
    slot = li if stacked else 0
    spec = pl.BlockSpec((ng, None) + tail, lambda b: (b, slot) + zeros)
    return spec, jax.ShapeDtypeStruct((nb, DEPTH if stacked else 1) + tail, F32)


def _carry_prev(args, in_specs, prevs, first_out):
    aliases = {}
    for j, prev in enumerate(prevs):
        aliases[len(args)] = first_out + j
        args.append(prev)
        in_specs.append(pl.BlockSpec(memory_space=pl.ANY))
    return aliases


SSD_SEL_W = 2 * SSM_HEADS * SSM_STATE
CUM_PIECES = 3
WGT_PIECES = 2
DT_REP = CUM_PIECES + WGT_PIECES


def _ssd_kernel(nc, ng, has_s0, *refs):
    refs = list(refs)
    (z_ref, xbc_ref, dt_ref, cw_ref, cb_ref, dtb_ref, alog_ref, dsk_ref, nw_ref) = refs[:9]
    s0_ref = refs[9] if has_s0 else None
    (y_ref, s_ref, xs_scr, xk_scr, xt_scr, dt_scr, cum_scr, dtt_scr, cumt_scr, yf_scr, yb_scr, st_scr,
     msk_scr, sel_scr) = refs[-14:]
    q = CHUNK
    nh, n, p = SSM_HEADS, SSM_STATE, SSM_HEAD_DIM
    rep = nh // SSM_GROUPS
    hm = _head_masks(D_SSM, nh)

    @pl.when(pl.program_id(0) == 0)
    def _():
        i = lax.broadcasted_iota(jnp.int32, (D_SSM, nh * n), 0) // p
        j = lax.broadcasted_iota(jnp.int32, (D_SSM, nh * n), 1) // n
        msk_scr[...] = jnp.where(i == j, 1.0, 0.0).astype(F32)
        k = lax.broadcasted_iota(jnp.int32, (LANES, SSD_SEL_W), 0)
        col = lax.broadcasted_iota(jnp.int32, (LANES, SSD_SEL_W), 1)
        is_cum = col < nh * n
        grp = k // (2 * nh)
        grp_ok = (is_cum & (grp < CUM_PIECES)) | ((~is_cum) & (grp >= CUM_PIECES) & (grp < DT_REP))
        hit = grp_ok & (k % nh == (col % (nh * n)) // n)
        sel_scr[...] = jnp.where(hit, 1.0, 0.0).astype(BF16)

    cw = cw_ref[...]
    cb = cb_ref[...]
    neg_a = -jnp.exp(alog_ref[...])
    dtb = dtb_ref[...]

    class Seq:
        def __init__(self, g):
            self.z, self.xbc, self.dt_in, self.y, self.s = (z_ref.at[g], xbc_ref.at[g], dt_ref.at[g], y_ref.at[g],
                                                            s_ref.at[g])
            self.s0 = s0_ref.at[g] if has_s0 else None
            self.xs, self.xk, self.xt, self.dt, self.cum = (xs_scr.at[g], xk_scr.at[g], xt_scr.at[g], dt_scr.at[g],
                                                            cum_scr.at[g])
            self.dtt, self.cumt, self.yf, self.yb, self.st = (dtt_scr.at[g], cumt_scr.at[g], yf_scr.at[g],
                                                              yb_scr.at[g], st_scr.at[g])

    seqs = [Seq(g) for g in range(ng)]

    def prep_seq(sq, c):
        rows = _rows(c)
        xall = _silu(_conv3_chunk(sq.xbc, c, nc, cw, cb))
        sq.xs[rows, :] = xall
        xs = xall[:, 0:D_SSM]
        sq.xk[c] = _stack_heads(xs, hm).astype(BF16)
        sq.xt[c] = xs.T.astype(BF16)
        dt = _softplus(sq.dt_in[rows, :] + dtb)
        la = _split_cat(dt * neg_a, 3, axis=0)
        tri = jnp.concatenate([_tri(True), _tri(False)], axis=0)
        tri = jnp.where(tri, 1.0, 0.0).astype(BF16)
        cs = jnp.dot(jnp.concatenate([tri] * 3, axis=1), la, preferred_element_type=F32)
        lane = lax.broadcasted_iota(jnp.int32, (1, LANES), 1)
        cum = jnp.where(lane % (2 * nh) < nh, cs[0:q], cs[q:2 * q])
        sq.dt[rows, :] = dt
        sq.cum[rows, :] = cum
        sq.dtt[c] = dt.T
        sq.cumt[c] = cum.T

    def prep(c, carry):
        for sq in seqs:
            prep_seq(sq, c)
        return carry

    lax.fori_loop(0, nc, prep, 0)

    for sq in seqs:
        for d in range(2):
            for h in range(nh):
                blk = slice(h * p, (h + 1) * p)
                if has_s0:
                    parts = [jnp.zeros((p, n), F32)] * nh
                    parts[h] = sq.s0[d, h].T
                    sq.st[d, blk, :] = jnp.concatenate(parts, axis=1)
                else:
                    sq.st[d, blk, :] = jnp.zeros((p, nh * n), F32)

    def pack_scalars(sq, c, d):
        rows = _rows(c)
        dt = sq.dt[rows, :]
        cum = sq.cum[rows, :]
        edge = q - 1 if d == 0 else 0
        lane = lax.broadcasted_iota(jnp.int32, (1, LANES), 1)
        used = (lane < DT_REP * 2 * nh) & ((lane % (2 * nh)) // nh == d)
        cum = jnp.where(used, cum, 0.0)
        wgt = jnp.exp(cum[edge:edge + 1, :] - cum) * dt
        grp = lane // (2 * nh)
        packed = jnp.zeros_like(cum)
        for src, pieces, g0 in ((cum, CUM_PIECES, 0), (wgt, WGT_PIECES, CUM_PIECES)):
            rest = src
            for i in range(pieces):
                piece = rest.astype(BF16).astype(F32)
                packed = jnp.where(grp == g0 + i, piece, packed)
                rest = rest - piece
        return jnp.where(used, packed, 0.0).astype(BF16)

    def scan_chunk(sq, c, d, e):
        rows = _rows(c)
        bm = [sq.xs[rows, D_SSM + g * n:D_SSM + (g + 1) * n] for g in range(SSM_GROUPS)]
        cm = [sq.xs[rows, D_SSM + (SSM_GROUPS + g) * n:D_SSM + (SSM_GROUPS + g + 1) * n]
              for g in range(SSM_GROUPS)]
        mask = _tri(d == 0)
        edge = q - 1 if d == 0 else 0
        cb_t = [_bdot_nt(cm[g], bm[g]) for g in range(SSM_GROUPS)]
        blocks, ecol = [], []
        for h in range(nh):
            r = d * nh + h
            col = e[:, h * n:(h + 1) * n]
            decay = jnp.exp(jnp.where(mask, col - sq.cumt[c, r:r + 1, :], -jnp.inf))
            blocks.append((cb_t[h // rep] * decay * sq.dtt[c, r:r + 1, :]).astype(BF16))
            ecol.append(jnp.exp(col))
        y = jnp.dot(jnp.concatenate(blocks, axis=1), sq.xk[c], preferred_element_type=F32)
        s_old = sq.st[d]
        c4 = jnp.concatenate([cm[h // rep] for h in range(nh)], axis=1)
        lo = lax.broadcasted_iota(jnp.int32, (1, n), 1) < p
        ecol = jnp.concatenate([jnp.where(lo, ecol[2 * j], ecol[2 * j + 1]) for j in range(nh // 2)], axis=1)
        y = y + ecol * _bdot_nt(c4, s_old)
        wk = jnp.concatenate([bm[h // rep] for h in range(nh)], axis=1) * e[:, nh * n:2 * nh * n]
        upd = jnp.dot(sq.xt[c], wk.astype(BF16), preferred_element_type=F32)
        sq.st[d] = jnp.exp(e[edge:edge + 1, 0:nh * n]) * s_old + upd * msk_scr[...]
        return y

    def scan(i, carry):
        j = nc - 1 - i
        packed = jnp.concatenate([pack_scalars(sq, c, d) for sq in seqs for c, d in ((i, 0), (j, 1))], axis=0)
        e = jnp.dot(packed, sel_scr[...], preferred_element_type=F32)
        for g, sq in enumerate(seqs):
            sq.yf[_rows(i), :] = scan_chunk(sq, i, 0, e[2 * g * q:(2 * g + 1) * q])
            sq.yb[_rows(j), :] = scan_chunk(sq, j, 1, e[(2 * g + 1) * q:(2 * g + 2) * q])
        return carry

    lax.fori_loop(0, nc, scan, 0)

    for sq in seqs:
        for d in range(2):
            for h in range(nh):
                sq.s[d, h] = sq.st[d, h * p:(h + 1) * p, h * n:(h + 1) * n].T

    dsk = dsk_ref[...]
    nw = nw_ref[...]

    def fin(c, carry):
        rows = _rows(c)
        for sq in seqs:
            y = sq.yf[rows, :] + sq.yb[rows, :] + dsk * sq.xs[rows, 0:D_SSM]
            sq.y[rows, :] = _rms(y * _silu(sq.z[rows, :].astype(F32)), nw)
        return carry

    lax.fori_loop(0, nc, fin, 0)


def _ssd(z, xbc, dt, wts, li, s0, nb, l, stacked=False, prev=None):
    nc = l // CHUNK
    has_s0 = s0 is not None
    names = ("ssd_conv_w", "ssd_conv_b", "ssd_dt_bias", "ssd_a_log", "ssd_d", "ssd_norm_w")
    ng = SEQ_GROUP
    args = [z.reshape(nb, l, D_SSM), xbc.reshape(nb, l, SSM_CONV_CH), dt.reshape(nb, l, DT_PAD)]
    args += [wts[n] for n in names]
    in_specs = [_seq_spec(l, D_SSM, ng), _seq_spec(l, SSM_CONV_CH, ng), _seq_spec(l, DT_PAD, ng)]
    in_specs += [_layer_const(wts[n], li) for n in names]
    if has_s0:
        args.append(s0)
        in_specs.append(pl.BlockSpec((ng, None, 2, SSM_HEADS, SSM_STATE, SSM_HEAD_DIM),
                                     lambda b: (b, li, 0, 0, 0, 0)))
    st_spec, st_shape = _slot_out((2, SSM_HEADS, SSM_STATE, SSM_HEAD_DIM), nb, li, stacked, ng)
    aliases = _carry_prev(args, in_specs, [] if prev is None else [prev], 1)
    y, s = pl.pallas_call(
        functools.partial(_ssd_kernel, nc, ng, has_s0),
        grid=(nb // ng,),
        in_specs=in_specs,
        out_specs=[_seq_spec(l, D_SSM, ng), st_spec],
        out_shape=[jax.ShapeDtypeStruct((nb, l, D_SSM), F32), st_shape],
        input_output_aliases=aliases,
        scratch_shapes=[pltpu.VMEM((ng, l, SSM_CONV_CH), F32),
                        pltpu.VMEM((ng, nc, SSM_HEADS * CHUNK, D_SSM), BF16),
                        pltpu.VMEM((ng, nc, D_SSM, CHUNK), BF16),
                        pltpu.VMEM((ng, l, DT_PAD), F32), pltpu.VMEM((ng, l, DT_PAD), F32),
                        pltpu.VMEM((ng, nc, DT_PAD, CHUNK), F32), pltpu.VMEM((ng, nc, DT_PAD, CHUNK), F32),
                        pltpu.VMEM((ng, l, D_SSM), F32), pltpu.VMEM((ng, l, D_SSM), F32),
                        pltpu.VMEM((ng, 2, D_SSM, SSM_HEADS * SSM_STATE), F32),
                        pltpu.VMEM((D_SSM, SSM_HEADS * SSM_STATE), F32),
                        pltpu.VMEM((LANES, SSD_SEL_W), BF16)],
        compiler_params=_cparams("arbitrary"),
        name="ssd",
    )(*args)
    return y.reshape(nb * l, D_SSM), s


def _ret_kernel(nc, ng, has_s0, *refs):
    refs = list(refs)
    u_ref, dl_ref, gn_ref = refs[:3]
    s0_ref = refs[3] if has_s0 else None
    y_ref, s_ref, yf_scr, yb_scr, st_scr, dm_scr, e_scr, ea_scr, bd_scr = refs[-9:]
    q = CHUNK
    hd = RET_HEAD_DIM
    hm = _head_masks(D_RET, RET_HEADS)

    @pl.when(pl.program_id(0) == 0)
    def _():
        log_g = -_softplus(-dl_ref[...])
        ii = lax.broadcasted_iota(jnp.int32, (q, q), 0)
        jj = lax.broadcasted_iota(jnp.int32, (q, q), 1)
        dij = (ii - jj).astype(F32)
        ri = lax.broadcasted_iota(jnp.int32, (q, 1), 0).astype(F32)
        lfs, lbs = [], []
        for h in range(RET_HEADS):
            lf = log_g[:, h:h + 1]
            lb = log_g[:, RET_HEADS + h:RET_HEADS + h + 1]
            lfs.append(lf)
            lbs.append(lb)
            d_f = jnp.exp(jnp.where(dij >= 0, dij * lf, -jnp.inf))
            d_b = jnp.exp(jnp.where(dij <= 0, -dij * lb, -jnp.inf))
            dm_scr[:, h * q:(h + 1) * q] = d_f + d_b
        lf_l = _by_head(hm, lfs)
        lb_l = _by_head(hm, lbs)
        e_scr[0] = jnp.exp((ri + 1.0) * lf_l)
        e_scr[1] = jnp.exp((q - ri) * lb_l)
        e_scr[2] = jnp.exp((q - 1.0 - ri) * lf_l)
        e_scr[3] = jnp.exp(ri * lb_l)
        ea_scr[0:1, :] = jnp.exp(q * lf_l)
        ea_scr[1:2, :] = jnp.exp(q * lb_l)
        bd_scr[...] = _block_diag(D_RET, hd, 1.0)

    class Seq:
        def __init__(self, g):
            self.u, self.y, self.s = u_ref.at[g], y_ref.at[g], s_ref.at[g]
            self.s0 = s0_ref.at[g] if has_s0 else None
            self.yf, self.yb, self.st = yf_scr.at[g], yb_scr.at[g], st_scr.at[g]

    seqs = [Seq(g) for g in range(ng)]

    for sq in seqs:
        for d in range(2):
            for h in range(RET_HEADS):
                blk = slice(h * hd, (h + 1) * hd)
                if has_s0:
                    parts = [jnp.zeros((hd, hd), F32)] * RET_HEADS
                    parts[h] = sq.s0[d, h]
                    sq.st[d, blk, :] = jnp.concatenate(parts, axis=1)
                else:
                    sq.st[d, blk, :] = jnp.zeros((hd, D_RET), F32)

    def qkv(sq, rows):
        return (sq.u[rows, 0:D_RET], sq.u[rows, D_RET:2 * D_RET].astype(F32) * (RET_HEAD_DIM ** -0.5),
                sq.u[rows, 2 * D_RET:3 * D_RET])

    def state_step(sq, d, qq, kk, vv):
        s_old = sq.st[d]
        y = e_scr[d] * _bdot(qq, s_old)
        sq.st[d] = ea_scr[d:d + 1, :] * s_old + _bdot_tn(kk * e_scr[2 + d], vv) * bd_scr[...]
        return y

    def scan(i, carry):
        rows = _rows(i)
        rows_b = _rows(nc - 1 - i)
        for sq in seqs:
            qq, kk, vv = qkv(sq, rows)
            sc = _bdot_nt(qq, _stack_heads(kk, hm)) * dm_scr[...]
            sq.yf[rows, :] = _bdot(sc, _stack_heads(vv, hm)) + state_step(sq, 0, qq, kk, vv)
            qq, kk, vv = qkv(sq, rows_b)
            sq.yb[rows_b, :] = state_step(sq, 1, qq, kk, vv)
        return carry

    lax.fori_loop(0, nc, scan, 0)

    for sq in seqs:
        for d in range(2):
            for h in range(RET_HEADS):
                sq.s[d, h] = sq.st[d, h * hd:(h + 1) * hd, h * hd:(h + 1) * hd]

    gn = gn_ref[...]

    def fin(c, carry):
        rows = _rows(c)
        for sq in seqs:
            y = sq.yf[rows, :] + sq.yb[rows, :]
            cen = y - _seg_mean(y, hd)
            var = _seg_mean(cen * cen, hd)
            sq.y[rows, :] = cen * lax.rsqrt(var + NORM_EPS) * gn * _silu(sq.u[rows, 3 * D_RET:4 * D_RET].astype(F32))
        return carry

    lax.fori_loop(0, nc, fin, 0)


def _retention(u, wts, li, s0, nb, l, stacked=False, prev=None):
    nc = l // CHUNK
    has_s0 = s0 is not None
    names = ("ret_decay_logit", "ret_gn_w")
    ng = SEQ_GROUP
    args = [u.reshape(nb, l, RET_IN)] + [wts[n] for n in names]
    in_specs = [_seq_spec(l, RET_IN, ng)] + [_layer_const(wts[n], li) for n in names]
    if has_s0:
        args.append(s0)
        in_specs.append(pl.BlockSpec((ng, None, 2, RET_HEADS, RET_HEAD_DIM, RET_HEAD_DIM),
                                     lambda b: (b, li, 0, 0, 0, 0)))
    st_spec, st_shape = _slot_out((2, RET_HEADS, RET_HEAD_DIM, RET_HEAD_DIM), nb, li, stacked, ng)
    aliases = _carry_prev(args, in_specs, [] if prev is None else [prev], 1)
    y, s = pl.pallas_call(
        functools.partial(_ret_kernel, nc, ng, has_s0),
        grid=(nb // ng,),
        in_specs=in_specs,
        out_specs=[_seq_spec(l, D_RET, ng), st_spec],
        out_shape=[jax.ShapeDtypeStruct((nb, l, D_RET), F32), st_shape],
        input_output_aliases=aliases,
        scratch_shapes=[pltpu.VMEM((ng, l, D_RET), F32), pltpu.VMEM((ng, l, D_RET), F32),
                        pltpu.VMEM((ng, 2, D_RET, D_RET), F32), pltpu.VMEM((CHUNK, RET_HEADS * CHUNK), F32),
                        pltpu.VMEM((4, CHUNK, D_RET), F32), pltpu.VMEM((8, D_RET), F32),
                        pltpu.VMEM((D_RET, D_RET), F32)],
        compiler_params=_cparams("arbitrary"),
        name="retention",
    )(*args)
    return y.reshape(nb * l, D_RET), s


def _split(x):
    hi = x.astype(BF16)
    return hi, (x - hi.astype(F32)).astype(BF16)


def _dot3(a_hi, a_lo, b_hi, b_lo):
    d = lambda p, q: jnp.dot(p, q, preferred_element_type=F32)
    return d(a_hi, b_hi) + (d(a_lo, b_hi) + d(a_hi, b_lo))


@functools.lru_cache(maxsize=None)
def _dft_fwd_host(l):
    n = 2 * l
    f = np.arange(l, dtype=np.int64)[:, None]
    s = np.arange(l, dtype=np.int64)[None, :]
    ang = ((f * s) % n).astype(np.float64) * (2.0 * math.pi / n)
    im = -np.sin(ang)
    im[0] = np.where(np.arange(l) % 2 == 0, 1.0, -1.0)
    return np.concatenate([np.cos(ang), im], axis=0).astype(np.float32)


def _dft_tables(l):
    n = 2 * l
    fwd = jnp.asarray(_dft_fwd_host(l))
    wgt = np.full((n, 1), 2.0 / n, np.float32)
    wgt[0] = wgt[l] = 1.0 / n
    return _split(fwd) + _split((fwd * wgt).T)


def _dft_block_tables(tables, l):
    fwd_hi, fwd_lo, inv_hi, inv_lo = tables
    r = DFT_BLOCK
    nblk = l // r
    fb = lambda t: (t[0:l].reshape(nblk, r, l), t[l:2 * l].reshape(nblk, r, l))
    fwd_blk = jnp.concatenate(fb(fwd_hi) + fb(fwd_lo), axis=1)
    inv_blk = jnp.concatenate([inv_hi.reshape(nblk, r, 2 * l), inv_lo.reshape(nblk, r, 2 * l)], axis=1)
    return fwd_blk, inv_blk


def _hy_filter_kernel(l, feats_ref, dec_ref, w1_ref, b1_ref, w2_ref, b2_ref, w3_ref, fr_ref, fh_ref, fl_ref,
                      a_ref, b_ref, d_ref):
    fr = fr_ref[...]
    xdot = lambda a, b: _dot3(*_split(a), *_split(b))
    h = jnp.sin(fr * (xdot(feats_ref[...], w1_ref[...]) + b1_ref[...]))
    h = jnp.sin(fr * (xdot(h, w2_ref[...]) + b2_ref[...]))
    h = xdot(h, w3_ref[...])
    dec = jnp.concatenate([dec_ref[...]] * HY_ORDER, axis=-1)
    row0 = lax.broadcasted_iota(jnp.int32, (l, 1), 0) == 0
    hf = h[:, 0:HY_ORDER * D_HY] * dec
    hb = h[:, HY_ORDER * D_HY:2 * HY_ORDER * D_HY] * dec
    hb = jnp.where(row0, 0.0, hb)
    hs = _split(hf + hb)
    hd = _split(hf - hb)
    re = _dot3(fh_ref[0:l, :], fl_ref[0:l, :], *hs)
    ny = _dot3(fh_ref[l:l + 8, :], fl_ref[l:l + 8, :], *hs)[0:1]
    im = _dot3(fh_ref[l:2 * l, :], fl_ref[l:2 * l, :], *hd)
    for o in range(HY_ORDER):
        cols = slice(o * D_HY, (o + 1) * D_HY)
        a_ref[o] = re[:, cols]
        b_ref[o] = jnp.where(row0, 0.0, im[:, cols])
        d_ref[o] = jnp.where(row0, ny[:, cols], re[:, cols])


def _hy_filter(l, fwd_hi, fwd_lo, wts, li):
    pos = np.arange(l, dtype=np.float32)
    t = pos / np.float32(l - 1)
    bands = np.linspace(1e-4, HY_BANDS - 1, HY_BANDS, dtype=np.float32)
    ang = np.float32(2.0 * math.pi / l) * pos[:, None] * bands[None, :]
    feats = np.concatenate([t[:, None], np.cos(ang), -np.sin(ang)], axis=-1).astype(np.float32)
    feats = np.pad(feats, ((0, 0), (0, LANES - HY_EMB)))
    max_decay = math.log(HY_TARGET) / HY_FAST_DECAY
    min_decay = math.log(HY_TARGET) / HY_SLOW_DECAY
    deltas = np.abs(np.linspace(min_decay, max_decay, D_HY, dtype=np.float32))
    dec = np.exp(-t[:, None] * deltas[None, :]).astype(np.float32)
    spec = jax.ShapeDtypeStruct((HY_ORDER, l, D_HY), F32)
    names = ("hy_w1", "hy_b1", "hy_w2", "hy_b2", "hy_w3", "hy_freq")
    return pl.pallas_call(
        functools.partial(_hy_filter_kernel, l),
        grid=(1,),
        in_specs=[_full(feats.shape), _full(dec.shape)] + [_layer_const(wts[n], li) for n in names]
                 + [_full(fwd_hi.shape), _full(fwd_lo.shape)],
        out_specs=[_full(spec.shape)] * 3,
        out_shape=[spec] * 3,
        compiler_params=_cparams("arbitrary"),
        name="hyena_filter",
    )(jnp.asarray(feats), jnp.asarray(dec), *[wts[n] for n in names], fwd_hi, fwd_lo)


def _hy_kernel(nc, ng, u_ref, cw_ref, cb_ref, f_ref, g_ref, a_ref, b_ref, d_ref, bias_ref, y_ref,
               uc_scr, vh_scr, vl_scr, sh_scr, sl_scr, z_scr):
    l = nc * CHUNK
    r = DFT_BLOCK
    nblk = l // r
    cw = cw_ref[...]
    cb = cb_ref[...]
    dot = lambda p, q: jnp.dot(p, q, preferred_element_type=F32)
    wide = lambda x: jnp.concatenate([x] * ng, axis=1)
    seq_cols = lambda g: slice(g * D_HY, (g + 1) * D_HY)

    def for_blocks(body):
        if nblk == 1:
            body(0)
        else:
            lax.fori_loop(0, nblk, lambda i, carry: (body(i), carry)[1], 0)

    def conv(c, carry):
        rows = _rows(c)
        for g in range(ng):
            uc = _conv3_chunk(u_ref.at[g], c, nc, cw, cb)
            uc_scr[g, rows, :] = uc
            vh_scr[rows, seq_cols(g)], vl_scr[rows, seq_cols(g)] = _split(uc[:, 0:D_HY])
        return carry

    lax.fori_loop(0, nc, conv, 0)

    def long_conv(o):
        def spectrum(i):
            rows = pl.ds(pl.multiple_of(i * r, r), r)
            rows_im = pl.ds(pl.multiple_of(l + i * r, r), r)
            p = dot(f_ref[i], vh_scr[...])
            pl_ = dot(f_ref[i, 0:2 * r, :], vl_scr[...])
            zr = p[0:r] + (p[2 * r:3 * r] + pl_[0:r])
            zi = p[r:2 * r] + (p[3 * r:4 * r] + pl_[r:2 * r])
            fa, fb, fd = wide(a_ref[o, rows, :]), wide(b_ref[o, rows, :]), wide(d_ref[o, rows, :])
            sh_scr[rows, :], sl_scr[rows, :] = _split(zr * fa - zi * fb)
            sh_scr[rows_im, :], sl_scr[rows_im, :] = _split(zr * fb + zi * fd)

        for_blocks(spectrum)

        def inverse(i):
            rows = pl.ds(pl.multiple_of(i * r, r), r)
            p = dot(g_ref[i], sh_scr[...])
            y = p[0:r] + (p[r:2 * r] + dot(g_ref[i, 0:r, :], sl_scr[...]))
            gate = jnp.concatenate([uc_scr[g, rows, (o + 1) * D_HY:(o + 2) * D_HY] for g in range(ng)], axis=1)
            if o == 0:
                v = jnp.concatenate([uc_scr[g, rows, 0:D_HY] for g in range(ng)], axis=1)
            else:
                v = z_scr[rows, :]
            out = gate * (y + v * wide(bias_ref[o:o + 1, :]))
            if o + 1 < HY_ORDER:
                z_scr[rows, :] = out
                vh_scr[rows, :], vl_scr[rows, :] = _split(out)
            else:
                for g in range(ng):
                    y_ref[g, rows, :] = out[:, seq_cols(g)]

        for_blocks(inverse)

    for o in range(HY_ORDER):
        long_conv(o)


def _hyena(u, wts, li, tables, spectra, nb, l):
    nc = l // CHUNK
    ng = HY_GROUP
    a, b, d = spectra
    names = ("hy_conv_w", "hy_conv_b")
    y = pl.pallas_call(
        functools.partial(_hy_kernel, nc, ng),
        grid=(nb // ng,),
        in_specs=[pl.BlockSpec((ng, l, HY_IN), lambda i: (i, 0, 0))] + [_layer_const(wts[n], li) for n in names]
                 + [_const(tables[0].shape), _const(tables[1].shape),
                    _const((HY_ORDER, l, D_HY)), _const((HY_ORDER, l, D_HY)), _const((HY_ORDER, l, D_HY)),
                    _layer_const(wts["hy_bias"], li)],
        out_specs=pl.BlockSpec((ng, l, D_HY), lambda i: (i, 0, 0)),
        out_shape=jax.ShapeDtypeStruct((nb, l, D_HY), F32),
        scratch_shapes=[pltpu.VMEM((ng, l, HY_IN), F32),
                        pltpu.VMEM((l, ng * D_HY), BF16), pltpu.VMEM((l, ng * D_HY), BF16),
                        pltpu.VMEM((2 * l, ng * D_HY), BF16), pltpu.VMEM((2 * l, ng * D_HY), BF16),
                        pltpu.VMEM((l, ng * D_HY), F32)],
        compiler_params=_cparams("parallel"),
        name="hyena",
    )(u.reshape(nb, l, HY_IN), *[wts[n] for n in names], *tables, a, b, d, wts["hy_bias"])
    return y.reshape(nb * l, D_HY)


def _seg_rms(x, w):
    return x * lax.rsqrt(_seg_mean(x * x, HEAD_DIM) + NORM_EPS) * w


def _stack_q(q):
    lo = lax.broadcasted_iota(jnp.int32, (1, D_KV), 1) < HEAD_DIM
    qa = q[:, 0:D_KV]
    qb = q[:, D_KV:2 * D_KV]
    return jnp.concatenate([jnp.where(lo, qa, 0.0), jnp.where(lo, pltpu.roll(qa, HEAD_DIM, 1), 0.0),
                            jnp.where(lo, 0.0, pltpu.roll(qb, HEAD_DIM, 1)), jnp.where(lo, 0.0, qb)], axis=0)


def _unstack_o(o):
    r = o.shape[0] // ATT_HEADS
    lo = lax.broadcasted_iota(jnp.int32, (1, D_KV), 1) < HEAD_DIM
    ya = jnp.where(lo, o[0:r], pltpu.roll(o[r:2 * r], HEAD_DIM, 1))
    yb = jnp.where(lo, pltpu.roll(o[2 * r:3 * r], HEAD_DIM, 1), o[3 * r:4 * r])
    return jnp.concatenate([ya, yb], axis=1)


def _sink_col(sink_ref, li, r):
    rb = lax.broadcasted_iota(jnp.int32, (ATT_HEADS * r, 1), 0) // r
    col = jnp.full((ATT_HEADS * r, 1), sink_ref[li * ATT_HEADS + ATT_HEADS - 1], F32)
    for h in range(ATT_HEADS - 2, -1, -1):
        col = jnp.where(rb == h, sink_ref[li * ATT_HEADS + h], col)
    return col


def _ctx_attn_kernel(li, ng, u_ref, qn_ref, kn_ref, sink_ref, *rest):
    y_ref, k_ref, v_ref = rest[-3:]
    l = u_ref.shape[1]
    sink = _sink_col(sink_ref, li, l)
    for g in range(ng):
        u = u_ref[g].astype(F32)
        q = _seg_rms(u[:, 0:D_ATT], qn_ref[...])
        k = _seg_rms(u[:, D_ATT:D_ATT + D_KV], kn_ref[:, 0:D_KV])
        v = u[:, D_ATT + D_KV:D_ATT + 2 * D_KV]
        k_ref[g] = k
        v_ref[g] = v
        s = _bdot_nt(_stack_q(q), k) * (HEAD_DIM ** -0.5)
        m = jnp.maximum(jnp.max(s, axis=-1, keepdims=True), sink)
        p = jnp.exp(s - m)
        den = jnp.sum(p, axis=-1, keepdims=True) + jnp.exp(sink - m)
        y_ref[g] = _unstack_o(_bdot(p * (1.0 / den), v))


def _ctx_attention(u, wts, li, nb, l, prev=None):
    ng = ATT_GROUP
    kv_spec, kv_shape = _slot_out((l, D_KV), nb, li, True, ng)
    args = [u.reshape(nb, l, ATT_IN), wts["attn_q_norm"], wts["attn_k_norm"], wts["attn_sink"]]
    in_specs = [_seq_spec(l, ATT_IN, ng), _layer_const(wts["attn_q_norm"], li),
                _layer_const(wts["attn_k_norm"], li), pl.BlockSpec(memory_space=pltpu.SMEM)]
    aliases = _carry_prev(args, in_specs, [] if prev is None else list(prev), 1)
    y, k, v = pl.pallas_call(
        functools.partial(_ctx_attn_kernel, li, ng),
        grid=(nb // ng,),
        in_specs=in_specs,
        out_specs=[_seq_spec(l, D_ATT, ng), kv_spec, kv_spec],
        out_shape=[jax.ShapeDtypeStruct((nb, l, D_ATT), F32), kv_shape, kv_shape],
        input_output_aliases=aliases,
        compiler_params=_cparams("parallel"),
        name="ctx_attention",
    )(*args)
    return y.reshape(nb * l, D_ATT), k, v


@functools.lru_cache(maxsize=None)
def _rope_tables_host(l):
    n_rows = l // GRID_W
    rows = np.repeat(np.arange(n_rows, dtype=np.float32), GRID_W)
    cols = np.tile(np.arange(GRID_W, dtype=np.float32), n_rows)
    nf = HEAD_DIM // 4
    inv = (np.float32(ROPE_BASE) ** (-np.arange(nf, dtype=np.float32) / np.float32(nf))).astype(np.float32)
    ar = rows[:, None] * inv[None, :]
    ac = cols[:, None] * inv[None, :]
    cos = np.concatenate([np.cos(ar), np.cos(ar), np.cos(ac), np.cos(ac)], axis=-1)
    sin = np.concatenate([-np.sin(ar), np.sin(ar), -np.sin(ac), np.sin(ac)], axis=-1)
    return (np.tile(cos, (1, ATT_HEADS)).astype(np.float32), np.tile(sin, (1, ATT_HEADS)).astype(np.float32))


def _rope(x, cos, sin):
    w = x.shape[-1]
    nf = HEAD_DIM // 4
    lane = lax.broadcasted_iota(jnp.int32, x.shape, 1)
    first = (lane % (2 * nf)) < nf
    partner = jnp.where(first, pltpu.roll(x, w - nf, 1), pltpu.roll(x, nf, 1))
    return x * cos + partner * sin


def _lat_attn_kernel(nblk, li, u_ref, qn_ref, kn_ref, cos_ref, sin_ref, ck_ref, cv_ref, sink_ref,
                     y_ref, q_scr, k_scr, v_scr):
    blk = ATT_BLOCK
    l = nblk * blk
    cos_q = cos_ref[...]
    sin_q = sin_ref[...]
    u = u_ref[...].astype(F32)
    q = _seg_rms(u[:, 0:D_ATT], qn_ref[...])
    k = _seg_rms(u[:, D_ATT:D_ATT + D_KV], kn_ref[:, 0:D_KV])
    q_scr[...] = _rope(q, cos_q, sin_q)
    zeros = jnp.zeros((blk, D_KV), F32)
    k_scr[0:blk, :] = zeros
    k_scr[blk + l:2 * blk + l, :] = zeros
    v_scr[0:blk, :] = zeros
    v_scr[blk + l:2 * blk + l, :] = zeros
    k_scr[blk:blk + l, :] = _rope(k, cos_q[:, 0:D_KV], sin_q[:, 0:D_KV])
    v_scr[blk:blk + l, :] = u[:, D_ATT + D_KV:D_ATT + 2 * D_KV]

    scale = HEAD_DIM ** -0.5
    r = lax.broadcasted_iota(jnp.int32, (blk, 3 * blk), 0)
    cidx = lax.broadcasted_iota(jnp.int32, (blk, 3 * blk), 1)
    band = (cidx - r >= blk - WINDOW) & (cidx - r <= blk + WINDOW)
    sink = _sink_col(sink_ref, li, blk)

    def block(i, carry):
        rows = _rows(i)
        win = pl.ds(pl.multiple_of(i * blk, blk), 3 * blk)
        kpos = cidx + (i - 1) * blk
        valid = band & (kpos >= 0) & (kpos < l)
        valid = jnp.concatenate([valid] * ATT_HEADS, axis=0)
        qs = _stack_q(q_scr[rows, :])
        s_loc = jnp.where(valid, _bdot_nt(qs, k_scr[win, :]) * scale, -jnp.inf)
        s_ctx = _bdot_nt(qs, ck_ref[...]) * scale
        m = jnp.maximum(jnp.maximum(jnp.max(s_loc, axis=-1, keepdims=True),
                                    jnp.max(s_ctx, axis=-1, keepdims=True)), sink)
        p_loc = jnp.exp(s_loc - m)
        p_ctx = jnp.exp(s_ctx - m)
        den = (jnp.sum(p_loc, axis=-1, keepdims=True) + jnp.sum(p_ctx, axis=-1, keepdims=True)
               + jnp.exp(sink - m))
        inv = 1.0 / den
        y_ref[rows, :] = _unstack_o(_bdot(p_ctx * inv, cv_ref[...]) + _bdot(p_loc * inv, v_scr[win, :]))
        return carry

    lax.fori_loop(0, nblk, block, 0)


def _lat_attention(u, wts, li, ck, cv, nb, l):
    lc = ck.shape[2]
    nblk = l // ATT_BLOCK
    cos, sin = _rope_tables_host(l)
    cache_spec = pl.BlockSpec((None, None, lc, D_KV), lambda b: (b, li, 0, 0))
    y = pl.pallas_call(
        functools.partial(_lat_attn_kernel, nblk, li),
        grid=(nb,),
        in_specs=[_seq_spec(l, ATT_IN), _layer_const(wts["attn_q_norm"], li), _layer_const(wts["attn_k_norm"], li),
                  _const((l, D_ATT)), _const((l, D_ATT)), cache_spec, cache_spec,
                  pl.BlockSpec(memory_space=pltpu.SMEM)],
        out_specs=_seq_spec(l, D_ATT),
        out_shape=jax.ShapeDtypeStruct((nb, l, D_ATT), F32),
        scratch_shapes=[pltpu.VMEM((l, D_ATT), F32), pltpu.VMEM((l + 2 * ATT_BLOCK, D_KV), F32),
                        pltpu.VMEM((l + 2 * ATT_BLOCK, D_KV), F32)],
        compiler_params=_cparams("parallel"),
        name="lat_attention",
    )(u.reshape(nb, l, ATT_IN), wts["attn_q_norm"], wts["attn_k_norm"], jnp.asarray(cos), jnp.asarray(sin),
      ck, cv, wts["attn_sink"])
    return y.reshape(nb * l, D_ATT)


def _dt_weight(w):
    zeros = jnp.zeros(w.shape[:-1] + (DT_PAD - DT_REP * N_DT,), w.dtype)
    return jnp.concatenate([w[..., C_DT:C_DT + N_DT]] * DT_REP + [zeros], axis=-1).astype(BF16)


def _prep_weights(p):
    row = lambda a: a.reshape(DEPTH, 1, -1)
    pad_lanes = lambda a: jnp.pad(row(a), ((0, 0), (0, 0), (0, LANES - a[0].size)))
    return dict(
        norm_w=p["norm_w"].reshape(DEPTH * 3, 1, D_MODEL),
        ffn_w_in=p["ffn_w_in"].astype(BF16).reshape(DEPTH * 2, D_MODEL, 2 * D_FF),
        ffn_w_out=p["ffn_w_out"].astype(BF16).reshape(DEPTH * 2, D_FF, D_MODEL),
        mix_w_in=p["mix_w_in"].astype(BF16), mix_w_dt=_dt_weight(p["mix_w_in"]),
        mix_w_out=p["mix_w_out"].astype(BF16),
        ssd_conv_w=p["ssd_conv_w"], ssd_conv_b=row(p["ssd_conv_b"]),
        ssd_dt_bias=pad_lanes(jnp.tile(row(p["ssd_dt_bias"]), (1, 1, DT_REP))),
        ssd_a_log=pad_lanes(jnp.tile(row(p["ssd_a_log"]), (1, 1, DT_REP))),
        ssd_d=row(jnp.repeat(p["ssd_d"], SSM_HEAD_DIM, axis=-1)), ssd_norm_w=row(p["ssd_norm_w"]),
        hy_conv_w=p["hy_conv_w"], hy_conv_b=row(p["hy_conv_b"]), hy_bias=p["hy_bias"],
        hy_w1=jnp.pad(p["hy_w1"], ((0, 0), (0, LANES - HY_EMB), (0, 0))), hy_b1=row(p["hy_b1"]),
        hy_w2=p["hy_w2"], hy_b2=row(p["hy_b2"]), hy_w3=p["hy_w3"], hy_freq=row(p["hy_freq"]),
        ret_decay_logit=pad_lanes(p["ret_decay_logit"]), ret_gn_w=row(p["ret_gn_w"]),
        attn_q_norm=row(jnp.tile(p["attn_q_norm"], (1, ATT_HEADS))),
        attn_k_norm=row(jnp.tile(p["attn_k_norm"], (1, ATT_HEADS))),
        attn_sink=p["attn_sink"].reshape(DEPTH * ATT_HEADS),
    )


def _layer(x, mod, row0, wts, li, nb, l, ssd_s0, ret_s0, ctx_kv, hy_tables, hy_spectra, carried):
    ctx = ctx_kv is None
    rpm = nb * l if ctx else l
    x = _ffn(x, mod, row0, rpm, wts, li, 0)
    z, xbc, hy, ret, att, dt = _inproj(x, mod, row0, rpm, wts, li)
    y_ssd, s_ssd = _ssd(z, xbc, dt, wts, li, ssd_s0, nb, l, stacked=ctx, prev=carried.get("ssd"))
    y_hy = _hyena(hy, wts, li, hy_tables, hy_spectra, nb, l)
    y_ret, s_ret = _retention(ret, wts, li, ret_s0, nb, l, stacked=ctx, prev=carried.get("ret"))
    if ctx:
        y_att, k, v = _ctx_attention(att, wts, li, nb, l, prev=carried.get("kv"))
        carried = dict(ssd=s_ssd, ret=s_ret, kv=(k, v))
    else:
        y_att = _lat_attention(att, wts, li, ctx_kv[0], ctx_kv[1], nb, l)
    x = _ffn(x, mod, row0, rpm, wts, li, 1, mix=(y_ssd, y_hy, y_ret, y_att))
    return x, carried


def kernel(x_prompt, x_sample, cache_k, cache_v, state_ssd, state_ret, c, c_ctx, w_mod, b_mod, norm_w, ffn_w_in, ffn_w_out, mix_w_in, mix_w_out, ssd_conv_w, ssd_conv_b, ssd_dt_bias, ssd_a_log, ssd_d, ssd_norm_w, hy_conv_w, hy_conv_b, hy_w1, hy_b1, hy_w2, hy_b2, hy_w3, hy_freq, hy_bias, ret_decay_logit, ret_gn_w, attn_q_norm, attn_k_norm, attn_sink):
    bp, lp_len, _ = x_prompt.shape
    bs, ls_len, _ = x_sample.shape
    lc = cache_k.shape[2]

    cond = jnp.concatenate([c_ctx[None, :], c, jnp.zeros((MOD_ROWS - 1 - bs, D_MODEL), F32)], axis=0)
    mod = _modulation(cond, w_mod, b_mod)
    wts = _prep_weights(dict(
        norm_w=norm_w, ffn_w_in=ffn_w_in, ffn_w_out=ffn_w_out, mix_w_in=mix_w_in, mix_w_out=mix_w_out,
        ssd_conv_w=ssd_conv_w, ssd_conv_b=ssd_conv_b, ssd_dt_bias=ssd_dt_bias, ssd_a_log=ssd_a_log, ssd_d=ssd_d,
        ssd_norm_w=ssd_norm_w, hy_conv_w=hy_conv_w, hy_conv_b=hy_conv_b, hy_w1=hy_w1, hy_b1=hy_b1, hy_w2=hy_w2,
        hy_b2=hy_b2, hy_w3=hy_w3, hy_freq=hy_freq, hy_bias=hy_bias, ret_decay_logit=ret_decay_logit,
        ret_gn_w=ret_gn_w, attn_q_norm=attn_q_norm, attn_k_norm=attn_k_norm, attn_sink=attn_sink))
    ck = cache_k.reshape(bs, DEPTH, lc, D_KV)
    cv = cache_v.reshape(bs, DEPTH, lc, D_KV)

    tab_p = _dft_tables(lp_len)
    tab_s = _dft_tables(ls_len)
    blk_p = _dft_block_tables(tab_p, lp_len)
    blk_s = _dft_block_tables(tab_s, ls_len)

    yp = x_prompt.reshape(bp * lp_len, D_MODEL)
    ys = x_sample.reshape(bs * ls_len, D_MODEL)
    carried = dict(
        ssd=jnp.zeros((bp, DEPTH, 2, SSM_HEADS, SSM_STATE, SSM_HEAD_DIM), F32),
        ret=jnp.zeros((bp, DEPTH, 2, RET_HEADS, RET_HEAD_DIM, RET_HEAD_DIM), F32),
        kv=(jnp.zeros((bp, DEPTH, lp_len, D_KV), F32), jnp.zeros((bp, DEPTH, lp_len, D_KV), F32)))
    for li in range(DEPTH):
        spec_p = _hy_filter(lp_len, tab_p[0], tab_p[1], wts, li)
        spec_s = _hy_filter(ls_len, tab_s[0], tab_s[1], wts, li)
        yp, carried = _layer(yp, mod, li * MOD_ROWS, wts, li, bp, lp_len, None, None, None, blk_p, spec_p, carried)
        ys, _ = _layer(ys, mod, li * MOD_ROWS + 1, wts, li, bs, ls_len, state_ssd, state_ret, (ck, cv),
                       blk_s, spec_s, {})

    kv_shape = (bp, DEPTH, lp_len, ATT_KV_HEADS, HEAD_DIM)
    new_k, new_v = carried["kv"]
    return (yp.reshape(bp, lp_len, D_MODEL), ys.reshape(bs, ls_len, D_MODEL),
            new_k.reshape(kv_shape), new_v.reshape(kv_shape), carried["ssd"], carried["ret"])
```

```python
import functools
import math

import numpy as np
import jax
import jax.numpy as jnp
from jax import lax
from jax.experimental import pallas as pl
from jax.experimental.pallas import tpu as pltpu

F32 = jnp.float32
BF16 = jnp.bfloat16
HI = lax.Precision.HIGHEST

D_MODEL = 1024
DEPTH = 2
GRID_W = 64
D_FF = 2816
N_MOD = 9
NORM_EPS = 1e-6
CHUNK = 128
D_SSM = 256
SSM_HEADS = 4
SSM_HEAD_DIM = 64
SSM_STATE = 128
SSM_GROUPS = 2
SSM_CONV_CH = D_SSM + 2 * SSM_GROUPS * SSM_STATE
D_HY = 256
HY_ORDER = 2
HY_BANDS = 16
HY_EMB = 1 + 2 * HY_BANDS
HY_HIDDEN = 64
HY_FAST_DECAY = 0.3
HY_SLOW_DECAY = 1.5
HY_TARGET = 1e-2
HY_IN = (HY_ORDER + 1) * D_HY
D_RET = 256
RET_HEADS = 4
RET_HEAD_DIM = 64
RET_IN = 4 * D_RET
ATT_HEADS = 4
ATT_KV_HEADS = 2
HEAD_DIM = 64
D_ATT = ATT_HEADS * HEAD_DIM
D_KV = ATT_KV_HEADS * HEAD_DIM
ATT_IN = D_ATT + 2 * D_KV
WINDOW = 128
ATT_BLOCK = 128
ROPE_BASE = 10000.0
D_MIX = D_SSM + D_HY + D_RET + D_ATT

LANES = 128
DT_PAD = LANES
VMEM_LIMIT = 56 * 1024 * 1024
MOD_ROWS = 8

TOKEN_TILE = 512
DFT_BLOCK = 256
HY_GROUP = 2
SEQ_GROUP = 2
ATT_GROUP = 2


def _cparams(*sem):
    return pltpu.CompilerParams(dimension_semantics=sem, vmem_limit_bytes=VMEM_LIMIT)


def _rms(x, w):
    return x * lax.rsqrt(jnp.mean(x * x, axis=-1, keepdims=True) + NORM_EPS) * w


def _silu(x):
    return x * (1.0 / (1.0 + jnp.exp(-x)))


def _softplus(x):
    return jnp.maximum(x, 0.0) + jnp.log1p(jnp.exp(-jnp.abs(x)))


def _bdot(a, b):
    return jnp.dot(a.astype(BF16), b.astype(BF16), preferred_element_type=F32)


def _bdot_nt(a, b):
    return lax.dot_general(a.astype(BF16), b.astype(BF16), (((1,), (1,)), ((), ())),
                           preferred_element_type=F32)


def _bdot_tn(a, b):
    return lax.dot_general(a.astype(BF16), b.astype(BF16), (((0,), (0,)), ((), ())),
                           preferred_element_type=F32)


def _hdot(a, b):
    return jnp.dot(a, b, preferred_element_type=F32, precision=HI)


def _full(shape):
    n = len(shape)
    return pl.BlockSpec(shape, lambda *_: (0,) * n)


def _const(shape):
    n = len(shape)
    return pl.BlockSpec(shape, lambda *_: (0,) * n, pipeline_mode=pl.Buffered(1))


def _layer_const(arr, li):
    tail = arr.shape[1:]
    zeros = (0,) * len(tail)
    return pl.BlockSpec((None,) + tail, lambda *_: (li,) + zeros, pipeline_mode=pl.Buffered(1))


def _mod_spec(k, tm, rows_per_mod, row0):
    return pl.BlockSpec((None, 1, D_MODEL), lambda i: (row0 + (i * tm) // rows_per_mod, 0, k))


def _seq_spec(l, w, ng=None):
    return pl.BlockSpec((ng, l, w), lambda b: (b, 0, 0))


def _head_masks(width, heads):
    lane = lax.broadcasted_iota(jnp.int32, (1, width), 1)
    hd = width // heads
    return [(lane >= h * hd) & (lane < (h + 1) * hd) for h in range(heads)]


def _by_head(masks, vals):
    out = vals[-1]
    for m, v in zip(masks[-2::-1], vals[-2::-1]):
        out = jnp.where(m, v, out)
    return out


def _block_diag(n, blk, value):
    i = lax.broadcasted_iota(jnp.int32, (n, n), 0) // blk
    j = lax.broadcasted_iota(jnp.int32, (n, n), 1) // blk
    return jnp.where(i == j, value, 0.0).astype(F32)


def _mod_kernel(c_ref, w_ref, b_ref, o_ref):
    c = c_ref[...]
    o_ref[...] = _bdot(_silu(c), w_ref[...]) + b_ref[...]


def _modulation(cond, w_mod, b_mod):
    out = pl.pallas_call(
        _mod_kernel,
        grid=(DEPTH, N_MOD),
        in_specs=[pl.BlockSpec((MOD_ROWS, D_MODEL), lambda l, j: (0, 0)),
                  pl.BlockSpec((None, D_MODEL, D_MODEL), lambda l, j: (l, 0, j)),
                  pl.BlockSpec((None, 1, D_MODEL), lambda l, j: (l, 0, j))],
        out_specs=pl.BlockSpec((None, MOD_ROWS, D_MODEL), lambda l, j: (l, 0, j)),
        out_shape=jax.ShapeDtypeStruct((DEPTH, MOD_ROWS, N_MOD * D_MODEL), F32),
        compiler_params=_cparams("arbitrary", "arbitrary"),
        name="modulation",
    )(cond, w_mod, b_mod.reshape(DEPTH, 1, N_MOD * D_MODEL))
    return out.reshape(DEPTH * MOD_ROWS, 1, N_MOD * D_MODEL)


def _ffn_kernel(n_mix, x_ref, sh_ref, sc_ref, g_ref, nw_ref, wi_ref, wo_ref, *rest):
    o_ref = rest[-1]
    x = x_ref[...]
    if n_mix:
        gm_ref, wm_ref = rest[0], rest[1 + n_mix]
        w = D_MIX // n_mix
        acc = _bdot(rest[1][...], wm_ref[0:w, :])
        for j in range(1, n_mix):
            acc += _bdot(rest[1 + j][...], wm_ref[j * w:(j + 1) * w, :])
        x = x + gm_ref[...] * acc
    h = (_rms(x, nw_ref[...]) * (1.0 + sc_ref[...]) + sh_ref[...]).astype(BF16)
    gate = jnp.dot(h, wi_ref[:, 0:D_FF], preferred_element_type=F32)
    up = jnp.dot(h, wi_ref[:, D_FF:2 * D_FF], preferred_element_type=F32)
    o_ref[...] = x + 0.5 * g_ref[...] * _bdot(_silu(gate) * up, wo_ref[...])


def _ffn(x, mod, row0, rows_per_mod, wts, li, k, mix=()):
    t = x.shape[0]
    tm = min(TOKEN_TILE, rows_per_mod)
    w_in, w_out = wts["ffn_w_in"], wts["ffn_w_out"]
    args = [x, mod, mod, mod, wts["norm_w"], w_in, w_out]
    in_specs = [pl.BlockSpec((tm, D_MODEL), lambda i: (i, 0)),
                _mod_spec(6 * k, tm, rows_per_mod, row0),
                _mod_spec(6 * k + 1, tm, rows_per_mod, row0),
                _mod_spec(6 * k + 2, tm, rows_per_mod, row0),
                _layer_const(wts["norm_w"], 3 * li + 2 * k),
                _layer_const(w_in, 2 * li + k), _layer_const(w_out, 2 * li + k)]
    if mix:
        args += [mod, *mix, wts["mix_w_out"]]
        in_specs += [_mod_spec(5, tm, rows_per_mod, row0)]
        in_specs += [pl.BlockSpec((tm, y.shape[1]), lambda i: (i, 0)) for y in mix]
        in_specs += [_layer_const(wts["mix_w_out"], li)]
    return pl.pallas_call(
        functools.partial(_ffn_kernel, len(mix)),
        grid=(t // tm,),
        in_specs=in_specs,
        out_specs=pl.BlockSpec((tm, D_MODEL), lambda i: (i, 0)),
        out_shape=jax.ShapeDtypeStruct((t, D_MODEL), F32),
        compiler_params=_cparams("parallel"),
        name="ffn",
    )(*args)


_IN_SPLITS = (("z", D_SSM, BF16), ("xbc", SSM_CONV_CH, F32), ("hy", HY_IN, F32), ("ret", RET_IN, BF16),
              ("att", ATT_IN, BF16), ("dt", DT_PAD, F32))


C_DT = D_SSM + SSM_CONV_CH
N_DT = 2 * SSM_HEADS
W_TAIL = HY_IN + RET_IN + ATT_IN


def _inproj_kernel(x_ref, sh_ref, sc_ref, nw_ref, w_ref, wdt_ref, *rest):
    o_refs, wt_scr = rest[:-1], rest[-1]

    @pl.when(pl.program_id(0) == 0)
    def _():
        wt_scr[...] = w_ref[:, C_DT + N_DT:C_DT + N_DT + W_TAIL]

    h = (_rms(x_ref[...], nw_ref[...]) * (1.0 + sc_ref[...]) + sh_ref[...]).astype(BF16)
    off = 0
    for (name, width, dtype), o_ref in zip(_IN_SPLITS, o_refs):
        if name == "dt":
            w = wdt_ref[...]
        elif off < C_DT:
            w = w_ref[:, off:off + width]
        else:
            w = wt_scr[:, off - C_DT:off - C_DT + width]
        o_ref[...] = jnp.dot(h, w, preferred_element_type=F32).astype(dtype)
        off += width


def _inproj(x, mod, row0, rows_per_mod, wts, li):
    t = x.shape[0]
    tm = min(TOKEN_TILE, rows_per_mod)
    return pl.pallas_call(
        _inproj_kernel,
        grid=(t // tm,),
        in_specs=[pl.BlockSpec((tm, D_MODEL), lambda i: (i, 0)),
                  _mod_spec(3, tm, rows_per_mod, row0),
                  _mod_spec(4, tm, rows_per_mod, row0),
                  _layer_const(wts["norm_w"], 3 * li + 1),
                  _layer_const(wts["mix_w_in"], li), _layer_const(wts["mix_w_dt"], li)],
        out_specs=[pl.BlockSpec((tm, width), lambda i: (i, 0)) for _, width, _ in _IN_SPLITS],
        out_shape=[jax.ShapeDtypeStruct((t, width), dtype) for _, width, dtype in _IN_SPLITS],
        scratch_shapes=[pltpu.VMEM((D_MODEL, W_TAIL), BF16)],
        compiler_params=_cparams("arbitrary"),
        name="mix_in",
    )(x, mod, mod, wts["norm_w"], wts["mix_w_in"], wts["mix_w_dt"])


def _conv3_chunk(x_ref, c, nc, w, b):
    q = CHUNK
    l = nc * q
    r0 = pl.multiple_of(c * q, q)
    x = x_ref[pl.ds(r0, q), :]
    prev = x_ref[pl.ds(jnp.maximum(r0 - 1, 0), 1), :]
    nxt = x_ref[pl.ds(jnp.minimum(r0 + q, l - 1), 1), :]
    prev = jnp.where(c > 0, prev, 0.0)
    nxt = jnp.where(c < nc - 1, nxt, 0.0)
    rid = lax.broadcasted_iota(jnp.int32, (q, 1), 0)
    xm1 = jnp.where(rid == 0, prev, pltpu.roll(x, 1, 0))
    xp1 = jnp.where(rid == q - 1, nxt, pltpu.roll(x, q - 1, 0))
    return xm1 * w[0:1, :] + x * w[1:2, :] + xp1 * w[2:3, :] + b


def _tri(lower):
    i = lax.broadcasted_iota(jnp.int32, (CHUNK, CHUNK), 0)
    j = lax.broadcasted_iota(jnp.int32, (CHUNK, CHUNK), 1)
    return (j <= i) if lower else (j >= i)


def _rows(c):
    return pl.ds(pl.multiple_of(c * CHUNK, CHUNK), CHUNK)


def _split_cat(v, parts, axis):
    out, r = [], v
    for i in range(parts):
        piece = r.astype(BF16)
        out.append(piece)
        if i + 1 < parts:
            r = r - piece.astype(F32)
    return jnp.concatenate(out, axis=axis)


def _seg_mean(x, seg):
    w = x.shape[-1]
    ones = _block_diag(w, seg, 1.0).astype(BF16)
    return jnp.dot(_split_cat(x, 2, axis=1), jnp.concatenate([ones, ones], axis=0),
                   preferred_element_type=F32) * (1.0 / seg)


def _stack_heads(x, masks):
    return jnp.concatenate([jnp.where(m, x, 0.0) for m in masks], axis=0)


def _slot_out(tail, nb, li, stacked, ng=None):
    zeros = (0,) * len(tail)
    slot = li if stacked else 0
    spec = pl.BlockSpec((ng, None) + tail, lambda b: (b, slot) + zeros)
    return spec, jax.ShapeDtypeStruct((nb, DEPTH if stacked else 1) + tail, F32)


def _carry_prev(args, in_specs, prevs, first_out):
    aliases = {}
    for j, prev in enumerate(prevs):
        aliases[len(args)] = first_out + j
        args.append(prev)
        in_specs.append(pl.BlockSpec(memory_space=pl.ANY))
    return aliases


SSD_SEL_W = 2 * SSM_HEADS * SSM_STATE
CUM_PIECES = 3
WGT_PIECES = 2
DT_REP = CUM_PIECES + WGT_PIECES


def _ssd_kernel(nc, ng, has_s0, *refs):
    refs = list(refs)
    (z_ref, xbc_ref, dt_ref, cw_ref, cb_ref, dtb_ref, alog_ref, dsk_ref, nw_ref) = refs[:9]
    s0_ref = refs[9] if has_s0 else None
    (y_ref, s_ref, xs_scr, xk_scr, xt_scr, dt_scr, cum_scr, dtt_scr, cumt_scr, yf_scr, yb_scr, st_scr,
     msk_scr, sel_scr) = refs[-14:]
    q = CHUNK
    nh, n, p = SSM_HEADS, SSM_STATE, SSM_HEAD_DIM
    rep = nh // SSM_GROUPS
    hm = _head_masks(D_SSM, nh)

    @pl.when(pl.program_id(0) == 0)
    def _():
        i = lax.broadcasted_iota(jnp.int32, (D_SSM, nh * n), 0) // p
        j = lax.broadcasted_iota(jnp.int32, (D_SSM, nh * n), 1) // n
        msk_scr[...] = jnp.where(i == j, 1.0, 0.0).astype(F32)
        k = lax.broadcasted_iota(jnp.int32, (LANES, SSD_SEL_W), 0)
        col = lax.broadcasted_iota(jnp.int32, (LANES, SSD_SEL_W), 1)
        is_cum = col < nh * n
        grp = k // (2 * nh)
        grp_ok = (is_cum & (grp < CUM_PIECES)) | ((~is_cum) & (grp >= CUM_PIECES) & (grp < DT_REP))
        hit = grp_ok & (k % nh == (col % (nh * n)) // n)
        sel_scr[...] = jnp.where(hit, 1.0, 0.0).astype(BF16)

    cw = cw_ref[...]
    cb = cb_ref[...]
    neg_a = -jnp.exp(alog_ref[...])
    dtb = dtb_ref[...]

    class Seq:
        def __init__(self, g):
            self.z, self.xbc, self.dt_in, self.y, self.s = (z_ref.at[g], xbc_ref.at[g], dt_ref.at[g], y_ref.at[g],
                                                            s_ref.at[g])
            self.s0 = s0_ref.at[g] if has_s0 else None
            self.xs, self.xk, self.xt, self.dt, self.cum = (xs_scr.at[g], xk_scr.at[g], xt_scr.at[g], dt_scr.at[g],
                                                            cum_scr.at[g])
            self.dtt, self.cumt, self.yf, self.yb, self.st = (dtt_scr.at[g], cumt_scr.at[g], yf_scr.at[g],
                                                              yb_scr.at[g], st_scr.at[g])

    seqs = [Seq(g) for g in range(ng)]

    def prep_seq(sq, c):
        rows = _rows(c)
        xall = _silu(_conv3_chunk(sq.xbc, c, nc, cw, cb))
        sq.xs[rows, :] = xall
        xs = xall[:, 0:D_SSM]
        sq.xk[c] = _stack_heads(xs, hm).astype(BF16)
        sq.xt[c] = xs.T.astype(BF16)
        dt = _softplus(sq.dt_in[rows, :] + dtb)
        la = _split_cat(dt * neg_a, 3, axis=0)
        tri = jnp.concatenate([_tri(True), _tri(False)], axis=0)
        tri = jnp.where(tri, 1.0, 0.0).astype(BF16)
        cs = jnp.dot(jnp.concatenate([tri] * 3, axis=1), la, preferred_element_type=F32)
        lane = lax.broadcasted_iota(jnp.int32, (1, LANES), 1)
        cum = jnp.where(lane % (2 * nh) < nh, cs[0:q], cs[q:2 * q])
        sq.dt[rows, :] = dt
        sq.cum[rows, :] = cum
        sq.dtt[c] = dt.T
        sq.cumt[c] = cum.T

    def prep(c, carry):
        for sq in seqs:
            prep_seq(sq, c)
        return carry

    lax.fori_loop(0, nc, prep, 0)

    for sq in seqs:
        for d in range(2):
            for h in range(nh):
                blk = slice(h * p, (h + 1) * p)
                if has_s0:
                    parts = [jnp.zeros((p, n), F32)] * nh
                    parts[h] = sq.s0[d, h].T
                    sq.st[d, blk, :] = jnp.concatenate(parts, axis=1)
                else:
                    sq.st[d, blk, :] = jnp.zeros((p, nh * n), F32)

    def pack_scalars(sq, c, d):
        rows = _rows(c)
        dt = sq.dt[rows, :]
        cum = sq.cum[rows, :]
        edge = q - 1 if d == 0 else 0
        lane = lax.broadcasted_iota(jnp.int32, (1, LANES), 1)
        used = (lane < DT_REP * 2 * nh) & ((lane % (2 * nh)) // nh == d)
        cum = jnp.where(used, cum, 0.0)
        wgt = jnp.exp(cum[edge:edge + 1, :] - cum) * dt
        grp = lane // (2 * nh)
        packed = jnp.zeros_like(cum)
        for src, pieces, g0 in ((cum, CUM_PIECES, 0), (wgt, WGT_PIECES, CUM_PIECES)):
            rest = src
            for i in range(pieces):
                piece = rest.astype(BF16).astype(F32)
                packed = jnp.where(grp == g0 + i, piece, packed)
                rest = rest - piece
        return jnp.where(used, packed, 0.0).astype(BF16)

    def operands(sq, c, d):
        rows = _rows(c)
        bm = [sq.xs[rows, D_SSM + g * n:D_SSM + (g + 1) * n] for g in range(SSM_GROUPS)]
        cm = [sq.xs[rows, D_SSM + (SSM_GROUPS + g) * n:D_SSM + (SSM_GROUPS + g + 1) * n]
              for g in range(SSM_GROUPS)]
        return bm, cm

    def decay_blocks(sq, c, d, e, cb_t):
        mask = _tri(d == 0)
        blocks, ecol = [], []
        for h in range(nh):
            r = d * nh + h
            col = e[:, h * n:(h + 1) * n]
            decay = jnp.exp(jnp.where(mask, col - sq.cumt[c, r:r + 1, :], -jnp.inf))
            blocks.append((cb_t[h // rep] * decay * sq.dtt[c, r:r + 1, :]).astype(BF16))
            ecol.append(jnp.exp(col))
        lo = lax.broadcasted_iota(jnp.int32, (1, n), 1) < p
        ecol = jnp.concatenate([jnp.where(lo, ecol[2 * j], ecol[2 * j + 1]) for j in range(nh // 2)], axis=1)
        return jnp.concatenate(blocks, axis=1), ecol

    def scan(i, carry):
        j = nc - 1 - i
        jobs = [(sq, c, d) for sq in seqs for c, d in ((i, 0), (j, 1))]
        packed = jnp.concatenate([pack_scalars(*job) for job in jobs], axis=0)
        e_all = jnp.dot(packed, sel_scr[...], preferred_element_type=F32)
        es = [e_all[t * q:(t + 1) * q] for t in range(len(jobs))]
        ops = [operands(*job) for job in jobs]
        cb_t = [[_bdot_nt(cm[g], bm[g]) for g in range(SSM_GROUPS)] for bm, cm in ops]
        dec = [decay_blocks(*job, es[t], cb_t[t]) for t, job in enumerate(jobs)]
        intra = [jnp.dot(dec[t][0], sq.xk[c], preferred_element_type=F32) for t, (sq, c, d) in enumerate(jobs)]
        s_old = [sq.st[d] for sq, c, d in jobs]
        inter = [_bdot_nt(jnp.concatenate([cm[h // rep] for h in range(nh)], axis=1), s_old[t])
                 for t, (bm, cm) in enumerate(ops)]
        wk = [(jnp.concatenate([bm[h // rep] for h in range(nh)], axis=1) * es[t][:, nh * n:2 * nh * n]).astype(BF16)
              for t, (bm, cm) in enumerate(ops)]
        upd = [jnp.dot(sq.xt[c], wk[t], preferred_element_type=F32) for t, (sq, c, d) in enumerate(jobs)]
        for t, (sq, c, d) in enumerate(jobs):
            edge = q - 1 if d == 0 else 0
            sq.st[d] = jnp.exp(es[t][edge:edge + 1, 0:nh * n]) * s_old[t] + upd[t] * msk_scr[...]
            y = intra[t] + dec[t][1] * inter[t]
            if d == 0:
                sq.yf[_rows(c), :] = y
            else:
                sq.yb[_rows(c), :] = y
        return carry

    lax.fori_loop(0, nc, scan, 0)

    for sq in seqs:
        for d in range(2):
            for h in range(nh):
                sq.s[d, h] = sq.st[d, h * p:(h + 1) * p, h * n:(h + 1) * n].T

    dsk = dsk_ref[...]
    nw = nw_ref[...]

    def fin(c, carry):
        rows = _rows(c)
        for sq in seqs:
            y = sq.yf[rows, :] + sq.yb[rows, :] + dsk * sq.xs[rows, 0:D_SSM]
            sq.y[rows, :] = _rms(y * _silu(sq.z[rows, :].astype(F32)), nw)
        return carry

    lax.fori_loop(0, nc, fin, 0)


def _ssd(z, xbc, dt, wts, li, s0, nb, l, stacked=False, prev=None):
    nc = l // CHUNK
    has_s0 = s0 is not None
    names = ("ssd_conv_w", "ssd_conv_b", "ssd_dt_bias", "ssd_a_log", "ssd_d", "ssd_norm_w")
    ng = SEQ_GROUP
    args = [z.reshape(nb, l, D_SSM), xbc.reshape(nb, l, SSM_CONV_CH), dt.reshape(nb, l, DT_PAD)]
    args += [wts[n] for n in names]
    in_specs = [_seq_spec(l, D_SSM, ng), _seq_spec(l, SSM_CONV_CH, ng), _seq_spec(l, DT_PAD, ng)]
    in_specs += [_layer_const(wts[n], li) for n in names]
    if has_s0:
        args.append(s0)
        in_specs.append(pl.BlockSpec((ng, None, 2, SSM_HEADS, SSM_STATE, SSM_HEAD_DIM),
                                     lambda b: (b, li, 0, 0, 0, 0)))
    st_spec, st_shape = _slot_out((2, SSM_HEADS, SSM_STATE, SSM_HEAD_DIM), nb, li, stacked, ng)
    aliases = _carry_prev(args, in_specs, [] if prev is None else [prev], 1)
    y, s = pl.pallas_call(
        functools.partial(_ssd_kernel, nc, ng, has_s0),
        grid=(nb // ng,),
        in_specs=in_specs,
        out_specs=[_seq_spec(l, D_SSM, ng), st_spec],
        out_shape=[jax.ShapeDtypeStruct((nb, l, D_SSM), F32), st_shape],
        input_output_aliases=aliases,
        scratch_shapes=[pltpu.VMEM((ng, l, SSM_CONV_CH), F32),
                        pltpu.VMEM((ng, nc, SSM_HEADS * CHUNK, D_SSM), BF16),
                        pltpu.VMEM((ng, nc, D_SSM, CHUNK), BF16),
                        pltpu.VMEM((ng, l, DT_PAD), F32), pltpu.VMEM((ng, l, DT_PAD), F32),
                        pltpu.VMEM((ng, nc, DT_PAD, CHUNK), F32), pltpu.VMEM((ng, nc, DT_PAD, CHUNK), F32),
                        pltpu.VMEM((ng, l, D_SSM), F32), pltpu.VMEM((ng, l, D_SSM), F32),
                        pltpu.VMEM((ng, 2, D_SSM, SSM_HEADS * SSM_STATE), F32),
                        pltpu.VMEM((D_SSM, SSM_HEADS * SSM_STATE), F32),
                        pltpu.VMEM((LANES, SSD_SEL_W), BF16)],
        compiler_params=_cparams("arbitrary"),
        name="ssd",
    )(*args)
    return y.reshape(nb * l, D_SSM), s


def _ret_kernel(nc, ng, has_s0, *refs):
    refs = list(refs)
    u_ref, dl_ref, gn_ref = refs[:3]
    s0_ref = refs[3] if has_s0 else None
    y_ref, s_ref, yf_scr, yb_scr, st_scr, dm_scr, e_scr, ea_scr, bd_scr = refs[-9:]
    q = CHUNK
    hd = RET_HEAD_DIM
    hm = _head_masks(D_RET, RET_HEADS)

    @pl.when(pl.program_id(0) == 0)
    def _():
        log_g = -_softplus(-dl_ref[...])
        ii = lax.broadcasted_iota(jnp.int32, (q, q), 0)
        jj = lax.broadcasted_iota(jnp.int32, (q, q), 1)
        dij = (ii - jj).astype(F32)
        ri = lax.broadcasted_iota(jnp.int32, (q, 1), 0).astype(F32)
        lfs, lbs = [], []
        for h in range(RET_HEADS):
            lf = log_g[:, h:h + 1]
            lb = log_g[:, RET_HEADS + h:RET_HEADS + h + 1]
            lfs.append(lf)
            lbs.append(lb)
            d_f = jnp.exp(jnp.where(dij >= 0, dij * lf, -jnp.inf))
            d_b = jnp.exp(jnp.where(dij <= 0, -dij * lb, -jnp.inf))
            dm_scr[:, h * q:(h + 1) * q] = d_f + d_b
        lf_l = _by_head(hm, lfs)
        lb_l = _by_head(hm, lbs)
        e_scr[0] = jnp.exp((ri + 1.0) * lf_l)
        e_scr[1] = jnp.exp((q - ri) * lb_l)
        e_scr[2] = jnp.exp((q - 1.0 - ri) * lf_l)
        e_scr[3] = jnp.exp(ri * lb_l)
        ea_scr[0:1, :] = jnp.exp(q * lf_l)
        ea_scr[1:2, :] = jnp.exp(q * lb_l)
        bd_scr[...] = _block_diag(D_RET, hd, 1.0)

    class Seq:
        def __init__(self, g):
            self.u, self.y, self.s = u_ref.at[g], y_ref.at[g], s_ref.at[g]
            self.s0 = s0_ref.at[g] if has_s0 else None
            self.yf, self.yb, self.st = yf_scr.at[g], yb_scr.at[g], st_scr.at[g]

    seqs = [Seq(g) for g in range(ng)]

    for sq in seqs:
        for d in range(2):
            for h in range(RET_HEADS):
                blk = slice(h * hd, (h + 1) * hd)
                if has_s0:
                    parts = [jnp.zeros((hd, hd), F32)] * RET_HEADS
                    parts[h] = sq.s0[d, h]
                    sq.st[d, blk, :] = jnp.concatenate(parts, axis=1)
                else:
                    sq.st[d, blk, :] = jnp.zeros((hd, D_RET), F32)

    def qkv(sq, rows):
        return (sq.u[rows, 0:D_RET], sq.u[rows, D_RET:2 * D_RET].astype(F32) * (RET_HEAD_DIM ** -0.5),
                sq.u[rows, 2 * D_RET:3 * D_RET])

    def state_step(sq, d, qq, kk, vv):
        s_old = sq.st[d]
        y = e_scr[d] * _bdot(qq, s_old)
        sq.st[d] = ea_scr[d:d + 1, :] * s_old + _bdot_tn(kk * e_scr[2 + d], vv) * bd_scr[...]
        return y

    def scan(i, carry):
        rows = _rows(i)
        rows_b = _rows(nc - 1 - i)
        fwd = [qkv(sq, rows) for sq in seqs]
        bwd = [qkv(sq, rows_b) for sq in seqs]
        sc = [_bdot_nt(qq, _stack_heads(kk, hm)) * dm_scr[...] for qq, kk, _ in fwd]
        intra = [_bdot(sc[g], _stack_heads(fwd[g][2], hm)) for g in range(ng)]
        inter = [state_step(sq, 0, *fwd[g]) for g, sq in enumerate(seqs)]
        back = [state_step(sq, 1, *bwd[g]) for g, sq in enumerate(seqs)]
        for g, sq in enumerate(seqs):
            sq.yf[rows, :] = intra[g] + inter[g]
            sq.yb[rows_b, :] = back[g]
        return carry

    lax.fori_loop(0, nc, scan, 0)

    for sq in seqs:
        for d in range(2):
            for h in range(RET_HEADS):
                sq.s[d, h] = sq.st[d, h * hd:(h + 1) * hd, h * hd:(h + 1) * hd]

    gn = gn_ref[...]

    def fin(c, carry):
        rows = _rows(c)
        for sq in seqs:
            y = sq.yf[rows, :] + sq.yb[rows, :]
            cen = y - _seg_mean(y, hd)
            var = _seg_mean(cen * cen, hd)
            sq.y[rows, :] = cen * lax.rsqrt(var + NORM_EPS) * gn * _silu(sq.u[rows, 3 * D_RET:4 * D_RET].astype(F32))
        return carry

    lax.fori_loop(0, nc, fin, 0)


def _retention(u, wts, li, s0, nb, l, stacked=False, prev=None):
    nc = l // CHUNK
    has_s0 = s0 is not None
    names = ("ret_decay_logit", "ret_gn_w")
    ng = SEQ_GROUP
    args = [u.reshape(nb, l, RET_IN)] + [wts[n] for n in names]
    in_specs = [_seq_spec(l, RET_IN, ng)] + [_layer_const(wts[n], li) for n in names]
    if has_s0:
        args.append(s0)
        in_specs.append(pl.BlockSpec((ng, None, 2, RET_HEADS, RET_HEAD_DIM, RET_HEAD_DIM),
                                     lambda b: (b, li, 0, 0, 0, 0)))
    st_spec, st_shape = _slot_out((2, RET_HEADS, RET_HEAD_DIM, RET_HEAD_DIM), nb, li, stacked, ng)
    aliases = _carry_prev(args, in_specs, [] if prev is None else [prev], 1)
    y, s = pl.pallas_call(
        functools.partial(_ret_kernel, nc, ng, has_s0),
        grid=(nb // ng,),
        in_specs=in_specs,
        out_specs=[_seq_spec(l, D_RET, ng), st_spec],
        out_shape=[jax.ShapeDtypeStruct((nb, l, D_RET), F32), st_shape],
        input_output_aliases=aliases,
        scratch_shapes=[pltpu.VMEM((ng, l, D_RET), F32), pltpu.VMEM((ng, l, D_RET), F32),
                        pltpu.VMEM((ng, 2, D_RET, D_RET), F32), pltpu.VMEM((CHUNK, RET_HEADS * CHUNK), F32),
                        pltpu.VMEM((4, CHUNK, D_RET), F32), pltpu.VMEM((8, D_RET), F32),
                        pltpu.VMEM((D_RET, D_RET), F32)],
        compiler_params=_cparams("arbitrary"),
        name="retention",
    )(*args)
    return y.reshape(nb * l, D_RET), s


def _split(x):
    hi = x.astype(BF16)
    return hi, (x - hi.astype(F32)).astype(BF16)


def _dot3(a_hi, a_lo, b_hi, b_lo):
    d = lambda p, q: jnp.dot(p, q, preferred_element_type=F32)
    return d(a_hi, b_hi) + (d(a_lo, b_hi) + d(a_hi, b_lo))


@functools.lru_cache(maxsize=None)
def _dft_fwd_host(l):
    n = 2 * l
    f = np.arange(l, dtype=np.int64)[:, None]
    s = np.arange(l, dtype=np.int64)[None, :]
    ang = ((f * s) % n).astype(np.float64) * (2.0 * math.pi / n)
    im = -np.sin(ang)
    im[0] = np.where(np.arange(l) % 2 == 0, 1.0, -1.0)
    return np.concatenate([np.cos(ang), im], axis=0).astype(np.float32)


def _dft_tables(l):
    n = 2 * l
    fwd = jnp.asarray(_dft_fwd_host(l))
    wgt = np.full((n, 1), 2.0 / n, np.float32)
    wgt[0] = wgt[l] = 1.0 / n
    return _split(fwd) + _split((fwd * wgt).T)


def _dft_block_tables(tables, l):
    fwd_hi, fwd_lo, inv_hi, inv_lo = tables
    r = DFT_BLOCK
    nblk = l // r
    fb = lambda t: (t[0:l].reshape(nblk, r, l), t[l:2 * l].reshape(nblk, r, l))
    fwd_blk = jnp.concatenate(fb(fwd_hi) + fb(fwd_lo), axis=1)
    inv_blk = jnp.concatenate([inv_hi.reshape(nblk, r, 2 * l), inv_lo.reshape(nblk, r, 2 * l)], axis=1)
    return fwd_blk, inv_blk


def _hy_filter_kernel(l, feats_ref, dec_ref, w1_ref, b1_ref, w2_ref, b2_ref, w3_ref, fr_ref, fh_ref, fl_ref,
                      a_ref, b_ref, d_ref):
    fr = fr_ref[...]
    xdot = lambda a, b: _dot3(*_split(a), *_split(b))
    h = jnp.sin(fr * (xdot(feats_ref[...], w1_ref[...]) + b1_ref[...]))
    h = jnp.sin(fr * (xdot(h, w2_ref[...]) + b2_ref[...]))
    h = xdot(h, w3_ref[...])
    dec = jnp.concatenate([dec_ref[...]] * HY_ORDER, axis=-1)
    row0 = lax.broadcasted_iota(jnp.int32, (l, 1), 0) == 0
    hf = h[:, 0:HY_ORDER * D_HY] * dec
    hb = h[:, HY_ORDER * D_HY:2 * HY_ORDER * D_HY] * dec
    hb = jnp.where(row0, 0.0, hb)
    hs = _split(hf + hb)
    hd = _split(hf - hb)
    re = _dot3(fh_ref[0:l, :], fl_ref[0:l, :], *hs)
    ny = _dot3(fh_ref[l:l + 8, :], fl_ref[l:l + 8, :], *hs)[0:1]
    im = _dot3(fh_ref[l:2 * l, :], fl_ref[l:2 * l, :], *hd)
    for o in range(HY_ORDER):
        cols = slice(o * D_HY, (o + 1) * D_HY)
        a_ref[o] = re[:, cols]
        b_ref[o] = jnp.where(row0, 0.0, im[:, cols])
        d_ref[o] = jnp.where(row0, ny[:, cols], re[:, cols])


def _hy_filter(l, fwd_hi, fwd_lo, wts, li):
    pos = np.arange(l, dtype=np.float32)
    t = pos / np.float32(l - 1)
    bands = np.linspace(1e-4, HY_BANDS - 1, HY_BANDS, dtype=np.float32)
    ang = np.float32(2.0 * math.pi / l) * pos[:, None] * bands[None, :]
    feats = np.concatenate([t[:, None], np.cos(ang), -np.sin(ang)], axis=-1).astype(np.float32)
    feats = np.pad(feats, ((0, 0), (0, LANES - HY_EMB)))
    max_decay = math.log(HY_TARGET) / HY_FAST_DECAY
    min_decay = math.log(HY_TARGET) / HY_SLOW_DECAY
    deltas = np.abs(np.linspace(min_decay, max_decay, D_HY, dtype=np.float32))
    dec = np.exp(-t[:, None] * deltas[None, :]).astype(np.float32)
    spec = jax.ShapeDtypeStruct((HY_ORDER, l, D_HY), F32)
    names = ("hy_w1", "hy_b1", "hy_w2", "hy_b2", "hy_w3", "hy_freq")
    return pl.pallas_call(
        functools.partial(_hy_filter_kernel, l),
        grid=(1,),
        in_specs=[_full(feats.shape), _full(dec.shape)] + [_layer_const(wts[n], li) for n in names]
                 + [_full(fwd_hi.shape), _full(fwd_lo.shape)],
        out_specs=[_full(spec.shape)] * 3,
        out_shape=[spec] * 3,
        compiler_params=_cparams("arbitrary"),
        name="hyena_filter",
    )(jnp.asarray(feats), jnp.asarray(dec), *[wts[n] for n in names], fwd_hi, fwd_lo)


def _hy_kernel(nc, ng, u_ref, cw_ref, cb_ref, f_ref, g_ref, a_ref, b_ref, d_ref, bias_ref, y_ref,
               uc_scr, vh_scr, vl_scr, sh_scr, sl_scr, z_scr):
    l = nc * CHUNK
    r = DFT_BLOCK
    nblk = l // r
    cw = cw_ref[...]
    cb = cb_ref[...]
    dot = lambda p, q: jnp.dot(p, q, preferred_element_type=F32)
    wide = lambda x: jnp.concatenate([x] * ng, axis=1)
    seq_cols = lambda g: slice(g * D_HY, (g + 1) * D_HY)

    def for_blocks(body):
        if nblk == 1:
            body(0)
        else:
            lax.fori_loop(0, nblk, lambda i, carry: (body(i), carry)[1], 0)

    def conv(c, carry):
        rows = _rows(c)
        for g in range(ng):
            uc = _conv3_chunk(u_ref.at[g], c, nc, cw, cb)
            uc_scr[g, rows, :] = uc
            vh_scr[rows, seq_cols(g)], vl_scr[rows, seq_cols(g)] = _split(uc[:, 0:D_HY])
        return carry

    lax.fori_loop(0, nc, conv, 0)

    def long_conv(o):
        def spectrum(i):
            rows = pl.ds(pl.multiple_of(i * r, r), r)
            rows_im = pl.ds(pl.multiple_of(l + i * r, r), r)
            p = dot(f_ref[i], vh_scr[...])
            pl_ = dot(f_ref[i, 0:2 * r, :], vl_scr[...])
            zr = p[0:r] + (p[2 * r:3 * r] + pl_[0:r])
            zi = p[r:2 * r] + (p[3 * r:4 * r] + pl_[r:2 * r])
            fa, fb, fd = wide(a_ref[o, rows, :]), wide(b_ref[o, rows, :]), wide(d_ref[o, rows, :])
            sh_scr[rows, :], sl_scr[rows, :] = _split(zr * fa - zi * fb)
            sh_scr[rows_im, :], sl_scr[rows_im, :] = _split(zr * fb + zi * fd)

        for_blocks(spectrum)

        def inverse(i):
            rows = pl.ds(pl.multiple_of(i * r, r), r)
            p = dot(g_ref[i], sh_scr[...])
            y = p[0:r] + (p[r:2 * r] + dot(g_ref[i, 0:r, :], sl_scr[...]))
            gate = jnp.concatenate([uc_scr[g, rows, (o + 1) * D_HY:(o + 2) * D_HY] for g in range(ng)], axis=1)
            if o == 0:
                v = jnp.concatenate([uc_scr[g, rows, 0:D_HY] for g in range(ng)], axis=1)
            else:
                v = z_scr[rows, :]
            out = gate * (y + v * wide(bias_ref[o:o + 1, :]))
            if o + 1 < HY_ORDER:
                z_scr[rows, :] = out
                vh_scr[rows, :], vl_scr[rows, :] = _split(out)
            else:
                for g in range(ng):
                    y_ref[g, rows, :] = out[:, seq_cols(g)]

        for_blocks(inverse)

    for o in range(HY_ORDER):
        long_conv(o)


def _hyena(u, wts, li, tables, spectra, nb, l):
    nc = l // CHUNK
    ng = HY_GROUP
    a, b, d = spectra
    names = ("hy_conv_w", "hy_conv_b")
    y = pl.pallas_call(
        functools.partial(_hy_kernel, nc, ng),
        grid=(nb // ng,),
        in_specs=[pl.BlockSpec((ng, l, HY_IN), lambda i: (i, 0, 0))] + [_layer_const(wts[n], li) for n in names]
                 + [_const(tables[0].shape), _const(tables[1].shape),
                    _const((HY_ORDER, l, D_HY)), _const((HY_ORDER, l, D_HY)), _const((HY_ORDER, l, D_HY)),
                    _layer_const(wts["hy_bias"], li)],
        out_specs=pl.BlockSpec((ng, l, D_HY), lambda i: (i, 0, 0)),
        out_shape=jax.ShapeDtypeStruct((nb, l, D_HY), F32),
        scratch_shapes=[pltpu.VMEM((ng, l, HY_IN), F32),
                        pltpu.VMEM((l, ng * D_HY), BF16), pltpu.VMEM((l, ng * D_HY), BF16),
                        pltpu.VMEM((2 * l, ng * D_HY), BF16), pltpu.VMEM((2 * l, ng * D_HY), BF16),
                        pltpu.VMEM((l, ng * D_HY), F32)],
        compiler_params=_cparams("parallel"),
        name="hyena",
    )(u.reshape(nb, l, HY_IN), *[wts[n] for n in names], *tables, a, b, d, wts["hy_bias"])
    return y.reshape(nb * l, D_HY)


def _seg_rms(x, w):
    return x * lax.rsqrt(_seg_mean(x * x, HEAD_DIM) + NORM_EPS) * w


def _stack_q(q):
    lo = lax.broadcasted_iota(jnp.int32, (1, D_KV), 1) < HEAD_DIM
    qa = q[:, 0:D_KV]
    qb = q[:, D_KV:2 * D_KV]
    return jnp.concatenate([jnp.where(lo, qa, 0.0), jnp.where(lo, pltpu.roll(qa, HEAD_DIM, 1), 0.0),
                            jnp.where(lo, 0.0, pltpu.roll(qb, HEAD_DIM, 1)), jnp.where(lo, 0.0, qb)], axis=0)


def _unstack_o(o):
    r = o.shape[0] // ATT_HEADS
    lo = lax.broadcasted_iota(jnp.int32, (1, D_KV), 1) < HEAD_DIM
    ya = jnp.where(lo, o[0:r], pltpu.roll(o[r:2 * r], HEAD_DIM, 1))
    yb = jnp.where(lo, pltpu.roll(o[2 * r:3 * r], HEAD_DIM, 1), o[3 * r:4 * r])
    return jnp.concatenate([ya, yb], axis=1)


def _sink_col(sink_ref, li, r):
    rb = lax.broadcasted_iota(jnp.int32, (ATT_HEADS * r, 1), 0) // r
    col = jnp.full((ATT_HEADS * r, 1), sink_ref[li * ATT_HEADS + ATT_HEADS - 1], F32)
    for h in range(ATT_HEADS - 2, -1, -1):
        col = jnp.where(rb == h, sink_ref[li * ATT_HEADS + h], col)
    return col


def _ctx_attn_kernel(li, ng, u_ref, qn_ref, kn_ref, sink_ref, *rest):
    y_ref, k_ref, v_ref = rest[-3:]
    l = u_ref.shape[1]
    sink = _sink_col(sink_ref, li, l)
    seqs = range(ng)
    u = [u_ref[g].astype(F32) for g in seqs]
    q = [_seg_rms(u[g][:, 0:D_ATT], qn_ref[...]) for g in seqs]
    k = [_seg_rms(u[g][:, D_ATT:D_ATT + D_KV], kn_ref[:, 0:D_KV]) for g in seqs]
    v = [u[g][:, D_ATT + D_KV:D_ATT + 2 * D_KV] for g in seqs]
    for g in seqs:
        k_ref[g] = k[g]
        v_ref[g] = v[g]
    s = [_bdot_nt(_stack_q(q[g]), k[g]) * (HEAD_DIM ** -0.5) for g in seqs]
    m = [jnp.maximum(jnp.max(s[g], axis=-1, keepdims=True), sink) for g in seqs]
    p = [jnp.exp(s[g] - m[g]) for g in seqs]
    den = [jnp.sum(p[g], axis=-1, keepdims=True) + jnp.exp(sink - m[g]) for g in seqs]
    o = [_bdot(p[g] * (1.0 / den[g]), v[g]) for g in seqs]
    for g in seqs:
        y_ref[g] = _unstack_o(o[g])


def _ctx_attention(u, wts, li, nb, l, prev=None):
    ng = ATT_GROUP
    kv_spec, kv_shape = _slot_out((l, D_KV), nb, li, True, ng)
    args = [u.reshape(nb, l, ATT_IN), wts["attn_q_norm"], wts["attn_k_norm"], wts["attn_sink"]]
    in_specs = [_seq_spec(l, ATT_IN, ng), _layer_const(wts["attn_q_norm"], li),
                _layer_const(wts["attn_k_norm"], li), pl.BlockSpec(memory_space=pltpu.SMEM)]
    aliases = _carry_prev(args, in_specs, [] if prev is None else list(prev), 1)
    y, k, v = pl.pallas_call(
        functools.partial(_ctx_attn_kernel, li, ng),
        grid=(nb // ng,),
        in_specs=in_specs,
        out_specs=[_seq_spec(l, D_ATT, ng), kv_spec, kv_spec],
        out_shape=[jax.ShapeDtypeStruct((nb, l, D_ATT), F32), kv_shape, kv_shape],
        input_output_aliases=aliases,
        compiler_params=_cparams("parallel"),
        name="ctx_attention",
    )(*args)
    return y.reshape(nb * l, D_ATT), k, v


@functools.lru_cache(maxsize=None)
def _rope_tables_host(l):
    n_rows = l // GRID_W
    rows = np.repeat(np.arange(n_rows, dtype=np.float32), GRID_W)
    cols = np.tile(np.arange(GRID_W, dtype=np.float32), n_rows)
    nf = HEAD_DIM // 4
    inv = (np.float32(ROPE_BASE) ** (-np.arange(nf, dtype=np.float32) / np.float32(nf))).astype(np.float32)
    ar = rows[:, None] * inv[None, :]
    ac = cols[:, None] * inv[None, :]
    cos = np.concatenate([np.cos(ar), np.cos(ar), np.cos(ac), np.cos(ac)], axis=-1)
    sin = np.concatenate([-np.sin(ar), np.sin(ar), -np.sin(ac), np.sin(ac)], axis=-1)
    return (np.tile(cos, (1, ATT_HEADS)).astype(np.float32), np.tile(sin, (1, ATT_HEADS)).astype(np.float32))


def _rope(x, cos, sin):
    w = x.shape[-1]
    nf = HEAD_DIM // 4
    lane = lax.broadcasted_iota(jnp.int32, x.shape, 1)
    first = (lane % (2 * nf)) < nf
    partner = jnp.where(first, pltpu.roll(x, w - nf, 1), pltpu.roll(x, nf, 1))
    return x * cos + partner * sin


def _lat_attn_kernel(nblk, li, ng, u_ref, qn_ref, kn_ref, cos_ref, sin_ref, ck_ref, cv_ref, sink_ref,
                     y_ref, q_scr, k_scr, v_scr):
    blk = ATT_BLOCK
    l = nblk * blk
    seqs = range(ng)
    cos_q = cos_ref[...]
    sin_q = sin_ref[...]
    zeros = jnp.zeros((blk, D_KV), F32)
    for g in seqs:
        u = u_ref[g].astype(F32)
        q = _seg_rms(u[:, 0:D_ATT], qn_ref[...])
        k = _seg_rms(u[:, D_ATT:D_ATT + D_KV], kn_ref[:, 0:D_KV])
        q_scr[g] = _rope(q, cos_q, sin_q)
        k_scr[g, 0:blk, :] = zeros
        k_scr[g, blk + l:2 * blk + l, :] = zeros
        v_scr[g, 0:blk, :] = zeros
        v_scr[g, blk + l:2 * blk + l, :] = zeros
        k_scr[g, blk:blk + l, :] = _rope(k, cos_q[:, 0:D_KV], sin_q[:, 0:D_KV])
        v_scr[g, blk:blk + l, :] = u[:, D_ATT + D_KV:D_ATT + 2 * D_KV]

    scale = HEAD_DIM ** -0.5
    r = lax.broadcasted_iota(jnp.int32, (blk, 3 * blk), 0)
    cidx = lax.broadcasted_iota(jnp.int32, (blk, 3 * blk), 1)
    band = (cidx - r >= blk - WINDOW) & (cidx - r <= blk + WINDOW)
    sink = _sink_col(sink_ref, li, blk)

    def block(i, carry):
        rows = _rows(i)
        win = pl.ds(pl.multiple_of(i * blk, blk), 3 * blk)
        kpos = cidx + (i - 1) * blk
        valid = band & (kpos >= 0) & (kpos < l)
        valid = jnp.concatenate([valid] * ATT_HEADS, axis=0)
        qs = [_stack_q(q_scr[g, rows, :]) for g in seqs]
        s_loc = [jnp.where(valid, _bdot_nt(qs[g], k_scr[g, win, :]) * scale, -jnp.inf) for g in seqs]
        s_ctx = [_bdot_nt(qs[g], ck_ref[g]) * scale for g in seqs]
        m = [jnp.maximum(jnp.maximum(jnp.max(s_loc[g], axis=-1, keepdims=True),
                                     jnp.max(s_ctx[g], axis=-1, keepdims=True)), sink) for g in seqs]
        p_loc = [jnp.exp(s_loc[g] - m[g]) for g in seqs]
        p_ctx = [jnp.exp(s_ctx[g] - m[g]) for g in seqs]
        inv = [1.0 / (jnp.sum(p_loc[g], axis=-1, keepdims=True) + jnp.sum(p_ctx[g], axis=-1, keepdims=True)
                      + jnp.exp(sink - m[g])) for g in seqs]
        o = [_bdot(p_ctx[g] * inv[g], cv_ref[g]) + _bdot(p_loc[g] * inv[g], v_scr[g, win, :]) for g in seqs]
        for g in seqs:
            y_ref[g, rows, :] = _unstack_o(o[g])
        return carry

    lax.fori_loop(0, nblk, block, 0)


def _lat_attention(u, wts, li, ck, cv, nb, l):
    lc = ck.shape[2]
    nblk = l // ATT_BLOCK
    ng = ATT_GROUP
    cos, sin = _rope_tables_host(l)
    cache_spec = pl.BlockSpec((ng, None, lc, D_KV), lambda b: (b, li, 0, 0))
    y = pl.pallas_call(
        functools.partial(_lat_attn_kernel, nblk, li, ng),
        grid=(nb // ng,),
        in_specs=[_seq_spec(l, ATT_IN, ng), _layer_const(wts["attn_q_norm"], li),
                  _layer_const(wts["attn_k_norm"], li), _const((l, D_ATT)), _const((l, D_ATT)),
                  cache_spec, cache_spec, pl.BlockSpec(memory_space=pltpu.SMEM)],
        out_specs=_seq_spec(l, D_ATT, ng),
        out_shape=jax.ShapeDtypeStruct((nb, l, D_ATT), F32),
        scratch_shapes=[pltpu.VMEM((ng, l, D_ATT), F32), pltpu.VMEM((ng, l + 2 * ATT_BLOCK, D_KV), F32),
                        pltpu.VMEM((ng, l + 2 * ATT_BLOCK, D_KV), F32)],
        compiler_params=_cparams("parallel"),
        name="lat_attention",
    )(u.reshape(nb, l, ATT_IN), wts["attn_q_norm"], wts["attn_k_norm"], jnp.asarray(cos), jnp.asarray(sin),
      ck, cv, wts["attn_sink"])
    return y.reshape(nb * l, D_ATT)


def _dt_weight(w):
    zeros = jnp.zeros(w.shape[:-1] + (DT_PAD - DT_REP * N_DT,), w.dtype)
    return jnp.concatenate([w[..., C_DT:C_DT + N_DT]] * DT_REP + [zeros], axis=-1).astype(BF16)


def _prep_weights(p):
    row = lambda a: a.reshape(DEPTH, 1, -1)
    pad_lanes = lambda a: jnp.pad(row(a), ((0, 0), (0, 0), (0, LANES - a[0].size)))
    return dict(
        norm_w=p["norm_w"].reshape(DEPTH * 3, 1, D_MODEL),
        ffn_w_in=p["ffn_w_in"].astype(BF16).reshape(DEPTH * 2, D_MODEL, 2 * D_FF),
        ffn_w_out=p["ffn_w_out"].astype(BF16).reshape(DEPTH * 2, D_FF, D_MODEL),
        mix_w_in=p["mix_w_in"].astype(BF16), mix_w_dt=_dt_weight(p["mix_w_in"]),
        mix_w_out=p["mix_w_out"].astype(BF16),
        ssd_conv_w=p["ssd_conv_w"], ssd_conv_b=row(p["ssd_conv_b"]),
        ssd_dt_bias=pad_lanes(jnp.tile(row(p["ssd_dt_bias"]), (1, 1, DT_REP))),
        ssd_a_log=pad_lanes(jnp.tile(row(p["ssd_a_log"]), (1, 1, DT_REP))),
        ssd_d=row(jnp.repeat(p["ssd_d"], SSM_HEAD_DIM, axis=-1)), ssd_norm_w=row(p["ssd_norm_w"]),
        hy_conv_w=p["hy_conv_w"], hy_conv_b=row(p["hy_conv_b"]), hy_bias=p["hy_bias"],
        hy_w1=jnp.pad(p["hy_w1"], ((0, 0), (0, LANES - HY_EMB), (0, 0))), hy_b1=row(p["hy_b1"]),
        hy_w2=p["hy_w2"], hy_b2=row(p["hy_b2"]), hy_w3=p["hy_w3"], hy_freq=row(p["hy_freq"]),
        ret_decay_logit=pad_lanes(p["ret_decay_logit"]), ret_gn_w=row(p["ret_gn_w"]),
        attn_q_norm=row(jnp.tile(p["attn_q_norm"], (1, ATT_HEADS))),
        attn_k_norm=row(jnp.tile(p["attn_k_norm"], (1, ATT_HEADS))),
        attn_sink=p["attn_sink"].reshape(DEPTH * ATT_HEADS),
    )


def _layer(x, mod, row0, wts, li, nb, l, ssd_s0, ret_s0, ctx_kv, hy_tables, hy_spectra, carried):
    ctx = ctx_kv is None
    rpm = nb * l if ctx else l
    x = _ffn(x, mod, row0, rpm, wts, li, 0)
    z, xbc, hy, ret, att, dt = _inproj(x, mod, row0, rpm, wts, li)
    y_ssd, s_ssd = _ssd(z, xbc, dt, wts, li, ssd_s0, nb, l, stacked=ctx, prev=carried.get("ssd"))
    y_hy = _hyena(hy, wts, li, hy_tables, hy_spectra, nb, l)
    y_ret, s_ret = _retention(ret, wts, li, ret_s0, nb, l, stacked=ctx, prev=carried.get("ret"))
    if ctx:
        y_att, k, v = _ctx_attention(att, wts, li, nb, l, prev=carried.get("kv"))
        carried = dict(ssd=s_ssd, ret=s_ret, kv=(k, v))
    else:
        y_att = _lat_attention(att, wts, li, ctx_kv[0], ctx_kv[1], nb, l)
    x = _ffn(x, mod, row0, rpm, wts, li, 1, mix=(y_ssd, y_hy, y_ret, y_att))
    return x, carried


def kernel(x_prompt, x_sample, cache_k, cache_v, state_ssd, state_ret, c, c_ctx, w_mod, b_mod, norm_w, ffn_w_in, ffn_w_out, mix_w_in, mix_w_out, ssd_conv_w, ssd_conv_b, ssd_dt_bias, ssd_a_log, ssd_d, ssd_norm_w, hy_conv_w, hy_conv_b, hy_w1, hy_b1, hy_w2, hy_b2, hy_w3, hy_freq, hy_bias, ret_decay_logit, ret_gn_w, attn_q_norm, attn_k_norm, attn_sink):
    bp, lp_len, _ = x_prompt.shape
    bs, ls_len, _ = x_sample.shape
    lc = cache_k.shape[2]

    cond = jnp.concatenate([c_ctx[None, :], c, jnp.zeros((MOD_ROWS - 1 - bs, D_MODEL), F32)], axis=0)
    mod = _modulation(cond, w_mod, b_mod)
    wts = _prep_weights(dict(
        norm_w=norm_w, ffn_w_in=ffn_w_in, ffn_w_out=ffn_w_out, mix_w_in=mix_w_in, mix_w_out=mix_w_out,
        ssd_conv_w=ssd_conv_w, ssd_conv_b=ssd_conv_b, ssd_dt_bias=ssd_dt_bias, ssd_a_log=ssd_a_log, ssd_d=ssd_d,
        ssd_norm_w=ssd_norm_w, hy_conv_w=hy_conv_w, hy_conv_b=hy_conv_b, hy_w1=hy_w1, hy_b1=hy_b1, hy_w2=hy_w2,
        hy_b2=hy_b2, hy_w3=hy_w3, hy_freq=hy_freq, hy_bias=hy_bias, ret_decay_logit=ret_decay_logit,
        ret_gn_w=ret_gn_w, attn_q_norm=attn_q_norm, attn_k_norm=attn_k_norm, attn_sink=attn_sink))
    ck = cache_k.reshape(bs, DEPTH, lc, D_KV)
    cv = cache_v.reshape(bs, DEPTH, lc, D_KV)

    tab_p = _dft_tables(lp_len)
    tab_s = _dft_tables(ls_len)
    blk_p = _dft_block_tables(tab_p, lp_len)
    blk_s = _dft_block_tables(tab_s, ls_len)

    yp = x_prompt.reshape(bp * lp_len, D_MODEL)
    ys = x_sample.reshape(bs * ls_len, D_MODEL)
    carried = dict(
        ssd=jnp.zeros((bp, DEPTH, 2, SSM_HEADS, SSM_STATE, SSM_HEAD_DIM), F32),
        ret=jnp.zeros((bp, DEPTH, 2, RET_HEADS, RET_HEAD_DIM, RET_HEAD_DIM), F32),
        kv=(jnp.zeros((bp, DEPTH, lp_len, D_KV), F32), jnp.zeros((bp, DEPTH, lp_len, D_KV), F32)))
    for li in range(DEPTH):
        spec_p = _hy_filter(lp_len, tab_p[0], tab_p[1], wts, li)
        spec_s = _hy_filter(ls_len, tab_s[0], tab_s[1], wts, li)
        yp, carried = _layer(yp, mod, li * MOD_ROWS, wts, li, bp, lp_len, None, None, None, blk_p, spec_p, carried)
        ys, _ = _layer(ys, mod, li * MOD_ROWS + 1, wts, li, bs, ls_len, state_ssd, state_ret, (ck, cv),
                       blk_s, spec_s, {})

    kv_shape = (bp, DEPTH, lp_len, ATT_KV_HEADS, HEAD_DIM)
    new_k, new_v = carried["kv"]
    return (yp.reshape(bp, lp_len, D_MODEL), ys.reshape(bs, ls_len, D_MODEL),
            new_k.reshape(kv_shape), new_v.reshape(kv_shape), carried["ssd"], carried["ret"])
```

```python
import functools
import math

import numpy as np
import jax
import jax.numpy as jnp
from jax import lax
from jax.experimental import pallas as pl
from jax.experimental.pallas import tpu as pltpu

F32 = jnp.float32
BF16 = jnp.bfloat16
HI = lax.Precision.HIGHEST

D_MODEL = 1024
DEPTH = 2
GRID_W = 64
D_FF = 2816
N_MOD = 9
NORM_EPS = 1e-6
CHUNK = 128
D_SSM = 256
SSM_HEADS = 4
SSM_HEAD_DIM = 64
SSM_STATE = 128
SSM_GROUPS = 2
SSM_CONV_CH = D_SSM + 2 * SSM_GROUPS * SSM_STATE
D_HY = 256
HY_ORDER = 2
HY_BANDS = 16
HY_EMB = 1 + 2 * HY_BANDS
HY_HIDDEN = 64
HY_FAST_DECAY = 0.3
HY_SLOW_DECAY = 1.5
HY_TARGET = 1e-2
HY_IN = (HY_ORDER + 1) * D_HY
D_RET = 256
RET_HEADS = 4
RET_HEAD_DIM = 64
RET_IN = 4 * D_RET
ATT_HEADS = 4
ATT_KV_HEADS = 2
HEAD_DIM = 64
D_ATT = ATT_HEADS * HEAD_DIM
D_KV = ATT_KV_HEADS * HEAD_DIM
ATT_IN = D_ATT + 2 * D_KV
WINDOW = 128
ATT_BLOCK = 128
ROPE_BASE = 10000.0
D_MIX = D_SSM + D_HY + D_RET + D_ATT

LANES = 128
DT_PAD = LANES
VMEM_LIMIT = 56 * 1024 * 1024
MOD_ROWS = 8

TOKEN_TILE = 512
DFT_BLOCK = 256
HY_GROUP = 4
SEQ_GROUP = 2
ATT_GROUP = 2


def _cparams(*sem):
    return pltpu.CompilerParams(dimension_semantics=sem, vmem_limit_bytes=VMEM_LIMIT)


def _rms(x, w):
    return x * lax.rsqrt(jnp.mean(x * x, axis=-1, keepdims=True) + NORM_EPS) * w


def _silu(x):
    return x * (1.0 / (1.0 + jnp.exp(-x)))


def _softplus(x):
    return jnp.maximum(x, 0.0) + jnp.log1p(jnp.exp(-jnp.abs(x)))


def _bdot(a, b):
    return jnp.dot(a.astype(BF16), b.astype(BF16), preferred_element_type=F32)


def _bdot_nt(a, b):
    return lax.dot_general(a.astype(BF16), b.astype(BF16), (((1,), (1,)), ((), ())),
                           preferred_element_type=F32)


def _bdot_tn(a, b):
    return lax.dot_general(a.astype(BF16), b.astype(BF16), (((0,), (0,)), ((), ())),
                           preferred_element_type=F32)


def _hdot(a, b):
    return jnp.dot(a, b, preferred_element_type=F32, precision=HI)


def _full(shape):
    n = len(shape)
    return pl.BlockSpec(shape, lambda *_: (0,) * n)


def _const(shape):
    n = len(shape)
    return pl.BlockSpec(shape, lambda *_: (0,) * n, pipeline_mode=pl.Buffered(1))


def _layer_const(arr, li):
    tail = arr.shape[1:]
    zeros = (0,) * len(tail)
    return pl.BlockSpec((None,) + tail, lambda *_: (li,) + zeros, pipeline_mode=pl.Buffered(1))


def _mod_spec(k, tm, rows_per_mod, row0):
    return pl.BlockSpec((None, 1, D_MODEL), lambda i: (row0 + (i * tm) // rows_per_mod, 0, k))


def _seq_spec(l, w, ng=None):
    return pl.BlockSpec((ng, l, w), lambda b: (b, 0, 0))


def _head_masks(width, heads):
    lane = lax.broadcasted_iota(jnp.int32, (1, width), 1)
    hd = width // heads
    return [(lane >= h * hd) & (lane < (h + 1) * hd) for h in range(heads)]


def _by_head(masks, vals):
    out = vals[-1]
    for m, v in zip(masks[-2::-1], vals[-2::-1]):
        out = jnp.where(m, v, out)
    return out


def _block_diag(n, blk, value):
    i = lax.broadcasted_iota(jnp.int32, (n, n), 0) // blk
    j = lax.broadcasted_iota(jnp.int32, (n, n), 1) // blk
    return jnp.where(i == j, value, 0.0).astype(F32)


def _mod_kernel(c_ref, w_ref, b_ref, o_ref):
    c = c_ref[...]
    o_ref[...] = _bdot(_silu(c), w_ref[...]) + b_ref[...]


def _modulation(cond, w_mod, b_mod):
    out = pl.pallas_call(
        _mod_kernel,
        grid=(DEPTH, N_MOD),
        in_specs=[pl.BlockSpec((MOD_ROWS, D_MODEL), lambda l, j: (0, 0)),
                  pl.BlockSpec((None, D_MODEL, D_MODEL), lambda l, j: (l, 0, j)),
                  pl.BlockSpec((None, 1, D_MODEL), lambda l, j: (l, 0, j))],
        out_specs=pl.BlockSpec((None, MOD_ROWS, D_MODEL), lambda l, j: (l, 0, j)),
        out_shape=jax.ShapeDtypeStruct((DEPTH, MOD_ROWS, N_MOD * D_MODEL), F32),
        compiler_params=_cparams("arbitrary", "arbitrary"),
        name="modulation",
    )(cond, w_mod, b_mod.reshape(DEPTH, 1, N_MOD * D_MODEL))
    return out.reshape(DEPTH * MOD_ROWS, 1, N_MOD * D_MODEL)


def _ffn_kernel(n_mix, x_ref, sh_ref, sc_ref, g_ref, nw_ref, wi_ref, wo_ref, *rest):
    o_ref = rest[-1]
    x = x_ref[...]
    if n_mix:
        gm_ref, wm_ref = rest[0], rest[1 + n_mix]
        w = D_MIX // n_mix
        acc = _bdot(rest[1][...], wm_ref[0:w, :])
        for j in range(1, n_mix):
            acc += _bdot(rest[1 + j][...], wm_ref[j * w:(j + 1) * w, :])
        x = x + gm_ref[...] * acc
    h = (_rms(x, nw_ref[...]) * (1.0 + sc_ref[...]) + sh_ref[...]).astype(BF16)
    gate = jnp.dot(h, wi_ref[:, 0:D_FF], preferred_element_type=F32)
    up = jnp.dot(h, wi_ref[:, D_FF:2 * D_FF], preferred_element_type=F32)
    o_ref[...] = x + 0.5 * g_ref[...] * _bdot(_silu(gate) * up, wo_ref[...])


def _ffn(x, mod, row0, rows_per_mod, wts, li, k, mix=()):
    t = x.shape[0]
    tm = min(TOKEN_TILE, rows_per_mod)
    w_in, w_out = wts["ffn_w_in"], wts["ffn_w_out"]
    args = [x, mod, mod, mod, wts["norm_w"], w_in, w_out]
    in_specs = [pl.BlockSpec((tm, D_MODEL), lambda i: (i, 0)),
                _mod_spec(6 * k, tm, rows_per_mod, row0),
                _mod_spec(6 * k + 1, tm, rows_per_mod, row0),
                _mod_spec(6 * k + 2, tm, rows_per_mod, row0),
                _layer_const(wts["norm_w"], 3 * li + 2 * k),
                _layer_const(w_in, 2 * li + k), _layer_const(w_out, 2 * li + k)]
    if mix:
        args += [mod, *mix, wts["mix_w_out"]]
        in_specs += [_mod_spec(5, tm, rows_per_mod, row0)]
        in_specs += [pl.BlockSpec((tm, y.shape[1]), lambda i: (i, 0)) for y in mix]
        in_specs += [_layer_const(wts["mix_w_out"], li)]
    return pl.pallas_call(
        functools.partial(_ffn_kernel, len(mix)),
        grid=(t // tm,),
        in_specs=in_specs,
        out_specs=pl.BlockSpec((tm, D_MODEL), lambda i: (i, 0)),
        out_shape=jax.ShapeDtypeStruct((t, D_MODEL), F32),
        compiler_params=_cparams("parallel"),
        name="ffn",
    )(*args)


_IN_SPLITS = (("z", D_SSM, BF16), ("xbc", SSM_CONV_CH, F32), ("hy", HY_IN, F32), ("ret", RET_IN, BF16),
              ("att", ATT_IN, BF16), ("dt", DT_PAD, F32))


C_DT = D_SSM + SSM_CONV_CH
N_DT = 2 * SSM_HEADS
W_TAIL = HY_IN + RET_IN + ATT_IN


def _inproj_kernel(x_ref, sh_ref, sc_ref, nw_ref, w_ref, wdt_ref, *rest):
    o_refs, wt_scr = rest[:-1], rest[-1]

    @pl.when(pl.program_id(0) == 0)
    def _():
        wt_scr[...] = w_ref[:, C_DT + N_DT:C_DT + N_DT + W_TAIL]

    h = (_rms(x_ref[...], nw_ref[...]) * (1.0 + sc_ref[...]) + sh_ref[...]).astype(BF16)
    off = 0
    for (name, width, dtype), o_ref in zip(_IN_SPLITS, o_refs):
        if name == "dt":
            w = wdt_ref[...]
        elif off < C_DT:
            w = w_ref[:, off:off + width]
        else:
            w = wt_scr[:, off - C_DT:off - C_DT + width]
        o_ref[...] = jnp.dot(h, w, preferred_element_type=F32).astype(dtype)
        off += width


def _inproj(x, mod, row0, rows_per_mod, wts, li):
    t = x.shape[0]
    tm = min(TOKEN_TILE, rows_per_mod)
    return pl.pallas_call(
        _inproj_kernel,
        grid=(t // tm,),
        in_specs=[pl.BlockSpec((tm, D_MODEL), lambda i: (i, 0)),
                  _mod_spec(3, tm, rows_per_mod, row0),
                  _mod_spec(4, tm, rows_per_mod, row0),
                  _layer_const(wts["norm_w"], 3 * li + 1),
                  _layer_const(wts["mix_w_in"], li), _layer_const(wts["mix_w_dt"], li)],
        out_specs=[pl.BlockSpec((tm, width), lambda i: (i, 0)) for _, width, _ in _IN_SPLITS],
        out_shape=[jax.ShapeDtypeStruct((t, width), dtype) for _, width, dtype in _IN_SPLITS],
        scratch_shapes=[pltpu.VMEM((D_MODEL, W_TAIL), BF16)],
        compiler_params=_cparams("arbitrary"),
        name="mix_in",
    )(x, mod, mod, wts["norm_w"], wts["mix_w_in"], wts["mix_w_dt"])


def _conv3_chunk(x_ref, c, nc, w, b):
    q = CHUNK
    l = nc * q
    r0 = pl.multiple_of(c * q, q)
    x = x_ref[pl.ds(r0, q), :]
    prev = x_ref[pl.ds(jnp.maximum(r0 - 1, 0), 1), :]
    nxt = x_ref[pl.ds(jnp.minimum(r0 + q, l - 1), 1), :]
    prev = jnp.where(c > 0, prev, 0.0)
    nxt = jnp.where(c < nc - 1, nxt, 0.0)
    rid = lax.broadcasted_iota(jnp.int32, (q, 1), 0)
    xm1 = jnp.where(rid == 0, prev, pltpu.roll(x, 1, 0))
    xp1 = jnp.where(rid == q - 1, nxt, pltpu.roll(x, q - 1, 0))
    return xm1 * w[0:1, :] + x * w[1:2, :] + xp1 * w[2:3, :] + b


def _tri(lower):
    i = lax.broadcasted_iota(jnp.int32, (CHUNK, CHUNK), 0)
    j = lax.broadcasted_iota(jnp.int32, (CHUNK, CHUNK), 1)
    return (j <= i) if lower else (j >= i)


def _rows(c):
    return pl.ds(pl.multiple_of(c * CHUNK, CHUNK), CHUNK)


def _split_cat(v, parts, axis):
    out, r = [], v
    for i in range(parts):
        piece = r.astype(BF16)
        out.append(piece)
        if i + 1 < parts:
            r = r - piece.astype(F32)
    return jnp.concatenate(out, axis=axis)


def _seg_mean(x, seg):
    w = x.shape[-1]
    ones = _block_diag(w, seg, 1.0).astype(BF16)
    return jnp.dot(_split_cat(x, 2, axis=1), jnp.concatenate([ones, ones], axis=0),
                   preferred_element_type=F32) * (1.0 / seg)


def _stack_heads(x, masks):
    return jnp.concatenate([jnp.where(m, x, 0.0) for m in masks], axis=0)


def _slot_out(tail, nb, li, stacked, ng, whole):
    zeros = (0,) * len(tail)
    slot = li if stacked else 0
    if whole:
        spec = pl.BlockSpec((ng, DEPTH) + tail, lambda b: (b, 0) + zeros)
    else:
        spec = pl.BlockSpec((ng, None) + tail, lambda b: (b, slot) + zeros)
    return spec, jax.ShapeDtypeStruct((nb, DEPTH if stacked else 1) + tail, F32)


def _slot_view(ref, g, slot):
    if slot is None:
        return ref.at[g]
    for other in range(DEPTH):
        if other != slot:
            ref[g, other] = jnp.zeros(ref.shape[2:], ref.dtype)
    return ref.at[g, slot]


def _carry_prev(args, in_specs, prevs, first_out):
    aliases = {}
    for j, prev in enumerate(prevs):
        aliases[len(args)] = first_out + j
        args.append(prev)
        in_specs.append(pl.BlockSpec(memory_space=pl.ANY))
    return aliases


SSD_SEL_W = 2 * SSM_HEADS * SSM_STATE
CUM_PIECES = 3
WGT_PIECES = 2
DT_REP = CUM_PIECES + WGT_PIECES


def _ssd_kernel(nc, ng, has_s0, slot, *refs):
    refs = list(refs)
    (z_ref, xbc_ref, dt_ref, cw_ref, cb_ref, dtb_ref, alog_ref, dsk_ref, nw_ref) = refs[:9]
    s0_ref = refs[9] if has_s0 else None
    (y_ref, s_ref, xs_scr, xk_scr, xt_scr, dt_scr, cum_scr, dtt_scr, cumt_scr, yf_scr, yb_scr, st_scr,
     msk_scr, sel_scr) = refs[-14:]
    q = CHUNK
    nh, n, p = SSM_HEADS, SSM_STATE, SSM_HEAD_DIM
    rep = nh // SSM_GROUPS
    hm = _head_masks(D_SSM, nh)

    @pl.when(pl.program_id(0) == 0)
    def _():
        i = lax.broadcasted_iota(jnp.int32, (D_SSM, nh * n), 0) // p
        j = lax.broadcasted_iota(jnp.int32, (D_SSM, nh * n), 1) // n
        msk_scr[...] = jnp.where(i == j, 1.0, 0.0).astype(F32)
        k = lax.broadcasted_iota(jnp.int32, (LANES, SSD_SEL_W), 0)
        col = lax.broadcasted_iota(jnp.int32, (LANES, SSD_SEL_W), 1)
        is_cum = col < nh * n
        grp = k // (2 * nh)
        grp_ok = (is_cum & (grp < CUM_PIECES)) | ((~is_cum) & (grp >= CUM_PIECES) & (grp < DT_REP))
        hit = grp_ok & (k % nh == (col % (nh * n)) // n)
        sel_scr[...] = jnp.where(hit, 1.0, 0.0).astype(BF16)

    cw = cw_ref[...]
    cb = cb_ref[...]
    neg_a = -jnp.exp(alog_ref[...])
    dtb = dtb_ref[...]

    class Seq:
        def __init__(self, g):
            self.z, self.xbc, self.dt_in, self.y, self.s = (z_ref.at[g], xbc_ref.at[g], dt_ref.at[g], y_ref.at[g],
                                                            _slot_view(s_ref, g, slot))
            self.s0 = s0_ref.at[g] if has_s0 else None
            self.xs, self.xk, self.xt, self.dt, self.cum = (xs_scr.at[g], xk_scr.at[g], xt_scr.at[g], dt_scr.at[g],
                                                            cum_scr.at[g])
            self.dtt, self.cumt, self.yf, self.yb, self.st = (dtt_scr.at[g], cumt_scr.at[g], yf_scr.at[g],
                                                              yb_scr.at[g], st_scr.at[g])

    seqs = [Seq(g) for g in range(ng)]

    def prep_seq(sq, c):
        rows = _rows(c)
        xall = _silu(_conv3_chunk(sq.xbc, c, nc, cw, cb))
        sq.xs[rows, :] = xall
        xs = xall[:, 0:D_SSM]
        sq.xk[c] = _stack_heads(xs, hm).astype(BF16)
        sq.xt[c] = xs.T.astype(BF16)
        dt = _softplus(sq.dt_in[rows, :] + dtb)
        la = _split_cat(dt * neg_a, 3, axis=0)
        tri = jnp.concatenate([_tri(True), _tri(False)], axis=0)
        tri = jnp.where(tri, 1.0, 0.0).astype(BF16)
        cs = jnp.dot(jnp.concatenate([tri] * 3, axis=1), la, preferred_element_type=F32)
        lane = lax.broadcasted_iota(jnp.int32, (1, LANES), 1)
        cum = jnp.where(lane % (2 * nh) < nh, cs[0:q], cs[q:2 * q])
        sq.dt[rows, :] = dt
        sq.cum[rows, :] = cum
        sq.dtt[c] = dt.T
        sq.cumt[c] = cum.T

    def prep(c, carry):
        for sq in seqs:
            prep_seq(sq, c)
        return carry

    lax.fori_loop(0, nc, prep, 0)

    for sq in seqs:
        for d in range(2):
            for h in range(nh):
                blk = slice(h * p, (h + 1) * p)
                if has_s0:
                    parts = [jnp.zeros((p, n), F32)] * nh
                    parts[h] = sq.s0[d, h].T
                    sq.st[d, blk, :] = jnp.concatenate(parts, axis=1)
                else:
                    sq.st[d, blk, :] = jnp.zeros((p, nh * n), F32)

    def pack_scalars(sq, c, d):
        rows = _rows(c)
        dt = sq.dt[rows, :]
        cum = sq.cum[rows, :]
        edge = q - 1 if d == 0 else 0
        lane = lax.broadcasted_iota(jnp.int32, (1, LANES), 1)
        used = (lane < DT_REP * 2 * nh) & ((lane % (2 * nh)) // nh == d)
        cum = jnp.where(used, cum, 0.0)
        wgt = jnp.exp(cum[edge:edge + 1, :] - cum) * dt
        grp = lane // (2 * nh)
        packed = jnp.zeros_like(cum)
        for src, pieces, g0 in ((cum, CUM_PIECES, 0), (wgt, WGT_PIECES, CUM_PIECES)):
            rest = src
            for i in range(pieces):
                piece = rest.astype(BF16).astype(F32)
                packed = jnp.where(grp == g0 + i, piece, packed)
                rest = rest - piece
        return jnp.where(used, packed, 0.0).astype(BF16)

    def operands(sq, c, d):
        rows = _rows(c)
        bm = [sq.xs[rows, D_SSM + g * n:D_SSM + (g + 1) * n] for g in range(SSM_GROUPS)]
        cm = [sq.xs[rows, D_SSM + (SSM_GROUPS + g) * n:D_SSM + (SSM_GROUPS + g + 1) * n]
              for g in range(SSM_GROUPS)]
        return bm, cm

    def decay_blocks(sq, c, d, e, cb_t):
        mask = _tri(d == 0)
        blocks, ecol = [], []
        for h in range(nh):
            r = d * nh + h
            col = e[:, h * n:(h + 1) * n]
            decay = jnp.exp(jnp.where(mask, col - sq.cumt[c, r:r + 1, :], -jnp.inf))
            blocks.append((cb_t[h // rep] * decay * sq.dtt[c, r:r + 1, :]).astype(BF16))
            ecol.append(jnp.exp(col))
        lo = lax.broadcasted_iota(jnp.int32, (1, n), 1) < p
        ecol = jnp.concatenate([jnp.where(lo, ecol[2 * j], ecol[2 * j + 1]) for j in range(nh // 2)], axis=1)
        return jnp.concatenate(blocks, axis=1), ecol

    def scan(i, carry):
        j = nc - 1 - i
        jobs = [(sq, c, d) for sq in seqs for c, d in ((i, 0), (j, 1))]
        packed = jnp.concatenate([pack_scalars(*job) for job in jobs], axis=0)
        e_all = jnp.dot(packed, sel_scr[...], preferred_element_type=F32)
        es = [e_all[t * q:(t + 1) * q] for t in range(len(jobs))]
        ops = [operands(*job) for job in jobs]
        cb_t = [[_bdot_nt(cm[g], bm[g]) for g in range(SSM_GROUPS)] for bm, cm in ops]
        dec = [decay_blocks(*job, es[t], cb_t[t]) for t, job in enumerate(jobs)]
        intra = [jnp.dot(dec[t][0], sq.xk[c], preferred_element_type=F32) for t, (sq, c, d) in enumerate(jobs)]
        s_old = [sq.st[d] for sq, c, d in jobs]
        inter = [_bdot_nt(jnp.concatenate([cm[h // rep] for h in range(nh)], axis=1), s_old[t])
                 for t, (bm, cm) in enumerate(ops)]
        wk = [(jnp.concatenate([bm[h // rep] for h in range(nh)], axis=1) * es[t][:, nh * n:2 * nh * n]).astype(BF16)
              for t, (bm, cm) in enumerate(ops)]
        upd = [jnp.dot(sq.xt[c], wk[t], preferred_element_type=F32) for t, (sq, c, d) in enumerate(jobs)]
        for t, (sq, c, d) in enumerate(jobs):
            edge = q - 1 if d == 0 else 0
            sq.st[d] = jnp.exp(es[t][edge:edge + 1, 0:nh * n]) * s_old[t] + upd[t] * msk_scr[...]
            y = intra[t] + dec[t][1] * inter[t]
            if d == 0:
                sq.yf[_rows(c), :] = y
            else:
                sq.yb[_rows(c), :] = y
        return carry

    lax.fori_loop(0, nc, scan, 0)

    for sq in seqs:
        for d in range(2):
            for h in range(nh):
                sq.s[d, h] = sq.st[d, h * p:(h + 1) * p, h * n:(h + 1) * n].T

    dsk = dsk_ref[...]
    nw = nw_ref[...]

    def fin(c, carry):
        rows = _rows(c)
        for sq in seqs:
            y = sq.yf[rows, :] + sq.yb[rows, :] + dsk * sq.xs[rows, 0:D_SSM]
            sq.y[rows, :] = _rms(y * _silu(sq.z[rows, :].astype(F32)), nw)
        return carry

    lax.fori_loop(0, nc, fin, 0)


def _ssd(z, xbc, dt, wts, li, s0, nb, l, stacked=False, prev=None):
    nc = l // CHUNK
    has_s0 = s0 is not None
    names = ("ssd_conv_w", "ssd_conv_b", "ssd_dt_bias", "ssd_a_log", "ssd_d", "ssd_norm_w")
    ng = SEQ_GROUP
    args = [z.reshape(nb, l, D_SSM), xbc.reshape(nb, l, SSM_CONV_CH), dt.reshape(nb, l, DT_PAD)]
    args += [wts[n] for n in names]
    in_specs = [_seq_spec(l, D_SSM, ng), _seq_spec(l, SSM_CONV_CH, ng), _seq_spec(l, DT_PAD, ng)]
    in_specs += [_layer_const(wts[n], li) for n in names]
    if has_s0:
        args.append(s0)
        in_specs.append(pl.BlockSpec((ng, None, 2, SSM_HEADS, SSM_STATE, SSM_HEAD_DIM),
                                     lambda b: (b, li, 0, 0, 0, 0)))
    whole = stacked and prev is None
    st_spec, st_shape = _slot_out((2, SSM_HEADS, SSM_STATE, SSM_HEAD_DIM), nb, li, stacked, ng, whole)
    aliases = _carry_prev(args, in_specs, [] if prev is None else [prev], 1)
    y, s = pl.pallas_call(
        functools.partial(_ssd_kernel, nc, ng, has_s0, li if whole else None),
        grid=(nb // ng,),
        in_specs=in_specs,
        out_specs=[_seq_spec(l, D_SSM, ng), st_spec],
        out_shape=[jax.ShapeDtypeStruct((nb, l, D_SSM), F32), st_shape],
        input_output_aliases=aliases,
        scratch_shapes=[pltpu.VMEM((ng, l, SSM_CONV_CH), F32),
                        pltpu.VMEM((ng, nc, SSM_HEADS * CHUNK, D_SSM), BF16),
                        pltpu.VMEM((ng, nc, D_SSM, CHUNK), BF16),
                        pltpu.VMEM((ng, l, DT_PAD), F32), pltpu.VMEM((ng, l, DT_PAD), F32),
                        pltpu.VMEM((ng, nc, DT_PAD, CHUNK), F32), pltpu.VMEM((ng, nc, DT_PAD, CHUNK), F32),
                        pltpu.VMEM((ng, l, D_SSM), F32), pltpu.VMEM((ng, l, D_SSM), F32),
                        pltpu.VMEM((ng, 2, D_SSM, SSM_HEADS * SSM_STATE), F32),
                        pltpu.VMEM((D_SSM, SSM_HEADS * SSM_STATE), F32),
                        pltpu.VMEM((LANES, SSD_SEL_W), BF16)],
        compiler_params=_cparams("arbitrary"),
        name="ssd",
    )(*args)
    return y.reshape(nb * l, D_SSM), s


def _ret_kernel(nc, ng, has_s0, slot, *refs):
    refs = list(refs)
    u_ref, dl_ref, gn_ref = refs[:3]
    s0_ref = refs[3] if has_s0 else None
    y_ref, s_ref, yf_scr, yb_scr, st_scr, dm_scr, e_scr, ea_scr, bd_scr = refs[-9:]
    q = CHUNK
    hd = RET_HEAD_DIM
    hm = _head_masks(D_RET, RET_HEADS)

    @pl.when(pl.program_id(0) == 0)
    def _():
        log_g = -_softplus(-dl_ref[...])
        ii = lax.broadcasted_iota(jnp.int32, (q, q), 0)
        jj = lax.broadcasted_iota(jnp.int32, (q, q), 1)
        dij = (ii - jj).astype(F32)
        ri = lax.broadcasted_iota(jnp.int32, (q, 1), 0).astype(F32)
        lfs, lbs = [], []
        for h in range(RET_HEADS):
            lf = log_g[:, h:h + 1]
            lb = log_g[:, RET_HEADS + h:RET_HEADS + h + 1]
            lfs.append(lf)
            lbs.append(lb)
            d_f = jnp.exp(jnp.where(dij >= 0, dij * lf, -jnp.inf))
            d_b = jnp.exp(jnp.where(dij <= 0, -dij * lb, -jnp.inf))
            dm_scr[:, h * q:(h + 1) * q] = d_f + d_b
        lf_l = _by_head(hm, lfs)
        lb_l = _by_head(hm, lbs)
        e_scr[0] = jnp.exp((ri + 1.0) * lf_l)
        e_scr[1] = jnp.exp((q - ri) * lb_l)
        e_scr[2] = jnp.exp((q - 1.0 - ri) * lf_l)
        e_scr[3] = jnp.exp(ri * lb_l)
        ea_scr[0:1, :] = jnp.exp(q * lf_l)
        ea_scr[1:2, :] = jnp.exp(q * lb_l)
        bd_scr[...] = _block_diag(D_RET, hd, 1.0)

    class Seq:
        def __init__(self, g):
            self.u, self.y, self.s = u_ref.at[g], y_ref.at[g], _slot_view(s_ref, g, slot)
            self.s0 = s0_ref.at[g] if has_s0 else None
            self.yf, self.yb, self.st = yf_scr.at[g], yb_scr.at[g], st_scr.at[g]

    seqs = [Seq(g) for g in range(ng)]

    for sq in seqs:
        for d in range(2):
            for h in range(RET_HEADS):
                blk = slice(h * hd, (h + 1) * hd)
                if has_s0:
                    parts = [jnp.zeros((hd, hd), F32)] * RET_HEADS
                    parts[h] = sq.s0[d, h]
                    sq.st[d, blk, :] = jnp.concatenate(parts, axis=1)
                else:
                    sq.st[d, blk, :] = jnp.zeros((hd, D_RET), F32)

    def qkv(sq, rows):
        return (sq.u[rows, 0:D_RET], sq.u[rows, D_RET:2 * D_RET].astype(F32) * (RET_HEAD_DIM ** -0.5),
                sq.u[rows, 2 * D_RET:3 * D_RET])

    def state_step(sq, d, qq, kk, vv):
        s_old = sq.st[d]
        y = e_scr[d] * _bdot(qq, s_old)
        sq.st[d] = ea_scr[d:d + 1, :] * s_old + _bdot_tn(kk * e_scr[2 + d], vv) * bd_scr[...]
        return y

    def scan(i, carry):
        rows = _rows(i)
        rows_b = _rows(nc - 1 - i)
        fwd = [qkv(sq, rows) for sq in seqs]
        bwd = [qkv(sq, rows_b) for sq in seqs]
        sc = [_bdot_nt(qq, _stack_heads(kk, hm)) * dm_scr[...] for qq, kk, _ in fwd]
        intra = [_bdot(sc[g], _stack_heads(fwd[g][2], hm)) for g in range(ng)]
        inter = [state_step(sq, 0, *fwd[g]) for g, sq in enumerate(seqs)]
        back = [state_step(sq, 1, *bwd[g]) for g, sq in enumerate(seqs)]
        for g, sq in enumerate(seqs):
            sq.yf[rows, :] = intra[g] + inter[g]
            sq.yb[rows_b, :] = back[g]
        return carry

    lax.fori_loop(0, nc, scan, 0)

    for sq in seqs:
        for d in range(2):
            for h in range(RET_HEADS):
                sq.s[d, h] = sq.st[d, h * hd:(h + 1) * hd, h * hd:(h + 1) * hd]

    gn = gn_ref[...]

    def fin(c, carry):
        rows = _rows(c)
        for sq in seqs:
            y = sq.yf[rows, :] + sq.yb[rows, :]
            cen = y - _seg_mean(y, hd)
            var = _seg_mean(cen * cen, hd)
            sq.y[rows, :] = cen * lax.rsqrt(var + NORM_EPS) * gn * _silu(sq.u[rows, 3 * D_RET:4 * D_RET].astype(F32))
        return carry

    lax.fori_loop(0, nc, fin, 0)


def _retention(u, wts, li, s0, nb, l, stacked=False, prev=None):
    nc = l // CHUNK
    has_s0 = s0 is not None
    names = ("ret_decay_logit", "ret_gn_w")
    ng = SEQ_GROUP
    args = [u.reshape(nb, l, RET_IN)] + [wts[n] for n in names]
    in_specs = [_seq_spec(l, RET_IN, ng)] + [_layer_const(wts[n], li) for n in names]
    if has_s0:
        args.append(s0)
        in_specs.append(pl.BlockSpec((ng, None, 2, RET_HEADS, RET_HEAD_DIM, RET_HEAD_DIM),
                                     lambda b: (b, li, 0, 0, 0, 0)))
    whole = stacked and prev is None
    st_spec, st_shape = _slot_out((2, RET_HEADS, RET_HEAD_DIM, RET_HEAD_DIM), nb, li, stacked, ng, whole)
    aliases = _carry_prev(args, in_specs, [] if prev is None else [prev], 1)
    y, s = pl.pallas_call(
        functools.partial(_ret_kernel, nc, ng, has_s0, li if whole else None),
        grid=(nb // ng,),
        in_specs=in_specs,
        out_specs=[_seq_spec(l, D_RET, ng), st_spec],
        out_shape=[jax.ShapeDtypeStruct((nb, l, D_RET), F32), st_shape],
        input_output_aliases=aliases,
        scratch_shapes=[pltpu.VMEM((ng, l, D_RET), F32), pltpu.VMEM((ng, l, D_RET), F32),
                        pltpu.VMEM((ng, 2, D_RET, D_RET), F32), pltpu.VMEM((CHUNK, RET_HEADS * CHUNK), F32),
                        pltpu.VMEM((4, CHUNK, D_RET), F32), pltpu.VMEM((8, D_RET), F32),
                        pltpu.VMEM((D_RET, D_RET), F32)],
        compiler_params=_cparams("arbitrary"),
        name="retention",
    )(*args)
    return y.reshape(nb * l, D_RET), s


def _split(x):
    hi = x.astype(BF16)
    return hi, (x - hi.astype(F32)).astype(BF16)


def _dot3(a_hi, a_lo, b_hi, b_lo):
    d = lambda p, q: jnp.dot(p, q, preferred_element_type=F32)
    return d(a_hi, b_hi) + (d(a_lo, b_hi) + d(a_hi, b_lo))


@functools.lru_cache(maxsize=None)
def _dft_fwd_host(l):
    n = 2 * l
    f = np.arange(l, dtype=np.int64)[:, None]
    s = np.arange(l, dtype=np.int64)[None, :]
    ang = ((f * s) % n).astype(np.float64) * (2.0 * math.pi / n)
    im = -np.sin(ang)
    im[0] = np.where(np.arange(l) % 2 == 0, 1.0, -1.0)
    return np.concatenate([np.cos(ang), im], axis=0).astype(np.float32)


def _dft_tables(l):
    n = 2 * l
    fwd = jnp.asarray(_dft_fwd_host(l))
    wgt = np.full((n, 1), 2.0 / n, np.float32)
    wgt[0] = wgt[l] = 1.0 / n
    return _split(fwd) + _split((fwd * wgt).T)


def _dft_block_tables(tables, l):
    fwd_hi, fwd_lo, inv_hi, inv_lo = tables
    r = DFT_BLOCK
    nblk = l // r
    fb = lambda t: (t[0:l].reshape(nblk, r, l), t[l:2 * l].reshape(nblk, r, l))
    fwd_blk = jnp.concatenate(fb(fwd_hi) + fb(fwd_lo), axis=1)
    inv_blk = jnp.concatenate([inv_hi.reshape(nblk, r, 2 * l), inv_lo.reshape(nblk, r, 2 * l)], axis=1)
    return fwd_blk, inv_blk


def _hy_filter_kernel(l, feats_ref, dec_ref, w1_ref, b1_ref, w2_ref, b2_ref, w3_ref, fr_ref, fh_ref, fl_ref,
                      a_ref, b_ref, d_ref):
    fr = fr_ref[...]
    xdot = lambda a, b: _dot3(*_split(a), *_split(b))
    h = jnp.sin(fr * (xdot(feats_ref[...], w1_ref[...]) + b1_ref[...]))
    h = jnp.sin(fr * (xdot(h, w2_ref[...]) + b2_ref[...]))
    h = xdot(h, w3_ref[...])
    dec = jnp.concatenate([dec_ref[...]] * HY_ORDER, axis=-1)
    row0 = lax.broadcasted_iota(jnp.int32, (l, 1), 0) == 0
    hf = h[:, 0:HY_ORDER * D_HY] * dec
    hb = h[:, HY_ORDER * D_HY:2 * HY_ORDER * D_HY] * dec
    hb = jnp.where(row0, 0.0, hb)
    hs = _split(hf + hb)
    hd = _split(hf - hb)
    re = _dot3(fh_ref[0:l, :], fl_ref[0:l, :], *hs)
    ny = _dot3(fh_ref[l:l + 8, :], fl_ref[l:l + 8, :], *hs)[0:1]
    im = _dot3(fh_ref[l:2 * l, :], fl_ref[l:2 * l, :], *hd)
    for o in range(HY_ORDER):
        cols = slice(o * D_HY, (o + 1) * D_HY)
        a_ref[o] = re[:, cols]
        b_ref[o] = jnp.where(row0, 0.0, im[:, cols])
        d_ref[o] = jnp.where(row0, ny[:, cols], re[:, cols])


def _hy_filter(l, fwd_hi, fwd_lo, wts, li):
    pos = np.arange(l, dtype=np.float32)
    t = pos / np.float32(l - 1)
    bands = np.linspace(1e-4, HY_BANDS - 1, HY_BANDS, dtype=np.float32)
    ang = np.float32(2.0 * math.pi / l) * pos[:, None] * bands[None, :]
    feats = np.concatenate([t[:, None], np.cos(ang), -np.sin(ang)], axis=-1).astype(np.float32)
    feats = np.pad(feats, ((0, 0), (0, LANES - HY_EMB)))
    max_decay = math.log(HY_TARGET) / HY_FAST_DECAY
    min_decay = math.log(HY_TARGET) / HY_SLOW_DECAY
    deltas = np.abs(np.linspace(min_decay, max_decay, D_HY, dtype=np.float32))
    dec = np.exp(-t[:, None] * deltas[None, :]).astype(np.float32)
    spec = jax.ShapeDtypeStruct((HY_ORDER, l, D_HY), F32)
    names = ("hy_w1", "hy_b1", "hy_w2", "hy_b2", "hy_w3", "hy_freq")
    return pl.pallas_call(
        functools.partial(_hy_filter_kernel, l),
        grid=(1,),
        in_specs=[_full(feats.shape), _full(dec.shape)] + [_layer_const(wts[n], li) for n in names]
                 + [_full(fwd_hi.shape), _full(fwd_lo.shape)],
        out_specs=[_full(spec.shape)] * 3,
        out_shape=[spec] * 3,
        compiler_params=_cparams("arbitrary"),
        name="hyena_filter",
    )(jnp.asarray(feats), jnp.asarray(dec), *[wts[n] for n in names], fwd_hi, fwd_lo)


def _hy_kernel(nc, ng, u_ref, cw_ref, cb_ref, f_ref, g_ref, a_ref, b_ref, d_ref, bias_ref, y_ref,
               uc_scr, vh_scr, vl_scr, sh_scr, sl_scr, z_scr):
    l = nc * CHUNK
    r = DFT_BLOCK
    nblk = l // r
    cw = cw_ref[...]
    cb = cb_ref[...]
    dot = lambda p, q: jnp.dot(p, q, preferred_element_type=F32)
    wide = lambda x: jnp.concatenate([x] * ng, axis=1)
    seq_cols = lambda g: slice(g * D_HY, (g + 1) * D_HY)

    def for_blocks(body):
        if nblk == 1:
            body(0)
        else:
            lax.fori_loop(0, nblk, lambda i, carry: (body(i), carry)[1], 0)

    def conv(c, carry):
        rows = _rows(c)
        for g in range(ng):
            uc = _conv3_chunk(u_ref.at[g], c, nc, cw, cb)
            uc_scr[g, rows, :] = uc
            vh_scr[rows, seq_cols(g)], vl_scr[rows, seq_cols(g)] = _split(uc[:, 0:D_HY])
        return carry

    lax.fori_loop(0, nc, conv, 0)

    def long_conv(o):
        def spectrum(i):
            rows = pl.ds(pl.multiple_of(i * r, r), r)
            rows_im = pl.ds(pl.multiple_of(l + i * r, r), r)
            p = dot(f_ref[i], vh_scr[...])
            pl_ = dot(f_ref[i, 0:2 * r, :], vl_scr[...])
            zr = p[0:r] + (p[2 * r:3 * r] + pl_[0:r])
            zi = p[r:2 * r] + (p[3 * r:4 * r] + pl_[r:2 * r])
            fa, fb, fd = wide(a_ref[o, rows, :]), wide(b_ref[o, rows, :]), wide(d_ref[o, rows, :])
            sh_scr[rows, :], sl_scr[rows, :] = _split(zr * fa - zi * fb)
            sh_scr[rows_im, :], sl_scr[rows_im, :] = _split(zr * fb + zi * fd)

        for_blocks(spectrum)

        def inverse(i):
            rows = pl.ds(pl.multiple_of(i * r, r), r)
            p = dot(g_ref[i], sh_scr[...])
            y = p[0:r] + (p[r:2 * r] + dot(g_ref[i, 0:r, :], sl_scr[...]))
            gate = jnp.concatenate([uc_scr[g, rows, (o + 1) * D_HY:(o + 2) * D_HY] for g in range(ng)], axis=1)
            if o == 0:
                v = jnp.concatenate([uc_scr[g, rows, 0:D_HY] for g in range(ng)], axis=1)
            else:
                v = z_scr[rows, :]
            out = gate * (y + v * wide(bias_ref[o:o + 1, :]))
            if o + 1 < HY_ORDER:
                z_scr[rows, :] = out
                vh_scr[rows, :], vl_scr[rows, :] = _split(out)
            else:
                for g in range(ng):
                    y_ref[g, rows, :] = out[:, seq_cols(g)]

        for_blocks(inverse)

    for o in range(HY_ORDER):
        long_conv(o)


def _hyena(u, wts, li, tables, spectra, nb, l):
    nc = l // CHUNK
    ng = min(HY_GROUP, nb)
    a, b, d = spectra
    names = ("hy_conv_w", "hy_conv_b")
    y = pl.pallas_call(
        functools.partial(_hy_kernel, nc, ng),
        grid=(nb // ng,),
        in_specs=[pl.BlockSpec((ng, l, HY_IN), lambda i: (i, 0, 0))] + [_layer_const(wts[n], li) for n in names]
                 + [_const(tables[0].shape), _const(tables[1].shape),
                    _const((HY_ORDER, l, D_HY)), _const((HY_ORDER, l, D_HY)), _const((HY_ORDER, l, D_HY)),
                    _layer_const(wts["hy_bias"], li)],
        out_specs=pl.BlockSpec((ng, l, D_HY), lambda i: (i, 0, 0)),
        out_shape=jax.ShapeDtypeStruct((nb, l, D_HY), F32),
        scratch_shapes=[pltpu.VMEM((ng, l, HY_IN), F32),
                        pltpu.VMEM((l, ng * D_HY), BF16), pltpu.VMEM((l, ng * D_HY), BF16),
                        pltpu.VMEM((2 * l, ng * D_HY), BF16), pltpu.VMEM((2 * l, ng * D_HY), BF16),
                        pltpu.VMEM((l, ng * D_HY), F32)],
        compiler_params=_cparams("parallel"),
        name="hyena",
    )(u.reshape(nb, l, HY_IN), *[wts[n] for n in names], *tables, a, b, d, wts["hy_bias"])
    return y.reshape(nb * l, D_HY)


def _seg_rms(x, w):
    return x * lax.rsqrt(_seg_mean(x * x, HEAD_DIM) + NORM_EPS) * w


def _stack_q(q):
    lo = lax.broadcasted_iota(jnp.int32, (1, D_KV), 1) < HEAD_DIM
    qa = q[:, 0:D_KV]
    qb = q[:, D_KV:2 * D_KV]
    return jnp.concatenate([jnp.where(lo, qa, 0.0), jnp.where(lo, pltpu.roll(qa, HEAD_DIM, 1), 0.0),
                            jnp.where(lo, 0.0, pltpu.roll(qb, HEAD_DIM, 1)), jnp.where(lo, 0.0, qb)], axis=0)


def _unstack_o(o):
    r = o.shape[0] // ATT_HEADS
    lo = lax.broadcasted_iota(jnp.int32, (1, D_KV), 1) < HEAD_DIM
    ya = jnp.where(lo, o[0:r], pltpu.roll(o[r:2 * r], HEAD_DIM, 1))
    yb = jnp.where(lo, pltpu.roll(o[2 * r:3 * r], HEAD_DIM, 1), o[3 * r:4 * r])
    return jnp.concatenate([ya, yb], axis=1)


def _sink_col(sink_ref, li, r):
    rb = lax.broadcasted_iota(jnp.int32, (ATT_HEADS * r, 1), 0) // r
    col = jnp.full((ATT_HEADS * r, 1), sink_ref[li * ATT_HEADS + ATT_HEADS - 1], F32)
    for h in range(ATT_HEADS - 2, -1, -1):
        col = jnp.where(rb == h, sink_ref[li * ATT_HEADS + h], col)
    return col


def _ctx_attn_kernel(li, ng, slot, u_ref, qn_ref, kn_ref, sink_ref, *rest):
    y_ref, k_ref, v_ref = rest[-3:]
    l = u_ref.shape[1]
    sink = _sink_col(sink_ref, li, l)
    seqs = range(ng)
    u = [u_ref[g].astype(F32) for g in seqs]
    q = [_seg_rms(u[g][:, 0:D_ATT], qn_ref[...]) for g in seqs]
    k = [_seg_rms(u[g][:, D_ATT:D_ATT + D_KV], kn_ref[:, 0:D_KV]) for g in seqs]
    v = [u[g][:, D_ATT + D_KV:D_ATT + 2 * D_KV] for g in seqs]
    for g in seqs:
        _slot_view(k_ref, g, slot)[...] = k[g]
        _slot_view(v_ref, g, slot)[...] = v[g]
    s = [_bdot_nt(_stack_q(q[g]), k[g]) * (HEAD_DIM ** -0.5) for g in seqs]
    m = [jnp.maximum(jnp.max(s[g], axis=-1, keepdims=True), sink) for g in seqs]
    p = [jnp.exp(s[g] - m[g]) for g in seqs]
    den = [jnp.sum(p[g], axis=-1, keepdims=True) + jnp.exp(sink - m[g]) for g in seqs]
    o = [_bdot(p[g] * (1.0 / den[g]), v[g]) for g in seqs]
    for g in seqs:
        y_ref[g] = _unstack_o(o[g])


def _ctx_attention(u, wts, li, nb, l, prev=None):
    ng = ATT_GROUP
    whole = prev is None
    kv_spec, kv_shape = _slot_out((l, D_KV), nb, li, True, ng, whole)
    args = [u.reshape(nb, l, ATT_IN), wts["attn_q_norm"], wts["attn_k_norm"], wts["attn_sink"]]
    in_specs = [_seq_spec(l, ATT_IN, ng), _layer_const(wts["attn_q_norm"], li),
                _layer_const(wts["attn_k_norm"], li), pl.BlockSpec(memory_space=pltpu.SMEM)]
    aliases = _carry_prev(args, in_specs, [] if prev is None else list(prev), 1)
    y, k, v = pl.pallas_call(
        functools.partial(_ctx_attn_kernel, li, ng, li if whole else None),
        grid=(nb // ng,),
        in_specs=in_specs,
        out_specs=[_seq_spec(l, D_ATT, ng), kv_spec, kv_spec],
        out_shape=[jax.ShapeDtypeStruct((nb, l, D_ATT), F32), kv_shape, kv_shape],
        input_output_aliases=aliases,
        compiler_params=_cparams("parallel"),
        name="ctx_attention",
    )(*args)
    return y.reshape(nb * l, D_ATT), k, v


@functools.lru_cache(maxsize=None)
def _rope_tables_host(l):
    n_rows = l // GRID_W
    rows = np.repeat(np.arange(n_rows, dtype=np.float32), GRID_W)
    cols = np.tile(np.arange(GRID_W, dtype=np.float32), n_rows)
    nf = HEAD_DIM // 4
    inv = (np.float32(ROPE_BASE) ** (-np.arange(nf, dtype=np.float32) / np.float32(nf))).astype(np.float32)
    ar = rows[:, None] * inv[None, :]
    ac = cols[:, None] * inv[None, :]
    cos = np.concatenate([np.cos(ar), np.cos(ar), np.cos(ac), np.cos(ac)], axis=-1)
    sin = np.concatenate([-np.sin(ar), np.sin(ar), -np.sin(ac), np.sin(ac)], axis=-1)
    return (np.tile(cos, (1, ATT_HEADS)).astype(np.float32), np.tile(sin, (1, ATT_HEADS)).astype(np.float32))


def _rope(x, cos, sin):
    w = x.shape[-1]
    nf = HEAD_DIM // 4
    lane = lax.broadcasted_iota(jnp.int32, x.shape, 1)
    first = (lane % (2 * nf)) < nf
    partner = jnp.where(first, pltpu.roll(x, w - nf, 1), pltpu.roll(x, nf, 1))
    return x * cos + partner * sin


def _lat_attn_kernel(nblk, li, ng, u_ref, qn_ref, kn_ref, cos_ref, sin_ref, ck_ref, cv_ref, sink_ref,
                     y_ref, q_scr, k_scr, v_scr):
    blk = ATT_BLOCK
    l = nblk * blk
    seqs = range(ng)
    cos_q = cos_ref[...]
    sin_q = sin_ref[...]
    zeros = jnp.zeros((blk, D_KV), F32)
    for g in seqs:
        u = u_ref[g].astype(F32)
        q = _seg_rms(u[:, 0:D_ATT], qn_ref[...])
        k = _seg_rms(u[:, D_ATT:D_ATT + D_KV], kn_ref[:, 0:D_KV])
        q_scr[g] = _rope(q, cos_q, sin_q)
        k_scr[g, 0:blk, :] = zeros
        k_scr[g, blk + l:2 * blk + l, :] = zeros
        v_scr[g, 0:blk, :] = zeros
        v_scr[g, blk + l:2 * blk + l, :] = zeros
        k_scr[g, blk:blk + l, :] = _rope(k, cos_q[:, 0:D_KV], sin_q[:, 0:D_KV])
        v_scr[g, blk:blk + l, :] = u[:, D_ATT + D_KV:D_ATT + 2 * D_KV]

    scale = HEAD_DIM ** -0.5
    r = lax.broadcasted_iota(jnp.int32, (blk, 3 * blk), 0)
    cidx = lax.broadcasted_iota(jnp.int32, (blk, 3 * blk), 1)
    band = (cidx - r >= blk - WINDOW) & (cidx - r <= blk + WINDOW)
    sink = _sink_col(sink_ref, li, blk)

    def block(i, carry):
        rows = _rows(i)
        win = pl.ds(pl.multiple_of(i * blk, blk), 3 * blk)
        kpos = cidx + (i - 1) * blk
        valid = band & (kpos >= 0) & (kpos < l)
        valid = jnp.concatenate([valid] * ATT_HEADS, axis=0)
        qs = [_stack_q(q_scr[g, rows, :]) for g in seqs]
        s_loc = [jnp.where(valid, _bdot_nt(qs[g], k_scr[g, win, :]) * scale, -jnp.inf) for g in seqs]
        s_ctx = [_bdot_nt(qs[g], ck_ref[g]) * scale for g in seqs]
        m = [jnp.maximum(jnp.maximum(jnp.max(s_loc[g], axis=-1, keepdims=True),
                                     jnp.max(s_ctx[g], axis=-1, keepdims=True)), sink) for g in seqs]
        p_loc = [jnp.exp(s_loc[g] - m[g]) for g in seqs]
        p_ctx = [jnp.exp(s_ctx[g] - m[g]) for g in seqs]
        inv = [1.0 / (jnp.sum(p_loc[g], axis=-1, keepdims=True) + jnp.sum(p_ctx[g], axis=-1, keepdims=True)
                      + jnp.exp(sink - m[g])) for g in seqs]
        o = [_bdot(p_ctx[g] * inv[g], cv_ref[g]) + _bdot(p_loc[g] * inv[g], v_scr[g, win, :]) for g in seqs]
        for g in seqs:
            y_ref[g, rows, :] = _unstack_o(o[g])
        return carry

    lax.fori_loop(0, nblk, block, 0)


def _lat_attention(u, wts, li, ck, cv, nb, l):
    lc = ck.shape[2]
    nblk = l // ATT_BLOCK
    ng = ATT_GROUP
    cos, sin = _rope_tables_host(l)
    cache_spec = pl.BlockSpec((ng, None, lc, D_KV), lambda b: (b, li, 0, 0))
    y = pl.pallas_call(
        functools.partial(_lat_attn_kernel, nblk, li, ng),
        grid=(nb // ng,),
        in_specs=[_seq_spec(l, ATT_IN, ng), _layer_const(wts["attn_q_norm"], li),
                  _layer_const(wts["attn_k_norm"], li), _const((l, D_ATT)), _const((l, D_ATT)),
                  cache_spec, cache_spec, pl.BlockSpec(memory_space=pltpu.SMEM)],
        out_specs=_seq_spec(l, D_ATT, ng),
        out_shape=jax.ShapeDtypeStruct((nb, l, D_ATT), F32),
        scratch_shapes=[pltpu.VMEM((ng, l, D_ATT), F32), pltpu.VMEM((ng, l + 2 * ATT_BLOCK, D_KV), F32),
                        pltpu.VMEM((ng, l + 2 * ATT_BLOCK, D_KV), F32)],
        compiler_params=_cparams("parallel"),
        name="lat_attention",
    )(u.reshape(nb, l, ATT_IN), wts["attn_q_norm"], wts["attn_k_norm"], jnp.asarray(cos), jnp.asarray(sin),
      ck, cv, wts["attn_sink"])
    return y.reshape(nb * l, D_ATT)


def _pad_lane_tile(w):
    return jnp.pad(w, [(0, 0)] * (w.ndim - 1) + [(0, -w.shape[-1] % LANES)])


def _dt_weight(w):
    zeros = jnp.zeros(w.shape[:-1] + (DT_PAD - DT_REP * N_DT,), w.dtype)
    return jnp.concatenate([w[..., C_DT:C_DT + N_DT]] * DT_REP + [zeros], axis=-1).astype(BF16)


def _prep_weights(p):
    row = lambda a: a.reshape(DEPTH, 1, -1)
    pad_lanes = lambda a: jnp.pad(row(a), ((0, 0), (0, 0), (0, LANES - a[0].size)))
    return dict(
        norm_w=p["norm_w"].reshape(DEPTH * 3, 1, D_MODEL),
        ffn_w_in=p["ffn_w_in"].astype(BF16).reshape(DEPTH * 2, D_MODEL, 2 * D_FF),
        ffn_w_out=p["ffn_w_out"].astype(BF16).reshape(DEPTH * 2, D_FF, D_MODEL),
        mix_w_in=_pad_lane_tile(p["mix_w_in"]).astype(BF16), mix_w_dt=_dt_weight(p["mix_w_in"]),
        mix_w_out=p["mix_w_out"].astype(BF16),
        ssd_conv_w=p["ssd_conv_w"], ssd_conv_b=row(p["ssd_conv_b"]),
        ssd_dt_bias=pad_lanes(jnp.tile(row(p["ssd_dt_bias"]), (1, 1, DT_REP))),
        ssd_a_log=pad_lanes(jnp.tile(row(p["ssd_a_log"]), (1, 1, DT_REP))),
        ssd_d=row(jnp.repeat(p["ssd_d"], SSM_HEAD_DIM, axis=-1)), ssd_norm_w=row(p["ssd_norm_w"]),
        hy_conv_w=p["hy_conv_w"], hy_conv_b=row(p["hy_conv_b"]), hy_bias=p["hy_bias"],
        hy_w1=jnp.pad(p["hy_w1"], ((0, 0), (0, LANES - HY_EMB), (0, 0))), hy_b1=row(p["hy_b1"]),
        hy_w2=p["hy_w2"], hy_b2=row(p["hy_b2"]), hy_w3=p["hy_w3"], hy_freq=row(p["hy_freq"]),
        ret_decay_logit=pad_lanes(p["ret_decay_logit"]), ret_gn_w=row(p["ret_gn_w"]),
        attn_q_norm=row(jnp.tile(p["attn_q_norm"], (1, ATT_HEADS))),
        attn_k_norm=row(jnp.tile(p["attn_k_norm"], (1, ATT_HEADS))),
        attn_sink=p["attn_sink"].reshape(DEPTH * ATT_HEADS),
    )


def _layer(x, mod, row0, wts, li, nb, l, ssd_s0, ret_s0, ctx_kv, hy_tables, hy_spectra, carried):
    ctx = ctx_kv is None
    rpm = nb * l if ctx else l
    x = _ffn(x, mod, row0, rpm, wts, li, 0)
    z, xbc, hy, ret, att, dt = _inproj(x, mod, row0, rpm, wts, li)
    y_ssd, s_ssd = _ssd(z, xbc, dt, wts, li, ssd_s0, nb, l, stacked=ctx, prev=carried.get("ssd"))
    y_hy = _hyena(hy, wts, li, hy_tables, hy_spectra, nb, l)
    y_ret, s_ret = _retention(ret, wts, li, ret_s0, nb, l, stacked=ctx, prev=carried.get("ret"))
    if ctx:
        y_att, k, v = _ctx_attention(att, wts, li, nb, l, prev=carried.get("kv"))
        carried = dict(ssd=s_ssd, ret=s_ret, kv=(k, v))
    else:
        y_att = _lat_attention(att, wts, li, ctx_kv[0], ctx_kv[1], nb, l)
    x = _ffn(x, mod, row0, rpm, wts, li, 1, mix=(y_ssd, y_hy, y_ret, y_att))
    return x, carried


def kernel(x_prompt, x_sample, cache_k, cache_v, state_ssd, state_ret, c, c_ctx, w_mod, b_mod, norm_w, ffn_w_in, ffn_w_out, mix_w_in, mix_w_out, ssd_conv_w, ssd_conv_b, ssd_dt_bias, ssd_a_log, ssd_d, ssd_norm_w, hy_conv_w, hy_conv_b, hy_w1, hy_b1, hy_w2, hy_b2, hy_w3, hy_freq, hy_bias, ret_decay_logit, ret_gn_w, attn_q_norm, attn_k_norm, attn_sink):
    bp, lp_len, _ = x_prompt.shape
    bs, ls_len, _ = x_sample.shape
    lc = cache_k.shape[2]

    cond = jnp.concatenate([c_ctx[None, :], c, jnp.zeros((MOD_ROWS - 1 - bs, D_MODEL), F32)], axis=0)
    mod = _modulation(cond, w_mod, b_mod)
    wts = _prep_weights(dict(
        norm_w=norm_w, ffn_w_in=ffn_w_in, ffn_w_out=ffn_w_out, mix_w_in=mix_w_in, mix_w_out=mix_w_out,
        ssd_conv_w=ssd_conv_w, ssd_conv_b=ssd_conv_b, ssd_dt_bias=ssd_dt_bias, ssd_a_log=ssd_a_log, ssd_d=ssd_d,
        ssd_norm_w=ssd_norm_w, hy_conv_w=hy_conv_w, hy_conv_b=hy_conv_b, hy_w1=hy_w1, hy_b1=hy_b1, hy_w2=hy_w2,
        hy_b2=hy_b2, hy_w3=hy_w3, hy_freq=hy_freq, hy_bias=hy_bias, ret_decay_logit=ret_decay_logit,
        ret_gn_w=ret_gn_w, attn_q_norm=attn_q_norm, attn_k_norm=attn_k_norm, attn_sink=attn_sink))
    ck = cache_k.reshape(bs, DEPTH, lc, D_KV)
    cv = cache_v.reshape(bs, DEPTH, lc, D_KV)

    tab_p = _dft_tables(lp_len)
    tab_s = _dft_tables(ls_len)
    blk_p = _dft_block_tables(tab_p, lp_len)
    blk_s = _dft_block_tables(tab_s, ls_len)

    yp = x_prompt.reshape(bp * lp_len, D_MODEL)
    ys = x_sample.reshape(bs * ls_len, D_MODEL)
    carried = {}
    for li in range(DEPTH):
        spec_p = _hy_filter(lp_len, tab_p[0], tab_p[1], wts, li)
        spec_s = _hy_filter(ls_len, tab_s[0], tab_s[1], wts, li)
        yp, carried = _layer(yp, mod, li * MOD_ROWS, wts, li, bp, lp_len, None, None, None, blk_p, spec_p, carried)
        ys, _ = _layer(ys, mod, li * MOD_ROWS + 1, wts, li, bs, ls_len, state_ssd, state_ret, (ck, cv),
                       blk_s, spec_s, {})

    kv_shape = (bp, DEPTH, lp_len, ATT_KV_HEADS, HEAD_DIM)
    new_k, new_v = carried["kv"]
    return (yp.reshape(bp, lp_len, D_MODEL), ys.reshape(bs, ls_len, D_MODEL),
            new_k.reshape(kv_shape), new_v.reshape(kv_shape), carried["ssd"], carried["ret"])
```

```python
import functools
import math

import numpy as np
import jax
import jax.numpy as jnp
from jax import lax
from jax.experimental import pallas as pl
from jax.experimental.pallas import tpu as pltpu

F32 = jnp.float32
BF16 = jnp.bfloat16
HI = lax.Precision.HIGHEST

D_MODEL = 1024
DEPTH = 2
GRID_W = 64
D_FF = 2816
N_MOD = 9
NORM_EPS = 1e-6
CHUNK = 128
D_SSM = 256
SSM_HEADS = 4
SSM_HEAD_DIM = 64
SSM_STATE = 128
SSM_GROUPS = 2
SSM_CONV_CH = D_SSM + 2 * SSM_GROUPS * SSM_STATE
D_HY = 256
HY_ORDER = 2
HY_BANDS = 16
HY_EMB = 1 + 2 * HY_BANDS
HY_HIDDEN = 64
HY_FAST_DECAY = 0.3
HY_SLOW_DECAY = 1.5
HY_TARGET = 1e-2
HY_IN = (HY_ORDER + 1) * D_HY
D_RET = 256
RET_HEADS = 4
RET_HEAD_DIM = 64
RET_IN = 4 * D_RET
ATT_HEADS = 4
ATT_KV_HEADS = 2
HEAD_DIM = 64
D_ATT = ATT_HEADS * HEAD_DIM
D_KV = ATT_KV_HEADS * HEAD_DIM
ATT_IN = D_ATT + 2 * D_KV
WINDOW = 128
ATT_BLOCK = 128
ROPE_BASE = 10000.0
D_MIX = D_SSM + D_HY + D_RET + D_ATT

LANES = 128
DT_PAD = LANES
VMEM_LIMIT = 56 * 1024 * 1024
MOD_ROWS = 8

TOKEN_TILE = 512
DFT_BLOCK = 256
HY_GROUP = 4
SEQ_GROUP = 2
ATT_GROUP = 2


def _cparams(*sem):
    return pltpu.CompilerParams(dimension_semantics=sem, vmem_limit_bytes=VMEM_LIMIT)


def _rms(x, w):
    return x * lax.rsqrt(jnp.mean(x * x, axis=-1, keepdims=True) + NORM_EPS) * w


def _silu(x):
    return x * (1.0 / (1.0 + jnp.exp(-x)))


def _softplus(x):
    return jnp.maximum(x, 0.0) + jnp.log1p(jnp.exp(-jnp.abs(x)))


def _bdot(a, b):
    return jnp.dot(a.astype(BF16), b.astype(BF16), preferred_element_type=F32)


def _bdot_nt(a, b):
    return lax.dot_general(a.astype(BF16), b.astype(BF16), (((1,), (1,)), ((), ())),
                           preferred_element_type=F32)


def _bdot_tn(a, b):
    return lax.dot_general(a.astype(BF16), b.astype(BF16), (((0,), (0,)), ((), ())),
                           preferred_element_type=F32)


def _hdot(a, b):
    return jnp.dot(a, b, preferred_element_type=F32, precision=HI)


def _full(shape):
    n = len(shape)
    return pl.BlockSpec(shape, lambda *_: (0,) * n)


def _const(shape):
    n = len(shape)
    return pl.BlockSpec(shape, lambda *_: (0,) * n, pipeline_mode=pl.Buffered(1))


def _layer_const(arr, li):
    tail = arr.shape[1:]
    zeros = (0,) * len(tail)
    return pl.BlockSpec((None,) + tail, lambda *_: (li,) + zeros, pipeline_mode=pl.Buffered(1))


class _Stream:
    def __init__(self, tm, nct, nlt, l_lat, first=0, count=None):
        self.tm, self.nct, self.nlt, self.l_lat = tm, nct, nlt, l_lat
        self.first = first
        self.count = nct + nlt if count is None else count

    def part(self, path):
        first, count = (0, self.nct) if path == 0 else (self.nct, self.nlt)
        return _Stream(self.tm, self.nct, self.nlt, self.l_lat, first, count)

    @property
    def tokens(self):
        return self.count * self.tm

    def merged(self, width):
        return pl.BlockSpec((self.tm, width), lambda i: (i + self.first, 0))

    def owned(self, width):
        return pl.BlockSpec((self.tm, width), lambda i: (i, 0))

    def pair(self, width):
        ctx = pl.BlockSpec((self.tm, width), lambda i: (jnp.minimum(i + self.first, self.nct - 1), 0))
        lat = pl.BlockSpec((self.tm, width), lambda i: (jnp.maximum(i + self.first - self.nct, 0), 0))
        return [ctx, lat]

    def mod(self, k, row_ctx, row_lat):
        def index(i):
            t = i + self.first
            lat_row = row_lat + (jnp.maximum(t - self.nct, 0) * self.tm) // self.l_lat
            return (jnp.where(t < self.nct, row_ctx, lat_row), 0, k)
        return pl.BlockSpec((None, 1, D_MODEL), index)


def _seq_spec(l, w, ng=None, seq0=0):
    step0 = seq0 // (ng or 1)
    return pl.BlockSpec((ng, l, w), lambda b: (b + step0, 0, 0))


def _head_masks(width, heads):
    lane = lax.broadcasted_iota(jnp.int32, (1, width), 1)
    hd = width // heads
    return [(lane >= h * hd) & (lane < (h + 1) * hd) for h in range(heads)]


def _by_head(masks, vals):
    out = vals[-1]
    for m, v in zip(masks[-2::-1], vals[-2::-1]):
        out = jnp.where(m, v, out)
    return out


def _block_diag(n, blk, value):
    i = lax.broadcasted_iota(jnp.int32, (n, n), 0) // blk
    j = lax.broadcasted_iota(jnp.int32, (n, n), 1) // blk
    return jnp.where(i == j, value, 0.0).astype(F32)


def _mod_kernel(c_ref, w_ref, b_ref, o_ref):
    c = c_ref[...]
    o_ref[...] = _bdot(_silu(c), w_ref[...]) + b_ref[...]


def _modulation(cond, w_mod, b_mod):
    out = pl.pallas_call(
        _mod_kernel,
        grid=(DEPTH, N_MOD),
        in_specs=[pl.BlockSpec((MOD_ROWS, D_MODEL), lambda l, j: (0, 0)),
                  pl.BlockSpec((None, D_MODEL, D_MODEL), lambda l, j: (l, 0, j)),
                  pl.BlockSpec((None, 1, D_MODEL), lambda l, j: (l, 0, j))],
        out_specs=pl.BlockSpec((None, MOD_ROWS, D_MODEL), lambda l, j: (l, 0, j)),
        out_shape=jax.ShapeDtypeStruct((DEPTH, MOD_ROWS, N_MOD * D_MODEL), F32),
        compiler_params=_cparams("arbitrary", "arbitrary"),
        name="modulation",
    )(cond, w_mod, b_mod.reshape(DEPTH, 1, N_MOD * D_MODEL))
    return out.reshape(DEPTH * MOD_ROWS, 1, N_MOD * D_MODEL)


def _ffn_kernel(n_mix, pair_x, pair_mix, nct, first, *refs):
    refs = list(refs)
    o_ref = refs.pop()
    is_ctx = pl.program_id(0) + first < nct

    def read(pair):
        if pair:
            a, b = refs.pop(0), refs.pop(0)
            return jnp.where(is_ctx, a[...], b[...])
        return refs.pop(0)[...]

    x = read(pair_x)
    sh_ref, sc_ref, g_ref, nw_ref, wi_ref, wo_ref = (refs.pop(0) for _ in range(6))
    if n_mix:
        gm_ref = refs.pop(0)
        ys = [read(pair_mix) for _ in range(n_mix)]
        wm_ref = refs.pop(0)
        w = D_MIX // n_mix
        acc = _bdot(ys[0], wm_ref[0:w, :])
        for j in range(1, n_mix):
            acc += _bdot(ys[j], wm_ref[j * w:(j + 1) * w, :])
        x = x + gm_ref[...] * acc
    h = (_rms(x, nw_ref[...]) * (1.0 + sc_ref[...]) + sh_ref[...]).astype(BF16)
    gate = jnp.dot(h, wi_ref[:, 0:D_FF], preferred_element_type=F32)
    up = jnp.dot(h, wi_ref[:, D_FF:2 * D_FF], preferred_element_type=F32)
    o_ref[...] = x + 0.5 * g_ref[...] * _bdot(_silu(gate) * up, wo_ref[...])


def _ffn(st, x, mod, wts, li, k, mix=()):
    row_ctx, row_lat = li * MOD_ROWS, li * MOD_ROWS + 1
    w_in, w_out = wts["ffn_w_in"], wts["ffn_w_out"]
    pair_x = isinstance(x, tuple)
    pair_mix = bool(mix) and isinstance(mix[0], tuple)
    args = list(x) if pair_x else [x]
    in_specs = st.pair(D_MODEL) if pair_x else [st.merged(D_MODEL)]
    args += [mod, mod, mod, wts["norm_w"], w_in, w_out]
    in_specs += [st.mod(6 * k + j, row_ctx, row_lat) for j in range(3)]
    in_specs += [_layer_const(wts["norm_w"], 3 * li + 2 * k), _layer_const(w_in, 2 * li + k),
                 _layer_const(w_out, 2 * li + k)]
    if mix:
        args.append(mod)
        in_specs.append(st.mod(5, row_ctx, row_lat))
        for y in mix:
            width = (y[0] if pair_mix else y).shape[1]
            args += list(y) if pair_mix else [y]
            in_specs += st.pair(width) if pair_mix else [st.owned(width)]
        args.append(wts["mix_w_out"])
        in_specs.append(_layer_const(wts["mix_w_out"], li))
    return pl.pallas_call(
        functools.partial(_ffn_kernel, len(mix), pair_x, pair_mix, st.nct, st.first),
        grid=(st.count,),
        in_specs=in_specs,
        out_specs=st.owned(D_MODEL),
        out_shape=jax.ShapeDtypeStruct((st.tokens, D_MODEL), F32),
        compiler_params=_cparams("parallel"),
        name="ffn",
    )(*args)


_IN_SPLITS = (("z", D_SSM, BF16), ("xbc", SSM_CONV_CH, F32), ("hy", HY_IN, F32), ("ret", RET_IN, BF16),
              ("att", ATT_IN, BF16), ("dt", DT_PAD, F32))


C_DT = D_SSM + SSM_CONV_CH
N_DT = 2 * SSM_HEADS
W_TAIL = HY_IN + RET_IN + ATT_IN


def _inproj_kernel(x_ref, sh_ref, sc_ref, nw_ref, w_ref, wdt_ref, *rest):
    o_refs, wt_scr = rest[:-1], rest[-1]

    @pl.when(pl.program_id(0) == 0)
    def _():
        wt_scr[...] = w_ref[:, C_DT + N_DT:C_DT + N_DT + W_TAIL]

    h = (_rms(x_ref[...], nw_ref[...]) * (1.0 + sc_ref[...]) + sh_ref[...]).astype(BF16)
    off = 0
    for (name, width, dtype), o_ref in zip(_IN_SPLITS, o_refs):
        if name == "dt":
            w = wdt_ref[...]
        elif off < C_DT:
            w = w_ref[:, off:off + width]
        else:
            w = wt_scr[:, off - C_DT:off - C_DT + width]
        o_ref[...] = jnp.dot(h, w, preferred_element_type=F32).astype(dtype)
        off += width


def _inproj(st, x, mod, wts, li):
    row_ctx, row_lat = li * MOD_ROWS, li * MOD_ROWS + 1
    return pl.pallas_call(
        _inproj_kernel,
        grid=(st.count,),
        in_specs=[st.merged(D_MODEL), st.mod(3, row_ctx, row_lat), st.mod(4, row_ctx, row_lat),
                  _layer_const(wts["norm_w"], 3 * li + 1),
                  _layer_const(wts["mix_w_in"], li), _layer_const(wts["mix_w_dt"], li)],
        out_specs=[st.owned(width) for _, width, _ in _IN_SPLITS],
        out_shape=[jax.ShapeDtypeStruct((st.tokens, width), dtype) for _, width, dtype in _IN_SPLITS],
        scratch_shapes=[pltpu.VMEM((D_MODEL, W_TAIL), BF16)],
        compiler_params=_cparams("arbitrary"),
        name="mix_in",
    )(x, mod, mod, wts["norm_w"], wts["mix_w_in"], wts["mix_w_dt"])


def _conv3_chunk(x_ref, c, nc, w, b):
    q = CHUNK
    l = nc * q
    r0 = pl.multiple_of(c * q, q)
    x = x_ref[pl.ds(r0, q), :]
    prev = x_ref[pl.ds(jnp.maximum(r0 - 1, 0), 1), :]
    nxt = x_ref[pl.ds(jnp.minimum(r0 + q, l - 1), 1), :]
    prev = jnp.where(c > 0, prev, 0.0)
    nxt = jnp.where(c < nc - 1, nxt, 0.0)
    rid = lax.broadcasted_iota(jnp.int32, (q, 1), 0)
    xm1 = jnp.where(rid == 0, prev, pltpu.roll(x, 1, 0))
    xp1 = jnp.where(rid == q - 1, nxt, pltpu.roll(x, q - 1, 0))
    return xm1 * w[0:1, :] + x * w[1:2, :] + xp1 * w[2:3, :] + b


def _tri(lower):
    i = lax.broadcasted_iota(jnp.int32, (CHUNK, CHUNK), 0)
    j = lax.broadcasted_iota(jnp.int32, (CHUNK, CHUNK), 1)
    return (j <= i) if lower else (j >= i)


def _rows(c):
    return pl.ds(pl.multiple_of(c * CHUNK, CHUNK), CHUNK)


def _split_cat(v, parts, axis):
    out, r = [], v
    for i in range(parts):
        piece = r.astype(BF16)
        out.append(piece)
        if i + 1 < parts:
            r = r - piece.astype(F32)
    return jnp.concatenate(out, axis=axis)


def _seg_mean(x, seg):
    w = x.shape[-1]
    ones = _block_diag(w, seg, 1.0).astype(BF16)
    return jnp.dot(_split_cat(x, 2, axis=1), jnp.concatenate([ones, ones], axis=0),
                   preferred_element_type=F32) * (1.0 / seg)


def _stack_heads(x, masks):
    return jnp.concatenate([jnp.where(m, x, 0.0) for m in masks], axis=0)


def _slot_out(tail, nb, li, stacked, ng, whole):
    zeros = (0,) * len(tail)
    slot = li if stacked else 0
    if whole:
        spec = pl.BlockSpec((ng, DEPTH) + tail, lambda b: (b, 0) + zeros)
    else:
        spec = pl.BlockSpec((ng, None) + tail, lambda b: (b, slot) + zeros)
    return spec, jax.ShapeDtypeStruct((nb, DEPTH if stacked else 1) + tail, F32)


def _slot_view(ref, g, slot):
    if slot is None:
        return ref.at[g]
    for other in range(DEPTH):
        if other != slot:
            ref[g, other] = jnp.zeros(ref.shape[2:], ref.dtype)
    return ref.at[g, slot]


def _carry_prev(args, in_specs, prevs, first_out):
    aliases = {}
    for j, prev in enumerate(prevs):
        aliases[len(args)] = first_out + j
        args.append(prev)
        in_specs.append(pl.BlockSpec(memory_space=pl.ANY))
    return aliases


SSD_SEL_W = 2 * SSM_HEADS * SSM_STATE
CUM_PIECES = 3
WGT_PIECES = 2
DT_REP = CUM_PIECES + WGT_PIECES


def _ssd_kernel(nc, ng, has_s0, slot, *refs):
    refs = list(refs)
    (z_ref, xbc_ref, dt_ref, cw_ref, cb_ref, dtb_ref, alog_ref, dsk_ref, nw_ref) = refs[:9]
    s0_ref = refs[9] if has_s0 else None
    (y_ref, s_ref, xs_scr, xk_scr, xt_scr, dt_scr, cum_scr, dtt_scr, cumt_scr, yf_scr, yb_scr, st_scr,
     msk_scr, sel_scr) = refs[-14:]
    q = CHUNK
    nh, n, p = SSM_HEADS, SSM_STATE, SSM_HEAD_DIM
    rep = nh // SSM_GROUPS
    hm = _head_masks(D_SSM, nh)

    @pl.when(pl.program_id(0) == 0)
    def _():
        i = lax.broadcasted_iota(jnp.int32, (D_SSM, nh * n), 0) // p
        j = lax.broadcasted_iota(jnp.int32, (D_SSM, nh * n), 1) // n
        msk_scr[...] = jnp.where(i == j, 1.0, 0.0).astype(F32)
        k = lax.broadcasted_iota(jnp.int32, (LANES, SSD_SEL_W), 0)
        col = lax.broadcasted_iota(jnp.int32, (LANES, SSD_SEL_W), 1)
        is_cum = col < nh * n
        grp = k // (2 * nh)
        grp_ok = (is_cum & (grp < CUM_PIECES)) | ((~is_cum) & (grp >= CUM_PIECES) & (grp < DT_REP))
        hit = grp_ok & (k % nh == (col % (nh * n)) // n)
        sel_scr[...] = jnp.where(hit, 1.0, 0.0).astype(BF16)

    cw = cw_ref[...]
    cb = cb_ref[...]
    neg_a = -jnp.exp(alog_ref[...])
    dtb = dtb_ref[...]

    class Seq:
        def __init__(self, g):
            self.z, self.xbc, self.dt_in, self.y, self.s = (z_ref.at[g], xbc_ref.at[g], dt_ref.at[g], y_ref.at[g],
                                                            _slot_view(s_ref, g, slot))
            self.s0 = s0_ref.at[g] if has_s0 else None
            self.xs, self.xk, self.xt, self.dt, self.cum = (xs_scr.at[g], xk_scr.at[g], xt_scr.at[g], dt_scr.at[g],
                                                            cum_scr.at[g])
            self.dtt, self.cumt, self.yf, self.yb, self.st = (dtt_scr.at[g], cumt_scr.at[g], yf_scr.at[g],
                                                              yb_scr.at[g], st_scr.at[g])

    seqs = [Seq(g) for g in range(ng)]

    def prep_seq(sq, c):
        rows = _rows(c)
        xall = _silu(_conv3_chunk(sq.xbc, c, nc, cw, cb))
        sq.xs[rows, :] = xall
        xs = xall[:, 0:D_SSM]
        sq.xk[c] = _stack_heads(xs, hm).astype(BF16)
        sq.xt[c] = xs.T.astype(BF16)
        dt = _softplus(sq.dt_in[rows, :] + dtb)
        la = _split_cat(dt * neg_a, 3, axis=0)
        tri = jnp.concatenate([_tri(True), _tri(False)], axis=0)
        tri = jnp.where(tri, 1.0, 0.0).astype(BF16)
        cs = jnp.dot(jnp.concatenate([tri] * 3, axis=1), la, preferred_element_type=F32)
        lane = lax.broadcasted_iota(jnp.int32, (1, LANES), 1)
        cum = jnp.where(lane % (2 * nh) < nh, cs[0:q], cs[q:2 * q])
        sq.dt[rows, :] = dt
        sq.cum[rows, :] = cum
        sq.dtt[c] = dt.T
        sq.cumt[c] = cum.T

    def prep(c, carry):
        for sq in seqs:
            prep_seq(sq, c)
        return carry

    lax.fori_loop(0, nc, prep, 0)

    for sq in seqs:
        for d in range(2):
            for h in range(nh):
                blk = slice(h * p, (h + 1) * p)
                if has_s0:
                    parts = [jnp.zeros((p, n), F32)] * nh
                    parts[h] = sq.s0[d, h].T
                    sq.st[d, blk, :] = jnp.concatenate(parts, axis=1)
                else:
                    sq.st[d, blk, :] = jnp.zeros((p, nh * n), F32)

    def pack_scalars(sq, c, d):
        rows = _rows(c)
        dt = sq.dt[rows, :]
        cum = sq.cum[rows, :]
        edge = q - 1 if d == 0 else 0
        lane = lax.broadcasted_iota(jnp.int32, (1, LANES), 1)
        used = (lane < DT_REP * 2 * nh) & ((lane % (2 * nh)) // nh == d)
        cum = jnp.where(used, cum, 0.0)
        wgt = jnp.exp(cum[edge:edge + 1, :] - cum) * dt
        grp = lane // (2 * nh)
        packed = jnp.zeros_like(cum)
        for src, pieces, g0 in ((cum, CUM_PIECES, 0), (wgt, WGT_PIECES, CUM_PIECES)):
            rest = src
            for i in range(pieces):
                piece = rest.astype(BF16).astype(F32)
                packed = jnp.where(grp == g0 + i, piece, packed)
                rest = rest - piece
        return jnp.where(used, packed, 0.0).astype(BF16)

    def operands(sq, c, d):
        rows = _rows(c)
        bm = [sq.xs[rows, D_SSM + g * n:D_SSM + (g + 1) * n] for g in range(SSM_GROUPS)]
        cm = [sq.xs[rows, D_SSM + (SSM_GROUPS + g) * n:D_SSM + (SSM_GROUPS + g + 1) * n]
              for g in range(SSM_GROUPS)]
        return bm, cm

    def decay_blocks(sq, c, d, e, cb_t):
        mask = _tri(d == 0)
        blocks, ecol = [], []
        for h in range(nh):
            r = d * nh + h
            col = e[:, h * n:(h + 1) * n]
            decay = jnp.exp(jnp.where(mask, col - sq.cumt[c, r:r + 1, :], -jnp.inf))
            blocks.append((cb_t[h // rep] * decay * sq.dtt[c, r:r + 1, :]).astype(BF16))
            ecol.append(jnp.exp(col))
        lo = lax.broadcasted_iota(jnp.int32, (1, n), 1) < p
        ecol = jnp.concatenate([jnp.where(lo, ecol[2 * j], ecol[2 * j + 1]) for j in range(nh // 2)], axis=1)
        return jnp.concatenate(blocks, axis=1), ecol

    def scan(i, carry):
        j = nc - 1 - i
        jobs = [(sq, c, d) for sq in seqs for c, d in ((i, 0), (j, 1))]
        packed = jnp.concatenate([pack_scalars(*job) for job in jobs], axis=0)
        e_all = jnp.dot(packed, sel_scr[...], preferred_element_type=F32)
        es = [e_all[t * q:(t + 1) * q] for t in range(len(jobs))]
        ops = [operands(*job) for job in jobs]
        cb_t = [[_bdot_nt(cm[g], bm[g]) for g in range(SSM_GROUPS)] for bm, cm in ops]
        dec = [decay_blocks(*job, es[t], cb_t[t]) for t, job in enumerate(jobs)]
        intra = [jnp.dot(dec[t][0], sq.xk[c], preferred_element_type=F32) for t, (sq, c, d) in enumerate(jobs)]
        s_old = [sq.st[d] for sq, c, d in jobs]
        inter = [_bdot_nt(jnp.concatenate([cm[h // rep] for h in range(nh)], axis=1), s_old[t])
                 for t, (bm, cm) in enumerate(ops)]
        wk = [(jnp.concatenate([bm[h // rep] for h in range(nh)], axis=1) * es[t][:, nh * n:2 * nh * n]).astype(BF16)
              for t, (bm, cm) in enumerate(ops)]
        upd = [jnp.dot(sq.xt[c], wk[t], preferred_element_type=F32) for t, (sq, c, d) in enumerate(jobs)]
        for t, (sq, c, d) in enumerate(jobs):
            edge = q - 1 if d == 0 else 0
            sq.st[d] = jnp.exp(es[t][edge:edge + 1, 0:nh * n]) * s_old[t] + upd[t] * msk_scr[...]
            y = intra[t] + dec[t][1] * inter[t]
            if d == 0:
                sq.yf[_rows(c), :] = y
            else:
                sq.yb[_rows(c), :] = y
        return carry

    lax.fori_loop(0, nc, scan, 0)

    for sq in seqs:
        for d in range(2):
            for h in range(nh):
                sq.s[d, h] = sq.st[d, h * p:(h + 1) * p, h * n:(h + 1) * n].T

    dsk = dsk_ref[...]
    nw = nw_ref[...]

    def fin(c, carry):
        rows = _rows(c)
        for sq in seqs:
            y = sq.yf[rows, :] + sq.yb[rows, :] + dsk * sq.xs[rows, 0:D_SSM]
            sq.y[rows, :] = _rms(y * _silu(sq.z[rows, :].astype(F32)), nw)
        return carry

    lax.fori_loop(0, nc, fin, 0)


def _ssd(z, xbc, dt, wts, li, s0, nb, l, seq0, stacked=False, prev=None):
    nc = l // CHUNK
    has_s0 = s0 is not None
    names = ("ssd_conv_w", "ssd_conv_b", "ssd_dt_bias", "ssd_a_log", "ssd_d", "ssd_norm_w")
    ng = SEQ_GROUP
    args = [z.reshape(-1, l, D_SSM), xbc.reshape(-1, l, SSM_CONV_CH), dt.reshape(-1, l, DT_PAD)]
    args += [wts[n] for n in names]
    in_specs = [_seq_spec(l, D_SSM, ng, seq0), _seq_spec(l, SSM_CONV_CH, ng, seq0), _seq_spec(l, DT_PAD, ng, seq0)]
    in_specs += [_layer_const(wts[n], li) for n in names]
    if has_s0:
        args.append(s0)
        in_specs.append(pl.BlockSpec((ng, None, 2, SSM_HEADS, SSM_STATE, SSM_HEAD_DIM),
                                     lambda b: (b, li, 0, 0, 0, 0)))
    whole = stacked and prev is None
    st_spec, st_shape = _slot_out((2, SSM_HEADS, SSM_STATE, SSM_HEAD_DIM), nb, li, stacked, ng, whole)
    aliases = _carry_prev(args, in_specs, [] if prev is None else [prev], 1)
    y, s = pl.pallas_call(
        functools.partial(_ssd_kernel, nc, ng, has_s0, li if whole else None),
        grid=(nb // ng,),
        in_specs=in_specs,
        out_specs=[_seq_spec(l, D_SSM, ng), st_spec],
        out_shape=[jax.ShapeDtypeStruct((nb, l, D_SSM), F32), st_shape],
        input_output_aliases=aliases,
        scratch_shapes=[pltpu.VMEM((ng, l, SSM_CONV_CH), F32),
                        pltpu.VMEM((ng, nc, SSM_HEADS * CHUNK, D_SSM), BF16),
                        pltpu.VMEM((ng, nc, D_SSM, CHUNK), BF16),
                        pltpu.VMEM((ng, l, DT_PAD), F32), pltpu.VMEM((ng, l, DT_PAD), F32),
                        pltpu.VMEM((ng, nc, DT_PAD, CHUNK), F32), pltpu.VMEM((ng, nc, DT_PAD, CHUNK), F32),
                        pltpu.VMEM((ng, l, D_SSM), F32), pltpu.VMEM((ng, l, D_SSM), F32),
                        pltpu.VMEM((ng, 2, D_SSM, SSM_HEADS * SSM_STATE), F32),
                        pltpu.VMEM((D_SSM, SSM_HEADS * SSM_STATE), F32),
                        pltpu.VMEM((LANES, SSD_SEL_W), BF16)],
        compiler_params=_cparams("arbitrary"),
        name="ssd",
    )(*args)
    return y.reshape(nb * l, D_SSM), s


def _ret_kernel(nc, ng, has_s0, slot, *refs):
    refs = list(refs)
    u_ref, dl_ref, gn_ref = refs[:3]
    s0_ref = refs[3] if has_s0 else None
    y_ref, s_ref, yf_scr, yb_scr, st_scr, dm_scr, e_scr, ea_scr, bd_scr = refs[-9:]
    q = CHUNK
    hd = RET_HEAD_DIM
    hm = _head_masks(D_RET, RET_HEADS)

    @pl.when(pl.program_id(0) == 0)
    def _():
        log_g = -_softplus(-dl_ref[...])
        ii = lax.broadcasted_iota(jnp.int32, (q, q), 0)
        jj = lax.broadcasted_iota(jnp.int32, (q, q), 1)
        dij = (ii - jj).astype(F32)
        ri = lax.broadcasted_iota(jnp.int32, (q, 1), 0).astype(F32)
        lfs, lbs = [], []
        for h in range(RET_HEADS):
            lf = log_g[:, h:h + 1]
            lb = log_g[:, RET_HEADS + h:RET_HEADS + h + 1]
            lfs.append(lf)
            lbs.append(lb)
            d_f = jnp.exp(jnp.where(dij >= 0, dij * lf, -jnp.inf))
            d_b = jnp.exp(jnp.where(dij <= 0, -dij * lb, -jnp.inf))
            dm_scr[:, h * q:(h + 1) * q] = d_f + d_b
        lf_l = _by_head(hm, lfs)
        lb_l = _by_head(hm, lbs)
        e_scr[0] = jnp.exp((ri + 1.0) * lf_l)
        e_scr[1] = jnp.exp((q - ri) * lb_l)
        e_scr[2] = jnp.exp((q - 1.0 - ri) * lf_l)
        e_scr[3] = jnp.exp(ri * lb_l)
        ea_scr[0:1, :] = jnp.exp(q * lf_l)
        ea_scr[1:2, :] = jnp.exp(q * lb_l)
        bd_scr[...] = _block_diag(D_RET, hd, 1.0)

    class Seq:
        def __init__(self, g):
            self.u, self.y, self.s = u_ref.at[g], y_ref.at[g], _slot_view(s_ref, g, slot)
            self.s0 = s0_ref.at[g] if has_s0 else None
            self.yf, self.yb, self.st = yf_scr.at[g], yb_scr.at[g], st_scr.at[g]

    seqs = [Seq(g) for g in range(ng)]

    for sq in seqs:
        for d in range(2):
            for h in range(RET_HEADS):
                blk = slice(h * hd, (h + 1) * hd)
                if has_s0:
                    parts = [jnp.zeros((hd, hd), F32)] * RET_HEADS
                    parts[h] = sq.s0[d, h]
                    sq.st[d, blk, :] = jnp.concatenate(parts, axis=1)
                else:
                    sq.st[d, blk, :] = jnp.zeros((hd, D_RET), F32)

    def qkv(sq, rows):
        return (sq.u[rows, 0:D_RET], sq.u[rows, D_RET:2 * D_RET].astype(F32) * (RET_HEAD_DIM ** -0.5),
                sq.u[rows, 2 * D_RET:3 * D_RET])

    def state_step(sq, d, qq, kk, vv):
        s_old = sq.st[d]
        y = e_scr[d] * _bdot(qq, s_old)
        sq.st[d] = ea_scr[d:d + 1, :] * s_old + _bdot_tn(kk * e_scr[2 + d], vv) * bd_scr[...]
        return y

    def scan(i, carry):
        rows = _rows(i)
        rows_b = _rows(nc - 1 - i)
        fwd = [qkv(sq, rows) for sq in seqs]
        bwd = [qkv(sq, rows_b) for sq in seqs]
        sc = [_bdot_nt(qq, _stack_heads(kk, hm)) * dm_scr[...] for qq, kk, _ in fwd]
        intra = [_bdot(sc[g], _stack_heads(fwd[g][2], hm)) for g in range(ng)]
        inter = [state_step(sq, 0, *fwd[g]) for g, sq in enumerate(seqs)]
        back = [state_step(sq, 1, *bwd[g]) for g, sq in enumerate(seqs)]
        for g, sq in enumerate(seqs):
            sq.yf[rows, :] = intra[g] + inter[g]
            sq.yb[rows_b, :] = back[g]
        return carry

    lax.fori_loop(0, nc, scan, 0)

    for sq in seqs:
        for d in range(2):
            for h in range(RET_HEADS):
                sq.s[d, h] = sq.st[d, h * hd:(h + 1) * hd, h * hd:(h + 1) * hd]

    gn = gn_ref[...]

    def fin(c, carry):
        rows = _rows(c)
        for sq in seqs:
            y = sq.yf[rows, :] + sq.yb[rows, :]
            cen = y - _seg_mean(y, hd)
            var = _seg_mean(cen * cen, hd)
            sq.y[rows, :] = cen * lax.rsqrt(var + NORM_EPS) * gn * _silu(sq.u[rows, 3 * D_RET:4 * D_RET].astype(F32))
        return carry

    lax.fori_loop(0, nc, fin, 0)


def _retention(u, wts, li, s0, nb, l, seq0, stacked=False, prev=None):
    nc = l // CHUNK
    has_s0 = s0 is not None
    names = ("ret_decay_logit", "ret_gn_w")
    ng = SEQ_GROUP
    args = [u.reshape(-1, l, RET_IN)] + [wts[n] for n in names]
    in_specs = [_seq_spec(l, RET_IN, ng, seq0)] + [_layer_const(wts[n], li) for n in names]
    if has_s0:
        args.append(s0)
        in_specs.append(pl.BlockSpec((ng, None, 2, RET_HEADS, RET_HEAD_DIM, RET_HEAD_DIM),
                                     lambda b: (b, li, 0, 0, 0, 0)))
    whole = stacked and prev is None
    st_spec, st_shape = _slot_out((2, RET_HEADS, RET_HEAD_DIM, RET_HEAD_DIM), nb, li, stacked, ng, whole)
    aliases = _carry_prev(args, in_specs, [] if prev is None else [prev], 1)
    y, s = pl.pallas_call(
        functools.partial(_ret_kernel, nc, ng, has_s0, li if whole else None),
        grid=(nb // ng,),
        in_specs=in_specs,
        out_specs=[_seq_spec(l, D_RET, ng), st_spec],
        out_shape=[jax.ShapeDtypeStruct((nb, l, D_RET), F32), st_shape],
        input_output_aliases=aliases,
        scratch_shapes=[pltpu.VMEM((ng, l, D_RET), F32), pltpu.VMEM((ng, l, D_RET), F32),
                        pltpu.VMEM((ng, 2, D_RET, D_RET), F32), pltpu.VMEM((CHUNK, RET_HEADS * CHUNK), F32),
                        pltpu.VMEM((4, CHUNK, D_RET), F32), pltpu.VMEM((8, D_RET), F32),
                        pltpu.VMEM((D_RET, D_RET), F32)],
        compiler_params=_cparams("arbitrary"),
        name="retention",
    )(*args)
    return y.reshape(nb * l, D_RET), s


def _split(x):
    hi = x.astype(BF16)
    return hi, (x - hi.astype(F32)).astype(BF16)


def _dot3(a_hi, a_lo, b_hi, b_lo):
    d = lambda p, q: jnp.dot(p, q, preferred_element_type=F32)
    return d(a_hi, b_hi) + (d(a_lo, b_hi) + d(a_hi, b_lo))


@functools.lru_cache(maxsize=None)
def _dft_fwd_host(l):
    n = 2 * l
    f = np.arange(l, dtype=np.int64)[:, None]
    s = np.arange(l, dtype=np.int64)[None, :]
    ang = ((f * s) % n).astype(np.float64) * (2.0 * math.pi / n)
    im = -np.sin(ang)
    im[0] = np.where(np.arange(l) % 2 == 0, 1.0, -1.0)
    return np.concatenate([np.cos(ang), im], axis=0).astype(np.float32)


def _dft_tables(l):
    n = 2 * l
    fwd = jnp.asarray(_dft_fwd_host(l))
    wgt = np.full((n, 1), 2.0 / n, np.float32)
    wgt[0] = wgt[l] = 1.0 / n
    return _split(fwd) + _split((fwd * wgt).T)


def _dft_block_tables(tables, l):
    fwd_hi, fwd_lo, inv_hi, inv_lo = tables
    r = DFT_BLOCK
    nblk = l // r
    fb = lambda t: (t[0:l].reshape(nblk, r, l), t[l:2 * l].reshape(nblk, r, l))
    fwd_blk = jnp.concatenate(fb(fwd_hi) + fb(fwd_lo), axis=1)
    inv_blk = jnp.concatenate([inv_hi.reshape(nblk, r, 2 * l), inv_lo.reshape(nblk, r, 2 * l)], axis=1)
    return fwd_blk, inv_blk


def _hy_filter_kernel(l, feats_ref, dec_ref, w1_ref, b1_ref, w2_ref, b2_ref, w3_ref, fr_ref, fh_ref, fl_ref,
                      a_ref, b_ref, d_ref):
    fr = fr_ref[...]
    xdot = lambda a, b: _dot3(*_split(a), *_split(b))
    h = jnp.sin(fr * (xdot(feats_ref[...], w1_ref[...]) + b1_ref[...]))
    h = jnp.sin(fr * (xdot(h, w2_ref[...]) + b2_ref[...]))
    h = xdot(h, w3_ref[...])
    dec = jnp.concatenate([dec_ref[...]] * HY_ORDER, axis=-1)
    row0 = lax.broadcasted_iota(jnp.int32, (l, 1), 0) == 0
    hf = h[:, 0:HY_ORDER * D_HY] * dec
    hb = h[:, HY_ORDER * D_HY:2 * HY_ORDER * D_HY] * dec
    hb = jnp.where(row0, 0.0, hb)
    hs = _split(hf + hb)
    hd = _split(hf - hb)
    re = _dot3(fh_ref[0:l, :], fl_ref[0:l, :], *hs)
    ny = _dot3(fh_ref[l:l + 8, :], fl_ref[l:l + 8, :], *hs)[0:1]
    im = _dot3(fh_ref[l:2 * l, :], fl_ref[l:2 * l, :], *hd)
    for o in range(HY_ORDER):
        cols = slice(o * D_HY, (o + 1) * D_HY)
        a_ref[o] = re[:, cols]
        b_ref[o] = jnp.where(row0, 0.0, im[:, cols])
        d_ref[o] = jnp.where(row0, ny[:, cols], re[:, cols])


def _hy_filter(l, fwd_hi, fwd_lo, wts, li):
    pos = np.arange(l, dtype=np.float32)
    t = pos / np.float32(l - 1)
    bands = np.linspace(1e-4, HY_BANDS - 1, HY_BANDS, dtype=np.float32)
    ang = np.float32(2.0 * math.pi / l) * pos[:, None] * bands[None, :]
    feats = np.concatenate([t[:, None], np.cos(ang), -np.sin(ang)], axis=-1).astype(np.float32)
    feats = np.pad(feats, ((0, 0), (0, LANES - HY_EMB)))
    max_decay = math.log(HY_TARGET) / HY_FAST_DECAY
    min_decay = math.log(HY_TARGET) / HY_SLOW_DECAY
    deltas = np.abs(np.linspace(min_decay, max_decay, D_HY, dtype=np.float32))
    dec = np.exp(-t[:, None] * deltas[None, :]).astype(np.float32)
    spec = jax.ShapeDtypeStruct((HY_ORDER, l, D_HY), F32)
    names = ("hy_w1", "hy_b1", "hy_w2", "hy_b2", "hy_w3", "hy_freq")
    return pl.pallas_call(
        functools.partial(_hy_filter_kernel, l),
        grid=(1,),
        in_specs=[_full(feats.shape), _full(dec.shape)] + [_layer_const(wts[n], li) for n in names]
                 + [_full(fwd_hi.shape), _full(fwd_lo.shape)],
        out_specs=[_full(spec.shape)] * 3,
        out_shape=[spec] * 3,
        compiler_params=_cparams("arbitrary"),
        name="hyena_filter",
    )(jnp.asarray(feats), jnp.asarray(dec), *[wts[n] for n in names], fwd_hi, fwd_lo)


def _hy_kernel(nc, ng, u_ref, cw_ref, cb_ref, f_ref, g_ref, a_ref, b_ref, d_ref, bias_ref, y_ref,
               uc_scr, vh_scr, vl_scr, sh_scr, sl_scr, z_scr):
    l = nc * CHUNK
    r = DFT_BLOCK
    nblk = l // r
    cw = cw_ref[...]
    cb = cb_ref[...]
    dot = lambda p, q: jnp.dot(p, q, preferred_element_type=F32)
    wide = lambda x: jnp.concatenate([x] * ng, axis=1)
    seq_cols = lambda g: slice(g * D_HY, (g + 1) * D_HY)

    def for_blocks(body):
        if nblk == 1:
            body(0)
        else:
            lax.fori_loop(0, nblk, lambda i, carry: (body(i), carry)[1], 0)

    def conv(c, carry):
        rows = _rows(c)
        for g in range(ng):
            uc = _conv3_chunk(u_ref.at[g], c, nc, cw, cb)
            uc_scr[g, rows, :] = uc
            vh_scr[rows, seq_cols(g)], vl_scr[rows, seq_cols(g)] = _split(uc[:, 0:D_HY])
        return carry

    lax.fori_loop(0, nc, conv, 0)

    def long_conv(o):
        def spectrum(i):
            rows = pl.ds(pl.multiple_of(i * r, r), r)
            rows_im = pl.ds(pl.multiple_of(l + i * r, r), r)
            p = dot(f_ref[i], vh_scr[...])
            pl_ = dot(f_ref[i, 0:2 * r, :], vl_scr[...])
            zr = p[0:r] + (p[2 * r:3 * r] + pl_[0:r])
            zi = p[r:2 * r] + (p[3 * r:4 * r] + pl_[r:2 * r])
            fa, fb, fd = wide(a_ref[o, rows, :]), wide(b_ref[o, rows, :]), wide(d_ref[o, rows, :])
            sh_scr[rows, :], sl_scr[rows, :] = _split(zr * fa - zi * fb)
            sh_scr[rows_im, :], sl_scr[rows_im, :] = _split(zr * fb + zi * fd)

        for_blocks(spectrum)

        def inverse(i):
            rows = pl.ds(pl.multiple_of(i * r, r), r)
            p = dot(g_ref[i], sh_scr[...])
            y = p[0:r] + (p[r:2 * r] + dot(g_ref[i, 0:r, :], sl_scr[...]))
            gate = jnp.concatenate([uc_scr[g, rows, (o + 1) * D_HY:(o + 2) * D_HY] for g in range(ng)], axis=1)
            if o == 0:
                v = jnp.concatenate([uc_scr[g, rows, 0:D_HY] for g in range(ng)], axis=1)
            else:
                v = z_scr[rows, :]
            out = gate * (y + v * wide(bias_ref[o:o + 1, :]))
            if o + 1 < HY_ORDER:
                z_scr[rows, :] = out
                vh_scr[rows, :], vl_scr[rows, :] = _split(out)
            else:
                for g in range(ng):
                    y_ref[g, rows, :] = out[:, seq_cols(g)]

        for_blocks(inverse)

    for o in range(HY_ORDER):
        long_conv(o)


def _hyena(u, wts, li, tables, spectra, nb, l, seq0):
    nc = l // CHUNK
    ng = min(HY_GROUP, nb)
    a, b, d = spectra
    names = ("hy_conv_w", "hy_conv_b")
    y = pl.pallas_call(
        functools.partial(_hy_kernel, nc, ng),
        grid=(nb // ng,),
        in_specs=[_seq_spec(l, HY_IN, ng, seq0)] + [_layer_const(wts[n], li) for n in names]
                 + [_const(tables[0].shape), _const(tables[1].shape),
                    _const((HY_ORDER, l, D_HY)), _const((HY_ORDER, l, D_HY)), _const((HY_ORDER, l, D_HY)),
                    _layer_const(wts["hy_bias"], li)],
        out_specs=pl.BlockSpec((ng, l, D_HY), lambda i: (i, 0, 0)),
        out_shape=jax.ShapeDtypeStruct((nb, l, D_HY), F32),
        scratch_shapes=[pltpu.VMEM((ng, l, HY_IN), F32),
                        pltpu.VMEM((l, ng * D_HY), BF16), pltpu.VMEM((l, ng * D_HY), BF16),
                        pltpu.VMEM((2 * l, ng * D_HY), BF16), pltpu.VMEM((2 * l, ng * D_HY), BF16),
                        pltpu.VMEM((l, ng * D_HY), F32)],
        compiler_params=_cparams("parallel"),
        name="hyena",
    )(u.reshape(-1, l, HY_IN), *[wts[n] for n in names], *tables, a, b, d, wts["hy_bias"])
    return y.reshape(nb * l, D_HY)


def _seg_rms(x, w):
    return x * lax.rsqrt(_seg_mean(x * x, HEAD_DIM) + NORM_EPS) * w


def _stack_q(q):
    lo = lax.broadcasted_iota(jnp.int32, (1, D_KV), 1) < HEAD_DIM
    qa = q[:, 0:D_KV]
    qb = q[:, D_KV:2 * D_KV]
    return jnp.concatenate([jnp.where(lo, qa, 0.0), jnp.where(lo, pltpu.roll(qa, HEAD_DIM, 1), 0.0),
                            jnp.where(lo, 0.0, pltpu.roll(qb, HEAD_DIM, 1)), jnp.where(lo, 0.0, qb)], axis=0)


def _unstack_o(o):
    r = o.shape[0] // ATT_HEADS
    lo = lax.broadcasted_iota(jnp.int32, (1, D_KV), 1) < HEAD_DIM
    ya = jnp.where(lo, o[0:r], pltpu.roll(o[r:2 * r], HEAD_DIM, 1))
    yb = jnp.where(lo, pltpu.roll(o[2 * r:3 * r], HEAD_DIM, 1), o[3 * r:4 * r])
    return jnp.concatenate([ya, yb], axis=1)


def _sink_col(sink_ref, li, r):
    rb = lax.broadcasted_iota(jnp.int32, (ATT_HEADS * r, 1), 0) // r
    col = jnp.full((ATT_HEADS * r, 1), sink_ref[li * ATT_HEADS + ATT_HEADS - 1], F32)
    for h in range(ATT_HEADS - 2, -1, -1):
        col = jnp.where(rb == h, sink_ref[li * ATT_HEADS + h], col)
    return col


def _ctx_attn_kernel(li, ng, slot, u_ref, qn_ref, kn_ref, sink_ref, *rest):
    y_ref, k_ref, v_ref = rest[-3:]
    l = u_ref.shape[1]
    sink = _sink_col(sink_ref, li, l)
    seqs = range(ng)
    u = [u_ref[g].astype(F32) for g in seqs]
    q = [_seg_rms(u[g][:, 0:D_ATT], qn_ref[...]) for g in seqs]
    k = [_seg_rms(u[g][:, D_ATT:D_ATT + D_KV], kn_ref[:, 0:D_KV]) for g in seqs]
    v = [u[g][:, D_ATT + D_KV:D_ATT + 2 * D_KV] for g in seqs]
    for g in seqs:
        _slot_view(k_ref, g, slot)[...] = k[g]
        _slot_view(v_ref, g, slot)[...] = v[g]
    s = [_bdot_nt(_stack_q(q[g]), k[g]) * (HEAD_DIM ** -0.5) for g in seqs]
    m = [jnp.maximum(jnp.max(s[g], axis=-1, keepdims=True), sink) for g in seqs]
    p = [jnp.exp(s[g] - m[g]) for g in seqs]
    den = [jnp.sum(p[g], axis=-1, keepdims=True) + jnp.exp(sink - m[g]) for g in seqs]
    o = [_bdot(p[g] * (1.0 / den[g]), v[g]) for g in seqs]
    for g in seqs:
        y_ref[g] = _unstack_o(o[g])


def _ctx_attention(u, wts, li, nb, l, seq0, prev=None):
    ng = ATT_GROUP
    whole = prev is None
    kv_spec, kv_shape = _slot_out((l, D_KV), nb, li, True, ng, whole)
    args = [u.reshape(-1, l, ATT_IN), wts["attn_q_norm"], wts["attn_k_norm"], wts["attn_sink"]]
    in_specs = [_seq_spec(l, ATT_IN, ng, seq0), _layer_const(wts["attn_q_norm"], li),
                _layer_const(wts["attn_k_norm"], li), pl.BlockSpec(memory_space=pltpu.SMEM)]
    aliases = _carry_prev(args, in_specs, [] if prev is None else list(prev), 1)
    y, k, v = pl.pallas_call(
        functools.partial(_ctx_attn_kernel, li, ng, li if whole else None),
        grid=(nb // ng,),
        in_specs=in_specs,
        out_specs=[_seq_spec(l, D_ATT, ng), kv_spec, kv_spec],
        out_shape=[jax.ShapeDtypeStruct((nb, l, D_ATT), F32), kv_shape, kv_shape],
        input_output_aliases=aliases,
        compiler_params=_cparams("parallel"),
        name="ctx_attention",
    )(*args)
    return y.reshape(nb * l, D_ATT), k, v


@functools.lru_cache(maxsize=None)
def _rope_tables_host(l):
    n_rows = l // GRID_W
    rows = np.repeat(np.arange(n_rows, dtype=np.float32), GRID_W)
    cols = np.tile(np.arange(GRID_W, dtype=np.float32), n_rows)
    nf = HEAD_DIM // 4
    inv = (np.float32(ROPE_BASE) ** (-np.arange(nf, dtype=np.float32) / np.float32(nf))).astype(np.float32)
    ar = rows[:, None] * inv[None, :]
    ac = cols[:, None] * inv[None, :]
    cos = np.concatenate([np.cos(ar), np.cos(ar), np.cos(ac), np.cos(ac)], axis=-1)
    sin = np.concatenate([-np.sin(ar), np.sin(ar), -np.sin(ac), np.sin(ac)], axis=-1)
    return (np.tile(cos, (1, ATT_HEADS)).astype(np.float32), np.tile(sin, (1, ATT_HEADS)).astype(np.float32))


def _rope(x, cos, sin):
    w = x.shape[-1]
    nf = HEAD_DIM // 4
    lane = lax.broadcasted_iota(jnp.int32, x.shape, 1)
    first = (lane % (2 * nf)) < nf
    partner = jnp.where(first, pltpu.roll(x, w - nf, 1), pltpu.roll(x, nf, 1))
    return x * cos + partner * sin


def _lat_attn_kernel(nblk, li, ng, u_ref, qn_ref, kn_ref, cos_ref, sin_ref, ck_ref, cv_ref, sink_ref,
                     y_ref, q_scr, k_scr, v_scr):
    blk = ATT_BLOCK
    l = nblk * blk
    seqs = range(ng)
    cos_q = cos_ref[...]
    sin_q = sin_ref[...]
    zeros = jnp.zeros((blk, D_KV), F32)
    for g in seqs:
        u = u_ref[g].astype(F32)
        q = _seg_rms(u[:, 0:D_ATT], qn_ref[...])
        k = _seg_rms(u[:, D_ATT:D_ATT + D_KV], kn_ref[:, 0:D_KV])
        q_scr[g] = _rope(q, cos_q, sin_q)
        k_scr[g, 0:blk, :] = zeros
        k_scr[g, blk + l:2 * blk + l, :] = zeros
        v_scr[g, 0:blk, :] = zeros
        v_scr[g, blk + l:2 * blk + l, :] = zeros
        k_scr[g, blk:blk + l, :] = _rope(k, cos_q[:, 0:D_KV], sin_q[:, 0:D_KV])
        v_scr[g, blk:blk + l, :] = u[:, D_ATT + D_KV:D_ATT + 2 * D_KV]

    scale = HEAD_DIM ** -0.5
    r = lax.broadcasted_iota(jnp.int32, (blk, 3 * blk), 0)
    cidx = lax.broadcasted_iota(jnp.int32, (blk, 3 * blk), 1)
    band = (cidx - r >= blk - WINDOW) & (cidx - r <= blk + WINDOW)
    sink = _sink_col(sink_ref, li, blk)

    def block(i, carry):
        rows = _rows(i)
        win = pl.ds(pl.multiple_of(i * blk, blk), 3 * blk)
        kpos = cidx + (i - 1) * blk
        valid = band & (kpos >= 0) & (kpos < l)
        valid = jnp.concatenate([valid] * ATT_HEADS, axis=0)
        qs = [_stack_q(q_scr[g, rows, :]) for g in seqs]
        s_loc = [jnp.where(valid, _bdot_nt(qs[g], k_scr[g, win, :]) * scale, -jnp.inf) for g in seqs]
        s_ctx = [_bdot_nt(qs[g], ck_ref[g]) * scale for g in seqs]
        m = [jnp.maximum(jnp.maximum(jnp.max(s_loc[g], axis=-1, keepdims=True),
                                     jnp.max(s_ctx[g], axis=-1, keepdims=True)), sink) for g in seqs]
        p_loc = [jnp.exp(s_loc[g] - m[g]) for g in seqs]
        p_ctx = [jnp.exp(s_ctx[g] - m[g]) for g in seqs]
        inv = [1.0 / (jnp.sum(p_loc[g], axis=-1, keepdims=True) + jnp.sum(p_ctx[g], axis=-1, keepdims=True)
                      + jnp.exp(sink - m[g])) for g in seqs]
        o = [_bdot(p_ctx[g] * inv[g], cv_ref[g]) + _bdot(p_loc[g] * inv[g], v_scr[g, win, :]) for g in seqs]
        for g in seqs:
            y_ref[g, rows, :] = _unstack_o(o[g])
        return carry

    lax.fori_loop(0, nblk, block, 0)


def _lat_attention(u, wts, li, ck, cv, nb, l, seq0):
    lc = ck.shape[2]
    nblk = l // ATT_BLOCK
    ng = ATT_GROUP
    cos, sin = _rope_tables_host(l)
    cache_spec = pl.BlockSpec((ng, None, lc, D_KV), lambda b: (b, li, 0, 0))
    y = pl.pallas_call(
        functools.partial(_lat_attn_kernel, nblk, li, ng),
        grid=(nb // ng,),
        in_specs=[_seq_spec(l, ATT_IN, ng, seq0), _layer_const(wts["attn_q_norm"], li),
                  _layer_const(wts["attn_k_norm"], li), _const((l, D_ATT)), _const((l, D_ATT)),
                  cache_spec, cache_spec, pl.BlockSpec(memory_space=pltpu.SMEM)],
        out_specs=_seq_spec(l, D_ATT, ng),
        out_shape=jax.ShapeDtypeStruct((nb, l, D_ATT), F32),
        scratch_shapes=[pltpu.VMEM((ng, l, D_ATT), F32), pltpu.VMEM((ng, l + 2 * ATT_BLOCK, D_KV), F32),
                        pltpu.VMEM((ng, l + 2 * ATT_BLOCK, D_KV), F32)],
        compiler_params=_cparams("parallel"),
        name="lat_attention",
    )(u.reshape(-1, l, ATT_IN), wts["attn_q_norm"], wts["attn_k_norm"], jnp.asarray(cos), jnp.asarray(sin),
      ck, cv, wts["attn_sink"])
    return y.reshape(nb * l, D_ATT)


def _pad_lane_tile(w):
    return jnp.pad(w, [(0, 0)] * (w.ndim - 1) + [(0, -w.shape[-1] % LANES)])


def _dt_weight(w):
    zeros = jnp.zeros(w.shape[:-1] + (DT_PAD - DT_REP * N_DT,), w.dtype)
    return jnp.concatenate([w[..., C_DT:C_DT + N_DT]] * DT_REP + [zeros], axis=-1).astype(BF16)


def _prep_weights(p):
    row = lambda a: a.reshape(DEPTH, 1, -1)
    pad_lanes = lambda a: jnp.pad(row(a), ((0, 0), (0, 0), (0, LANES - a[0].size)))
    return dict(
        norm_w=p["norm_w"].reshape(DEPTH * 3, 1, D_MODEL),
        ffn_w_in=p["ffn_w_in"].astype(BF16).reshape(DEPTH * 2, D_MODEL, 2 * D_FF),
        ffn_w_out=p["ffn_w_out"].astype(BF16).reshape(DEPTH * 2, D_FF, D_MODEL),
        mix_w_in=_pad_lane_tile(p["mix_w_in"]).astype(BF16), mix_w_dt=_dt_weight(p["mix_w_in"]),
        mix_w_out=p["mix_w_out"].astype(BF16),
        ssd_conv_w=p["ssd_conv_w"], ssd_conv_b=row(p["ssd_conv_b"]),
        ssd_dt_bias=pad_lanes(jnp.tile(row(p["ssd_dt_bias"]), (1, 1, DT_REP))),
        ssd_a_log=pad_lanes(jnp.tile(row(p["ssd_a_log"]), (1, 1, DT_REP))),
        ssd_d=row(jnp.repeat(p["ssd_d"], SSM_HEAD_DIM, axis=-1)), ssd_norm_w=row(p["ssd_norm_w"]),
        hy_conv_w=p["hy_conv_w"], hy_conv_b=row(p["hy_conv_b"]), hy_bias=p["hy_bias"],
        hy_w1=jnp.pad(p["hy_w1"], ((0, 0), (0, LANES - HY_EMB), (0, 0))), hy_b1=row(p["hy_b1"]),
        hy_w2=p["hy_w2"], hy_b2=row(p["hy_b2"]), hy_w3=p["hy_w3"], hy_freq=row(p["hy_freq"]),
        ret_decay_logit=pad_lanes(p["ret_decay_logit"]), ret_gn_w=row(p["ret_gn_w"]),
        attn_q_norm=row(jnp.tile(p["attn_q_norm"], (1, ATT_HEADS))),
        attn_k_norm=row(jnp.tile(p["attn_k_norm"], (1, ATT_HEADS))),
        attn_sink=p["attn_sink"].reshape(DEPTH * ATT_HEADS),
    )


def _mixers(u, wts, li, nb, l, seq0, ssd_s0, ret_s0, ctx_kv, hy_tables, hy_spectra, carried):
    z, xbc, hy, ret, att, dt = u
    ctx = ctx_kv is None
    y_ssd, s_ssd = _ssd(z, xbc, dt, wts, li, ssd_s0, nb, l, seq0, stacked=ctx, prev=carried.get("ssd"))
    y_hy = _hyena(hy, wts, li, hy_tables, hy_spectra, nb, l, seq0)
    y_ret, s_ret = _retention(ret, wts, li, ret_s0, nb, l, seq0, stacked=ctx, prev=carried.get("ret"))
    if ctx:
        y_att, k, v = _ctx_attention(att, wts, li, nb, l, seq0, prev=carried.get("kv"))
        carried = dict(ssd=s_ssd, ret=s_ret, kv=(k, v))
    else:
        y_att = _lat_attention(att, wts, li, ctx_kv[0], ctx_kv[1], nb, l, seq0)
    return (y_ssd, y_hy, y_ret, y_att), carried


def kernel(x_prompt, x_sample, cache_k, cache_v, state_ssd, state_ret, c, c_ctx, w_mod, b_mod, norm_w, ffn_w_in, ffn_w_out, mix_w_in, mix_w_out, ssd_conv_w, ssd_conv_b, ssd_dt_bias, ssd_a_log, ssd_d, ssd_norm_w, hy_conv_w, hy_conv_b, hy_w1, hy_b1, hy_w2, hy_b2, hy_w3, hy_freq, hy_bias, ret_decay_logit, ret_gn_w, attn_q_norm, attn_k_norm, attn_sink):
    bp, lp_len, _ = x_prompt.shape
    bs, ls_len, _ = x_sample.shape
    lc = cache_k.shape[2]

    cond = jnp.concatenate([c_ctx[None, :], c, jnp.zeros((MOD_ROWS - 1 - bs, D_MODEL), F32)], axis=0)
    mod = _modulation(cond, w_mod, b_mod)
    wts = _prep_weights(dict(
        norm_w=norm_w, ffn_w_in=ffn_w_in, ffn_w_out=ffn_w_out, mix_w_in=mix_w_in, mix_w_out=mix_w_out,
        ssd_conv_w=ssd_conv_w, ssd_conv_b=ssd_conv_b, ssd_dt_bias=ssd_dt_bias, ssd_a_log=ssd_a_log, ssd_d=ssd_d,
        ssd_norm_w=ssd_norm_w, hy_conv_w=hy_conv_w, hy_conv_b=hy_conv_b, hy_w1=hy_w1, hy_b1=hy_b1, hy_w2=hy_w2,
        hy_b2=hy_b2, hy_w3=hy_w3, hy_freq=hy_freq, hy_bias=hy_bias, ret_decay_logit=ret_decay_logit,
        ret_gn_w=ret_gn_w, attn_q_norm=attn_q_norm, attn_k_norm=attn_k_norm, attn_sink=attn_sink))
    ck = cache_k.reshape(bs, DEPTH, lc, D_KV)
    cv = cache_v.reshape(bs, DEPTH, lc, D_KV)

    tab_p = _dft_tables(lp_len)
    tab_s = _dft_tables(ls_len)
    blk_p = _dft_block_tables(tab_p, lp_len)
    blk_s = _dft_block_tables(tab_s, ls_len)

    tm = TOKEN_TILE
    tp, ts = bp * lp_len, bs * ls_len
    assert tp % tm == 0 and ts % tm == 0 and ls_len % tm == 0 and tp % ls_len == 0
    st = _Stream(tm, tp // tm, ts // tm, ls_len)
    x = (x_prompt.reshape(tp, D_MODEL), x_sample.reshape(ts, D_MODEL))
    carried = {}
    for li in range(DEPTH):
        spec_p = _hy_filter(lp_len, tab_p[0], tab_p[1], wts, li)
        spec_s = _hy_filter(ls_len, tab_s[0], tab_s[1], wts, li)
        x = _ffn(st, x, mod, wts, li, 0)
        u = _inproj(st, x, mod, wts, li)
        y_p, carried = _mixers(u, wts, li, bp, lp_len, 0, None, None, None, blk_p, spec_p, carried)
        y_s, _ = _mixers(u, wts, li, bs, ls_len, tp // ls_len, state_ssd, state_ret, (ck, cv), blk_s, spec_s, {})
        if li + 1 < DEPTH:
            x = _ffn(st, x, mod, wts, li, 1, mix=tuple(zip(y_p, y_s)))
        else:
            yp = _ffn(st.part(0), x, mod, wts, li, 1, mix=y_p)
            ys = _ffn(st.part(1), x, mod, wts, li, 1, mix=y_s)

    kv_shape = (bp, DEPTH, lp_len, ATT_KV_HEADS, HEAD_DIM)
    new_k, new_v = carried["kv"]
    return (yp.reshape(bp, lp_len, D_MODEL), ys.reshape(bs, ls_len, D_MODEL),
            new_k.reshape(kv_shape), new_v.reshape(kv_shape), carried["ssd"], carried["ret"])
```

```python
import functools
import math

import numpy as np
import jax
import jax.numpy as jnp
from jax import lax
from jax.experimental import pallas as pl
from jax.experimental.pallas import tpu as pltpu

F32 = jnp.float32
BF16 = jnp.bfloat16
HI = lax.Precision.HIGHEST

D_MODEL = 1024
DEPTH = 2
GRID_W = 64
D_FF = 2816
N_MOD = 9
NORM_EPS = 1e-6
CHUNK = 128
D_SSM = 256
SSM_HEADS = 4
SSM_HEAD_DIM = 64
SSM_STATE = 128
SSM_GROUPS = 2
SSM_CONV_CH = D_SSM + 2 * SSM_GROUPS * SSM_STATE
D_HY = 256
HY_ORDER = 2
HY_BANDS = 16
HY_EMB = 1 + 2 * HY_BANDS
HY_HIDDEN = 64
HY_FAST_DECAY = 0.3
HY_SLOW_DECAY = 1.5
HY_TARGET = 1e-2
HY_IN = (HY_ORDER + 1) * D_HY
D_RET = 256
RET_HEADS = 4
RET_HEAD_DIM = 64
RET_IN = 4 * D_RET
ATT_HEADS = 4
ATT_KV_HEADS = 2
HEAD_DIM = 64
D_ATT = ATT_HEADS * HEAD_DIM
D_KV = ATT_KV_HEADS * HEAD_DIM
ATT_IN = D_ATT + 2 * D_KV
WINDOW = 128
ATT_BLOCK = 128
ROPE_BASE = 10000.0
D_MIX = D_SSM + D_HY + D_RET + D_ATT

LANES = 128
DT_PAD = LANES
VMEM_LIMIT = 56 * 1024 * 1024
MOD_ROWS = 8

TOKEN_TILE = 512
DFT_BLOCK = 256
HY_GROUP = 4
SEQ_GROUP = 2
ATT_GROUP = 2


def _cparams(*sem):
    return pltpu.CompilerParams(dimension_semantics=sem, vmem_limit_bytes=VMEM_LIMIT)


def _rms(x, w):
    return x * lax.rsqrt(jnp.mean(x * x, axis=-1, keepdims=True) + NORM_EPS) * w


def _silu(x):
    return x * (1.0 / (1.0 + jnp.exp(-x)))


def _softplus(x):
    return jnp.maximum(x, 0.0) + jnp.log1p(jnp.exp(-jnp.abs(x)))


def _bdot(a, b):
    return jnp.dot(a.astype(BF16), b.astype(BF16), preferred_element_type=F32)


def _bdot_nt(a, b):
    return lax.dot_general(a.astype(BF16), b.astype(BF16), (((1,), (1,)), ((), ())),
                           preferred_element_type=F32)


def _bdot_tn(a, b):
    return lax.dot_general(a.astype(BF16), b.astype(BF16), (((0,), (0,)), ((), ())),
                           preferred_element_type=F32)


def _hdot(a, b):
    return jnp.dot(a, b, preferred_element_type=F32, precision=HI)


def _full(shape):
    n = len(shape)
    return pl.BlockSpec(shape, lambda *_: (0,) * n)


def _const(shape):
    n = len(shape)
    return pl.BlockSpec(shape, lambda *_: (0,) * n, pipeline_mode=pl.Buffered(1))


def _layer_const(arr, li):
    tail = arr.shape[1:]
    zeros = (0,) * len(tail)
    return pl.BlockSpec((None,) + tail, lambda *_: (li,) + zeros, pipeline_mode=pl.Buffered(1))


class _Stream:
    def __init__(self, tm, nct, nlt, l_lat, first=0, count=None):
        self.tm, self.nct, self.nlt, self.l_lat = tm, nct, nlt, l_lat
        self.first = first
        self.count = nct + nlt if count is None else count

    def part(self, path):
        first, count = (0, self.nct) if path == 0 else (self.nct, self.nlt)
        return _Stream(self.tm, self.nct, self.nlt, self.l_lat, first, count)

    @property
    def tokens(self):
        return self.count * self.tm

    def merged(self, width):
        return pl.BlockSpec((self.tm, width), lambda i: (i + self.first, 0))

    def owned(self, width):
        return pl.BlockSpec((self.tm, width), lambda i: (i, 0))

    def pair(self, width):
        ctx = pl.BlockSpec((self.tm, width), lambda i: (jnp.minimum(i + self.first, self.nct - 1), 0))
        lat = pl.BlockSpec((self.tm, width), lambda i: (jnp.maximum(i + self.first - self.nct, 0), 0))
        return [ctx, lat]

    def mod(self, k, row_ctx, row_lat):
        def index(i):
            t = i + self.first
            lat_row = row_lat + (jnp.maximum(t - self.nct, 0) * self.tm) // self.l_lat
            return (jnp.where(t < self.nct, row_ctx, lat_row), 0, k)
        return pl.BlockSpec((None, 1, D_MODEL), index)


def _seq_spec(l, w, ng=None, seq0=0):
    step0 = seq0 // (ng or 1)
    return pl.BlockSpec((ng, l, w), lambda b: (b + step0, 0, 0))


def _head_masks(width, heads):
    lane = lax.broadcasted_iota(jnp.int32, (1, width), 1)
    hd = width // heads
    return [(lane >= h * hd) & (lane < (h + 1) * hd) for h in range(heads)]


def _by_head(masks, vals):
    out = vals[-1]
    for m, v in zip(masks[-2::-1], vals[-2::-1]):
        out = jnp.where(m, v, out)
    return out


def _block_diag(n, blk, value):
    i = lax.broadcasted_iota(jnp.int32, (n, n), 0) // blk
    j = lax.broadcasted_iota(jnp.int32, (n, n), 1) // blk
    return jnp.where(i == j, value, 0.0).astype(F32)


def _mod_kernel(c_ref, w_ref, b_ref, o_ref):
    c = c_ref[...]
    o_ref[...] = _bdot(_silu(c), w_ref[...]) + b_ref[...]


def _modulation(cond, w_mod, b_mod):
    out = pl.pallas_call(
        _mod_kernel,
        grid=(DEPTH, N_MOD),
        in_specs=[pl.BlockSpec((MOD_ROWS, D_MODEL), lambda l, j: (0, 0)),
                  pl.BlockSpec((None, D_MODEL, D_MODEL), lambda l, j: (l, 0, j)),
                  pl.BlockSpec((None, 1, D_MODEL), lambda l, j: (l, 0, j))],
        out_specs=pl.BlockSpec((None, MOD_ROWS, D_MODEL), lambda l, j: (l, 0, j)),
        out_shape=jax.ShapeDtypeStruct((DEPTH, MOD_ROWS, N_MOD * D_MODEL), F32),
        compiler_params=_cparams("arbitrary", "arbitrary"),
        name="modulation",
    )(cond, w_mod, b_mod.reshape(DEPTH, 1, N_MOD * D_MODEL))
    return out.reshape(DEPTH * MOD_ROWS, 1, N_MOD * D_MODEL)


def _ffn_kernel(n_mix, pair_x, pair_mix, nct, first, *refs):
    refs = list(refs)
    o_ref = refs.pop()
    is_ctx = pl.program_id(0) + first < nct

    def read(pair):
        if pair:
            a, b = refs.pop(0), refs.pop(0)
            return jnp.where(is_ctx, a[...], b[...])
        return refs.pop(0)[...]

    x = read(pair_x)
    sh_ref, sc_ref, g_ref, nw_ref, wi_ref, wo_ref = (refs.pop(0) for _ in range(6))
    if n_mix:
        gm_ref = refs.pop(0)
        ys = [read(pair_mix) for _ in range(n_mix)]
        wm_ref = refs.pop(0)
        w = D_MIX // n_mix
        acc = _bdot(ys[0], wm_ref[0:w, :])
        for j in range(1, n_mix):
            acc += _bdot(ys[j], wm_ref[j * w:(j + 1) * w, :])
        x = x + gm_ref[...] * acc
    h = (_rms(x, nw_ref[...]) * (1.0 + sc_ref[...]) + sh_ref[...]).astype(BF16)
    gate = jnp.dot(h, wi_ref[:, 0:D_FF], preferred_element_type=F32)
    up = jnp.dot(h, wi_ref[:, D_FF:2 * D_FF], preferred_element_type=F32)
    o_ref[...] = x + 0.5 * g_ref[...] * _bdot(_silu(gate) * up, wo_ref[...])


def _ffn(st, x, mod, wts, li, k, mix=()):
    row_ctx, row_lat = li * MOD_ROWS, li * MOD_ROWS + 1
    w_in, w_out = wts["ffn_w_in"], wts["ffn_w_out"]
    pair_x = isinstance(x, tuple)
    pair_mix = bool(mix) and isinstance(mix[0], tuple)
    args = list(x) if pair_x else [x]
    in_specs = st.pair(D_MODEL) if pair_x else [st.merged(D_MODEL)]
    args += [mod, mod, mod, wts["norm_w"], w_in, w_out]
    in_specs += [st.mod(6 * k + j, row_ctx, row_lat) for j in range(3)]
    in_specs += [_layer_const(wts["norm_w"], 3 * li + 2 * k), _layer_const(w_in, 2 * li + k),
                 _layer_const(w_out, 2 * li + k)]
    if mix:
        args.append(mod)
        in_specs.append(st.mod(5, row_ctx, row_lat))
        for y in mix:
            width = (y[0] if pair_mix else y).shape[1]
            args += list(y) if pair_mix else [y]
            in_specs += st.pair(width) if pair_mix else [st.owned(width)]
        args.append(wts["mix_w_out"])
        in_specs.append(_layer_const(wts["mix_w_out"], li))
    return pl.pallas_call(
        functools.partial(_ffn_kernel, len(mix), pair_x, pair_mix, st.nct, st.first),
        grid=(st.count,),
        in_specs=in_specs,
        out_specs=st.owned(D_MODEL),
        out_shape=jax.ShapeDtypeStruct((st.tokens, D_MODEL), F32),
        compiler_params=_cparams("parallel"),
        name="ffn",
    )(*args)


_IN_SPLITS = (("z", D_SSM, BF16), ("xbc", SSM_CONV_CH, F32), ("hy", HY_IN, F32), ("ret", RET_IN, BF16),
              ("att", ATT_IN, BF16), ("dt", DT_PAD, F32))


C_DT = D_SSM + SSM_CONV_CH
N_DT = 2 * SSM_HEADS
W_TAIL = HY_IN + RET_IN + ATT_IN


def _inproj_kernel(x_ref, sh_ref, sc_ref, nw_ref, w_ref, wdt_ref, *rest):
    o_refs, wt_scr = rest[:-1], rest[-1]

    @pl.when(pl.program_id(0) == 0)
    def _():
        wt_scr[...] = w_ref[:, C_DT + N_DT:C_DT + N_DT + W_TAIL]

    h = (_rms(x_ref[...], nw_ref[...]) * (1.0 + sc_ref[...]) + sh_ref[...]).astype(BF16)
    off = 0
    for (name, width, dtype), o_ref in zip(_IN_SPLITS, o_refs):
        if name == "dt":
            w = wdt_ref[...]
        elif off < C_DT:
            w = w_ref[:, off:off + width]
        else:
            w = wt_scr[:, off - C_DT:off - C_DT + width]
        o_ref[...] = jnp.dot(h, w, preferred_element_type=F32).astype(dtype)
        off += width


def _inproj(st, x, mod, wts, li):
    row_ctx, row_lat = li * MOD_ROWS, li * MOD_ROWS + 1
    return pl.pallas_call(
        _inproj_kernel,
        grid=(st.count,),
        in_specs=[st.merged(D_MODEL), st.mod(3, row_ctx, row_lat), st.mod(4, row_ctx, row_lat),
                  _layer_const(wts["norm_w"], 3 * li + 1),
                  _layer_const(wts["mix_w_in"], li), _layer_const(wts["mix_w_dt"], li)],
        out_specs=[st.owned(width) for _, width, _ in _IN_SPLITS],
        out_shape=[jax.ShapeDtypeStruct((st.tokens, width), dtype) for _, width, dtype in _IN_SPLITS],
        scratch_shapes=[pltpu.VMEM((D_MODEL, W_TAIL), BF16)],
        compiler_params=_cparams("arbitrary"),
        name="mix_in",
    )(x, mod, mod, wts["norm_w"], wts["mix_w_in"], wts["mix_w_dt"])


def _conv3_chunk(x_ref, c, nc, w, b):
    q = CHUNK
    l = nc * q
    r0 = pl.multiple_of(c * q, q)
    x = x_ref[pl.ds(r0, q), :]
    prev = x_ref[pl.ds(jnp.maximum(r0 - 1, 0), 1), :]
    nxt = x_ref[pl.ds(jnp.minimum(r0 + q, l - 1), 1), :]
    prev = jnp.where(c > 0, prev, 0.0)
    nxt = jnp.where(c < nc - 1, nxt, 0.0)
    rid = lax.broadcasted_iota(jnp.int32, (q, 1), 0)
    xm1 = jnp.where(rid == 0, prev, pltpu.roll(x, 1, 0))
    xp1 = jnp.where(rid == q - 1, nxt, pltpu.roll(x, q - 1, 0))
    return xm1 * w[0:1, :] + x * w[1:2, :] + xp1 * w[2:3, :] + b


def _tri(lower):
    i = lax.broadcasted_iota(jnp.int32, (CHUNK, CHUNK), 0)
    j = lax.broadcasted_iota(jnp.int32, (CHUNK, CHUNK), 1)
    return (j <= i) if lower else (j >= i)


def _rows(c):
    return pl.ds(pl.multiple_of(c * CHUNK, CHUNK), CHUNK)


def _split_cat(v, parts, axis):
    out, r = [], v
    for i in range(parts):
        piece = r.astype(BF16)
        out.append(piece)
        if i + 1 < parts:
            r = r - piece.astype(F32)
    return jnp.concatenate(out, axis=axis)


def _seg_mean(x, seg):
    w = x.shape[-1]
    ones = _block_diag(w, seg, 1.0).astype(BF16)
    return jnp.dot(_split_cat(x, 2, axis=1), jnp.concatenate([ones, ones], axis=0),
                   preferred_element_type=F32) * (1.0 / seg)


def _stack_heads(x, masks):
    return jnp.concatenate([jnp.where(m, x, 0.0) for m in masks], axis=0)


def _slot_out(tail, nb, li, stacked, ng, whole):
    zeros = (0,) * len(tail)
    slot = li if stacked else 0
    if whole:
        spec = pl.BlockSpec((ng, DEPTH) + tail, lambda b: (b, 0) + zeros)
    else:
        spec = pl.BlockSpec((ng, None) + tail, lambda b: (b, slot) + zeros)
    return spec, jax.ShapeDtypeStruct((nb, DEPTH if stacked else 1) + tail, F32)


def _slot_view(ref, g, slot):
    if slot is None:
        return ref.at[g]
    for other in range(DEPTH):
        if other != slot:
            ref[g, other] = jnp.zeros(ref.shape[2:], ref.dtype)
    return ref.at[g, slot]


def _carry_prev(args, in_specs, prevs, first_out):
    aliases = {}
    for j, prev in enumerate(prevs):
        aliases[len(args)] = first_out + j
        args.append(prev)
        in_specs.append(pl.BlockSpec(memory_space=pl.ANY))
    return aliases


SSD_SEL_W = 2 * SSM_HEADS * SSM_STATE
CUM_PIECES = 3
WGT_PIECES = 2
DT_REP = CUM_PIECES + WGT_PIECES


def _ssd_kernel(nc, ng, has_s0, slot, *refs):
    refs = list(refs)
    (z_ref, xbc_ref, dt_ref, cw_ref, cb_ref, dtb_ref, alog_ref, dsk_ref, nw_ref) = refs[:9]
    s0_ref = refs[9] if has_s0 else None
    (y_ref, s_ref, xs_scr, xk_scr, xt_scr, dt_scr, cum_scr, dtt_scr, cumt_scr, yf_scr, yb_scr, st_scr,
     sel_scr) = refs[-13:]
    q = CHUNK
    nh, n, p = SSM_HEADS, SSM_STATE, SSM_HEAD_DIM
    rep = nh // SSM_GROUPS
    hm = _head_masks(D_SSM, nh)

    @pl.when(pl.program_id(0) == 0)
    def _():
        k = lax.broadcasted_iota(jnp.int32, (LANES, SSD_SEL_W), 0)
        col = lax.broadcasted_iota(jnp.int32, (LANES, SSD_SEL_W), 1)
        is_cum = col < nh * n
        grp = k // (2 * nh)
        grp_ok = (is_cum & (grp < CUM_PIECES)) | ((~is_cum) & (grp >= CUM_PIECES) & (grp < DT_REP))
        hit = grp_ok & (k % nh == (col % (nh * n)) // n)
        sel_scr[...] = jnp.where(hit, 1.0, 0.0).astype(BF16)

    cw = cw_ref[...]
    cb = cb_ref[...]
    neg_a = -jnp.exp(alog_ref[...])
    dtb = dtb_ref[...]

    class Seq:
        def __init__(self, g):
            self.z, self.xbc, self.dt_in, self.y, self.s = (z_ref.at[g], xbc_ref.at[g], dt_ref.at[g], y_ref.at[g],
                                                            _slot_view(s_ref, g, slot))
            self.s0 = s0_ref.at[g] if has_s0 else None
            self.xs, self.xk, self.xt, self.dt, self.cum = (xs_scr.at[g], xk_scr.at[g], xt_scr.at[g], dt_scr.at[g],
                                                            cum_scr.at[g])
            self.dtt, self.cumt, self.yf, self.yb, self.st = (dtt_scr.at[g], cumt_scr.at[g], yf_scr.at[g],
                                                              yb_scr.at[g], st_scr.at[g])

    seqs = [Seq(g) for g in range(ng)]

    def prep_seq(sq, c):
        rows = _rows(c)
        xall = _silu(_conv3_chunk(sq.xbc, c, nc, cw, cb))
        sq.xs[rows, :] = xall
        xs = xall[:, 0:D_SSM]
        sq.xk[c] = _stack_heads(xs, hm).astype(BF16)
        sq.xt[c] = xs.T.astype(BF16)
        dt = _softplus(sq.dt_in[rows, :] + dtb)
        la = _split_cat(dt * neg_a, 3, axis=0)
        tri = jnp.concatenate([_tri(True), _tri(False)], axis=0)
        tri = jnp.where(tri, 1.0, 0.0).astype(BF16)
        cs = jnp.dot(jnp.concatenate([tri] * 3, axis=1), la, preferred_element_type=F32)
        lane = lax.broadcasted_iota(jnp.int32, (1, LANES), 1)
        cum = jnp.where(lane % (2 * nh) < nh, cs[0:q], cs[q:2 * q])
        sq.dt[rows, :] = dt
        sq.cum[rows, :] = cum
        sq.dtt[c] = dt.T
        sq.cumt[c] = cum.T

    def prep(c, carry):
        for sq in seqs:
            prep_seq(sq, c)
        return carry

    lax.fori_loop(0, nc, prep, 0)

    for sq in seqs:
        for d in range(2):
            for h in range(nh):
                blk = slice(h * p, (h + 1) * p)
                sq.st[d, blk, :] = sq.s0[d, h].T if has_s0 else jnp.zeros((p, n), F32)

    def pack_scalars(sq, c, d):
        rows = _rows(c)
        dt = sq.dt[rows, :]
        cum = sq.cum[rows, :]
        edge = q - 1 if d == 0 else 0
        lane = lax.broadcasted_iota(jnp.int32, (1, LANES), 1)
        used = (lane < DT_REP * 2 * nh) & ((lane % (2 * nh)) // nh == d)
        cum = jnp.where(used, cum, 0.0)
        wgt = jnp.exp(cum[edge:edge + 1, :] - cum) * dt
        grp = lane // (2 * nh)
        packed = jnp.zeros_like(cum)
        for src, pieces, g0 in ((cum, CUM_PIECES, 0), (wgt, WGT_PIECES, CUM_PIECES)):
            rest = src
            for i in range(pieces):
                piece = rest.astype(BF16).astype(F32)
                packed = jnp.where(grp == g0 + i, piece, packed)
                rest = rest - piece
        return jnp.where(used, packed, 0.0).astype(BF16)

    def operands(sq, c, d):
        rows = _rows(c)
        bm = [sq.xs[rows, D_SSM + g * n:D_SSM + (g + 1) * n] for g in range(SSM_GROUPS)]
        cm = [sq.xs[rows, D_SSM + (SSM_GROUPS + g) * n:D_SSM + (SSM_GROUPS + g + 1) * n]
              for g in range(SSM_GROUPS)]
        return bm, cm

    def decay_blocks(sq, c, d, e, cb_t):
        mask = _tri(d == 0)
        blocks, ecol = [], []
        for h in range(nh):
            r = d * nh + h
            col = e[:, h * n:(h + 1) * n]
            decay = jnp.exp(jnp.where(mask, col - sq.cumt[c, r:r + 1, :], -jnp.inf))
            blocks.append((cb_t[h // rep] * decay * sq.dtt[c, r:r + 1, :]).astype(BF16))
            ecol.append(jnp.exp(col))
        lo = lax.broadcasted_iota(jnp.int32, (1, n), 1) < p
        ecol = jnp.concatenate([jnp.where(lo, ecol[2 * j], ecol[2 * j + 1]) for j in range(nh // 2)], axis=1)
        return jnp.concatenate(blocks, axis=1), ecol

    def scan(i, carry):
        j = nc - 1 - i
        jobs = [(sq, c, d) for sq in seqs for c, d in ((i, 0), (j, 1))]
        packed = jnp.concatenate([pack_scalars(*job) for job in jobs], axis=0)
        e_all = jnp.dot(packed, sel_scr[...], preferred_element_type=F32)
        es = [e_all[t * q:(t + 1) * q] for t in range(len(jobs))]
        ops = [operands(*job) for job in jobs]
        cb_t = [[_bdot_nt(cm[g], bm[g]) for g in range(SSM_GROUPS)] for bm, cm in ops]
        dec = [decay_blocks(*job, es[t], cb_t[t]) for t, job in enumerate(jobs)]
        intra = [jnp.dot(dec[t][0], sq.xk[c], preferred_element_type=F32) for t, (sq, c, d) in enumerate(jobs)]
        s_old = [sq.st[d] for sq, c, d in jobs]
        inter = [jnp.concatenate([_bdot_nt(cm[g], s_old[t][g * rep * p:(g + 1) * rep * p, :])
                                  for g in range(SSM_GROUPS)], axis=1) for t, (bm, cm) in enumerate(ops)]
        for t, (sq, c, d) in enumerate(jobs):
            bm = ops[t][0]
            edge = q - 1 if d == 0 else 0
            for h in range(nh):
                blk, cols = slice(h * p, (h + 1) * p), slice(h * n, (h + 1) * n)
                wk = (bm[h // rep] * es[t][:, nh * n + h * n:nh * n + (h + 1) * n]).astype(BF16)
                upd = jnp.dot(sq.xt[c, blk, :], wk, preferred_element_type=F32)
                sq.st[d, blk, :] = jnp.exp(es[t][edge:edge + 1, cols]) * s_old[t][blk, :] + upd
            y = intra[t] + dec[t][1] * inter[t]
            if d == 0:
                sq.yf[_rows(c), :] = y
            else:
                sq.yb[_rows(c), :] = y
        return carry

    lax.fori_loop(0, nc, scan, 0)

    for sq in seqs:
        for d in range(2):
            for h in range(nh):
                sq.s[d, h] = sq.st[d, h * p:(h + 1) * p, :].T

    dsk = dsk_ref[...]
    nw = nw_ref[...]

    def fin(c, carry):
        rows = _rows(c)
        for sq in seqs:
            y = sq.yf[rows, :] + sq.yb[rows, :] + dsk * sq.xs[rows, 0:D_SSM]
            sq.y[rows, :] = _rms(y * _silu(sq.z[rows, :].astype(F32)), nw)
        return carry

    lax.fori_loop(0, nc, fin, 0)


def _ssd(z, xbc, dt, wts, li, s0, nb, l, seq0, stacked=False, prev=None):
    nc = l // CHUNK
    has_s0 = s0 is not None
    names = ("ssd_conv_w", "ssd_conv_b", "ssd_dt_bias", "ssd_a_log", "ssd_d", "ssd_norm_w")
    ng = SEQ_GROUP
    args = [z.reshape(-1, l, D_SSM), xbc.reshape(-1, l, SSM_CONV_CH), dt.reshape(-1, l, DT_PAD)]
    args += [wts[n] for n in names]
    in_specs = [_seq_spec(l, D_SSM, ng, seq0), _seq_spec(l, SSM_CONV_CH, ng, seq0), _seq_spec(l, DT_PAD, ng, seq0)]
    in_specs += [_layer_const(wts[n], li) for n in names]
    if has_s0:
        args.append(s0)
        in_specs.append(pl.BlockSpec((ng, None, 2, SSM_HEADS, SSM_STATE, SSM_HEAD_DIM),
                                     lambda b: (b, li, 0, 0, 0, 0)))
    whole = stacked and prev is None
    st_spec, st_shape = _slot_out((2, SSM_HEADS, SSM_STATE, SSM_HEAD_DIM), nb, li, stacked, ng, whole)
    aliases = _carry_prev(args, in_specs, [] if prev is None else [prev], 1)
    y, s = pl.pallas_call(
        functools.partial(_ssd_kernel, nc, ng, has_s0, li if whole else None),
        grid=(nb // ng,),
        in_specs=in_specs,
        out_specs=[_seq_spec(l, D_SSM, ng), st_spec],
        out_shape=[jax.ShapeDtypeStruct((nb, l, D_SSM), F32), st_shape],
        input_output_aliases=aliases,
        scratch_shapes=[pltpu.VMEM((ng, l, SSM_CONV_CH), F32),
                        pltpu.VMEM((ng, nc, SSM_HEADS * CHUNK, D_SSM), BF16),
                        pltpu.VMEM((ng, nc, D_SSM, CHUNK), BF16),
                        pltpu.VMEM((ng, l, DT_PAD), F32), pltpu.VMEM((ng, l, DT_PAD), F32),
                        pltpu.VMEM((ng, nc, DT_PAD, CHUNK), F32), pltpu.VMEM((ng, nc, DT_PAD, CHUNK), F32),
                        pltpu.VMEM((ng, l, D_SSM), F32), pltpu.VMEM((ng, l, D_SSM), F32),
                        pltpu.VMEM((ng, 2, D_SSM, SSM_STATE), F32),
                        pltpu.VMEM((LANES, SSD_SEL_W), BF16)],
        compiler_params=_cparams("arbitrary"),
        name="ssd",
    )(*args)
    return y.reshape(nb * l, D_SSM), s


def _ret_kernel(nc, ng, has_s0, slot, *refs):
    refs = list(refs)
    u_ref, dl_ref, gn_ref = refs[:3]
    s0_ref = refs[3] if has_s0 else None
    y_ref, s_ref, yf_scr, yb_scr, st_scr, dm_scr, e_scr, ea_scr, bd_scr = refs[-9:]
    q = CHUNK
    hd = RET_HEAD_DIM
    hm = _head_masks(D_RET, RET_HEADS)

    @pl.when(pl.program_id(0) == 0)
    def _():
        log_g = -_softplus(-dl_ref[...])
        ii = lax.broadcasted_iota(jnp.int32, (q, q), 0)
        jj = lax.broadcasted_iota(jnp.int32, (q, q), 1)
        dij = (ii - jj).astype(F32)
        ri = lax.broadcasted_iota(jnp.int32, (q, 1), 0).astype(F32)
        lfs, lbs = [], []
        for h in range(RET_HEADS):
            lf = log_g[:, h:h + 1]
            lb = log_g[:, RET_HEADS + h:RET_HEADS + h + 1]
            lfs.append(lf)
            lbs.append(lb)
            d_f = jnp.exp(jnp.where(dij >= 0, dij * lf, -jnp.inf))
            d_b = jnp.exp(jnp.where(dij <= 0, -dij * lb, -jnp.inf))
            dm_scr[:, h * q:(h + 1) * q] = d_f + d_b
        lf_l = _by_head(hm, lfs)
        lb_l = _by_head(hm, lbs)
        e_scr[0] = jnp.exp((ri + 1.0) * lf_l)
        e_scr[1] = jnp.exp((q - ri) * lb_l)
        e_scr[2] = jnp.exp((q - 1.0 - ri) * lf_l)
        e_scr[3] = jnp.exp(ri * lb_l)
        ea_scr[0:1, :] = jnp.exp(q * lf_l)
        ea_scr[1:2, :] = jnp.exp(q * lb_l)
        bd_scr[...] = _block_diag(D_RET, hd, 1.0)

    class Seq:
        def __init__(self, g):
            self.u, self.y, self.s = u_ref.at[g], y_ref.at[g], _slot_view(s_ref, g, slot)
            self.s0 = s0_ref.at[g] if has_s0 else None
            self.yf, self.yb, self.st = yf_scr.at[g], yb_scr.at[g], st_scr.at[g]

    seqs = [Seq(g) for g in range(ng)]

    for sq in seqs:
        for d in range(2):
            for h in range(RET_HEADS):
                blk = slice(h * hd, (h + 1) * hd)
                if has_s0:
                    parts = [jnp.zeros((hd, hd), F32)] * RET_HEADS
                    parts[h] = sq.s0[d, h]
                    sq.st[d, blk, :] = jnp.concatenate(parts, axis=1)
                else:
                    sq.st[d, blk, :] = jnp.zeros((hd, D_RET), F32)

    def qkv(sq, rows):
        return (sq.u[rows, 0:D_RET], sq.u[rows, D_RET:2 * D_RET].astype(F32) * (RET_HEAD_DIM ** -0.5),
                sq.u[rows, 2 * D_RET:3 * D_RET])

    def state_step(sq, d, qq, kk, vv):
        s_old = sq.st[d]
        y = e_scr[d] * _bdot(qq, s_old)
        sq.st[d] = ea_scr[d:d + 1, :] * s_old + _bdot_tn(kk * e_scr[2 + d], vv) * bd_scr[...]
        return y

    def scan(i, carry):
        rows = _rows(i)
        rows_b = _rows(nc - 1 - i)
        fwd = [qkv(sq, rows) for sq in seqs]
        bwd = [qkv(sq, rows_b) for sq in seqs]
        sc = [_bdot_nt(qq, _stack_heads(kk, hm)) * dm_scr[...] for qq, kk, _ in fwd]
        intra = [_bdot(sc[g], _stack_heads(fwd[g][2], hm)) for g in range(ng)]
        inter = [state_step(sq, 0, *fwd[g]) for g, sq in enumerate(seqs)]
        back = [state_step(sq, 1, *bwd[g]) for g, sq in enumerate(seqs)]
        for g, sq in enumerate(seqs):
            sq.yf[rows, :] = intra[g] + inter[g]
            sq.yb[rows_b, :] = back[g]
        return carry

    lax.fori_loop(0, nc, scan, 0)

    for sq in seqs:
        for d in range(2):
            for h in range(RET_HEADS):
                sq.s[d, h] = sq.st[d, h * hd:(h + 1) * hd, h * hd:(h + 1) * hd]

    gn = gn_ref[...]

    def fin(c, carry):
        rows = _rows(c)
        for sq in seqs:
            y = sq.yf[rows, :] + sq.yb[rows, :]
            cen = y - _seg_mean(y, hd)
            var = _seg_mean(cen * cen, hd)
            sq.y[rows, :] = cen * lax.rsqrt(var + NORM_EPS) * gn * _silu(sq.u[rows, 3 * D_RET:4 * D_RET].astype(F32))
        return carry

    lax.fori_loop(0, nc, fin, 0)


def _retention(u, wts, li, s0, nb, l, seq0, stacked=False, prev=None):
    nc = l // CHUNK
    has_s0 = s0 is not None
    names = ("ret_decay_logit", "ret_gn_w")
    ng = SEQ_GROUP
    args = [u.reshape(-1, l, RET_IN)] + [wts[n] for n in names]
    in_specs = [_seq_spec(l, RET_IN, ng, seq0)] + [_layer_const(wts[n], li) for n in names]
    if has_s0:
        args.append(s0)
        in_specs.append(pl.BlockSpec((ng, None, 2, RET_HEADS, RET_HEAD_DIM, RET_HEAD_DIM),
                                     lambda b: (b, li, 0, 0, 0, 0)))
    whole = stacked and prev is None
    st_spec, st_shape = _slot_out((2, RET_HEADS, RET_HEAD_DIM, RET_HEAD_DIM), nb, li, stacked, ng, whole)
    aliases = _carry_prev(args, in_specs, [] if prev is None else [prev], 1)
    y, s = pl.pallas_call(
        functools.partial(_ret_kernel, nc, ng, has_s0, li if whole else None),
        grid=(nb // ng,),
        in_specs=in_specs,
        out_specs=[_seq_spec(l, D_RET, ng), st_spec],
        out_shape=[jax.ShapeDtypeStruct((nb, l, D_RET), F32), st_shape],
        input_output_aliases=aliases,
        scratch_shapes=[pltpu.VMEM((ng, l, D_RET), F32), pltpu.VMEM((ng, l, D_RET), F32),
                        pltpu.VMEM((ng, 2, D_RET, D_RET), F32), pltpu.VMEM((CHUNK, RET_HEADS * CHUNK), F32),
                        pltpu.VMEM((4, CHUNK, D_RET), F32), pltpu.VMEM((8, D_RET), F32),
                        pltpu.VMEM((D_RET, D_RET), F32)],
        compiler_params=_cparams("arbitrary"),
        name="retention",
    )(*args)
    return y.reshape(nb * l, D_RET), s


def _split(x):
    hi = x.astype(BF16)
    return hi, (x - hi.astype(F32)).astype(BF16)


def _dot3(a_hi, a_lo, b_hi, b_lo):
    d = lambda p, q: jnp.dot(p, q, preferred_element_type=F32)
    return d(a_hi, b_hi) + (d(a_lo, b_hi) + d(a_hi, b_lo))


@functools.lru_cache(maxsize=None)
def _dft_fwd_host(l):
    n = 2 * l
    f = np.arange(l, dtype=np.int64)[:, None]
    s = np.arange(l, dtype=np.int64)[None, :]
    ang = ((f * s) % n).astype(np.float64) * (2.0 * math.pi / n)
    im = -np.sin(ang)
    im[0] = np.where(np.arange(l) % 2 == 0, 1.0, -1.0)
    return np.concatenate([np.cos(ang), im], axis=0).astype(np.float32)


def _dft_tables(l):
    n = 2 * l
    fwd = jnp.asarray(_dft_fwd_host(l))
    wgt = np.full((n, 1), 2.0 / n, np.float32)
    wgt[0] = wgt[l] = 1.0 / n
    return _split(fwd) + _split((fwd * wgt).T)


def _dft_block_tables(tables, l):
    fwd_hi, fwd_lo, inv_hi, inv_lo = tables
    r = DFT_BLOCK
    nblk = l // r
    fb = lambda t: (t[0:l].reshape(nblk, r, l), t[l:2 * l].reshape(nblk, r, l))
    fwd_blk = jnp.concatenate(fb(fwd_hi) + fb(fwd_lo), axis=1)
    inv_blk = jnp.concatenate([inv_hi.reshape(nblk, r, 2 * l), inv_lo.reshape(nblk, r, 2 * l)], axis=1)
    return fwd_blk, inv_blk


def _hy_filter_kernel(l, feats_ref, dec_ref, w1_ref, b1_ref, w2_ref, b2_ref, w3_ref, fr_ref, fh_ref, fl_ref,
                      a_ref, b_ref, d_ref):
    fr = fr_ref[...]
    xdot = lambda a, b: _dot3(*_split(a), *_split(b))
    h = jnp.sin(fr * (xdot(feats_ref[...], w1_ref[...]) + b1_ref[...]))
    h = jnp.sin(fr * (xdot(h, w2_ref[...]) + b2_ref[...]))
    h = xdot(h, w3_ref[...])
    dec = jnp.concatenate([dec_ref[...]] * HY_ORDER, axis=-1)
    row0 = lax.broadcasted_iota(jnp.int32, (l, 1), 0) == 0
    hf = h[:, 0:HY_ORDER * D_HY] * dec
    hb = h[:, HY_ORDER * D_HY:2 * HY_ORDER * D_HY] * dec
    hb = jnp.where(row0, 0.0, hb)
    hs = _split(hf + hb)
    hd = _split(hf - hb)
    re = _dot3(fh_ref[0:l, :], fl_ref[0:l, :], *hs)
    ny = _dot3(fh_ref[l:l + 8, :], fl_ref[l:l + 8, :], *hs)[0:1]
    im = _dot3(fh_ref[l:2 * l, :], fl_ref[l:2 * l, :], *hd)
    for o in range(HY_ORDER):
        cols = slice(o * D_HY, (o + 1) * D_HY)
        a_ref[o] = re[:, cols]
        b_ref[o] = jnp.where(row0, 0.0, im[:, cols])
        d_ref[o] = jnp.where(row0, ny[:, cols], re[:, cols])


def _hy_filter(l, fwd_hi, fwd_lo, wts, li):
    pos = np.arange(l, dtype=np.float32)
    t = pos / np.float32(l - 1)
    bands = np.linspace(1e-4, HY_BANDS - 1, HY_BANDS, dtype=np.float32)
    ang = np.float32(2.0 * math.pi / l) * pos[:, None] * bands[None, :]
    feats = np.concatenate([t[:, None], np.cos(ang), -np.sin(ang)], axis=-1).astype(np.float32)
    feats = np.pad(feats, ((0, 0), (0, LANES - HY_EMB)))
    max_decay = math.log(HY_TARGET) / HY_FAST_DECAY
    min_decay = math.log(HY_TARGET) / HY_SLOW_DECAY
    deltas = np.abs(np.linspace(min_decay, max_decay, D_HY, dtype=np.float32))
    dec = np.exp(-t[:, None] * deltas[None, :]).astype(np.float32)
    spec = jax.ShapeDtypeStruct((HY_ORDER, l, D_HY), F32)
    names = ("hy_w1", "hy_b1", "hy_w2", "hy_b2", "hy_w3", "hy_freq")
    return pl.pallas_call(
        functools.partial(_hy_filter_kernel, l),
        grid=(1,),
        in_specs=[_full(feats.shape), _full(dec.shape)] + [_layer_const(wts[n], li) for n in names]
                 + [_full(fwd_hi.shape), _full(fwd_lo.shape)],
        out_specs=[_full(spec.shape)] * 3,
        out_shape=[spec] * 3,
        compiler_params=_cparams("arbitrary"),
        name="hyena_filter",
    )(jnp.asarray(feats), jnp.asarray(dec), *[wts[n] for n in names], fwd_hi, fwd_lo)


def _hy_kernel(nc, ng, u_ref, cw_ref, cb_ref, f_ref, g_ref, a_ref, b_ref, d_ref, bias_ref, y_ref,
               uc_scr, vh_scr, vl_scr, sh_scr, sl_scr, z_scr):
    l = nc * CHUNK
    r = DFT_BLOCK
    nblk = l // r
    cw = cw_ref[...]
    cb = cb_ref[...]
    dot = lambda p, q: jnp.dot(p, q, preferred_element_type=F32)
    wide = lambda x: jnp.concatenate([x] * ng, axis=1)
    seq_cols = lambda g: slice(g * D_HY, (g + 1) * D_HY)

    def for_blocks(body):
        if nblk == 1:
            body(0)
        else:
            lax.fori_loop(0, nblk, lambda i, carry: (body(i), carry)[1], 0)

    def conv(c, carry):
        rows = _rows(c)
        for g in range(ng):
            uc = _conv3_chunk(u_ref.at[g], c, nc, cw, cb)
            uc_scr[g, rows, :] = uc
            vh_scr[rows, seq_cols(g)], vl_scr[rows, seq_cols(g)] = _split(uc[:, 0:D_HY])
        return carry

    lax.fori_loop(0, nc, conv, 0)

    def long_conv(o):
        def spectrum(i):
            rows = pl.ds(pl.multiple_of(i * r, r), r)
            rows_im = pl.ds(pl.multiple_of(l + i * r, r), r)
            p = dot(f_ref[i], vh_scr[...])
            pl_ = dot(f_ref[i, 0:2 * r, :], vl_scr[...])
            zr = p[0:r] + (p[2 * r:3 * r] + pl_[0:r])
            zi = p[r:2 * r] + (p[3 * r:4 * r] + pl_[r:2 * r])
            fa, fb, fd = wide(a_ref[o, rows, :]), wide(b_ref[o, rows, :]), wide(d_ref[o, rows, :])
            sh_scr[rows, :], sl_scr[rows, :] = _split(zr * fa - zi * fb)
            sh_scr[rows_im, :], sl_scr[rows_im, :] = _split(zr * fb + zi * fd)

        for_blocks(spectrum)

        def inverse(i):
            rows = pl.ds(pl.multiple_of(i * r, r), r)
            p = dot(g_ref[i], sh_scr[...])
            y = p[0:r] + (p[r:2 * r] + dot(g_ref[i, 0:r, :], sl_scr[...]))
            gate = jnp.concatenate([uc_scr[g, rows, (o + 1) * D_HY:(o + 2) * D_HY] for g in range(ng)], axis=1)
            if o == 0:
                v = jnp.concatenate([uc_scr[g, rows, 0:D_HY] for g in range(ng)], axis=1)
            else:
                v = z_scr[rows, :]
            out = gate * (y + v * wide(bias_ref[o:o + 1, :]))
            if o + 1 < HY_ORDER:
                z_scr[rows, :] = out
                vh_scr[rows, :], vl_scr[rows, :] = _split(out)
            else:
                for g in range(ng):
                    y_ref[g, rows, :] = out[:, seq_cols(g)]

        for_blocks(inverse)

    for o in range(HY_ORDER):
        long_conv(o)


def _hyena(u, wts, li, tables, spectra, nb, l, seq0):
    nc = l // CHUNK
    ng = min(HY_GROUP, nb)
    a, b, d = spectra
    names = ("hy_conv_w", "hy_conv_b")
    y = pl.pallas_call(
        functools.partial(_hy_kernel, nc, ng),
        grid=(nb // ng,),
        in_specs=[_seq_spec(l, HY_IN, ng, seq0)] + [_layer_const(wts[n], li) for n in names]
                 + [_const(tables[0].shape), _const(tables[1].shape),
                    _const((HY_ORDER, l, D_HY)), _const((HY_ORDER, l, D_HY)), _const((HY_ORDER, l, D_HY)),
                    _layer_const(wts["hy_bias"], li)],
        out_specs=pl.BlockSpec((ng, l, D_HY), lambda i: (i, 0, 0)),
        out_shape=jax.ShapeDtypeStruct((nb, l, D_HY), F32),
        scratch_shapes=[pltpu.VMEM((ng, l, HY_IN), F32),
                        pltpu.VMEM((l, ng * D_HY), BF16), pltpu.VMEM((l, ng * D_HY), BF16),
                        pltpu.VMEM((2 * l, ng * D_HY), BF16), pltpu.VMEM((2 * l, ng * D_HY), BF16),
                        pltpu.VMEM((l, ng * D_HY), F32)],
        compiler_params=_cparams("parallel"),
        name="hyena",
    )(u.reshape(-1, l, HY_IN), *[wts[n] for n in names], *tables, a, b, d, wts["hy_bias"])
    return y.reshape(nb * l, D_HY)


def _seg_rms(x, w):
    return x * lax.rsqrt(_seg_mean(x * x, HEAD_DIM) + NORM_EPS) * w


def _stack_q(q):
    lo = lax.broadcasted_iota(jnp.int32, (1, D_KV), 1) < HEAD_DIM
    qa = q[:, 0:D_KV]
    qb = q[:, D_KV:2 * D_KV]
    return jnp.concatenate([jnp.where(lo, qa, 0.0), jnp.where(lo, pltpu.roll(qa, HEAD_DIM, 1), 0.0),
                            jnp.where(lo, 0.0, pltpu.roll(qb, HEAD_DIM, 1)), jnp.where(lo, 0.0, qb)], axis=0)


def _unstack_o(o):
    r = o.shape[0] // ATT_HEADS
    lo = lax.broadcasted_iota(jnp.int32, (1, D_KV), 1) < HEAD_DIM
    ya = jnp.where(lo, o[0:r], pltpu.roll(o[r:2 * r], HEAD_DIM, 1))
    yb = jnp.where(lo, pltpu.roll(o[2 * r:3 * r], HEAD_DIM, 1), o[3 * r:4 * r])
    return jnp.concatenate([ya, yb], axis=1)


def _sink_col(sink_ref, li, r):
    rb = lax.broadcasted_iota(jnp.int32, (ATT_HEADS * r, 1), 0) // r
    col = jnp.full((ATT_HEADS * r, 1), sink_ref[li * ATT_HEADS + ATT_HEADS - 1], F32)
    for h in range(ATT_HEADS - 2, -1, -1):
        col = jnp.where(rb == h, sink_ref[li * ATT_HEADS + h], col)
    return col


def _ctx_attn_kernel(li, ng, slot, u_ref, qn_ref, kn_ref, sink_ref, *rest):
    y_ref, k_ref, v_ref = rest[-3:]
    l = u_ref.shape[1]
    sink = _sink_col(sink_ref, li, l)
    seqs = range(ng)
    u = [u_ref[g].astype(F32) for g in seqs]
    q = [_seg_rms(u[g][:, 0:D_ATT], qn_ref[...]) for g in seqs]
    k = [_seg_rms(u[g][:, D_ATT:D_ATT + D_KV], kn_ref[:, 0:D_KV]) for g in seqs]
    v = [u[g][:, D_ATT + D_KV:D_ATT + 2 * D_KV] for g in seqs]
    for g in seqs:
        _slot_view(k_ref, g, slot)[...] = k[g]
        _slot_view(v_ref, g, slot)[...] = v[g]
    s = [_bdot_nt(_stack_q(q[g]), k[g]) * (HEAD_DIM ** -0.5) for g in seqs]
    m = [jnp.maximum(jnp.max(s[g], axis=-1, keepdims=True), sink) for g in seqs]
    p = [jnp.exp(s[g] - m[g]) for g in seqs]
    den = [jnp.sum(p[g], axis=-1, keepdims=True) + jnp.exp(sink - m[g]) for g in seqs]
    o = [_bdot(p[g] * (1.0 / den[g]), v[g]) for g in seqs]
    for g in seqs:
        y_ref[g] = _unstack_o(o[g])


def _ctx_attention(u, wts, li, nb, l, seq0, prev=None):
    ng = ATT_GROUP
    whole = prev is None
    kv_spec, kv_shape = _slot_out((l, D_KV), nb, li, True, ng, whole)
    args = [u.reshape(-1, l, ATT_IN), wts["attn_q_norm"], wts["attn_k_norm"], wts["attn_sink"]]
    in_specs = [_seq_spec(l, ATT_IN, ng, seq0), _layer_const(wts["attn_q_norm"], li),
                _layer_const(wts["attn_k_norm"], li), pl.BlockSpec(memory_space=pltpu.SMEM)]
    aliases = _carry_prev(args, in_specs, [] if prev is None else list(prev), 1)
    y, k, v = pl.pallas_call(
        functools.partial(_ctx_attn_kernel, li, ng, li if whole else None),
        grid=(nb // ng,),
        in_specs=in_specs,
        out_specs=[_seq_spec(l, D_ATT, ng), kv_spec, kv_spec],
        out_shape=[jax.ShapeDtypeStruct((nb, l, D_ATT), F32), kv_shape, kv_shape],
        input_output_aliases=aliases,
        compiler_params=_cparams("parallel"),
        name="ctx_attention",
    )(*args)
    return y.reshape(nb * l, D_ATT), k, v


@functools.lru_cache(maxsize=None)
def _rope_tables_host(l):
    n_rows = l // GRID_W
    rows = np.repeat(np.arange(n_rows, dtype=np.float32), GRID_W)
    cols = np.tile(np.arange(GRID_W, dtype=np.float32), n_rows)
    nf = HEAD_DIM // 4
    inv = (np.float32(ROPE_BASE) ** (-np.arange(nf, dtype=np.float32) / np.float32(nf))).astype(np.float32)
    ar = rows[:, None] * inv[None, :]
    ac = cols[:, None] * inv[None, :]
    cos = np.concatenate([np.cos(ar), np.cos(ar), np.cos(ac), np.cos(ac)], axis=-1)
    sin = np.concatenate([-np.sin(ar), np.sin(ar), -np.sin(ac), np.sin(ac)], axis=-1)
    return (np.tile(cos, (1, ATT_HEADS)).astype(np.float32), np.tile(sin, (1, ATT_HEADS)).astype(np.float32))


def _rope(x, cos, sin):
    w = x.shape[-1]
    nf = HEAD_DIM // 4
    lane = lax.broadcasted_iota(jnp.int32, x.shape, 1)
    first = (lane % (2 * nf)) < nf
    partner = jnp.where(first, pltpu.roll(x, w - nf, 1), pltpu.roll(x, nf, 1))
    return x * cos + partner * sin


def _lat_attn_kernel(nblk, li, ng, u_ref, qn_ref, kn_ref, cos_ref, sin_ref, ck_ref, cv_ref, sink_ref,
                     y_ref, q_scr, k_scr, v_scr):
    blk = ATT_BLOCK
    l = nblk * blk
    seqs = range(ng)
    cos_q = cos_ref[...]
    sin_q = sin_ref[...]
    zeros = jnp.zeros((blk, D_KV), F32)
    for g in seqs:
        u = u_ref[g].astype(F32)
        q = _seg_rms(u[:, 0:D_ATT], qn_ref[...])
        k = _seg_rms(u[:, D_ATT:D_ATT + D_KV], kn_ref[:, 0:D_KV])
        q_scr[g] = _rope(q, cos_q, sin_q)
        k_scr[g, 0:blk, :] = zeros
        k_scr[g, blk + l:2 * blk + l, :] = zeros
        v_scr[g, 0:blk, :] = zeros
        v_scr[g, blk + l:2 * blk + l, :] = zeros
        k_scr[g, blk:blk + l, :] = _rope(k, cos_q[:, 0:D_KV], sin_q[:, 0:D_KV])
        v_scr[g, blk:blk + l, :] = u[:, D_ATT + D_KV:D_ATT + 2 * D_KV]

    scale = HEAD_DIM ** -0.5
    r = lax.broadcasted_iota(jnp.int32, (blk, 3 * blk), 0)
    cidx = lax.broadcasted_iota(jnp.int32, (blk, 3 * blk), 1)
    band = (cidx - r >= blk - WINDOW) & (cidx - r <= blk + WINDOW)
    sink = _sink_col(sink_ref, li, blk)

    def block(i, carry):
        rows = _rows(i)
        win = pl.ds(pl.multiple_of(i * blk, blk), 3 * blk)
        kpos = cidx + (i - 1) * blk
        valid = band & (kpos >= 0) & (kpos < l)
        valid = jnp.concatenate([valid] * ATT_HEADS, axis=0)
        qs = [_stack_q(q_scr[g, rows, :]) for g in seqs]
        s_loc = [jnp.where(valid, _bdot_nt(qs[g], k_scr[g, win, :]) * scale, -jnp.inf) for g in seqs]
        s_ctx = [_bdot_nt(qs[g], ck_ref[g]) * scale for g in seqs]
        m = [jnp.maximum(jnp.maximum(jnp.max(s_loc[g], axis=-1, keepdims=True),
                                     jnp.max(s_ctx[g], axis=-1, keepdims=True)), sink) for g in seqs]
        p_loc = [jnp.exp(s_loc[g] - m[g]) for g in seqs]
        p_ctx = [jnp.exp(s_ctx[g] - m[g]) for g in seqs]
        inv = [1.0 / (jnp.sum(p_loc[g], axis=-1, keepdims=True) + jnp.sum(p_ctx[g], axis=-1, keepdims=True)
                      + jnp.exp(sink - m[g])) for g in seqs]
        o = [_bdot(p_ctx[g] * inv[g], cv_ref[g]) + _bdot(p_loc[g] * inv[g], v_scr[g, win, :]) for g in seqs]
        for g in seqs:
            y_ref[g, rows, :] = _unstack_o(o[g])
        return carry

    lax.fori_loop(0, nblk, block, 0)


def _lat_attention(u, wts, li, ck, cv, nb, l, seq0):
    lc = ck.shape[2]
    nblk = l // ATT_BLOCK
    ng = ATT_GROUP
    cos, sin = _rope_tables_host(l)
    cache_spec = pl.BlockSpec((ng, None, lc, D_KV), lambda b: (b, li, 0, 0))
    y = pl.pallas_call(
        functools.partial(_lat_attn_kernel, nblk, li, ng),
        grid=(nb // ng,),
        in_specs=[_seq_spec(l, ATT_IN, ng, seq0), _layer_const(wts["attn_q_norm"], li),
                  _layer_const(wts["attn_k_norm"], li), _const((l, D_ATT)), _const((l, D_ATT)),
                  cache_spec, cache_spec, pl.BlockSpec(memory_space=pltpu.SMEM)],
        out_specs=_seq_spec(l, D_ATT, ng),
        out_shape=jax.ShapeDtypeStruct((nb, l, D_ATT), F32),
        scratch_shapes=[pltpu.VMEM((ng, l, D_ATT), F32), pltpu.VMEM((ng, l + 2 * ATT_BLOCK, D_KV), F32),
                        pltpu.VMEM((ng, l + 2 * ATT_BLOCK, D_KV), F32)],
        compiler_params=_cparams("parallel"),
        name="lat_attention",
    )(u.reshape(-1, l, ATT_IN), wts["attn_q_norm"], wts["attn_k_norm"], jnp.asarray(cos), jnp.asarray(sin),
      ck, cv, wts["attn_sink"])
    return y.reshape(nb * l, D_ATT)


def _pad_lane_tile(w):
    return jnp.pad(w, [(0, 0)] * (w.ndim - 1) + [(0, -w.shape[-1] % LANES)])


def _dt_weight(w):
    zeros = jnp.zeros(w.shape[:-1] + (DT_PAD - DT_REP * N_DT,), w.dtype)
    return jnp.concatenate([w[..., C_DT:C_DT + N_DT]] * DT_REP + [zeros], axis=-1).astype(BF16)


def _prep_weights(p):
    row = lambda a: a.reshape(DEPTH, 1, -1)
    pad_lanes = lambda a: jnp.pad(row(a), ((0, 0), (0, 0), (0, LANES - a[0].size)))
    return dict(
        norm_w=p["norm_w"].reshape(DEPTH * 3, 1, D_MODEL),
        ffn_w_in=p["ffn_w_in"].astype(BF16).reshape(DEPTH * 2, D_MODEL, 2 * D_FF),
        ffn_w_out=p["ffn_w_out"].astype(BF16).reshape(DEPTH * 2, D_FF, D_MODEL),
        mix_w_in=_pad_lane_tile(p["mix_w_in"]).astype(BF16), mix_w_dt=_dt_weight(p["mix_w_in"]),
        mix_w_out=p["mix_w_out"].astype(BF16),
        ssd_conv_w=p["ssd_conv_w"], ssd_conv_b=row(p["ssd_conv_b"]),
        ssd_dt_bias=pad_lanes(jnp.tile(row(p["ssd_dt_bias"]), (1, 1, DT_REP))),
        ssd_a_log=pad_lanes(jnp.tile(row(p["ssd_a_log"]), (1, 1, DT_REP))),
        ssd_d=row(jnp.repeat(p["ssd_d"], SSM_HEAD_DIM, axis=-1)), ssd_norm_w=row(p["ssd_norm_w"]),
        hy_conv_w=p["hy_conv_w"], hy_conv_b=row(p["hy_conv_b"]), hy_bias=p["hy_bias"],
        hy_w1=jnp.pad(p["hy_w1"], ((0, 0), (0, LANES - HY_EMB), (0, 0))), hy_b1=row(p["hy_b1"]),
        hy_w2=p["hy_w2"], hy_b2=row(p["hy_b2"]), hy_w3=p["hy_w3"], hy_freq=row(p["hy_freq"]),
        ret_decay_logit=pad_lanes(p["ret_decay_logit"]), ret_gn_w=row(p["ret_gn_w"]),
        attn_q_norm=row(jnp.tile(p["attn_q_norm"], (1, ATT_HEADS))),
        attn_k_norm=row(jnp.tile(p["attn_k_norm"], (1, ATT_HEADS))),
        attn_sink=p["attn_sink"].reshape(DEPTH * ATT_HEADS),
    )


def _mixers(u, wts, li, nb, l, seq0, ssd_s0, ret_s0, ctx_kv, hy_tables, hy_spectra, carried):
    z, xbc, hy, ret, att, dt = u
    ctx = ctx_kv is None
    y_ssd, s_ssd = _ssd(z, xbc, dt, wts, li, ssd_s0, nb, l, seq0, stacked=ctx, prev=carried.get("ssd"))
    y_hy = _hyena(hy, wts, li, hy_tables, hy_spectra, nb, l, seq0)
    y_ret, s_ret = _retention(ret, wts, li, ret_s0, nb, l, seq0, stacked=ctx, prev=carried.get("ret"))
    if ctx:
        y_att, k, v = _ctx_attention(att, wts, li, nb, l, seq0, prev=carried.get("kv"))
        carried = dict(ssd=s_ssd, ret=s_ret, kv=(k, v))
    else:
        y_att = _lat_attention(att, wts, li, ctx_kv[0], ctx_kv[1], nb, l, seq0)
    return (y_ssd, y_hy, y_ret, y_att), carried


def kernel(x_prompt, x_sample, cache_k, cache_v, state_ssd, state_ret, c, c_ctx, w_mod, b_mod, norm_w, ffn_w_in, ffn_w_out, mix_w_in, mix_w_out, ssd_conv_w, ssd_conv_b, ssd_dt_bias, ssd_a_log, ssd_d, ssd_norm_w, hy_conv_w, hy_conv_b, hy_w1, hy_b1, hy_w2, hy_b2, hy_w3, hy_freq, hy_bias, ret_decay_logit, ret_gn_w, attn_q_norm, attn_k_norm, attn_sink):
    bp, lp_len, _ = x_prompt.shape
    bs, ls_len, _ = x_sample.shape
    lc = cache_k.shape[2]

    cond = jnp.concatenate([c_ctx[None, :], c, jnp.zeros((MOD_ROWS - 1 - bs, D_MODEL), F32)], axis=0)
    mod = _modulation(cond, w_mod, b_mod)
    wts = _prep_weights(dict(
        norm_w=norm_w, ffn_w_in=ffn_w_in, ffn_w_out=ffn_w_out, mix_w_in=mix_w_in, mix_w_out=mix_w_out,
        ssd_conv_w=ssd_conv_w, ssd_conv_b=ssd_conv_b, ssd_dt_bias=ssd_dt_bias, ssd_a_log=ssd_a_log, ssd_d=ssd_d,
        ssd_norm_w=ssd_norm_w, hy_conv_w=hy_conv_w, hy_conv_b=hy_conv_b, hy_w1=hy_w1, hy_b1=hy_b1, hy_w2=hy_w2,
        hy_b2=hy_b2, hy_w3=hy_w3, hy_freq=hy_freq, hy_bias=hy_bias, ret_decay_logit=ret_decay_logit,
        ret_gn_w=ret_gn_w, attn_q_norm=attn_q_norm, attn_k_norm=attn_k_norm, attn_sink=attn_sink))
    ck = cache_k.reshape(bs, DEPTH, lc, D_KV)
    cv = cache_v.reshape(bs, DEPTH, lc, D_KV)

    tab_p = _dft_tables(lp_len)
    tab_s = _dft_tables(ls_len)
    blk_p = _dft_block_tables(tab_p, lp_len)
    blk_s = _dft_block_tables(tab_s, ls_len)

    tm = TOKEN_TILE
    tp, ts = bp * lp_len, bs * ls_len
    assert tp % tm == 0 and ts % tm == 0 and ls_len % tm == 0 and tp % ls_len == 0
    st = _Stream(tm, tp // tm, ts // tm, ls_len)
    x = (x_prompt.reshape(tp, D_MODEL), x_sample.reshape(ts, D_MODEL))
    carried = {}
    for li in range(DEPTH):
        spec_p = _hy_filter(lp_len, tab_p[0], tab_p[1], wts, li)
        spec_s = _hy_filter(ls_len, tab_s[0], tab_s[1], wts, li)
        x = _ffn(st, x, mod, wts, li, 0)
        u = _inproj(st, x, mod, wts, li)
        y_p, carried = _mixers(u, wts, li, bp, lp_len, 0, None, None, None, blk_p, spec_p, carried)
        y_s, _ = _mixers(u, wts, li, bs, ls_len, tp // ls_len, state_ssd, state_ret, (ck, cv), blk_s, spec_s, {})
        if li + 1 < DEPTH:
            x = _ffn(st, x, mod, wts, li, 1, mix=tuple(zip(y_p, y_s)))
        else:
            yp = _ffn(st.part(0), x, mod, wts, li, 1, mix=y_p)
            ys = _ffn(st.part(1), x, mod, wts, li, 1, mix=y_s)

    kv_shape = (bp, DEPTH, lp_len, ATT_KV_HEADS, HEAD_DIM)
    new_k, new_v = carried["kv"]
    return (yp.reshape(bp, lp_len, D_MODEL), ys.reshape(bs, ls_len, D_MODEL),
            new_k.reshape(kv_shape), new_v.reshape(kv_shape), carried["ssd"], carried["ret"])
```

```python
import functools
import math

import numpy as np
import jax
import jax.numpy as jnp
from jax import lax
from jax.experimental import pallas as pl
from jax.experimental.pallas import tpu as pltpu

F32 = jnp.float32
BF16 = jnp.bfloat16

D_MODEL = 1024
DEPTH = 2
GRID_W = 64
D_FF = 2816
N_MOD = 9
NORM_EPS = 1e-6
CHUNK = 128
D_SSM = 256
SSM_HEADS = 4
SSM_HEAD_DIM = 64
SSM_STATE = 128
SSM_GROUPS = 2
SSM_CONV_CH = D_SSM + 2 * SSM_GROUPS * SSM_STATE
D_HY = 256
HY_ORDER = 2
HY_BANDS = 16
HY_EMB = 1 + 2 * HY_BANDS
HY_HIDDEN = 64
HY_FAST_DECAY = 0.3
HY_SLOW_DECAY = 1.5
HY_TARGET = 1e-2
HY_IN = (HY_ORDER + 1) * D_HY
D_RET = 256
RET_HEADS = 4
RET_HEAD_DIM = 64
RET_IN = 4 * D_RET
ATT_HEADS = 4
ATT_KV_HEADS = 2
HEAD_DIM = 64
D_ATT = ATT_HEADS * HEAD_DIM
D_KV = ATT_KV_HEADS * HEAD_DIM
ATT_IN = D_ATT + 2 * D_KV
WINDOW = 128
ATT_BLOCK = 128
ROPE_BASE = 10000.0
D_MIX = D_SSM + D_HY + D_RET + D_ATT

LANES = 128
DT_PAD = LANES
VMEM_LIMIT = 56 * 1024 * 1024
MOD_ROWS = 8

TOKEN_TILE = 512
DFT_BLOCK = 512
HY_GROUP = 4
SEQ_GROUP = 2
ATT_GROUP = 2


def _cparams(*sem):
    return pltpu.CompilerParams(dimension_semantics=sem, vmem_limit_bytes=VMEM_LIMIT)


def _rms(x, w):
    return x * lax.rsqrt(jnp.mean(x * x, axis=-1, keepdims=True) + NORM_EPS) * w


def _silu(x):
    return x * (1.0 / (1.0 + jnp.exp(-x)))


def _softplus(x):
    return jnp.maximum(x, 0.0) + jnp.log1p(jnp.exp(-jnp.abs(x)))


def _bdot(a, b):
    return jnp.dot(a.astype(BF16), b.astype(BF16), preferred_element_type=F32)


def _bdot_nt(a, b):
    return lax.dot_general(a.astype(BF16), b.astype(BF16), (((1,), (1,)), ((), ())),
                           preferred_element_type=F32)


def _bdot_tn(a, b):
    return lax.dot_general(a.astype(BF16), b.astype(BF16), (((0,), (0,)), ((), ())),
                           preferred_element_type=F32)


def _full(shape):
    n = len(shape)
    return pl.BlockSpec(shape, lambda *_: (0,) * n)


def _const(shape):
    n = len(shape)
    return pl.BlockSpec(shape, lambda *_: (0,) * n, pipeline_mode=pl.Buffered(1))


def _layer_const(arr, li):
    tail = arr.shape[1:]
    zeros = (0,) * len(tail)
    return pl.BlockSpec((None,) + tail, lambda *_: (li,) + zeros, pipeline_mode=pl.Buffered(1))


class _Stream:
    def __init__(self, tm, nct, nlt, l_lat, first=0, count=None):
        self.tm, self.nct, self.nlt, self.l_lat = tm, nct, nlt, l_lat
        self.first = first
        self.count = nct + nlt if count is None else count

    def part(self, path):
        first, count = (0, self.nct) if path == 0 else (self.nct, self.nlt)
        return _Stream(self.tm, self.nct, self.nlt, self.l_lat, first, count)

    @property
    def tokens(self):
        return self.count * self.tm

    def merged(self, width):
        return pl.BlockSpec((self.tm, width), lambda i: (i + self.first, 0))

    def owned(self, width):
        return pl.BlockSpec((self.tm, width), lambda i: (i, 0))

    def pair(self, width):
        ctx = pl.BlockSpec((self.tm, width), lambda i: (jnp.minimum(i + self.first, self.nct - 1), 0))
        lat = pl.BlockSpec((self.tm, width), lambda i: (jnp.maximum(i + self.first - self.nct, 0), 0))
        return [ctx, lat]

    def mod(self, k, row_ctx, row_lat):
        def index(i):
            t = i + self.first
            lat_row = row_lat + (jnp.maximum(t - self.nct, 0) * self.tm) // self.l_lat
            return (jnp.where(t < self.nct, row_ctx, lat_row), 0, k)
        return pl.BlockSpec((None, 1, D_MODEL), index)


def _seq_spec(l, w, ng=None, seq0=0):
    step0 = seq0 // (ng or 1)
    return pl.BlockSpec((ng, l, w), lambda b: (b + step0, 0, 0))


def _head_masks(width, heads):
    lane = lax.broadcasted_iota(jnp.int32, (1, width), 1)
    hd = width // heads
    return [(lane >= h * hd) & (lane < (h + 1) * hd) for h in range(heads)]


def _by_head(masks, vals):
    out = vals[-1]
    for m, v in zip(masks[-2::-1], vals[-2::-1]):
        out = jnp.where(m, v, out)
    return out


def _block_diag(n, blk, value):
    i = lax.broadcasted_iota(jnp.int32, (n, n), 0) // blk
    j = lax.broadcasted_iota(jnp.int32, (n, n), 1) // blk
    return jnp.where(i == j, value, 0.0).astype(F32)


def _mod_kernel(c_ref, w_ref, b_ref, o_ref):
    c = c_ref[...]
    o_ref[...] = _bdot(_silu(c), w_ref[...]) + b_ref[...]


def _modulation(cond, w_mod, b_mod):
    out = pl.pallas_call(
        _mod_kernel,
        grid=(DEPTH, N_MOD),
        in_specs=[pl.BlockSpec((MOD_ROWS, D_MODEL), lambda l, j: (0, 0)),
                  pl.BlockSpec((None, D_MODEL, D_MODEL), lambda l, j: (l, 0, j)),
                  pl.BlockSpec((None, 1, D_MODEL), lambda l, j: (l, 0, j))],
        out_specs=pl.BlockSpec((None, MOD_ROWS, D_MODEL), lambda l, j: (l, 0, j)),
        out_shape=jax.ShapeDtypeStruct((DEPTH, MOD_ROWS, N_MOD * D_MODEL), F32),
        compiler_params=_cparams("arbitrary", "arbitrary"),
        name="modulation",
    )(cond, w_mod, b_mod.reshape(DEPTH, 1, N_MOD * D_MODEL))
    return out.reshape(DEPTH * MOD_ROWS, 1, N_MOD * D_MODEL)


def _ffn_kernel(n_mix, pair_x, pair_mix, nct, first, *refs):
    refs = list(refs)
    o_ref = refs.pop()
    is_ctx = pl.program_id(0) + first < nct

    def read(pair):
        if pair:
            a, b = refs.pop(0), refs.pop(0)
            return jnp.where(is_ctx, a[...], b[...])
        return refs.pop(0)[...]

    x = read(pair_x)
    sh_ref, sc_ref, g_ref, nw_ref, wi_ref, wo_ref = (refs.pop(0) for _ in range(6))
    if n_mix:
        gm_ref = refs.pop(0)
        ys = [read(pair_mix) for _ in range(n_mix)]
        wm_ref = refs.pop(0)
        w = D_MIX // n_mix
        acc = _bdot(ys[0], wm_ref[0:w, :])
        for j in range(1, n_mix):
            acc += _bdot(ys[j], wm_ref[j * w:(j + 1) * w, :])
        x = x + gm_ref[...] * acc
    h = (_rms(x, nw_ref[...] * (1.0 + sc_ref[...])) + sh_ref[...]).astype(BF16)
    gate = jnp.dot(h, wi_ref[:, 0:D_FF], preferred_element_type=F32)
    up = jnp.dot(h, wi_ref[:, D_FF:2 * D_FF], preferred_element_type=F32)
    o_ref[...] = x + 0.5 * g_ref[...] * _bdot(_silu(gate) * up, wo_ref[...])


def _ffn(st, x, mod, wts, li, k, mix=()):
    row_ctx, row_lat = li * MOD_ROWS, li * MOD_ROWS + 1
    w_in, w_out = wts["ffn_w_in"], wts["ffn_w_out"]
    pair_x = isinstance(x, tuple)
    pair_mix = bool(mix) and isinstance(mix[0], tuple)
    args = list(x) if pair_x else [x]
    in_specs = st.pair(D_MODEL) if pair_x else [st.merged(D_MODEL)]
    args += [mod, mod, mod, wts["norm_w"], w_in, w_out]
    in_specs += [st.mod(6 * k + j, row_ctx, row_lat) for j in range(3)]
    in_specs += [_layer_const(wts["norm_w"], 3 * li + 2 * k), _layer_const(w_in, 2 * li + k),
                 _layer_const(w_out, 2 * li + k)]
    if mix:
        args.append(mod)
        in_specs.append(st.mod(5, row_ctx, row_lat))
        for y in mix:
            width = (y[0] if pair_mix else y).shape[1]
            args += list(y) if pair_mix else [y]
            in_specs += st.pair(width) if pair_mix else [st.owned(width)]
        args.append(wts["mix_w_out"])
        in_specs.append(_layer_const(wts["mix_w_out"], li))
    return pl.pallas_call(
        functools.partial(_ffn_kernel, len(mix), pair_x, pair_mix, st.nct, st.first),
        grid=(st.count,),
        in_specs=in_specs,
        out_specs=st.owned(D_MODEL),
        out_shape=jax.ShapeDtypeStruct((st.tokens, D_MODEL), F32),
        compiler_params=_cparams("parallel"),
        name="ffn",
    )(*args)


_IN_SPLITS = (("z", D_SSM, BF16), ("xbc", SSM_CONV_CH, F32), ("hy", HY_IN, F32), ("ret", RET_IN, BF16),
              ("att", ATT_IN, BF16), ("dt", DT_PAD, F32))


C_DT = D_SSM + SSM_CONV_CH
N_DT = 2 * SSM_HEADS
W_TAIL = HY_IN + RET_IN + ATT_IN


def _inproj_kernel(x_ref, sh_ref, sc_ref, nw_ref, w_ref, wdt_ref, *rest):
    o_refs, wt_scr = rest[:-1], rest[-1]

    @pl.when(pl.program_id(0) == 0)
    def _():
        wt_scr[...] = w_ref[:, C_DT + N_DT:C_DT + N_DT + W_TAIL]

    h = (_rms(x_ref[...], nw_ref[...] * (1.0 + sc_ref[...])) + sh_ref[...]).astype(BF16)
    off = 0
    for (name, width, dtype), o_ref in zip(_IN_SPLITS, o_refs):
        if name == "dt":
            w = wdt_ref[...]
        elif off < C_DT:
            w = w_ref[:, off:off + width]
        else:
            w = wt_scr[:, off - C_DT:off - C_DT + width]
        o_ref[...] = jnp.dot(h, w, preferred_element_type=F32).astype(dtype)
        off += width


def _inproj(st, x, mod, wts, li):
    row_ctx, row_lat = li * MOD_ROWS, li * MOD_ROWS + 1
    return pl.pallas_call(
        _inproj_kernel,
        grid=(st.count,),
        in_specs=[st.merged(D_MODEL), st.mod(3, row_ctx, row_lat), st.mod(4, row_ctx, row_lat),
                  _layer_const(wts["norm_w"], 3 * li + 1),
                  _layer_const(wts["mix_w_in"], li), _layer_const(wts["mix_w_dt"], li)],
        out_specs=[st.owned(width) for _, width, _ in _IN_SPLITS],
        out_shape=[jax.ShapeDtypeStruct((st.tokens, width), dtype) for _, width, dtype in _IN_SPLITS],
        scratch_shapes=[pltpu.VMEM((D_MODEL, W_TAIL), BF16)],
        compiler_params=_cparams("arbitrary"),
        name="mix_in",
    )(x, mod, mod, wts["norm_w"], wts["mix_w_in"], wts["mix_w_dt"])


def _conv3_chunk(x_ref, c, nc, w, b):
    q = CHUNK
    l = nc * q
    r0 = pl.multiple_of(c * q, q)
    x = x_ref[pl.ds(r0, q), :]
    prev = x_ref[pl.ds(jnp.maximum(r0 - 1, 0), 1), :]
    nxt = x_ref[pl.ds(jnp.minimum(r0 + q, l - 1), 1), :]
    prev = jnp.where(c > 0, prev, 0.0)
    nxt = jnp.where(c < nc - 1, nxt, 0.0)
    rid = lax.broadcasted_iota(jnp.int32, (q, 1), 0)
    xm1 = jnp.where(rid == 0, prev, pltpu.roll(x, 1, 0))
    xp1 = jnp.where(rid == q - 1, nxt, pltpu.roll(x, q - 1, 0))
    return xm1 * w[0:1, :] + x * w[1:2, :] + xp1 * w[2:3, :] + b


def _tri(lower):
    i = lax.broadcasted_iota(jnp.int32, (CHUNK, CHUNK), 0)
    j = lax.broadcasted_iota(jnp.int32, (CHUNK, CHUNK), 1)
    return (j <= i) if lower else (j >= i)


def _rows(c):
    return pl.ds(pl.multiple_of(c * CHUNK, CHUNK), CHUNK)


def _split_cat(v, parts, axis):
    out, r = [], v
    for i in range(parts):
        piece = r.astype(BF16)
        out.append(piece)
        if i + 1 < parts:
            r = r - piece.astype(F32)
    return jnp.concatenate(out, axis=axis)


def _seg_mean(x, seg):
    w = x.shape[-1]
    ones = _block_diag(w, seg, 1.0).astype(BF16)
    return jnp.dot(_split_cat(x, 2, axis=1), jnp.concatenate([ones, ones], axis=0),
                   preferred_element_type=F32) * (1.0 / seg)


def _stack_heads(x, masks):
    return jnp.concatenate([jnp.where(m, x, 0.0) for m in masks], axis=0)


def _slot_out(tail, nb, li, stacked, ng, whole):
    zeros = (0,) * len(tail)
    slot = li if stacked else 0
    if whole:
        spec = pl.BlockSpec((ng, DEPTH) + tail, lambda b: (b, 0) + zeros)
    else:
        spec = pl.BlockSpec((ng, None) + tail, lambda b: (b, slot) + zeros)
    return spec, jax.ShapeDtypeStruct((nb, DEPTH if stacked else 1) + tail, F32)


def _slot_view(ref, g, slot):
    if slot is None:
        return ref.at[g]
    for other in range(DEPTH):
        if other != slot:
            ref[g, other] = jnp.zeros(ref.shape[2:], ref.dtype)
    return ref.at[g, slot]


def _carry_prev(args, in_specs, prevs, first_out):
    aliases = {}
    for j, prev in enumerate(prevs):
        aliases[len(args)] = first_out + j
        args.append(prev)
        in_specs.append(pl.BlockSpec(memory_space=pl.ANY))
    return aliases


SSD_SEL_W = 2 * SSM_HEADS * SSM_STATE
CUM_PIECES = 3
WGT_PIECES = 2
DT_REP = CUM_PIECES + WGT_PIECES


def _ssd_kernel(nc, ng, has_s0, slot, *refs):
    refs = list(refs)
    (z_ref, xbc_ref, dt_ref, cw_ref, cb_ref, dtb_ref, alog_ref, dsk_ref, nw_ref) = refs[:9]
    s0_ref = refs[9] if has_s0 else None
    (y_ref, s_ref, xs_scr, xk_scr, xt_scr, dt_scr, cum_scr, dtt_scr, cumt_scr, yf_scr, yb_scr, st_scr,
     sel_scr) = refs[-13:]
    q = CHUNK
    nh, n, p = SSM_HEADS, SSM_STATE, SSM_HEAD_DIM
    rep = nh // SSM_GROUPS
    hm = _head_masks(D_SSM, nh)

    @pl.when(pl.program_id(0) == 0)
    def _():
        k = lax.broadcasted_iota(jnp.int32, (LANES, SSD_SEL_W), 0)
        col = lax.broadcasted_iota(jnp.int32, (LANES, SSD_SEL_W), 1)
        is_cum = col < nh * n
        grp = k // (2 * nh)
        grp_ok = (is_cum & (grp < CUM_PIECES)) | ((~is_cum) & (grp >= CUM_PIECES) & (grp < DT_REP))
        hit = grp_ok & (k % nh == (col % (nh * n)) // n)
        sel_scr[...] = jnp.where(hit, 1.0, 0.0).astype(BF16)

    cw = cw_ref[...]
    cb = cb_ref[...]
    neg_a = -jnp.exp(alog_ref[...])
    dtb = dtb_ref[...]

    class Seq:
        def __init__(self, g):
            self.z, self.xbc, self.dt_in, self.y, self.s = (z_ref.at[g], xbc_ref.at[g], dt_ref.at[g], y_ref.at[g],
                                                            _slot_view(s_ref, g, slot))
            self.s0 = s0_ref.at[g] if has_s0 else None
            self.xs, self.xk, self.xt, self.dt, self.cum = (xs_scr.at[g], xk_scr.at[g], xt_scr.at[g], dt_scr.at[g],
                                                            cum_scr.at[g])
            self.dtt, self.cumt, self.yf, self.yb, self.st = (dtt_scr.at[g], cumt_scr.at[g], yf_scr.at[g],
                                                              yb_scr.at[g], st_scr.at[g])

    seqs = [Seq(g) for g in range(ng)]

    def prep_seq(sq, c):
        rows = _rows(c)
        xall = _silu(_conv3_chunk(sq.xbc, c, nc, cw, cb))
        sq.xs[rows, :] = xall
        xs = xall[:, 0:D_SSM]
        sq.xk[c] = _stack_heads(xs, hm).astype(BF16)
        sq.xt[c] = xs.T.astype(BF16)
        dt = _softplus(sq.dt_in[rows, :] + dtb)
        la = _split_cat(dt * neg_a, 3, axis=0)
        tri = jnp.concatenate([_tri(True), _tri(False)], axis=0)
        tri = jnp.where(tri, 1.0, 0.0).astype(BF16)
        cs = jnp.dot(jnp.concatenate([tri] * 3, axis=1), la, preferred_element_type=F32)
        lane = lax.broadcasted_iota(jnp.int32, (1, LANES), 1)
        cum = jnp.where(lane % (2 * nh) < nh, cs[0:q], cs[q:2 * q])
        sq.dt[rows, :] = dt
        sq.cum[rows, :] = cum
        sq.dtt[c] = dt.T
        sq.cumt[c] = cum.T

    def prep(c, carry):
        for sq in seqs:
            prep_seq(sq, c)
        return carry

    lax.fori_loop(0, nc, prep, 0)

    for sq in seqs:
        for d in range(2):
            for h in range(nh):
                blk = slice(h * p, (h + 1) * p)
                sq.st[d, blk, :] = sq.s0[d, h].T if has_s0 else jnp.zeros((p, n), F32)

    def pack_scalars(sq, c, d):
        rows = _rows(c)
        dt = sq.dt[rows, :]
        cum = sq.cum[rows, :]
        edge = q - 1 if d == 0 else 0
        lane = lax.broadcasted_iota(jnp.int32, (1, LANES), 1)
        used = (lane < DT_REP * 2 * nh) & ((lane % (2 * nh)) // nh == d)
        cum = jnp.where(used, cum, 0.0)
        wgt = jnp.exp(cum[edge:edge + 1, :] - cum) * dt
        grp = lane // (2 * nh)
        packed = jnp.zeros_like(cum)
        for src, pieces, g0 in ((cum, CUM_PIECES, 0), (wgt, WGT_PIECES, CUM_PIECES)):
            rest = src
            for i in range(pieces):
                piece = rest.astype(BF16).astype(F32)
                packed = jnp.where(grp == g0 + i, piece, packed)
                rest = rest - piece
        return jnp.where(used, packed, 0.0).astype(BF16)

    def operands(sq, c, d):
        rows = _rows(c)
        bm = [sq.xs[rows, D_SSM + g * n:D_SSM + (g + 1) * n] for g in range(SSM_GROUPS)]
        cm = [sq.xs[rows, D_SSM + (SSM_GROUPS + g) * n:D_SSM + (SSM_GROUPS + g + 1) * n]
              for g in range(SSM_GROUPS)]
        return bm, cm

    def decay_blocks(sq, c, d, e, cb_t):
        mask = _tri(d == 0)
        blocks, ecol = [], []
        for h in range(nh):
            r = d * nh + h
            col = e[:, h * n:(h + 1) * n]
            decay = jnp.exp(jnp.where(mask, col - sq.cumt[c, r:r + 1, :], -jnp.inf))
            blocks.append((cb_t[h // rep] * decay * sq.dtt[c, r:r + 1, :]).astype(BF16))
            ecol.append(jnp.exp(col))
        lo = lax.broadcasted_iota(jnp.int32, (1, n), 1) < p
        ecol = jnp.concatenate([jnp.where(lo, ecol[2 * j], ecol[2 * j + 1]) for j in range(nh // 2)], axis=1)
        return jnp.concatenate(blocks, axis=1), ecol

    def scan(i, carry):
        j = nc - 1 - i
        jobs = [(sq, c, d) for sq in seqs for c, d in ((i, 0), (j, 1))]
        packed = jnp.concatenate([pack_scalars(*job) for job in jobs], axis=0)
        e_all = jnp.dot(packed, sel_scr[...], preferred_element_type=F32)
        es = [e_all[t * q:(t + 1) * q] for t in range(len(jobs))]
        ops = [operands(*job) for job in jobs]
        cb_t = [[_bdot_nt(cm[g], bm[g]) for g in range(SSM_GROUPS)] for bm, cm in ops]
        dec = [decay_blocks(*job, es[t], cb_t[t]) for t, job in enumerate(jobs)]
        intra = [jnp.dot(dec[t][0], sq.xk[c], preferred_element_type=F32) for t, (sq, c, d) in enumerate(jobs)]
        s_old = [sq.st[d] for sq, c, d in jobs]
        inter = [jnp.concatenate([_bdot_nt(cm[g], s_old[t][g * rep * p:(g + 1) * rep * p, :])
                                  for g in range(SSM_GROUPS)], axis=1) for t, (bm, cm) in enumerate(ops)]
        for t, (sq, c, d) in enumerate(jobs):
            bm = ops[t][0]
            edge = q - 1 if d == 0 else 0
            for h in range(nh):
                blk, cols = slice(h * p, (h + 1) * p), slice(h * n, (h + 1) * n)
                wk = (bm[h // rep] * es[t][:, nh * n + h * n:nh * n + (h + 1) * n]).astype(BF16)
                upd = jnp.dot(sq.xt[c, blk, :], wk, preferred_element_type=F32)
                sq.st[d, blk, :] = jnp.exp(es[t][edge:edge + 1, cols]) * s_old[t][blk, :] + upd
            y = intra[t] + dec[t][1] * inter[t]
            if d == 0:
                sq.yf[_rows(c), :] = y
            else:
                sq.yb[_rows(c), :] = y
        return carry

    lax.fori_loop(0, nc, scan, 0)

    for sq in seqs:
        for d in range(2):
            for h in range(nh):
                sq.s[d, h] = sq.st[d, h * p:(h + 1) * p, :].T

    dsk = dsk_ref[...]
    nw = nw_ref[...]

    def fin(c, carry):
        rows = _rows(c)
        for sq in seqs:
            y = sq.yf[rows, :] + sq.yb[rows, :] + dsk * sq.xs[rows, 0:D_SSM]
            sq.y[rows, :] = _rms(y * _silu(sq.z[rows, :].astype(F32)), nw)
        return carry

    lax.fori_loop(0, nc, fin, 0)


def _ssd(z, xbc, dt, wts, li, s0, nb, l, seq0, stacked=False, prev=None):
    nc = l // CHUNK
    has_s0 = s0 is not None
    names = ("ssd_conv_w", "ssd_conv_b", "ssd_dt_bias", "ssd_a_log", "ssd_d", "ssd_norm_w")
    ng = SEQ_GROUP
    args = [z.reshape(-1, l, D_SSM), xbc.reshape(-1, l, SSM_CONV_CH), dt.reshape(-1, l, DT_PAD)]
    args += [wts[n] for n in names]
    in_specs = [_seq_spec(l, D_SSM, ng, seq0), _seq_spec(l, SSM_CONV_CH, ng, seq0), _seq_spec(l, DT_PAD, ng, seq0)]
    in_specs += [_layer_const(wts[n], li) for n in names]
    if has_s0:
        args.append(s0)
        in_specs.append(pl.BlockSpec((ng, None, 2, SSM_HEADS, SSM_STATE, SSM_HEAD_DIM),
                                     lambda b: (b, li, 0, 0, 0, 0)))
    whole = stacked and prev is None
    st_spec, st_shape = _slot_out((2, SSM_HEADS, SSM_STATE, SSM_HEAD_DIM), nb, li, stacked, ng, whole)
    aliases = _carry_prev(args, in_specs, [] if prev is None else [prev], 1)
    y, s = pl.pallas_call(
        functools.partial(_ssd_kernel, nc, ng, has_s0, li if whole else None),
        grid=(nb // ng,),
        in_specs=in_specs,
        out_specs=[_seq_spec(l, D_SSM, ng), st_spec],
        out_shape=[jax.ShapeDtypeStruct((nb, l, D_SSM), F32), st_shape],
        input_output_aliases=aliases,
        scratch_shapes=[pltpu.VMEM((ng, l, SSM_CONV_CH), F32),
                        pltpu.VMEM((ng, nc, SSM_HEADS * CHUNK, D_SSM), BF16),
                        pltpu.VMEM((ng, nc, D_SSM, CHUNK), BF16),
                        pltpu.VMEM((ng, l, DT_PAD), F32), pltpu.VMEM((ng, l, DT_PAD), F32),
                        pltpu.VMEM((ng, nc, DT_PAD, CHUNK), F32), pltpu.VMEM((ng, nc, DT_PAD, CHUNK), F32),
                        pltpu.VMEM((ng, l, D_SSM), F32), pltpu.VMEM((ng, l, D_SSM), F32),
                        pltpu.VMEM((ng, 2, D_SSM, SSM_STATE), F32),
                        pltpu.VMEM((LANES, SSD_SEL_W), BF16)],
        compiler_params=_cparams("arbitrary"),
        name="ssd",
    )(*args)
    return y.reshape(nb * l, D_SSM), s


def _ret_kernel(nc, ng, has_s0, slot, *refs):
    refs = list(refs)
    u_ref, dl_ref, gn_ref = refs[:3]
    s0_ref = refs[3] if has_s0 else None
    y_ref, s_ref, yf_scr, yb_scr, st_scr, dm_scr, e_scr, ea_scr, bd_scr = refs[-9:]
    q = CHUNK
    hd = RET_HEAD_DIM
    hm = _head_masks(D_RET, RET_HEADS)

    @pl.when(pl.program_id(0) == 0)
    def _():
        log_g = -_softplus(-dl_ref[...])
        ii = lax.broadcasted_iota(jnp.int32, (q, q), 0)
        jj = lax.broadcasted_iota(jnp.int32, (q, q), 1)
        dij = (ii - jj).astype(F32)
        ri = lax.broadcasted_iota(jnp.int32, (q, 1), 0).astype(F32)
        lfs, lbs = [], []
        for h in range(RET_HEADS):
            lf = log_g[:, h:h + 1]
            lb = log_g[:, RET_HEADS + h:RET_HEADS + h + 1]
            lfs.append(lf)
            lbs.append(lb)
            d_f = jnp.exp(jnp.where(dij >= 0, dij * lf, -jnp.inf))
            d_b = jnp.exp(jnp.where(dij <= 0, -dij * lb, -jnp.inf))
            dm_scr[:, h * q:(h + 1) * q] = d_f + d_b
        lf_l = _by_head(hm, lfs)
        lb_l = _by_head(hm, lbs)
        e_scr[0] = jnp.exp((ri + 1.0) * lf_l)
        e_scr[1] = jnp.exp((q - ri) * lb_l)
        e_scr[2] = jnp.exp((q - 1.0 - ri) * lf_l)
        e_scr[3] = jnp.exp(ri * lb_l)
        ea_scr[0:1, :] = jnp.exp(q * lf_l)
        ea_scr[1:2, :] = jnp.exp(q * lb_l)
        bd_scr[...] = _block_diag(D_RET, hd, 1.0)

    class Seq:
        def __init__(self, g):
            self.u, self.y, self.s = u_ref.at[g], y_ref.at[g], _slot_view(s_ref, g, slot)
            self.s0 = s0_ref.at[g] if has_s0 else None
            self.yf, self.yb, self.st = yf_scr.at[g], yb_scr.at[g], st_scr.at[g]

    seqs = [Seq(g) for g in range(ng)]

    for sq in seqs:
        for d in range(2):
            for h in range(RET_HEADS):
                blk = slice(h * hd, (h + 1) * hd)
                if has_s0:
                    parts = [jnp.zeros((hd, hd), F32)] * RET_HEADS
                    parts[h] = sq.s0[d, h]
                    sq.st[d, blk, :] = jnp.concatenate(parts, axis=1)
                else:
                    sq.st[d, blk, :] = jnp.zeros((hd, D_RET), F32)

    def qkv(sq, rows):
        return (sq.u[rows, 0:D_RET], sq.u[rows, D_RET:2 * D_RET].astype(F32) * (RET_HEAD_DIM ** -0.5),
                sq.u[rows, 2 * D_RET:3 * D_RET])

    def state_step(sq, d, qq, kk, vv):
        s_old = sq.st[d]
        y = e_scr[d] * _bdot(qq, s_old)
        sq.st[d] = ea_scr[d:d + 1, :] * s_old + _bdot_tn(kk * e_scr[2 + d], vv) * bd_scr[...]
        return y

    def scan(i, carry):
        rows = _rows(i)
        rows_b = _rows(nc - 1 - i)
        fwd = [qkv(sq, rows) for sq in seqs]
        bwd = [qkv(sq, rows_b) for sq in seqs]
        sc = [_bdot_nt(qq, _stack_heads(kk, hm)) * dm_scr[...] for qq, kk, _ in fwd]
        intra = [_bdot(sc[g], _stack_heads(fwd[g][2], hm)) for g in range(ng)]
        inter = [state_step(sq, 0, *fwd[g]) for g, sq in enumerate(seqs)]
        back = [state_step(sq, 1, *bwd[g]) for g, sq in enumerate(seqs)]
        for g, sq in enumerate(seqs):
            sq.yf[rows, :] = intra[g] + inter[g]
            sq.yb[rows_b, :] = back[g]
        return carry

    lax.fori_loop(0, nc, scan, 0)

    for sq in seqs:
        for d in range(2):
            for h in range(RET_HEADS):
                sq.s[d, h] = sq.st[d, h * hd:(h + 1) * hd, h * hd:(h + 1) * hd]

    gn = gn_ref[...]

    def fin(c, carry):
        rows = _rows(c)
        for sq in seqs:
            y = sq.yf[rows, :] + sq.yb[rows, :]
            cen = y - _seg_mean(y, hd)
            var = _seg_mean(cen * cen, hd)
            sq.y[rows, :] = cen * lax.rsqrt(var + NORM_EPS) * gn * _silu(sq.u[rows, 3 * D_RET:4 * D_RET].astype(F32))
        return carry

    lax.fori_loop(0, nc, fin, 0)


def _retention(u, wts, li, s0, nb, l, seq0, stacked=False, prev=None):
    nc = l // CHUNK
    has_s0 = s0 is not None
    names = ("ret_decay_logit", "ret_gn_w")
    ng = SEQ_GROUP
    args = [u.reshape(-1, l, RET_IN)] + [wts[n] for n in names]
    in_specs = [_seq_spec(l, RET_IN, ng, seq0)] + [_layer_const(wts[n], li) for n in names]
    if has_s0:
        args.append(s0)
        in_specs.append(pl.BlockSpec((ng, None, 2, RET_HEADS, RET_HEAD_DIM, RET_HEAD_DIM),
                                     lambda b: (b, li, 0, 0, 0, 0)))
    whole = stacked and prev is None
    st_spec, st_shape = _slot_out((2, RET_HEADS, RET_HEAD_DIM, RET_HEAD_DIM), nb, li, stacked, ng, whole)
    aliases = _carry_prev(args, in_specs, [] if prev is None else [prev], 1)
    y, s = pl.pallas_call(
        functools.partial(_ret_kernel, nc, ng, has_s0, li if whole else None),
        grid=(nb // ng,),
        in_specs=in_specs,
        out_specs=[_seq_spec(l, D_RET, ng), st_spec],
        out_shape=[jax.ShapeDtypeStruct((nb, l, D_RET), F32), st_shape],
        input_output_aliases=aliases,
        scratch_shapes=[pltpu.VMEM((ng, l, D_RET), F32), pltpu.VMEM((ng, l, D_RET), F32),
                        pltpu.VMEM((ng, 2, D_RET, D_RET), F32), pltpu.VMEM((CHUNK, RET_HEADS * CHUNK), F32),
                        pltpu.VMEM((4, CHUNK, D_RET), F32), pltpu.VMEM((8, D_RET), F32),
                        pltpu.VMEM((D_RET, D_RET), F32)],
        compiler_params=_cparams("arbitrary"),
        name="retention",
    )(*args)
    return y.reshape(nb * l, D_RET), s


def _split(x):
    hi = x.astype(BF16)
    return hi, (x - hi.astype(F32)).astype(BF16)


def _dot3(a_hi, a_lo, b_hi, b_lo):
    d = lambda p, q: jnp.dot(p, q, preferred_element_type=F32)
    return d(a_hi, b_hi) + (d(a_lo, b_hi) + d(a_hi, b_lo))


@functools.lru_cache(maxsize=None)
def _dft_fwd_host(l):
    n = 2 * l
    f = np.arange(l, dtype=np.int64)[:, None]
    s = np.arange(l, dtype=np.int64)[None, :]
    ang = ((f * s) % n).astype(np.float64) * (2.0 * math.pi / n)
    im = -np.sin(ang)
    im[0] = np.where(np.arange(l) % 2 == 0, 1.0, -1.0)
    return np.concatenate([np.cos(ang), im], axis=0).astype(np.float32)


def _dft_tables(l):
    n = 2 * l
    fwd = jnp.asarray(_dft_fwd_host(l))
    wgt = np.full((n, 1), 2.0 / n, np.float32)
    wgt[0] = wgt[l] = 1.0 / n
    return _split(fwd) + _split((fwd * wgt).T)


def _dft_block_tables(tables, l):
    fwd_hi, fwd_lo, inv_hi, inv_lo = tables
    r = min(DFT_BLOCK, l)
    nblk = l // r
    fb = lambda t: (t[0:l].reshape(nblk, r, l), t[l:2 * l].reshape(nblk, r, l))
    fwd_blk = jnp.concatenate(fb(fwd_hi) + fb(fwd_lo), axis=1)
    inv_blk = jnp.concatenate([inv_hi.reshape(nblk, r, 2 * l), inv_lo.reshape(nblk, r, 2 * l)], axis=1)
    return fwd_blk, inv_blk


def _hy_filter_kernel(l, feats_ref, dec_ref, w1_ref, b1_ref, w2_ref, b2_ref, w3_ref, fr_ref, fh_ref, fl_ref,
                      a_ref, b_ref, d_ref):
    fr = fr_ref[...]
    xdot = lambda a, b: _dot3(*_split(a), *_split(b))
    h = jnp.sin(fr * (xdot(feats_ref[...], w1_ref[...]) + b1_ref[...]))
    h = jnp.sin(fr * (xdot(h, w2_ref[...]) + b2_ref[...]))
    h = xdot(h, w3_ref[...])
    dec = jnp.concatenate([dec_ref[...]] * HY_ORDER, axis=-1)
    row0 = lax.broadcasted_iota(jnp.int32, (l, 1), 0) == 0
    hf = h[:, 0:HY_ORDER * D_HY] * dec
    hb = h[:, HY_ORDER * D_HY:2 * HY_ORDER * D_HY] * dec
    hb = jnp.where(row0, 0.0, hb)
    hs = _split(hf + hb)
    hd = _split(hf - hb)
    re = _dot3(fh_ref[0:l, :], fl_ref[0:l, :], *hs)
    ny = _dot3(fh_ref[l:l + 8, :], fl_ref[l:l + 8, :], *hs)[0:1]
    im = _dot3(fh_ref[l:2 * l, :], fl_ref[l:2 * l, :], *hd)
    for o in range(HY_ORDER):
        cols = slice(o * D_HY, (o + 1) * D_HY)
        a_ref[o] = re[:, cols]
        b_ref[o] = jnp.where(row0, 0.0, im[:, cols])
        d_ref[o] = jnp.where(row0, ny[:, cols], re[:, cols])


def _hy_filter(l, fwd_hi, fwd_lo, wts, li):
    pos = np.arange(l, dtype=np.float32)
    t = pos / np.float32(l - 1)
    bands = np.linspace(1e-4, HY_BANDS - 1, HY_BANDS, dtype=np.float32)
    ang = np.float32(2.0 * math.pi / l) * pos[:, None] * bands[None, :]
    feats = np.concatenate([t[:, None], np.cos(ang), -np.sin(ang)], axis=-1).astype(np.float32)
    feats = np.pad(feats, ((0, 0), (0, LANES - HY_EMB)))
    max_decay = math.log(HY_TARGET) / HY_FAST_DECAY
    min_decay = math.log(HY_TARGET) / HY_SLOW_DECAY
    deltas = np.abs(np.linspace(min_decay, max_decay, D_HY, dtype=np.float32))
    dec = np.exp(-t[:, None] * deltas[None, :]).astype(np.float32)
    spec = jax.ShapeDtypeStruct((HY_ORDER, l, D_HY), F32)
    names = ("hy_w1", "hy_b1", "hy_w2", "hy_b2", "hy_w3", "hy_freq")
    return pl.pallas_call(
        functools.partial(_hy_filter_kernel, l),
        grid=(1,),
        in_specs=[_full(feats.shape), _full(dec.shape)] + [_layer_const(wts[n], li) for n in names]
                 + [_full(fwd_hi.shape), _full(fwd_lo.shape)],
        out_specs=[_full(spec.shape)] * 3,
        out_shape=[spec] * 3,
        compiler_params=_cparams("arbitrary"),
        name="hyena_filter",
    )(jnp.asarray(feats), jnp.asarray(dec), *[wts[n] for n in names], fwd_hi, fwd_lo)


def _hy_kernel(nc, ng, u_ref, cw_ref, cb_ref, f_ref, g_ref, a_ref, b_ref, d_ref, bias_ref, y_ref,
               uc_scr, vh_scr, vl_scr, sh_scr, sl_scr, z_scr):
    l = nc * CHUNK
    r = min(DFT_BLOCK, l)
    nblk = l // r
    cw = cw_ref[...]
    cb = cb_ref[...]
    dot = lambda p, q: jnp.dot(p, q, preferred_element_type=F32)
    wide = lambda x: jnp.concatenate([x] * ng, axis=1)
    seq_cols = lambda g: slice(g * D_HY, (g + 1) * D_HY)

    def for_blocks(body):
        if nblk == 1:
            body(0)
        else:
            lax.fori_loop(0, nblk, lambda i, carry: (body(i), carry)[1], 0)

    def conv(c, carry):
        rows = _rows(c)
        for g in range(ng):
            uc = _conv3_chunk(u_ref.at[g], c, nc, cw, cb)
            uc_scr[g, rows, :] = uc
            vh_scr[rows, seq_cols(g)], vl_scr[rows, seq_cols(g)] = _split(uc[:, 0:D_HY])
        return carry

    lax.fori_loop(0, nc, conv, 0)

    def long_conv(o):
        def spectrum(i):
            rows = pl.ds(pl.multiple_of(i * r, r), r)
            rows_im = pl.ds(pl.multiple_of(l + i * r, r), r)
            p = dot(f_ref[i], vh_scr[...])
            pl_ = dot(f_ref[i, 0:2 * r, :], vl_scr[...])
            zr = p[0:r] + (p[2 * r:3 * r] + pl_[0:r])
            zi = p[r:2 * r] + (p[3 * r:4 * r] + pl_[r:2 * r])
            fa, fb, fd = wide(a_ref[o, rows, :]), wide(b_ref[o, rows, :]), wide(d_ref[o, rows, :])
            sh_scr[rows, :], sl_scr[rows, :] = _split(zr * fa - zi * fb)
            sh_scr[rows_im, :], sl_scr[rows_im, :] = _split(zr * fb + zi * fd)

        for_blocks(spectrum)

        def inverse(i):
            rows = pl.ds(pl.multiple_of(i * r, r), r)
            p = dot(g_ref[i], sh_scr[...])
            y = p[0:r] + (p[r:2 * r] + dot(g_ref[i, 0:r, :], sl_scr[...]))
            gate = jnp.concatenate([uc_scr[g, rows, (o + 1) * D_HY:(o + 2) * D_HY] for g in range(ng)], axis=1)
            if o == 0:
                v = jnp.concatenate([uc_scr[g, rows, 0:D_HY] for g in range(ng)], axis=1)
            else:
                v = z_scr[rows, :]
            out = gate * (y + v * wide(bias_ref[o:o + 1, :]))
            if o + 1 < HY_ORDER:
                z_scr[rows, :] = out
                vh_scr[rows, :], vl_scr[rows, :] = _split(out)
            else:
                for g in range(ng):
                    y_ref[g, rows, :] = out[:, seq_cols(g)]

        for_blocks(inverse)

    for o in range(HY_ORDER):
        long_conv(o)


def _hyena(u, wts, li, tables, spectra, nb, l, seq0):
    nc = l // CHUNK
    ng = min(HY_GROUP, nb)
    a, b, d = spectra
    names = ("hy_conv_w", "hy_conv_b")
    y = pl.pallas_call(
        functools.partial(_hy_kernel, nc, ng),
        grid=(nb // ng,),
        in_specs=[_seq_spec(l, HY_IN, ng, seq0)] + [_layer_const(wts[n], li) for n in names]
                 + [_const(tables[0].shape), _const(tables[1].shape),
                    _const((HY_ORDER, l, D_HY)), _const((HY_ORDER, l, D_HY)), _const((HY_ORDER, l, D_HY)),
                    _layer_const(wts["hy_bias"], li)],
        out_specs=pl.BlockSpec((ng, l, D_HY), lambda i: (i, 0, 0)),
        out_shape=jax.ShapeDtypeStruct((nb, l, D_HY), F32),
        scratch_shapes=[pltpu.VMEM((ng, l, HY_IN), F32),
                        pltpu.VMEM((l, ng * D_HY), BF16), pltpu.VMEM((l, ng * D_HY), BF16),
                        pltpu.VMEM((2 * l, ng * D_HY), BF16), pltpu.VMEM((2 * l, ng * D_HY), BF16),
                        pltpu.VMEM((l, ng * D_HY), F32)],
        compiler_params=_cparams("parallel"),
        name="hyena",
    )(u.reshape(-1, l, HY_IN), *[wts[n] for n in names], *tables, a, b, d, wts["hy_bias"])
    return y.reshape(nb * l, D_HY)


def _seg_rms(x, w):
    return x * lax.rsqrt(_seg_mean(x * x, HEAD_DIM) + NORM_EPS) * w


def _stack_q(q):
    lo = lax.broadcasted_iota(jnp.int32, (1, D_KV), 1) < HEAD_DIM
    qa = q[:, 0:D_KV]
    qb = q[:, D_KV:2 * D_KV]
    return jnp.concatenate([jnp.where(lo, qa, 0.0), jnp.where(lo, pltpu.roll(qa, HEAD_DIM, 1), 0.0),
                            jnp.where(lo, 0.0, pltpu.roll(qb, HEAD_DIM, 1)), jnp.where(lo, 0.0, qb)], axis=0)


def _unstack_o(o):
    r = o.shape[0] // ATT_HEADS
    lo = lax.broadcasted_iota(jnp.int32, (1, D_KV), 1) < HEAD_DIM
    ya = jnp.where(lo, o[0:r], pltpu.roll(o[r:2 * r], HEAD_DIM, 1))
    yb = jnp.where(lo, pltpu.roll(o[2 * r:3 * r], HEAD_DIM, 1), o[3 * r:4 * r])
    return jnp.concatenate([ya, yb], axis=1)


def _sink_col(sink_ref, li, r):
    rb = lax.broadcasted_iota(jnp.int32, (ATT_HEADS * r, 1), 0) // r
    col = jnp.full((ATT_HEADS * r, 1), sink_ref[li * ATT_HEADS + ATT_HEADS - 1], F32)
    for h in range(ATT_HEADS - 2, -1, -1):
        col = jnp.where(rb == h, sink_ref[li * ATT_HEADS + h], col)
    return col


def _ctx_attn_kernel(li, ng, slot, u_ref, qn_ref, kn_ref, sink_ref, *rest):
    y_ref, k_ref, v_ref = rest[-3:]
    l = u_ref.shape[1]
    sink = _sink_col(sink_ref, li, l)
    seqs = range(ng)
    u = [u_ref[g].astype(F32) for g in seqs]
    q = [_seg_rms(u[g][:, 0:D_ATT], qn_ref[...]) for g in seqs]
    k = [_seg_rms(u[g][:, D_ATT:D_ATT + D_KV], kn_ref[:, 0:D_KV]) for g in seqs]
    v = [u[g][:, D_ATT + D_KV:D_ATT + 2 * D_KV] for g in seqs]
    for g in seqs:
        _slot_view(k_ref, g, slot)[...] = k[g]
        _slot_view(v_ref, g, slot)[...] = v[g]
    s = [_bdot_nt(_stack_q(q[g]), k[g]) * (HEAD_DIM ** -0.5) for g in seqs]
    m = [jnp.maximum(jnp.max(s[g], axis=-1, keepdims=True), sink) for g in seqs]
    p = [jnp.exp(s[g] - m[g]) for g in seqs]
    den = [jnp.sum(p[g], axis=-1, keepdims=True) + jnp.exp(sink - m[g]) for g in seqs]
    o = [_bdot(p[g] * (1.0 / den[g]), v[g]) for g in seqs]
    for g in seqs:
        y_ref[g] = _unstack_o(o[g])


def _ctx_attention(u, wts, li, nb, l, seq0, prev=None):
    ng = ATT_GROUP
    whole = prev is None
    kv_spec, kv_shape = _slot_out((l, D_KV), nb, li, True, ng, whole)
    args = [u.reshape(-1, l, ATT_IN), wts["attn_q_norm"], wts["attn_k_norm"], wts["attn_sink"]]
    in_specs = [_seq_spec(l, ATT_IN, ng, seq0), _layer_const(wts["attn_q_norm"], li),
                _layer_const(wts["attn_k_norm"], li), pl.BlockSpec(memory_space=pltpu.SMEM)]
    aliases = _carry_prev(args, in_specs, [] if prev is None else list(prev), 1)
    y, k, v = pl.pallas_call(
        functools.partial(_ctx_attn_kernel, li, ng, li if whole else None),
        grid=(nb // ng,),
        in_specs=in_specs,
        out_specs=[_seq_spec(l, D_ATT, ng), kv_spec, kv_spec],
        out_shape=[jax.ShapeDtypeStruct((nb, l, D_ATT), F32), kv_shape, kv_shape],
        input_output_aliases=aliases,
        compiler_params=_cparams("parallel"),
        name="ctx_attention",
    )(*args)
    return y.reshape(nb * l, D_ATT), k, v


@functools.lru_cache(maxsize=None)
def _rope_tables_host(l):
    n_rows = l // GRID_W
    rows = np.repeat(np.arange(n_rows, dtype=np.float32), GRID_W)
    cols = np.tile(np.arange(GRID_W, dtype=np.float32), n_rows)
    nf = HEAD_DIM // 4
    inv = (np.float32(ROPE_BASE) ** (-np.arange(nf, dtype=np.float32) / np.float32(nf))).astype(np.float32)
    ar = rows[:, None] * inv[None, :]
    ac = cols[:, None] * inv[None, :]
    cos = np.concatenate([np.cos(ar), np.cos(ar), np.cos(ac), np.cos(ac)], axis=-1)
    sin = np.concatenate([-np.sin(ar), np.sin(ar), -np.sin(ac), np.sin(ac)], axis=-1)
    return (np.tile(cos, (1, ATT_HEADS)).astype(np.float32), np.tile(sin, (1, ATT_HEADS)).astype(np.float32))


def _rope(x, cos, sin):
    w = x.shape[-1]
    nf = HEAD_DIM // 4
    lane = lax.broadcasted_iota(jnp.int32, x.shape, 1)
    first = (lane % (2 * nf)) < nf
    partner = jnp.where(first, pltpu.roll(x, w - nf, 1), pltpu.roll(x, nf, 1))
    return x * cos + partner * sin


def _lat_attn_kernel(nblk, li, ng, u_ref, qn_ref, kn_ref, cos_ref, sin_ref, ck_ref, cv_ref, sink_ref,
                     y_ref, q_scr, k_scr, v_scr):
    blk = ATT_BLOCK
    l = nblk * blk
    seqs = range(ng)
    cos_q = cos_ref[...]
    sin_q = sin_ref[...]
    zeros = jnp.zeros((blk, D_KV), F32)
    for g in seqs:
        u = u_ref[g].astype(F32)
        q = _seg_rms(u[:, 0:D_ATT], qn_ref[...])
        k = _seg_rms(u[:, D_ATT:D_ATT + D_KV], kn_ref[:, 0:D_KV])
        q_scr[g] = _rope(q, cos_q, sin_q)
        k_scr[g, 0:blk, :] = zeros
        k_scr[g, blk + l:2 * blk + l, :] = zeros
        v_scr[g, 0:blk, :] = zeros
        v_scr[g, blk + l:2 * blk + l, :] = zeros
        k_scr[g, blk:blk + l, :] = _rope(k, cos_q[:, 0:D_KV], sin_q[:, 0:D_KV])
        v_scr[g, blk:blk + l, :] = u[:, D_ATT + D_KV:D_ATT + 2 * D_KV]

    scale = HEAD_DIM ** -0.5
    r = lax.broadcasted_iota(jnp.int32, (blk, 3 * blk), 0)
    cidx = lax.broadcasted_iota(jnp.int32, (blk, 3 * blk), 1)
    band = (cidx - r >= blk - WINDOW) & (cidx - r <= blk + WINDOW)
    sink = _sink_col(sink_ref, li, blk)

    def block(i, carry):
        rows = _rows(i)
        win = pl.ds(pl.multiple_of(i * blk, blk), 3 * blk)
        kpos = cidx + (i - 1) * blk
        valid = band & (kpos >= 0) & (kpos < l)
        valid = jnp.concatenate([valid] * ATT_HEADS, axis=0)
        qs = [_stack_q(q_scr[g, rows, :]) for g in seqs]
        s_loc = [jnp.where(valid, _bdot_nt(qs[g], k_scr[g, win, :]) * scale, -jnp.inf) for g in seqs]
        s_ctx = [_bdot_nt(qs[g], ck_ref[g]) * scale for g in seqs]
        m = [jnp.maximum(jnp.maximum(jnp.max(s_loc[g], axis=-1, keepdims=True),
                                     jnp.max(s_ctx[g], axis=-1, keepdims=True)), sink) for g in seqs]
        p_loc = [jnp.exp(s_loc[g] - m[g]) for g in seqs]
        p_ctx = [jnp.exp(s_ctx[g] - m[g]) for g in seqs]
        inv = [1.0 / (jnp.sum(p_loc[g], axis=-1, keepdims=True) + jnp.sum(p_ctx[g], axis=-1, keepdims=True)
                      + jnp.exp(sink - m[g])) for g in seqs]
        o = [_bdot(p_ctx[g] * inv[g], cv_ref[g]) + _bdot(p_loc[g] * inv[g], v_scr[g, win, :]) for g in seqs]
        for g in seqs:
            y_ref[g, rows, :] = _unstack_o(o[g])
        return carry

    lax.fori_loop(0, nblk, block, 0)


def _lat_attention(u, wts, li, ck, cv, nb, l, seq0):
    lc = ck.shape[2]
    nblk = l // ATT_BLOCK
    ng = ATT_GROUP
    cos, sin = _rope_tables_host(l)
    cache_spec = pl.BlockSpec((ng, None, lc, D_KV), lambda b: (b, li, 0, 0))
    y = pl.pallas_call(
        functools.partial(_lat_attn_kernel, nblk, li, ng),
        grid=(nb // ng,),
        in_specs=[_seq_spec(l, ATT_IN, ng, seq0), _layer_const(wts["attn_q_norm"], li),
                  _layer_const(wts["attn_k_norm"], li), _const((l, D_ATT)), _const((l, D_ATT)),
                  cache_spec, cache_spec, pl.BlockSpec(memory_space=pltpu.SMEM)],
        out_specs=_seq_spec(l, D_ATT, ng),
        out_shape=jax.ShapeDtypeStruct((nb, l, D_ATT), F32),
        scratch_shapes=[pltpu.VMEM((ng, l, D_ATT), F32), pltpu.VMEM((ng, l + 2 * ATT_BLOCK, D_KV), F32),
                        pltpu.VMEM((ng, l + 2 * ATT_BLOCK, D_KV), F32)],
        compiler_params=_cparams("parallel"),
        name="lat_attention",
    )(u.reshape(-1, l, ATT_IN), wts["attn_q_norm"], wts["attn_k_norm"], jnp.asarray(cos), jnp.asarray(sin),
      ck, cv, wts["attn_sink"])
    return y.reshape(nb * l, D_ATT)


def _pad_lane_tile(w):
    return jnp.pad(w, [(0, 0)] * (w.ndim - 1) + [(0, -w.shape[-1] % LANES)])


def _dt_weight(w):
    zeros = jnp.zeros(w.shape[:-1] + (DT_PAD - DT_REP * N_DT,), w.dtype)
    return jnp.concatenate([w[..., C_DT:C_DT + N_DT]] * DT_REP + [zeros], axis=-1).astype(BF16)


def _prep_weights(p):
    row = lambda a: a.reshape(DEPTH, 1, -1)
    pad_lanes = lambda a: jnp.pad(row(a), ((0, 0), (0, 0), (0, LANES - a[0].size)))
    return dict(
        norm_w=p["norm_w"].reshape(DEPTH * 3, 1, D_MODEL),
        ffn_w_in=p["ffn_w_in"].astype(BF16).reshape(DEPTH * 2, D_MODEL, 2 * D_FF),
        ffn_w_out=p["ffn_w_out"].astype(BF16).reshape(DEPTH * 2, D_FF, D_MODEL),
        mix_w_in=_pad_lane_tile(p["mix_w_in"]).astype(BF16), mix_w_dt=_dt_weight(p["mix_w_in"]),
        mix_w_out=p["mix_w_out"].astype(BF16),
        ssd_conv_w=p["ssd_conv_w"], ssd_conv_b=row(p["ssd_conv_b"]),
        ssd_dt_bias=pad_lanes(jnp.tile(row(p["ssd_dt_bias"]), (1, 1, DT_REP))),
        ssd_a_log=pad_lanes(jnp.tile(row(p["ssd_a_log"]), (1, 1, DT_REP))),
        ssd_d=row(jnp.repeat(p["ssd_d"], SSM_HEAD_DIM, axis=-1)), ssd_norm_w=row(p["ssd_norm_w"]),
        hy_conv_w=p["hy_conv_w"], hy_conv_b=row(p["hy_conv_b"]), hy_bias=p["hy_bias"],
        hy_w1=jnp.pad(p["hy_w1"], ((0, 0), (0, LANES - HY_EMB), (0, 0))), hy_b1=row(p["hy_b1"]),
        hy_w2=p["hy_w2"], hy_b2=row(p["hy_b2"]), hy_w3=p["hy_w3"], hy_freq=row(p["hy_freq"]),
        ret_decay_logit=pad_lanes(p["ret_decay_logit"]), ret_gn_w=row(p["ret_gn_w"]),
        attn_q_norm=row(jnp.tile(p["attn_q_norm"], (1, ATT_HEADS))),
        attn_k_norm=row(jnp.tile(p["attn_k_norm"], (1, ATT_HEADS))),
        attn_sink=p["attn_sink"].reshape(DEPTH * ATT_HEADS),
    )


def _mixers(u, wts, li, nb, l, seq0, ssd_s0, ret_s0, ctx_kv, hy_tables, hy_spectra, carried):
    z, xbc, hy, ret, att, dt = u
    ctx = ctx_kv is None
    y_ssd, s_ssd = _ssd(z, xbc, dt, wts, li, ssd_s0, nb, l, seq0, stacked=ctx, prev=carried.get("ssd"))
    y_hy = _hyena(hy, wts, li, hy_tables, hy_spectra, nb, l, seq0)
    y_ret, s_ret = _retention(ret, wts, li, ret_s0, nb, l, seq0, stacked=ctx, prev=carried.get("ret"))
    if ctx:
        y_att, k, v = _ctx_attention(att, wts, li, nb, l, seq0, prev=carried.get("kv"))
        carried = dict(ssd=s_ssd, ret=s_ret, kv=(k, v))
    else:
        y_att = _lat_attention(att, wts, li, ctx_kv[0], ctx_kv[1], nb, l, seq0)
    return (y_ssd, y_hy, y_ret, y_att), carried


def kernel(x_prompt, x_sample, cache_k, cache_v, state_ssd, state_ret, c, c_ctx, w_mod, b_mod, norm_w, ffn_w_in, ffn_w_out, mix_w_in, mix_w_out, ssd_conv_w, ssd_conv_b, ssd_dt_bias, ssd_a_log, ssd_d, ssd_norm_w, hy_conv_w, hy_conv_b, hy_w1, hy_b1, hy_w2, hy_b2, hy_w3, hy_freq, hy_bias, ret_decay_logit, ret_gn_w, attn_q_norm, attn_k_norm, attn_sink):
    bp, lp_len, _ = x_prompt.shape
    bs, ls_len, _ = x_sample.shape
    lc = cache_k.shape[2]

    cond = jnp.concatenate([c_ctx[None, :], c, jnp.zeros((MOD_ROWS - 1 - bs, D_MODEL), F32)], axis=0)
    mod = _modulation(cond, w_mod, b_mod)
    wts = _prep_weights(dict(
        norm_w=norm_w, ffn_w_in=ffn_w_in, ffn_w_out=ffn_w_out, mix_w_in=mix_w_in, mix_w_out=mix_w_out,
        ssd_conv_w=ssd_conv_w, ssd_conv_b=ssd_conv_b, ssd_dt_bias=ssd_dt_bias, ssd_a_log=ssd_a_log, ssd_d=ssd_d,
        ssd_norm_w=ssd_norm_w, hy_conv_w=hy_conv_w, hy_conv_b=hy_conv_b, hy_w1=hy_w1, hy_b1=hy_b1, hy_w2=hy_w2,
        hy_b2=hy_b2, hy_w3=hy_w3, hy_freq=hy_freq, hy_bias=hy_bias, ret_decay_logit=ret_decay_logit,
        ret_gn_w=ret_gn_w, attn_q_norm=attn_q_norm, attn_k_norm=attn_k_norm, attn_sink=attn_sink))
    ck = cache_k.reshape(bs, DEPTH, lc, D_KV)
    cv = cache_v.reshape(bs, DEPTH, lc, D_KV)

    tab_p = _dft_tables(lp_len)
    tab_s = _dft_tables(ls_len)
    blk_p = _dft_block_tables(tab_p, lp_len)
    blk_s = _dft_block_tables(tab_s, ls_len)

    tm = TOKEN_TILE
    tp, ts = bp * lp_len, bs * ls_len
    assert tp % tm == 0 and ts % tm == 0 and ls_len % tm == 0 and tp % ls_len == 0
    st = _Stream(tm, tp // tm, ts // tm, ls_len)
    x = (x_prompt.reshape(tp, D_MODEL), x_sample.reshape(ts, D_MODEL))
    carried = {}
    for li in range(DEPTH):
        spec_p = _hy_filter(lp_len, tab_p[0], tab_p[1], wts, li)
        spec_s = _hy_filter(ls_len, tab_s[0], tab_s[1], wts, li)
        x = _ffn(st, x, mod, wts, li, 0)
        u = _inproj(st, x, mod, wts, li)
        y_p, carried = _mixers(u, wts, li, bp, lp_len, 0, None, None, None, blk_p, spec_p, carried)
        y_s, _ = _mixers(u, wts, li, bs, ls_len, tp // ls_len, state_ssd, state_ret, (ck, cv), blk_s, spec_s, {})
        if li + 1 < DEPTH:
            x = _ffn(st, x, mod, wts, li, 1, mix=tuple(zip(y_p, y_s)))
        else:
            yp = _ffn(st.part(0), x, mod, wts, li, 1, mix=y_p)
            ys = _ffn(st.part(1), x, mod, wts, li, 1, mix=y_s)

    kv_shape = (bp, DEPTH, lp_len, ATT_KV_HEADS, HEAD_DIM)
    new_k, new_v = carried["kv"]
    return (yp.reshape(bp, lp_len, D_MODEL), ys.reshape(bs, ls_len, D_MODEL),
            new_k.reshape(kv_shape), new_v.reshape(kv_shape), carried["ssd"], carried["ret"])
```

```python
import functools
import math

import numpy as np
import jax
import jax.numpy as jnp
from jax import lax
from jax.experimental import pallas as pl
from jax.experimental.pallas import tpu as pltpu

F32 = jnp.float32
BF16 = jnp.bfloat16

D_MODEL = 1024
DEPTH = 2
GRID_W = 64
D_FF = 2816
N_MOD = 9
NORM_EPS = 1e-6
CHUNK = 128
D_SSM = 256
SSM_HEADS = 4
SSM_HEAD_DIM = 64
SSM_STATE = 128
SSM_GROUPS = 2
SSM_CONV_CH = D_SSM + 2 * SSM_GROUPS * SSM_STATE
D_HY = 256
HY_ORDER = 2
HY_BANDS = 16
HY_EMB = 1 + 2 * HY_BANDS
HY_HIDDEN = 64
HY_FAST_DECAY = 0.3
HY_SLOW_DECAY = 1.5
HY_TARGET = 1e-2
HY_IN = (HY_ORDER + 1) * D_HY
D_RET = 256
RET_HEADS = 4
RET_HEAD_DIM = 64
RET_IN = 4 * D_RET
ATT_HEADS = 4
ATT_KV_HEADS = 2
HEAD_DIM = 64
D_ATT = ATT_HEADS * HEAD_DIM
D_KV = ATT_KV_HEADS * HEAD_DIM
ATT_IN = D_ATT + 2 * D_KV
WINDOW = 128
ATT_BLOCK = 128
ROPE_BASE = 10000.0
D_MIX = D_SSM + D_HY + D_RET + D_ATT

LANES = 128
DT_PAD = LANES
VMEM_LIMIT = 56 * 1024 * 1024
MOD_ROWS = 8

TOKEN_TILE = 512
DFT_BLOCK = 512
HY_GROUP = 4
SEQ_GROUP = 4
RET_GROUP = 4
ATT_GROUP = 4


def _cparams(*sem):
    return pltpu.CompilerParams(dimension_semantics=sem, vmem_limit_bytes=VMEM_LIMIT)


def _rms(x, w):
    return x * lax.rsqrt(jnp.mean(x * x, axis=-1, keepdims=True) + NORM_EPS) * w


def _silu(x):
    return x * (1.0 / (1.0 + jnp.exp(-x)))


def _softplus(x):
    return jnp.maximum(x, 0.0) + jnp.log1p(jnp.exp(-jnp.abs(x)))


def _bdot(a, b):
    return jnp.dot(a.astype(BF16), b.astype(BF16), preferred_element_type=F32)


def _bdot_nt(a, b):
    return lax.dot_general(a.astype(BF16), b.astype(BF16), (((1,), (1,)), ((), ())),
                           preferred_element_type=F32)


def _bdot_tn(a, b):
    return lax.dot_general(a.astype(BF16), b.astype(BF16), (((0,), (0,)), ((), ())),
                           preferred_element_type=F32)


def _full(shape):
    n = len(shape)
    return pl.BlockSpec(shape, lambda *_: (0,) * n)


def _const(shape):
    n = len(shape)
    return pl.BlockSpec(shape, lambda *_: (0,) * n, pipeline_mode=pl.Buffered(1))


def _layer_const(arr, li):
    tail = arr.shape[1:]
    zeros = (0,) * len(tail)
    return pl.BlockSpec((None,) + tail, lambda *_: (li,) + zeros, pipeline_mode=pl.Buffered(1))


class _Stream:
    def __init__(self, tm, nct, nlt, l_lat, first=0, count=None):
        self.tm, self.nct, self.nlt, self.l_lat = tm, nct, nlt, l_lat
        self.first = first
        self.count = nct + nlt if count is None else count

    def part(self, path):
        first, count = (0, self.nct) if path == 0 else (self.nct, self.nlt)
        return _Stream(self.tm, self.nct, self.nlt, self.l_lat, first, count)

    @property
    def tokens(self):
        return self.count * self.tm

    def merged(self, width):
        return pl.BlockSpec((self.tm, width), lambda i: (i + self.first, 0))

    def owned(self, width):
        return pl.BlockSpec((self.tm, width), lambda i: (i, 0))

    def pair(self, width):
        ctx = pl.BlockSpec((self.tm, width), lambda i: (jnp.minimum(i + self.first, self.nct - 1), 0))
        lat = pl.BlockSpec((self.tm, width), lambda i: (jnp.maximum(i + self.first - self.nct, 0), 0))
        return [ctx, lat]

    def mod(self, k, row_ctx, row_lat):
        def index(i):
            t = i + self.first
            lat_row = row_lat + (jnp.maximum(t - self.nct, 0) * self.tm) // self.l_lat
            return (jnp.where(t < self.nct, row_ctx, lat_row), 0, k)
        return pl.BlockSpec((None, 1, D_MODEL), index)


def _seq_spec(l, w, ng=None, seq0=0):
    step0 = seq0 // (ng or 1)
    return pl.BlockSpec((ng, l, w), lambda b: (b + step0, 0, 0))


def _head_masks(width, heads):
    lane = lax.broadcasted_iota(jnp.int32, (1, width), 1)
    hd = width // heads
    return [(lane >= h * hd) & (lane < (h + 1) * hd) for h in range(heads)]


def _by_head(masks, vals):
    out = vals[-1]
    for m, v in zip(masks[-2::-1], vals[-2::-1]):
        out = jnp.where(m, v, out)
    return out


def _block_diag(n, blk, value):
    i = lax.broadcasted_iota(jnp.int32, (n, n), 0) // blk
    j = lax.broadcasted_iota(jnp.int32, (n, n), 1) // blk
    return jnp.where(i == j, value, 0.0).astype(F32)


def _mod_kernel(c_ref, w_ref, b_ref, o_ref):
    c = c_ref[...]
    o_ref[...] = _bdot(_silu(c), w_ref[...]) + b_ref[...]


def _modulation(cond, w_mod, b_mod):
    out = pl.pallas_call(
        _mod_kernel,
        grid=(DEPTH, N_MOD),
        in_specs=[pl.BlockSpec((MOD_ROWS, D_MODEL), lambda l, j: (0, 0)),
                  pl.BlockSpec((None, D_MODEL, D_MODEL), lambda l, j: (l, 0, j)),
                  pl.BlockSpec((None, 1, D_MODEL), lambda l, j: (l, 0, j))],
        out_specs=pl.BlockSpec((None, MOD_ROWS, D_MODEL), lambda l, j: (l, 0, j)),
        out_shape=jax.ShapeDtypeStruct((DEPTH, MOD_ROWS, N_MOD * D_MODEL), F32),
        compiler_params=_cparams("arbitrary", "arbitrary"),
        name="modulation",
    )(cond, w_mod, b_mod.reshape(DEPTH, 1, N_MOD * D_MODEL))
    return out.reshape(DEPTH * MOD_ROWS, 1, N_MOD * D_MODEL)


def _ffn_kernel(n_mix, pair_x, pair_mix, nct, first, *refs):
    refs = list(refs)
    o_ref = refs.pop()
    is_ctx = pl.program_id(0) + first < nct

    def read(pair):
        if pair:
            a, b = refs.pop(0), refs.pop(0)
            return jnp.where(is_ctx, a[...], b[...])
        return refs.pop(0)[...]

    x = read(pair_x)
    sh_ref, sc_ref, g_ref, nw_ref, wi_ref, wo_ref = (refs.pop(0) for _ in range(6))
    if n_mix:
        gm_ref = refs.pop(0)
        ys = [read(pair_mix) for _ in range(n_mix)]
        wm_ref = refs.pop(0)
        w = D_MIX // n_mix
        acc = _bdot(ys[0], wm_ref[0:w, :])
        for j in range(1, n_mix):
            acc += _bdot(ys[j], wm_ref[j * w:(j + 1) * w, :])
        x = x + gm_ref[...] * acc
    h = (_rms(x, nw_ref[...] * (1.0 + sc_ref[...])) + sh_ref[...]).astype(BF16)
    gate = jnp.dot(h, wi_ref[:, 0:D_FF], preferred_element_type=F32)
    up = jnp.dot(h, wi_ref[:, D_FF:2 * D_FF], preferred_element_type=F32)
    o_ref[...] = x + 0.5 * g_ref[...] * _bdot(_silu(gate) * up, wo_ref[...])


def _ffn(st, x, mod, wts, li, k, mix=()):
    row_ctx, row_lat = li * MOD_ROWS, li * MOD_ROWS + 1
    w_in, w_out = wts["ffn_w_in"], wts["ffn_w_out"]
    pair_x = isinstance(x, tuple)
    pair_mix = bool(mix) and isinstance(mix[0], tuple)
    args = list(x) if pair_x else [x]
    in_specs = st.pair(D_MODEL) if pair_x else [st.merged(D_MODEL)]
    args += [mod, mod, mod, wts["norm_w"], w_in, w_out]
    in_specs += [st.mod(6 * k + j, row_ctx, row_lat) for j in range(3)]
    in_specs += [_layer_const(wts["norm_w"], 3 * li + 2 * k), _layer_const(w_in, 2 * li + k),
                 _layer_const(w_out, 2 * li + k)]
    if mix:
        args.append(mod)
        in_specs.append(st.mod(5, row_ctx, row_lat))
        for y in mix:
            width = (y[0] if pair_mix else y).shape[1]
            args += list(y) if pair_mix else [y]
            in_specs += st.pair(width) if pair_mix else [st.owned(width)]
        args.append(wts["mix_w_out"])
        in_specs.append(_layer_const(wts["mix_w_out"], li))
    return pl.pallas_call(
        functools.partial(_ffn_kernel, len(mix), pair_x, pair_mix, st.nct, st.first),
        grid=(st.count,),
        in_specs=in_specs,
        out_specs=st.owned(D_MODEL),
        out_shape=jax.ShapeDtypeStruct((st.tokens, D_MODEL), F32),
        compiler_params=_cparams("parallel"),
        name="ffn",
    )(*args)


_IN_SPLITS = (("z", D_SSM, BF16), ("xbc", SSM_CONV_CH, F32), ("hy", HY_IN, F32), ("ret", RET_IN, BF16),
              ("att", ATT_IN, BF16), ("dt", DT_PAD, F32))


C_DT = D_SSM + SSM_CONV_CH
N_DT = 2 * SSM_HEADS
W_TAIL = HY_IN + RET_IN + ATT_IN


def _inproj_kernel(x_ref, sh_ref, sc_ref, nw_ref, w_ref, wdt_ref, *rest):
    o_refs, wt_scr = rest[:-1], rest[-1]

    @pl.when(pl.program_id(0) == 0)
    def _():
        wt_scr[...] = w_ref[:, C_DT + N_DT:C_DT + N_DT + W_TAIL]

    h = (_rms(x_ref[...], nw_ref[...] * (1.0 + sc_ref[...])) + sh_ref[...]).astype(BF16)
    off = 0
    for (name, width, dtype), o_ref in zip(_IN_SPLITS, o_refs):
        if name == "dt":
            w = wdt_ref[...]
        elif off < C_DT:
            w = w_ref[:, off:off + width]
        else:
            w = wt_scr[:, off - C_DT:off - C_DT + width]
        o_ref[...] = jnp.dot(h, w, preferred_element_type=F32).astype(dtype)
        off += width


def _inproj(st, x, mod, wts, li):
    row_ctx, row_lat = li * MOD_ROWS, li * MOD_ROWS + 1
    return pl.pallas_call(
        _inproj_kernel,
        grid=(st.count,),
        in_specs=[st.merged(D_MODEL), st.mod(3, row_ctx, row_lat), st.mod(4, row_ctx, row_lat),
                  _layer_const(wts["norm_w"], 3 * li + 1),
                  _layer_const(wts["mix_w_in"], li), _layer_const(wts["mix_w_dt"], li)],
        out_specs=[st.owned(width) for _, width, _ in _IN_SPLITS],
        out_shape=[jax.ShapeDtypeStruct((st.tokens, width), dtype) for _, width, dtype in _IN_SPLITS],
        scratch_shapes=[pltpu.VMEM((D_MODEL, W_TAIL), BF16)],
        compiler_params=_cparams("arbitrary"),
        name="mix_in",
    )(x, mod, mod, wts["norm_w"], wts["mix_w_in"], wts["mix_w_dt"])


def _conv3_chunk(x_ref, c, nc, w, b):
    q = CHUNK
    l = nc * q
    r0 = pl.multiple_of(c * q, q)
    x = x_ref[pl.ds(r0, q), :]
    prev = x_ref[pl.ds(jnp.maximum(r0 - 1, 0), 1), :]
    nxt = x_ref[pl.ds(jnp.minimum(r0 + q, l - 1), 1), :]
    prev = jnp.where(c > 0, prev, 0.0)
    nxt = jnp.where(c < nc - 1, nxt, 0.0)
    rid = lax.broadcasted_iota(jnp.int32, (q, 1), 0)
    xm1 = jnp.where(rid == 0, prev, pltpu.roll(x, 1, 0))
    xp1 = jnp.where(rid == q - 1, nxt, pltpu.roll(x, q - 1, 0))
    return xm1 * w[0:1, :] + x * w[1:2, :] + xp1 * w[2:3, :] + b


def _tri(lower):
    i = lax.broadcasted_iota(jnp.int32, (CHUNK, CHUNK), 0)
    j = lax.broadcasted_iota(jnp.int32, (CHUNK, CHUNK), 1)
    return (j <= i) if lower else (j >= i)


def _rows(c):
    return pl.ds(pl.multiple_of(c * CHUNK, CHUNK), CHUNK)


def _split_cat(v, parts, axis):
    out, r = [], v
    for i in range(parts):
        piece = r.astype(BF16)
        out.append(piece)
        if i + 1 < parts:
            r = r - piece.astype(F32)
    return jnp.concatenate(out, axis=axis)


def _seg_mean(x, seg):
    w = x.shape[-1]
    ones = _block_diag(w, seg, 1.0).astype(BF16)
    return jnp.dot(_split_cat(x, 2, axis=1), jnp.concatenate([ones, ones], axis=0),
                   preferred_element_type=F32) * (1.0 / seg)


def _stack_heads(x, masks):
    return jnp.concatenate([jnp.where(m, x, 0.0) for m in masks], axis=0)


def _slot_out(tail, nb, li, stacked, ng, whole):
    zeros = (0,) * len(tail)
    slot = li if stacked else 0
    if whole:
        spec = pl.BlockSpec((ng, DEPTH) + tail, lambda b: (b, 0) + zeros)
    else:
        spec = pl.BlockSpec((ng, None) + tail, lambda b: (b, slot) + zeros)
    return spec, jax.ShapeDtypeStruct((nb, DEPTH if stacked else 1) + tail, F32)


def _slot_view(ref, g, slot):
    if slot is None:
        return ref.at[g]
    for other in range(DEPTH):
        if other != slot:
            ref[g, other] = jnp.zeros(ref.shape[2:], ref.dtype)
    return ref.at[g, slot]


def _carry_prev(args, in_specs, prevs, first_out):
    aliases = {}
    for j, prev in enumerate(prevs):
        aliases[len(args)] = first_out + j
        args.append(prev)
        in_specs.append(pl.BlockSpec(memory_space=pl.ANY))
    return aliases


SSD_SEL_W = 2 * SSM_HEADS * SSM_STATE
CUM_PIECES = 3
WGT_PIECES = 2
DT_REP = CUM_PIECES + WGT_PIECES


def _ssd_kernel(nc, ng, has_s0, slot, *refs):
    refs = list(refs)
    (z_ref, xbc_ref, dt_ref, cw_ref, cb_ref, dtb_ref, alog_ref, dsk_ref, nw_ref) = refs[:9]
    s0_ref = refs[9] if has_s0 else None
    (y_ref, s_ref, xs_scr, xk_scr, xt_scr, dt_scr, cum_scr, dtt_scr, cumt_scr, yf_scr, yb_scr, st_scr,
     sel_scr) = refs[-13:]
    q = CHUNK
    nh, n, p = SSM_HEADS, SSM_STATE, SSM_HEAD_DIM
    rep = nh // SSM_GROUPS
    hm = _head_masks(D_SSM, nh)

    @pl.when(pl.program_id(0) == 0)
    def _():
        k = lax.broadcasted_iota(jnp.int32, (LANES, SSD_SEL_W), 0)
        col = lax.broadcasted_iota(jnp.int32, (LANES, SSD_SEL_W), 1)
        is_cum = col < nh * n
        grp = k // (2 * nh)
        grp_ok = (is_cum & (grp < CUM_PIECES)) | ((~is_cum) & (grp >= CUM_PIECES) & (grp < DT_REP))
        hit = grp_ok & (k % nh == (col % (nh * n)) // n)
        sel_scr[...] = jnp.where(hit, 1.0, 0.0).astype(BF16)

    cw = cw_ref[...]
    cb = cb_ref[...]
    neg_a = -jnp.exp(alog_ref[...])
    dtb = dtb_ref[...]

    class Seq:
        def __init__(self, g):
            self.z, self.xbc, self.dt_in, self.y, self.s = (z_ref.at[g], xbc_ref.at[g], dt_ref.at[g], y_ref.at[g],
                                                            _slot_view(s_ref, g, slot))
            self.s0 = s0_ref.at[g] if has_s0 else None
            self.xs, self.xk, self.xt, self.dt, self.cum = (xs_scr.at[g], xk_scr.at[g], xt_scr.at[g], dt_scr.at[g],
                                                            cum_scr.at[g])
            self.dtt, self.cumt, self.yf, self.yb, self.st = (dtt_scr.at[g], cumt_scr.at[g], yf_scr.at[g],
                                                              yb_scr.at[g], st_scr.at[g])

    seqs = [Seq(g) for g in range(ng)]

    def prep_seq(sq, c):
        rows = _rows(c)
        xall = _silu(_conv3_chunk(sq.xbc, c, nc, cw, cb))
        sq.xs[rows, :] = xall
        xs = xall[:, 0:D_SSM]
        sq.xk[c] = _stack_heads(xs, hm).astype(BF16)
        sq.xt[c] = xs.T.astype(BF16)
        dt = _softplus(sq.dt_in[rows, :] + dtb)
        la = _split_cat(dt * neg_a, 3, axis=0)
        tri = jnp.concatenate([_tri(True), _tri(False)], axis=0)
        tri = jnp.where(tri, 1.0, 0.0).astype(BF16)
        cs = jnp.dot(jnp.concatenate([tri] * 3, axis=1), la, preferred_element_type=F32)
        lane = lax.broadcasted_iota(jnp.int32, (1, LANES), 1)
        cum = jnp.where(lane % (2 * nh) < nh, cs[0:q], cs[q:2 * q])
        sq.dt[rows, :] = dt
        sq.cum[rows, :] = cum
        sq.dtt[c] = dt.T
        sq.cumt[c] = cum.T

    def prep(c, carry):
        for sq in seqs:
            prep_seq(sq, c)
        return carry

    lax.fori_loop(0, nc, prep, 0)

    for sq in seqs:
        for d in range(2):
            for h in range(nh):
                blk = slice(h * p, (h + 1) * p)
                sq.st[d, blk, :] = sq.s0[d, h].T if has_s0 else jnp.zeros((p, n), F32)

    def pack_scalars(sq, c, d):
        rows = _rows(c)
        dt = sq.dt[rows, :]
        cum = sq.cum[rows, :]
        edge = q - 1 if d == 0 else 0
        lane = lax.broadcasted_iota(jnp.int32, (1, LANES), 1)
        used = (lane < DT_REP * 2 * nh) & ((lane % (2 * nh)) // nh == d)
        cum = jnp.where(used, cum, 0.0)
        wgt = jnp.exp(cum[edge:edge + 1, :] - cum) * dt
        grp = lane // (2 * nh)
        packed = jnp.zeros_like(cum)
        for src, pieces, g0 in ((cum, CUM_PIECES, 0), (wgt, WGT_PIECES, CUM_PIECES)):
            rest = src
            for i in range(pieces):
                piece = rest.astype(BF16).astype(F32)
                packed = jnp.where(grp == g0 + i, piece, packed)
                rest = rest - piece
        return jnp.where(used, packed, 0.0).astype(BF16)

    def operands(sq, c, d):
        rows = _rows(c)
        bm = [sq.xs[rows, D_SSM + g * n:D_SSM + (g + 1) * n] for g in range(SSM_GROUPS)]
        cm = [sq.xs[rows, D_SSM + (SSM_GROUPS + g) * n:D_SSM + (SSM_GROUPS + g + 1) * n]
              for g in range(SSM_GROUPS)]
        return bm, cm

    def decay_blocks(sq, c, d, e, cb_t):
        mask = _tri(d == 0)
        blocks, ecol = [], []
        for h in range(nh):
            r = d * nh + h
            col = e[:, h * n:(h + 1) * n]
            decay = jnp.exp(jnp.where(mask, col - sq.cumt[c, r:r + 1, :], -jnp.inf))
            blocks.append((cb_t[h // rep] * decay * sq.dtt[c, r:r + 1, :]).astype(BF16))
            ecol.append(jnp.exp(col))
        lo = lax.broadcasted_iota(jnp.int32, (1, n), 1) < p
        ecol = jnp.concatenate([jnp.where(lo, ecol[2 * j], ecol[2 * j + 1]) for j in range(nh // 2)], axis=1)
        return jnp.concatenate(blocks, axis=1), ecol

    def scan(i, carry):
        j = nc - 1 - i
        jobs = [(sq, c, d) for sq in seqs for c, d in ((i, 0), (j, 1))]
        packed = jnp.concatenate([pack_scalars(*job) for job in jobs], axis=0)
        e_all = jnp.dot(packed, sel_scr[...], preferred_element_type=F32)
        es = [e_all[t * q:(t + 1) * q] for t in range(len(jobs))]
        ops = [operands(*job) for job in jobs]
        cb_t = [[_bdot_nt(cm[g], bm[g]) for g in range(SSM_GROUPS)] for bm, cm in ops]
        dec = [decay_blocks(*job, es[t], cb_t[t]) for t, job in enumerate(jobs)]
        intra = [jnp.dot(dec[t][0], sq.xk[c], preferred_element_type=F32) for t, (sq, c, d) in enumerate(jobs)]
        s_old = [sq.st[d] for sq, c, d in jobs]
        inter = [jnp.concatenate([_bdot_nt(cm[g], s_old[t][g * rep * p:(g + 1) * rep * p, :])
                                  for g in range(SSM_GROUPS)], axis=1) for t, (bm, cm) in enumerate(ops)]
        for t, (sq, c, d) in enumerate(jobs):
            bm = ops[t][0]
            edge = q - 1 if d == 0 else 0
            for h in range(nh):
                blk, cols = slice(h * p, (h + 1) * p), slice(h * n, (h + 1) * n)
                wk = (bm[h // rep] * es[t][:, nh * n + h * n:nh * n + (h + 1) * n]).astype(BF16)
                upd = jnp.dot(sq.xt[c, blk, :], wk, preferred_element_type=F32)
                sq.st[d, blk, :] = jnp.exp(es[t][edge:edge + 1, cols]) * s_old[t][blk, :] + upd
            y = intra[t] + dec[t][1] * inter[t]
            if d == 0:
                sq.yf[_rows(c), :] = y
            else:
                sq.yb[_rows(c), :] = y
        return carry

    lax.fori_loop(0, nc, scan, 0)

    for sq in seqs:
        for d in range(2):
            for h in range(nh):
                sq.s[d, h] = sq.st[d, h * p:(h + 1) * p, :].T

    dsk = dsk_ref[...]
    nw = nw_ref[...]

    def fin(c, carry):
        rows = _rows(c)
        for sq in seqs:
            y = sq.yf[rows, :] + sq.yb[rows, :] + dsk * sq.xs[rows, 0:D_SSM]
            sq.y[rows, :] = _rms(y * _silu(sq.z[rows, :].astype(F32)), nw)
        return carry

    lax.fori_loop(0, nc, fin, 0)


def _ssd(z, xbc, dt, wts, li, s0, nb, l, seq0, stacked=False, prev=None):
    nc = l // CHUNK
    has_s0 = s0 is not None
    names = ("ssd_conv_w", "ssd_conv_b", "ssd_dt_bias", "ssd_a_log", "ssd_d", "ssd_norm_w")
    ng = min(SEQ_GROUP, nb)
    args = [z.reshape(-1, l, D_SSM), xbc.reshape(-1, l, SSM_CONV_CH), dt.reshape(-1, l, DT_PAD)]
    args += [wts[n] for n in names]
    in_specs = [_seq_spec(l, D_SSM, ng, seq0), _seq_spec(l, SSM_CONV_CH, ng, seq0), _seq_spec(l, DT_PAD, ng, seq0)]
    in_specs += [_layer_const(wts[n], li) for n in names]
    if has_s0:
        args.append(s0)
        in_specs.append(pl.BlockSpec((ng, None, 2, SSM_HEADS, SSM_STATE, SSM_HEAD_DIM),
                                     lambda b: (b, li, 0, 0, 0, 0)))
    whole = stacked and prev is None
    st_spec, st_shape = _slot_out((2, SSM_HEADS, SSM_STATE, SSM_HEAD_DIM), nb, li, stacked, ng, whole)
    aliases = _carry_prev(args, in_specs, [] if prev is None else [prev], 1)
    y, s = pl.pallas_call(
        functools.partial(_ssd_kernel, nc, ng, has_s0, li if whole else None),
        grid=(nb // ng,),
        in_specs=in_specs,
        out_specs=[_seq_spec(l, D_SSM, ng), st_spec],
        out_shape=[jax.ShapeDtypeStruct((nb, l, D_SSM), F32), st_shape],
        input_output_aliases=aliases,
        scratch_shapes=[pltpu.VMEM((ng, l, SSM_CONV_CH), F32),
                        pltpu.VMEM((ng, nc, SSM_HEADS * CHUNK, D_SSM), BF16),
                        pltpu.VMEM((ng, nc, D_SSM, CHUNK), BF16),
                        pltpu.VMEM((ng, l, DT_PAD), F32), pltpu.VMEM((ng, l, DT_PAD), F32),
                        pltpu.VMEM((ng, nc, DT_PAD, CHUNK), F32), pltpu.VMEM((ng, nc, DT_PAD, CHUNK), F32),
                        pltpu.VMEM((ng, l, D_SSM), F32), pltpu.VMEM((ng, l, D_SSM), F32),
                        pltpu.VMEM((ng, 2, D_SSM, SSM_STATE), F32),
                        pltpu.VMEM((LANES, SSD_SEL_W), BF16)],
        compiler_params=_cparams("arbitrary"),
        name="ssd",
    )(*args)
    return y.reshape(nb * l, D_SSM), s


def _ret_kernel(nc, ng, has_s0, slot, *refs):
    refs = list(refs)
    u_ref, dl_ref, gn_ref = refs[:3]
    s0_ref = refs[3] if has_s0 else None
    y_ref, s_ref, yf_scr, yb_scr, st_scr, dm_scr, e_scr, ea_scr, bd_scr = refs[-9:]
    q = CHUNK
    hd = RET_HEAD_DIM
    hm = _head_masks(D_RET, RET_HEADS)

    @pl.when(pl.program_id(0) == 0)
    def _():
        log_g = -_softplus(-dl_ref[...])
        ii = lax.broadcasted_iota(jnp.int32, (q, q), 0)
        jj = lax.broadcasted_iota(jnp.int32, (q, q), 1)
        dij = (ii - jj).astype(F32)
        ri = lax.broadcasted_iota(jnp.int32, (q, 1), 0).astype(F32)
        lfs, lbs = [], []
        for h in range(RET_HEADS):
            lf = log_g[:, h:h + 1]
            lb = log_g[:, RET_HEADS + h:RET_HEADS + h + 1]
            lfs.append(lf)
            lbs.append(lb)
            d_f = jnp.exp(jnp.where(dij >= 0, dij * lf, -jnp.inf))
            d_b = jnp.exp(jnp.where(dij <= 0, -dij * lb, -jnp.inf))
            dm_scr[:, h * q:(h + 1) * q] = d_f + d_b
        lf_l = _by_head(hm, lfs)
        lb_l = _by_head(hm, lbs)
        e_scr[0] = jnp.exp((ri + 1.0) * lf_l)
        e_scr[1] = jnp.exp((q - ri) * lb_l)
        e_scr[2] = jnp.exp((q - 1.0 - ri) * lf_l)
        e_scr[3] = jnp.exp(ri * lb_l)
        ea_scr[0:1, :] = jnp.exp(q * lf_l)
        ea_scr[1:2, :] = jnp.exp(q * lb_l)
        bd_scr[...] = _block_diag(D_RET, hd, 1.0)

    class Seq:
        def __init__(self, g):
            self.u, self.y, self.s = u_ref.at[g], y_ref.at[g], _slot_view(s_ref, g, slot)
            self.s0 = s0_ref.at[g] if has_s0 else None
            self.yf, self.yb, self.st = yf_scr.at[g], yb_scr.at[g], st_scr.at[g]

    seqs = [Seq(g) for g in range(ng)]

    for sq in seqs:
        for d in range(2):
            for h in range(RET_HEADS):
                blk = slice(h * hd, (h + 1) * hd)
                if has_s0:
                    parts = [jnp.zeros((hd, hd), F32)] * RET_HEADS
                    parts[h] = sq.s0[d, h]
                    sq.st[d, blk, :] = jnp.concatenate(parts, axis=1)
                else:
                    sq.st[d, blk, :] = jnp.zeros((hd, D_RET), F32)

    def qkv(sq, rows):
        return (sq.u[rows, 0:D_RET], sq.u[rows, D_RET:2 * D_RET].astype(F32) * (RET_HEAD_DIM ** -0.5),
                sq.u[rows, 2 * D_RET:3 * D_RET])

    def state_step(sq, d, qq, kk, vv):
        s_old = sq.st[d]
        y = e_scr[d] * _bdot(qq, s_old)
        sq.st[d] = ea_scr[d:d + 1, :] * s_old + _bdot_tn(kk * e_scr[2 + d], vv) * bd_scr[...]
        return y

    def scan(i, carry):
        rows = _rows(i)
        rows_b = _rows(nc - 1 - i)
        fwd = [qkv(sq, rows) for sq in seqs]
        bwd = [qkv(sq, rows_b) for sq in seqs]
        sc = [_bdot_nt(qq, _stack_heads(kk, hm)) * dm_scr[...] for qq, kk, _ in fwd]
        intra = [_bdot(sc[g], _stack_heads(fwd[g][2], hm)) for g in range(ng)]
        inter = [state_step(sq, 0, *fwd[g]) for g, sq in enumerate(seqs)]
        back = [state_step(sq, 1, *bwd[g]) for g, sq in enumerate(seqs)]
        for g, sq in enumerate(seqs):
            sq.yf[rows, :] = intra[g] + inter[g]
            sq.yb[rows_b, :] = back[g]
        return carry

    lax.fori_loop(0, nc, scan, 0)

    for sq in seqs:
        for d in range(2):
            for h in range(RET_HEADS):
                sq.s[d, h] = sq.st[d, h * hd:(h + 1) * hd, h * hd:(h + 1) * hd]

    gn = gn_ref[...]

    def fin(c, carry):
        rows = _rows(c)
        for sq in seqs:
            y = sq.yf[rows, :] + sq.yb[rows, :]
            cen = y - _seg_mean(y, hd)
            var = _seg_mean(cen * cen, hd)
            sq.y[rows, :] = cen * lax.rsqrt(var + NORM_EPS) * gn * _silu(sq.u[rows, 3 * D_RET:4 * D_RET].astype(F32))
        return carry

    lax.fori_loop(0, nc, fin, 0)


def _retention(u, wts, li, s0, nb, l, seq0, stacked=False, prev=None):
    nc = l // CHUNK
    has_s0 = s0 is not None
    names = ("ret_decay_logit", "ret_gn_w")
    ng = min(RET_GROUP, nb)
    args = [u.reshape(-1, l, RET_IN)] + [wts[n] for n in names]
    in_specs = [_seq_spec(l, RET_IN, ng, seq0)] + [_layer_const(wts[n], li) for n in names]
    if has_s0:
        args.append(s0)
        in_specs.append(pl.BlockSpec((ng, None, 2, RET_HEADS, RET_HEAD_DIM, RET_HEAD_DIM),
                                     lambda b: (b, li, 0, 0, 0, 0)))
    whole = stacked and prev is None
    st_spec, st_shape = _slot_out((2, RET_HEADS, RET_HEAD_DIM, RET_HEAD_DIM), nb, li, stacked, ng, whole)
    aliases = _carry_prev(args, in_specs, [] if prev is None else [prev], 1)
    y, s = pl.pallas_call(
        functools.partial(_ret_kernel, nc, ng, has_s0, li if whole else None),
        grid=(nb // ng,),
        in_specs=in_specs,
        out_specs=[_seq_spec(l, D_RET, ng), st_spec],
        out_shape=[jax.ShapeDtypeStruct((nb, l, D_RET), F32), st_shape],
        input_output_aliases=aliases,
        scratch_shapes=[pltpu.VMEM((ng, l, D_RET), F32), pltpu.VMEM((ng, l, D_RET), F32),
                        pltpu.VMEM((ng, 2, D_RET, D_RET), F32), pltpu.VMEM((CHUNK, RET_HEADS * CHUNK), F32),
                        pltpu.VMEM((4, CHUNK, D_RET), F32), pltpu.VMEM((8, D_RET), F32),
                        pltpu.VMEM((D_RET, D_RET), F32)],
        compiler_params=_cparams("arbitrary"),
        name="retention",
    )(*args)
    return y.reshape(nb * l, D_RET), s


def _split(x):
    hi = x.astype(BF16)
    return hi, (x - hi.astype(F32)).astype(BF16)


def _dot3(a_hi, a_lo, b_hi, b_lo):
    d = lambda p, q: jnp.dot(p, q, preferred_element_type=F32)
    return d(a_hi, b_hi) + (d(a_lo, b_hi) + d(a_hi, b_lo))


@functools.lru_cache(maxsize=None)
def _dft_fwd_host(l):
    n = 2 * l
    f = np.arange(l, dtype=np.int64)[:, None]
    s = np.arange(l, dtype=np.int64)[None, :]
    ang = ((f * s) % n).astype(np.float64) * (2.0 * math.pi / n)
    im = -np.sin(ang)
    im[0] = np.where(np.arange(l) % 2 == 0, 1.0, -1.0)
    return np.concatenate([np.cos(ang), im], axis=0).astype(np.float32)


def _dft_tables(l):
    n = 2 * l
    fwd = jnp.asarray(_dft_fwd_host(l))
    wgt = np.full((n, 1), 2.0 / n, np.float32)
    wgt[0] = wgt[l] = 1.0 / n
    return _split(fwd) + _split((fwd * wgt).T)


def _dft_block_tables(tables, l):
    fwd_hi, fwd_lo, inv_hi, inv_lo = tables
    r = min(DFT_BLOCK, l)
    nblk = l // r
    fb = lambda t: (t[0:l].reshape(nblk, r, l), t[l:2 * l].reshape(nblk, r, l))
    fwd_blk = jnp.concatenate(fb(fwd_hi) + fb(fwd_lo), axis=1)
    inv_blk = jnp.concatenate([inv_hi.reshape(nblk, r, 2 * l), inv_lo.reshape(nblk, r, 2 * l)], axis=1)
    return fwd_blk, inv_blk


def _hy_filter_kernel(l, feats_ref, dec_ref, w1_ref, b1_ref, w2_ref, b2_ref, w3_ref, fr_ref, fh_ref, fl_ref,
                      a_ref, b_ref, d_ref):
    fr = fr_ref[...]
    xdot = lambda a, b: _dot3(*_split(a), *_split(b))
    h = jnp.sin(fr * (xdot(feats_ref[...], w1_ref[...]) + b1_ref[...]))
    h = jnp.sin(fr * (xdot(h, w2_ref[...]) + b2_ref[...]))
    h = xdot(h, w3_ref[...])
    dec = jnp.concatenate([dec_ref[...]] * HY_ORDER, axis=-1)
    row0 = lax.broadcasted_iota(jnp.int32, (l, 1), 0) == 0
    hf = h[:, 0:HY_ORDER * D_HY] * dec
    hb = h[:, HY_ORDER * D_HY:2 * HY_ORDER * D_HY] * dec
    hb = jnp.where(row0, 0.0, hb)
    hs = _split(hf + hb)
    hd = _split(hf - hb)
    re = _dot3(fh_ref[0:l, :], fl_ref[0:l, :], *hs)
    ny = _dot3(fh_ref[l:l + 8, :], fl_ref[l:l + 8, :], *hs)[0:1]
    im = _dot3(fh_ref[l:2 * l, :], fl_ref[l:2 * l, :], *hd)
    for o in range(HY_ORDER):
        cols = slice(o * D_HY, (o + 1) * D_HY)
        a_ref[o] = re[:, cols]
        b_ref[o] = jnp.where(row0, 0.0, im[:, cols])
        d_ref[o] = jnp.where(row0, ny[:, cols], re[:, cols])


def _hy_filter(l, fwd_hi, fwd_lo, wts, li):
    pos = np.arange(l, dtype=np.float32)
    t = pos / np.float32(l - 1)
    bands = np.linspace(1e-4, HY_BANDS - 1, HY_BANDS, dtype=np.float32)
    ang = np.float32(2.0 * math.pi / l) * pos[:, None] * bands[None, :]
    feats = np.concatenate([t[:, None], np.cos(ang), -np.sin(ang)], axis=-1).astype(np.float32)
    feats = np.pad(feats, ((0, 0), (0, LANES - HY_EMB)))
    max_decay = math.log(HY_TARGET) / HY_FAST_DECAY
    min_decay = math.log(HY_TARGET) / HY_SLOW_DECAY
    deltas = np.abs(np.linspace(min_decay, max_decay, D_HY, dtype=np.float32))
    dec = np.exp(-t[:, None] * deltas[None, :]).astype(np.float32)
    spec = jax.ShapeDtypeStruct((HY_ORDER, l, D_HY), F32)
    names = ("hy_w1", "hy_b1", "hy_w2", "hy_b2", "hy_w3", "hy_freq")
    return pl.pallas_call(
        functools.partial(_hy_filter_kernel, l),
        grid=(1,),
        in_specs=[_full(feats.shape), _full(dec.shape)] + [_layer_const(wts[n], li) for n in names]
                 + [_full(fwd_hi.shape), _full(fwd_lo.shape)],
        out_specs=[_full(spec.shape)] * 3,
        out_shape=[spec] * 3,
        compiler_params=_cparams("arbitrary"),
        name="hyena_filter",
    )(jnp.asarray(feats), jnp.asarray(dec), *[wts[n] for n in names], fwd_hi, fwd_lo)


def _hy_kernel(nc, ng, u_ref, cw_ref, cb_ref, f_ref, g_ref, a_ref, b_ref, d_ref, bias_ref, y_ref,
               uc_scr, vh_scr, vl_scr, sh_scr, sl_scr, z_scr):
    l = nc * CHUNK
    r = min(DFT_BLOCK, l)
    nblk = l // r
    cw = cw_ref[...]
    cb = cb_ref[...]
    dot = lambda p, q: jnp.dot(p, q, preferred_element_type=F32)
    wide = lambda x: jnp.concatenate([x] * ng, axis=1)
    seq_cols = lambda g: slice(g * D_HY, (g + 1) * D_HY)

    def for_blocks(body):
        if nblk == 1:
            body(0)
        else:
            lax.fori_loop(0, nblk, lambda i, carry: (body(i), carry)[1], 0)

    def conv(c, carry):
        rows = _rows(c)
        for g in range(ng):
            uc = _conv3_chunk(u_ref.at[g], c, nc, cw, cb)
            uc_scr[g, rows, :] = uc
            vh_scr[rows, seq_cols(g)], vl_scr[rows, seq_cols(g)] = _split(uc[:, 0:D_HY])
        return carry

    lax.fori_loop(0, nc, conv, 0)

    def long_conv(o):
        def spectrum(i):
            rows = pl.ds(pl.multiple_of(i * r, r), r)
            rows_im = pl.ds(pl.multiple_of(l + i * r, r), r)
            p = dot(f_ref[i], vh_scr[...])
            pl_ = dot(f_ref[i, 0:2 * r, :], vl_scr[...])
            zr = p[0:r] + (p[2 * r:3 * r] + pl_[0:r])
            zi = p[r:2 * r] + (p[3 * r:4 * r] + pl_[r:2 * r])
            fa, fb, fd = wide(a_ref[o, rows, :]), wide(b_ref[o, rows, :]), wide(d_ref[o, rows, :])
            sh_scr[rows, :], sl_scr[rows, :] = _split(zr * fa - zi * fb)
            sh_scr[rows_im, :], sl_scr[rows_im, :] = _split(zr * fb + zi * fd)

        for_blocks(spectrum)

        def inverse(i):
            rows = pl.ds(pl.multiple_of(i * r, r), r)
            p = dot(g_ref[i], sh_scr[...])
            y = p[0:r] + (p[r:2 * r] + dot(g_ref[i, 0:r, :], sl_scr[...]))
            gate = jnp.concatenate([uc_scr[g, rows, (o + 1) * D_HY:(o + 2) * D_HY] for g in range(ng)], axis=1)
            if o == 0:
                v = jnp.concatenate([uc_scr[g, rows, 0:D_HY] for g in range(ng)], axis=1)
            else:
                v = z_scr[rows, :]
            out = gate * (y + v * wide(bias_ref[o:o + 1, :]))
            if o + 1 < HY_ORDER:
                z_scr[rows, :] = out
                vh_scr[rows, :], vl_scr[rows, :] = _split(out)
            else:
                for g in range(ng):
                    y_ref[g, rows, :] = out[:, seq_cols(g)]

        for_blocks(inverse)

    for o in range(HY_ORDER):
        long_conv(o)


def _hyena(u, wts, li, tables, spectra, nb, l, seq0):
    nc = l // CHUNK
    ng = min(HY_GROUP, nb)
    a, b, d = spectra
    names = ("hy_conv_w", "hy_conv_b")
    y = pl.pallas_call(
        functools.partial(_hy_kernel, nc, ng),
        grid=(nb // ng,),
        in_specs=[_seq_spec(l, HY_IN, ng, seq0)] + [_layer_const(wts[n], li) for n in names]
                 + [_const(tables[0].shape), _const(tables[1].shape),
                    _const((HY_ORDER, l, D_HY)), _const((HY_ORDER, l, D_HY)), _const((HY_ORDER, l, D_HY)),
                    _layer_const(wts["hy_bias"], li)],
        out_specs=pl.BlockSpec((ng, l, D_HY), lambda i: (i, 0, 0)),
        out_shape=jax.ShapeDtypeStruct((nb, l, D_HY), F32),
        scratch_shapes=[pltpu.VMEM((ng, l, HY_IN), F32),
                        pltpu.VMEM((l, ng * D_HY), BF16), pltpu.VMEM((l, ng * D_HY), BF16),
                        pltpu.VMEM((2 * l, ng * D_HY), BF16), pltpu.VMEM((2 * l, ng * D_HY), BF16),
                        pltpu.VMEM((l, ng * D_HY), F32)],
        compiler_params=_cparams("parallel"),
        name="hyena",
    )(u.reshape(-1, l, HY_IN), *[wts[n] for n in names], *tables, a, b, d, wts["hy_bias"])
    return y.reshape(nb * l, D_HY)


def _seg_rms(x, w):
    return x * lax.rsqrt(_seg_mean(x * x, HEAD_DIM) + NORM_EPS) * w


def _stack_q(q):
    lo = lax.broadcasted_iota(jnp.int32, (1, D_KV), 1) < HEAD_DIM
    qa = q[:, 0:D_KV]
    qb = q[:, D_KV:2 * D_KV]
    return jnp.concatenate([jnp.where(lo, qa, 0.0), jnp.where(lo, pltpu.roll(qa, HEAD_DIM, 1), 0.0),
                            jnp.where(lo, 0.0, pltpu.roll(qb, HEAD_DIM, 1)), jnp.where(lo, 0.0, qb)], axis=0)


def _unstack_o(o):
    r = o.shape[0] // ATT_HEADS
    lo = lax.broadcasted_iota(jnp.int32, (1, D_KV), 1) < HEAD_DIM
    ya = jnp.where(lo, o[0:r], pltpu.roll(o[r:2 * r], HEAD_DIM, 1))
    yb = jnp.where(lo, pltpu.roll(o[2 * r:3 * r], HEAD_DIM, 1), o[3 * r:4 * r])
    return jnp.concatenate([ya, yb], axis=1)


def _sink_col(sink_ref, li, r):
    rb = lax.broadcasted_iota(jnp.int32, (ATT_HEADS * r, 1), 0) // r
    col = jnp.full((ATT_HEADS * r, 1), sink_ref[li * ATT_HEADS + ATT_HEADS - 1], F32)
    for h in range(ATT_HEADS - 2, -1, -1):
        col = jnp.where(rb == h, sink_ref[li * ATT_HEADS + h], col)
    return col


def _ctx_attn_kernel(li, ng, slot, u_ref, qn_ref, kn_ref, sink_ref, *rest):
    y_ref, k_ref, v_ref = rest[-3:]
    l = u_ref.shape[1]
    sink = _sink_col(sink_ref, li, l)
    seqs = range(ng)
    u = [u_ref[g].astype(F32) for g in seqs]
    q = [_seg_rms(u[g][:, 0:D_ATT], qn_ref[...]) for g in seqs]
    k = [_seg_rms(u[g][:, D_ATT:D_ATT + D_KV], kn_ref[:, 0:D_KV]) for g in seqs]
    v = [u[g][:, D_ATT + D_KV:D_ATT + 2 * D_KV] for g in seqs]
    for g in seqs:
        _slot_view(k_ref, g, slot)[...] = k[g]
        _slot_view(v_ref, g, slot)[...] = v[g]
    s = [_bdot_nt(_stack_q(q[g]), k[g]) * (HEAD_DIM ** -0.5) for g in seqs]
    m = [jnp.maximum(jnp.max(s[g], axis=-1, keepdims=True), sink) for g in seqs]
    p = [jnp.exp(s[g] - m[g]) for g in seqs]
    den = [jnp.sum(p[g], axis=-1, keepdims=True) + jnp.exp(sink - m[g]) for g in seqs]
    o = [_bdot(p[g] * (1.0 / den[g]), v[g]) for g in seqs]
    for g in seqs:
        y_ref[g] = _unstack_o(o[g])


def _ctx_attention(u, wts, li, nb, l, seq0, prev=None):
    ng = min(ATT_GROUP, nb)
    whole = prev is None
    kv_spec, kv_shape = _slot_out((l, D_KV), nb, li, True, ng, whole)
    args = [u.reshape(-1, l, ATT_IN), wts["attn_q_norm"], wts["attn_k_norm"], wts["attn_sink"]]
    in_specs = [_seq_spec(l, ATT_IN, ng, seq0), _layer_const(wts["attn_q_norm"], li),
                _layer_const(wts["attn_k_norm"], li), pl.BlockSpec(memory_space=pltpu.SMEM)]
    aliases = _carry_prev(args, in_specs, [] if prev is None else list(prev), 1)
    y, k, v = pl.pallas_call(
        functools.partial(_ctx_attn_kernel, li, ng, li if whole else None),
        grid=(nb // ng,),
        in_specs=in_specs,
        out_specs=[_seq_spec(l, D_ATT, ng), kv_spec, kv_spec],
        out_shape=[jax.ShapeDtypeStruct((nb, l, D_ATT), F32), kv_shape, kv_shape],
        input_output_aliases=aliases,
        compiler_params=_cparams("parallel"),
        name="ctx_attention",
    )(*args)
    return y.reshape(nb * l, D_ATT), k, v


@functools.lru_cache(maxsize=None)
def _rope_tables_host(l):
    n_rows = l // GRID_W
    rows = np.repeat(np.arange(n_rows, dtype=np.float32), GRID_W)
    cols = np.tile(np.arange(GRID_W, dtype=np.float32), n_rows)
    nf = HEAD_DIM // 4
    inv = (np.float32(ROPE_BASE) ** (-np.arange(nf, dtype=np.float32) / np.float32(nf))).astype(np.float32)
    ar = rows[:, None] * inv[None, :]
    ac = cols[:, None] * inv[None, :]
    cos = np.concatenate([np.cos(ar), np.cos(ar), np.cos(ac), np.cos(ac)], axis=-1)
    sin = np.concatenate([-np.sin(ar), np.sin(ar), -np.sin(ac), np.sin(ac)], axis=-1)
    return (np.tile(cos, (1, ATT_HEADS)).astype(np.float32), np.tile(sin, (1, ATT_HEADS)).astype(np.float32))


def _rope(x, cos, sin):
    w = x.shape[-1]
    nf = HEAD_DIM // 4
    lane = lax.broadcasted_iota(jnp.int32, x.shape, 1)
    first = (lane % (2 * nf)) < nf
    partner = jnp.where(first, pltpu.roll(x, w - nf, 1), pltpu.roll(x, nf, 1))
    return x * cos + partner * sin


def _lat_attn_kernel(nblk, li, ng, u_ref, qn_ref, kn_ref, cos_ref, sin_ref, ck_ref, cv_ref, sink_ref,
                     y_ref, q_scr, k_scr, v_scr):
    blk = ATT_BLOCK
    l = nblk * blk
    seqs = range(ng)
    cos_q = cos_ref[...]
    sin_q = sin_ref[...]
    zeros = jnp.zeros((blk, D_KV), F32)
    for g in seqs:
        u = u_ref[g].astype(F32)
        q = _seg_rms(u[:, 0:D_ATT], qn_ref[...])
        k = _seg_rms(u[:, D_ATT:D_ATT + D_KV], kn_ref[:, 0:D_KV])
        q_scr[g] = _rope(q, cos_q, sin_q)
        k_scr[g, 0:blk, :] = zeros
        k_scr[g, blk + l:2 * blk + l, :] = zeros
        v_scr[g, 0:blk, :] = zeros
        v_scr[g, blk + l:2 * blk + l, :] = zeros
        k_scr[g, blk:blk + l, :] = _rope(k, cos_q[:, 0:D_KV], sin_q[:, 0:D_KV])
        v_scr[g, blk:blk + l, :] = u[:, D_ATT + D_KV:D_ATT + 2 * D_KV]

    scale = HEAD_DIM ** -0.5
    r = lax.broadcasted_iota(jnp.int32, (blk, 3 * blk), 0)
    cidx = lax.broadcasted_iota(jnp.int32, (blk, 3 * blk), 1)
    band = (cidx - r >= blk - WINDOW) & (cidx - r <= blk + WINDOW)
    sink = _sink_col(sink_ref, li, blk)

    def block(i, carry):
        rows = _rows(i)
        win = pl.ds(pl.multiple_of(i * blk, blk), 3 * blk)
        kpos = cidx + (i - 1) * blk
        valid = band & (kpos >= 0) & (kpos < l)
        valid = jnp.concatenate([valid] * ATT_HEADS, axis=0)
        qs = [_stack_q(q_scr[g, rows, :]) for g in seqs]
        s_loc = [jnp.where(valid, _bdot_nt(qs[g], k_scr[g, win, :]) * scale, -jnp.inf) for g in seqs]
        s_ctx = [_bdot_nt(qs[g], ck_ref[g]) * scale for g in seqs]
        m = [jnp.maximum(jnp.maximum(jnp.max(s_loc[g], axis=-1, keepdims=True),
                                     jnp.max(s_ctx[g], axis=-1, keepdims=True)), sink) for g in seqs]
        p_loc = [jnp.exp(s_loc[g] - m[g]) for g in seqs]
        p_ctx = [jnp.exp(s_ctx[g] - m[g]) for g in seqs]
        inv = [1.0 / (jnp.sum(p_loc[g], axis=-1, keepdims=True) + jnp.sum(p_ctx[g], axis=-1, keepdims=True)
                      + jnp.exp(sink - m[g])) for g in seqs]
        o = [_bdot(p_ctx[g] * inv[g], cv_ref[g]) + _bdot(p_loc[g] * inv[g], v_scr[g, win, :]) for g in seqs]
        for g in seqs:
            y_ref[g, rows, :] = _unstack_o(o[g])
        return carry

    lax.fori_loop(0, nblk, block, 0)


def _lat_attention(u, wts, li, ck, cv, nb, l, seq0):
    lc = ck.shape[2]
    nblk = l // ATT_BLOCK
    ng = min(ATT_GROUP, nb)
    cos, sin = _rope_tables_host(l)
    cache_spec = pl.BlockSpec((ng, None, lc, D_KV), lambda b: (b, li, 0, 0))
    y = pl.pallas_call(
        functools.partial(_lat_attn_kernel, nblk, li, ng),
        grid=(nb // ng,),
        in_specs=[_seq_spec(l, ATT_IN, ng, seq0), _layer_const(wts["attn_q_norm"], li),
                  _layer_const(wts["attn_k_norm"], li), _const((l, D_ATT)), _const((l, D_ATT)),
                  cache_spec, cache_spec, pl.BlockSpec(memory_space=pltpu.SMEM)],
        out_specs=_seq_spec(l, D_ATT, ng),
        out_shape=jax.ShapeDtypeStruct((nb, l, D_ATT), F32),
        scratch_shapes=[pltpu.VMEM((ng, l, D_ATT), F32), pltpu.VMEM((ng, l + 2 * ATT_BLOCK, D_KV), F32),
                        pltpu.VMEM((ng, l + 2 * ATT_BLOCK, D_KV), F32)],
        compiler_params=_cparams("parallel"),
        name="lat_attention",
    )(u.reshape(-1, l, ATT_IN), wts["attn_q_norm"], wts["attn_k_norm"], jnp.asarray(cos), jnp.asarray(sin),
      ck, cv, wts["attn_sink"])
    return y.reshape(nb * l, D_ATT)


def _pad_lane_tile(w):
    return jnp.pad(w, [(0, 0)] * (w.ndim - 1) + [(0, -w.shape[-1] % LANES)])


def _dt_weight(w):
    zeros = jnp.zeros(w.shape[:-1] + (DT_PAD - DT_REP * N_DT,), w.dtype)
    return jnp.concatenate([w[..., C_DT:C_DT + N_DT]] * DT_REP + [zeros], axis=-1).astype(BF16)


def _prep_weights(p):
    row = lambda a: a.reshape(DEPTH, 1, -1)
    pad_lanes = lambda a: jnp.pad(row(a), ((0, 0), (0, 0), (0, LANES - a[0].size)))
    return dict(
        norm_w=p["norm_w"].reshape(DEPTH * 3, 1, D_MODEL),
        ffn_w_in=p["ffn_w_in"].astype(BF16).reshape(DEPTH * 2, D_MODEL, 2 * D_FF),
        ffn_w_out=p["ffn_w_out"].astype(BF16).reshape(DEPTH * 2, D_FF, D_MODEL),
        mix_w_in=_pad_lane_tile(p["mix_w_in"]).astype(BF16), mix_w_dt=_dt_weight(p["mix_w_in"]),
        mix_w_out=p["mix_w_out"].astype(BF16),
        ssd_conv_w=p["ssd_conv_w"], ssd_conv_b=row(p["ssd_conv_b"]),
        ssd_dt_bias=pad_lanes(jnp.tile(row(p["ssd_dt_bias"]), (1, 1, DT_REP))),
        ssd_a_log=pad_lanes(jnp.tile(row(p["ssd_a_log"]), (1, 1, DT_REP))),
        ssd_d=row(jnp.repeat(p["ssd_d"], SSM_HEAD_DIM, axis=-1)), ssd_norm_w=row(p["ssd_norm_w"]),
        hy_conv_w=p["hy_conv_w"], hy_conv_b=row(p["hy_conv_b"]), hy_bias=p["hy_bias"],
        hy_w1=jnp.pad(p["hy_w1"], ((0, 0), (0, LANES - HY_EMB), (0, 0))), hy_b1=row(p["hy_b1"]),
        hy_w2=p["hy_w2"], hy_b2=row(p["hy_b2"]), hy_w3=p["hy_w3"], hy_freq=row(p["hy_freq"]),
        ret_decay_logit=pad_lanes(p["ret_decay_logit"]), ret_gn_w=row(p["ret_gn_w"]),
        attn_q_norm=row(jnp.tile(p["attn_q_norm"], (1, ATT_HEADS))),
        attn_k_norm=row(jnp.tile(p["attn_k_norm"], (1, ATT_HEADS))),
        attn_sink=p["attn_sink"].reshape(DEPTH * ATT_HEADS),
    )


def _mixers(u, wts, li, nb, l, seq0, ssd_s0, ret_s0, ctx_kv, hy_tables, hy_spectra, carried):
    z, xbc, hy, ret, att, dt = u
    ctx = ctx_kv is None
    y_ssd, s_ssd = _ssd(z, xbc, dt, wts, li, ssd_s0, nb, l, seq0, stacked=ctx, prev=carried.get("ssd"))
    y_hy = _hyena(hy, wts, li, hy_tables, hy_spectra, nb, l, seq0)
    y_ret, s_ret = _retention(ret, wts, li, ret_s0, nb, l, seq0, stacked=ctx, prev=carried.get("ret"))
    if ctx:
        y_att, k, v = _ctx_attention(att, wts, li, nb, l, seq0, prev=carried.get("kv"))
        carried = dict(ssd=s_ssd, ret=s_ret, kv=(k, v))
    else:
        y_att = _lat_attention(att, wts, li, ctx_kv[0], ctx_kv[1], nb, l, seq0)
    return (y_ssd, y_hy, y_ret, y_att), carried


def kernel(x_prompt, x_sample, cache_k, cache_v, state_ssd, state_ret, c, c_ctx, w_mod, b_mod, norm_w, ffn_w_in, ffn_w_out, mix_w_in, mix_w_out, ssd_conv_w, ssd_conv_b, ssd_dt_bias, ssd_a_log, ssd_d, ssd_norm_w, hy_conv_w, hy_conv_b, hy_w1, hy_b1, hy_w2, hy_b2, hy_w3, hy_freq, hy_bias, ret_decay_logit, ret_gn_w, attn_q_norm, attn_k_norm, attn_sink):
    bp, lp_len, _ = x_prompt.shape
    bs, ls_len, _ = x_sample.shape
    lc = cache_k.shape[2]

    cond = jnp.concatenate([c_ctx[None, :], c, jnp.zeros((MOD_ROWS - 1 - bs, D_MODEL), F32)], axis=0)
    mod = _modulation(cond, w_mod, b_mod)
    wts = _prep_weights(dict(
        norm_w=norm_w, ffn_w_in=ffn_w_in, ffn_w_out=ffn_w_out, mix_w_in=mix_w_in, mix_w_out=mix_w_out,
        ssd_conv_w=ssd_conv_w, ssd_conv_b=ssd_conv_b, ssd_dt_bias=ssd_dt_bias, ssd_a_log=ssd_a_log, ssd_d=ssd_d,
        ssd_norm_w=ssd_norm_w, hy_conv_w=hy_conv_w, hy_conv_b=hy_conv_b, hy_w1=hy_w1, hy_b1=hy_b1, hy_w2=hy_w2,
        hy_b2=hy_b2, hy_w3=hy_w3, hy_freq=hy_freq, hy_bias=hy_bias, ret_decay_logit=ret_decay_logit,
        ret_gn_w=ret_gn_w, attn_q_norm=attn_q_norm, attn_k_norm=attn_k_norm, attn_sink=attn_sink))
    ck = cache_k.reshape(bs, DEPTH, lc, D_KV)
    cv = cache_v.reshape(bs, DEPTH, lc, D_KV)

    tab_p = _dft_tables(lp_len)
    tab_s = _dft_tables(ls_len)
    blk_p = _dft_block_tables(tab_p, lp_len)
    blk_s = _dft_block_tables(tab_s, ls_len)

    tm = TOKEN_TILE
    tp, ts = bp * lp_len, bs * ls_len
    assert tp % tm == 0 and ts % tm == 0 and ls_len % tm == 0 and tp % ls_len == 0
    st = _Stream(tm, tp // tm, ts // tm, ls_len)
    x = (x_prompt.reshape(tp, D_MODEL), x_sample.reshape(ts, D_MODEL))
    carried = {}
    for li in range(DEPTH):
        spec_p = _hy_filter(lp_len, tab_p[0], tab_p[1], wts, li)
        spec_s = _hy_filter(ls_len, tab_s[0], tab_s[1], wts, li)
        x = _ffn(st, x, mod, wts, li, 0)
        u = _inproj(st, x, mod, wts, li)
        y_p, carried = _mixers(u, wts, li, bp, lp_len, 0, None, None, None, blk_p, spec_p, carried)
        y_s, _ = _mixers(u, wts, li, bs, ls_len, tp // ls_len, state_ssd, state_ret, (ck, cv), blk_s, spec_s, {})
        if li + 1 < DEPTH:
            x = _ffn(st, x, mod, wts, li, 1, mix=tuple(zip(y_p, y_s)))
        else:
            yp = _ffn(st.part(0), x, mod, wts, li, 1, mix=y_p)
            ys = _ffn(st.part(1), x, mod, wts, li, 1, mix=y_s)

    kv_shape = (bp, DEPTH, lp_len, ATT_KV_HEADS, HEAD_DIM)
    new_k, new_v = carried["kv"]
    return (yp.reshape(bp, lp_len, D_MODEL), ys.reshape(bs, ls_len, D_MODEL),
            new_k.reshape(kv_shape), new_v.reshape(kv_shape), carried["ssd"], carried["ret"])
```

```python
import functools
import math

import numpy as np
import jax
import jax.numpy as jnp
from jax import lax
from jax.experimental import pallas as pl
from jax.experimental.pallas import tpu as pltpu

F32 = jnp.float32
BF16 = jnp.bfloat16

D_MODEL = 1024
DEPTH = 2
GRID_W = 64
D_FF = 2816
N_MOD = 9
NORM_EPS = 1e-6
CHUNK = 128
D_SSM = 256
SSM_HEADS = 4
SSM_HEAD_DIM = 64
SSM_STATE = 128
SSM_GROUPS = 2
SSM_CONV_CH = D_SSM + 2 * SSM_GROUPS * SSM_STATE
D_HY = 256
HY_ORDER = 2
HY_BANDS = 16
HY_EMB = 1 + 2 * HY_BANDS
HY_HIDDEN = 64
HY_FAST_DECAY = 0.3
HY_SLOW_DECAY = 1.5
HY_TARGET = 1e-2
HY_IN = (HY_ORDER + 1) * D_HY
D_RET = 256
RET_HEADS = 4
RET_HEAD_DIM = 64
RET_IN = 4 * D_RET
ATT_HEADS = 4
ATT_KV_HEADS = 2
HEAD_DIM = 64
D_ATT = ATT_HEADS * HEAD_DIM
D_KV = ATT_KV_HEADS * HEAD_DIM
ATT_IN = D_ATT + 2 * D_KV
WINDOW = 128
ATT_BLOCK = 128
ROPE_BASE = 10000.0
D_MIX = D_SSM + D_HY + D_RET + D_ATT

LANES = 128
DT_PAD = LANES
VMEM_LIMIT = 56 * 1024 * 1024
MOD_ROWS = 8

TOKEN_TILE = 512
DFT_BLOCK = 512
HY_GROUP = 4
SEQ_GROUP = 4
RET_GROUP = 8
ATT_GROUP = 4


def _cparams(*sem):
    return pltpu.CompilerParams(dimension_semantics=sem, vmem_limit_bytes=VMEM_LIMIT)


def _rms(x, w):
    return x * lax.rsqrt(jnp.mean(x * x, axis=-1, keepdims=True) + NORM_EPS) * w


def _silu(x):
    return x * (1.0 / (1.0 + jnp.exp(-x)))


def _softplus(x):
    return jnp.maximum(x, 0.0) + jnp.log1p(jnp.exp(-jnp.abs(x)))


def _bdot(a, b):
    return jnp.dot(a.astype(BF16), b.astype(BF16), preferred_element_type=F32)


def _bdot_nt(a, b):
    return lax.dot_general(a.astype(BF16), b.astype(BF16), (((1,), (1,)), ((), ())),
                           preferred_element_type=F32)


def _bdot_tn(a, b):
    return lax.dot_general(a.astype(BF16), b.astype(BF16), (((0,), (0,)), ((), ())),
                           preferred_element_type=F32)


def _full(shape):
    n = len(shape)
    return pl.BlockSpec(shape, lambda *_: (0,) * n)


def _const(shape):
    n = len(shape)
    return pl.BlockSpec(shape, lambda *_: (0,) * n, pipeline_mode=pl.Buffered(1))


def _layer_const(arr, li):
    tail = arr.shape[1:]
    zeros = (0,) * len(tail)
    return pl.BlockSpec((None,) + tail, lambda *_: (li,) + zeros, pipeline_mode=pl.Buffered(1))


class _Stream:
    def __init__(self, tm, nct, nlt, l_lat, first=0, count=None):
        self.tm, self.nct, self.nlt, self.l_lat = tm, nct, nlt, l_lat
        self.first = first
        self.count = nct + nlt if count is None else count

    def part(self, path):
        first, count = (0, self.nct) if path == 0 else (self.nct, self.nlt)
        return _Stream(self.tm, self.nct, self.nlt, self.l_lat, first, count)

    @property
    def tokens(self):
        return self.count * self.tm

    def merged(self, width):
        return pl.BlockSpec((self.tm, width), lambda i: (i + self.first, 0))

    def owned(self, width):
        return pl.BlockSpec((self.tm, width), lambda i: (i, 0))

    def pair(self, width):
        ctx = pl.BlockSpec((self.tm, width), lambda i: (jnp.minimum(i + self.first, self.nct - 1), 0))
        lat = pl.BlockSpec((self.tm, width), lambda i: (jnp.maximum(i + self.first - self.nct, 0), 0))
        return [ctx, lat]

    def mod(self, k, row_ctx, row_lat):
        def index(i):
            t = i + self.first
            lat_row = row_lat + (jnp.maximum(t - self.nct, 0) * self.tm) // self.l_lat
            return (jnp.where(t < self.nct, row_ctx, lat_row), 0, k)
        return pl.BlockSpec((None, 1, D_MODEL), index)


def _seq_spec(l, w, ng=None, seq0=0):
    step0 = seq0 // (ng or 1)
    return pl.BlockSpec((ng, l, w), lambda b: (b + step0, 0, 0))


def _head_masks(width, heads):
    lane = lax.broadcasted_iota(jnp.int32, (1, width), 1)
    hd = width // heads
    return [(lane >= h * hd) & (lane < (h + 1) * hd) for h in range(heads)]


def _by_head(masks, vals):
    out = vals[-1]
    for m, v in zip(masks[-2::-1], vals[-2::-1]):
        out = jnp.where(m, v, out)
    return out


def _block_diag(n, blk, value):
    i = lax.broadcasted_iota(jnp.int32, (n, n), 0) // blk
    j = lax.broadcasted_iota(jnp.int32, (n, n), 1) // blk
    return jnp.where(i == j, value, 0.0).astype(F32)


def _mod_kernel(c_ref, w_ref, b_ref, o_ref):
    c = c_ref[...]
    o_ref[...] = _bdot(_silu(c), w_ref[...]) + b_ref[...]


def _modulation(cond, w_mod, b_mod):
    out = pl.pallas_call(
        _mod_kernel,
        grid=(DEPTH, N_MOD),
        in_specs=[pl.BlockSpec((MOD_ROWS, D_MODEL), lambda l, j: (0, 0)),
                  pl.BlockSpec((None, D_MODEL, D_MODEL), lambda l, j: (l, 0, j)),
                  pl.BlockSpec((None, 1, D_MODEL), lambda l, j: (l, 0, j))],
        out_specs=pl.BlockSpec((None, MOD_ROWS, D_MODEL), lambda l, j: (l, 0, j)),
        out_shape=jax.ShapeDtypeStruct((DEPTH, MOD_ROWS, N_MOD * D_MODEL), F32),
        compiler_params=_cparams("arbitrary", "arbitrary"),
        name="modulation",
    )(cond, w_mod, b_mod.reshape(DEPTH, 1, N_MOD * D_MODEL))
    return out.reshape(DEPTH * MOD_ROWS, 1, N_MOD * D_MODEL)


def _ffn_kernel(n_mix, pair_x, pair_mix, nct, first, *refs):
    refs = list(refs)
    o_ref = refs.pop()
    is_ctx = pl.program_id(0) + first < nct

    def read(pair):
        if pair:
            a, b = refs.pop(0), refs.pop(0)
            return jnp.where(is_ctx, a[...], b[...])
        return refs.pop(0)[...]

    x = read(pair_x)
    sh_ref, sc_ref, g_ref, nw_ref, wi_ref, wo_ref = (refs.pop(0) for _ in range(6))
    if n_mix:
        gm_ref = refs.pop(0)
        ys = [read(pair_mix) for _ in range(n_mix)]
        wm_ref = refs.pop(0)
        w = D_MIX // n_mix
        acc = _bdot(ys[0], wm_ref[0:w, :])
        for j in range(1, n_mix):
            acc += _bdot(ys[j], wm_ref[j * w:(j + 1) * w, :])
        x = x + gm_ref[...] * acc
    h = (_rms(x, nw_ref[...] * (1.0 + sc_ref[...])) + sh_ref[...]).astype(BF16)
    gate = jnp.dot(h, wi_ref[:, 0:D_FF], preferred_element_type=F32)
    up = jnp.dot(h, wi_ref[:, D_FF:2 * D_FF], preferred_element_type=F32)
    o_ref[...] = x + 0.5 * g_ref[...] * _bdot(_silu(gate) * up, wo_ref[...])


def _ffn(st, x, mod, wts, li, k, mix=()):
    row_ctx, row_lat = li * MOD_ROWS, li * MOD_ROWS + 1
    w_in, w_out = wts["ffn_w_in"], wts["ffn_w_out"]
    pair_x = isinstance(x, tuple)
    pair_mix = bool(mix) and isinstance(mix[0], tuple)
    args = list(x) if pair_x else [x]
    in_specs = st.pair(D_MODEL) if pair_x else [st.merged(D_MODEL)]
    args += [mod, mod, mod, wts["norm_w"], w_in, w_out]
    in_specs += [st.mod(6 * k + j, row_ctx, row_lat) for j in range(3)]
    in_specs += [_layer_const(wts["norm_w"], 3 * li + 2 * k), _layer_const(w_in, 2 * li + k),
                 _layer_const(w_out, 2 * li + k)]
    if mix:
        args.append(mod)
        in_specs.append(st.mod(5, row_ctx, row_lat))
        for y in mix:
            width = (y[0] if pair_mix else y).shape[1]
            args += list(y) if pair_mix else [y]
            in_specs += st.pair(width) if pair_mix else [st.owned(width)]
        args.append(wts["mix_w_out"])
        in_specs.append(_layer_const(wts["mix_w_out"], li))
    return pl.pallas_call(
        functools.partial(_ffn_kernel, len(mix), pair_x, pair_mix, st.nct, st.first),
        grid=(st.count,),
        in_specs=in_specs,
        out_specs=st.owned(D_MODEL),
        out_shape=jax.ShapeDtypeStruct((st.tokens, D_MODEL), F32),
        compiler_params=_cparams("parallel"),
        name="ffn",
    )(*args)


_IN_SPLITS = (("z", D_SSM, BF16), ("xbc", SSM_CONV_CH, F32), ("hy", HY_IN, F32), ("ret", RET_IN, BF16),
              ("att", ATT_IN, BF16), ("dt", DT_PAD, F32))


C_DT = D_SSM + SSM_CONV_CH
N_DT = 2 * SSM_HEADS
W_TAIL = HY_IN + RET_IN + ATT_IN


def _inproj_kernel(x_ref, sh_ref, sc_ref, nw_ref, w_ref, wdt_ref, *rest):
    o_refs, wt_scr = rest[:-1], rest[-1]

    @pl.when(pl.program_id(0) == 0)
    def _():
        wt_scr[...] = w_ref[:, C_DT + N_DT:C_DT + N_DT + W_TAIL]

    h = (_rms(x_ref[...], nw_ref[...] * (1.0 + sc_ref[...])) + sh_ref[...]).astype(BF16)
    off = 0
    for (name, width, dtype), o_ref in zip(_IN_SPLITS, o_refs):
        if name == "dt":
            w = wdt_ref[...]
        elif off < C_DT:
            w = w_ref[:, off:off + width]
        else:
            w = wt_scr[:, off - C_DT:off - C_DT + width]
        o_ref[...] = jnp.dot(h, w, preferred_element_type=F32).astype(dtype)
        off += width


def _inproj(st, x, mod, wts, li):
    row_ctx, row_lat = li * MOD_ROWS, li * MOD_ROWS + 1
    return pl.pallas_call(
        _inproj_kernel,
        grid=(st.count,),
        in_specs=[st.merged(D_MODEL), st.mod(3, row_ctx, row_lat), st.mod(4, row_ctx, row_lat),
                  _layer_const(wts["norm_w"], 3 * li + 1),
                  _layer_const(wts["mix_w_in"], li), _layer_const(wts["mix_w_dt"], li)],
        out_specs=[st.owned(width) for _, width, _ in _IN_SPLITS],
        out_shape=[jax.ShapeDtypeStruct((st.tokens, width), dtype) for _, width, dtype in _IN_SPLITS],
        scratch_shapes=[pltpu.VMEM((D_MODEL, W_TAIL), BF16)],
        compiler_params=_cparams("arbitrary"),
        name="mix_in",
    )(x, mod, mod, wts["norm_w"], wts["mix_w_in"], wts["mix_w_dt"])


def _conv3_chunk(x_ref, c, nc, w, b):
    q = CHUNK
    l = nc * q
    r0 = pl.multiple_of(c * q, q)
    x = x_ref[pl.ds(r0, q), :]
    prev = x_ref[pl.ds(jnp.maximum(r0 - 1, 0), 1), :]
    nxt = x_ref[pl.ds(jnp.minimum(r0 + q, l - 1), 1), :]
    prev = jnp.where(c > 0, prev, 0.0)
    nxt = jnp.where(c < nc - 1, nxt, 0.0)
    rid = lax.broadcasted_iota(jnp.int32, (q, 1), 0)
    xm1 = jnp.where(rid == 0, prev, pltpu.roll(x, 1, 0))
    xp1 = jnp.where(rid == q - 1, nxt, pltpu.roll(x, q - 1, 0))
    return xm1 * w[0:1, :] + x * w[1:2, :] + xp1 * w[2:3, :] + b


def _tri(lower):
    i = lax.broadcasted_iota(jnp.int32, (CHUNK, CHUNK), 0)
    j = lax.broadcasted_iota(jnp.int32, (CHUNK, CHUNK), 1)
    return (j <= i) if lower else (j >= i)


def _rows(c):
    return pl.ds(pl.multiple_of(c * CHUNK, CHUNK), CHUNK)


def _split_cat(v, parts, axis):
    out, r = [], v
    for i in range(parts):
        piece = r.astype(BF16)
        out.append(piece)
        if i + 1 < parts:
            r = r - piece.astype(F32)
    return jnp.concatenate(out, axis=axis)


def _seg_mean(x, seg):
    w = x.shape[-1]
    ones = _block_diag(w, seg, 1.0).astype(BF16)
    return jnp.dot(_split_cat(x, 2, axis=1), jnp.concatenate([ones, ones], axis=0),
                   preferred_element_type=F32) * (1.0 / seg)


def _stack_heads(x, masks):
    return jnp.concatenate([jnp.where(m, x, 0.0) for m in masks], axis=0)


def _slot_out(tail, nb, li, stacked, ng, whole):
    zeros = (0,) * len(tail)
    slot = li if stacked else 0
    if whole:
        spec = pl.BlockSpec((ng, DEPTH) + tail, lambda b: (b, 0) + zeros)
    else:
        spec = pl.BlockSpec((ng, None) + tail, lambda b: (b, slot) + zeros)
    return spec, jax.ShapeDtypeStruct((nb, DEPTH if stacked else 1) + tail, F32)


def _slot_view(ref, g, slot):
    if slot is None:
        return ref.at[g]
    for other in range(DEPTH):
        if other != slot:
            ref[g, other] = jnp.zeros(ref.shape[2:], ref.dtype)
    return ref.at[g, slot]


def _carry_prev(args, in_specs, prevs, first_out):
    aliases = {}
    for j, prev in enumerate(prevs):
        aliases[len(args)] = first_out + j
        args.append(prev)
        in_specs.append(pl.BlockSpec(memory_space=pl.ANY))
    return aliases


SSD_SEL_W = 2 * SSM_HEADS * SSM_STATE
CUM_PIECES = 3
WGT_PIECES = 2
DT_REP = CUM_PIECES + WGT_PIECES


def _ssd_kernel(nc, ng, has_s0, slot, *refs):
    refs = list(refs)
    (z_ref, xbc_ref, dt_ref, cw_ref, cb_ref, dtb_ref, alog_ref, dsk_ref, nw_ref) = refs[:9]
    s0_ref = refs[9] if has_s0 else None
    (y_ref, s_ref, xs_scr, xk_scr, xt_scr, dt_scr, cum_scr, dtt_scr, cumt_scr, yf_scr, yb_scr, st_scr,
     sel_scr) = refs[-13:]
    q = CHUNK
    nh, n, p = SSM_HEADS, SSM_STATE, SSM_HEAD_DIM
    rep = nh // SSM_GROUPS
    hm = _head_masks(D_SSM, nh)

    @pl.when(pl.program_id(0) == 0)
    def _():
        k = lax.broadcasted_iota(jnp.int32, (LANES, SSD_SEL_W), 0)
        col = lax.broadcasted_iota(jnp.int32, (LANES, SSD_SEL_W), 1)
        is_cum = col < nh * n
        grp = k // (2 * nh)
        grp_ok = (is_cum & (grp < CUM_PIECES)) | ((~is_cum) & (grp >= CUM_PIECES) & (grp < DT_REP))
        hit = grp_ok & (k % nh == (col % (nh * n)) // n)
        sel_scr[...] = jnp.where(hit, 1.0, 0.0).astype(BF16)

    cw = cw_ref[...]
    cb = cb_ref[...]
    neg_a = -jnp.exp(alog_ref[...])
    dtb = dtb_ref[...]

    class Seq:
        def __init__(self, g):
            self.z, self.xbc, self.dt_in, self.y, self.s = (z_ref.at[g], xbc_ref.at[g], dt_ref.at[g], y_ref.at[g],
                                                            _slot_view(s_ref, g, slot))
            self.s0 = s0_ref.at[g] if has_s0 else None
            self.xs, self.xk, self.xt, self.dt, self.cum = (xs_scr.at[g], xk_scr.at[g], xt_scr.at[g], dt_scr.at[g],
                                                            cum_scr.at[g])
            self.dtt, self.cumt, self.yf, self.yb, self.st = (dtt_scr.at[g], cumt_scr.at[g], yf_scr.at[g],
                                                              yb_scr.at[g], st_scr.at[g])

    seqs = [Seq(g) for g in range(ng)]

    def prep_seq(sq, c):
        rows = _rows(c)
        xall = _silu(_conv3_chunk(sq.xbc, c, nc, cw, cb))
        sq.xs[rows, :] = xall
        xs = xall[:, 0:D_SSM]
        sq.xk[c] = _stack_heads(xs, hm).astype(BF16)
        sq.xt[c] = xs.T.astype(BF16)
        dt = _softplus(sq.dt_in[rows, :] + dtb)
        la = _split_cat(dt * neg_a, 3, axis=0)
        tri = jnp.concatenate([_tri(True), _tri(False)], axis=0)
        tri = jnp.where(tri, 1.0, 0.0).astype(BF16)
        cs = jnp.dot(jnp.concatenate([tri] * 3, axis=1), la, preferred_element_type=F32)
        lane = lax.broadcasted_iota(jnp.int32, (1, LANES), 1)
        cum = jnp.where(lane % (2 * nh) < nh, cs[0:q], cs[q:2 * q])
        sq.dt[rows, :] = dt
        sq.cum[rows, :] = cum
        sq.dtt[c] = dt.T
        sq.cumt[c] = cum.T

    def prep(c, carry):
        for sq in seqs:
            prep_seq(sq, c)
        return carry

    lax.fori_loop(0, nc, prep, 0)

    for sq in seqs:
        for d in range(2):
            for h in range(nh):
                blk = slice(h * p, (h + 1) * p)
                sq.st[d, blk, :] = sq.s0[d, h].T if has_s0 else jnp.zeros((p, n), F32)

    def pack_scalars(sq, c, d):
        rows = _rows(c)
        dt = sq.dt[rows, :]
        cum = sq.cum[rows, :]
        edge = q - 1 if d == 0 else 0
        lane = lax.broadcasted_iota(jnp.int32, (1, LANES), 1)
        used = (lane < DT_REP * 2 * nh) & ((lane % (2 * nh)) // nh == d)
        cum = jnp.where(used, cum, 0.0)
        wgt = jnp.exp(cum[edge:edge + 1, :] - cum) * dt
        grp = lane // (2 * nh)
        packed = jnp.zeros_like(cum)
        for src, pieces, g0 in ((cum, CUM_PIECES, 0), (wgt, WGT_PIECES, CUM_PIECES)):
            rest = src
            for i in range(pieces):
                piece = rest.astype(BF16).astype(F32)
                packed = jnp.where(grp == g0 + i, piece, packed)
                rest = rest - piece
        return jnp.where(used, packed, 0.0).astype(BF16)

    def operands(sq, c, d):
        rows = _rows(c)
        bm = [sq.xs[rows, D_SSM + g * n:D_SSM + (g + 1) * n] for g in range(SSM_GROUPS)]
        cm = [sq.xs[rows, D_SSM + (SSM_GROUPS + g) * n:D_SSM + (SSM_GROUPS + g + 1) * n]
              for g in range(SSM_GROUPS)]
        return bm, cm

    def decay_blocks(sq, c, d, e, cb_t):
        mask = _tri(d == 0)
        blocks, ecol = [], []
        for h in range(nh):
            r = d * nh + h
            col = e[:, h * n:(h + 1) * n]
            decay = jnp.exp(jnp.where(mask, col - sq.cumt[c, r:r + 1, :], -jnp.inf))
            blocks.append((cb_t[h // rep] * decay * sq.dtt[c, r:r + 1, :]).astype(BF16))
            ecol.append(jnp.exp(col))
        lo = lax.broadcasted_iota(jnp.int32, (1, n), 1) < p
        ecol = jnp.concatenate([jnp.where(lo, ecol[2 * j], ecol[2 * j + 1]) for j in range(nh // 2)], axis=1)
        return jnp.concatenate(blocks, axis=1), ecol

    def scan(i, carry):
        j = nc - 1 - i
        jobs = [(sq, c, d) for sq in seqs for c, d in ((i, 0), (j, 1))]
        packed = jnp.concatenate([pack_scalars(*job) for job in jobs], axis=0)
        e_all = jnp.dot(packed, sel_scr[...], preferred_element_type=F32)
        es = [e_all[t * q:(t + 1) * q] for t in range(len(jobs))]
        ops = [operands(*job) for job in jobs]
        cb_t = [[_bdot_nt(cm[g], bm[g]) for g in range(SSM_GROUPS)] for bm, cm in ops]
        dec = [decay_blocks(*job, es[t], cb_t[t]) for t, job in enumerate(jobs)]
        intra = [jnp.dot(dec[t][0], sq.xk[c], preferred_element_type=F32) for t, (sq, c, d) in enumerate(jobs)]
        s_old = [sq.st[d] for sq, c, d in jobs]
        inter = [jnp.concatenate([_bdot_nt(cm[g], s_old[t][g * rep * p:(g + 1) * rep * p, :])
                                  for g in range(SSM_GROUPS)], axis=1) for t, (bm, cm) in enumerate(ops)]
        for t, (sq, c, d) in enumerate(jobs):
            bm = ops[t][0]
            edge = q - 1 if d == 0 else 0
            for h in range(nh):
                blk, cols = slice(h * p, (h + 1) * p), slice(h * n, (h + 1) * n)
                wk = (bm[h // rep] * es[t][:, nh * n + h * n:nh * n + (h + 1) * n]).astype(BF16)
                upd = jnp.dot(sq.xt[c, blk, :], wk, preferred_element_type=F32)
                sq.st[d, blk, :] = jnp.exp(es[t][edge:edge + 1, cols]) * s_old[t][blk, :] + upd
            y = intra[t] + dec[t][1] * inter[t]
            if d == 0:
                sq.yf[_rows(c), :] = y
            else:
                sq.yb[_rows(c), :] = y
        return carry

    lax.fori_loop(0, nc, scan, 0)

    for sq in seqs:
        for d in range(2):
            for h in range(nh):
                sq.s[d, h] = sq.st[d, h * p:(h + 1) * p, :].T

    dsk = dsk_ref[...]
    nw = nw_ref[...]

    def fin(c, carry):
        rows = _rows(c)
        for sq in seqs:
            y = sq.yf[rows, :] + sq.yb[rows, :] + dsk * sq.xs[rows, 0:D_SSM]
            sq.y[rows, :] = _rms(y * _silu(sq.z[rows, :].astype(F32)), nw)
        return carry

    lax.fori_loop(0, nc, fin, 0)


def _ssd(z, xbc, dt, wts, li, s0, nb, l, seq0, stacked=False, prev=None):
    nc = l // CHUNK
    has_s0 = s0 is not None
    names = ("ssd_conv_w", "ssd_conv_b", "ssd_dt_bias", "ssd_a_log", "ssd_d", "ssd_norm_w")
    ng = min(SEQ_GROUP, nb)
    args = [z.reshape(-1, l, D_SSM), xbc.reshape(-1, l, SSM_CONV_CH), dt.reshape(-1, l, DT_PAD)]
    args += [wts[n] for n in names]
    in_specs = [_seq_spec(l, D_SSM, ng, seq0), _seq_spec(l, SSM_CONV_CH, ng, seq0), _seq_spec(l, DT_PAD, ng, seq0)]
    in_specs += [_layer_const(wts[n], li) for n in names]
    if has_s0:
        args.append(s0)
        in_specs.append(pl.BlockSpec((ng, None, 2, SSM_HEADS, SSM_STATE, SSM_HEAD_DIM),
                                     lambda b: (b, li, 0, 0, 0, 0)))
    whole = stacked and prev is None
    st_spec, st_shape = _slot_out((2, SSM_HEADS, SSM_STATE, SSM_HEAD_DIM), nb, li, stacked, ng, whole)
    aliases = _carry_prev(args, in_specs, [] if prev is None else [prev], 1)
    y, s = pl.pallas_call(
        functools.partial(_ssd_kernel, nc, ng, has_s0, li if whole else None),
        grid=(nb // ng,),
        in_specs=in_specs,
        out_specs=[_seq_spec(l, D_SSM, ng), st_spec],
        out_shape=[jax.ShapeDtypeStruct((nb, l, D_SSM), F32), st_shape],
        input_output_aliases=aliases,
        scratch_shapes=[pltpu.VMEM((ng, l, SSM_CONV_CH), F32),
                        pltpu.VMEM((ng, nc, SSM_HEADS * CHUNK, D_SSM), BF16),
                        pltpu.VMEM((ng, nc, D_SSM, CHUNK), BF16),
                        pltpu.VMEM((ng, l, DT_PAD), F32), pltpu.VMEM((ng, l, DT_PAD), F32),
                        pltpu.VMEM((ng, nc, DT_PAD, CHUNK), F32), pltpu.VMEM((ng, nc, DT_PAD, CHUNK), F32),
                        pltpu.VMEM((ng, l, D_SSM), F32), pltpu.VMEM((ng, l, D_SSM), F32),
                        pltpu.VMEM((ng, 2, D_SSM, SSM_STATE), F32),
                        pltpu.VMEM((LANES, SSD_SEL_W), BF16)],
        compiler_params=_cparams("arbitrary"),
        name="ssd",
    )(*args)
    return y.reshape(nb * l, D_SSM), s


def _ret_kernel(nc, ng, has_s0, slot, *refs):
    refs = list(refs)
    u_ref, dl_ref, gn_ref = refs[:3]
    s0_ref = refs[3] if has_s0 else None
    y_ref, s_ref, yf_scr, yb_scr, st_scr, dm_scr, e_scr, ea_scr, bd_scr = refs[-9:]
    q = CHUNK
    hd = RET_HEAD_DIM
    hm = _head_masks(D_RET, RET_HEADS)

    @pl.when(pl.program_id(0) == 0)
    def _():
        log_g = -_softplus(-dl_ref[...])
        ii = lax.broadcasted_iota(jnp.int32, (q, q), 0)
        jj = lax.broadcasted_iota(jnp.int32, (q, q), 1)
        dij = (ii - jj).astype(F32)
        ri = lax.broadcasted_iota(jnp.int32, (q, 1), 0).astype(F32)
        lfs, lbs = [], []
        for h in range(RET_HEADS):
            lf = log_g[:, h:h + 1]
            lb = log_g[:, RET_HEADS + h:RET_HEADS + h + 1]
            lfs.append(lf)
            lbs.append(lb)
            d_f = jnp.exp(jnp.where(dij >= 0, dij * lf, -jnp.inf))
            d_b = jnp.exp(jnp.where(dij <= 0, -dij * lb, -jnp.inf))
            dm_scr[:, h * q:(h + 1) * q] = d_f + d_b
        lf_l = _by_head(hm, lfs)
        lb_l = _by_head(hm, lbs)
        e_scr[0] = jnp.exp((ri + 1.0) * lf_l)
        e_scr[1] = jnp.exp((q - ri) * lb_l)
        e_scr[2] = jnp.exp((q - 1.0 - ri) * lf_l)
        e_scr[3] = jnp.exp(ri * lb_l)
        ea_scr[0:1, :] = jnp.exp(q * lf_l)
        ea_scr[1:2, :] = jnp.exp(q * lb_l)
        bd_scr[...] = _block_diag(D_RET, hd, 1.0)

    class Seq:
        def __init__(self, g):
            self.u, self.y, self.s = u_ref.at[g], y_ref.at[g], _slot_view(s_ref, g, slot)
            self.s0 = s0_ref.at[g] if has_s0 else None
            self.yf, self.yb, self.st = yf_scr.at[g], yb_scr.at[g], st_scr.at[g]

    seqs = [Seq(g) for g in range(ng)]

    for sq in seqs:
        for d in range(2):
            for h in range(RET_HEADS):
                blk = slice(h * hd, (h + 1) * hd)
                if has_s0:
                    parts = [jnp.zeros((hd, hd), F32)] * RET_HEADS
                    parts[h] = sq.s0[d, h]
                    sq.st[d, blk, :] = jnp.concatenate(parts, axis=1)
                else:
                    sq.st[d, blk, :] = jnp.zeros((hd, D_RET), F32)

    def qkv(sq, rows):
        return (sq.u[rows, 0:D_RET], sq.u[rows, D_RET:2 * D_RET].astype(F32) * (RET_HEAD_DIM ** -0.5),
                sq.u[rows, 2 * D_RET:3 * D_RET])

    def state_step(sq, d, qq, kk, vv):
        s_old = sq.st[d]
        y = e_scr[d] * _bdot(qq, s_old)
        sq.st[d] = ea_scr[d:d + 1, :] * s_old + _bdot_tn(kk * e_scr[2 + d], vv) * bd_scr[...]
        return y

    def scan(i, carry):
        rows = _rows(i)
        rows_b = _rows(nc - 1 - i)
        fwd = [qkv(sq, rows) for sq in seqs]
        bwd = [qkv(sq, rows_b) for sq in seqs]
        sc = [_bdot_nt(qq, _stack_heads(kk, hm)) * dm_scr[...] for qq, kk, _ in fwd]
        intra = [_bdot(sc[g], _stack_heads(fwd[g][2], hm)) for g in range(ng)]
        inter = [state_step(sq, 0, *fwd[g]) for g, sq in enumerate(seqs)]
        back = [state_step(sq, 1, *bwd[g]) for g, sq in enumerate(seqs)]
        for g, sq in enumerate(seqs):
            sq.yf[rows, :] = intra[g] + inter[g]
            sq.yb[rows_b, :] = back[g]
        return carry

    lax.fori_loop(0, nc, scan, 0)

    for sq in seqs:
        for d in range(2):
            for h in range(RET_HEADS):
                sq.s[d, h] = sq.st[d, h * hd:(h + 1) * hd, h * hd:(h + 1) * hd]

    gn = gn_ref[...]

    def fin(c, carry):
        rows = _rows(c)
        for sq in seqs:
            y = sq.yf[rows, :] + sq.yb[rows, :]
            cen = y - _seg_mean(y, hd)
            var = _seg_mean(cen * cen, hd)
            sq.y[rows, :] = cen * lax.rsqrt(var + NORM_EPS) * gn * _silu(sq.u[rows, 3 * D_RET:4 * D_RET].astype(F32))
        return carry

    lax.fori_loop(0, nc, fin, 0)


def _retention(u, wts, li, s0, nb, l, seq0, stacked=False, prev=None):
    nc = l // CHUNK
    has_s0 = s0 is not None
    names = ("ret_decay_logit", "ret_gn_w")
    ng = min(RET_GROUP, nb)
    args = [u.reshape(-1, l, RET_IN)] + [wts[n] for n in names]
    in_specs = [_seq_spec(l, RET_IN, ng, seq0)] + [_layer_const(wts[n], li) for n in names]
    if has_s0:
        args.append(s0)
        in_specs.append(pl.BlockSpec((ng, None, 2, RET_HEADS, RET_HEAD_DIM, RET_HEAD_DIM),
                                     lambda b: (b, li, 0, 0, 0, 0)))
    whole = stacked and prev is None
    st_spec, st_shape = _slot_out((2, RET_HEADS, RET_HEAD_DIM, RET_HEAD_DIM), nb, li, stacked, ng, whole)
    aliases = _carry_prev(args, in_specs, [] if prev is None else [prev], 1)
    y, s = pl.pallas_call(
        functools.partial(_ret_kernel, nc, ng, has_s0, li if whole else None),
        grid=(nb // ng,),
        in_specs=in_specs,
        out_specs=[_seq_spec(l, D_RET, ng), st_spec],
        out_shape=[jax.ShapeDtypeStruct((nb, l, D_RET), F32), st_shape],
        input_output_aliases=aliases,
        scratch_shapes=[pltpu.VMEM((ng, l, D_RET), F32), pltpu.VMEM((ng, l, D_RET), F32),
                        pltpu.VMEM((ng, 2, D_RET, D_RET), F32), pltpu.VMEM((CHUNK, RET_HEADS * CHUNK), F32),
                        pltpu.VMEM((4, CHUNK, D_RET), F32), pltpu.VMEM((8, D_RET), F32),
                        pltpu.VMEM((D_RET, D_RET), F32)],
        compiler_params=_cparams("arbitrary"),
        name="retention",
    )(*args)
    return y.reshape(nb * l, D_RET), s


def _split(x):
    hi = x.astype(BF16)
    return hi, (x - hi.astype(F32)).astype(BF16)


def _dot3(a_hi, a_lo, b_hi, b_lo):
    d = lambda p, q: jnp.dot(p, q, preferred_element_type=F32)
    return d(a_hi, b_hi) + (d(a_lo, b_hi) + d(a_hi, b_lo))


@functools.lru_cache(maxsize=None)
def _dft_fwd_host(l):
    n = 2 * l
    f = np.arange(l, dtype=np.int64)[:, None]
    s = np.arange(l, dtype=np.int64)[None, :]
    ang = ((f * s) % n).astype(np.float64) * (2.0 * math.pi / n)
    im = -np.sin(ang)
    im[0] = np.where(np.arange(l) % 2 == 0, 1.0, -1.0)
    return np.concatenate([np.cos(ang), im], axis=0).astype(np.float32)


def _dft_tables(l):
    n = 2 * l
    fwd = jnp.asarray(_dft_fwd_host(l))
    wgt = np.full((n, 1), 2.0 / n, np.float32)
    wgt[0] = wgt[l] = 1.0 / n
    return _split(fwd) + _split((fwd * wgt).T)


def _dft_block_tables(tables, l):
    fwd_hi, fwd_lo, inv_hi, inv_lo = tables
    r = min(DFT_BLOCK, l)
    nblk = l // r
    fb = lambda t: (t[0:l].reshape(nblk, r, l), t[l:2 * l].reshape(nblk, r, l))
    fwd_blk = jnp.concatenate(fb(fwd_hi) + fb(fwd_lo), axis=1)
    inv_blk = jnp.concatenate([inv_hi.reshape(nblk, r, 2 * l), inv_lo.reshape(nblk, r, 2 * l)], axis=1)
    return fwd_blk, inv_blk


def _hy_filter_kernel(l, feats_ref, dec_ref, w1_ref, b1_ref, w2_ref, b2_ref, w3_ref, fr_ref, fh_ref, fl_ref,
                      a_ref, b_ref, d_ref):
    fr = fr_ref[...]
    xdot = lambda a, b: _dot3(*_split(a), *_split(b))
    h = jnp.sin(fr * (xdot(feats_ref[...], w1_ref[...]) + b1_ref[...]))
    h = jnp.sin(fr * (xdot(h, w2_ref[...]) + b2_ref[...]))
    h = xdot(h, w3_ref[...])
    dec = jnp.concatenate([dec_ref[...]] * HY_ORDER, axis=-1)
    row0 = lax.broadcasted_iota(jnp.int32, (l, 1), 0) == 0
    hf = h[:, 0:HY_ORDER * D_HY] * dec
    hb = h[:, HY_ORDER * D_HY:2 * HY_ORDER * D_HY] * dec
    hb = jnp.where(row0, 0.0, hb)
    hs = _split(hf + hb)
    hd = _split(hf - hb)
    re = _dot3(fh_ref[0:l, :], fl_ref[0:l, :], *hs)
    ny = _dot3(fh_ref[l:l + 8, :], fl_ref[l:l + 8, :], *hs)[0:1]
    im = _dot3(fh_ref[l:2 * l, :], fl_ref[l:2 * l, :], *hd)
    for o in range(HY_ORDER):
        cols = slice(o * D_HY, (o + 1) * D_HY)
        a_ref[o] = re[:, cols]
        b_ref[o] = jnp.where(row0, 0.0, im[:, cols])
        d_ref[o] = jnp.where(row0, ny[:, cols], re[:, cols])


def _hy_filter(l, fwd_hi, fwd_lo, wts, li):
    pos = np.arange(l, dtype=np.float32)
    t = pos / np.float32(l - 1)
    bands = np.linspace(1e-4, HY_BANDS - 1, HY_BANDS, dtype=np.float32)
    ang = np.float32(2.0 * math.pi / l) * pos[:, None] * bands[None, :]
    feats = np.concatenate([t[:, None], np.cos(ang), -np.sin(ang)], axis=-1).astype(np.float32)
    feats = np.pad(feats, ((0, 0), (0, LANES - HY_EMB)))
    max_decay = math.log(HY_TARGET) / HY_FAST_DECAY
    min_decay = math.log(HY_TARGET) / HY_SLOW_DECAY
    deltas = np.abs(np.linspace(min_decay, max_decay, D_HY, dtype=np.float32))
    dec = np.exp(-t[:, None] * deltas[None, :]).astype(np.float32)
    spec = jax.ShapeDtypeStruct((HY_ORDER, l, D_HY), F32)
    names = ("hy_w1", "hy_b1", "hy_w2", "hy_b2", "hy_w3", "hy_freq")
    return pl.pallas_call(
        functools.partial(_hy_filter_kernel, l),
        grid=(1,),
        in_specs=[_full(feats.shape), _full(dec.shape)] + [_layer_const(wts[n], li) for n in names]
                 + [_full(fwd_hi.shape), _full(fwd_lo.shape)],
        out_specs=[_full(spec.shape)] * 3,
        out_shape=[spec] * 3,
        compiler_params=_cparams("arbitrary"),
        name="hyena_filter",
    )(jnp.asarray(feats), jnp.asarray(dec), *[wts[n] for n in names], fwd_hi, fwd_lo)


def _hy_kernel(nc, ng, u_ref, cw_ref, cb_ref, f_ref, g_ref, a_ref, b_ref, d_ref, bias_ref, y_ref,
               uc_scr, vh_scr, vl_scr, sh_scr, sl_scr, z_scr):
    l = nc * CHUNK
    r = min(DFT_BLOCK, l)
    nblk = l // r
    cw = cw_ref[...]
    cb = cb_ref[...]
    dot = lambda p, q: jnp.dot(p, q, preferred_element_type=F32)
    wide = lambda x: jnp.concatenate([x] * ng, axis=1)
    seq_cols = lambda g: slice(g * D_HY, (g + 1) * D_HY)

    def for_blocks(body):
        if nblk == 1:
            body(0)
        else:
            lax.fori_loop(0, nblk, lambda i, carry: (body(i), carry)[1], 0)

    def conv(c, carry):
        rows = _rows(c)
        for g in range(ng):
            uc = _conv3_chunk(u_ref.at[g], c, nc, cw, cb)
            uc_scr[g, rows, :] = uc
            vh_scr[rows, seq_cols(g)], vl_scr[rows, seq_cols(g)] = _split(uc[:, 0:D_HY])
        return carry

    lax.fori_loop(0, nc, conv, 0)

    def long_conv(o):
        def spectrum(i):
            rows = pl.ds(pl.multiple_of(i * r, r), r)
            rows_im = pl.ds(pl.multiple_of(l + i * r, r), r)
            p = dot(f_ref[i], vh_scr[...])
            pl_ = dot(f_ref[i, 0:2 * r, :], vl_scr[...])
            zr = p[0:r] + (p[2 * r:3 * r] + pl_[0:r])
            zi = p[r:2 * r] + (p[3 * r:4 * r] + pl_[r:2 * r])
            fa, fb, fd = wide(a_ref[o, rows, :]), wide(b_ref[o, rows, :]), wide(d_ref[o, rows, :])
            sh_scr[rows, :], sl_scr[rows, :] = _split(zr * fa - zi * fb)
            sh_scr[rows_im, :], sl_scr[rows_im, :] = _split(zr * fb + zi * fd)

        for_blocks(spectrum)

        def inverse(i):
            rows = pl.ds(pl.multiple_of(i * r, r), r)
            p = dot(g_ref[i], sh_scr[...])
            y = p[0:r] + (p[r:2 * r] + dot(g_ref[i, 0:r, :], sl_scr[...]))
            gate = jnp.concatenate([uc_scr[g, rows, (o + 1) * D_HY:(o + 2) * D_HY] for g in range(ng)], axis=1)
            if o == 0:
                v = jnp.concatenate([uc_scr[g, rows, 0:D_HY] for g in range(ng)], axis=1)
            else:
                v = z_scr[rows, :]
            out = gate * (y + v * wide(bias_ref[o:o + 1, :]))
            if o + 1 < HY_ORDER:
                z_scr[rows, :] = out
                vh_scr[rows, :], vl_scr[rows, :] = _split(out)
            else:
                for g in range(ng):
                    y_ref[g, rows, :] = out[:, seq_cols(g)]

        for_blocks(inverse)

    for o in range(HY_ORDER):
        long_conv(o)


def _hyena(u, wts, li, tables, spectra, nb, l, seq0):
    nc = l // CHUNK
    ng = min(HY_GROUP, nb)
    a, b, d = spectra
    names = ("hy_conv_w", "hy_conv_b")
    y = pl.pallas_call(
        functools.partial(_hy_kernel, nc, ng),
        grid=(nb // ng,),
        in_specs=[_seq_spec(l, HY_IN, ng, seq0)] + [_layer_const(wts[n], li) for n in names]
                 + [_const(tables[0].shape), _const(tables[1].shape),
                    _const((HY_ORDER, l, D_HY)), _const((HY_ORDER, l, D_HY)), _const((HY_ORDER, l, D_HY)),
                    _layer_const(wts["hy_bias"], li)],
        out_specs=pl.BlockSpec((ng, l, D_HY), lambda i: (i, 0, 0)),
        out_shape=jax.ShapeDtypeStruct((nb, l, D_HY), F32),
        scratch_shapes=[pltpu.VMEM((ng, l, HY_IN), F32),
                        pltpu.VMEM((l, ng * D_HY), BF16), pltpu.VMEM((l, ng * D_HY), BF16),
                        pltpu.VMEM((2 * l, ng * D_HY), BF16), pltpu.VMEM((2 * l, ng * D_HY), BF16),
                        pltpu.VMEM((l, ng * D_HY), F32)],
        compiler_params=_cparams("parallel"),
        name="hyena",
    )(u.reshape(-1, l, HY_IN), *[wts[n] for n in names], *tables, a, b, d, wts["hy_bias"])
    return y.reshape(nb * l, D_HY)


def _seg_rms(x, w):
    return x * lax.rsqrt(_seg_mean(x * x, HEAD_DIM) + NORM_EPS) * w


def _stack_q(q):
    lo = lax.broadcasted_iota(jnp.int32, (1, D_KV), 1) < HEAD_DIM
    qa = q[:, 0:D_KV]
    qb = q[:, D_KV:2 * D_KV]
    return jnp.concatenate([jnp.where(lo, qa, 0.0), jnp.where(lo, pltpu.roll(qa, HEAD_DIM, 1), 0.0),
                            jnp.where(lo, 0.0, pltpu.roll(qb, HEAD_DIM, 1)), jnp.where(lo, 0.0, qb)], axis=0)


def _unstack_o(o):
    r = o.shape[0] // ATT_HEADS
    lo = lax.broadcasted_iota(jnp.int32, (1, D_KV), 1) < HEAD_DIM
    ya = jnp.where(lo, o[0:r], pltpu.roll(o[r:2 * r], HEAD_DIM, 1))
    yb = jnp.where(lo, pltpu.roll(o[2 * r:3 * r], HEAD_DIM, 1), o[3 * r:4 * r])
    return jnp.concatenate([ya, yb], axis=1)


def _sink_col(sink_ref, li, r):
    rb = lax.broadcasted_iota(jnp.int32, (ATT_HEADS * r, 1), 0) // r
    col = jnp.full((ATT_HEADS * r, 1), sink_ref[li * ATT_HEADS + ATT_HEADS - 1], F32)
    for h in range(ATT_HEADS - 2, -1, -1):
        col = jnp.where(rb == h, sink_ref[li * ATT_HEADS + h], col)
    return col


def _ctx_attn_kernel(li, ng, slot, u_ref, qn_ref, kn_ref, sink_ref, *rest):
    y_ref, k_ref, v_ref = rest[-3:]
    l = u_ref.shape[1]
    sink = _sink_col(sink_ref, li, l)
    seqs = range(ng)
    u = [u_ref[g].astype(F32) for g in seqs]
    q = [_seg_rms(u[g][:, 0:D_ATT], qn_ref[...]) for g in seqs]
    k = [_seg_rms(u[g][:, D_ATT:D_ATT + D_KV], kn_ref[:, 0:D_KV]) for g in seqs]
    v = [u[g][:, D_ATT + D_KV:D_ATT + 2 * D_KV] for g in seqs]
    for g in seqs:
        _slot_view(k_ref, g, slot)[...] = k[g]
        _slot_view(v_ref, g, slot)[...] = v[g]
    s = [_bdot_nt(_stack_q(q[g]), k[g]) * (HEAD_DIM ** -0.5) for g in seqs]
    m = [jnp.maximum(jnp.max(s[g], axis=-1, keepdims=True), sink) for g in seqs]
    p = [jnp.exp(s[g] - m[g]) for g in seqs]
    den = [jnp.sum(p[g], axis=-1, keepdims=True) + jnp.exp(sink - m[g]) for g in seqs]
    o = [_bdot(p[g] * (1.0 / den[g]), v[g]) for g in seqs]
    for g in seqs:
        y_ref[g] = _unstack_o(o[g])


def _ctx_attention(u, wts, li, nb, l, seq0, prev=None):
    ng = min(ATT_GROUP, nb)
    whole = prev is None
    kv_spec, kv_shape = _slot_out((l, D_KV), nb, li, True, ng, whole)
    args = [u.reshape(-1, l, ATT_IN), wts["attn_q_norm"], wts["attn_k_norm"], wts["attn_sink"]]
    in_specs = [_seq_spec(l, ATT_IN, ng, seq0), _layer_const(wts["attn_q_norm"], li),
                _layer_const(wts["attn_k_norm"], li), pl.BlockSpec(memory_space=pltpu.SMEM)]
    aliases = _carry_prev(args, in_specs, [] if prev is None else list(prev), 1)
    y, k, v = pl.pallas_call(
        functools.partial(_ctx_attn_kernel, li, ng, li if whole else None),
        grid=(nb // ng,),
        in_specs=in_specs,
        out_specs=[_seq_spec(l, D_ATT, ng), kv_spec, kv_spec],
        out_shape=[jax.ShapeDtypeStruct((nb, l, D_ATT), F32), kv_shape, kv_shape],
        input_output_aliases=aliases,
        compiler_params=_cparams("parallel"),
        name="ctx_attention",
    )(*args)
    return y.reshape(nb * l, D_ATT), k, v


@functools.lru_cache(maxsize=None)
def _rope_tables_host(l):
    n_rows = l // GRID_W
    rows = np.repeat(np.arange(n_rows, dtype=np.float32), GRID_W)
    cols = np.tile(np.arange(GRID_W, dtype=np.float32), n_rows)
    nf = HEAD_DIM // 4
    inv = (np.float32(ROPE_BASE) ** (-np.arange(nf, dtype=np.float32) / np.float32(nf))).astype(np.float32)
    ar = rows[:, None] * inv[None, :]
    ac = cols[:, None] * inv[None, :]
    cos = np.concatenate([np.cos(ar), np.cos(ar), np.cos(ac), np.cos(ac)], axis=-1)
    sin = np.concatenate([-np.sin(ar), np.sin(ar), -np.sin(ac), np.sin(ac)], axis=-1)
    return (np.tile(cos, (1, ATT_HEADS)).astype(np.float32), np.tile(sin, (1, ATT_HEADS)).astype(np.float32))


def _rope(x, cos, sin):
    w = x.shape[-1]
    nf = HEAD_DIM // 4
    lane = lax.broadcasted_iota(jnp.int32, x.shape, 1)
    first = (lane % (2 * nf)) < nf
    partner = jnp.where(first, pltpu.roll(x, w - nf, 1), pltpu.roll(x, nf, 1))
    return x * cos + partner * sin


def _lat_attn_kernel(nblk, li, ng, u_ref, qn_ref, kn_ref, cos_ref, sin_ref, ck_ref, cv_ref, sink_ref,
                     y_ref, q_scr, k_scr, v_scr):
    blk = ATT_BLOCK
    l = nblk * blk
    seqs = range(ng)
    cos_q = cos_ref[...]
    sin_q = sin_ref[...]
    zeros = jnp.zeros((blk, D_KV), F32)
    for g in seqs:
        u = u_ref[g].astype(F32)
        q = _seg_rms(u[:, 0:D_ATT], qn_ref[...])
        k = _seg_rms(u[:, D_ATT:D_ATT + D_KV], kn_ref[:, 0:D_KV])
        q_scr[g] = _rope(q, cos_q, sin_q)
        k_scr[g, 0:blk, :] = zeros
        k_scr[g, blk + l:2 * blk + l, :] = zeros
        v_scr[g, 0:blk, :] = zeros
        v_scr[g, blk + l:2 * blk + l, :] = zeros
        k_scr[g, blk:blk + l, :] = _rope(k, cos_q[:, 0:D_KV], sin_q[:, 0:D_KV])
        v_scr[g, blk:blk + l, :] = u[:, D_ATT + D_KV:D_ATT + 2 * D_KV]

    scale = HEAD_DIM ** -0.5
    r = lax.broadcasted_iota(jnp.int32, (blk, 3 * blk), 0)
    cidx = lax.broadcasted_iota(jnp.int32, (blk, 3 * blk), 1)
    band = (cidx - r >= blk - WINDOW) & (cidx - r <= blk + WINDOW)
    sink = _sink_col(sink_ref, li, blk)

    def block(i, carry):
        rows = _rows(i)
        win = pl.ds(pl.multiple_of(i * blk, blk), 3 * blk)
        kpos = cidx + (i - 1) * blk
        valid = band & (kpos >= 0) & (kpos < l)
        valid = jnp.concatenate([valid] * ATT_HEADS, axis=0)
        qs = [_stack_q(q_scr[g, rows, :]) for g in seqs]
        s_loc = [jnp.where(valid, _bdot_nt(qs[g], k_scr[g, win, :]) * scale, -jnp.inf) for g in seqs]
        s_ctx = [_bdot_nt(qs[g], ck_ref[g]) * scale for g in seqs]
        m = [jnp.maximum(jnp.maximum(jnp.max(s_loc[g], axis=-1, keepdims=True),
                                     jnp.max(s_ctx[g], axis=-1, keepdims=True)), sink) for g in seqs]
        p_loc = [jnp.exp(s_loc[g] - m[g]) for g in seqs]
        p_ctx = [jnp.exp(s_ctx[g] - m[g]) for g in seqs]
        inv = [1.0 / (jnp.sum(p_loc[g], axis=-1, keepdims=True) + jnp.sum(p_ctx[g], axis=-1, keepdims=True)
                      + jnp.exp(sink - m[g])) for g in seqs]
        o = [_bdot(p_ctx[g] * inv[g], cv_ref[g]) + _bdot(p_loc[g] * inv[g], v_scr[g, win, :]) for g in seqs]
        for g in seqs:
            y_ref[g, rows, :] = _unstack_o(o[g])
        return carry

    lax.fori_loop(0, nblk, block, 0)


def _lat_attention(u, wts, li, ck, cv, nb, l, seq0):
    lc = ck.shape[2]
    nblk = l // ATT_BLOCK
    ng = min(ATT_GROUP, nb)
    cos, sin = _rope_tables_host(l)
    cache_spec = pl.BlockSpec((ng, None, lc, D_KV), lambda b: (b, li, 0, 0))
    y = pl.pallas_call(
        functools.partial(_lat_attn_kernel, nblk, li, ng),
        grid=(nb // ng,),
        in_specs=[_seq_spec(l, ATT_IN, ng, seq0), _layer_const(wts["attn_q_norm"], li),
                  _layer_const(wts["attn_k_norm"], li), _const((l, D_ATT)), _const((l, D_ATT)),
                  cache_spec, cache_spec, pl.BlockSpec(memory_space=pltpu.SMEM)],
        out_specs=_seq_spec(l, D_ATT, ng),
        out_shape=jax.ShapeDtypeStruct((nb, l, D_ATT), F32),
        scratch_shapes=[pltpu.VMEM((ng, l, D_ATT), F32), pltpu.VMEM((ng, l + 2 * ATT_BLOCK, D_KV), F32),
                        pltpu.VMEM((ng, l + 2 * ATT_BLOCK, D_KV), F32)],
        compiler_params=_cparams("parallel"),
        name="lat_attention",
    )(u.reshape(-1, l, ATT_IN), wts["attn_q_norm"], wts["attn_k_norm"], jnp.asarray(cos), jnp.asarray(sin),
      ck, cv, wts["attn_sink"])
    return y.reshape(nb * l, D_ATT)


def _pad_lane_tile(w):
    return jnp.pad(w, [(0, 0)] * (w.ndim - 1) + [(0, -w.shape[-1] % LANES)])


def _dt_weight(w):
    zeros = jnp.zeros(w.shape[:-1] + (DT_PAD - DT_REP * N_DT,), w.dtype)
    return jnp.concatenate([w[..., C_DT:C_DT + N_DT]] * DT_REP + [zeros], axis=-1).astype(BF16)


def _prep_weights(p):
    row = lambda a: a.reshape(DEPTH, 1, -1)
    pad_lanes = lambda a: jnp.pad(row(a), ((0, 0), (0, 0), (0, LANES - a[0].size)))
    return dict(
        norm_w=p["norm_w"].reshape(DEPTH * 3, 1, D_MODEL),
        ffn_w_in=p["ffn_w_in"].astype(BF16).reshape(DEPTH * 2, D_MODEL, 2 * D_FF),
        ffn_w_out=p["ffn_w_out"].astype(BF16).reshape(DEPTH * 2, D_FF, D_MODEL),
        mix_w_in=_pad_lane_tile(p["mix_w_in"].astype(BF16)), mix_w_dt=_dt_weight(p["mix_w_in"]),
        mix_w_out=p["mix_w_out"].astype(BF16),
        ssd_conv_w=p["ssd_conv_w"], ssd_conv_b=row(p["ssd_conv_b"]),
        ssd_dt_bias=pad_lanes(jnp.tile(row(p["ssd_dt_bias"]), (1, 1, DT_REP))),
        ssd_a_log=pad_lanes(jnp.tile(row(p["ssd_a_log"]), (1, 1, DT_REP))),
        ssd_d=row(jnp.repeat(p["ssd_d"], SSM_HEAD_DIM, axis=-1)), ssd_norm_w=row(p["ssd_norm_w"]),
        hy_conv_w=p["hy_conv_w"], hy_conv_b=row(p["hy_conv_b"]), hy_bias=p["hy_bias"],
        hy_w1=jnp.pad(p["hy_w1"], ((0, 0), (0, LANES - HY_EMB), (0, 0))), hy_b1=row(p["hy_b1"]),
        hy_w2=p["hy_w2"], hy_b2=row(p["hy_b2"]), hy_w3=p["hy_w3"], hy_freq=row(p["hy_freq"]),
        ret_decay_logit=pad_lanes(p["ret_decay_logit"]), ret_gn_w=row(p["ret_gn_w"]),
        attn_q_norm=row(jnp.tile(p["attn_q_norm"], (1, ATT_HEADS))),
        attn_k_norm=row(jnp.tile(p["attn_k_norm"], (1, ATT_HEADS))),
        attn_sink=p["attn_sink"].reshape(DEPTH * ATT_HEADS),
    )


def _mixers(u, wts, li, nb, l, seq0, ssd_s0, ret_s0, ctx_kv, hy_tables, hy_spectra, carried):
    z, xbc, hy, ret, att, dt = u
    ctx = ctx_kv is None
    y_ssd, s_ssd = _ssd(z, xbc, dt, wts, li, ssd_s0, nb, l, seq0, stacked=ctx, prev=carried.get("ssd"))
    y_hy = _hyena(hy, wts, li, hy_tables, hy_spectra, nb, l, seq0)
    y_ret, s_ret = _retention(ret, wts, li, ret_s0, nb, l, seq0, stacked=ctx, prev=carried.get("ret"))
    if ctx:
        y_att, k, v = _ctx_attention(att, wts, li, nb, l, seq0, prev=carried.get("kv"))
        carried = dict(ssd=s_ssd, ret=s_ret, kv=(k, v))
    else:
        y_att = _lat_attention(att, wts, li, ctx_kv[0], ctx_kv[1], nb, l, seq0)
    return (y_ssd, y_hy, y_ret, y_att), carried


def kernel(x_prompt, x_sample, cache_k, cache_v, state_ssd, state_ret, c, c_ctx, w_mod, b_mod, norm_w, ffn_w_in, ffn_w_out, mix_w_in, mix_w_out, ssd_conv_w, ssd_conv_b, ssd_dt_bias, ssd_a_log, ssd_d, ssd_norm_w, hy_conv_w, hy_conv_b, hy_w1, hy_b1, hy_w2, hy_b2, hy_w3, hy_freq, hy_bias, ret_decay_logit, ret_gn_w, attn_q_norm, attn_k_norm, attn_sink):
    bp, lp_len, _ = x_prompt.shape
    bs, ls_len, _ = x_sample.shape
    lc = cache_k.shape[2]

    cond = jnp.concatenate([c_ctx[None, :], c, jnp.zeros((MOD_ROWS - 1 - bs, D_MODEL), F32)], axis=0)
    mod = _modulation(cond, w_mod, b_mod)
    wts = _prep_weights(dict(
        norm_w=norm_w, ffn_w_in=ffn_w_in, ffn_w_out=ffn_w_out, mix_w_in=mix_w_in, mix_w_out=mix_w_out,
        ssd_conv_w=ssd_conv_w, ssd_conv_b=ssd_conv_b, ssd_dt_bias=ssd_dt_bias, ssd_a_log=ssd_a_log, ssd_d=ssd_d,
        ssd_norm_w=ssd_norm_w, hy_conv_w=hy_conv_w, hy_conv_b=hy_conv_b, hy_w1=hy_w1, hy_b1=hy_b1, hy_w2=hy_w2,
        hy_b2=hy_b2, hy_w3=hy_w3, hy_freq=hy_freq, hy_bias=hy_bias, ret_decay_logit=ret_decay_logit,
        ret_gn_w=ret_gn_w, attn_q_norm=attn_q_norm, attn_k_norm=attn_k_norm, attn_sink=attn_sink))
    ck = cache_k.reshape(bs, DEPTH, lc, D_KV)
    cv = cache_v.reshape(bs, DEPTH, lc, D_KV)

    tab_p = _dft_tables(lp_len)
    tab_s = _dft_tables(ls_len)
    blk_p = _dft_block_tables(tab_p, lp_len)
    blk_s = _dft_block_tables(tab_s, ls_len)

    tm = TOKEN_TILE
    tp, ts = bp * lp_len, bs * ls_len
    assert tp % tm == 0 and ts % tm == 0 and ls_len % tm == 0 and tp % ls_len == 0
    st = _Stream(tm, tp // tm, ts // tm, ls_len)
    x = (x_prompt.reshape(tp, D_MODEL), x_sample.reshape(ts, D_MODEL))
    carried = {}
    for li in range(DEPTH):
        spec_p = _hy_filter(lp_len, tab_p[0], tab_p[1], wts, li)
        spec_s = _hy_filter(ls_len, tab_s[0], tab_s[1], wts, li)
        x = _ffn(st, x, mod, wts, li, 0)
        u = _inproj(st, x, mod, wts, li)
        y_p, carried = _mixers(u, wts, li, bp, lp_len, 0, None, None, None, blk_p, spec_p, carried)
        y_s, _ = _mixers(u, wts, li, bs, ls_len, tp // ls_len, state_ssd, state_ret, (ck, cv), blk_s, spec_s, {})
        if li + 1 < DEPTH:
            x = _ffn(st, x, mod, wts, li, 1, mix=tuple(zip(y_p, y_s)))
        else:
            yp = _ffn(st.part(0), x, mod, wts, li, 1, mix=y_p)
            ys = _ffn(st.part(1), x, mod, wts, li, 1, mix=y_s)

    kv_shape = (bp, DEPTH, lp_len, ATT_KV_HEADS, HEAD_DIM)
    new_k, new_v = carried["kv"]
    return (yp.reshape(bp, lp_len, D_MODEL), ys.reshape(bs, ls_len, D_MODEL),
            new_k.reshape(kv_shape), new_v.reshape(kv_shape), carried["ssd"], carried["ret"])
```

```python
import functools
import math

import numpy as np
import jax
import jax.numpy as jnp
from jax import lax
from jax.experimental import pallas as pl
from jax.experimental.pallas import tpu as pltpu

F32 = jnp.float32
BF16 = jnp.bfloat16

D_MODEL = 1024
DEPTH = 2
GRID_W = 64
D_FF = 2816
N_MOD = 9
NORM_EPS = 1e-6
CHUNK = 128
D_SSM = 256
SSM_HEADS = 4
SSM_HEAD_DIM = 64
SSM_STATE = 128
SSM_GROUPS = 2
SSM_CONV_CH = D_SSM + 2 * SSM_GROUPS * SSM_STATE
D_HY = 256
HY_ORDER = 2
HY_BANDS = 16
HY_EMB = 1 + 2 * HY_BANDS
HY_HIDDEN = 64
HY_FAST_DECAY = 0.3
HY_SLOW_DECAY = 1.5
HY_TARGET = 1e-2
HY_IN = (HY_ORDER + 1) * D_HY
D_RET = 256
RET_HEADS = 4
RET_HEAD_DIM = 64
RET_IN = 4 * D_RET
ATT_HEADS = 4
ATT_KV_HEADS = 2
HEAD_DIM = 64
D_ATT = ATT_HEADS * HEAD_DIM
D_KV = ATT_KV_HEADS * HEAD_DIM
ATT_IN = D_ATT + 2 * D_KV
WINDOW = 128
ATT_BLOCK = 128
ROPE_BASE = 10000.0
D_MIX = D_SSM + D_HY + D_RET + D_ATT

LANES = 128
DT_PAD = LANES
VMEM_LIMIT = 56 * 1024 * 1024
MOD_ROWS = 8

TOKEN_TILE = 512
DFT_BLOCK = 512
HY_GROUP = 4
SEQ_GROUP = 4
RET_GROUP = 4
ATT_GROUP = 4


def _cparams(*sem):
    return pltpu.CompilerParams(dimension_semantics=sem, vmem_limit_bytes=VMEM_LIMIT)


def _rms(x, w):
    return x * lax.rsqrt(jnp.mean(x * x, axis=-1, keepdims=True) + NORM_EPS) * w


def _silu(x):
    return x * (1.0 / (1.0 + jnp.exp(-x)))


def _softplus(x):
    return jnp.maximum(x, 0.0) + jnp.log1p(jnp.exp(-jnp.abs(x)))


def _bdot(a, b):
    return jnp.dot(a.astype(BF16), b.astype(BF16), preferred_element_type=F32)


def _bdot_nt(a, b):
    return lax.dot_general(a.astype(BF16), b.astype(BF16), (((1,), (1,)), ((), ())),
                           preferred_element_type=F32)


def _bdot_tn(a, b):
    return lax.dot_general(a.astype(BF16), b.astype(BF16), (((0,), (0,)), ((), ())),
                           preferred_element_type=F32)


def _full(shape):
    n = len(shape)
    return pl.BlockSpec(shape, lambda *_: (0,) * n)


def _const(shape):
    n = len(shape)
    return pl.BlockSpec(shape, lambda *_: (0,) * n, pipeline_mode=pl.Buffered(1))


def _layer_const(arr, li):
    tail = arr.shape[1:]
    zeros = (0,) * len(tail)
    return pl.BlockSpec((None,) + tail, lambda *_: (li,) + zeros, pipeline_mode=pl.Buffered(1))


class _Stream:
    def __init__(self, tm, nct, nlt, l_lat, first=0, count=None):
        self.tm, self.nct, self.nlt, self.l_lat = tm, nct, nlt, l_lat
        self.first = first
        self.count = nct + nlt if count is None else count

    def part(self, path):
        first, count = (0, self.nct) if path == 0 else (self.nct, self.nlt)
        return _Stream(self.tm, self.nct, self.nlt, self.l_lat, first, count)

    @property
    def tokens(self):
        return self.count * self.tm

    def merged(self, width):
        return pl.BlockSpec((self.tm, width), lambda i: (i + self.first, 0))

    def owned(self, width):
        return pl.BlockSpec((self.tm, width), lambda i: (i, 0))

    def pair(self, width):
        ctx = pl.BlockSpec((self.tm, width), lambda i: (jnp.minimum(i + self.first, self.nct - 1), 0))
        lat = pl.BlockSpec((self.tm, width), lambda i: (jnp.maximum(i + self.first - self.nct, 0), 0))
        return [ctx, lat]

    def mod(self, k, row_ctx, row_lat):
        def index(i):
            t = i + self.first
            lat_row = row_lat + (jnp.maximum(t - self.nct, 0) * self.tm) // self.l_lat
            return (jnp.where(t < self.nct, row_ctx, lat_row), 0, k)
        return pl.BlockSpec((None, 1, D_MODEL), index)


def _seq_spec(l, w, ng=None, seq0=0):
    step0 = seq0 // (ng or 1)
    return pl.BlockSpec((ng, l, w), lambda b: (b + step0, 0, 0))


def _head_masks(width, heads):
    lane = lax.broadcasted_iota(jnp.int32, (1, width), 1)
    hd = width // heads
    return [(lane >= h * hd) & (lane < (h + 1) * hd) for h in range(heads)]


def _by_head(masks, vals):
    out = vals[-1]
    for m, v in zip(masks[-2::-1], vals[-2::-1]):
        out = jnp.where(m, v, out)
    return out


def _block_diag(n, blk, value):
    i = lax.broadcasted_iota(jnp.int32, (n, n), 0) // blk
    j = lax.broadcasted_iota(jnp.int32, (n, n), 1) // blk
    return jnp.where(i == j, value, 0.0).astype(F32)


def _mod_kernel(c_ref, w_ref, b_ref, o_ref):
    c = c_ref[...]
    o_ref[...] = _bdot(_silu(c), w_ref[...]) + b_ref[...]


def _modulation(cond, w_mod, b_mod):
    out = pl.pallas_call(
        _mod_kernel,
        grid=(DEPTH, N_MOD),
        in_specs=[pl.BlockSpec((MOD_ROWS, D_MODEL), lambda l, j: (0, 0)),
                  pl.BlockSpec((None, D_MODEL, D_MODEL), lambda l, j: (l, 0, j)),
                  pl.BlockSpec((None, 1, D_MODEL), lambda l, j: (l, 0, j))],
        out_specs=pl.BlockSpec((None, MOD_ROWS, D_MODEL), lambda l, j: (l, 0, j)),
        out_shape=jax.ShapeDtypeStruct((DEPTH, MOD_ROWS, N_MOD * D_MODEL), F32),
        compiler_params=_cparams("arbitrary", "arbitrary"),
        name="modulation",
    )(cond, w_mod, b_mod.reshape(DEPTH, 1, N_MOD * D_MODEL))
    return out.reshape(DEPTH * MOD_ROWS, 1, N_MOD * D_MODEL)


def _ffn_kernel(n_mix, pair_x, pair_mix, nct, first, *refs):
    refs = list(refs)
    o_ref = refs.pop()
    is_ctx = pl.program_id(0) + first < nct

    def read(pair):
        if pair:
            a, b = refs.pop(0), refs.pop(0)
            return jnp.where(is_ctx, a[...], b[...])
        return refs.pop(0)[...]

    x = read(pair_x)
    sh_ref, sc_ref, g_ref, nw_ref, wi_ref, wo_ref = (refs.pop(0) for _ in range(6))
    if n_mix:
        gm_ref = refs.pop(0)
        ys = [read(pair_mix) for _ in range(n_mix)]
        wm_ref = refs.pop(0)
        w = D_MIX // n_mix
        acc = _bdot(ys[0], wm_ref[0:w, :])
        for j in range(1, n_mix):
            acc += _bdot(ys[j], wm_ref[j * w:(j + 1) * w, :])
        x = x + gm_ref[...] * acc
    h = (_rms(x, nw_ref[...] * (1.0 + sc_ref[...])) + sh_ref[...]).astype(BF16)
    gate = jnp.dot(h, wi_ref[:, 0:D_FF], preferred_element_type=F32)
    up = jnp.dot(h, wi_ref[:, D_FF:2 * D_FF], preferred_element_type=F32)
    o_ref[...] = x + 0.5 * g_ref[...] * _bdot(_silu(gate) * up, wo_ref[...])


def _ffn(st, x, mod, wts, li, k, mix=()):
    row_ctx, row_lat = li * MOD_ROWS, li * MOD_ROWS + 1
    w_in, w_out = wts["ffn_w_in"], wts["ffn_w_out"]
    pair_x = isinstance(x, tuple)
    pair_mix = bool(mix) and isinstance(mix[0], tuple)
    args = list(x) if pair_x else [x]
    in_specs = st.pair(D_MODEL) if pair_x else [st.merged(D_MODEL)]
    args += [mod, mod, mod, wts["norm_w"], w_in, w_out]
    in_specs += [st.mod(6 * k + j, row_ctx, row_lat) for j in range(3)]
    in_specs += [_layer_const(wts["norm_w"], 3 * li + 2 * k), _layer_const(w_in, 2 * li + k),
                 _layer_const(w_out, 2 * li + k)]
    if mix:
        args.append(mod)
        in_specs.append(st.mod(5, row_ctx, row_lat))
        for y in mix:
            width = (y[0] if pair_mix else y).shape[1]
            args += list(y) if pair_mix else [y]
            in_specs += st.pair(width) if pair_mix else [st.owned(width)]
        args.append(wts["mix_w_out"])
        in_specs.append(_layer_const(wts["mix_w_out"], li))
    return pl.pallas_call(
        functools.partial(_ffn_kernel, len(mix), pair_x, pair_mix, st.nct, st.first),
        grid=(st.count,),
        in_specs=in_specs,
        out_specs=st.owned(D_MODEL),
        out_shape=jax.ShapeDtypeStruct((st.tokens, D_MODEL), F32),
        compiler_params=_cparams("parallel"),
        name="ffn",
    )(*args)


_IN_SPLITS = (("z", D_SSM, BF16), ("xbc", SSM_CONV_CH, F32), ("hy", HY_IN, F32), ("ret", RET_IN, BF16),
              ("att", ATT_IN, BF16), ("dt", DT_PAD, F32))


C_DT = D_SSM + SSM_CONV_CH
N_DT = 2 * SSM_HEADS
W_TAIL = HY_IN + RET_IN + ATT_IN


def _inproj_kernel(x_ref, sh_ref, sc_ref, nw_ref, w_ref, wdt_ref, *rest):
    o_refs, wt_scr = rest[:-1], rest[-1]

    @pl.when(pl.program_id(0) == 0)
    def _():
        wt_scr[...] = w_ref[:, C_DT + N_DT:C_DT + N_DT + W_TAIL]

    h = (_rms(x_ref[...], nw_ref[...] * (1.0 + sc_ref[...])) + sh_ref[...]).astype(BF16)
    off = 0
    for (name, width, dtype), o_ref in zip(_IN_SPLITS, o_refs):
        if name == "dt":
            w = wdt_ref[...]
        elif off < C_DT:
            w = w_ref[:, off:off + width]
        else:
            w = wt_scr[:, off - C_DT:off - C_DT + width]
        o_ref[...] = jnp.dot(h, w, preferred_element_type=F32).astype(dtype)
        off += width


def _inproj(st, x, mod, wts, li):
    row_ctx, row_lat = li * MOD_ROWS, li * MOD_ROWS + 1
    return pl.pallas_call(
        _inproj_kernel,
        grid=(st.count,),
        in_specs=[st.merged(D_MODEL), st.mod(3, row_ctx, row_lat), st.mod(4, row_ctx, row_lat),
                  _layer_const(wts["norm_w"], 3 * li + 1),
                  _layer_const(wts["mix_w_in"], li), _layer_const(wts["mix_w_dt"], li)],
        out_specs=[st.owned(width) for _, width, _ in _IN_SPLITS],
        out_shape=[jax.ShapeDtypeStruct((st.tokens, width), dtype) for _, width, dtype in _IN_SPLITS],
        scratch_shapes=[pltpu.VMEM((D_MODEL, W_TAIL), BF16)],
        compiler_params=_cparams("arbitrary"),
        name="mix_in",
    )(x, mod, mod, wts["norm_w"], wts["mix_w_in"], wts["mix_w_dt"])


def _conv3_chunk(x_ref, c, nc, w, b):
    q = CHUNK
    l = nc * q
    r0 = pl.multiple_of(c * q, q)
    x = x_ref[pl.ds(r0, q), :]
    prev = x_ref[pl.ds(jnp.maximum(r0 - 1, 0), 1), :]
    nxt = x_ref[pl.ds(jnp.minimum(r0 + q, l - 1), 1), :]
    prev = jnp.where(c > 0, prev, 0.0)
    nxt = jnp.where(c < nc - 1, nxt, 0.0)
    rid = lax.broadcasted_iota(jnp.int32, (q, 1), 0)
    xm1 = jnp.where(rid == 0, prev, pltpu.roll(x, 1, 0))
    xp1 = jnp.where(rid == q - 1, nxt, pltpu.roll(x, q - 1, 0))
    return xm1 * w[0:1, :] + x * w[1:2, :] + xp1 * w[2:3, :] + b


def _tri(lower):
    i = lax.broadcasted_iota(jnp.int32, (CHUNK, CHUNK), 0)
    j = lax.broadcasted_iota(jnp.int32, (CHUNK, CHUNK), 1)
    return (j <= i) if lower else (j >= i)


def _rows(c):
    return pl.ds(pl.multiple_of(c * CHUNK, CHUNK), CHUNK)


def _split_cat(v, parts, axis):
    out, r = [], v
    for i in range(parts):
        piece = r.astype(BF16)
        out.append(piece)
        if i + 1 < parts:
            r = r - piece.astype(F32)
    return jnp.concatenate(out, axis=axis)


def _seg_mean(x, seg):
    w = x.shape[-1]
    ones = _block_diag(w, seg, 1.0).astype(BF16)
    return jnp.dot(_split_cat(x, 2, axis=1), jnp.concatenate([ones, ones], axis=0),
                   preferred_element_type=F32) * (1.0 / seg)


def _stack_heads(x, masks):
    return jnp.concatenate([jnp.where(m, x, 0.0) for m in masks], axis=0)


def _slot_out(tail, nb, li, stacked, ng, whole):
    zeros = (0,) * len(tail)
    slot = li if stacked else 0
    if whole:
        spec = pl.BlockSpec((ng, DEPTH) + tail, lambda b: (b, 0) + zeros)
    else:
        spec = pl.BlockSpec((ng, None) + tail, lambda b: (b, slot) + zeros)
    return spec, jax.ShapeDtypeStruct((nb, DEPTH if stacked else 1) + tail, F32)


def _slot_view(ref, g, slot):
    if slot is None:
        return ref.at[g]
    for other in range(DEPTH):
        if other != slot:
            ref[g, other] = jnp.zeros(ref.shape[2:], ref.dtype)
    return ref.at[g, slot]


def _carry_prev(args, in_specs, prevs, first_out):
    aliases = {}
    for j, prev in enumerate(prevs):
        aliases[len(args)] = first_out + j
        args.append(prev)
        in_specs.append(pl.BlockSpec(memory_space=pl.ANY))
    return aliases


SSD_SEL_W = 2 * SSM_HEADS * SSM_STATE
CUM_PIECES = 3
WGT_PIECES = 2
DT_REP = CUM_PIECES + WGT_PIECES


def _ssd_kernel(nc, ng, has_s0, slot, *refs):
    refs = list(refs)
    (z_ref, xbc_ref, dt_ref, cw_ref, cb_ref, dtb_ref, alog_ref, dsk_ref, nw_ref) = refs[:9]
    s0_ref = refs[9] if has_s0 else None
    (y_ref, s_ref, xs_scr, xk_scr, xt_scr, dt_scr, cum_scr, dtt_scr, cumt_scr, yf_scr, yb_scr, st_scr,
     sel_scr) = refs[-13:]
    q = CHUNK
    nh, n, p = SSM_HEADS, SSM_STATE, SSM_HEAD_DIM
    rep = nh // SSM_GROUPS
    hm = _head_masks(D_SSM, nh)

    @pl.when(pl.program_id(0) == 0)
    def _():
        k = lax.broadcasted_iota(jnp.int32, (LANES, SSD_SEL_W), 0)
        col = lax.broadcasted_iota(jnp.int32, (LANES, SSD_SEL_W), 1)
        is_cum = col < nh * n
        grp = k // (2 * nh)
        grp_ok = (is_cum & (grp < CUM_PIECES)) | ((~is_cum) & (grp >= CUM_PIECES) & (grp < DT_REP))
        hit = grp_ok & (k % nh == (col % (nh * n)) // n)
        sel_scr[...] = jnp.where(hit, 1.0, 0.0).astype(BF16)

    cw = cw_ref[...]
    cb = cb_ref[...]
    neg_a = -jnp.exp(alog_ref[...])
    dtb = dtb_ref[...]

    class Seq:
        def __init__(self, g):
            self.z, self.xbc, self.dt_in, self.y, self.s = (z_ref.at[g], xbc_ref.at[g], dt_ref.at[g], y_ref.at[g],
                                                            _slot_view(s_ref, g, slot))
            self.s0 = s0_ref.at[g] if has_s0 else None
            self.xs, self.xk, self.xt, self.dt, self.cum = (xs_scr.at[g], xk_scr.at[g], xt_scr.at[g], dt_scr.at[g],
                                                            cum_scr.at[g])
            self.dtt, self.cumt, self.yf, self.yb, self.st = (dtt_scr.at[g], cumt_scr.at[g], yf_scr.at[g],
                                                              yb_scr.at[g], st_scr.at[g])

    seqs = [Seq(g) for g in range(ng)]

    def prep_seq(sq, c):
        rows = _rows(c)
        xall = _silu(_conv3_chunk(sq.xbc, c, nc, cw, cb))
        sq.xs[rows, :] = xall
        xs = xall[:, 0:D_SSM]
        sq.xk[c] = _stack_heads(xs, hm).astype(BF16)
        sq.xt[c] = xs.T.astype(BF16)
        dt = _softplus(sq.dt_in[rows, :] + dtb)
        la = _split_cat(dt * neg_a, 3, axis=0)
        tri = jnp.concatenate([_tri(True), _tri(False)], axis=0)
        tri = jnp.where(tri, 1.0, 0.0).astype(BF16)
        cs = jnp.dot(jnp.concatenate([tri] * 3, axis=1), la, preferred_element_type=F32)
        lane = lax.broadcasted_iota(jnp.int32, (1, LANES), 1)
        cum = jnp.where(lane % (2 * nh) < nh, cs[0:q], cs[q:2 * q])
        sq.dt[rows, :] = dt
        sq.cum[rows, :] = cum
        sq.dtt[c] = dt.T
        sq.cumt[c] = cum.T

    def prep(c, carry):
        for sq in seqs:
            prep_seq(sq, c)
        return carry

    lax.fori_loop(0, nc, prep, 0)

    for sq in seqs:
        for d in range(2):
            for h in range(nh):
                blk = slice(h * p, (h + 1) * p)
                sq.st[d, blk, :] = sq.s0[d, h].T if has_s0 else jnp.zeros((p, n), F32)

    def pack_scalars(sq, c, d):
        rows = _rows(c)
        dt = sq.dt[rows, :]
        cum = sq.cum[rows, :]
        edge = q - 1 if d == 0 else 0
        lane = lax.broadcasted_iota(jnp.int32, (1, LANES), 1)
        used = (lane < DT_REP * 2 * nh) & ((lane % (2 * nh)) // nh == d)
        cum = jnp.where(used, cum, 0.0)
        wgt = jnp.exp(cum[edge:edge + 1, :] - cum) * dt
        grp = lane // (2 * nh)
        packed = jnp.zeros_like(cum)
        for src, pieces, g0 in ((cum, CUM_PIECES, 0), (wgt, WGT_PIECES, CUM_PIECES)):
            rest = src
            for i in range(pieces):
                piece = rest.astype(BF16).astype(F32)
                packed = jnp.where(grp == g0 + i, piece, packed)
                rest = rest - piece
        return jnp.where(used, packed, 0.0).astype(BF16)

    def operands(sq, c, d):
        rows = _rows(c)
        bm = [sq.xs[rows, D_SSM + g * n:D_SSM + (g + 1) * n] for g in range(SSM_GROUPS)]
        cm = [sq.xs[rows, D_SSM + (SSM_GROUPS + g) * n:D_SSM + (SSM_GROUPS + g + 1) * n]
              for g in range(SSM_GROUPS)]
        return bm, cm

    def decay_blocks(sq, c, d, e, cb_t):
        mask = _tri(d == 0)
        blocks, ecol = [], []
        for h in range(nh):
            r = d * nh + h
            col = e[:, h * n:(h + 1) * n]
            decay = jnp.exp(jnp.where(mask, col - sq.cumt[c, r:r + 1, :], -jnp.inf))
            blocks.append((cb_t[h // rep] * decay * sq.dtt[c, r:r + 1, :]).astype(BF16))
            ecol.append(jnp.exp(col))
        lo = lax.broadcasted_iota(jnp.int32, (1, n), 1) < p
        ecol = jnp.concatenate([jnp.where(lo, ecol[2 * j], ecol[2 * j + 1]) for j in range(nh // 2)], axis=1)
        return jnp.concatenate(blocks, axis=1), ecol

    def scan(i, carry):
        j = nc - 1 - i
        jobs = [(sq, c, d) for sq in seqs for c, d in ((i, 0), (j, 1))]
        packed = jnp.concatenate([pack_scalars(*job) for job in jobs], axis=0)
        e_all = jnp.dot(packed, sel_scr[...], preferred_element_type=F32)
        es = [e_all[t * q:(t + 1) * q] for t in range(len(jobs))]
        ops = [operands(*job) for job in jobs]
        cb_t = [[_bdot_nt(cm[g], bm[g]) for g in range(SSM_GROUPS)] for bm, cm in ops]
        dec = [decay_blocks(*job, es[t], cb_t[t]) for t, job in enumerate(jobs)]
        intra = [jnp.dot(dec[t][0], sq.xk[c], preferred_element_type=F32) for t, (sq, c, d) in enumerate(jobs)]
        s_old = [sq.st[d] for sq, c, d in jobs]
        inter = [jnp.concatenate([_bdot_nt(cm[g], s_old[t][g * rep * p:(g + 1) * rep * p, :])
                                  for g in range(SSM_GROUPS)], axis=1) for t, (bm, cm) in enumerate(ops)]
        for t, (sq, c, d) in enumerate(jobs):
            bm = ops[t][0]
            edge = q - 1 if d == 0 else 0
            for h in range(nh):
                blk, cols = slice(h * p, (h + 1) * p), slice(h * n, (h + 1) * n)
                wk = (bm[h // rep] * es[t][:, nh * n + h * n:nh * n + (h + 1) * n]).astype(BF16)
                upd = jnp.dot(sq.xt[c, blk, :], wk, preferred_element_type=F32)
                sq.st[d, blk, :] = jnp.exp(es[t][edge:edge + 1, cols]) * s_old[t][blk, :] + upd
            y = intra[t] + dec[t][1] * inter[t]
            if d == 0:
                sq.yf[_rows(c), :] = y
            else:
                sq.yb[_rows(c), :] = y
        return carry

    lax.fori_loop(0, nc, scan, 0)

    for sq in seqs:
        for d in range(2):
            for h in range(nh):
                sq.s[d, h] = sq.st[d, h * p:(h + 1) * p, :].T

    dsk = dsk_ref[...]
    nw = nw_ref[...]

    def fin(c, carry):
        rows = _rows(c)
        for sq in seqs:
            y = sq.yf[rows, :] + sq.yb[rows, :] + dsk * sq.xs[rows, 0:D_SSM]
            sq.y[rows, :] = _rms(y * _silu(sq.z[rows, :].astype(F32)), nw)
        return carry

    lax.fori_loop(0, nc, fin, 0)


def _ssd(z, xbc, dt, wts, li, s0, nb, l, seq0, stacked=False, prev=None):
    nc = l // CHUNK
    has_s0 = s0 is not None
    names = ("ssd_conv_w", "ssd_conv_b", "ssd_dt_bias", "ssd_a_log", "ssd_d", "ssd_norm_w")
    ng = min(SEQ_GROUP, nb)
    args = [z.reshape(-1, l, D_SSM), xbc.reshape(-1, l, SSM_CONV_CH), dt.reshape(-1, l, DT_PAD)]
    args += [wts[n] for n in names]
    in_specs = [_seq_spec(l, D_SSM, ng, seq0), _seq_spec(l, SSM_CONV_CH, ng, seq0), _seq_spec(l, DT_PAD, ng, seq0)]
    in_specs += [_layer_const(wts[n], li) for n in names]
    if has_s0:
        args.append(s0)
        in_specs.append(pl.BlockSpec((ng, None, 2, SSM_HEADS, SSM_STATE, SSM_HEAD_DIM),
                                     lambda b: (b, li, 0, 0, 0, 0)))
    whole = stacked and prev is None
    st_spec, st_shape = _slot_out((2, SSM_HEADS, SSM_STATE, SSM_HEAD_DIM), nb, li, stacked, ng, whole)
    aliases = _carry_prev(args, in_specs, [] if prev is None else [prev], 1)
    y, s = pl.pallas_call(
        functools.partial(_ssd_kernel, nc, ng, has_s0, li if whole else None),
        grid=(nb // ng,),
        in_specs=in_specs,
        out_specs=[_seq_spec(l, D_SSM, ng), st_spec],
        out_shape=[jax.ShapeDtypeStruct((nb, l, D_SSM), F32), st_shape],
        input_output_aliases=aliases,
        scratch_shapes=[pltpu.VMEM((ng, l, SSM_CONV_CH), F32),
                        pltpu.VMEM((ng, nc, SSM_HEADS * CHUNK, D_SSM), BF16),
                        pltpu.VMEM((ng, nc, D_SSM, CHUNK), BF16),
                        pltpu.VMEM((ng, l, DT_PAD), F32), pltpu.VMEM((ng, l, DT_PAD), F32),
                        pltpu.VMEM((ng, nc, DT_PAD, CHUNK), F32), pltpu.VMEM((ng, nc, DT_PAD, CHUNK), F32),
                        pltpu.VMEM((ng, l, D_SSM), F32), pltpu.VMEM((ng, l, D_SSM), F32),
                        pltpu.VMEM((ng, 2, D_SSM, SSM_STATE), F32),
                        pltpu.VMEM((LANES, SSD_SEL_W), BF16)],
        compiler_params=_cparams("arbitrary"),
        name="ssd",
    )(*args)
    return y.reshape(nb * l, D_SSM), s


def _ret_kernel(nc, ng, has_s0, slot, *refs):
    refs = list(refs)
    u_ref, dl_ref, gn_ref = refs[:3]
    s0_ref = refs[3] if has_s0 else None
    y_ref, s_ref, yf_scr, yb_scr, st_scr, dm_scr, e_scr, ea_scr, bd_scr = refs[-9:]
    q = CHUNK
    hd = RET_HEAD_DIM
    hm = _head_masks(D_RET, RET_HEADS)

    @pl.when(pl.program_id(0) == 0)
    def _():
        log_g = -_softplus(-dl_ref[...])
        ii = lax.broadcasted_iota(jnp.int32, (q, q), 0)
        jj = lax.broadcasted_iota(jnp.int32, (q, q), 1)
        dij = (ii - jj).astype(F32)
        ri = lax.broadcasted_iota(jnp.int32, (q, 1), 0).astype(F32)
        lfs, lbs = [], []
        for h in range(RET_HEADS):
            lf = log_g[:, h:h + 1]
            lb = log_g[:, RET_HEADS + h:RET_HEADS + h + 1]
            lfs.append(lf)
            lbs.append(lb)
            d_f = jnp.exp(jnp.where(dij >= 0, dij * lf, -jnp.inf))
            d_b = jnp.exp(jnp.where(dij <= 0, -dij * lb, -jnp.inf))
            dm_scr[:, h * q:(h + 1) * q] = d_f + d_b
        lf_l = _by_head(hm, lfs)
        lb_l = _by_head(hm, lbs)
        e_scr[0] = jnp.exp((ri + 1.0) * lf_l)
        e_scr[1] = jnp.exp((q - ri) * lb_l)
        e_scr[2] = jnp.exp((q - 1.0 - ri) * lf_l)
        e_scr[3] = jnp.exp(ri * lb_l)
        ea_scr[0:1, :] = jnp.exp(q * lf_l)
        ea_scr[1:2, :] = jnp.exp(q * lb_l)
        bd_scr[...] = _block_diag(D_RET, hd, 1.0)

    class Seq:
        def __init__(self, g):
            self.u, self.y, self.s = u_ref.at[g], y_ref.at[g], _slot_view(s_ref, g, slot)
            self.s0 = s0_ref.at[g] if has_s0 else None
            self.yf, self.yb, self.st = yf_scr.at[g], yb_scr.at[g], st_scr.at[g]

    seqs = [Seq(g) for g in range(ng)]

    for sq in seqs:
        for d in range(2):
            for h in range(RET_HEADS):
                blk = slice(h * hd, (h + 1) * hd)
                if has_s0:
                    parts = [jnp.zeros((hd, hd), F32)] * RET_HEADS
                    parts[h] = sq.s0[d, h]
                    sq.st[d, blk, :] = jnp.concatenate(parts, axis=1)
                else:
                    sq.st[d, blk, :] = jnp.zeros((hd, D_RET), F32)

    def qkv(sq, rows):
        return (sq.u[rows, 0:D_RET], sq.u[rows, D_RET:2 * D_RET].astype(F32) * (RET_HEAD_DIM ** -0.5),
                sq.u[rows, 2 * D_RET:3 * D_RET])

    def state_step(sq, d, qq, kk, vv):
        s_old = sq.st[d]
        y = e_scr[d] * _bdot(qq, s_old)
        sq.st[d] = ea_scr[d:d + 1, :] * s_old + _bdot_tn(kk * e_scr[2 + d], vv) * bd_scr[...]
        return y

    def scan(i, carry):
        rows = _rows(i)
        rows_b = _rows(nc - 1 - i)
        fwd = [qkv(sq, rows) for sq in seqs]
        bwd = [qkv(sq, rows_b) for sq in seqs]
        sc = [_bdot_nt(qq, _stack_heads(kk, hm)) * dm_scr[...] for qq, kk, _ in fwd]
        intra = [_bdot(sc[g], _stack_heads(fwd[g][2], hm)) for g in range(ng)]
        inter = [state_step(sq, 0, *fwd[g]) for g, sq in enumerate(seqs)]
        back = [state_step(sq, 1, *bwd[g]) for g, sq in enumerate(seqs)]
        for g, sq in enumerate(seqs):
            sq.yf[rows, :] = intra[g] + inter[g]
            sq.yb[rows_b, :] = back[g]
        return carry

    lax.fori_loop(0, nc, scan, 0)

    for sq in seqs:
        for d in range(2):
            for h in range(RET_HEADS):
                sq.s[d, h] = sq.st[d, h * hd:(h + 1) * hd, h * hd:(h + 1) * hd]

    gn = gn_ref[...]

    def fin(c, carry):
        rows = _rows(c)
        for sq in seqs:
            y = sq.yf[rows, :] + sq.yb[rows, :]
            cen = y - _seg_mean(y, hd)
            var = _seg_mean(cen * cen, hd)
            sq.y[rows, :] = cen * lax.rsqrt(var + NORM_EPS) * gn * _silu(sq.u[rows, 3 * D_RET:4 * D_RET].astype(F32))
        return carry

    lax.fori_loop(0, nc, fin, 0)


def _retention(u, wts, li, s0, nb, l, seq0, stacked=False, prev=None):
    nc = l // CHUNK
    has_s0 = s0 is not None
    names = ("ret_decay_logit", "ret_gn_w")
    ng = min(RET_GROUP, nb)
    args = [u.reshape(-1, l, RET_IN)] + [wts[n] for n in names]
    in_specs = [_seq_spec(l, RET_IN, ng, seq0)] + [_layer_const(wts[n], li) for n in names]
    if has_s0:
        args.append(s0)
        in_specs.append(pl.BlockSpec((ng, None, 2, RET_HEADS, RET_HEAD_DIM, RET_HEAD_DIM),
                                     lambda b: (b, li, 0, 0, 0, 0)))
    whole = stacked and prev is None
    st_spec, st_shape = _slot_out((2, RET_HEADS, RET_HEAD_DIM, RET_HEAD_DIM), nb, li, stacked, ng, whole)
    aliases = _carry_prev(args, in_specs, [] if prev is None else [prev], 1)
    y, s = pl.pallas_call(
        functools.partial(_ret_kernel, nc, ng, has_s0, li if whole else None),
        grid=(nb // ng,),
        in_specs=in_specs,
        out_specs=[_seq_spec(l, D_RET, ng), st_spec],
        out_shape=[jax.ShapeDtypeStruct((nb, l, D_RET), F32), st_shape],
        input_output_aliases=aliases,
        scratch_shapes=[pltpu.VMEM((ng, l, D_RET), F32), pltpu.VMEM((ng, l, D_RET), F32),
                        pltpu.VMEM((ng, 2, D_RET, D_RET), F32), pltpu.VMEM((CHUNK, RET_HEADS * CHUNK), F32),
                        pltpu.VMEM((4, CHUNK, D_RET), F32), pltpu.VMEM((8, D_RET), F32),
                        pltpu.VMEM((D_RET, D_RET), F32)],
        compiler_params=_cparams("arbitrary"),
        name="retention",
    )(*args)
    return y.reshape(nb * l, D_RET), s


def _split(x):
    hi = x.astype(BF16)
    return hi, (x - hi.astype(F32)).astype(BF16)


def _dot3(a_hi, a_lo, b_hi, b_lo):
    d = lambda p, q: jnp.dot(p, q, preferred_element_type=F32)
    return d(a_hi, b_hi) + (d(a_lo, b_hi) + d(a_hi, b_lo))


@functools.lru_cache(maxsize=None)
def _dft_fwd_host(l):
    n = 2 * l
    f = np.arange(l, dtype=np.int64)[:, None]
    s = np.arange(l, dtype=np.int64)[None, :]
    ang = ((f * s) % n).astype(np.float64) * (2.0 * math.pi / n)
    im = -np.sin(ang)
    im[0] = np.where(np.arange(l) % 2 == 0, 1.0, -1.0)
    return np.concatenate([np.cos(ang), im], axis=0).astype(np.float32)


def _dft_tables(l):
    n = 2 * l
    fwd = jnp.asarray(_dft_fwd_host(l))
    wgt = np.full((n, 1), 2.0 / n, np.float32)
    wgt[0] = wgt[l] = 1.0 / n
    return _split(fwd) + _split((fwd * wgt).T)


def _dft_block_tables(tables, l):
    fwd_hi, _, inv_hi, _ = tables
    r = min(DFT_BLOCK, l)
    nblk = l // r
    fwd_blk = jnp.concatenate([fwd_hi[0:l].reshape(nblk, r, l), fwd_hi[l:2 * l].reshape(nblk, r, l)], axis=1)
    return fwd_blk, inv_hi.reshape(nblk, r, 2 * l)


def _hy_filter_kernel(l, feats_ref, dec_ref, w1_ref, b1_ref, w2_ref, b2_ref, w3_ref, fr_ref, fh_ref, fl_ref,
                      a_ref, b_ref, d_ref):
    fr = fr_ref[...]
    xdot = lambda a, b: _dot3(*_split(a), *_split(b))
    h = jnp.sin(fr * (xdot(feats_ref[...], w1_ref[...]) + b1_ref[...]))
    h = jnp.sin(fr * (xdot(h, w2_ref[...]) + b2_ref[...]))
    h = xdot(h, w3_ref[...])
    dec = jnp.concatenate([dec_ref[...]] * HY_ORDER, axis=-1)
    row0 = lax.broadcasted_iota(jnp.int32, (l, 1), 0) == 0
    hf = h[:, 0:HY_ORDER * D_HY] * dec
    hb = h[:, HY_ORDER * D_HY:2 * HY_ORDER * D_HY] * dec
    hb = jnp.where(row0, 0.0, hb)
    hs = _split(hf + hb)
    hd = _split(hf - hb)
    re = _dot3(fh_ref[0:l, :], fl_ref[0:l, :], *hs)
    ny = _dot3(fh_ref[l:l + 8, :], fl_ref[l:l + 8, :], *hs)[0:1]
    im = _dot3(fh_ref[l:2 * l, :], fl_ref[l:2 * l, :], *hd)
    for o in range(HY_ORDER):
        cols = slice(o * D_HY, (o + 1) * D_HY)
        a_ref[o] = re[:, cols]
        b_ref[o] = jnp.where(row0, 0.0, im[:, cols])
        d_ref[o] = jnp.where(row0, ny[:, cols], re[:, cols])


def _hy_filter(l, fwd_hi, fwd_lo, wts, li):
    pos = np.arange(l, dtype=np.float32)
    t = pos / np.float32(l - 1)
    bands = np.linspace(1e-4, HY_BANDS - 1, HY_BANDS, dtype=np.float32)
    ang = np.float32(2.0 * math.pi / l) * pos[:, None] * bands[None, :]
    feats = np.concatenate([t[:, None], np.cos(ang), -np.sin(ang)], axis=-1).astype(np.float32)
    feats = np.pad(feats, ((0, 0), (0, LANES - HY_EMB)))
    max_decay = math.log(HY_TARGET) / HY_FAST_DECAY
    min_decay = math.log(HY_TARGET) / HY_SLOW_DECAY
    deltas = np.abs(np.linspace(min_decay, max_decay, D_HY, dtype=np.float32))
    dec = np.exp(-t[:, None] * deltas[None, :]).astype(np.float32)
    spec = jax.ShapeDtypeStruct((HY_ORDER, l, D_HY), F32)
    names = ("hy_w1", "hy_b1", "hy_w2", "hy_b2", "hy_w3", "hy_freq")
    return pl.pallas_call(
        functools.partial(_hy_filter_kernel, l),
        grid=(1,),
        in_specs=[_full(feats.shape), _full(dec.shape)] + [_layer_const(wts[n], li) for n in names]
                 + [_full(fwd_hi.shape), _full(fwd_lo.shape)],
        out_specs=[_full(spec.shape)] * 3,
        out_shape=[spec] * 3,
        compiler_params=_cparams("arbitrary"),
        name="hyena_filter",
    )(jnp.asarray(feats), jnp.asarray(dec), *[wts[n] for n in names], fwd_hi, fwd_lo)


def _hy_kernel(nc, ng, u_ref, cw_ref, cb_ref, f_ref, g_ref, a_ref, b_ref, d_ref, bias_ref, y_ref,
               uc_scr, vh_scr, vl_scr, sh_scr, sl_scr, z_scr):
    l = nc * CHUNK
    r = min(DFT_BLOCK, l)
    nblk = l // r
    cw = cw_ref[...]
    cb = cb_ref[...]
    dot = lambda p, q: jnp.dot(p, q, preferred_element_type=F32)
    wide = lambda x: jnp.concatenate([x] * ng, axis=1)
    seq_cols = lambda g: slice(g * D_HY, (g + 1) * D_HY)

    def for_blocks(body):
        if nblk == 1:
            body(0)
        else:
            lax.fori_loop(0, nblk, lambda i, carry: (body(i), carry)[1], 0)

    def conv(c, carry):
        rows = _rows(c)
        for g in range(ng):
            uc = _conv3_chunk(u_ref.at[g], c, nc, cw, cb)
            uc_scr[g, rows, :] = uc
            vh_scr[rows, seq_cols(g)], vl_scr[rows, seq_cols(g)] = _split(uc[:, 0:D_HY])
        return carry

    lax.fori_loop(0, nc, conv, 0)

    def long_conv(o):
        def spectrum(i):
            rows = pl.ds(pl.multiple_of(i * r, r), r)
            rows_im = pl.ds(pl.multiple_of(l + i * r, r), r)
            p = dot(f_ref[i], vh_scr[...]) + dot(f_ref[i], vl_scr[...])
            zr, zi = p[0:r], p[r:2 * r]
            fa, fb, fd = wide(a_ref[o, rows, :]), wide(b_ref[o, rows, :]), wide(d_ref[o, rows, :])
            sh_scr[rows, :], sl_scr[rows, :] = _split(zr * fa - zi * fb)
            sh_scr[rows_im, :], sl_scr[rows_im, :] = _split(zr * fb + zi * fd)

        for_blocks(spectrum)

        def inverse(i):
            rows = pl.ds(pl.multiple_of(i * r, r), r)
            y = dot(g_ref[i], sh_scr[...]) + dot(g_ref[i], sl_scr[...])
            gate = jnp.concatenate([uc_scr[g, rows, (o + 1) * D_HY:(o + 2) * D_HY] for g in range(ng)], axis=1)
            if o == 0:
                v = jnp.concatenate([uc_scr[g, rows, 0:D_HY] for g in range(ng)], axis=1)
            else:
                v = z_scr[rows, :]
            out = gate * (y + v * wide(bias_ref[o:o + 1, :]))
            if o + 1 < HY_ORDER:
                z_scr[rows, :] = out
                vh_scr[rows, :], vl_scr[rows, :] = _split(out)
            else:
                for g in range(ng):
                    y_ref[g, rows, :] = out[:, seq_cols(g)]

        for_blocks(inverse)

    for o in range(HY_ORDER):
        long_conv(o)


def _hyena(u, wts, li, tables, spectra, nb, l, seq0):
    nc = l // CHUNK
    ng = min(HY_GROUP, nb)
    a, b, d = spectra
    names = ("hy_conv_w", "hy_conv_b")
    y = pl.pallas_call(
        functools.partial(_hy_kernel, nc, ng),
        grid=(nb // ng,),
        in_specs=[_seq_spec(l, HY_IN, ng, seq0)] + [_layer_const(wts[n], li) for n in names]
                 + [_const(tables[0].shape), _const(tables[1].shape),
                    _const((HY_ORDER, l, D_HY)), _const((HY_ORDER, l, D_HY)), _const((HY_ORDER, l, D_HY)),
                    _layer_const(wts["hy_bias"], li)],
        out_specs=pl.BlockSpec((ng, l, D_HY), lambda i: (i, 0, 0)),
        out_shape=jax.ShapeDtypeStruct((nb, l, D_HY), F32),
        scratch_shapes=[pltpu.VMEM((ng, l, HY_IN), F32),
                        pltpu.VMEM((l, ng * D_HY), BF16), pltpu.VMEM((l, ng * D_HY), BF16),
                        pltpu.VMEM((2 * l, ng * D_HY), BF16), pltpu.VMEM((2 * l, ng * D_HY), BF16),
                        pltpu.VMEM((l, ng * D_HY), F32)],
        compiler_params=_cparams("parallel"),
        name="hyena",
    )(u.reshape(-1, l, HY_IN), *[wts[n] for n in names], *tables, a, b, d, wts["hy_bias"])
    return y.reshape(nb * l, D_HY)


def _seg_rms(x, w):
    return x * lax.rsqrt(_seg_mean(x * x, HEAD_DIM) + NORM_EPS) * w


def _stack_q(q):
    lo = lax.broadcasted_iota(jnp.int32, (1, D_KV), 1) < HEAD_DIM
    qa = q[:, 0:D_KV]
    qb = q[:, D_KV:2 * D_KV]
    return jnp.concatenate([jnp.where(lo, qa, 0.0), jnp.where(lo, pltpu.roll(qa, HEAD_DIM, 1), 0.0),
                            jnp.where(lo, 0.0, pltpu.roll(qb, HEAD_DIM, 1)), jnp.where(lo, 0.0, qb)], axis=0)


def _unstack_o(o):
    r = o.shape[0] // ATT_HEADS
    lo = lax.broadcasted_iota(jnp.int32, (1, D_KV), 1) < HEAD_DIM
    ya = jnp.where(lo, o[0:r], pltpu.roll(o[r:2 * r], HEAD_DIM, 1))
    yb = jnp.where(lo, pltpu.roll(o[2 * r:3 * r], HEAD_DIM, 1), o[3 * r:4 * r])
    return jnp.concatenate([ya, yb], axis=1)


def _sink_col(sink_ref, li, r):
    rb = lax.broadcasted_iota(jnp.int32, (ATT_HEADS * r, 1), 0) // r
    col = jnp.full((ATT_HEADS * r, 1), sink_ref[li * ATT_HEADS + ATT_HEADS - 1], F32)
    for h in range(ATT_HEADS - 2, -1, -1):
        col = jnp.where(rb == h, sink_ref[li * ATT_HEADS + h], col)
    return col


def _ctx_attn_kernel(li, ng, slot, u_ref, qn_ref, kn_ref, sink_ref, *rest):
    y_ref, k_ref, v_ref = rest[-3:]
    l = u_ref.shape[1]
    sink = _sink_col(sink_ref, li, l)
    seqs = range(ng)
    u = [u_ref[g].astype(F32) for g in seqs]
    q = [_seg_rms(u[g][:, 0:D_ATT], qn_ref[...]) for g in seqs]
    k = [_seg_rms(u[g][:, D_ATT:D_ATT + D_KV], kn_ref[:, 0:D_KV]) for g in seqs]
    v = [u[g][:, D_ATT + D_KV:D_ATT + 2 * D_KV] for g in seqs]
    for g in seqs:
        _slot_view(k_ref, g, slot)[...] = k[g]
        _slot_view(v_ref, g, slot)[...] = v[g]
    s = [_bdot_nt(_stack_q(q[g]), k[g]) * (HEAD_DIM ** -0.5) for g in seqs]
    m = [jnp.maximum(jnp.max(s[g], axis=-1, keepdims=True), sink) for g in seqs]
    p = [jnp.exp(s[g] - m[g]) for g in seqs]
    den = [jnp.sum(p[g], axis=-1, keepdims=True) + jnp.exp(sink - m[g]) for g in seqs]
    o = [_bdot(p[g] * (1.0 / den[g]), v[g]) for g in seqs]
    for g in seqs:
        y_ref[g] = _unstack_o(o[g])


def _ctx_attention(u, wts, li, nb, l, seq0, prev=None):
    ng = min(ATT_GROUP, nb)
    whole = prev is None
    kv_spec, kv_shape = _slot_out((l, D_KV), nb, li, True, ng, whole)
    args = [u.reshape(-1, l, ATT_IN), wts["attn_q_norm"], wts["attn_k_norm"], wts["attn_sink"]]
    in_specs = [_seq_spec(l, ATT_IN, ng, seq0), _layer_const(wts["attn_q_norm"], li),
                _layer_const(wts["attn_k_norm"], li), pl.BlockSpec(memory_space=pltpu.SMEM)]
    aliases = _carry_prev(args, in_specs, [] if prev is None else list(prev), 1)
    y, k, v = pl.pallas_call(
        functools.partial(_ctx_attn_kernel, li, ng, li if whole else None),
        grid=(nb // ng,),
        in_specs=in_specs,
        out_specs=[_seq_spec(l, D_ATT, ng), kv_spec, kv_spec],
        out_shape=[jax.ShapeDtypeStruct((nb, l, D_ATT), F32), kv_shape, kv_shape],
        input_output_aliases=aliases,
        compiler_params=_cparams("parallel"),
        name="ctx_attention",
    )(*args)
    return y.reshape(nb * l, D_ATT), k, v


@functools.lru_cache(maxsize=None)
def _rope_tables_host(l):
    n_rows = l // GRID_W
    rows = np.repeat(np.arange(n_rows, dtype=np.float32), GRID_W)
    cols = np.tile(np.arange(GRID_W, dtype=np.float32), n_rows)
    nf = HEAD_DIM // 4
    inv = (np.float32(ROPE_BASE) ** (-np.arange(nf, dtype=np.float32) / np.float32(nf))).astype(np.float32)
    ar = rows[:, None] * inv[None, :]
    ac = cols[:, None] * inv[None, :]
    cos = np.concatenate([np.cos(ar), np.cos(ar), np.cos(ac), np.cos(ac)], axis=-1)
    sin = np.concatenate([-np.sin(ar), np.sin(ar), -np.sin(ac), np.sin(ac)], axis=-1)
    return (np.tile(cos, (1, ATT_HEADS)).astype(np.float32), np.tile(sin, (1, ATT_HEADS)).astype(np.float32))


def _rope(x, cos, sin):
    w = x.shape[-1]
    nf = HEAD_DIM // 4
    lane = lax.broadcasted_iota(jnp.int32, x.shape, 1)
    first = (lane % (2 * nf)) < nf
    partner = jnp.where(first, pltpu.roll(x, w - nf, 1), pltpu.roll(x, nf, 1))
    return x * cos + partner * sin


def _lat_attn_kernel(nblk, li, ng, u_ref, qn_ref, kn_ref, cos_ref, sin_ref, ck_ref, cv_ref, sink_ref,
                     y_ref, q_scr, k_scr, v_scr):
    blk = ATT_BLOCK
    l = nblk * blk
    seqs = range(ng)
    cos_q = cos_ref[...]
    sin_q = sin_ref[...]
    zeros = jnp.zeros((blk, D_KV), F32)
    for g in seqs:
        u = u_ref[g].astype(F32)
        q = _seg_rms(u[:, 0:D_ATT], qn_ref[...])
        k = _seg_rms(u[:, D_ATT:D_ATT + D_KV], kn_ref[:, 0:D_KV])
        q_scr[g] = _rope(q, cos_q, sin_q)
        k_scr[g, 0:blk, :] = zeros
        k_scr[g, blk + l:2 * blk + l, :] = zeros
        v_scr[g, 0:blk, :] = zeros
        v_scr[g, blk + l:2 * blk + l, :] = zeros
        k_scr[g, blk:blk + l, :] = _rope(k, cos_q[:, 0:D_KV], sin_q[:, 0:D_KV])
        v_scr[g, blk:blk + l, :] = u[:, D_ATT + D_KV:D_ATT + 2 * D_KV]

    scale = HEAD_DIM ** -0.5
    r = lax.broadcasted_iota(jnp.int32, (blk, 3 * blk), 0)
    cidx = lax.broadcasted_iota(jnp.int32, (blk, 3 * blk), 1)
    band = (cidx - r >= blk - WINDOW) & (cidx - r <= blk + WINDOW)
    sink = _sink_col(sink_ref, li, blk)

    def block(i, carry):
        rows = _rows(i)
        win = pl.ds(pl.multiple_of(i * blk, blk), 3 * blk)
        kpos = cidx + (i - 1) * blk
        valid = band & (kpos >= 0) & (kpos < l)
        valid = jnp.concatenate([valid] * ATT_HEADS, axis=0)
        qs = [_stack_q(q_scr[g, rows, :]) for g in seqs]
        s_loc = [jnp.where(valid, _bdot_nt(qs[g], k_scr[g, win, :]) * scale, -jnp.inf) for g in seqs]
        s_ctx = [_bdot_nt(qs[g], ck_ref[g]) * scale for g in seqs]
        m = [jnp.maximum(jnp.maximum(jnp.max(s_loc[g], axis=-1, keepdims=True),
                                     jnp.max(s_ctx[g], axis=-1, keepdims=True)), sink) for g in seqs]
        p_loc = [jnp.exp(s_loc[g] - m[g]) for g in seqs]
        p_ctx = [jnp.exp(s_ctx[g] - m[g]) for g in seqs]
        inv = [1.0 / (jnp.sum(p_loc[g], axis=-1, keepdims=True) + jnp.sum(p_ctx[g], axis=-1, keepdims=True)
                      + jnp.exp(sink - m[g])) for g in seqs]
        o = [_bdot(p_ctx[g] * inv[g], cv_ref[g]) + _bdot(p_loc[g] * inv[g], v_scr[g, win, :]) for g in seqs]
        for g in seqs:
            y_ref[g, rows, :] = _unstack_o(o[g])
        return carry

    lax.fori_loop(0, nblk, block, 0)


def _lat_attention(u, wts, li, ck, cv, nb, l, seq0):
    lc = ck.shape[2]
    nblk = l // ATT_BLOCK
    ng = min(ATT_GROUP, nb)
    cos, sin = _rope_tables_host(l)
    cache_spec = pl.BlockSpec((ng, None, lc, D_KV), lambda b: (b, li, 0, 0))
    y = pl.pallas_call(
        functools.partial(_lat_attn_kernel, nblk, li, ng),
        grid=(nb // ng,),
        in_specs=[_seq_spec(l, ATT_IN, ng, seq0), _layer_const(wts["attn_q_norm"], li),
                  _layer_const(wts["attn_k_norm"], li), _const((l, D_ATT)), _const((l, D_ATT)),
                  cache_spec, cache_spec, pl.BlockSpec(memory_space=pltpu.SMEM)],
        out_specs=_seq_spec(l, D_ATT, ng),
        out_shape=jax.ShapeDtypeStruct((nb, l, D_ATT), F32),
        scratch_shapes=[pltpu.VMEM((ng, l, D_ATT), F32), pltpu.VMEM((ng, l + 2 * ATT_BLOCK, D_KV), F32),
                        pltpu.VMEM((ng, l + 2 * ATT_BLOCK, D_KV), F32)],
        compiler_params=_cparams("parallel"),
        name="lat_attention",
    )(u.reshape(-1, l, ATT_IN), wts["attn_q_norm"], wts["attn_k_norm"], jnp.asarray(cos), jnp.asarray(sin),
      ck, cv, wts["attn_sink"])
    return y.reshape(nb * l, D_ATT)


def _pad_lane_tile(w):
    return jnp.pad(w, [(0, 0)] * (w.ndim - 1) + [(0, -w.shape[-1] % LANES)])


def _dt_weight(w):
    zeros = jnp.zeros(w.shape[:-1] + (DT_PAD - DT_REP * N_DT,), w.dtype)
    return jnp.concatenate([w[..., C_DT:C_DT + N_DT]] * DT_REP + [zeros], axis=-1).astype(BF16)


def _prep_weights(p):
    row = lambda a: a.reshape(DEPTH, 1, -1)
    pad_lanes = lambda a: jnp.pad(row(a), ((0, 0), (0, 0), (0, LANES - a[0].size)))
    return dict(
        norm_w=p["norm_w"].reshape(DEPTH * 3, 1, D_MODEL),
        ffn_w_in=p["ffn_w_in"].astype(BF16).reshape(DEPTH * 2, D_MODEL, 2 * D_FF),
        ffn_w_out=p["ffn_w_out"].astype(BF16).reshape(DEPTH * 2, D_FF, D_MODEL),
        mix_w_in=_pad_lane_tile(p["mix_w_in"]).astype(BF16), mix_w_dt=_dt_weight(p["mix_w_in"]),
        mix_w_out=p["mix_w_out"].astype(BF16),
        ssd_conv_w=p["ssd_conv_w"], ssd_conv_b=row(p["ssd_conv_b"]),
        ssd_dt_bias=pad_lanes(jnp.tile(row(p["ssd_dt_bias"]), (1, 1, DT_REP))),
        ssd_a_log=pad_lanes(jnp.tile(row(p["ssd_a_log"]), (1, 1, DT_REP))),
        ssd_d=row(jnp.repeat(p["ssd_d"], SSM_HEAD_DIM, axis=-1)), ssd_norm_w=row(p["ssd_norm_w"]),
        hy_conv_w=p["hy_conv_w"], hy_conv_b=row(p["hy_conv_b"]), hy_bias=p["hy_bias"],
        hy_w1=jnp.pad(p["hy_w1"], ((0, 0), (0, LANES - HY_EMB), (0, 0))), hy_b1=row(p["hy_b1"]),
        hy_w2=p["hy_w2"], hy_b2=row(p["hy_b2"]), hy_w3=p["hy_w3"], hy_freq=row(p["hy_freq"]),
        ret_decay_logit=pad_lanes(p["ret_decay_logit"]), ret_gn_w=row(p["ret_gn_w"]),
        attn_q_norm=row(jnp.tile(p["attn_q_norm"], (1, ATT_HEADS))),
        attn_k_norm=row(jnp.tile(p["attn_k_norm"], (1, ATT_HEADS))),
        attn_sink=p["attn_sink"].reshape(DEPTH * ATT_HEADS),
    )


def _mixers(u, wts, li, nb, l, seq0, ssd_s0, ret_s0, ctx_kv, hy_tables, hy_spectra, carried):
    z, xbc, hy, ret, att, dt = u
    ctx = ctx_kv is None
    y_ssd, s_ssd = _ssd(z, xbc, dt, wts, li, ssd_s0, nb, l, seq0, stacked=ctx, prev=carried.get("ssd"))
    y_hy = _hyena(hy, wts, li, hy_tables, hy_spectra, nb, l, seq0)
    y_ret, s_ret = _retention(ret, wts, li, ret_s0, nb, l, seq0, stacked=ctx, prev=carried.get("ret"))
    if ctx:
        y_att, k, v = _ctx_attention(att, wts, li, nb, l, seq0, prev=carried.get("kv"))
        carried = dict(ssd=s_ssd, ret=s_ret, kv=(k, v))
    else:
        y_att = _lat_attention(att, wts, li, ctx_kv[0], ctx_kv[1], nb, l, seq0)
    return (y_ssd, y_hy, y_ret, y_att), carried


def kernel(x_prompt, x_sample, cache_k, cache_v, state_ssd, state_ret, c, c_ctx, w_mod, b_mod, norm_w, ffn_w_in, ffn_w_out, mix_w_in, mix_w_out, ssd_conv_w, ssd_conv_b, ssd_dt_bias, ssd_a_log, ssd_d, ssd_norm_w, hy_conv_w, hy_conv_b, hy_w1, hy_b1, hy_w2, hy_b2, hy_w3, hy_freq, hy_bias, ret_decay_logit, ret_gn_w, attn_q_norm, attn_k_norm, attn_sink):
    bp, lp_len, _ = x_prompt.shape
    bs, ls_len, _ = x_sample.shape
    lc = cache_k.shape[2]

    cond = jnp.concatenate([c_ctx[None, :], c, jnp.zeros((MOD_ROWS - 1 - bs, D_MODEL), F32)], axis=0)
    mod = _modulation(cond, w_mod, b_mod)
    wts = _prep_weights(dict(
        norm_w=norm_w, ffn_w_in=ffn_w_in, ffn_w_out=ffn_w_out, mix_w_in=mix_w_in, mix_w_out=mix_w_out,
        ssd_conv_w=ssd_conv_w, ssd_conv_b=ssd_conv_b, ssd_dt_bias=ssd_dt_bias, ssd_a_log=ssd_a_log, ssd_d=ssd_d,
        ssd_norm_w=ssd_norm_w, hy_conv_w=hy_conv_w, hy_conv_b=hy_conv_b, hy_w1=hy_w1, hy_b1=hy_b1, hy_w2=hy_w2,
        hy_b2=hy_b2, hy_w3=hy_w3, hy_freq=hy_freq, hy_bias=hy_bias, ret_decay_logit=ret_decay_logit,
        ret_gn_w=ret_gn_w, attn_q_norm=attn_q_norm, attn_k_norm=attn_k_norm, attn_sink=attn_sink))
    ck = cache_k.reshape(bs, DEPTH, lc, D_KV)
    cv = cache_v.reshape(bs, DEPTH, lc, D_KV)

    tab_p = _dft_tables(lp_len)
    tab_s = _dft_tables(ls_len)
    blk_p = _dft_block_tables(tab_p, lp_len)
    blk_s = _dft_block_tables(tab_s, ls_len)

    tm = TOKEN_TILE
    tp, ts = bp * lp_len, bs * ls_len
    assert tp % tm == 0 and ts % tm == 0 and ls_len % tm == 0 and tp % ls_len == 0
    st = _Stream(tm, tp // tm, ts // tm, ls_len)
    x = (x_prompt.reshape(tp, D_MODEL), x_sample.reshape(ts, D_MODEL))
    carried = {}
    for li in range(DEPTH):
        spec_p = _hy_filter(lp_len, tab_p[0], tab_p[1], wts, li)
        spec_s = _hy_filter(ls_len, tab_s[0], tab_s[1], wts, li)
        x = _ffn(st, x, mod, wts, li, 0)
        u = _inproj(st, x, mod, wts, li)
        y_p, carried = _mixers(u, wts, li, bp, lp_len, 0, None, None, None, blk_p, spec_p, carried)
        y_s, _ = _mixers(u, wts, li, bs, ls_len, tp // ls_len, state_ssd, state_ret, (ck, cv), blk_s, spec_s, {})
        if li + 1 < DEPTH:
            x = _ffn(st, x, mod, wts, li, 1, mix=tuple(zip(y_p, y_s)))
        else:
            yp = _ffn(st.part(0), x, mod, wts, li, 1, mix=y_p)
            ys = _ffn(st.part(1), x, mod, wts, li, 1, mix=y_s)

    kv_shape = (bp, DEPTH, lp_len, ATT_KV_HEADS, HEAD_DIM)
    new_k, new_v = carried["kv"]
    return (yp.reshape(bp, lp_len, D_MODEL), ys.reshape(bs, ls_len, D_MODEL),
            new_k.reshape(kv_shape), new_v.reshape(kv_shape), carried["ssd"], carried["ret"])
```

```python
import functools
import math

import numpy as np
import jax
import jax.numpy as jnp
from jax import lax
from jax.experimental import pallas as pl
from jax.experimental.pallas import tpu as pltpu

F32 = jnp.float32
BF16 = jnp.bfloat16

D_MODEL = 1024
DEPTH = 2
GRID_W = 64
D_FF = 2816
N_MOD = 9
NORM_EPS = 1e-6
CHUNK = 128
D_SSM = 256
SSM_HEADS = 4
SSM_HEAD_DIM = 64
SSM_STATE = 128
SSM_GROUPS = 2
SSM_CONV_CH = D_SSM + 2 * SSM_GROUPS * SSM_STATE
D_HY = 256
HY_ORDER = 2
HY_BANDS = 16
HY_EMB = 1 + 2 * HY_BANDS
HY_HIDDEN = 64
HY_FAST_DECAY = 0.3
HY_SLOW_DECAY = 1.5
HY_TARGET = 1e-2
HY_IN = (HY_ORDER + 1) * D_HY
D_RET = 256
RET_HEADS = 4
RET_HEAD_DIM = 64
RET_IN = 4 * D_RET
ATT_HEADS = 4
ATT_KV_HEADS = 2
HEAD_DIM = 64
D_ATT = ATT_HEADS * HEAD_DIM
D_KV = ATT_KV_HEADS * HEAD_DIM
ATT_IN = D_ATT + 2 * D_KV
WINDOW = 128
ATT_BLOCK = 128
ROPE_BASE = 10000.0
D_MIX = D_SSM + D_HY + D_RET + D_ATT

LANES = 128
DT_PAD = LANES
VMEM_LIMIT = 56 * 1024 * 1024
MOD_ROWS = 8

TOKEN_TILE = 512
IN_TOKEN_TILE = 1024
DFT_BLOCK = 512
HY_GROUP = 4
SEQ_GROUP = 4
RET_GROUP = 4
ATT_GROUP = 4


def _cparams(*sem):
    return pltpu.CompilerParams(dimension_semantics=sem, vmem_limit_bytes=VMEM_LIMIT)


def _rms(x, w):
    return x * lax.rsqrt(jnp.mean(x * x, axis=-1, keepdims=True) + NORM_EPS) * w


def _silu(x):
    return x * (1.0 / (1.0 + jnp.exp(-x)))


def _softplus(x):
    return jnp.maximum(x, 0.0) + jnp.log1p(jnp.exp(-jnp.abs(x)))


def _bdot(a, b):
    return jnp.dot(a.astype(BF16), b.astype(BF16), preferred_element_type=F32)


def _bdot_nt(a, b):
    return lax.dot_general(a.astype(BF16), b.astype(BF16), (((1,), (1,)), ((), ())),
                           preferred_element_type=F32)


def _bdot_tn(a, b):
    return lax.dot_general(a.astype(BF16), b.astype(BF16), (((0,), (0,)), ((), ())),
                           preferred_element_type=F32)


def _full(shape):
    n = len(shape)
    return pl.BlockSpec(shape, lambda *_: (0,) * n)


def _const(shape):
    n = len(shape)
    return pl.BlockSpec(shape, lambda *_: (0,) * n, pipeline_mode=pl.Buffered(1))


def _layer_const(arr, li):
    tail = arr.shape[1:]
    zeros = (0,) * len(tail)
    return pl.BlockSpec((None,) + tail, lambda *_: (li,) + zeros, pipeline_mode=pl.Buffered(1))


class _Stream:
    def __init__(self, tm, nct, nlt, l_lat, first=0, count=None):
        self.tm, self.nct, self.nlt, self.l_lat = tm, nct, nlt, l_lat
        self.first = first
        self.count = nct + nlt if count is None else count

    def part(self, path):
        first, count = (0, self.nct) if path == 0 else (self.nct, self.nlt)
        return _Stream(self.tm, self.nct, self.nlt, self.l_lat, first, count)

    @property
    def tokens(self):
        return self.count * self.tm

    def merged(self, width):
        return pl.BlockSpec((self.tm, width), lambda i: (i + self.first, 0))

    def owned(self, width):
        return pl.BlockSpec((self.tm, width), lambda i: (i, 0))

    def pair(self, width):
        ctx = pl.BlockSpec((self.tm, width), lambda i: (jnp.minimum(i + self.first, self.nct - 1), 0))
        lat = pl.BlockSpec((self.tm, width), lambda i: (jnp.maximum(i + self.first - self.nct, 0), 0))
        return [ctx, lat]

    def mod(self, k, row_ctx, row_lat):
        def index(i):
            t = i + self.first
            lat_row = row_lat + (jnp.maximum(t - self.nct, 0) * self.tm) // self.l_lat
            return (jnp.where(t < self.nct, row_ctx, lat_row), 0, k)
        return pl.BlockSpec((None, 1, D_MODEL), index)


def _seq_spec(l, w, ng=None, seq0=0):
    step0 = seq0 // (ng or 1)
    return pl.BlockSpec((ng, l, w), lambda b: (b + step0, 0, 0))


def _head_masks(width, heads):
    lane = lax.broadcasted_iota(jnp.int32, (1, width), 1)
    hd = width // heads
    return [(lane >= h * hd) & (lane < (h + 1) * hd) for h in range(heads)]


def _by_head(masks, vals):
    out = vals[-1]
    for m, v in zip(masks[-2::-1], vals[-2::-1]):
        out = jnp.where(m, v, out)
    return out


def _block_diag(n, blk, value):
    i = lax.broadcasted_iota(jnp.int32, (n, n), 0) // blk
    j = lax.broadcasted_iota(jnp.int32, (n, n), 1) // blk
    return jnp.where(i == j, value, 0.0).astype(F32)


def _mod_kernel(c_ref, w_ref, b_ref, o_ref):
    c = c_ref[...]
    o_ref[...] = _bdot(_silu(c), w_ref[...]) + b_ref[...]


def _modulation(cond, w_mod, b_mod):
    out = pl.pallas_call(
        _mod_kernel,
        grid=(DEPTH, N_MOD),
        in_specs=[pl.BlockSpec((MOD_ROWS, D_MODEL), lambda l, j: (0, 0)),
                  pl.BlockSpec((None, D_MODEL, D_MODEL), lambda l, j: (l, 0, j)),
                  pl.BlockSpec((None, 1, D_MODEL), lambda l, j: (l, 0, j))],
        out_specs=pl.BlockSpec((None, MOD_ROWS, D_MODEL), lambda l, j: (l, 0, j)),
        out_shape=jax.ShapeDtypeStruct((DEPTH, MOD_ROWS, N_MOD * D_MODEL), F32),
        compiler_params=_cparams("arbitrary", "arbitrary"),
        name="modulation",
    )(cond, w_mod, b_mod.reshape(DEPTH, 1, N_MOD * D_MODEL))
    return out.reshape(DEPTH * MOD_ROWS, 1, N_MOD * D_MODEL)


def _ffn_kernel(n_mix, pair_x, pair_mix, nct, first, *refs):
    refs = list(refs)
    o_ref = refs.pop()
    is_ctx = pl.program_id(0) + first < nct

    def read(pair):
        if pair:
            a, b = refs.pop(0), refs.pop(0)
            return jnp.where(is_ctx, a[...], b[...])
        return refs.pop(0)[...]

    x = read(pair_x)
    sh_ref, sc_ref, g_ref, nw_ref, wi_ref, wo_ref = (refs.pop(0) for _ in range(6))
    if n_mix:
        gm_ref = refs.pop(0)
        ys = [read(pair_mix) for _ in range(n_mix)]
        wm_ref = refs.pop(0)
        w = D_MIX // n_mix
        acc = _bdot(ys[0], wm_ref[0:w, :])
        for j in range(1, n_mix):
            acc += _bdot(ys[j], wm_ref[j * w:(j + 1) * w, :])
        x = x + gm_ref[...] * acc
    h = (_rms(x, nw_ref[...] * (1.0 + sc_ref[...])) + sh_ref[...]).astype(BF16)
    gate = jnp.dot(h, wi_ref[:, 0:D_FF], preferred_element_type=F32)
    up = jnp.dot(h, wi_ref[:, D_FF:2 * D_FF], preferred_element_type=F32)
    o_ref[...] = x + 0.5 * g_ref[...] * _bdot(_silu(gate) * up, wo_ref[...])


def _ffn(st, x, mod, wts, li, k, mix=()):
    row_ctx, row_lat = li * MOD_ROWS, li * MOD_ROWS + 1
    w_in, w_out = wts["ffn_w_in"], wts["ffn_w_out"]
    pair_x = isinstance(x, tuple)
    pair_mix = bool(mix) and isinstance(mix[0], tuple)
    args = list(x) if pair_x else [x]
    in_specs = st.pair(D_MODEL) if pair_x else [st.merged(D_MODEL)]
    args += [mod, mod, mod, wts["norm_w"], w_in, w_out]
    in_specs += [st.mod(6 * k + j, row_ctx, row_lat) for j in range(3)]
    in_specs += [_layer_const(wts["norm_w"], 3 * li + 2 * k), _layer_const(w_in, 2 * li + k),
                 _layer_const(w_out, 2 * li + k)]
    if mix:
        args.append(mod)
        in_specs.append(st.mod(5, row_ctx, row_lat))
        for y in mix:
            width = (y[0] if pair_mix else y).shape[1]
            args += list(y) if pair_mix else [y]
            in_specs += st.pair(width) if pair_mix else [st.owned(width)]
        args.append(wts["mix_w_out"])
        in_specs.append(_layer_const(wts["mix_w_out"], li))
    return pl.pallas_call(
        functools.partial(_ffn_kernel, len(mix), pair_x, pair_mix, st.nct, st.first),
        grid=(st.count,),
        in_specs=in_specs,
        out_specs=st.owned(D_MODEL),
        out_shape=jax.ShapeDtypeStruct((st.tokens, D_MODEL), F32),
        compiler_params=_cparams("parallel"),
        name="ffn",
    )(*args)


_IN_SPLITS = (("z", D_SSM, BF16), ("xbc", SSM_CONV_CH, F32), ("hy", HY_IN, F32), ("ret", RET_IN, BF16),
              ("att", ATT_IN, BF16), ("dt", DT_PAD, F32))


C_DT = D_SSM + SSM_CONV_CH
N_DT = 2 * SSM_HEADS
W_TAIL = HY_IN + RET_IN + ATT_IN


def _inproj_kernel(x_ref, sh_ref, sc_ref, nw_ref, w_ref, wdt_ref, *rest):
    o_refs, wt_scr = rest[:-1], rest[-1]

    @pl.when(pl.program_id(0) == 0)
    def _():
        wt_scr[...] = w_ref[:, C_DT + N_DT:C_DT + N_DT + W_TAIL]

    h = (_rms(x_ref[...], nw_ref[...] * (1.0 + sc_ref[...])) + sh_ref[...]).astype(BF16)
    off = 0
    for (name, width, dtype), o_ref in zip(_IN_SPLITS, o_refs):
        if name == "dt":
            w = wdt_ref[...]
        elif off < C_DT:
            w = w_ref[:, off:off + width]
        else:
            w = wt_scr[:, off - C_DT:off - C_DT + width]
        o_ref[...] = jnp.dot(h, w, preferred_element_type=F32).astype(dtype)
        off += width


def _inproj(st, x, mod, wts, li):
    row_ctx, row_lat = li * MOD_ROWS, li * MOD_ROWS + 1
    return pl.pallas_call(
        _inproj_kernel,
        grid=(st.count,),
        in_specs=[st.merged(D_MODEL), st.mod(3, row_ctx, row_lat), st.mod(4, row_ctx, row_lat),
                  _layer_const(wts["norm_w"], 3 * li + 1),
                  _layer_const(wts["mix_w_in"], li), _layer_const(wts["mix_w_dt"], li)],
        out_specs=[st.owned(width) for _, width, _ in _IN_SPLITS],
        out_shape=[jax.ShapeDtypeStruct((st.tokens, width), dtype) for _, width, dtype in _IN_SPLITS],
        scratch_shapes=[pltpu.VMEM((D_MODEL, W_TAIL), BF16)],
        compiler_params=_cparams("arbitrary"),
        name="mix_in",
    )(x, mod, mod, wts["norm_w"], wts["mix_w_in"], wts["mix_w_dt"])


def _conv3_chunk(x_ref, c, nc, w, b):
    q = CHUNK
    l = nc * q
    r0 = pl.multiple_of(c * q, q)
    x = x_ref[pl.ds(r0, q), :]
    prev = x_ref[pl.ds(jnp.maximum(r0 - 1, 0), 1), :]
    nxt = x_ref[pl.ds(jnp.minimum(r0 + q, l - 1), 1), :]
    prev = jnp.where(c > 0, prev, 0.0)
    nxt = jnp.where(c < nc - 1, nxt, 0.0)
    rid = lax.broadcasted_iota(jnp.int32, (q, 1), 0)
    xm1 = jnp.where(rid == 0, prev, pltpu.roll(x, 1, 0))
    xp1 = jnp.where(rid == q - 1, nxt, pltpu.roll(x, q - 1, 0))
    return xm1 * w[0:1, :] + x * w[1:2, :] + xp1 * w[2:3, :] + b


def _tri(lower):
    i = lax.broadcasted_iota(jnp.int32, (CHUNK, CHUNK), 0)
    j = lax.broadcasted_iota(jnp.int32, (CHUNK, CHUNK), 1)
    return (j <= i) if lower else (j >= i)


def _rows(c):
    return pl.ds(pl.multiple_of(c * CHUNK, CHUNK), CHUNK)


def _split_cat(v, parts, axis):
    out, r = [], v
    for i in range(parts):
        piece = r.astype(BF16)
        out.append(piece)
        if i + 1 < parts:
            r = r - piece.astype(F32)
    return jnp.concatenate(out, axis=axis)


def _seg_mean(x, seg):
    w = x.shape[-1]
    ones = _block_diag(w, seg, 1.0).astype(BF16)
    return jnp.dot(_split_cat(x, 2, axis=1), jnp.concatenate([ones, ones], axis=0),
                   preferred_element_type=F32) * (1.0 / seg)


def _stack_heads(x, masks):
    return jnp.concatenate([jnp.where(m, x, 0.0) for m in masks], axis=0)


def _slot_out(tail, nb, li, stacked, ng, whole):
    zeros = (0,) * len(tail)
    slot = li if stacked else 0
    if whole:
        spec = pl.BlockSpec((ng, DEPTH) + tail, lambda b: (b, 0) + zeros)
    else:
        spec = pl.BlockSpec((ng, None) + tail, lambda b: (b, slot) + zeros)
    return spec, jax.ShapeDtypeStruct((nb, DEPTH if stacked else 1) + tail, F32)


def _slot_view(ref, g, slot):
    if slot is None:
        return ref.at[g]
    for other in range(DEPTH):
        if other != slot:
            ref[g, other] = jnp.zeros(ref.shape[2:], ref.dtype)
    return ref.at[g, slot]


def _carry_prev(args, in_specs, prevs, first_out):
    aliases = {}
    for j, prev in enumerate(prevs):
        aliases[len(args)] = first_out + j
        args.append(prev)
        in_specs.append(pl.BlockSpec(memory_space=pl.ANY))
    return aliases


SSD_SEL_W = 2 * SSM_HEADS * SSM_STATE
CUM_PIECES = 3
WGT_PIECES = 2
DT_REP = CUM_PIECES + WGT_PIECES


def _ssd_kernel(nc, ng, has_s0, slot, *refs):
    refs = list(refs)
    (z_ref, xbc_ref, dt_ref, cw_ref, cb_ref, dtb_ref, alog_ref, dsk_ref, nw_ref) = refs[:9]
    s0_ref = refs[9] if has_s0 else None
    (y_ref, s_ref, xs_scr, xk_scr, xt_scr, dt_scr, cum_scr, dtt_scr, cumt_scr, yf_scr, yb_scr, st_scr,
     sel_scr) = refs[-13:]
    q = CHUNK
    nh, n, p = SSM_HEADS, SSM_STATE, SSM_HEAD_DIM
    rep = nh // SSM_GROUPS
    hm = _head_masks(D_SSM, nh)

    @pl.when(pl.program_id(0) == 0)
    def _():
        k = lax.broadcasted_iota(jnp.int32, (LANES, SSD_SEL_W), 0)
        col = lax.broadcasted_iota(jnp.int32, (LANES, SSD_SEL_W), 1)
        is_cum = col < nh * n
        grp = k // (2 * nh)
        grp_ok = (is_cum & (grp < CUM_PIECES)) | ((~is_cum) & (grp >= CUM_PIECES) & (grp < DT_REP))
        hit = grp_ok & (k % nh == (col % (nh * n)) // n)
        sel_scr[...] = jnp.where(hit, 1.0, 0.0).astype(BF16)

    cw = cw_ref[...]
    cb = cb_ref[...]
    neg_a = -jnp.exp(alog_ref[...])
    dtb = dtb_ref[...]

    class Seq:
        def __init__(self, g):
            self.z, self.xbc, self.dt_in, self.y, self.s = (z_ref.at[g], xbc_ref.at[g], dt_ref.at[g], y_ref.at[g],
                                                            _slot_view(s_ref, g, slot))
            self.s0 = s0_ref.at[g] if has_s0 else None
            self.xs, self.xk, self.xt, self.dt, self.cum = (xs_scr.at[g], xk_scr.at[g], xt_scr.at[g], dt_scr.at[g],
                                                            cum_scr.at[g])
            self.dtt, self.cumt, self.yf, self.yb, self.st = (dtt_scr.at[g], cumt_scr.at[g], yf_scr.at[g],
                                                              yb_scr.at[g], st_scr.at[g])

    seqs = [Seq(g) for g in range(ng)]

    def prep_seq(sq, c):
        rows = _rows(c)
        xall = _silu(_conv3_chunk(sq.xbc, c, nc, cw, cb))
        sq.xs[rows, :] = xall
        xs = xall[:, 0:D_SSM]
        sq.xk[c] = _stack_heads(xs, hm).astype(BF16)
        sq.xt[c] = xs.T.astype(BF16)
        dt = _softplus(sq.dt_in[rows, :] + dtb)
        la = _split_cat(dt * neg_a, 3, axis=0)
        tri = jnp.concatenate([_tri(True), _tri(False)], axis=0)
        tri = jnp.where(tri, 1.0, 0.0).astype(BF16)
        cs = jnp.dot(jnp.concatenate([tri] * 3, axis=1), la, preferred_element_type=F32)
        lane = lax.broadcasted_iota(jnp.int32, (1, LANES), 1)
        cum = jnp.where(lane % (2 * nh) < nh, cs[0:q], cs[q:2 * q])
        sq.dt[rows, :] = dt
        sq.cum[rows, :] = cum
        sq.dtt[c] = dt.T
        sq.cumt[c] = cum.T

    def prep(c, carry):
        for sq in seqs:
            prep_seq(sq, c)
        return carry

    lax.fori_loop(0, nc, prep, 0)

    for sq in seqs:
        for d in range(2):
            for h in range(nh):
                blk = slice(h * p, (h + 1) * p)
                sq.st[d, blk, :] = sq.s0[d, h].T if has_s0 else jnp.zeros((p, n), F32)

    def pack_scalars(sq, c, d):
        rows = _rows(c)
        dt = sq.dt[rows, :]
        cum = sq.cum[rows, :]
        edge = q - 1 if d == 0 else 0
        lane = lax.broadcasted_iota(jnp.int32, (1, LANES), 1)
        used = (lane < DT_REP * 2 * nh) & ((lane % (2 * nh)) // nh == d)
        cum = jnp.where(used, cum, 0.0)
        wgt = jnp.exp(cum[edge:edge + 1, :] - cum) * dt
        grp = lane // (2 * nh)
        packed = jnp.zeros_like(cum)
        for src, pieces, g0 in ((cum, CUM_PIECES, 0), (wgt, WGT_PIECES, CUM_PIECES)):
            rest = src
            for i in range(pieces):
                piece = rest.astype(BF16).astype(F32)
                packed = jnp.where(grp == g0 + i, piece, packed)
                rest = rest - piece
        return jnp.where(used, packed, 0.0).astype(BF16)

    def operands(sq, c, d):
        rows = _rows(c)
        bm = [sq.xs[rows, D_SSM + g * n:D_SSM + (g + 1) * n] for g in range(SSM_GROUPS)]
        cm = [sq.xs[rows, D_SSM + (SSM_GROUPS + g) * n:D_SSM + (SSM_GROUPS + g + 1) * n]
              for g in range(SSM_GROUPS)]
        return bm, cm

    def decay_blocks(sq, c, d, e, cb_t):
        mask = _tri(d == 0)
        blocks, ecol = [], []
        for h in range(nh):
            r = d * nh + h
            col = e[:, h * n:(h + 1) * n]
            decay = jnp.exp(jnp.where(mask, col - sq.cumt[c, r:r + 1, :], -jnp.inf))
            blocks.append((cb_t[h // rep] * decay * sq.dtt[c, r:r + 1, :]).astype(BF16))
            ecol.append(jnp.exp(col))
        lo = lax.broadcasted_iota(jnp.int32, (1, n), 1) < p
        ecol = jnp.concatenate([jnp.where(lo, ecol[2 * j], ecol[2 * j + 1]) for j in range(nh // 2)], axis=1)
        return jnp.concatenate(blocks, axis=1), ecol

    def scan(i, carry):
        j = nc - 1 - i
        jobs = [(sq, c, d) for sq in seqs for c, d in ((i, 0), (j, 1))]
        packed = jnp.concatenate([pack_scalars(*job) for job in jobs], axis=0)
        e_all = jnp.dot(packed, sel_scr[...], preferred_element_type=F32)
        es = [e_all[t * q:(t + 1) * q] for t in range(len(jobs))]
        ops = [operands(*job) for job in jobs]
        cb_t = [[_bdot_nt(cm[g], bm[g]) for g in range(SSM_GROUPS)] for bm, cm in ops]
        dec = [decay_blocks(*job, es[t], cb_t[t]) for t, job in enumerate(jobs)]
        intra = [jnp.dot(dec[t][0], sq.xk[c], preferred_element_type=F32) for t, (sq, c, d) in enumerate(jobs)]
        s_old = [sq.st[d] for sq, c, d in jobs]
        inter = [jnp.concatenate([_bdot_nt(cm[g], s_old[t][g * rep * p:(g + 1) * rep * p, :])
                                  for g in range(SSM_GROUPS)], axis=1) for t, (bm, cm) in enumerate(ops)]
        for t, (sq, c, d) in enumerate(jobs):
            bm = ops[t][0]
            edge = q - 1 if d == 0 else 0
            for h in range(nh):
                blk, cols = slice(h * p, (h + 1) * p), slice(h * n, (h + 1) * n)
                wk = (bm[h // rep] * es[t][:, nh * n + h * n:nh * n + (h + 1) * n]).astype(BF16)
                upd = jnp.dot(sq.xt[c, blk, :], wk, preferred_element_type=F32)
                sq.st[d, blk, :] = jnp.exp(es[t][edge:edge + 1, cols]) * s_old[t][blk, :] + upd
            y = intra[t] + dec[t][1] * inter[t]
            if d == 0:
                sq.yf[_rows(c), :] = y
            else:
                sq.yb[_rows(c), :] = y
        return carry

    lax.fori_loop(0, nc, scan, 0)

    for sq in seqs:
        for d in range(2):
            for h in range(nh):
                sq.s[d, h] = sq.st[d, h * p:(h + 1) * p, :].T

    dsk = dsk_ref[...]
    nw = nw_ref[...]

    def fin(c, carry):
        rows = _rows(c)
        for sq in seqs:
            y = sq.yf[rows, :] + sq.yb[rows, :] + dsk * sq.xs[rows, 0:D_SSM]
            sq.y[rows, :] = _rms(y * _silu(sq.z[rows, :].astype(F32)), nw)
        return carry

    lax.fori_loop(0, nc, fin, 0)


def _ssd(z, xbc, dt, wts, li, s0, nb, l, seq0, stacked=False, prev=None):
    nc = l // CHUNK
    has_s0 = s0 is not None
    names = ("ssd_conv_w", "ssd_conv_b", "ssd_dt_bias", "ssd_a_log", "ssd_d", "ssd_norm_w")
    ng = min(SEQ_GROUP, nb)
    args = [z.reshape(-1, l, D_SSM), xbc.reshape(-1, l, SSM_CONV_CH), dt.reshape(-1, l, DT_PAD)]
    args += [wts[n] for n in names]
    in_specs = [_seq_spec(l, D_SSM, ng, seq0), _seq_spec(l, SSM_CONV_CH, ng, seq0), _seq_spec(l, DT_PAD, ng, seq0)]
    in_specs += [_layer_const(wts[n], li) for n in names]
    if has_s0:
        args.append(s0)
        in_specs.append(pl.BlockSpec((ng, None, 2, SSM_HEADS, SSM_STATE, SSM_HEAD_DIM),
                                     lambda b: (b, li, 0, 0, 0, 0)))
    whole = stacked and prev is None
    st_spec, st_shape = _slot_out((2, SSM_HEADS, SSM_STATE, SSM_HEAD_DIM), nb, li, stacked, ng, whole)
    aliases = _carry_prev(args, in_specs, [] if prev is None else [prev], 1)
    y, s = pl.pallas_call(
        functools.partial(_ssd_kernel, nc, ng, has_s0, li if whole else None),
        grid=(nb // ng,),
        in_specs=in_specs,
        out_specs=[_seq_spec(l, D_SSM, ng), st_spec],
        out_shape=[jax.ShapeDtypeStruct((nb, l, D_SSM), F32), st_shape],
        input_output_aliases=aliases,
        scratch_shapes=[pltpu.VMEM((ng, l, SSM_CONV_CH), F32),
                        pltpu.VMEM((ng, nc, SSM_HEADS * CHUNK, D_SSM), BF16),
                        pltpu.VMEM((ng, nc, D_SSM, CHUNK), BF16),
                        pltpu.VMEM((ng, l, DT_PAD), F32), pltpu.VMEM((ng, l, DT_PAD), F32),
                        pltpu.VMEM((ng, nc, DT_PAD, CHUNK), F32), pltpu.VMEM((ng, nc, DT_PAD, CHUNK), F32),
                        pltpu.VMEM((ng, l, D_SSM), F32), pltpu.VMEM((ng, l, D_SSM), F32),
                        pltpu.VMEM((ng, 2, D_SSM, SSM_STATE), F32),
                        pltpu.VMEM((LANES, SSD_SEL_W), BF16)],
        compiler_params=_cparams("arbitrary"),
        name="ssd",
    )(*args)
    return y.reshape(nb * l, D_SSM), s


def _ret_kernel(nc, ng, has_s0, slot, *refs):
    refs = list(refs)
    u_ref, dl_ref, gn_ref = refs[:3]
    s0_ref = refs[3] if has_s0 else None
    y_ref, s_ref, yf_scr, yb_scr, st_scr, dm_scr, e_scr, ea_scr, bd_scr = refs[-9:]
    q = CHUNK
    hd = RET_HEAD_DIM
    hm = _head_masks(D_RET, RET_HEADS)

    @pl.when(pl.program_id(0) == 0)
    def _():
        log_g = -_softplus(-dl_ref[...])
        ii = lax.broadcasted_iota(jnp.int32, (q, q), 0)
        jj = lax.broadcasted_iota(jnp.int32, (q, q), 1)
        dij = (ii - jj).astype(F32)
        ri = lax.broadcasted_iota(jnp.int32, (q, 1), 0).astype(F32)
        lfs, lbs = [], []
        for h in range(RET_HEADS):
            lf = log_g[:, h:h + 1]
            lb = log_g[:, RET_HEADS + h:RET_HEADS + h + 1]
            lfs.append(lf)
            lbs.append(lb)
            d_f = jnp.exp(jnp.where(dij >= 0, dij * lf, -jnp.inf))
            d_b = jnp.exp(jnp.where(dij <= 0, -dij * lb, -jnp.inf))
            dm_scr[:, h * q:(h + 1) * q] = d_f + d_b
        lf_l = _by_head(hm, lfs)
        lb_l = _by_head(hm, lbs)
        e_scr[0] = jnp.exp((ri + 1.0) * lf_l)
        e_scr[1] = jnp.exp((q - ri) * lb_l)
        e_scr[2] = jnp.exp((q - 1.0 - ri) * lf_l)
        e_scr[3] = jnp.exp(ri * lb_l)
        ea_scr[0:1, :] = jnp.exp(q * lf_l)
        ea_scr[1:2, :] = jnp.exp(q * lb_l)
        bd_scr[...] = _block_diag(D_RET, hd, 1.0)

    class Seq:
        def __init__(self, g):
            self.u, self.y, self.s = u_ref.at[g], y_ref.at[g], _slot_view(s_ref, g, slot)
            self.s0 = s0_ref.at[g] if has_s0 else None
            self.yf, self.yb, self.st = yf_scr.at[g], yb_scr.at[g], st_scr.at[g]

    seqs = [Seq(g) for g in range(ng)]

    for sq in seqs:
        for d in range(2):
            for h in range(RET_HEADS):
                blk = slice(h * hd, (h + 1) * hd)
                if has_s0:
                    parts = [jnp.zeros((hd, hd), F32)] * RET_HEADS
                    parts[h] = sq.s0[d, h]
                    sq.st[d, blk, :] = jnp.concatenate(parts, axis=1)
                else:
                    sq.st[d, blk, :] = jnp.zeros((hd, D_RET), F32)

    def qkv(sq, rows):
        return (sq.u[rows, 0:D_RET], sq.u[rows, D_RET:2 * D_RET].astype(F32) * (RET_HEAD_DIM ** -0.5),
                sq.u[rows, 2 * D_RET:3 * D_RET])

    def state_step(sq, d, qq, kk, vv):
        s_old = sq.st[d]
        y = e_scr[d] * _bdot(qq, s_old)
        sq.st[d] = ea_scr[d:d + 1, :] * s_old + _bdot_tn(kk * e_scr[2 + d], vv) * bd_scr[...]
        return y

    def scan(i, carry):
        rows = _rows(i)
        rows_b = _rows(nc - 1 - i)
        fwd = [qkv(sq, rows) for sq in seqs]
        bwd = [qkv(sq, rows_b) for sq in seqs]
        sc = [_bdot_nt(qq, _stack_heads(kk, hm)) * dm_scr[...] for qq, kk, _ in fwd]
        intra = [_bdot(sc[g], _stack_heads(fwd[g][2], hm)) for g in range(ng)]
        inter = [state_step(sq, 0, *fwd[g]) for g, sq in enumerate(seqs)]
        back = [state_step(sq, 1, *bwd[g]) for g, sq in enumerate(seqs)]
        for g, sq in enumerate(seqs):
            sq.yf[rows, :] = intra[g] + inter[g]
            sq.yb[rows_b, :] = back[g]
        return carry

    lax.fori_loop(0, nc, scan, 0)

    for sq in seqs:
        for d in range(2):
            for h in range(RET_HEADS):
                sq.s[d, h] = sq.st[d, h * hd:(h + 1) * hd, h * hd:(h + 1) * hd]

    gn = gn_ref[...]

    def fin(c, carry):
        rows = _rows(c)
        for sq in seqs:
            y = sq.yf[rows, :] + sq.yb[rows, :]
            cen = y - _seg_mean(y, hd)
            var = _seg_mean(cen * cen, hd)
            sq.y[rows, :] = cen * lax.rsqrt(var + NORM_EPS) * gn * _silu(sq.u[rows, 3 * D_RET:4 * D_RET].astype(F32))
        return carry

    lax.fori_loop(0, nc, fin, 0)


def _retention(u, wts, li, s0, nb, l, seq0, stacked=False, prev=None):
    nc = l // CHUNK
    has_s0 = s0 is not None
    names = ("ret_decay_logit", "ret_gn_w")
    ng = min(RET_GROUP, nb)
    args = [u.reshape(-1, l, RET_IN)] + [wts[n] for n in names]
    in_specs = [_seq_spec(l, RET_IN, ng, seq0)] + [_layer_const(wts[n], li) for n in names]
    if has_s0:
        args.append(s0)
        in_specs.append(pl.BlockSpec((ng, None, 2, RET_HEADS, RET_HEAD_DIM, RET_HEAD_DIM),
                                     lambda b: (b, li, 0, 0, 0, 0)))
    whole = stacked and prev is None
    st_spec, st_shape = _slot_out((2, RET_HEADS, RET_HEAD_DIM, RET_HEAD_DIM), nb, li, stacked, ng, whole)
    aliases = _carry_prev(args, in_specs, [] if prev is None else [prev], 1)
    y, s = pl.pallas_call(
        functools.partial(_ret_kernel, nc, ng, has_s0, li if whole else None),
        grid=(nb // ng,),
        in_specs=in_specs,
        out_specs=[_seq_spec(l, D_RET, ng), st_spec],
        out_shape=[jax.ShapeDtypeStruct((nb, l, D_RET), F32), st_shape],
        input_output_aliases=aliases,
        scratch_shapes=[pltpu.VMEM((ng, l, D_RET), F32), pltpu.VMEM((ng, l, D_RET), F32),
                        pltpu.VMEM((ng, 2, D_RET, D_RET), F32), pltpu.VMEM((CHUNK, RET_HEADS * CHUNK), F32),
                        pltpu.VMEM((4, CHUNK, D_RET), F32), pltpu.VMEM((8, D_RET), F32),
                        pltpu.VMEM((D_RET, D_RET), F32)],
        compiler_params=_cparams("arbitrary"),
        name="retention",
    )(*args)
    return y.reshape(nb * l, D_RET), s


def _split(x):
    hi = x.astype(BF16)
    return hi, (x - hi.astype(F32)).astype(BF16)


def _dot3(a_hi, a_lo, b_hi, b_lo):
    d = lambda p, q: jnp.dot(p, q, preferred_element_type=F32)
    return d(a_hi, b_hi) + (d(a_lo, b_hi) + d(a_hi, b_lo))


@functools.lru_cache(maxsize=None)
def _dft_fwd_host(l):
    n = 2 * l
    f = np.arange(l, dtype=np.int64)[:, None]
    s = np.arange(l, dtype=np.int64)[None, :]
    ang = ((f * s) % n).astype(np.float64) * (2.0 * math.pi / n)
    im = -np.sin(ang)
    im[0] = np.where(np.arange(l) % 2 == 0, 1.0, -1.0)
    return np.concatenate([np.cos(ang), im], axis=0).astype(np.float32)


def _dft_tables(l):
    n = 2 * l
    fwd = jnp.asarray(_dft_fwd_host(l))
    wgt = np.full((n, 1), 2.0 / n, np.float32)
    wgt[0] = wgt[l] = 1.0 / n
    return _split(fwd) + _split((fwd * wgt).T)


def _dft_block_tables(tables, l):
    fwd_hi, _, inv_hi, _ = tables
    r = min(DFT_BLOCK, l)
    nblk = l // r
    fwd_blk = jnp.concatenate([fwd_hi[0:l].reshape(nblk, r, l), fwd_hi[l:2 * l].reshape(nblk, r, l)], axis=1)
    return fwd_blk, inv_hi.reshape(nblk, r, 2 * l)


def _hy_filter_kernel(l, feats_ref, dec_ref, w1_ref, b1_ref, w2_ref, b2_ref, w3_ref, fr_ref, fh_ref, fl_ref,
                      a_ref, b_ref, d_ref):
    fr = fr_ref[...]
    xdot = lambda a, b: _dot3(*_split(a), *_split(b))
    h = jnp.sin(fr * (xdot(feats_ref[...], w1_ref[...]) + b1_ref[...]))
    h = jnp.sin(fr * (xdot(h, w2_ref[...]) + b2_ref[...]))
    h = xdot(h, w3_ref[...])
    dec = jnp.concatenate([dec_ref[...]] * HY_ORDER, axis=-1)
    row0 = lax.broadcasted_iota(jnp.int32, (l, 1), 0) == 0
    hf = h[:, 0:HY_ORDER * D_HY] * dec
    hb = h[:, HY_ORDER * D_HY:2 * HY_ORDER * D_HY] * dec
    hb = jnp.where(row0, 0.0, hb)
    hs = _split(hf + hb)
    hd = _split(hf - hb)
    re = _dot3(fh_ref[0:l, :], fl_ref[0:l, :], *hs)
    ny = _dot3(fh_ref[l:l + 8, :], fl_ref[l:l + 8, :], *hs)[0:1]
    im = _dot3(fh_ref[l:2 * l, :], fl_ref[l:2 * l, :], *hd)
    for o in range(HY_ORDER):
        cols = slice(o * D_HY, (o + 1) * D_HY)
        a_ref[o] = re[:, cols]
        b_ref[o] = jnp.where(row0, 0.0, im[:, cols])
        d_ref[o] = jnp.where(row0, ny[:, cols], re[:, cols])


def _hy_filter(l, fwd_hi, fwd_lo, wts, li):
    pos = np.arange(l, dtype=np.float32)
    t = pos / np.float32(l - 1)
    bands = np.linspace(1e-4, HY_BANDS - 1, HY_BANDS, dtype=np.float32)
    ang = np.float32(2.0 * math.pi / l) * pos[:, None] * bands[None, :]
    feats = np.concatenate([t[:, None], np.cos(ang), -np.sin(ang)], axis=-1).astype(np.float32)
    feats = np.pad(feats, ((0, 0), (0, LANES - HY_EMB)))
    max_decay = math.log(HY_TARGET) / HY_FAST_DECAY
    min_decay = math.log(HY_TARGET) / HY_SLOW_DECAY
    deltas = np.abs(np.linspace(min_decay, max_decay, D_HY, dtype=np.float32))
    dec = np.exp(-t[:, None] * deltas[None, :]).astype(np.float32)
    spec = jax.ShapeDtypeStruct((HY_ORDER, l, D_HY), F32)
    names = ("hy_w1", "hy_b1", "hy_w2", "hy_b2", "hy_w3", "hy_freq")
    return pl.pallas_call(
        functools.partial(_hy_filter_kernel, l),
        grid=(1,),
        in_specs=[_full(feats.shape), _full(dec.shape)] + [_layer_const(wts[n], li) for n in names]
                 + [_full(fwd_hi.shape), _full(fwd_lo.shape)],
        out_specs=[_full(spec.shape)] * 3,
        out_shape=[spec] * 3,
        compiler_params=_cparams("arbitrary"),
        name="hyena_filter",
    )(jnp.asarray(feats), jnp.asarray(dec), *[wts[n] for n in names], fwd_hi, fwd_lo)


def _hy_kernel(nc, ng, u_ref, cw_ref, cb_ref, f_ref, g_ref, a_ref, b_ref, d_ref, bias_ref, y_ref,
               uc_scr, vh_scr, vl_scr, sh_scr, sl_scr, z_scr):
    l = nc * CHUNK
    r = min(DFT_BLOCK, l)
    nblk = l // r
    cw = cw_ref[...]
    cb = cb_ref[...]
    dot = lambda p, q: jnp.dot(p, q, preferred_element_type=F32)
    wide = lambda x: jnp.concatenate([x] * ng, axis=1)
    seq_cols = lambda g: slice(g * D_HY, (g + 1) * D_HY)

    def for_blocks(body):
        if nblk == 1:
            body(0)
        else:
            lax.fori_loop(0, nblk, lambda i, carry: (body(i), carry)[1], 0)

    def conv(c, carry):
        rows = _rows(c)
        for g in range(ng):
            uc = _conv3_chunk(u_ref.at[g], c, nc, cw, cb)
            uc_scr[g, rows, :] = uc
            vh_scr[rows, seq_cols(g)], vl_scr[rows, seq_cols(g)] = _split(uc[:, 0:D_HY])
        return carry

    lax.fori_loop(0, nc, conv, 0)

    def long_conv(o):
        def spectrum(i):
            rows = pl.ds(pl.multiple_of(i * r, r), r)
            rows_im = pl.ds(pl.multiple_of(l + i * r, r), r)
            p = dot(f_ref[i], vh_scr[...]) + dot(f_ref[i], vl_scr[...])
            zr, zi = p[0:r], p[r:2 * r]
            fa, fb, fd = wide(a_ref[o, rows, :]), wide(b_ref[o, rows, :]), wide(d_ref[o, rows, :])
            sh_scr[rows, :], sl_scr[rows, :] = _split(zr * fa - zi * fb)
            sh_scr[rows_im, :], sl_scr[rows_im, :] = _split(zr * fb + zi * fd)

        for_blocks(spectrum)

        def inverse(i):
            rows = pl.ds(pl.multiple_of(i * r, r), r)
            y = dot(g_ref[i], sh_scr[...]) + dot(g_ref[i], sl_scr[...])
            gate = jnp.concatenate([uc_scr[g, rows, (o + 1) * D_HY:(o + 2) * D_HY] for g in range(ng)], axis=1)
            if o == 0:
                v = jnp.concatenate([uc_scr[g, rows, 0:D_HY] for g in range(ng)], axis=1)
            else:
                v = z_scr[rows, :]
            out = gate * (y + v * wide(bias_ref[o:o + 1, :]))
            if o + 1 < HY_ORDER:
                z_scr[rows, :] = out
                vh_scr[rows, :], vl_scr[rows, :] = _split(out)
            else:
                for g in range(ng):
                    y_ref[g, rows, :] = out[:, seq_cols(g)]

        for_blocks(inverse)

    for o in range(HY_ORDER):
        long_conv(o)


def _hyena(u, wts, li, tables, spectra, nb, l, seq0):
    nc = l // CHUNK
    ng = min(HY_GROUP, nb)
    a, b, d = spectra
    names = ("hy_conv_w", "hy_conv_b")
    y = pl.pallas_call(
        functools.partial(_hy_kernel, nc, ng),
        grid=(nb // ng,),
        in_specs=[_seq_spec(l, HY_IN, ng, seq0)] + [_layer_const(wts[n], li) for n in names]
                 + [_const(tables[0].shape), _const(tables[1].shape),
                    _const((HY_ORDER, l, D_HY)), _const((HY_ORDER, l, D_HY)), _const((HY_ORDER, l, D_HY)),
                    _layer_const(wts["hy_bias"], li)],
        out_specs=pl.BlockSpec((ng, l, D_HY), lambda i: (i, 0, 0)),
        out_shape=jax.ShapeDtypeStruct((nb, l, D_HY), F32),
        scratch_shapes=[pltpu.VMEM((ng, l, HY_IN), F32),
                        pltpu.VMEM((l, ng * D_HY), BF16), pltpu.VMEM((l, ng * D_HY), BF16),
                        pltpu.VMEM((2 * l, ng * D_HY), BF16), pltpu.VMEM((2 * l, ng * D_HY), BF16),
                        pltpu.VMEM((l, ng * D_HY), F32)],
        compiler_params=_cparams("parallel"),
        name="hyena",
    )(u.reshape(-1, l, HY_IN), *[wts[n] for n in names], *tables, a, b, d, wts["hy_bias"])
    return y.reshape(nb * l, D_HY)


def _seg_rms(x, w):
    return x * lax.rsqrt(_seg_mean(x * x, HEAD_DIM) + NORM_EPS) * w


def _stack_q(q):
    lo = lax.broadcasted_iota(jnp.int32, (1, D_KV), 1) < HEAD_DIM
    qa = q[:, 0:D_KV]
    qb = q[:, D_KV:2 * D_KV]
    return jnp.concatenate([jnp.where(lo, qa, 0.0), jnp.where(lo, pltpu.roll(qa, HEAD_DIM, 1), 0.0),
                            jnp.where(lo, 0.0, pltpu.roll(qb, HEAD_DIM, 1)), jnp.where(lo, 0.0, qb)], axis=0)


def _unstack_o(o):
    r = o.shape[0] // ATT_HEADS
    lo = lax.broadcasted_iota(jnp.int32, (1, D_KV), 1) < HEAD_DIM
    ya = jnp.where(lo, o[0:r], pltpu.roll(o[r:2 * r], HEAD_DIM, 1))
    yb = jnp.where(lo, pltpu.roll(o[2 * r:3 * r], HEAD_DIM, 1), o[3 * r:4 * r])
    return jnp.concatenate([ya, yb], axis=1)


def _sink_col(sink_ref, li, r):
    rb = lax.broadcasted_iota(jnp.int32, (ATT_HEADS * r, 1), 0) // r
    col = jnp.full((ATT_HEADS * r, 1), sink_ref[li * ATT_HEADS + ATT_HEADS - 1], F32)
    for h in range(ATT_HEADS - 2, -1, -1):
        col = jnp.where(rb == h, sink_ref[li * ATT_HEADS + h], col)
    return col


def _ctx_attn_kernel(li, ng, slot, u_ref, qn_ref, kn_ref, sink_ref, *rest):
    y_ref, k_ref, v_ref = rest[-3:]
    l = u_ref.shape[1]
    sink = _sink_col(sink_ref, li, l)
    seqs = range(ng)
    u = [u_ref[g].astype(F32) for g in seqs]
    q = [_seg_rms(u[g][:, 0:D_ATT], qn_ref[...]) for g in seqs]
    k = [_seg_rms(u[g][:, D_ATT:D_ATT + D_KV], kn_ref[:, 0:D_KV]) for g in seqs]
    v = [u[g][:, D_ATT + D_KV:D_ATT + 2 * D_KV] for g in seqs]
    for g in seqs:
        _slot_view(k_ref, g, slot)[...] = k[g]
        _slot_view(v_ref, g, slot)[...] = v[g]
    s = [_bdot_nt(_stack_q(q[g]), k[g]) * (HEAD_DIM ** -0.5) for g in seqs]
    m = [jnp.maximum(jnp.max(s[g], axis=-1, keepdims=True), sink) for g in seqs]
    p = [jnp.exp(s[g] - m[g]) for g in seqs]
    den = [jnp.sum(p[g], axis=-1, keepdims=True) + jnp.exp(sink - m[g]) for g in seqs]
    o = [_bdot(p[g] * (1.0 / den[g]), v[g]) for g in seqs]
    for g in seqs:
        y_ref[g] = _unstack_o(o[g])


def _ctx_attention(u, wts, li, nb, l, seq0, prev=None):
    ng = min(ATT_GROUP, nb)
    whole = prev is None
    kv_spec, kv_shape = _slot_out((l, D_KV), nb, li, True, ng, whole)
    args = [u.reshape(-1, l, ATT_IN), wts["attn_q_norm"], wts["attn_k_norm"], wts["attn_sink"]]
    in_specs = [_seq_spec(l, ATT_IN, ng, seq0), _layer_const(wts["attn_q_norm"], li),
                _layer_const(wts["attn_k_norm"], li), pl.BlockSpec(memory_space=pltpu.SMEM)]
    aliases = _carry_prev(args, in_specs, [] if prev is None else list(prev), 1)
    y, k, v = pl.pallas_call(
        functools.partial(_ctx_attn_kernel, li, ng, li if whole else None),
        grid=(nb // ng,),
        in_specs=in_specs,
        out_specs=[_seq_spec(l, D_ATT, ng), kv_spec, kv_spec],
        out_shape=[jax.ShapeDtypeStruct((nb, l, D_ATT), F32), kv_shape, kv_shape],
        input_output_aliases=aliases,
        compiler_params=_cparams("parallel"),
        name="ctx_attention",
    )(*args)
    return y.reshape(nb * l, D_ATT), k, v


@functools.lru_cache(maxsize=None)
def _rope_tables_host(l):
    n_rows = l // GRID_W
    rows = np.repeat(np.arange(n_rows, dtype=np.float32), GRID_W)
    cols = np.tile(np.arange(GRID_W, dtype=np.float32), n_rows)
    nf = HEAD_DIM // 4
    inv = (np.float32(ROPE_BASE) ** (-np.arange(nf, dtype=np.float32) / np.float32(nf))).astype(np.float32)
    ar = rows[:, None] * inv[None, :]
    ac = cols[:, None] * inv[None, :]
    cos = np.concatenate([np.cos(ar), np.cos(ar), np.cos(ac), np.cos(ac)], axis=-1)
    sin = np.concatenate([-np.sin(ar), np.sin(ar), -np.sin(ac), np.sin(ac)], axis=-1)
    return (np.tile(cos, (1, ATT_HEADS)).astype(np.float32), np.tile(sin, (1, ATT_HEADS)).astype(np.float32))


def _rope(x, cos, sin):
    w = x.shape[-1]
    nf = HEAD_DIM // 4
    lane = lax.broadcasted_iota(jnp.int32, x.shape, 1)
    first = (lane % (2 * nf)) < nf
    partner = jnp.where(first, pltpu.roll(x, w - nf, 1), pltpu.roll(x, nf, 1))
    return x * cos + partner * sin


def _lat_attn_kernel(nblk, li, ng, u_ref, qn_ref, kn_ref, cos_ref, sin_ref, ck_ref, cv_ref, sink_ref,
                     y_ref, q_scr, k_scr, v_scr):
    blk = ATT_BLOCK
    l = nblk * blk
    seqs = range(ng)
    cos_q = cos_ref[...]
    sin_q = sin_ref[...]
    zeros = jnp.zeros((blk, D_KV), F32)
    for g in seqs:
        u = u_ref[g].astype(F32)
        q = _seg_rms(u[:, 0:D_ATT], qn_ref[...])
        k = _seg_rms(u[:, D_ATT:D_ATT + D_KV], kn_ref[:, 0:D_KV])
        q_scr[g] = _rope(q, cos_q, sin_q)
        k_scr[g, 0:blk, :] = zeros
        k_scr[g, blk + l:2 * blk + l, :] = zeros
        v_scr[g, 0:blk, :] = zeros
        v_scr[g, blk + l:2 * blk + l, :] = zeros
        k_scr[g, blk:blk + l, :] = _rope(k, cos_q[:, 0:D_KV], sin_q[:, 0:D_KV])
        v_scr[g, blk:blk + l, :] = u[:, D_ATT + D_KV:D_ATT + 2 * D_KV]

    scale = HEAD_DIM ** -0.5
    r = lax.broadcasted_iota(jnp.int32, (blk, 3 * blk), 0)
    cidx = lax.broadcasted_iota(jnp.int32, (blk, 3 * blk), 1)
    band = (cidx - r >= blk - WINDOW) & (cidx - r <= blk + WINDOW)
    sink = _sink_col(sink_ref, li, blk)

    def block(i, carry):
        rows = _rows(i)
        win = pl.ds(pl.multiple_of(i * blk, blk), 3 * blk)
        kpos = cidx + (i - 1) * blk
        valid = band & (kpos >= 0) & (kpos < l)
        valid = jnp.concatenate([valid] * ATT_HEADS, axis=0)
        qs = [_stack_q(q_scr[g, rows, :]) for g in seqs]
        s_loc = [jnp.where(valid, _bdot_nt(qs[g], k_scr[g, win, :]) * scale, -jnp.inf) for g in seqs]
        s_ctx = [_bdot_nt(qs[g], ck_ref[g]) * scale for g in seqs]
        m = [jnp.maximum(jnp.maximum(jnp.max(s_loc[g], axis=-1, keepdims=True),
                                     jnp.max(s_ctx[g], axis=-1, keepdims=True)), sink) for g in seqs]
        p_loc = [jnp.exp(s_loc[g] - m[g]) for g in seqs]
        p_ctx = [jnp.exp(s_ctx[g] - m[g]) for g in seqs]
        inv = [1.0 / (jnp.sum(p_loc[g], axis=-1, keepdims=True) + jnp.sum(p_ctx[g], axis=-1, keepdims=True)
                      + jnp.exp(sink - m[g])) for g in seqs]
        o = [_bdot(p_ctx[g] * inv[g], cv_ref[g]) + _bdot(p_loc[g] * inv[g], v_scr[g, win, :]) for g in seqs]
        for g in seqs:
            y_ref[g, rows, :] = _unstack_o(o[g])
        return carry

    lax.fori_loop(0, nblk, block, 0)


def _lat_attention(u, wts, li, ck, cv, nb, l, seq0):
    lc = ck.shape[2]
    nblk = l // ATT_BLOCK
    ng = min(ATT_GROUP, nb)
    cos, sin = _rope_tables_host(l)
    cache_spec = pl.BlockSpec((ng, None, lc, D_KV), lambda b: (b, li, 0, 0))
    y = pl.pallas_call(
        functools.partial(_lat_attn_kernel, nblk, li, ng),
        grid=(nb // ng,),
        in_specs=[_seq_spec(l, ATT_IN, ng, seq0), _layer_const(wts["attn_q_norm"], li),
                  _layer_const(wts["attn_k_norm"], li), _const((l, D_ATT)), _const((l, D_ATT)),
                  cache_spec, cache_spec, pl.BlockSpec(memory_space=pltpu.SMEM)],
        out_specs=_seq_spec(l, D_ATT, ng),
        out_shape=jax.ShapeDtypeStruct((nb, l, D_ATT), F32),
        scratch_shapes=[pltpu.VMEM((ng, l, D_ATT), F32), pltpu.VMEM((ng, l + 2 * ATT_BLOCK, D_KV), F32),
                        pltpu.VMEM((ng, l + 2 * ATT_BLOCK, D_KV), F32)],
        compiler_params=_cparams("parallel"),
        name="lat_attention",
    )(u.reshape(-1, l, ATT_IN), wts["attn_q_norm"], wts["attn_k_norm"], jnp.asarray(cos), jnp.asarray(sin),
      ck, cv, wts["attn_sink"])
    return y.reshape(nb * l, D_ATT)


def _pad_lane_tile(w):
    return jnp.pad(w, [(0, 0)] * (w.ndim - 1) + [(0, -w.shape[-1] % LANES)])


def _dt_weight(w):
    zeros = jnp.zeros(w.shape[:-1] + (DT_PAD - DT_REP * N_DT,), w.dtype)
    return jnp.concatenate([w[..., C_DT:C_DT + N_DT]] * DT_REP + [zeros], axis=-1).astype(BF16)


def _prep_weights(p):
    row = lambda a: a.reshape(DEPTH, 1, -1)
    pad_lanes = lambda a: jnp.pad(row(a), ((0, 0), (0, 0), (0, LANES - a[0].size)))
    return dict(
        norm_w=p["norm_w"].reshape(DEPTH * 3, 1, D_MODEL),
        ffn_w_in=p["ffn_w_in"].astype(BF16).reshape(DEPTH * 2, D_MODEL, 2 * D_FF),
        ffn_w_out=p["ffn_w_out"].astype(BF16).reshape(DEPTH * 2, D_FF, D_MODEL),
        mix_w_in=_pad_lane_tile(p["mix_w_in"]).astype(BF16), mix_w_dt=_dt_weight(p["mix_w_in"]),
        mix_w_out=p["mix_w_out"].astype(BF16),
        ssd_conv_w=p["ssd_conv_w"], ssd_conv_b=row(p["ssd_conv_b"]),
        ssd_dt_bias=pad_lanes(jnp.tile(row(p["ssd_dt_bias"]), (1, 1, DT_REP))),
        ssd_a_log=pad_lanes(jnp.tile(row(p["ssd_a_log"]), (1, 1, DT_REP))),
        ssd_d=row(jnp.repeat(p["ssd_d"], SSM_HEAD_DIM, axis=-1)), ssd_norm_w=row(p["ssd_norm_w"]),
        hy_conv_w=p["hy_conv_w"], hy_conv_b=row(p["hy_conv_b"]), hy_bias=p["hy_bias"],
        hy_w1=jnp.pad(p["hy_w1"], ((0, 0), (0, LANES - HY_EMB), (0, 0))), hy_b1=row(p["hy_b1"]),
        hy_w2=p["hy_w2"], hy_b2=row(p["hy_b2"]), hy_w3=p["hy_w3"], hy_freq=row(p["hy_freq"]),
        ret_decay_logit=pad_lanes(p["ret_decay_logit"]), ret_gn_w=row(p["ret_gn_w"]),
        attn_q_norm=row(jnp.tile(p["attn_q_norm"], (1, ATT_HEADS))),
        attn_k_norm=row(jnp.tile(p["attn_k_norm"], (1, ATT_HEADS))),
        attn_sink=p["attn_sink"].reshape(DEPTH * ATT_HEADS),
    )


def _mixers(u, wts, li, nb, l, seq0, ssd_s0, ret_s0, ctx_kv, hy_tables, hy_spectra, carried):
    z, xbc, hy, ret, att, dt = u
    ctx = ctx_kv is None
    y_ssd, s_ssd = _ssd(z, xbc, dt, wts, li, ssd_s0, nb, l, seq0, stacked=ctx, prev=carried.get("ssd"))
    y_hy = _hyena(hy, wts, li, hy_tables, hy_spectra, nb, l, seq0)
    y_ret, s_ret = _retention(ret, wts, li, ret_s0, nb, l, seq0, stacked=ctx, prev=carried.get("ret"))
    if ctx:
        y_att, k, v = _ctx_attention(att, wts, li, nb, l, seq0, prev=carried.get("kv"))
        carried = dict(ssd=s_ssd, ret=s_ret, kv=(k, v))
    else:
        y_att = _lat_attention(att, wts, li, ctx_kv[0], ctx_kv[1], nb, l, seq0)
    return (y_ssd, y_hy, y_ret, y_att), carried


def kernel(x_prompt, x_sample, cache_k, cache_v, state_ssd, state_ret, c, c_ctx, w_mod, b_mod, norm_w, ffn_w_in, ffn_w_out, mix_w_in, mix_w_out, ssd_conv_w, ssd_conv_b, ssd_dt_bias, ssd_a_log, ssd_d, ssd_norm_w, hy_conv_w, hy_conv_b, hy_w1, hy_b1, hy_w2, hy_b2, hy_w3, hy_freq, hy_bias, ret_decay_logit, ret_gn_w, attn_q_norm, attn_k_norm, attn_sink):
    bp, lp_len, _ = x_prompt.shape
    bs, ls_len, _ = x_sample.shape
    lc = cache_k.shape[2]

    cond = jnp.concatenate([c_ctx[None, :], c, jnp.zeros((MOD_ROWS - 1 - bs, D_MODEL), F32)], axis=0)
    mod = _modulation(cond, w_mod, b_mod)
    wts = _prep_weights(dict(
        norm_w=norm_w, ffn_w_in=ffn_w_in, ffn_w_out=ffn_w_out, mix_w_in=mix_w_in, mix_w_out=mix_w_out,
        ssd_conv_w=ssd_conv_w, ssd_conv_b=ssd_conv_b, ssd_dt_bias=ssd_dt_bias, ssd_a_log=ssd_a_log, ssd_d=ssd_d,
        ssd_norm_w=ssd_norm_w, hy_conv_w=hy_conv_w, hy_conv_b=hy_conv_b, hy_w1=hy_w1, hy_b1=hy_b1, hy_w2=hy_w2,
        hy_b2=hy_b2, hy_w3=hy_w3, hy_freq=hy_freq, hy_bias=hy_bias, ret_decay_logit=ret_decay_logit,
        ret_gn_w=ret_gn_w, attn_q_norm=attn_q_norm, attn_k_norm=attn_k_norm, attn_sink=attn_sink))
    ck = cache_k.reshape(bs, DEPTH, lc, D_KV)
    cv = cache_v.reshape(bs, DEPTH, lc, D_KV)

    tab_p = _dft_tables(lp_len)
    tab_s = _dft_tables(ls_len)
    blk_p = _dft_block_tables(tab_p, lp_len)
    blk_s = _dft_block_tables(tab_s, ls_len)

    tm = TOKEN_TILE
    tp, ts = bp * lp_len, bs * ls_len
    assert tp % tm == 0 and ts % tm == 0 and ls_len % tm == 0 and tp % ls_len == 0
    st = _Stream(tm, tp // tm, ts // tm, ls_len)
    tm_in = IN_TOKEN_TILE
    assert tp % tm_in == 0 and ts % tm_in == 0 and ls_len % tm_in == 0
    st_in = _Stream(tm_in, tp // tm_in, ts // tm_in, ls_len)
    x = (x_prompt.reshape(tp, D_MODEL), x_sample.reshape(ts, D_MODEL))
    carried = {}
    for li in range(DEPTH):
        spec_p = _hy_filter(lp_len, tab_p[0], tab_p[1], wts, li)
        spec_s = _hy_filter(ls_len, tab_s[0], tab_s[1], wts, li)
        x = _ffn(st, x, mod, wts, li, 0)
        u = _inproj(st_in, x, mod, wts, li)
        y_p, carried = _mixers(u, wts, li, bp, lp_len, 0, None, None, None, blk_p, spec_p, carried)
        y_s, _ = _mixers(u, wts, li, bs, ls_len, tp // ls_len, state_ssd, state_ret, (ck, cv), blk_s, spec_s, {})
        if li + 1 < DEPTH:
            x = _ffn(st, x, mod, wts, li, 1, mix=tuple(zip(y_p, y_s)))
        else:
            yp = _ffn(st.part(0), x, mod, wts, li, 1, mix=y_p)
            ys = _ffn(st.part(1), x, mod, wts, li, 1, mix=y_s)

    kv_shape = (bp, DEPTH, lp_len, ATT_KV_HEADS, HEAD_DIM)
    new_k, new_v = carried["kv"]
    return (yp.reshape(bp, lp_len, D_MODEL), ys.reshape(bs, ls_len, D_MODEL),
            new_k.reshape(kv_shape), new_v.reshape(kv_shape), carried["ssd"], carried["ret"])
```
